```python
import jax, jax.numpy as jnp
from jax import lax
import numpy as np

D_MODEL = 1024
BATCH = 8
SEQ = 2048
DEPTH = 4

D_MIX = D_MODEL
SSD_HEAD_DIM = 64
SSD_WIDTH = D_MIX // 2
SSD_HEADS = SSD_WIDTH // SSD_HEAD_DIM
SSD_GROUPS = 2
SSD_HEADS_PER_GROUP = SSD_HEADS // SSD_GROUPS
D_STATE = 128
CONV_WIDTH = 4
CHUNK = 128
CONV_DIM = SSD_WIDTH + 2 * SSD_GROUPS * D_STATE
SB_HEAD_DIM = 64
SB_WIDTH = D_MIX // 4
SB_HEADS = SB_WIDTH // SB_HEAD_DIM
Q_BLOCK = 128
POOL_WINDOWS = (2, 4, 8, 16)
POOL_GROUPS = len(POOL_WINDOWS)
POOL_WIDTH = D_MIX - SSD_WIDTH - SB_WIDTH
POOL_GROUP_DIM = POOL_WIDTH // POOL_GROUPS
D_IN_PROJ = SSD_WIDTH + CONV_DIM + SSD_HEADS + 3 * SB_WIDTH + POOL_WIDTH
SPLIT_POINTS = (SSD_WIDTH,
                SSD_WIDTH + CONV_DIM,
                SSD_WIDTH + CONV_DIM + SSD_HEADS,
                SSD_WIDTH + CONV_DIM + SSD_HEADS + 3 * SB_WIDTH)
D_FF = -(-8 * D_MODEL // (3 * 256)) * 256
EPS = 1e-6

kernel_name = 'hybrid_ssd_stickbreak_pool_trunk'


def rmsnorm(x, w):
    xf = x.astype(jnp.float32)
    y = xf * lax.rsqrt(jnp.mean(xf * xf, axis=-1, keepdims=True) + EPS)
    return (y * w.astype(jnp.float32)).astype(x.dtype)


def causal_depthwise_conv(u, w, b):
    out = lax.conv_general_dilated(
        u, w[:, None, :].astype(u.dtype), window_strides=(1,),
        padding=[(CONV_WIDTH - 1, 0)],
        dimension_numbers=('NWC', 'WIO', 'NWC'),
        feature_group_count=u.shape[-1])
    return out + b.astype(u.dtype)


def ssd_mixer(z, xbc, dt_raw, conv_w, conv_b, dt_bias, a_log, d_skip, norm_w):
    f32 = jnp.float32
    bsz, seqlen, _ = xbc.shape
    nc = seqlen // CHUNK
    G, K, P, N, L = SSD_GROUPS, SSD_HEADS_PER_GROUP, SSD_HEAD_DIM, D_STATE, CHUNK
    xbc = jax.nn.silu(causal_depthwise_conv(xbc, conv_w, conv_b))
    xs, bm, cm = jnp.split(xbc, [SSD_WIDTH, SSD_WIDTH + SSD_GROUPS * D_STATE], axis=-1)
    dt = jax.nn.softplus(dt_raw.astype(f32) + dt_bias.astype(f32))
    a = -jnp.exp(a_log.astype(f32))
    xh = xs.astype(f32).reshape(bsz, nc, L, G, K, P)
    dtc = dt.reshape(bsz, nc, L, G, K)
    X = xh * dtc[..., None]
    Bc = bm.astype(f32).reshape(bsz, nc, L, G, N)
    Cc = cm.astype(f32).reshape(bsz, nc, L, G, N)
    dA = jnp.transpose(dtc * a.reshape(G, K), (0, 1, 3, 4, 2))
    acum = jnp.cumsum(dA, axis=-1)
    causal = jnp.tril(jnp.ones((L, L), dtype=bool))
    seg = jnp.where(causal, acum[..., :, None] - acum[..., None, :], -jnp.inf)
    decay_in = jnp.exp(seg)
    cb = jnp.einsum('bclgn,bcsgn->bcgls', Cc, Bc)
    y_diag = jnp.einsum('bcgls,bcgkls,bcsgkp->bclgkp', cb, decay_in, X)
    decay_to_end = jnp.exp(acum[..., -1:] - acum)
    chunk_states = jnp.einsum('bclgn,bcgkl,bclgkp->bcgkpn', Bc, decay_to_end, X)
    chunk_decay = jnp.exp(acum[..., -1])

    def step(state, inp):
        st, dec = inp
        return state * dec[..., None, None] + st, state

    init = jnp.zeros((bsz, G, K, P, N), f32)
    _, prev = lax.scan(step, init, (jnp.moveaxis(chunk_states, 1, 0), jnp.moveaxis(chunk_decay, 1, 0)))
    prev = jnp.moveaxis(prev, 0, 1)
    y_off = jnp.einsum('bclgn,bcgkpn,bcgkl->bclgkp', Cc, prev, jnp.exp(acum))
    y = y_diag + y_off + xh * d_skip.astype(f32).reshape(G, K)[:, :, None]
    y = y.reshape(bsz, seqlen, SSD_WIDTH) * jax.nn.silu(z.astype(f32))
    return rmsnorm(y, norm_w).astype(z.dtype)


def stick_breaking_attention(q, k, v):
    f32 = jnp.float32
    bsz, seqlen, _ = q.shape
    qh = q.astype(f32).reshape(bsz, seqlen, SB_HEADS, SB_HEAD_DIM) * (SB_HEAD_DIM ** -0.5)
    kh = k.astype(f32).reshape(bsz, seqlen, SB_HEADS, SB_HEAD_DIM)
    vh = v.astype(f32).reshape(bsz, seqlen, SB_HEADS, SB_HEAD_DIM)
    outs = []
    for start in range(0, seqlen, Q_BLOCK):
        end = start + Q_BLOCK
        logits = jnp.einsum('bthd,bshd->bhts', qh[:, start:end], kh[:, :end])
        before = jnp.arange(end)[None, :] < jnp.arange(start, end)[:, None]
        log_keep = jnp.where(before, jax.nn.log_sigmoid(-logits), 0.0)
        log_keep_after = lax.cumsum(log_keep, axis=3, reverse=True) - log_keep
        w = jnp.where(before, jnp.exp(jax.nn.log_sigmoid(logits) + log_keep_after), 0.0)
        outs.append(jnp.einsum('bhts,bshd->bthd', w, vh[:, :end]))
    o = jnp.concatenate(outs, axis=1)
    return o.reshape(bsz, seqlen, SB_WIDTH).astype(q.dtype)


def multiscale_pool(p, pool_w, pool_b, pool_scale):
    f32 = jnp.float32
    bsz, seqlen, _ = p.shape
    groups = p.astype(f32).reshape(bsz, seqlen, POOL_GROUPS, POOL_GROUP_DIM)
    csum = jnp.pad(jnp.cumsum(groups, axis=1), ((0, 0), (1, 0), (0, 0), (0, 0)))
    pos = jnp.arange(seqlen)
    pooled = []
    for gi, win in enumerate(POOL_WINDOWS):
        cg = csum[:, :, gi]
        lo = jnp.maximum(pos + 1 - win, 0)
        wsum = cg[:, 1:] - cg[:, lo]
        count = jnp.minimum(pos + 1, win).astype(f32)
        pooled.append(wsum / count[None, :, None] - groups[:, :, gi])
    pooled = jnp.stack(pooled, axis=2)
    mixed = jnp.einsum('bsgc,gcd->bsgd', pooled, pool_w.astype(f32)) + pool_b.astype(f32)
    return (mixed.reshape(bsz, seqlen, POOL_WIDTH) * pool_scale.astype(f32)).astype(p.dtype)


def _fwd_setup_inputs(seed: int = 0) -> dict:
    key = jax.random.key(seed)
    ks = jax.random.split(key, 20)
    f32 = jnp.float32
    nrm = lambda k, shape, s: jax.random.normal(k, shape, f32) * s
    dt0 = jnp.exp(jax.random.uniform(ks[5], (DEPTH, SSD_HEADS), f32, np.log(1e-3), np.log(1e-1)))
    return {
        'x': jax.random.normal(ks[0], (BATCH, SEQ, D_MODEL), f32),
        'norm1_w': 1.0 + nrm(ks[1], (DEPTH, D_MODEL), 0.02),
        'w_in': nrm(ks[2], (DEPTH, D_MODEL, D_IN_PROJ), D_MODEL ** -0.5),
        'conv_w': nrm(ks[3], (DEPTH, CONV_WIDTH, CONV_DIM), CONV_WIDTH ** -0.5),
        'conv_b': nrm(ks[4], (DEPTH, CONV_DIM), 0.02),
        'dt_bias': dt0 + jnp.log(-jnp.expm1(-dt0)),
        'a_log': jnp.log(jax.random.uniform(ks[6], (DEPTH, SSD_HEADS), f32, 1.0, 16.0)),
        'd_skip': 1.0 + nrm(ks[7], (DEPTH, SSD_HEADS), 0.1),
        'ssd_norm_w': 1.0 + nrm(ks[8], (DEPTH, SSD_WIDTH), 0.02),
        'pool_w': nrm(ks[9], (DEPTH, POOL_GROUPS, POOL_GROUP_DIM, POOL_GROUP_DIM), POOL_GROUP_DIM ** -0.5),
        'pool_b': nrm(ks[10], (DEPTH, POOL_GROUPS, POOL_GROUP_DIM), 0.02),
        'pool_scale': 1.0 + nrm(ks[11], (DEPTH, POOL_WIDTH), 0.1),
        'w_out': nrm(ks[12], (DEPTH, D_MIX, D_MODEL), D_MIX ** -0.5),
        'norm2_w': 1.0 + nrm(ks[13], (DEPTH, D_MODEL), 0.02),
        'w_gate': nrm(ks[14], (DEPTH, D_MODEL, D_FF), D_MODEL ** -0.5),
        'w_up': nrm(ks[15], (DEPTH, D_MODEL, D_FF), D_MODEL ** -0.5),
        'w_down': nrm(ks[16], (DEPTH, D_FF, D_MODEL), D_FF ** -0.5),
        'final_norm_w': 1.0 + nrm(ks[17], (D_MODEL,), 0.02),
    }


def _fwd_reference(x, norm1_w, w_in, conv_w, conv_b, dt_bias, a_log, d_skip, ssd_norm_w,
              pool_w, pool_b, pool_scale, w_out, norm2_w, w_gate, w_up, w_down, final_norm_w):
    for layer in range(DEPTH):
        h = rmsnorm(x, norm1_w[layer])
        proj = h @ w_in[layer]
        z, xbc, dt_raw, qkv, p = jnp.split(proj, list(SPLIT_POINTS), axis=-1)
        q, k, v = jnp.split(qkv, 3, axis=-1)
        y_ssd = ssd_mixer(z, xbc, dt_raw, conv_w[layer], conv_b[layer], dt_bias[layer],
                          a_log[layer], d_skip[layer], ssd_norm_w[layer])
        y_sb = stick_breaking_attention(q, k, v)
        y_pool = multiscale_pool(p, pool_w[layer], pool_b[layer], pool_scale[layer])
        y = jnp.concatenate([y_ssd, y_sb, y_pool], axis=-1)
        x = x + y @ w_out[layer]
        h = rmsnorm(x, norm2_w[layer])
        x = x + (jax.nn.silu(h @ w_gate[layer]) * (h @ w_up[layer])) @ w_down[layer]
    return rmsnorm(x, final_norm_w)


import jax as _jax
import jax.numpy as _jnp

TWIN_FORMAT = 'train_step'
FWD_PARAMS = ['x', 'norm1_w', 'w_in', 'conv_w', 'conv_b', 'dt_bias', 'a_log', 'd_skip', 'ssd_norm_w', 'pool_w', 'pool_b', 'pool_scale', 'w_out', 'norm2_w', 'w_gate', 'w_up', 'w_down', 'final_norm_w']
TWIN_WEIGHTS = ['norm1_w', 'w_in', 'conv_w', 'conv_b', 'dt_bias', 'a_log', 'd_skip', 'ssd_norm_w', 'pool_w', 'pool_b', 'pool_scale', 'w_out', 'norm2_w', 'w_gate', 'w_up', 'w_down', 'final_norm_w']
TWIN_DIFF_INPUT = 'x'
TWIN_INPUTS = ['x', 'norm1_w', 'w_in', 'conv_w', 'conv_b', 'dt_bias', 'a_log', 'd_skip', 'ssd_norm_w', 'pool_w', 'pool_b', 'pool_scale', 'w_out', 'norm2_w', 'w_gate', 'w_up', 'w_down', 'final_norm_w', 'loss_target', 'm_norm1_w', 'm_w_in', 'm_conv_w', 'm_conv_b', 'm_dt_bias', 'm_a_log', 'm_d_skip', 'm_ssd_norm_w', 'm_pool_w', 'm_pool_b', 'm_pool_scale', 'm_w_out', 'm_norm2_w', 'm_w_gate', 'm_w_up', 'm_w_down', 'm_final_norm_w', 'v_norm1_w', 'v_w_in', 'v_conv_w', 'v_conv_b', 'v_dt_bias', 'v_a_log', 'v_d_skip', 'v_ssd_norm_w', 'v_pool_w', 'v_pool_b', 'v_pool_scale', 'v_w_out', 'v_norm2_w', 'v_w_gate', 'v_w_up', 'v_w_down', 'v_final_norm_w']
TWIN_OUTPUTS = ['loss', 'grad_x', 'grad_norm1_w', 'grad_w_in', 'grad_conv_w', 'grad_conv_b', 'grad_dt_bias', 'grad_a_log', 'grad_d_skip', 'grad_ssd_norm_w', 'grad_pool_w', 'grad_pool_b', 'grad_pool_scale', 'grad_w_out', 'grad_norm2_w', 'grad_w_gate', 'grad_w_up', 'grad_w_down', 'grad_final_norm_w', 'delta_norm1_w', 'delta_w_in', 'delta_conv_w', 'delta_conv_b', 'delta_dt_bias', 'delta_a_log', 'delta_d_skip', 'delta_ssd_norm_w', 'delta_pool_w', 'delta_pool_b', 'delta_pool_scale', 'delta_w_out', 'delta_norm2_w', 'delta_w_gate', 'delta_w_up', 'delta_w_down', 'delta_final_norm_w', 'new_m_norm1_w', 'new_m_w_in', 'new_m_conv_w', 'new_m_conv_b', 'new_m_dt_bias', 'new_m_a_log', 'new_m_d_skip', 'new_m_ssd_norm_w', 'new_m_pool_w', 'new_m_pool_b', 'new_m_pool_scale', 'new_m_w_out', 'new_m_norm2_w', 'new_m_w_gate', 'new_m_w_up', 'new_m_w_down', 'new_m_final_norm_w', 'new_v_norm1_w', 'new_v_w_in', 'new_v_conv_w', 'new_v_conv_b', 'new_v_dt_bias', 'new_v_a_log', 'new_v_d_skip', 'new_v_ssd_norm_w', 'new_v_pool_w', 'new_v_pool_b', 'new_v_pool_scale', 'new_v_w_out', 'new_v_norm2_w', 'new_v_w_gate', 'new_v_w_up', 'new_v_w_down', 'new_v_final_norm_w']
TWIN_LEAF_KINDS = {'loss': 'loss', 'grad_x': 'grad_x', 'grad_norm1_w': 'grad_w', 'grad_w_in': 'grad_w', 'grad_conv_w': 'grad_w', 'grad_conv_b': 'grad_w', 'grad_dt_bias': 'grad_w', 'grad_a_log': 'grad_w', 'grad_d_skip': 'grad_w', 'grad_ssd_norm_w': 'grad_w', 'grad_pool_w': 'grad_w', 'grad_pool_b': 'grad_w', 'grad_pool_scale': 'grad_w', 'grad_w_out': 'grad_w', 'grad_norm2_w': 'grad_w', 'grad_w_gate': 'grad_w', 'grad_w_up': 'grad_w', 'grad_w_down': 'grad_w', 'grad_final_norm_w': 'grad_w', 'delta_norm1_w': 'delta_w', 'delta_w_in': 'delta_w', 'delta_conv_w': 'delta_w', 'delta_conv_b': 'delta_w', 'delta_dt_bias': 'delta_w', 'delta_a_log': 'delta_w', 'delta_d_skip': 'delta_w', 'delta_ssd_norm_w': 'delta_w', 'delta_pool_w': 'delta_w', 'delta_pool_b': 'delta_w', 'delta_pool_scale': 'delta_w', 'delta_w_out': 'delta_w', 'delta_norm2_w': 'delta_w', 'delta_w_gate': 'delta_w', 'delta_w_up': 'delta_w', 'delta_w_down': 'delta_w', 'delta_final_norm_w': 'delta_w', 'new_m_norm1_w': 'new_m', 'new_m_w_in': 'new_m', 'new_m_conv_w': 'new_m', 'new_m_conv_b': 'new_m', 'new_m_dt_bias': 'new_m', 'new_m_a_log': 'new_m', 'new_m_d_skip': 'new_m', 'new_m_ssd_norm_w': 'new_m', 'new_m_pool_w': 'new_m', 'new_m_pool_b': 'new_m', 'new_m_pool_scale': 'new_m', 'new_m_w_out': 'new_m', 'new_m_norm2_w': 'new_m', 'new_m_w_gate': 'new_m', 'new_m_w_up': 'new_m', 'new_m_w_down': 'new_m', 'new_m_final_norm_w': 'new_m', 'new_v_norm1_w': 'new_v', 'new_v_w_in': 'new_v', 'new_v_conv_w': 'new_v', 'new_v_conv_b': 'new_v', 'new_v_dt_bias': 'new_v', 'new_v_a_log': 'new_v', 'new_v_d_skip': 'new_v', 'new_v_ssd_norm_w': 'new_v', 'new_v_pool_w': 'new_v', 'new_v_pool_b': 'new_v', 'new_v_pool_scale': 'new_v', 'new_v_w_out': 'new_v', 'new_v_norm2_w': 'new_v', 'new_v_w_gate': 'new_v', 'new_v_w_up': 'new_v', 'new_v_w_down': 'new_v', 'new_v_final_norm_w': 'new_v'}


def _forward(args):
    return _fwd_reference(*[args[k] for k in FWD_PARAMS])


def _output_shape():
    out = _jax.eval_shape(lambda: _forward(_fwd_setup_inputs(0)))
    return out.shape, out.dtype

N_MICROBATCH = 1
ADAM_LR = 0.001
ADAM_B1 = 0.9
ADAM_B2 = 0.999
ADAM_EPS = 1e-08
ADAM_WD = 0.01
ADAM_STEP = 10
PER_EXAMPLE_BATCH_AXIS = {'x': 0, 'loss_target': 0}
SHARED_INPUTS = []
_WEIGHT_DTYPES = {'norm1_w': _jnp.float32, 'w_in': _jnp.float32, 'conv_w': _jnp.float32, 'conv_b': _jnp.float32, 'dt_bias': _jnp.float32, 'a_log': _jnp.float32, 'd_skip': _jnp.float32, 'ssd_norm_w': _jnp.float32, 'pool_w': _jnp.float32, 'pool_b': _jnp.float32, 'pool_scale': _jnp.float32, 'w_out': _jnp.float32, 'norm2_w': _jnp.float32, 'w_gate': _jnp.float32, 'w_up': _jnp.float32, 'w_down': _jnp.float32, 'final_norm_w': _jnp.float32}
MOMENT_SCALE = {'norm1_w': 1.229032e-01, 'w_in': 7.737799e-02, 'conv_w': 7.867305e-02, 'conv_b': 9.916019e-02, 'dt_bias': 3.704645e-01, 'a_log': 2.335207e-01, 'd_skip': 5.640710e-01, 'ssd_norm_w': 1.051511e-01, 'pool_w': 8.639267e-02, 'pool_b': 1.215226e-01, 'pool_scale': 8.723570e-02, 'w_out': 8.828778e-02, 'norm2_w': 7.341729e-02, 'w_gate': 3.209989e-02, 'w_up': 3.103556e-02, 'w_down': 5.146122e-02, 'final_norm_w': 1.603205e+01}


def _to_microbatches(a, axis):
    t = _jnp.moveaxis(a, axis, 0)
    t = t.reshape((N_MICROBATCH, t.shape[0] // N_MICROBATCH) + t.shape[1:])
    return _jnp.moveaxis(t, 1, axis + 1)


def setup_inputs(seed: int = 0) -> dict:
    inp = _fwd_setup_inputs(seed)
    key = _jax.random.fold_in(_jax.random.key(seed), 7919)
    shape, _ = _output_shape()
    out = dict(inp)
    out["loss_target"] = _jax.random.normal(_jax.random.fold_in(key, 0), shape, _jnp.float32)
    for i, name in enumerate(TWIN_WEIGHTS):
        w = inp[name].astype(_jnp.float32)
        if MOMENT_SCALE is None:
            s = _jnp.sqrt(_jnp.mean(_jnp.square(w)) + 1e-30)
        else:
            s = MOMENT_SCALE[name]
        km, kv = _jax.random.split(_jax.random.fold_in(key, i + 1))
        out[name] = w
        out["m_" + name] = s * _jax.random.normal(km, w.shape, _jnp.float32)
        out["v_" + name] = (s * s) * _jax.random.uniform(kv, w.shape, _jnp.float32, 0.5, 1.5)
    if N_MICROBATCH > 1:
        for name, axis in PER_EXAMPLE_BATCH_AXIS.items():
            out[name] = _to_microbatches(out[name], axis)
    return {'x': out['x'], 'norm1_w': out['norm1_w'], 'w_in': out['w_in'], 'conv_w': out['conv_w'], 'conv_b': out['conv_b'], 'dt_bias': out['dt_bias'], 'a_log': out['a_log'], 'd_skip': out['d_skip'], 'ssd_norm_w': out['ssd_norm_w'], 'pool_w': out['pool_w'], 'pool_b': out['pool_b'], 'pool_scale': out['pool_scale'], 'w_out': out['w_out'], 'norm2_w': out['norm2_w'], 'w_gate': out['w_gate'], 'w_up': out['w_up'], 'w_down': out['w_down'], 'final_norm_w': out['final_norm_w'], 'loss_target': out['loss_target'], 'm_norm1_w': out['m_norm1_w'], 'm_w_in': out['m_w_in'], 'm_conv_w': out['m_conv_w'], 'm_conv_b': out['m_conv_b'], 'm_dt_bias': out['m_dt_bias'], 'm_a_log': out['m_a_log'], 'm_d_skip': out['m_d_skip'], 'm_ssd_norm_w': out['m_ssd_norm_w'], 'm_pool_w': out['m_pool_w'], 'm_pool_b': out['m_pool_b'], 'm_pool_scale': out['m_pool_scale'], 'm_w_out': out['m_w_out'], 'm_norm2_w': out['m_norm2_w'], 'm_w_gate': out['m_w_gate'], 'm_w_up': out['m_w_up'], 'm_w_down': out['m_w_down'], 'm_final_norm_w': out['m_final_norm_w'], 'v_norm1_w': out['v_norm1_w'], 'v_w_in': out['v_w_in'], 'v_conv_w': out['v_conv_w'], 'v_conv_b': out['v_conv_b'], 'v_dt_bias': out['v_dt_bias'], 'v_a_log': out['v_a_log'], 'v_d_skip': out['v_d_skip'], 'v_ssd_norm_w': out['v_ssd_norm_w'], 'v_pool_w': out['v_pool_w'], 'v_pool_b': out['v_pool_b'], 'v_pool_scale': out['v_pool_scale'], 'v_w_out': out['v_w_out'], 'v_norm2_w': out['v_norm2_w'], 'v_w_gate': out['v_w_gate'], 'v_w_up': out['v_w_up'], 'v_w_down': out['v_w_down'], 'v_final_norm_w': out['v_final_norm_w']}


def _loss(weights, diff, rest, loss_target):
    with _jax.named_scope("forward"):
        args = {**rest, TWIN_DIFF_INPUT: diff, **{k: w.astype(_WEIGHT_DTYPES[k]) for k, w in weights.items()}}
        y = _forward(args)
    with _jax.named_scope("loss_head"):
        err = _jnp.square(y.astype(_jnp.float32) - loss_target)
        return 0.5 * _jnp.sum(_jnp.mean(err, axis=-1)) if err.ndim else 0.5 * err


def _adamw(w, g, m, v):
    m = ADAM_B1 * m + (1.0 - ADAM_B1) * g
    v = ADAM_B2 * v + (1.0 - ADAM_B2) * _jnp.square(g)
    m_hat = m / (1.0 - ADAM_B1 ** ADAM_STEP)
    v_hat = v / (1.0 - ADAM_B2 ** ADAM_STEP)
    delta = -ADAM_LR * (m_hat / (_jnp.sqrt(v_hat) + ADAM_EPS) + ADAM_WD * w)
    return delta, m, v


def reference(x, norm1_w, w_in, conv_w, conv_b, dt_bias, a_log, d_skip, ssd_norm_w, pool_w, pool_b, pool_scale, w_out, norm2_w, w_gate, w_up, w_down, final_norm_w, loss_target, m_norm1_w, m_w_in, m_conv_w, m_conv_b, m_dt_bias, m_a_log, m_d_skip, m_ssd_norm_w, m_pool_w, m_pool_b, m_pool_scale, m_w_out, m_norm2_w, m_w_gate, m_w_up, m_w_down, m_final_norm_w, v_norm1_w, v_w_in, v_conv_w, v_conv_b, v_dt_bias, v_a_log, v_d_skip, v_ssd_norm_w, v_pool_w, v_pool_b, v_pool_scale, v_w_out, v_norm2_w, v_w_gate, v_w_up, v_w_down, v_final_norm_w):
    given = dict(x=x, norm1_w=norm1_w, w_in=w_in, conv_w=conv_w, conv_b=conv_b, dt_bias=dt_bias, a_log=a_log, d_skip=d_skip, ssd_norm_w=ssd_norm_w, pool_w=pool_w, pool_b=pool_b, pool_scale=pool_scale, w_out=w_out, norm2_w=norm2_w, w_gate=w_gate, w_up=w_up, w_down=w_down, final_norm_w=final_norm_w, loss_target=loss_target, m_norm1_w=m_norm1_w, m_w_in=m_w_in, m_conv_w=m_conv_w, m_conv_b=m_conv_b, m_dt_bias=m_dt_bias, m_a_log=m_a_log, m_d_skip=m_d_skip, m_ssd_norm_w=m_ssd_norm_w, m_pool_w=m_pool_w, m_pool_b=m_pool_b, m_pool_scale=m_pool_scale, m_w_out=m_w_out, m_norm2_w=m_norm2_w, m_w_gate=m_w_gate, m_w_up=m_w_up, m_w_down=m_w_down, m_final_norm_w=m_final_norm_w, v_norm1_w=v_norm1_w, v_w_in=v_w_in, v_conv_w=v_conv_w, v_conv_b=v_conv_b, v_dt_bias=v_dt_bias, v_a_log=v_a_log, v_d_skip=v_d_skip, v_ssd_norm_w=v_ssd_norm_w, v_pool_w=v_pool_w, v_pool_b=v_pool_b, v_pool_scale=v_pool_scale, v_w_out=v_w_out, v_norm2_w=v_norm2_w, v_w_gate=v_w_gate, v_w_up=v_w_up, v_w_down=v_w_down, v_final_norm_w=v_final_norm_w)
    weights = {n: given[n] for n in TWIN_WEIGHTS}
    shared = {n: given[n] for n in SHARED_INPUTS}
    per_example = {n: given[n] for n in ['x']}
    grad_fn = _jax.value_and_grad(_loss, argnums=(0, 1))

    def one_microbatch(ex, loss_target):
        ex = dict(ex)
        diff = ex.pop(TWIN_DIFF_INPUT)
        return grad_fn(weights, diff, {**shared, **ex}, loss_target)

    if N_MICROBATCH == 1:
        loss, (grad_w, grad_x) = one_microbatch(per_example, given["loss_target"])
    else:
        def body(carry, xs):
            loss_sum, grad_sum = carry
            l_k, (gw_k, gx_k) = one_microbatch(xs[0], xs[1])
            with _jax.named_scope("update"):
                return (loss_sum + l_k, _jax.tree.map(_jnp.add, grad_sum, gw_k)), gx_k

        init = (_jnp.zeros((), _jnp.float32), _jax.tree.map(_jnp.zeros_like, weights))
        (loss, grad_w), grad_x = _jax.lax.scan(body, init, (per_example, given["loss_target"]))
    with _jax.named_scope("update"):
        delta_w, new_m, new_v = {}, {}, {}
        for n in TWIN_WEIGHTS:
            delta_w[n], new_m[n], new_v[n] = _adamw(weights[n], grad_w[n], given["m_" + n], given["v_" + n])
    return (loss, grad_x, *[grad_w[n] for n in TWIN_WEIGHTS], *[delta_w[n] for n in TWIN_WEIGHTS],
            *[new_m[n] for n in TWIN_WEIGHTS], *[new_v[n] for n in TWIN_WEIGHTS])
```

```python
import functools

import jax
import jax.numpy as jnp
from jax import lax
from jax.experimental import pallas as pl
from jax.experimental.pallas import tpu as pltpu

F32 = jnp.float32
BF16 = jnp.bfloat16
HIGHEST = lax.Precision.HIGHEST
MESH = pl.DeviceIdType.MESH

EPS = 1e-6
D_MODEL = 1024
SSD_WIDTH = 512
SSD_HEADS = 8
HEAD_DIM = 64
D_STATE = 128
CONV_WIDTH = 4
CONV_DIM = 1024
SB_WIDTH = 256
POOL_WIDTH = 256
POOL_WINDOWS = (2, 4, 8, 16)
D_IN_PROJ = 2568
D_FF = 2816
N_DEV = 8
DEPTH = 4
SEG = (512, 1024, 768, 256, 128)
D_IN_PAD = sum(SEG)
DT_LO, DT_HI = 1536, 1544

LANE = 128
BLK = 128
ROW_TILE = 256
VMEM_LIMIT = 56 * 2**20

ADAM_LR, ADAM_B1, ADAM_B2, ADAM_EPS, ADAM_WD, ADAM_STEP = 0.001, 0.9, 0.999, 1e-08, 0.01, 10


def _params(n_axes=1, vmem=None):
    return pltpu.CompilerParams(dimension_semantics=("arbitrary",) * n_axes, vmem_limit_bytes=vmem)


def _dot(a, b, dims, exact=False):
    if exact:
        return lax.dot_general(a.astype(F32), b.astype(F32), (dims, ((), ())), precision=HIGHEST,
                               preferred_element_type=F32)
    return lax.dot_general(a.astype(BF16), b.astype(BF16), (dims, ((), ())), preferred_element_type=F32)


def dot_nn(a, b, exact=False):
    return _dot(a, b, ((1,), (0,)), exact)


def dot_nt(a, b, exact=False):
    return _dot(a, b, ((1,), (1,)), exact)


def dot_tn(a, b, exact=False):
    return _dot(a, b, ((0,), (0,)), exact)


def _iota(shape, axis):
    return lax.broadcasted_iota(jnp.int32, shape, axis)


def _lane_col(x, h):
    return jnp.sum(jnp.where(_iota(x.shape, 1) == h, x, 0.0), axis=1, keepdims=True)


def _sub_row(x, h):
    return jnp.sum(jnp.where(_iota(x.shape, 0) == h, x, 0.0), axis=0, keepdims=True)


def _sigmoid(x):
    return 1.0 / (1.0 + jnp.exp(-x))


def _rms_fwd(x, w):
    r = lax.rsqrt(jnp.mean(x * x, axis=-1, keepdims=True) + EPS)
    return x * r * w


def _rms_bwd(x, w, dy):
    r = lax.rsqrt(jnp.mean(x * x, axis=-1, keepdims=True) + EPS)
    xh = x * r
    dxh = dy * w
    dx = r * (dxh - xh * jnp.mean(dxh * xh, axis=-1, keepdims=True))
    return dx, jnp.sum(dy * xh, axis=0, keepdims=True)


def _acc(ref, first, val):
    @pl.when(first)
    def _():
        ref[...] = val

    @pl.when(jnp.logical_not(first))
    def _():
        ref[...] += val


def _row_spec(tm, n):
    return pl.BlockSpec((tm, n), lambda i: (i, 0))


def _full_spec(shape):
    return pl.BlockSpec(shape, lambda *_: (0,) * len(shape))


def inproj_fwd(x, nw, w):
    s, d = x.shape
    tm = min(ROW_TILE, s)

    def body(x_ref, nw_ref, w_ref, z_ref, xbc_ref, qkv_ref, p_ref, dt_ref, h_ref):
        h = _rms_fwd(x_ref[...], nw_ref[...]).astype(BF16)
        h_ref[...] = h
        lo = 0
        for ref, n in zip((z_ref, xbc_ref, qkv_ref, p_ref, dt_ref), SEG):
            ref[...] = dot_nn(h, w_ref[:, lo:lo + n])
            lo += n

    return pl.pallas_call(
        body, name="inproj_fwd", grid=(s // tm,),
        in_specs=[_row_spec(tm, d), _full_spec((1, d)), _full_spec(w.shape)],
        out_specs=[_row_spec(tm, n) for n in SEG] + [_row_spec(tm, d)],
        out_shape=[jax.ShapeDtypeStruct((s, n), F32) for n in SEG] + [jax.ShapeDtypeStruct((s, d), BF16)],
        compiler_params=_params(1, VMEM_LIMIT),
    )(x, nw, w)


def inproj_bwd(dproj, w, x, nw, dres):
    s, d = x.shape
    tm = min(ROW_TILE, s)

    def body(dp_ref, w_ref, x_ref, nw_ref, dres_ref, dx_ref, dnw_ref):
        dh = dot_nt(dp_ref[...], w_ref[...])
        dx, dnw = _rms_bwd(x_ref[...], nw_ref[...], dh)
        dx_ref[...] = dres_ref[...] + dx
        _acc(dnw_ref, pl.program_id(0) == 0, dnw)

    return pl.pallas_call(
        body, name="inproj_bwd", grid=(s // tm,),
        in_specs=[_row_spec(tm, dproj.shape[1]), _full_spec(w.shape), _row_spec(tm, d), _full_spec((1, d)),
                  _row_spec(tm, d)],
        out_specs=[_row_spec(tm, d), _full_spec((1, d))],
        out_shape=[jax.ShapeDtypeStruct((s, d), F32), jax.ShapeDtypeStruct((1, d), F32)],
        compiler_params=_params(1, VMEM_LIMIT),
    )(dproj, w, x, nw, dres)


def outproj_fwd(y, w, res):
    s, d = res.shape
    tm = min(ROW_TILE, s)

    def body(y_ref, w_ref, r_ref, o_ref):
        o_ref[...] = r_ref[...] + dot_nn(y_ref[...], w_ref[...])

    return pl.pallas_call(
        body, name="outproj_fwd", grid=(s // tm,),
        in_specs=[_row_spec(tm, y.shape[1]), _full_spec(w.shape), _row_spec(tm, d)],
        out_specs=_row_spec(tm, d), out_shape=jax.ShapeDtypeStruct((s, d), F32),
        compiler_params=_params(1, VMEM_LIMIT),
    )(y, w, res)


def outproj_bwd(dx, w):
    s, d = dx.shape
    tm = min(ROW_TILE, s)

    def body(dx_ref, w_ref, o_ref):
        o_ref[...] = dot_nt(dx_ref[...], w_ref[...])

    return pl.pallas_call(
        body, name="outproj_bwd", grid=(s // tm,),
        in_specs=[_row_spec(tm, d), _full_spec(w.shape)],
        out_specs=_row_spec(tm, w.shape[0]), out_shape=jax.ShapeDtypeStruct((s, w.shape[0]), F32),
        compiler_params=_params(1, VMEM_LIMIT),
    )(dx, w)


def ffn_fwd(x, nw, wg, wu, wd):
    s, d = x.shape
    f = wg.shape[1]
    tm = min(ROW_TILE, s)

    def body(x_ref, nw_ref, wg_ref, wu_ref, wd_ref, o_ref, g_ref, u_ref):
        xv = x_ref[...]
        h = _rms_fwd(xv, nw_ref[...]).astype(BF16)
        g = dot_nn(h, wg_ref[...])
        u = dot_nn(h, wu_ref[...])
        g_ref[...] = g.astype(BF16)
        u_ref[...] = u.astype(BF16)
        o_ref[...] = xv + dot_nn(g * _sigmoid(g) * u, wd_ref[...])

    return pl.pallas_call(
        body, name="ffn_fwd", grid=(s // tm,),
        in_specs=[_row_spec(tm, d), _full_spec((1, d)), _full_spec(wg.shape), _full_spec(wu.shape),
                  _full_spec(wd.shape)],
        out_specs=[_row_spec(tm, d), _row_spec(tm, f), _row_spec(tm, f)],
        out_shape=[jax.ShapeDtypeStruct((s, d), F32), jax.ShapeDtypeStruct((s, f), BF16),
                   jax.ShapeDtypeStruct((s, f), BF16)],
        compiler_params=_params(1, VMEM_LIMIT),
    )(x, nw, wg, wu, wd)


def ffn_bwd(dxo, x, g, u, nw, wg, wu, wd):
    s, d = x.shape
    f = wg.shape[1]
    tm = min(ROW_TILE, s)

    def body(dxo_ref, x_ref, g_ref, u_ref, nw_ref, wg_ref, wu_ref, wd_ref, dx_ref, dnw_ref, a_ref, dg_ref,
             du_ref, h_ref):
        dxo_v = dxo_ref[...]
        xv = x_ref[...]
        da = dot_nt(dxo_v, wd_ref[...])
        gv = g_ref[...].astype(F32)
        uv = u_ref[...].astype(F32)
        sg = _sigmoid(gv)
        sl = gv * sg
        a_ref[...] = (sl * uv).astype(BF16)
        dg = (da * uv * (sg * (1.0 + gv * (1.0 - sg)))).astype(BF16)
        du = (da * sl).astype(BF16)
        dg_ref[...] = dg
        du_ref[...] = du
        dh = dot_nt(dg, wg_ref[...]) + dot_nt(du, wu_ref[...])
        h_ref[...] = _rms_fwd(xv, nw_ref[...]).astype(BF16)
        dx, dnw = _rms_bwd(xv, nw_ref[...], dh)
        dx_ref[...] = dxo_v + dx
        _acc(dnw_ref, pl.program_id(0) == 0, dnw)

    return pl.pallas_call(
        body, name="ffn_bwd", grid=(s // tm,),
        in_specs=[_row_spec(tm, d), _row_spec(tm, d), _row_spec(tm, f), _row_spec(tm, f), _full_spec((1, d)),
                  _full_spec(wg.shape), _full_spec(wu.shape), _full_spec(wd.shape)],
        out_specs=[_row_spec(tm, d), _full_spec((1, d)), _row_spec(tm, f), _row_spec(tm, f), _row_spec(tm, f),
                   _row_spec(tm, d)],
        out_shape=[jax.ShapeDtypeStruct((s, d), F32), jax.ShapeDtypeStruct((1, d), F32),
                   jax.ShapeDtypeStruct((s, f), BF16), jax.ShapeDtypeStruct((s, f), BF16),
                   jax.ShapeDtypeStruct((s, f), BF16), jax.ShapeDtypeStruct((s, d), BF16)],
        compiler_params=_params(1, VMEM_LIMIT),
    )(dxo, x, g, u, nw, wg, wu, wd)


def _tile(n, cap=512):
    best = LANE
    for t in range(LANE, cap + 1, LANE):
        if n % t == 0:
            best = t
    return best


def mm_tn(a, b):
    s, k = a.shape
    n = b.shape[1]
    tk, tn = _tile(k), _tile(n)

    def body(a_ref, b_ref, o_ref):
        o_ref[...] = dot_tn(a_ref[...], b_ref[...])

    return pl.pallas_call(
        body, name="mm_tn", grid=(k // tk, n // tn),
        in_specs=[pl.BlockSpec((s, tk), lambda i, j: (0, i)), pl.BlockSpec((s, tn), lambda i, j: (0, j))],
        out_specs=pl.BlockSpec((tk, tn), lambda i, j: (i, j)), out_shape=jax.ShapeDtypeStruct((k, n), F32),
        compiler_params=_params(2, VMEM_LIMIT),
    )(a, b)


def head_loss(x, fw, tgt):
    s, d = x.shape
    tm = min(ROW_TILE, s)

    def body(x_ref, fw_ref, t_ref, loss_ref, dx_ref, dfw_ref):
        xv = x_ref[...]
        err = _rms_fwd(xv, fw_ref[...]) - t_ref[...]
        part = jnp.zeros((1, LANE), F32) + 0.5 * jnp.sum(err * err) / d
        dx, dfw = _rms_bwd(xv, fw_ref[...], err / d)
        dx_ref[...] = dx
        first = pl.program_id(0) == 0
        _acc(loss_ref, first, part)
        _acc(dfw_ref, first, dfw)

    return pl.pallas_call(
        body, name="head_loss", grid=(s // tm,),
        in_specs=[_row_spec(tm, d), _full_spec((1, d)), _row_spec(tm, d)],
        out_specs=[_full_spec((1, LANE)), _row_spec(tm, d), _full_spec((1, d))],
        out_shape=[jax.ShapeDtypeStruct((1, LANE), F32), jax.ShapeDtypeStruct((s, d), F32),
                   jax.ShapeDtypeStruct((1, d), F32)],
        compiler_params=_params(1),
    )(x, fw, tgt)


def _conv_pre(ext, cw_ref, cb_ref):
    shifted = [pltpu.roll(ext, CONV_WIDTH - 1 - i, 0)[BLK:] if i < CONV_WIDTH - 1 else ext[BLK:]
               for i in range(CONV_WIDTH)]
    acc = cb_ref[...] + sum(cw_ref[i:i + 1, :] * shifted[i] for i in range(CONV_WIDTH))
    return acc, shifted


def conv_fwd(xbc, cw, cb):
    s, n = xbc.shape

    def body(cur_ref, prev_ref, cw_ref, cb_ref, o_ref):
        prev = jnp.where(pl.program_id(0) > 0, prev_ref[...], 0.0)
        acc, _ = _conv_pre(jnp.concatenate([prev, cur_ref[...]], axis=0), cw_ref, cb_ref)
        o_ref[...] = acc * _sigmoid(acc)

    return pl.pallas_call(
        body, name="conv_fwd", grid=(s // BLK,),
        in_specs=[pl.BlockSpec((BLK, n), lambda c: (c, 0)), pl.BlockSpec((BLK, n), lambda c: (jnp.maximum(c - 1, 0), 0)),
                  _full_spec(cw.shape), _full_spec((1, n))],
        out_specs=pl.BlockSpec((BLK, n), lambda c: (c, 0)), out_shape=jax.ShapeDtypeStruct((s, n), F32),
        compiler_params=_params(1),
    )(xbc, xbc, cw, cb)


def conv_bwd(xbc, cw, cb, du):
    s, n = xbc.shape
    nb = s // BLK

    def body(cur_ref, prev_ref, cw_ref, cb_ref, du_ref, dx_ref, dcw_ref, dcb_ref, nxt_ref):
        i = pl.program_id(0)
        c = nb - 1 - i
        prev = jnp.where(c > 0, prev_ref[...], 0.0)
        acc, shifted = _conv_pre(jnp.concatenate([prev, cur_ref[...]], axis=0), cw_ref, cb_ref)
        sg = _sigmoid(acc)
        dacc = du_ref[...] * (sg * (1.0 + acc * (1.0 - sg)))

        @pl.when(i == 0)
        def _():
            nxt_ref[...] = jnp.zeros_like(nxt_ref)
            dcw_ref[...] = jnp.zeros_like(dcw_ref)
            dcb_ref[...] = jnp.zeros_like(dcb_ref)

        dcb_ref[...] += jnp.sum(dacc, axis=0, keepdims=True)
        for t in range(CONV_WIDTH):
            dcw_ref[t:t + 1, :] += jnp.sum(dacc * shifted[t], axis=0, keepdims=True)
        ext = jnp.concatenate([dacc, nxt_ref[...]], axis=0)
        dx = cw_ref[CONV_WIDTH - 1:CONV_WIDTH, :] * dacc
        for t in range(CONV_WIDTH - 1):
            dx += cw_ref[t:t + 1, :] * pltpu.roll(ext, 2 * BLK - (CONV_WIDTH - 1 - t), 0)[:BLK]
        dx_ref[...] = dx
        nxt_ref[...] = dacc

    rev = lambda i: (nb - 1 - i, 0)
    return pl.pallas_call(
        body, name="conv_bwd", grid=(nb,),
        in_specs=[pl.BlockSpec((BLK, n), rev), pl.BlockSpec((BLK, n), lambda i: (jnp.maximum(nb - 2 - i, 0), 0)),
                  _full_spec(cw.shape), _full_spec((1, n)), pl.BlockSpec((BLK, n), rev)],
        out_specs=[pl.BlockSpec((BLK, n), rev), _full_spec((8, n)), _full_spec((1, n))],
        out_shape=[jax.ShapeDtypeStruct((s, n), F32), jax.ShapeDtypeStruct((8, n), F32),
                   jax.ShapeDtypeStruct((1, n), F32)],
        scratch_shapes=[pltpu.VMEM((BLK, n), F32)],
        compiler_params=_params(1),
    )(xbc, xbc, cw, cb, du)


N_PAIR = SSD_HEADS // 2
B_LO = SSD_WIDTH
C_LO = SSD_WIDTH + 2 * D_STATE


def _softplus(x):
    return jnp.maximum(x, 0.0) + jnp.log(1.0 + jnp.exp(-jnp.abs(x)))


def _ssd_chunk(u_ref, dt_ref, dtb_ref, alog_ref):
    shape = (BLK, BLK)
    tri = _iota(shape, 1) <= _iota(shape, 0)
    pre = dt_ref[...] + dtb_ref[...]
    dt = _softplus(pre)
    a = -jnp.exp(alog_ref[...])
    acum = dot_nn(tri.astype(F32), dt * a, exact=True)
    acum_t = acum.T
    last = _sub_row(acum, BLK - 1)
    heads = []
    for h in range(SSD_HEADS):
        col = _lane_col(acum, h)
        seg = jnp.where(tri, col - _sub_row(acum_t, h), -1e30)
        heads.append(dict(col=col, dm=jnp.exp(seg), dt=_lane_col(dt, h), last=_lane_col(last, h)))
    return tri, pre, dt, a, heads


def _pair_mix(lo_mask, v0, v1):
    return jnp.where(lo_mask, v0, v1)


def ssd_fwd(u, z, dtr, dtb, alog, dsk, nw):
    s = u.shape[0]
    nc = s // BLK

    def body(u_ref, z_ref, dt_ref, dtb_ref, alog_ref, dsk_ref, nw_ref, y_ref, st_ref, s_ref):
        @pl.when(pl.program_id(0) == 0)
        def _():
            s_ref[...] = jnp.zeros_like(s_ref)

        _, _, _, _, heads = _ssd_chunk(u_ref, dt_ref, dtb_ref, alog_ref)
        lo_lane = _iota((BLK, LANE), 1) < HEAD_DIM
        lo_sub = _iota((BLK, LANE), 0) < HEAD_DIM
        ys = []
        for p in range(N_PAIR):
            g = p // 2
            h0, h1 = heads[2 * p], heads[2 * p + 1]
            bg = u_ref[:, B_LO + g * D_STATE:B_LO + (g + 1) * D_STATE]
            cg = u_ref[:, C_LO + g * D_STATE:C_LO + (g + 1) * D_STATE]
            xs = u_ref[:, p * LANE:(p + 1) * LANE]
            xp = xs * _pair_mix(lo_lane, h0["dt"], h1["dt"])
            gm = dot_nt(cg, bg)
            yd = _pair_mix(lo_lane, dot_nn(gm * h0["dm"], xp), dot_nn(gm * h1["dm"], xp))
            sp = s_ref[p]
            st_ref[0, p] = sp
            yo = _pair_mix(lo_lane, jnp.exp(h0["col"]), jnp.exp(h1["col"])) * dot_nt(cg, sp)
            dskp = _pair_mix(lo_lane, _lane_col(dsk_ref[...], 2 * p), _lane_col(dsk_ref[...], 2 * p + 1))
            ys.append(yd + yo + xs * dskp)
            wp = _pair_mix(lo_lane, jnp.exp(h0["last"] - h0["col"]), jnp.exp(h1["last"] - h1["col"]))
            el = _pair_mix(lo_sub, jnp.exp(h0["last"]), jnp.exp(h1["last"]))
            s_ref[p] = el * sp + dot_tn(wp * xp, bg)
        y = jnp.concatenate(ys, axis=1)
        zv = z_ref[...]
        y_ref[...] = _rms_fwd(y * zv * _sigmoid(zv), nw_ref[...])

    vec = _full_spec((1, LANE))
    return pl.pallas_call(
        body, name="ssd_fwd", grid=(nc,),
        in_specs=[_row_spec(BLK, CONV_DIM), _row_spec(BLK, SSD_WIDTH), _row_spec(BLK, LANE), vec, vec, vec,
                  _full_spec((1, SSD_WIDTH))],
        out_specs=[_row_spec(BLK, SSD_WIDTH), pl.BlockSpec((1, N_PAIR, LANE, D_STATE), lambda c: (c, 0, 0, 0))],
        out_shape=[jax.ShapeDtypeStruct((s, SSD_WIDTH), F32), jax.ShapeDtypeStruct((nc, N_PAIR, LANE, D_STATE), F32)],
        scratch_shapes=[pltpu.VMEM((N_PAIR, LANE, D_STATE), F32)],
        compiler_params=_params(1),
    )(u, z, dtr, dtb, alog, dsk, nw)


def ssd_bwd(u, z, dtr, st, dyo, dtb, alog, dsk, nw):
    s = u.shape[0]
    nc = s // BLK

    def body(u_ref, z_ref, dt_ref, st_ref, dyo_ref, dtb_ref, alog_ref, dsk_ref, nw_ref,
             du_ref, dz_ref, ddt_ref, ddtb_ref, dalog_ref, ddsk_ref, dnw_ref, ds_ref):
        first = pl.program_id(0) == 0

        @pl.when(first)
        def _():
            ds_ref[...] = jnp.zeros_like(ds_ref)

        tri, pre, dt, a, heads = _ssd_chunk(u_ref, dt_ref, dtb_ref, alog_ref)
        shape = (BLK, LANE)
        lane = _iota(shape, 1)
        lo_lane = lane < HEAD_DIM
        lo_sub = _iota(shape, 0) < HEAD_DIM
        pairs = []
        ys = []
        for p in range(N_PAIR):
            g = p // 2
            h0, h1 = heads[2 * p], heads[2 * p + 1]
            bg = u_ref[:, B_LO + g * D_STATE:B_LO + (g + 1) * D_STATE]
            cg = u_ref[:, C_LO + g * D_STATE:C_LO + (g + 1) * D_STATE]
            xs = u_ref[:, p * LANE:(p + 1) * LANE]
            dtp = _pair_mix(lo_lane, h0["dt"], h1["dt"])
            xp = xs * dtp
            gm = dot_nt(cg, bg)
            m0, m1 = gm * h0["dm"], gm * h1["dm"]
            sp = st_ref[0, p]
            eap = _pair_mix(lo_lane, jnp.exp(h0["col"]), jnp.exp(h1["col"]))
            yo = eap * dot_nt(cg, sp)
            dskp = _pair_mix(lo_lane, _lane_col(dsk_ref[...], 2 * p), _lane_col(dsk_ref[...], 2 * p + 1))
            ys.append(_pair_mix(lo_lane, dot_nn(m0, xp), dot_nn(m1, xp)) + yo + xs * dskp)
            pairs.append(dict(bg=bg, cg=cg, xs=xs, dtp=dtp, xp=xp, gm=gm, m=(m0, m1), sp=sp, eap=eap, yo=yo, dskp=dskp))
        y = jnp.concatenate(ys, axis=1)
        zv = z_ref[...]
        sz = _sigmoid(zv)
        gate = zv * sz
        dyg, dnw = _rms_bwd(y * gate, nw_ref[...], dyo_ref[...])
        _acc(dnw_ref, first, dnw)
        dy = dyg * gate
        dz_ref[...] = dyg * y * (sz * (1.0 + zv * (1.0 - sz)))

        zeros = jnp.zeros(shape, F32)
        dacum_col = zeros
        dacum_row = zeros
        ddt = zeros
        ddsk = jnp.zeros((1, LANE), F32)
        dlast = jnp.zeros((1, LANE), F32)
        head_row = _iota((1, LANE), 1)
        sub = _iota(shape, 0)
        db = [zeros, zeros]
        dc = [zeros, zeros]
        for p in range(N_PAIR):
            g = p // 2
            q = pairs[p]
            dyp = dy[:, p * LANE:(p + 1) * LANE]
            dsn = ds_ref[p]
            t = dyp * q["xs"]
            dxs = dyp * q["dskp"]
            dcs = dyp * q["eap"]
            dc[g] = dc[g] + dot_nn(dcs, q["sp"])
            dsp = dot_tn(dcs, q["cg"])
            dea = dyp * q["yo"]
            elp = _pair_mix(lo_sub, jnp.exp(heads[2 * p]["last"]), jnp.exp(heads[2 * p + 1]["last"]))
            dsp = dsp + elp * dsn
            dels = dsn * q["sp"] * elp
            wp = _pair_mix(lo_lane, jnp.exp(heads[2 * p]["last"] - heads[2 * p]["col"]),
                           jnp.exp(heads[2 * p + 1]["last"] - heads[2 * p + 1]["col"]))
            dv = dot_nt(q["bg"], dsn)
            db[g] = db[g] + dot_nn(wp * q["xp"], dsn)
            dxp = dv * wp
            dwv = dv * q["xp"] * wp
            dgm = zeros
            for k in range(2):
                h = 2 * p + k
                mine = lo_lane if k == 0 else jnp.logical_not(lo_lane)
                mine_sub = lo_sub if k == 0 else jnp.logical_not(lo_sub)
                dyh = jnp.where(mine, dyp, 0.0)
                dm = dot_nt(dyh, q["xp"])
                dxp = dxp + dot_tn(q["m"][k], dyh)
                dgm = dgm + dm * heads[h]["dm"]
                e = dm * q["m"][k]
                onehot = lane == h
                dw_col = jnp.sum(jnp.where(mine, dwv, 0.0), axis=1, keepdims=True)
                col = (jnp.sum(e, axis=1, keepdims=True) + jnp.sum(jnp.where(mine, dea, 0.0), axis=1, keepdims=True)
                       - dw_col)
                dacum_col = dacum_col + jnp.where(onehot, col, 0.0)
                dacum_row = dacum_row - jnp.where(sub == h, jnp.sum(e, axis=0, keepdims=True), 0.0)
                dl = jnp.sum(dw_col) + jnp.sum(jnp.where(mine_sub, dels, 0.0))
                dlast = dlast + jnp.where(head_row == h, dl, 0.0)
                ddsk = ddsk + jnp.where(head_row == h, jnp.sum(jnp.where(mine, t, 0.0)), 0.0)
            dc[g] = dc[g] + dot_nn(dgm, q["bg"])
            db[g] = db[g] + dot_tn(dgm, q["cg"])
            dxs = dxs + dxp * q["dtp"]
            tt = dxp * q["xs"]
            for k in range(2):
                mine = lo_lane if k == 0 else jnp.logical_not(lo_lane)
                ddt = ddt + jnp.where(lane == 2 * p + k, jnp.sum(jnp.where(mine, tt, 0.0), axis=1, keepdims=True), 0.0)
            du_ref[:, p * LANE:(p + 1) * LANE] = dxs
            ds_ref[p] = dsp
        for g in range(2):
            du_ref[:, B_LO + g * D_STATE:B_LO + (g + 1) * D_STATE] = db[g]
            du_ref[:, C_LO + g * D_STATE:C_LO + (g + 1) * D_STATE] = dc[g]
        dacum = dacum_col + dacum_row.T + jnp.where(sub == BLK - 1, dlast, 0.0)
        dda = dot_tn(tri.astype(F32), dacum, exact=True)
        ddt = ddt + dda * a
        _acc(dalog_ref, first, jnp.sum(dda * dt, axis=0, keepdims=True) * a)
        dpre = ddt * _sigmoid(pre)
        ddt_ref[...] = dpre
        _acc(ddtb_ref, first, jnp.sum(dpre, axis=0, keepdims=True))
        _acc(ddsk_ref, first, ddsk)

    rev = lambda i: (nc - 1 - i, 0)
    vec = _full_spec((1, LANE))
    rows = lambda n: pl.BlockSpec((BLK, n), rev)
    return pl.pallas_call(
        body, name="ssd_bwd", grid=(nc,),
        in_specs=[rows(CONV_DIM), rows(SSD_WIDTH), rows(LANE),
                  pl.BlockSpec((1, N_PAIR, LANE, D_STATE), lambda i: (nc - 1 - i, 0, 0, 0)), rows(SSD_WIDTH),
                  vec, vec, vec, _full_spec((1, SSD_WIDTH))],
        out_specs=[rows(CONV_DIM), rows(SSD_WIDTH), rows(LANE), vec, vec, vec, _full_spec((1, SSD_WIDTH))],
        out_shape=[jax.ShapeDtypeStruct((s, CONV_DIM), F32), jax.ShapeDtypeStruct((s, SSD_WIDTH), F32),
                   jax.ShapeDtypeStruct((s, LANE), F32)] + [jax.ShapeDtypeStruct((1, LANE), F32)] * 3
        + [jax.ShapeDtypeStruct((1, SSD_WIDTH), F32)],
        scratch_shapes=[pltpu.VMEM((N_PAIR, LANE, D_STATE), F32)],
        compiler_params=_params(1),
    )(u, z, dtr, st, dyo, dtb, alog, dsk, nw)


SB_PAIRS = SB_WIDTH // LANE
SB_SCALE = HEAD_DIM ** -0.5


def _sb_logits(qh, kj, row0, col0):
    shape = (BLK, BLK)
    z = dot_nt(qh, kj)
    valid = (col0 + _iota(shape, 1)) < (row0 + _iota(shape, 0))
    t = jnp.log(1.0 + jnp.exp(-jnp.abs(z)))
    ls = jnp.minimum(z, 0.0) - t
    lk = jnp.where(valid, jnp.minimum(-z, 0.0) - t, 0.0)
    return valid, ls, lk


def _sb_specs(s):
    qspec = pl.BlockSpec((BLK, LANE), lambda p, i: (i, p))
    kspec = pl.BlockSpec((s, LANE), lambda p, i: (0, SB_PAIRS + p))
    vspec = pl.BlockSpec((s, LANE), lambda p, i: (0, 2 * SB_PAIRS + p))
    return qspec, kspec, vspec


def sb_fwd(qkv):
    s = qkv.shape[0]
    nq = s // BLK

    def body(q_ref, k_ref, v_ref, o_ref, t_ref):
        qi = pl.program_id(1)
        shape = (BLK, LANE)
        lo_lane = _iota(shape, 1) < HEAD_DIM
        q = q_ref[...] * SB_SCALE
        qs = (jnp.where(lo_lane, q, 0.0), jnp.where(lo_lane, 0.0, q))
        later = (_iota((BLK, BLK), 0) > _iota((BLK, BLK), 1)).astype(F32)

        def step(jj, carry):
            j = qi - jj
            kj = k_ref[pl.ds(pl.multiple_of(j * BLK, BLK), BLK), :]
            vj = v_ref[pl.ds(pl.multiple_of(j * BLK, BLK), BLK), :]
            out = []
            for k in range(2):
                r, o = carry[2 * k], carry[2 * k + 1]
                valid, ls, lk = _sb_logits(qs[k], kj, qi * BLK, j * BLK)
                after = r + dot_nn(lk, later, exact=True)
                w = jnp.where(valid, jnp.exp(ls + after), 0.0)
                out += [r + jnp.sum(lk, axis=1, keepdims=True), o + dot_nn(w, vj)]
            return tuple(out)

        zc = jnp.zeros((BLK, 1), F32)
        zo = jnp.zeros(shape, F32)
        r0, o0, r1, o1 = lax.fori_loop(0, qi + 1, step, (zc, zo, zc, zo))
        o_ref[...] = jnp.where(lo_lane, o0, o1)
        t_ref[...] = jnp.where(lo_lane, r0, r1)

    ospec = pl.BlockSpec((BLK, LANE), lambda p, i: (i, p))
    return pl.pallas_call(
        body, name="sb_fwd", grid=(SB_PAIRS, nq),
        in_specs=list(_sb_specs(s)), out_specs=[ospec, ospec],
        out_shape=[jax.ShapeDtypeStruct((s, SB_WIDTH), F32)] * 2,
        compiler_params=_params(2),
    )(qkv, qkv, qkv)


def sb_bwd(qkv, tot, do, do_col=0):
    s = qkv.shape[0]
    nq = s // BLK

    def body(q_ref, k_ref, v_ref, t_ref, do_ref, dq_ref, dk_ref, dv_ref):
        qi = pl.program_id(1)
        shape = (BLK, LANE)
        lo_lane = _iota(shape, 1) < HEAD_DIM
        q = q_ref[...] * SB_SCALE
        dov = do_ref[...]
        tv = t_ref[...]
        qs = (jnp.where(lo_lane, q, 0.0), jnp.where(lo_lane, 0.0, q))
        dos = (jnp.where(lo_lane, dov, 0.0), jnp.where(lo_lane, 0.0, dov))
        tots = (_lane_col(tv, 0), _lane_col(tv, HEAD_DIM))
        sq = (BLK, BLK)
        upto = (_iota(sq, 0) <= _iota(sq, 1)).astype(F32)
        before = (_iota(sq, 0) < _iota(sq, 1)).astype(F32)

        @pl.when(qi == 0)
        def _():
            dk_ref[...] = jnp.zeros_like(dk_ref)
            dv_ref[...] = jnp.zeros_like(dv_ref)

        def step(j, carry):
            rows = pl.ds(pl.multiple_of(j * BLK, BLK), BLK)
            kj = k_ref[rows, :]
            vj = v_ref[rows, :]
            out = []
            dk = jnp.zeros(shape, F32)
            dv = jnp.zeros(shape, F32)
            for k in range(2):
                pc, fc, dq = carry[3 * k:3 * k + 3]
                valid, ls, lk = _sb_logits(qs[k], kj, qi * BLK, j * BLK)
                a = tots[k] - (pc + dot_nn(lk, upto, exact=True))
                w = jnp.where(valid, jnp.exp(ls + a), 0.0)
                e = w * dot_nt(dos[k], vj)
                f = fc + dot_nn(e, before, exact=True)
                sg = jnp.exp(ls)
                dz = jnp.where(valid, e * (1.0 - sg) - f * sg, 0.0)
                dk = dk + dot_tn(dz, qs[k])
                dv = dv + dot_tn(w, dos[k])
                out += [pc + jnp.sum(lk, axis=1, keepdims=True), fc + jnp.sum(e, axis=1, keepdims=True),
                        dq + dot_nn(dz, kj)]
            dk_ref[rows, :] += dk
            dv_ref[rows, :] += dv
            return tuple(out)

        zc = jnp.zeros((BLK, 1), F32)
        zo = jnp.zeros(shape, F32)
        res = lax.fori_loop(0, qi + 1, step, (zc, zc, zo, zc, zc, zo))
        dq_ref[...] = SB_SCALE * jnp.where(lo_lane, res[2], res[5])

    qspec, kspec, vspec = _sb_specs(s)
    blk = pl.BlockSpec((BLK, LANE), lambda p, i: (i, p))
    acc = pl.BlockSpec((s, LANE), lambda p, i: (0, p))
    return pl.pallas_call(
        body, name="sb_bwd", grid=(SB_PAIRS, nq),
        in_specs=[qspec, kspec, vspec, blk, pl.BlockSpec((BLK, LANE), lambda p, i: (i, do_col + p))],
        out_specs=[blk, acc, acc],
        out_shape=[jax.ShapeDtypeStruct((s, SB_WIDTH), F32)] * 3,
        compiler_params=_params(2),
    )(qkv, qkv, qkv, tot, do)


POOL_GROUP_DIM = POOL_WIDTH // len(POOL_WINDOWS)


def _pool_consts(c):
    t = _iota((BLK, 2 * BLK), 0)
    j = _iota((BLK, 2 * BLK), 1) - BLK
    bands = [jnp.logical_and(j <= t, j > t - win).astype(F32) for win in POOL_WINDOWS]
    group = _iota((BLK, POOL_WIDTH), 1) // POOL_GROUP_DIM
    pos = c * BLK + _iota((BLK, POOL_WIDTH), 0)
    win = jnp.zeros((BLK, POOL_WIDTH), jnp.int32)
    for gi, wn in enumerate(POOL_WINDOWS):
        win = jnp.where(group == gi, wn, win)
    inv = 1.0 / jnp.minimum(pos + 1, win).astype(F32)
    return bands, group, inv


def _pool_pooled(ext, cur, bands, group, inv):
    sums = jnp.zeros(cur.shape, F32)
    for gi, band in enumerate(bands):
        sums = jnp.where(group == gi, dot_nn(band, ext, exact=True), sums)
    return sums * inv - cur


def pool_fwd(p, wblk, pb, ps):
    s, n = p.shape

    def body(cur_ref, prev_ref, w_ref, pb_ref, ps_ref, o_ref):
        c = pl.program_id(0)
        cur = cur_ref[...]
        prev = jnp.where(c > 0, prev_ref[...], 0.0)
        bands, group, inv = _pool_consts(c)
        pooled = _pool_pooled(jnp.concatenate([prev, cur], axis=0), cur, bands, group, inv)
        o_ref[...] = (dot_nn(pooled, w_ref[...]) + pb_ref[...]) * ps_ref[...]

    return pl.pallas_call(
        body, name="pool_fwd", grid=(s // BLK,),
        in_specs=[pl.BlockSpec((BLK, n), lambda c: (c, 0)), pl.BlockSpec((BLK, n), lambda c: (jnp.maximum(c - 1, 0), 0)),
                  _full_spec((n, n)), _full_spec((1, n)), _full_spec((1, n))],
        out_specs=pl.BlockSpec((BLK, n), lambda c: (c, 0)), out_shape=jax.ShapeDtypeStruct((s, n), F32),
        compiler_params=_params(1),
    )(p, p, wblk, pb, ps)


def pool_bwd(p, wblk, pb, ps, dout, do_col=0):
    s, n = p.shape
    nb = s // BLK

    def body(cur_ref, prev_ref, w_ref, pb_ref, ps_ref, do_ref, dp_ref, dw_ref, dpb_ref, dps_ref, carry_ref):
        i = pl.program_id(0)
        c = nb - 1 - i
        first = i == 0
        cur = cur_ref[...]
        prev = jnp.where(c > 0, prev_ref[...], 0.0)
        bands, group, inv = _pool_consts(c)
        pooled = _pool_pooled(jnp.concatenate([prev, cur], axis=0), cur, bands, group, inv)
        mixed = dot_nn(pooled, w_ref[...]) + pb_ref[...]
        dov = do_ref[...]
        dmixed = dov * ps_ref[...]
        _acc(dps_ref, first, jnp.sum(dov * mixed, axis=0, keepdims=True))
        _acc(dpb_ref, first, jnp.sum(dmixed, axis=0, keepdims=True))
        _acc(dw_ref, first, dot_tn(pooled, dmixed))
        dpooled = dot_nt(dmixed, w_ref[...])
        dsums = dpooled * inv
        dext = jnp.zeros((2 * BLK, n), F32)
        for gi, band in enumerate(bands):
            dext = dext + dot_tn(band, jnp.where(group == gi, dsums, 0.0), exact=True)

        @pl.when(first)
        def _():
            carry_ref[...] = jnp.zeros_like(carry_ref)

        dp_ref[...] = dext[BLK:] - dpooled + carry_ref[...]
        carry_ref[...] = dext[:BLK]

    rev = lambda i: (nb - 1 - i, 0)
    return pl.pallas_call(
        body, name="pool_bwd", grid=(nb,),
        in_specs=[pl.BlockSpec((BLK, n), rev), pl.BlockSpec((BLK, n), lambda i: (jnp.maximum(nb - 2 - i, 0), 0)),
                  _full_spec((n, n)), _full_spec((1, n)), _full_spec((1, n)),
                  pl.BlockSpec((BLK, n), lambda i: (nb - 1 - i, do_col))],
        out_specs=[pl.BlockSpec((BLK, n), rev), _full_spec((n, n)), _full_spec((1, n)), _full_spec((1, n))],
        out_shape=[jax.ShapeDtypeStruct((s, n), F32), jax.ShapeDtypeStruct((n, n), F32),
                   jax.ShapeDtypeStruct((1, n), F32), jax.ShapeDtypeStruct((1, n), F32)],
        scratch_shapes=[pltpu.VMEM((BLK, n), F32)],
        compiler_params=_params(1),
    )(p, p, wblk, pb, ps, dout)


def _row_tile(rows):
    for t in (256, 128, 64, 32, 16, 8):
        if rows % t == 0:
            return t
    return rows


def adamw(w, g, m, v):
    n, rows, cols = w.shape
    tr = _row_tile(rows)

    def body(w_ref, g_ref, m_ref, v_ref, d_ref, nm_ref, nv_ref):
        gv = g_ref[...]
        nm = ADAM_B1 * m_ref[...] + (1.0 - ADAM_B1) * gv
        nv = ADAM_B2 * v_ref[...] + (1.0 - ADAM_B2) * (gv * gv)
        m_hat = nm / (1.0 - ADAM_B1 ** ADAM_STEP)
        v_hat = nv / (1.0 - ADAM_B2 ** ADAM_STEP)
        d_ref[...] = -ADAM_LR * (m_hat / (jnp.sqrt(v_hat) + ADAM_EPS) + ADAM_WD * w_ref[...])
        nm_ref[...] = nm
        nv_ref[...] = nv

    spec = pl.BlockSpec((1, tr, cols), lambda i, j: (i, j, 0))
    return pl.pallas_call(
        body, name="adamw", grid=(n, rows // tr), in_specs=[spec] * 4, out_specs=[spec] * 3,
        out_shape=[jax.ShapeDtypeStruct(w.shape, F32)] * 3, compiler_params=_params(2),
    )(w, g, m, v)


def slab_sum(srcs, n_out, out_dtype):
    _, rows, cols = srcs[0][0].shape
    tr = _row_tile(rows)
    sel = jnp.stack([jnp.asarray(base, jnp.int32) for _, base, _ in srcs])

    def body(sel_ref, *refs):
        acc = refs[0][...].astype(F32)
        for r in refs[1:-1]:
            acc = acc + r[...].astype(F32)
        refs[-1][...] = acc.astype(out_dtype)

    def in_spec(k, step):
        return pl.BlockSpec((None, tr, cols), lambda o, i, sel_ref: (sel_ref[k] + step * o, i, 0))

    return pl.pallas_call(
        body, name="slab_sum",
        grid_spec=pltpu.PrefetchScalarGridSpec(
            num_scalar_prefetch=1, grid=(n_out, rows // tr),
            in_specs=[in_spec(k, step) for k, (_, _, step) in enumerate(srcs)],
            out_specs=pl.BlockSpec((None, tr, cols), lambda o, i, sel_ref: (o, i, 0))),
        out_shape=jax.ShapeDtypeStruct((n_out, rows, cols), out_dtype), compiler_params=_params(2),
    )(sel, *[a for a, _, _ in srcs])


ICI_FLIPS = ((1, 0, 0), (0, 1, 0), (1, 1, 0))
D2D_FLIPS = ((0, 0, 1),)
ANY = pl.BlockSpec(memory_space=pl.ANY)


def _me():
    return lax.axis_index("x"), lax.axis_index("y"), lax.axis_index("c")


def _flipped(me, flip):
    return tuple(1 - m if f else m for m, f in zip(me, flip))


def _chip(dev):
    return 2 * dev[0] + dev[1]


def exchange(x, n_out, flips, src_slot, dst_slot, own, name):
    n = len(flips)

    def body(x_ref, o_ref, send_sems, recv_sems, own_sem):
        me = _me()
        if own is not None:
            mine = pltpu.make_async_copy(x_ref.at[own[0](me)], o_ref.at[own[1](me)], own_sem)
            mine.start()

        def copy(i, sender):
            return pltpu.make_async_remote_copy(
                src_ref=x_ref.at[src_slot(i, me)], dst_ref=o_ref.at[dst_slot(i, sender)],
                send_sem=send_sems.at[i], recv_sem=recv_sems.at[i],
                device_id=_flipped(me, flips[i]), device_id_type=MESH)

        sends = [copy(i, me) for i in range(n)]
        for cp in sends:
            cp.start()
        for i in range(n):
            copy(i, _flipped(me, flips[i])).wait_recv()
        for cp in sends:
            cp.wait_send()
        if own is not None:
            mine.wait()

    return pl.pallas_call(
        body, name=name, in_specs=[ANY], out_specs=ANY,
        out_shape=jax.ShapeDtypeStruct((n_out,) + x.shape[1:], x.dtype),
        scratch_shapes=[pltpu.SemaphoreType.DMA((n,)), pltpu.SemaphoreType.DMA((n,)), pltpu.SemaphoreType.DMA(())],
    )(x)


def all_gather(x):
    chips = exchange(x[None], 4, ICI_FLIPS, lambda i, me: 0, lambda i, sender: _chip(sender),
                     (lambda me: 0, _chip), "gather_ici")
    return exchange(chips[None], 2, D2D_FLIPS, lambda i, me: 0, lambda i, sender: sender[2],
                    (lambda me: 0, lambda me: me[2]), "gather_d2d")


def reduce_scatter(parts):
    x, y, c = _me()
    sib = exchange(parts, 1, D2D_FLIPS, lambda i, me: 1 - me[2], lambda i, sender: 0, None, "scatter_d2d")
    n_chip = parts.shape[1]
    rows, cols = parts.shape[2:]
    mine = parts.reshape(2 * n_chip, rows, cols)
    sib = sib.reshape(n_chip, rows, cols)
    chip_sum = slab_sum([(mine, n_chip * c, 1), (sib, 0, 1)], n_chip, BF16)
    got = exchange(chip_sum, len(ICI_FLIPS), ICI_FLIPS, lambda i, me: _chip(_flipped(me, ICI_FLIPS[i])),
                   lambda i, sender: i, None, "scatter_ici")
    return slab_sum([(chip_sum, 2 * x + y, 0)] + [(got, i, 0) for i in range(len(ICI_FLIPS))], 1, F32)[0]


def all_reduce_small(v):
    flips = D2D_FLIPS + ICI_FLIPS[:2]

    def body(v_ref, o_ref, got_ref, send_sems, recv_sems):
        me = _me()
        o_ref[...] = v_ref[...]
        for i, flip in enumerate(flips):
            cp = pltpu.make_async_remote_copy(
                src_ref=o_ref, dst_ref=got_ref.at[i], send_sem=send_sems.at[i], recv_sem=recv_sems.at[i],
                device_id=_flipped(me, flip), device_id_type=MESH)
            cp.start()
            cp.wait()
            o_ref[...] = o_ref[...] + got_ref[i]

    vm = pl.BlockSpec(memory_space=pltpu.VMEM)
    return pl.pallas_call(
        body, name="all_reduce_small", in_specs=[vm], out_specs=vm, out_shape=jax.ShapeDtypeStruct(v.shape, F32),
        scratch_shapes=[pltpu.VMEM((len(flips),) + v.shape, F32), pltpu.SemaphoreType.DMA((len(flips),)),
                        pltpu.SemaphoreType.DMA((len(flips),))],
    )(v)


def _cols_from_gather(g):
    l, r, c = g.shape[2:]
    return g.transpose(2, 3, 1, 0, 4).reshape(l, r, N_DEV * c)


def _rows_from_gather(g):
    l, r, c = g.shape[2:]
    return g.transpose(2, 1, 0, 3, 4).reshape(l, N_DEV * r, c)


def _cols_to_parts(dw):
    l, r, c8 = dw.shape
    c = c8 // N_DEV
    return dw.reshape(l, r, 4, 2, c).transpose(3, 2, 0, 1, 4).reshape(2, 4, l * r, c)


def _rows_to_parts(dw):
    l, r8, c = dw.shape
    r = r8 // N_DEV
    return dw.reshape(l, 4, 2, r, c).transpose(2, 1, 0, 3, 4).reshape(2, 4, l * r, c)


def _perm_in(w):
    pad = jnp.zeros(w.shape[:-1] + (D_IN_PAD - D_IN_PROJ,), w.dtype)
    return jnp.concatenate([w[..., :DT_LO], w[..., DT_HI:], w[..., DT_LO:DT_HI], pad], axis=-1)


def _unperm_in(dw):
    n = D_IN_PROJ - (DT_HI - DT_LO)
    return jnp.concatenate([dw[..., :DT_LO], dw[..., n:D_IN_PROJ], dw[..., DT_LO:n]], axis=-1)


def _pad_lanes(v):
    return jnp.pad(v, (0, LANE - v.shape[0]))[None]


def _block_diag(w):
    g, n, _ = w.shape
    out = jnp.zeros((g * n, g * n), w.dtype)
    for i in range(g):
        out = out.at[i * n:(i + 1) * n, i * n:(i + 1) * n].set(w[i])
    return out


def _pack(arrs):
    flat = []
    for a in arrs:
        a = a.reshape(-1)
        flat.append(jnp.pad(a, (0, -a.shape[0] % LANE)))
    return jnp.concatenate(flat).reshape(-1, LANE)


def _unpack(buf, shapes):
    out = []
    lo = 0
    buf = buf.reshape(-1)
    for shp in shapes:
        n = 1
        for k in shp:
            n *= k
        out.append(buf[lo:lo + n].reshape(shp))
        lo += n + (-n % LANE)
    return out


def _layer_params(full, l):
    return dict(
        n1w=full["norm1_w"][l][None], w_in=full["w_in"][l], cw=jnp.pad(full["conv_w"][l], ((0, 8 - CONV_WIDTH), (0, 0))),
        cb=full["conv_b"][l][None], dtb=_pad_lanes(full["dt_bias"][l]), alog=_pad_lanes(full["a_log"][l]),
        dsk=_pad_lanes(full["d_skip"][l]), snw=full["ssd_norm_w"][l][None], wblk=_block_diag(full["pool_w"][l]),
        pb=full["pool_b"][l].reshape(1, POOL_WIDTH), ps=full["pool_scale"][l][None], w_out=full["w_out"][l],
        n2w=full["norm2_w"][l][None], wg=full["w_gate"][l], wu=full["w_up"][l], wd=full["w_down"][l])


def _layer_fwd(x, p):
    z, xbc, qkv, pp, dtr, h1 = inproj_fwd(x, p["n1w"], p["w_in"])
    u = conv_fwd(xbc, p["cw"], p["cb"])
    y_ssd, st = ssd_fwd(u, z, dtr, p["dtb"], p["alog"], p["dsk"], p["snw"])
    o, tot = sb_fwd(qkv)
    yp = pool_fwd(pp, p["wblk"], p["pb"], p["ps"])
    ycat = jnp.concatenate([y_ssd, o, yp], axis=1)
    x_mid = outproj_fwd(ycat, p["w_out"], x)
    x_out, g, uu = ffn_fwd(x_mid, p["n2w"], p["wg"], p["wu"], p["wd"])
    return x_out, dict(x=x, z=z, xbc=xbc, qkv=qkv, pp=pp, dtr=dtr, h1=h1, u=u, st=st, tot=tot, ycat=ycat, x_mid=x_mid,
                       g=g, uu=uu)


def _layer_bwd(dxo, sv, p):
    dx_mid, dn2w, a, dg, du, h2 = ffn_bwd(dxo, sv["x_mid"], sv["g"], sv["uu"], p["n2w"], p["wg"], p["wu"], p["wd"])
    gr = dict(norm2_w=dn2w[0], w_down=mm_tn(a, dxo), w_gate=mm_tn(h2, dg), w_up=mm_tn(h2, du))
    dycat = outproj_bwd(dx_mid, p["w_out"])
    gr["w_out"] = mm_tn(sv["ycat"], dx_mid)
    dp, dwblk, dpb, dps = pool_bwd(sv["pp"], p["wblk"], p["pb"], p["ps"], dycat,
                                   (SSD_WIDTH + SB_WIDTH) // POOL_WIDTH)
    n = POOL_GROUP_DIM
    gr["pool_w"] = jnp.stack([dwblk[i * n:(i + 1) * n, i * n:(i + 1) * n] for i in range(len(POOL_WINDOWS))])
    gr["pool_b"] = dpb.reshape(len(POOL_WINDOWS), n)
    gr["pool_scale"] = dps[0]
    dq, dk, dv = sb_bwd(sv["qkv"], sv["tot"], dycat, SSD_WIDTH // LANE)
    du_, dz, ddtr, ddtb, dalog, ddsk, dsnw = ssd_bwd(sv["u"], sv["z"], sv["dtr"], sv["st"], dycat, p["dtb"], p["alog"],
                                                     p["dsk"], p["snw"])
    gr.update(dt_bias=ddtb[0, :SSD_HEADS], a_log=dalog[0, :SSD_HEADS], d_skip=ddsk[0, :SSD_HEADS], ssd_norm_w=dsnw[0])
    dxbc, dcw, dcb = conv_bwd(sv["xbc"], p["cw"], p["cb"], du_)
    gr.update(conv_w=dcw[:CONV_WIDTH], conv_b=dcb[0])
    dproj = jnp.concatenate([dz, dxbc, dq, dk, dv, dp, ddtr], axis=1)
    dx, dn1w = inproj_bwd(dproj, p["w_in"], sv["x"], p["n1w"], dx_mid)
    gr.update(norm1_w=dn1w[0], w_in=mm_tn(sv["h1"], dproj))
    return dx, gr


def local_step(x, tgt, full):
    depth = full["w_in"].shape[0]
    params = [_layer_params(full, l) for l in range(depth)]
    saved = []
    for p in params:
        x, sv = _layer_fwd(x, p)
        saved.append(sv)
    loss, dx, dfw = head_loss(x, full["final_norm_w"][None], tgt)
    grads = []
    for p, sv in zip(reversed(params), reversed(saved)):
        dx, gr = _layer_bwd(dx, sv, p)
        grads.append(gr)
    grads.reverse()
    out = {k: jnp.stack([gr[k] for gr in grads]) for k in grads[0]}
    out["final_norm_w"] = dfw[0]
    return loss, dx, out


WEIGHTS = ("norm1_w", "w_in", "conv_w", "conv_b", "dt_bias", "a_log", "d_skip", "ssd_norm_w", "pool_w", "pool_b",
           "pool_scale", "w_out", "norm2_w", "w_gate", "w_up", "w_down", "final_norm_w")
COL_SHARDED = ("w_in", "w_gate", "w_up")
ROW_SHARDED = ("w_out", "w_down")
SMALL = tuple(k for k in WEIGHTS if k not in COL_SHARDED + ROW_SHARDED)


def kernel(x, norm1_w, w_in, conv_w, conv_b, dt_bias, a_log, d_skip, ssd_norm_w, pool_w, pool_b, pool_scale, w_out, norm2_w, w_gate, w_up, w_down, final_norm_w, loss_target, m_norm1_w, m_w_in, m_conv_w, m_conv_b, m_dt_bias, m_a_log, m_d_skip, m_ssd_norm_w, m_pool_w, m_pool_b, m_pool_scale, m_w_out, m_norm2_w, m_w_gate, m_w_up, m_w_down, m_final_norm_w, v_norm1_w, v_w_in, v_conv_w, v_conv_b, v_dt_bias, v_a_log, v_d_skip, v_ssd_norm_w, v_pool_w, v_pool_b, v_pool_scale, v_w_out, v_norm2_w, v_w_gate, v_w_up, v_w_down, v_final_norm_w):
    args = dict(locals())
    w = {k: args[k] for k in WEIGHTS}
    m = {k: args["m_" + k] for k in WEIGHTS}
    v = {k: args["v_" + k] for k in WEIGHTS}
    mx, my, mc = _me()
    dev = 4 * mx + 2 * my + mc

    full = dict(w)
    for k in COL_SHARDED:
        full[k] = _cols_from_gather(all_gather(w[k].astype(BF16)))
    for k in ROW_SHARDED:
        full[k] = _rows_from_gather(all_gather(w[k].astype(BF16)))
    full["w_in"] = _perm_in(full["w_in"])
    full["conv_w"] = _cols_from_gather(all_gather(conv_w))

    loss, dx, g = local_step(x[0], loss_target[0], full)
    g["w_in"] = _unperm_in(g["w_in"])

    grads = {}
    for k in COL_SHARDED:
        grads[k] = reduce_scatter(_cols_to_parts(g[k])).reshape(w[k].shape)
    for k in ROW_SHARDED:
        grads[k] = reduce_scatter(_rows_to_parts(g[k])).reshape(w[k].shape)
    small_shapes = [(1, LANE)] + [g[k].shape for k in SMALL]
    summed = _unpack(all_reduce_small(_pack([loss] + [g[k] for k in SMALL])), small_shapes)
    loss = summed[0][0, 0]
    grads.update(zip(SMALL, summed[1:]))
    n_cw = conv_w.shape[-1]
    grads["conv_w"] = lax.dynamic_slice_in_dim(grads["conv_w"], dev * n_cw, n_cw, axis=2)

    delta, new_m, new_v = {}, {}, {}
    for k in COL_SHARDED + ROW_SHARDED:
        delta[k], new_m[k], new_v[k] = adamw(w[k], grads[k], m[k], v[k])
    shapes = [w[k].shape for k in SMALL]
    packed = [_pack([t[k] for k in SMALL])[None] for t in (w, grads, m, v)]
    for dst, buf in zip((delta, new_m, new_v), adamw(*packed)):
        dst.update(zip(SMALL, _unpack(buf, shapes)))
    return (loss, dx[None], *[grads[k] for k in WEIGHTS], *[delta[k] for k in WEIGHTS],
            *[new_m[k] for k in WEIGHTS], *[new_v[k] for k in WEIGHTS])
```

```python
import functools

import jax
import jax.numpy as jnp
from jax import lax
from jax.experimental import pallas as pl
from jax.experimental.pallas import tpu as pltpu

F32 = jnp.float32
BF16 = jnp.bfloat16
HIGHEST = lax.Precision.HIGHEST
MESH = pl.DeviceIdType.MESH

EPS = 1e-6
D_MODEL = 1024
SSD_WIDTH = 512
SSD_HEADS = 8
HEAD_DIM = 64
D_STATE = 128
CONV_WIDTH = 4
CONV_DIM = 1024
SB_WIDTH = 256
POOL_WIDTH = 256
POOL_WINDOWS = (2, 4, 8, 16)
D_IN_PROJ = 2568
D_FF = 2816
N_DEV = 8
DEPTH = 4
SEG = (512, 1024, 768, 256, 128)
D_IN_PAD = sum(SEG)
DT_LO, DT_HI = 1536, 1544

LANE = 128
BLK = 128
ROW_TILE = 256
VMEM_LIMIT = 56 * 2**20

ADAM_LR, ADAM_B1, ADAM_B2, ADAM_EPS, ADAM_WD, ADAM_STEP = 0.001, 0.9, 0.999, 1e-08, 0.01, 10


def _params(n_axes=1, vmem=None):
    return pltpu.CompilerParams(dimension_semantics=("arbitrary",) * n_axes, vmem_limit_bytes=vmem)


def _dot(a, b, dims, exact=False):
    if exact:
        return lax.dot_general(a.astype(F32), b.astype(F32), (dims, ((), ())), precision=HIGHEST,
                               preferred_element_type=F32)
    return lax.dot_general(a.astype(BF16), b.astype(BF16), (dims, ((), ())), preferred_element_type=F32)


def dot_nn(a, b, exact=False):
    return _dot(a, b, ((1,), (0,)), exact)


def dot_nt(a, b, exact=False):
    return _dot(a, b, ((1,), (1,)), exact)


def dot_tn(a, b, exact=False):
    return _dot(a, b, ((0,), (0,)), exact)


def _iota(shape, axis):
    return lax.broadcasted_iota(jnp.int32, shape, axis)


def _lane_col(x, h):
    return jnp.sum(jnp.where(_iota(x.shape, 1) == h, x, 0.0), axis=1, keepdims=True)


def _sub_row(x, h):
    return jnp.sum(jnp.where(_iota(x.shape, 0) == h, x, 0.0), axis=0, keepdims=True)


def _sigmoid(x):
    return 1.0 / (1.0 + jnp.exp(-x))


def _rms_fwd(x, w):
    r = lax.rsqrt(jnp.mean(x * x, axis=-1, keepdims=True) + EPS)
    return x * r * w


def _rms_bwd(x, w, dy):
    r = lax.rsqrt(jnp.mean(x * x, axis=-1, keepdims=True) + EPS)
    xh = x * r
    dxh = dy * w
    dx = r * (dxh - xh * jnp.mean(dxh * xh, axis=-1, keepdims=True))
    return dx, jnp.sum(dy * xh, axis=0, keepdims=True)


def _acc(ref, first, val):
    @pl.when(first)
    def _():
        ref[...] = val

    @pl.when(jnp.logical_not(first))
    def _():
        ref[...] += val


def _row_spec(tm, n):
    return pl.BlockSpec((tm, n), lambda i: (i, 0))


def _full_spec(shape):
    return pl.BlockSpec(shape, lambda *_: (0,) * len(shape))


def inproj_fwd(x, nw, w):
    s, d = x.shape
    tm = min(ROW_TILE, s)

    def body(x_ref, nw_ref, w_ref, z_ref, xbc_ref, qkv_ref, p_ref, dt_ref, h_ref):
        h = _rms_fwd(x_ref[...], nw_ref[...]).astype(BF16)
        h_ref[...] = h
        lo = 0
        for ref, n in zip((z_ref, xbc_ref, qkv_ref, p_ref, dt_ref), SEG):
            ref[...] = dot_nn(h, w_ref[:, lo:lo + n])
            lo += n

    return pl.pallas_call(
        body, name="inproj_fwd", grid=(s // tm,),
        in_specs=[_row_spec(tm, d), _full_spec((1, d)), _full_spec(w.shape)],
        out_specs=[_row_spec(tm, n) for n in SEG] + [_row_spec(tm, d)],
        out_shape=[jax.ShapeDtypeStruct((s, n), F32) for n in SEG] + [jax.ShapeDtypeStruct((s, d), BF16)],
        compiler_params=_params(1, VMEM_LIMIT),
    )(x, nw, w)


def inproj_bwd(dproj, w, x, nw, dres):
    s, d = x.shape
    tm = min(ROW_TILE, s)

    def body(dp_ref, w_ref, x_ref, nw_ref, dres_ref, dx_ref, dnw_ref):
        dh = dot_nt(dp_ref[...], w_ref[...])
        dx, dnw = _rms_bwd(x_ref[...], nw_ref[...], dh)
        dx_ref[...] = dres_ref[...] + dx
        _acc(dnw_ref, pl.program_id(0) == 0, dnw)

    return pl.pallas_call(
        body, name="inproj_bwd", grid=(s // tm,),
        in_specs=[_row_spec(tm, dproj.shape[1]), _full_spec(w.shape), _row_spec(tm, d), _full_spec((1, d)),
                  _row_spec(tm, d)],
        out_specs=[_row_spec(tm, d), _full_spec((1, d))],
        out_shape=[jax.ShapeDtypeStruct((s, d), F32), jax.ShapeDtypeStruct((1, d), F32)],
        compiler_params=_params(1, VMEM_LIMIT),
    )(dproj, w, x, nw, dres)


def outproj_fwd(y, w, res):
    s, d = res.shape
    tm = min(ROW_TILE, s)

    def body(y_ref, w_ref, r_ref, o_ref):
        o_ref[...] = r_ref[...] + dot_nn(y_ref[...], w_ref[...])

    return pl.pallas_call(
        body, name="outproj_fwd", grid=(s // tm,),
        in_specs=[_row_spec(tm, y.shape[1]), _full_spec(w.shape), _row_spec(tm, d)],
        out_specs=_row_spec(tm, d), out_shape=jax.ShapeDtypeStruct((s, d), F32),
        compiler_params=_params(1, VMEM_LIMIT),
    )(y, w, res)


def outproj_bwd(dx, w):
    s, d = dx.shape
    tm = min(ROW_TILE, s)

    def body(dx_ref, w_ref, o_ref):
        o_ref[...] = dot_nt(dx_ref[...], w_ref[...])

    return pl.pallas_call(
        body, name="outproj_bwd", grid=(s // tm,),
        in_specs=[_row_spec(tm, d), _full_spec(w.shape)],
        out_specs=_row_spec(tm, w.shape[0]), out_shape=jax.ShapeDtypeStruct((s, w.shape[0]), F32),
        compiler_params=_params(1, VMEM_LIMIT),
    )(dx, w)


def ffn_fwd(x, nw, wg, wu, wd):
    s, d = x.shape
    f = wg.shape[1]
    tm = min(ROW_TILE, s)

    def body(x_ref, nw_ref, wg_ref, wu_ref, wd_ref, o_ref, g_ref, u_ref):
        xv = x_ref[...]
        h = _rms_fwd(xv, nw_ref[...]).astype(BF16)
        g = dot_nn(h, wg_ref[...])
        u = dot_nn(h, wu_ref[...])
        g_ref[...] = g.astype(BF16)
        u_ref[...] = u.astype(BF16)
        o_ref[...] = xv + dot_nn(g * _sigmoid(g) * u, wd_ref[...])

    return pl.pallas_call(
        body, name="ffn_fwd", grid=(s // tm,),
        in_specs=[_row_spec(tm, d), _full_spec((1, d)), _full_spec(wg.shape), _full_spec(wu.shape),
                  _full_spec(wd.shape)],
        out_specs=[_row_spec(tm, d), _row_spec(tm, f), _row_spec(tm, f)],
        out_shape=[jax.ShapeDtypeStruct((s, d), F32), jax.ShapeDtypeStruct((s, f), BF16),
                   jax.ShapeDtypeStruct((s, f), BF16)],
        compiler_params=_params(1, VMEM_LIMIT),
    )(x, nw, wg, wu, wd)


def ffn_bwd(dxo, x, g, u, nw, wg, wu, wd):
    s, d = x.shape
    f = wg.shape[1]
    tm = min(ROW_TILE, s)

    def body(dxo_ref, x_ref, g_ref, u_ref, nw_ref, wg_ref, wu_ref, wd_ref, dx_ref, dnw_ref, a_ref, dg_ref,
             du_ref, h_ref):
        dxo_v = dxo_ref[...]
        xv = x_ref[...]
        da = dot_nt(dxo_v, wd_ref[...])
        gv = g_ref[...].astype(F32)
        uv = u_ref[...].astype(F32)
        sg = _sigmoid(gv)
        sl = gv * sg
        a_ref[...] = (sl * uv).astype(BF16)
        dg = (da * uv * (sg * (1.0 + gv * (1.0 - sg)))).astype(BF16)
        du = (da * sl).astype(BF16)
        dg_ref[...] = dg
        du_ref[...] = du
        dh = dot_nt(dg, wg_ref[...]) + dot_nt(du, wu_ref[...])
        h_ref[...] = _rms_fwd(xv, nw_ref[...]).astype(BF16)
        dx, dnw = _rms_bwd(xv, nw_ref[...], dh)
        dx_ref[...] = dxo_v + dx
        _acc(dnw_ref, pl.program_id(0) == 0, dnw)

    return pl.pallas_call(
        body, name="ffn_bwd", grid=(s // tm,),
        in_specs=[_row_spec(tm, d), _row_spec(tm, d), _row_spec(tm, f), _row_spec(tm, f), _full_spec((1, d)),
                  _full_spec(wg.shape), _full_spec(wu.shape), _full_spec(wd.shape)],
        out_specs=[_row_spec(tm, d), _full_spec((1, d)), _row_spec(tm, f), _row_spec(tm, f), _row_spec(tm, f),
                   _row_spec(tm, d)],
        out_shape=[jax.ShapeDtypeStruct((s, d), F32), jax.ShapeDtypeStruct((1, d), F32),
                   jax.ShapeDtypeStruct((s, f), BF16), jax.ShapeDtypeStruct((s, f), BF16),
                   jax.ShapeDtypeStruct((s, f), BF16), jax.ShapeDtypeStruct((s, d), BF16)],
        compiler_params=_params(1, VMEM_LIMIT),
    )(dxo, x, g, u, nw, wg, wu, wd)


def _tile(n, cap=512):
    best = LANE
    for t in range(LANE, cap + 1, LANE):
        if n % t == 0:
            best = t
    return best


def mm_tn(a, b):
    s, k = a.shape
    n = b.shape[1]
    tk, tn = _tile(k), _tile(n)

    def body(a_ref, b_ref, o_ref, at_ref):
        @pl.when(pl.program_id(1) == 0)
        def _():
            at_ref[...] = a_ref[...].astype(BF16).T

        o_ref[...] = dot_nn(at_ref[...], b_ref[...])

    return pl.pallas_call(
        body, name="mm_tn", grid=(k // tk, n // tn),
        in_specs=[pl.BlockSpec((s, tk), lambda i, j: (0, i)), pl.BlockSpec((s, tn), lambda i, j: (0, j))],
        out_specs=pl.BlockSpec((tk, tn), lambda i, j: (i, j)), out_shape=jax.ShapeDtypeStruct((k, n), F32),
        scratch_shapes=[pltpu.VMEM((tk, s), BF16)],
        compiler_params=_params(2, VMEM_LIMIT),
    )(a, b)


def head_loss(x, fw, tgt):
    s, d = x.shape
    tm = min(ROW_TILE, s)

    def body(x_ref, fw_ref, t_ref, loss_ref, dx_ref, dfw_ref):
        xv = x_ref[...]
        err = _rms_fwd(xv, fw_ref[...]) - t_ref[...]
        part = jnp.zeros((1, LANE), F32) + 0.5 * jnp.sum(err * err) / d
        dx, dfw = _rms_bwd(xv, fw_ref[...], err / d)
        dx_ref[...] = dx
        first = pl.program_id(0) == 0
        _acc(loss_ref, first, part)
        _acc(dfw_ref, first, dfw)

    return pl.pallas_call(
        body, name="head_loss", grid=(s // tm,),
        in_specs=[_row_spec(tm, d), _full_spec((1, d)), _row_spec(tm, d)],
        out_specs=[_full_spec((1, LANE)), _row_spec(tm, d), _full_spec((1, d))],
        out_shape=[jax.ShapeDtypeStruct((1, LANE), F32), jax.ShapeDtypeStruct((s, d), F32),
                   jax.ShapeDtypeStruct((1, d), F32)],
        compiler_params=_params(1),
    )(x, fw, tgt)


def _conv_pre(ext, cw_ref, cb_ref):
    shifted = [pltpu.roll(ext, CONV_WIDTH - 1 - i, 0)[BLK:] if i < CONV_WIDTH - 1 else ext[BLK:]
               for i in range(CONV_WIDTH)]
    acc = cb_ref[...] + sum(cw_ref[i:i + 1, :] * shifted[i] for i in range(CONV_WIDTH))
    return acc, shifted


def conv_fwd(xbc, cw, cb):
    s, n = xbc.shape

    def body(cur_ref, prev_ref, cw_ref, cb_ref, o_ref):
        prev = jnp.where(pl.program_id(0) > 0, prev_ref[...], 0.0)
        acc, _ = _conv_pre(jnp.concatenate([prev, cur_ref[...]], axis=0), cw_ref, cb_ref)
        o_ref[...] = acc * _sigmoid(acc)

    return pl.pallas_call(
        body, name="conv_fwd", grid=(s // BLK,),
        in_specs=[pl.BlockSpec((BLK, n), lambda c: (c, 0)), pl.BlockSpec((BLK, n), lambda c: (jnp.maximum(c - 1, 0), 0)),
                  _full_spec(cw.shape), _full_spec((1, n))],
        out_specs=pl.BlockSpec((BLK, n), lambda c: (c, 0)), out_shape=jax.ShapeDtypeStruct((s, n), F32),
        compiler_params=_params(1),
    )(xbc, xbc, cw, cb)


def conv_bwd(xbc, cw, cb, du):
    s, n = xbc.shape
    nb = s // BLK

    def body(cur_ref, prev_ref, cw_ref, cb_ref, du_ref, dx_ref, dcw_ref, dcb_ref, nxt_ref):
        i = pl.program_id(0)
        c = nb - 1 - i
        prev = jnp.where(c > 0, prev_ref[...], 0.0)
        acc, shifted = _conv_pre(jnp.concatenate([prev, cur_ref[...]], axis=0), cw_ref, cb_ref)
        sg = _sigmoid(acc)
        dacc = du_ref[...] * (sg * (1.0 + acc * (1.0 - sg)))

        @pl.when(i == 0)
        def _():
            nxt_ref[...] = jnp.zeros_like(nxt_ref)
            dcw_ref[...] = jnp.zeros_like(dcw_ref)
            dcb_ref[...] = jnp.zeros_like(dcb_ref)

        dcb_ref[...] += jnp.sum(dacc, axis=0, keepdims=True)
        for t in range(CONV_WIDTH):
            dcw_ref[t:t + 1, :] += jnp.sum(dacc * shifted[t], axis=0, keepdims=True)
        ext = jnp.concatenate([dacc, nxt_ref[...]], axis=0)
        dx = cw_ref[CONV_WIDTH - 1:CONV_WIDTH, :] * dacc
        for t in range(CONV_WIDTH - 1):
            dx += cw_ref[t:t + 1, :] * pltpu.roll(ext, 2 * BLK - (CONV_WIDTH - 1 - t), 0)[:BLK]
        dx_ref[...] = dx
        nxt_ref[...] = dacc

    rev = lambda i: (nb - 1 - i, 0)
    return pl.pallas_call(
        body, name="conv_bwd", grid=(nb,),
        in_specs=[pl.BlockSpec((BLK, n), rev), pl.BlockSpec((BLK, n), lambda i: (jnp.maximum(nb - 2 - i, 0), 0)),
                  _full_spec(cw.shape), _full_spec((1, n)), pl.BlockSpec((BLK, n), rev)],
        out_specs=[pl.BlockSpec((BLK, n), rev), _full_spec((8, n)), _full_spec((1, n))],
        out_shape=[jax.ShapeDtypeStruct((s, n), F32), jax.ShapeDtypeStruct((8, n), F32),
                   jax.ShapeDtypeStruct((1, n), F32)],
        scratch_shapes=[pltpu.VMEM((BLK, n), F32)],
        compiler_params=_params(1),
    )(xbc, xbc, cw, cb, du)


N_PAIR = SSD_HEADS // 2
B_LO = SSD_WIDTH
C_LO = SSD_WIDTH + 2 * D_STATE


def _softplus(x):
    return jnp.maximum(x, 0.0) + jnp.log(1.0 + jnp.exp(-jnp.abs(x)))


def _ssd_chunk(u_ref, dt_ref, dtb_ref, alog_ref):
    shape = (BLK, BLK)
    tri = _iota(shape, 1) <= _iota(shape, 0)
    pre = dt_ref[...] + dtb_ref[...]
    dt = _softplus(pre)
    a = -jnp.exp(alog_ref[...])
    acum = dot_nn(tri.astype(F32), dt * a, exact=True)
    acum_t = acum.T
    last = _sub_row(acum, BLK - 1)
    heads = []
    for h in range(SSD_HEADS):
        col = _lane_col(acum, h)
        seg = jnp.where(tri, col - _sub_row(acum_t, h), -1e30)
        heads.append(dict(col=col, dm=jnp.exp(seg), dt=_lane_col(dt, h), last=_lane_col(last, h)))
    return tri, pre, dt, a, heads


def _pair_mix(lo_mask, v0, v1):
    return jnp.where(lo_mask, v0, v1)


def ssd_fwd(u, z, dtr, dtb, alog, dsk, nw):
    s = u.shape[0]
    nc = s // BLK

    def body(u_ref, z_ref, dt_ref, dtb_ref, alog_ref, dsk_ref, nw_ref, y_ref, st_ref, s_ref):
        @pl.when(pl.program_id(0) == 0)
        def _():
            s_ref[...] = jnp.zeros_like(s_ref)

        _, _, _, _, heads = _ssd_chunk(u_ref, dt_ref, dtb_ref, alog_ref)
        lo_lane = _iota((BLK, LANE), 1) < HEAD_DIM
        lo_sub = _iota((BLK, LANE), 0) < HEAD_DIM
        ys = []
        for p in range(N_PAIR):
            g = p // 2
            h0, h1 = heads[2 * p], heads[2 * p + 1]
            bg = u_ref[:, B_LO + g * D_STATE:B_LO + (g + 1) * D_STATE]
            cg = u_ref[:, C_LO + g * D_STATE:C_LO + (g + 1) * D_STATE]
            xs = u_ref[:, p * LANE:(p + 1) * LANE]
            xp = xs * _pair_mix(lo_lane, h0["dt"], h1["dt"])
            gm = dot_nt(cg, bg)
            yd = _pair_mix(lo_lane, dot_nn(gm * h0["dm"], xp), dot_nn(gm * h1["dm"], xp))
            sp = s_ref[p]
            st_ref[0, p] = sp
            yo = _pair_mix(lo_lane, jnp.exp(h0["col"]), jnp.exp(h1["col"])) * dot_nt(cg, sp)
            dskp = _pair_mix(lo_lane, _lane_col(dsk_ref[...], 2 * p), _lane_col(dsk_ref[...], 2 * p + 1))
            ys.append(yd + yo + xs * dskp)
            wp = _pair_mix(lo_lane, jnp.exp(h0["last"] - h0["col"]), jnp.exp(h1["last"] - h1["col"]))
            el = _pair_mix(lo_sub, jnp.exp(h0["last"]), jnp.exp(h1["last"]))
            s_ref[p] = el * sp + dot_tn(wp * xp, bg)
        y = jnp.concatenate(ys, axis=1)
        zv = z_ref[...]
        y_ref[...] = _rms_fwd(y * zv * _sigmoid(zv), nw_ref[...])

    vec = _full_spec((1, LANE))
    return pl.pallas_call(
        body, name="ssd_fwd", grid=(nc,),
        in_specs=[_row_spec(BLK, CONV_DIM), _row_spec(BLK, SSD_WIDTH), _row_spec(BLK, LANE), vec, vec, vec,
                  _full_spec((1, SSD_WIDTH))],
        out_specs=[_row_spec(BLK, SSD_WIDTH), pl.BlockSpec((1, N_PAIR, LANE, D_STATE), lambda c: (c, 0, 0, 0))],
        out_shape=[jax.ShapeDtypeStruct((s, SSD_WIDTH), F32), jax.ShapeDtypeStruct((nc, N_PAIR, LANE, D_STATE), F32)],
        scratch_shapes=[pltpu.VMEM((N_PAIR, LANE, D_STATE), F32)],
        compiler_params=_params(1),
    )(u, z, dtr, dtb, alog, dsk, nw)


def ssd_bwd(u, z, dtr, st, dyo, dtb, alog, dsk, nw):
    s = u.shape[0]
    nc = s // BLK

    def body(u_ref, z_ref, dt_ref, st_ref, dyo_ref, dtb_ref, alog_ref, dsk_ref, nw_ref,
             du_ref, dz_ref, ddt_ref, ddtb_ref, dalog_ref, ddsk_ref, dnw_ref, ds_ref):
        first = pl.program_id(0) == 0

        @pl.when(first)
        def _():
            ds_ref[...] = jnp.zeros_like(ds_ref)

        tri, pre, dt, a, heads = _ssd_chunk(u_ref, dt_ref, dtb_ref, alog_ref)
        shape = (BLK, LANE)
        lane = _iota(shape, 1)
        lo_lane = lane < HEAD_DIM
        lo_sub = _iota(shape, 0) < HEAD_DIM
        pairs = []
        ys = []
        for p in range(N_PAIR):
            g = p // 2
            h0, h1 = heads[2 * p], heads[2 * p + 1]
            bg = u_ref[:, B_LO + g * D_STATE:B_LO + (g + 1) * D_STATE]
            cg = u_ref[:, C_LO + g * D_STATE:C_LO + (g + 1) * D_STATE]
            xs = u_ref[:, p * LANE:(p + 1) * LANE]
            dtp = _pair_mix(lo_lane, h0["dt"], h1["dt"])
            xp = xs * dtp
            gm = dot_nt(cg, bg)
            m0, m1 = gm * h0["dm"], gm * h1["dm"]
            sp = st_ref[0, p]
            eap = _pair_mix(lo_lane, jnp.exp(h0["col"]), jnp.exp(h1["col"]))
            yo = eap * dot_nt(cg, sp)
            dskp = _pair_mix(lo_lane, _lane_col(dsk_ref[...], 2 * p), _lane_col(dsk_ref[...], 2 * p + 1))
            ys.append(_pair_mix(lo_lane, dot_nn(m0, xp), dot_nn(m1, xp)) + yo + xs * dskp)
            pairs.append(dict(bg=bg, cg=cg, xs=xs, dtp=dtp, xp=xp, gm=gm, m=(m0, m1), sp=sp, eap=eap, yo=yo, dskp=dskp))
        y = jnp.concatenate(ys, axis=1)
        zv = z_ref[...]
        sz = _sigmoid(zv)
        gate = zv * sz
        dyg, dnw = _rms_bwd(y * gate, nw_ref[...], dyo_ref[...])
        _acc(dnw_ref, first, dnw)
        dy = dyg * gate
        dz_ref[...] = dyg * y * (sz * (1.0 + zv * (1.0 - sz)))

        zeros = jnp.zeros(shape, F32)
        dacum_col = zeros
        dacum_row = zeros
        ddt = zeros
        ddsk = jnp.zeros((1, LANE), F32)
        dlast = jnp.zeros((1, LANE), F32)
        head_row = _iota((1, LANE), 1)
        sub = _iota(shape, 0)
        db = [zeros, zeros]
        dc = [zeros, zeros]
        for p in range(N_PAIR):
            g = p // 2
            q = pairs[p]
            dyp = dy[:, p * LANE:(p + 1) * LANE]
            dsn = ds_ref[p]
            t = dyp * q["xs"]
            dxs = dyp * q["dskp"]
            dcs = dyp * q["eap"]
            dc[g] = dc[g] + dot_nn(dcs, q["sp"])
            dsp = dot_tn(dcs, q["cg"])
            dea = dyp * q["yo"]
            elp = _pair_mix(lo_sub, jnp.exp(heads[2 * p]["last"]), jnp.exp(heads[2 * p + 1]["last"]))
            dsp = dsp + elp * dsn
            dels = dsn * q["sp"] * elp
            wp = _pair_mix(lo_lane, jnp.exp(heads[2 * p]["last"] - heads[2 * p]["col"]),
                           jnp.exp(heads[2 * p + 1]["last"] - heads[2 * p + 1]["col"]))
            dv = dot_nt(q["bg"], dsn)
            db[g] = db[g] + dot_nn(wp * q["xp"], dsn)
            dxp = dv * wp
            dwv = dv * q["xp"] * wp
            dgm = zeros
            for k in range(2):
                h = 2 * p + k
                mine = lo_lane if k == 0 else jnp.logical_not(lo_lane)
                mine_sub = lo_sub if k == 0 else jnp.logical_not(lo_sub)
                dyh = jnp.where(mine, dyp, 0.0)
                dm = dot_nt(dyh, q["xp"])
                dxp = dxp + dot_tn(q["m"][k], dyh)
                dgm = dgm + dm * heads[h]["dm"]
                e = dm * q["m"][k]
                onehot = lane == h
                dw_col = jnp.sum(jnp.where(mine, dwv, 0.0), axis=1, keepdims=True)
                col = (jnp.sum(e, axis=1, keepdims=True) + jnp.sum(jnp.where(mine, dea, 0.0), axis=1, keepdims=True)
                       - dw_col)
                dacum_col = dacum_col + jnp.where(onehot, col, 0.0)
                dacum_row = dacum_row - jnp.where(sub == h, jnp.sum(e, axis=0, keepdims=True), 0.0)
                dl = jnp.sum(dw_col) + jnp.sum(jnp.where(mine_sub, dels, 0.0))
                dlast = dlast + jnp.where(head_row == h, dl, 0.0)
                ddsk = ddsk + jnp.where(head_row == h, jnp.sum(jnp.where(mine, t, 0.0)), 0.0)
            dc[g] = dc[g] + dot_nn(dgm, q["bg"])
            db[g] = db[g] + dot_tn(dgm, q["cg"])
            dxs = dxs + dxp * q["dtp"]
            tt = dxp * q["xs"]
            for k in range(2):
                mine = lo_lane if k == 0 else jnp.logical_not(lo_lane)
                ddt = ddt + jnp.where(lane == 2 * p + k, jnp.sum(jnp.where(mine, tt, 0.0), axis=1, keepdims=True), 0.0)
            du_ref[:, p * LANE:(p + 1) * LANE] = dxs
            ds_ref[p] = dsp
        for g in range(2):
            du_ref[:, B_LO + g * D_STATE:B_LO + (g + 1) * D_STATE] = db[g]
            du_ref[:, C_LO + g * D_STATE:C_LO + (g + 1) * D_STATE] = dc[g]
        dacum = dacum_col + dacum_row.T + jnp.where(sub == BLK - 1, dlast, 0.0)
        dda = dot_tn(tri.astype(F32), dacum, exact=True)
        ddt = ddt + dda * a
        _acc(dalog_ref, first, jnp.sum(dda * dt, axis=0, keepdims=True) * a)
        dpre = ddt * _sigmoid(pre)
        ddt_ref[...] = dpre
        _acc(ddtb_ref, first, jnp.sum(dpre, axis=0, keepdims=True))
        _acc(ddsk_ref, first, ddsk)

    rev = lambda i: (nc - 1 - i, 0)
    vec = _full_spec((1, LANE))
    rows = lambda n: pl.BlockSpec((BLK, n), rev)
    return pl.pallas_call(
        body, name="ssd_bwd", grid=(nc,),
        in_specs=[rows(CONV_DIM), rows(SSD_WIDTH), rows(LANE),
                  pl.BlockSpec((1, N_PAIR, LANE, D_STATE), lambda i: (nc - 1 - i, 0, 0, 0)), rows(SSD_WIDTH),
                  vec, vec, vec, _full_spec((1, SSD_WIDTH))],
        out_specs=[rows(CONV_DIM), rows(SSD_WIDTH), rows(LANE), vec, vec, vec, _full_spec((1, SSD_WIDTH))],
        out_shape=[jax.ShapeDtypeStruct((s, CONV_DIM), F32), jax.ShapeDtypeStruct((s, SSD_WIDTH), F32),
                   jax.ShapeDtypeStruct((s, LANE), F32)] + [jax.ShapeDtypeStruct((1, LANE), F32)] * 3
        + [jax.ShapeDtypeStruct((1, SSD_WIDTH), F32)],
        scratch_shapes=[pltpu.VMEM((N_PAIR, LANE, D_STATE), F32)],
        compiler_params=_params(1),
    )(u, z, dtr, st, dyo, dtb, alog, dsk, nw)


SB_PAIRS = SB_WIDTH // LANE
SB_SCALE = HEAD_DIM ** -0.5


SB_TQ = 256


def _sb_tq(s):
    return min(SB_TQ, s)


def _sb_stack(x):
    lo_lane = _iota(x.shape, 1) < HEAD_DIM
    return jnp.concatenate([jnp.where(lo_lane, x, 0.0), jnp.where(lo_lane, 0.0, x)], axis=0)


def _sb_unstack(x2):
    tq = x2.shape[0] // 2
    lo_lane = _iota((tq, LANE), 1) < HEAD_DIM
    return jnp.where(lo_lane, x2[:tq], x2[tq:])


def _sb_logits(q2, kj, row0, col0):
    shape = (q2.shape[0], BLK)
    tq = shape[0] // 2
    z = dot_nt(q2, kj)
    row = _iota(shape, 0)
    valid = (col0 + _iota(shape, 1)) < (row0 + jnp.where(row < tq, row, row - tq))
    t = jnp.log(1.0 + jnp.exp(-jnp.abs(z)))
    ls = jnp.minimum(z, 0.0) - t
    lk = jnp.where(valid, jnp.minimum(-z, 0.0) - t, 0.0)
    return valid, ls, lk


def _sums(x, mask2, parts):
    acc = None
    rest = x
    for _ in range(parts):
        term = rest.astype(BF16)
        rest = rest - term.astype(F32)
        d = lax.dot_general(term, mask2, (((1,), (0,)), ((), ())), preferred_element_type=F32)
        acc = d if acc is None else acc + d
    return acc[:, :BLK], acc[:, BLK:]


def _mask2(cond):
    return jnp.concatenate([cond.astype(BF16), jnp.ones(cond.shape, BF16)], axis=1)


def _sb_specs(s):
    tq = _sb_tq(s)
    qspec = pl.BlockSpec((tq, LANE), lambda p, i: (i, p))
    kspec = pl.BlockSpec((s, LANE), lambda p, i: (0, SB_PAIRS + p))
    vspec = pl.BlockSpec((s, LANE), lambda p, i: (0, 2 * SB_PAIRS + p))
    return qspec, kspec, vspec


def sb_fwd(qkv):
    s = qkv.shape[0]
    tq = _sb_tq(s)
    kpq = tq // BLK

    def body(q_ref, k_ref, v_ref, o_ref, t_ref, acc_ref):
        qi = pl.program_id(1)
        q2 = _sb_stack(q_ref[...] * SB_SCALE).astype(BF16)
        later = _mask2(_iota((BLK, BLK), 0) > _iota((BLK, BLK), 1))
        acc_ref[...] = jnp.zeros_like(acc_ref)

        def step(jj, r):
            j = kpq * (qi + 1) - 1 - jj
            rows = pl.ds(pl.multiple_of(j * BLK, BLK), BLK)
            valid, ls, lk = _sb_logits(q2, k_ref[rows, :], qi * tq, j * BLK)
            after, total = _sums(lk, later, 3)
            w = jnp.where(valid, jnp.exp(ls + r + after), 0.0)
            acc_ref[...] += dot_nn(w, v_ref[rows, :])
            return r + total

        r = lax.fori_loop(0, kpq * (qi + 1), step, jnp.zeros((2 * tq, LANE), F32))
        o_ref[...] = _sb_unstack(acc_ref[...])
        t_ref[...] = jnp.concatenate([r[:tq], r[tq:]], axis=1)

    return pl.pallas_call(
        body, name="sb_fwd", grid=(SB_PAIRS, s // tq),
        in_specs=list(_sb_specs(s)),
        out_specs=[pl.BlockSpec((tq, LANE), lambda p, i: (i, p)), pl.BlockSpec((tq, 2 * LANE), lambda p, i: (i, p))],
        out_shape=[jax.ShapeDtypeStruct((s, SB_WIDTH), F32), jax.ShapeDtypeStruct((s, 2 * SB_WIDTH), F32)],
        scratch_shapes=[pltpu.VMEM((2 * tq, LANE), F32)],
        compiler_params=_params(2),
    )(qkv, qkv, qkv)


def sb_bwd(qkv, tot, do, do_col=0):
    s = qkv.shape[0]
    tq = _sb_tq(s)
    kpq = tq // BLK

    def body(q_ref, k_ref, v_ref, t_ref, do_ref, dq_ref, dk_ref, dv_ref, acc_ref):
        qi = pl.program_id(1)
        q2 = _sb_stack(q_ref[...] * SB_SCALE).astype(BF16)
        do2 = _sb_stack(do_ref[...]).astype(BF16)
        tot2 = jnp.concatenate([t_ref[:, :LANE], t_ref[:, LANE:]], axis=0)
        sq = (BLK, BLK)
        upto = _mask2(_iota(sq, 0) <= _iota(sq, 1))
        before = _mask2(_iota(sq, 0) < _iota(sq, 1))
        acc_ref[...] = jnp.zeros_like(acc_ref)

        @pl.when(qi == 0)
        def _():
            dk_ref[...] = jnp.zeros_like(dk_ref)
            dv_ref[...] = jnp.zeros_like(dv_ref)

        def step(j, carry):
            pc, fc = carry
            rows = pl.ds(pl.multiple_of(j * BLK, BLK), BLK)
            kj = k_ref[rows, :]
            vj = v_ref[rows, :]
            valid, ls, lk = _sb_logits(q2, kj, qi * tq, j * BLK)
            p_in, p_tot = _sums(lk, upto, 3)
            w = jnp.where(valid, jnp.exp(ls + (tot2 - (pc + p_in))), 0.0)
            e = w * dot_nt(do2, vj)
            f_in, f_tot = _sums(e, before, 2)
            sg = jnp.exp(ls)
            dz = jnp.where(valid, e * (1.0 - sg) - (fc + f_in) * sg, 0.0)
            acc_ref[...] += dot_nn(dz, kj)
            dk_ref[rows, :] += dot_tn(dz, q2)
            dv_ref[rows, :] += dot_tn(w, do2)
            return pc + p_tot, fc + f_tot

        zero = jnp.zeros((2 * tq, LANE), F32)
        lax.fori_loop(0, kpq * (qi + 1), step, (zero, zero))
        dq_ref[...] = SB_SCALE * _sb_unstack(acc_ref[...])

    qspec, kspec, vspec = _sb_specs(s)
    blk = pl.BlockSpec((tq, LANE), lambda p, i: (i, p))
    acc = pl.BlockSpec((s, LANE), lambda p, i: (0, p))
    return pl.pallas_call(
        body, name="sb_bwd", grid=(SB_PAIRS, s // tq),
        in_specs=[qspec, kspec, vspec, pl.BlockSpec((tq, 2 * LANE), lambda p, i: (i, p)),
                  pl.BlockSpec((tq, LANE), lambda p, i: (i, do_col + p))],
        out_specs=[blk, acc, acc],
        out_shape=[jax.ShapeDtypeStruct((s, SB_WIDTH), F32)] * 3,
        scratch_shapes=[pltpu.VMEM((2 * tq, LANE), F32)],
        compiler_params=_params(2),
    )(qkv, qkv, qkv, tot, do)


POOL_GROUP_DIM = POOL_WIDTH // len(POOL_WINDOWS)


def _pool_consts(c):
    t = _iota((BLK, 2 * BLK), 0)
    j = _iota((BLK, 2 * BLK), 1) - BLK
    bands = [jnp.logical_and(j <= t, j > t - win).astype(F32) for win in POOL_WINDOWS]
    group = _iota((BLK, POOL_WIDTH), 1) // POOL_GROUP_DIM
    pos = c * BLK + _iota((BLK, POOL_WIDTH), 0)
    win = jnp.zeros((BLK, POOL_WIDTH), jnp.int32)
    for gi, wn in enumerate(POOL_WINDOWS):
        win = jnp.where(group == gi, wn, win)
    inv = 1.0 / jnp.minimum(pos + 1, win).astype(F32)
    return bands, group, inv


def _pool_pooled(ext, cur, bands, group, inv):
    sums = jnp.zeros(cur.shape, F32)
    for gi, band in enumerate(bands):
        sums = jnp.where(group == gi, dot_nn(band, ext, exact=True), sums)
    return sums * inv - cur


def pool_fwd(p, wblk, pb, ps):
    s, n = p.shape

    def body(cur_ref, prev_ref, w_ref, pb_ref, ps_ref, o_ref):
        c = pl.program_id(0)
        cur = cur_ref[...]
        prev = jnp.where(c > 0, prev_ref[...], 0.0)
        bands, group, inv = _pool_consts(c)
        pooled = _pool_pooled(jnp.concatenate([prev, cur], axis=0), cur, bands, group, inv)
        o_ref[...] = (dot_nn(pooled, w_ref[...]) + pb_ref[...]) * ps_ref[...]

    return pl.pallas_call(
        body, name="pool_fwd", grid=(s // BLK,),
        in_specs=[pl.BlockSpec((BLK, n), lambda c: (c, 0)), pl.BlockSpec((BLK, n), lambda c: (jnp.maximum(c - 1, 0), 0)),
                  _full_spec((n, n)), _full_spec((1, n)), _full_spec((1, n))],
        out_specs=pl.BlockSpec((BLK, n), lambda c: (c, 0)), out_shape=jax.ShapeDtypeStruct((s, n), F32),
        compiler_params=_params(1),
    )(p, p, wblk, pb, ps)


def pool_bwd(p, wblk, pb, ps, dout, do_col=0):
    s, n = p.shape
    nb = s // BLK

    def body(cur_ref, prev_ref, w_ref, pb_ref, ps_ref, do_ref, dp_ref, dw_ref, dpb_ref, dps_ref, carry_ref):
        i = pl.program_id(0)
        c = nb - 1 - i
        first = i == 0
        cur = cur_ref[...]
        prev = jnp.where(c > 0, prev_ref[...], 0.0)
        bands, group, inv = _pool_consts(c)
        pooled = _pool_pooled(jnp.concatenate([prev, cur], axis=0), cur, bands, group, inv)
        mixed = dot_nn(pooled, w_ref[...]) + pb_ref[...]
        dov = do_ref[...]
        dmixed = dov * ps_ref[...]
        _acc(dps_ref, first, jnp.sum(dov * mixed, axis=0, keepdims=True))
        _acc(dpb_ref, first, jnp.sum(dmixed, axis=0, keepdims=True))
        _acc(dw_ref, first, dot_tn(pooled, dmixed))
        dpooled = dot_nt(dmixed, w_ref[...])
        dsums = dpooled * inv
        dext = jnp.zeros((2 * BLK, n), F32)
        for gi, band in enumerate(bands):
            dext = dext + dot_tn(band, jnp.where(group == gi, dsums, 0.0), exact=True)

        @pl.when(first)
        def _():
            carry_ref[...] = jnp.zeros_like(carry_ref)

        dp_ref[...] = dext[BLK:] - dpooled + carry_ref[...]
        carry_ref[...] = dext[:BLK]

    rev = lambda i: (nb - 1 - i, 0)
    return pl.pallas_call(
        body, name="pool_bwd", grid=(nb,),
        in_specs=[pl.BlockSpec((BLK, n), rev), pl.BlockSpec((BLK, n), lambda i: (jnp.maximum(nb - 2 - i, 0), 0)),
                  _full_spec((n, n)), _full_spec((1, n)), _full_spec((1, n)),
                  pl.BlockSpec((BLK, n), lambda i: (nb - 1 - i, do_col))],
        out_specs=[pl.BlockSpec((BLK, n), rev), _full_spec((n, n)), _full_spec((1, n)), _full_spec((1, n))],
        out_shape=[jax.ShapeDtypeStruct((s, n), F32), jax.ShapeDtypeStruct((n, n), F32),
                   jax.ShapeDtypeStruct((1, n), F32), jax.ShapeDtypeStruct((1, n), F32)],
        scratch_shapes=[pltpu.VMEM((BLK, n), F32)],
        compiler_params=_params(1),
    )(p, p, wblk, pb, ps, dout)


def _row_tile(rows):
    for t in (256, 128, 64, 32, 16, 8):
        if rows % t == 0:
            return t
    return rows


def adamw(w, g, m, v):
    n, rows, cols = w.shape
    tr = _row_tile(rows)

    def body(w_ref, g_ref, m_ref, v_ref, d_ref, nm_ref, nv_ref):
        gv = g_ref[...]
        nm = ADAM_B1 * m_ref[...] + (1.0 - ADAM_B1) * gv
        nv = ADAM_B2 * v_ref[...] + (1.0 - ADAM_B2) * (gv * gv)
        m_hat = nm / (1.0 - ADAM_B1 ** ADAM_STEP)
        v_hat = nv / (1.0 - ADAM_B2 ** ADAM_STEP)
        d_ref[...] = -ADAM_LR * (m_hat / (jnp.sqrt(v_hat) + ADAM_EPS) + ADAM_WD * w_ref[...])
        nm_ref[...] = nm
        nv_ref[...] = nv

    spec = pl.BlockSpec((1, tr, cols), lambda i, j: (i, j, 0))
    return pl.pallas_call(
        body, name="adamw", grid=(n, rows // tr), in_specs=[spec] * 4, out_specs=[spec] * 3,
        out_shape=[jax.ShapeDtypeStruct(w.shape, F32)] * 3, compiler_params=_params(2),
    )(w, g, m, v)


def slab_sum(srcs, n_out, out_dtype):
    _, rows, cols = srcs[0][0].shape
    tr = _row_tile(rows)
    sel = jnp.stack([jnp.asarray(base, jnp.int32) for _, base, _ in srcs])

    def body(sel_ref, *refs):
        acc = refs[0][...].astype(F32)
        for r in refs[1:-1]:
            acc = acc + r[...].astype(F32)
        refs[-1][...] = acc.astype(out_dtype)

    def in_spec(k, step):
        return pl.BlockSpec((None, tr, cols), lambda o, i, sel_ref: (sel_ref[k] + step * o, i, 0))

    return pl.pallas_call(
        body, name="slab_sum",
        grid_spec=pltpu.PrefetchScalarGridSpec(
            num_scalar_prefetch=1, grid=(n_out, rows // tr),
            in_specs=[in_spec(k, step) for k, (_, _, step) in enumerate(srcs)],
            out_specs=pl.BlockSpec((None, tr, cols), lambda o, i, sel_ref: (o, i, 0))),
        out_shape=jax.ShapeDtypeStruct((n_out, rows, cols), out_dtype), compiler_params=_params(2),
    )(sel, *[a for a, _, _ in srcs])


ICI_FLIPS = ((1, 0, 0), (0, 1, 0), (1, 1, 0))
D2D_FLIPS = ((0, 0, 1),)
ANY = pl.BlockSpec(memory_space=pl.ANY)


def _me():
    return lax.axis_index("x"), lax.axis_index("y"), lax.axis_index("c")


def _flipped(me, flip):
    return tuple(1 - m if f else m for m, f in zip(me, flip))


def _chip(dev):
    return 2 * dev[0] + dev[1]


def exchange(x, n_out, flips, src_slot, dst_slot, own, name):
    n = len(flips)

    def body(x_ref, o_ref, send_sems, recv_sems, own_sem):
        me = _me()
        if own is not None:
            mine = pltpu.make_async_copy(x_ref.at[own[0](me)], o_ref.at[own[1](me)], own_sem)
            mine.start()

        def copy(i, sender):
            return pltpu.make_async_remote_copy(
                src_ref=x_ref.at[src_slot(i, me)], dst_ref=o_ref.at[dst_slot(i, sender)],
                send_sem=send_sems.at[i], recv_sem=recv_sems.at[i],
                device_id=_flipped(me, flips[i]), device_id_type=MESH)

        sends = [copy(i, me) for i in range(n)]
        for cp in sends:
            cp.start()
        for i in range(n):
            copy(i, _flipped(me, flips[i])).wait_recv()
        for cp in sends:
            cp.wait_send()
        if own is not None:
            mine.wait()

    return pl.pallas_call(
        body, name=name, in_specs=[ANY], out_specs=ANY,
        out_shape=jax.ShapeDtypeStruct((n_out,) + x.shape[1:], x.dtype),
        scratch_shapes=[pltpu.SemaphoreType.DMA((n,)), pltpu.SemaphoreType.DMA((n,)), pltpu.SemaphoreType.DMA(())],
    )(x)


def all_gather(x):
    me = _me()
    chips = exchange(x[None], 4, ICI_FLIPS, lambda i, me: 0, lambda i, sender: _chip(sender), None, "gather_ici")
    chips = lax.dynamic_update_index_in_dim(chips, x, _chip(me), 0)
    both = exchange(chips[None], 2, D2D_FLIPS, lambda i, me: 0, lambda i, sender: sender[2], None, "gather_d2d")
    return lax.dynamic_update_index_in_dim(both, chips, me[2], 0)


def reduce_scatter(parts):
    x, y, c = _me()
    sib = exchange(parts, 1, D2D_FLIPS, lambda i, me: 1 - me[2], lambda i, sender: 0, None, "scatter_d2d")
    n_chip = parts.shape[1]
    rows, cols = parts.shape[2:]
    mine = parts.reshape(2 * n_chip, rows, cols)
    sib = sib.reshape(n_chip, rows, cols)
    chip_sum = slab_sum([(mine, n_chip * c, 1), (sib, 0, 1)], n_chip, BF16)
    got = exchange(chip_sum, len(ICI_FLIPS), ICI_FLIPS, lambda i, me: _chip(_flipped(me, ICI_FLIPS[i])),
                   lambda i, sender: i, None, "scatter_ici")
    return slab_sum([(chip_sum, 2 * x + y, 0)] + [(got, i, 0) for i in range(len(ICI_FLIPS))], 1, F32)[0]


def all_reduce_small(v):
    flips = D2D_FLIPS + ICI_FLIPS[:2]

    def body(v_ref, o_ref, got_ref, send_sems, recv_sems):
        me = _me()
        o_ref[...] = v_ref[...]
        for i, flip in enumerate(flips):
            cp = pltpu.make_async_remote_copy(
                src_ref=o_ref, dst_ref=got_ref.at[i], send_sem=send_sems.at[i], recv_sem=recv_sems.at[i],
                device_id=_flipped(me, flip), device_id_type=MESH)
            cp.start()
            cp.wait()
            o_ref[...] = o_ref[...] + got_ref[i]

    vm = pl.BlockSpec(memory_space=pltpu.VMEM)
    return pl.pallas_call(
        body, name="all_reduce_small", in_specs=[vm], out_specs=vm, out_shape=jax.ShapeDtypeStruct(v.shape, F32),
        scratch_shapes=[pltpu.VMEM((len(flips),) + v.shape, F32), pltpu.SemaphoreType.DMA((len(flips),)),
                        pltpu.SemaphoreType.DMA((len(flips),))],
    )(v)


def _cols_from_gather(g):
    l, r, c = g.shape[2:]
    return g.transpose(2, 3, 1, 0, 4).reshape(l, r, N_DEV * c)


def _rows_from_gather(g):
    l, r, c = g.shape[2:]
    return g.transpose(2, 1, 0, 3, 4).reshape(l, N_DEV * r, c)


def _cols_to_parts(dw):
    l, r, c8 = dw.shape
    c = c8 // N_DEV
    return dw.reshape(l, r, 4, 2, c).transpose(3, 2, 0, 1, 4).reshape(2, 4, l * r, c)


def _rows_to_parts(dw):
    l, r8, c = dw.shape
    r = r8 // N_DEV
    return dw.reshape(l, 4, 2, r, c).transpose(2, 1, 0, 3, 4).reshape(2, 4, l * r, c)


def _perm_in(w):
    pad = jnp.zeros(w.shape[:-1] + (D_IN_PAD - D_IN_PROJ,), w.dtype)
    return jnp.concatenate([w[..., :DT_LO], w[..., DT_HI:], w[..., DT_LO:DT_HI], pad], axis=-1)


def _unperm_in(dw):
    n = D_IN_PROJ - (DT_HI - DT_LO)
    return jnp.concatenate([dw[..., :DT_LO], dw[..., n:D_IN_PROJ], dw[..., DT_LO:n]], axis=-1)


def _pad_lanes(v):
    return jnp.pad(v, (0, LANE - v.shape[0]))[None]


def _block_diag(w):
    g, n, _ = w.shape
    out = jnp.zeros((g * n, g * n), w.dtype)
    for i in range(g):
        out = out.at[i * n:(i + 1) * n, i * n:(i + 1) * n].set(w[i])
    return out


def _pack(arrs):
    flat = []
    for a in arrs:
        a = a.reshape(-1)
        flat.append(jnp.pad(a, (0, -a.shape[0] % LANE)))
    return jnp.concatenate(flat).reshape(-1, LANE)


def _unpack(buf, shapes):
    out = []
    lo = 0
    buf = buf.reshape(-1)
    for shp in shapes:
        n = 1
        for k in shp:
            n *= k
        out.append(buf[lo:lo + n].reshape(shp))
        lo += n + (-n % LANE)
    return out


def _layer_params(full, l):
    return dict(
        n1w=full["norm1_w"][l][None], w_in=full["w_in"][l], cw=jnp.pad(full["conv_w"][l], ((0, 8 - CONV_WIDTH), (0, 0))),
        cb=full["conv_b"][l][None], dtb=_pad_lanes(full["dt_bias"][l]), alog=_pad_lanes(full["a_log"][l]),
        dsk=_pad_lanes(full["d_skip"][l]), snw=full["ssd_norm_w"][l][None], wblk=_block_diag(full["pool_w"][l]),
        pb=full["pool_b"][l].reshape(1, POOL_WIDTH), ps=full["pool_scale"][l][None], w_out=full["w_out"][l],
        n2w=full["norm2_w"][l][None], wg=full["w_gate"][l], wu=full["w_up"][l], wd=full["w_down"][l])


def _layer_fwd(x, p):
    z, xbc, qkv, pp, dtr, h1 = inproj_fwd(x, p["n1w"], p["w_in"])
    u = conv_fwd(xbc, p["cw"], p["cb"])
    y_ssd, st = ssd_fwd(u, z, dtr, p["dtb"], p["alog"], p["dsk"], p["snw"])
    o, tot = sb_fwd(qkv)
    yp = pool_fwd(pp, p["wblk"], p["pb"], p["ps"])
    ycat = jnp.concatenate([y_ssd, o, yp], axis=1)
    x_mid = outproj_fwd(ycat, p["w_out"], x)
    x_out, g, uu = ffn_fwd(x_mid, p["n2w"], p["wg"], p["wu"], p["wd"])
    return x_out, dict(x=x, z=z, xbc=xbc, qkv=qkv, pp=pp, dtr=dtr, h1=h1, u=u, st=st, tot=tot, ycat=ycat, x_mid=x_mid,
                       g=g, uu=uu)


def _layer_bwd(dxo, sv, p):
    dx_mid, dn2w, a, dg, du, h2 = ffn_bwd(dxo, sv["x_mid"], sv["g"], sv["uu"], p["n2w"], p["wg"], p["wu"], p["wd"])
    gr = dict(norm2_w=dn2w[0], w_down=mm_tn(a, dxo), w_gate=mm_tn(h2, dg), w_up=mm_tn(h2, du))
    dycat = outproj_bwd(dx_mid, p["w_out"])
    gr["w_out"] = mm_tn(sv["ycat"], dx_mid)
    dp, dwblk, dpb, dps = pool_bwd(sv["pp"], p["wblk"], p["pb"], p["ps"], dycat,
                                   (SSD_WIDTH + SB_WIDTH) // POOL_WIDTH)
    n = POOL_GROUP_DIM
    gr["pool_w"] = jnp.stack([dwblk[i * n:(i + 1) * n, i * n:(i + 1) * n] for i in range(len(POOL_WINDOWS))])
    gr["pool_b"] = dpb.reshape(len(POOL_WINDOWS), n)
    gr["pool_scale"] = dps[0]
    dq, dk, dv = sb_bwd(sv["qkv"], sv["tot"], dycat, SSD_WIDTH // LANE)
    du_, dz, ddtr, ddtb, dalog, ddsk, dsnw = ssd_bwd(sv["u"], sv["z"], sv["dtr"], sv["st"], dycat, p["dtb"], p["alog"],
                                                     p["dsk"], p["snw"])
    gr.update(dt_bias=ddtb[0, :SSD_HEADS], a_log=dalog[0, :SSD_HEADS], d_skip=ddsk[0, :SSD_HEADS], ssd_norm_w=dsnw[0])
    dxbc, dcw, dcb = conv_bwd(sv["xbc"], p["cw"], p["cb"], du_)
    gr.update(conv_w=dcw[:CONV_WIDTH], conv_b=dcb[0])
    dproj = jnp.concatenate([dz, dxbc, dq, dk, dv, dp, ddtr], axis=1)
    dx, dn1w = inproj_bwd(dproj, p["w_in"], sv["x"], p["n1w"], dx_mid)
    gr.update(norm1_w=dn1w[0], w_in=mm_tn(sv["h1"], dproj))
    return dx, gr


def local_step(x, tgt, full):
    depth = full["w_in"].shape[0]
    params = [_layer_params(full, l) for l in range(depth)]
    saved = []
    for p in params:
        x, sv = _layer_fwd(x, p)
        saved.append(sv)
    loss, dx, dfw = head_loss(x, full["final_norm_w"][None], tgt)
    grads = []
    for p, sv in zip(reversed(params), reversed(saved)):
        dx, gr = _layer_bwd(dx, sv, p)
        grads.append(gr)
    grads.reverse()
    out = {k: jnp.stack([gr[k] for gr in grads]) for k in grads[0]}
    out["final_norm_w"] = dfw[0]
    return loss, dx, out


WEIGHTS = ("norm1_w", "w_in", "conv_w", "conv_b", "dt_bias", "a_log", "d_skip", "ssd_norm_w", "pool_w", "pool_b",
           "pool_scale", "w_out", "norm2_w", "w_gate", "w_up", "w_down", "final_norm_w")
COL_SHARDED = ("w_in", "w_gate", "w_up")
ROW_SHARDED = ("w_out", "w_down")
SMALL = tuple(k for k in WEIGHTS if k not in COL_SHARDED + ROW_SHARDED)


def kernel(x, norm1_w, w_in, conv_w, conv_b, dt_bias, a_log, d_skip, ssd_norm_w, pool_w, pool_b, pool_scale, w_out, norm2_w, w_gate, w_up, w_down, final_norm_w, loss_target, m_norm1_w, m_w_in, m_conv_w, m_conv_b, m_dt_bias, m_a_log, m_d_skip, m_ssd_norm_w, m_pool_w, m_pool_b, m_pool_scale, m_w_out, m_norm2_w, m_w_gate, m_w_up, m_w_down, m_final_norm_w, v_norm1_w, v_w_in, v_conv_w, v_conv_b, v_dt_bias, v_a_log, v_d_skip, v_ssd_norm_w, v_pool_w, v_pool_b, v_pool_scale, v_w_out, v_norm2_w, v_w_gate, v_w_up, v_w_down, v_final_norm_w):
    args = dict(locals())
    w = {k: args[k] for k in WEIGHTS}
    m = {k: args["m_" + k] for k in WEIGHTS}
    v = {k: args["v_" + k] for k in WEIGHTS}
    mx, my, mc = _me()
    dev = 4 * mx + 2 * my + mc

    full = dict(w)
    for k in COL_SHARDED:
        full[k] = _cols_from_gather(all_gather(w[k].astype(BF16)))
    for k in ROW_SHARDED:
        full[k] = _rows_from_gather(all_gather(w[k].astype(BF16)))
    full["w_in"] = _perm_in(full["w_in"])
    full["conv_w"] = _cols_from_gather(all_gather(conv_w))

    loss, dx, g = local_step(x[0], loss_target[0], full)
    g["w_in"] = _unperm_in(g["w_in"])

    grads = {}
    for k in COL_SHARDED:
        grads[k] = reduce_scatter(_cols_to_parts(g[k])).reshape(w[k].shape)
    for k in ROW_SHARDED:
        grads[k] = reduce_scatter(_rows_to_parts(g[k])).reshape(w[k].shape)
    small_shapes = [(1, LANE)] + [g[k].shape for k in SMALL]
    summed = _unpack(all_reduce_small(_pack([loss] + [g[k] for k in SMALL])), small_shapes)
    loss = summed[0][0, 0]
    grads.update(zip(SMALL, summed[1:]))
    n_cw = conv_w.shape[-1]
    grads["conv_w"] = lax.dynamic_slice_in_dim(grads["conv_w"], dev * n_cw, n_cw, axis=2)

    delta, new_m, new_v = {}, {}, {}
    for k in COL_SHARDED + ROW_SHARDED:
        delta[k], new_m[k], new_v[k] = adamw(w[k], grads[k], m[k], v[k])
    shapes = [w[k].shape for k in SMALL]
    packed = [_pack([t[k] for k in SMALL])[None] for t in (w, grads, m, v)]
    for dst, buf in zip((delta, new_m, new_v), adamw(*packed)):
        dst.update(zip(SMALL, _unpack(buf, shapes)))
    return (loss, dx[None], *[grads[k] for k in WEIGHTS], *[delta[k] for k in WEIGHTS],
            *[new_m[k] for k in WEIGHTS], *[new_v[k] for k in WEIGHTS])
```

```python
import functools

import jax
import jax.numpy as jnp
from jax import lax
from jax.experimental import pallas as pl
from jax.experimental.pallas import tpu as pltpu

F32 = jnp.float32
BF16 = jnp.bfloat16
HIGHEST = lax.Precision.HIGHEST
MESH = pl.DeviceIdType.MESH

EPS = 1e-6
D_MODEL = 1024
SSD_WIDTH = 512
SSD_HEADS = 8
HEAD_DIM = 64
D_STATE = 128
CONV_WIDTH = 4
CONV_DIM = 1024
SB_WIDTH = 256
POOL_WIDTH = 256
POOL_WINDOWS = (2, 4, 8, 16)
D_IN_PROJ = 2568
D_FF = 2816
N_DEV = 8
DEPTH = 4
SEG = (512, 1024, 768, 256, 128)
D_IN_PAD = sum(SEG)
DT_LO, DT_HI = 1536, 1544

LANE = 128
BLK = 128
ROW_TILE = 256
VMEM_LIMIT = 56 * 2**20

ADAM_LR, ADAM_B1, ADAM_B2, ADAM_EPS, ADAM_WD, ADAM_STEP = 0.001, 0.9, 0.999, 1e-08, 0.01, 10


def _params(n_axes=1, vmem=None):
    return pltpu.CompilerParams(dimension_semantics=("arbitrary",) * n_axes, vmem_limit_bytes=vmem)


def _dot(a, b, dims, exact=False):
    if exact:
        return lax.dot_general(a.astype(F32), b.astype(F32), (dims, ((), ())), precision=HIGHEST,
                               preferred_element_type=F32)
    return lax.dot_general(a.astype(BF16), b.astype(BF16), (dims, ((), ())), preferred_element_type=F32)


def dot_nn(a, b, exact=False):
    return _dot(a, b, ((1,), (0,)), exact)


def dot_nt(a, b, exact=False):
    return _dot(a, b, ((1,), (1,)), exact)


def dot_tn(a, b, exact=False):
    return _dot(a, b, ((0,), (0,)), exact)


def _iota(shape, axis):
    return lax.broadcasted_iota(jnp.int32, shape, axis)


def _lane_col(x, h):
    return jnp.sum(jnp.where(_iota(x.shape, 1) == h, x, 0.0), axis=1, keepdims=True)


def _sub_row(x, h):
    return jnp.sum(jnp.where(_iota(x.shape, 0) == h, x, 0.0), axis=0, keepdims=True)


def _sigmoid(x):
    return 1.0 / (1.0 + jnp.exp(-x))


def _rms_fwd(x, w):
    r = lax.rsqrt(jnp.mean(x * x, axis=-1, keepdims=True) + EPS)
    return x * r * w


def _rms_bwd(x, w, dy):
    r = lax.rsqrt(jnp.mean(x * x, axis=-1, keepdims=True) + EPS)
    xh = x * r
    dxh = dy * w
    dx = r * (dxh - xh * jnp.mean(dxh * xh, axis=-1, keepdims=True))
    return dx, jnp.sum(dy * xh, axis=0, keepdims=True)


def _acc(ref, first, val):
    @pl.when(first)
    def _():
        ref[...] = val

    @pl.when(jnp.logical_not(first))
    def _():
        ref[...] += val


def _row_spec(tm, n):
    return pl.BlockSpec((tm, n), lambda i: (i, 0))


def _full_spec(shape):
    return pl.BlockSpec(shape, lambda *_: (0,) * len(shape))


def inproj_fwd(x, nw, w):
    s, d = x.shape
    tm = min(ROW_TILE, s)

    def body(x_ref, nw_ref, w_ref, z_ref, xbc_ref, qkv_ref, p_ref, dt_ref, h_ref):
        h = _rms_fwd(x_ref[...], nw_ref[...]).astype(BF16)
        h_ref[...] = h
        lo = 0
        for ref, n in zip((z_ref, xbc_ref, qkv_ref, p_ref, dt_ref), SEG):
            ref[...] = dot_nt(h, w_ref[lo:lo + n, :])
            lo += n

    return pl.pallas_call(
        body, name="inproj_fwd", grid=(s // tm,),
        in_specs=[_row_spec(tm, d), _full_spec((1, d)), _full_spec(w.shape)],
        out_specs=[_row_spec(tm, n) for n in SEG] + [_row_spec(tm, d)],
        out_shape=[jax.ShapeDtypeStruct((s, n), F32) for n in SEG] + [jax.ShapeDtypeStruct((s, d), BF16)],
        compiler_params=_params(1, VMEM_LIMIT),
    )(x, nw, w)


def inproj_bwd(dproj, w, x, nw, dres):
    s, d = x.shape
    tm = min(ROW_TILE, s)

    def body(dp_ref, w_ref, x_ref, nw_ref, dres_ref, dx_ref, dnw_ref):
        dh = dot_nn(dp_ref[...], w_ref[...])
        dx, dnw = _rms_bwd(x_ref[...], nw_ref[...], dh)
        dx_ref[...] = dres_ref[...] + dx
        _acc(dnw_ref, pl.program_id(0) == 0, dnw)

    return pl.pallas_call(
        body, name="inproj_bwd", grid=(s // tm,),
        in_specs=[_row_spec(tm, dproj.shape[1]), _full_spec(w.shape), _row_spec(tm, d), _full_spec((1, d)),
                  _row_spec(tm, d)],
        out_specs=[_row_spec(tm, d), _full_spec((1, d))],
        out_shape=[jax.ShapeDtypeStruct((s, d), F32), jax.ShapeDtypeStruct((1, d), F32)],
        compiler_params=_params(1, VMEM_LIMIT),
    )(dproj, w, x, nw, dres)


def outproj_fwd(y, w, res):
    s, d = res.shape
    tm = min(ROW_TILE, s)

    def body(y_ref, w_ref, r_ref, o_ref):
        o_ref[...] = r_ref[...] + dot_nn(y_ref[...], w_ref[...])

    return pl.pallas_call(
        body, name="outproj_fwd", grid=(s // tm,),
        in_specs=[_row_spec(tm, y.shape[1]), _full_spec(w.shape), _row_spec(tm, d)],
        out_specs=_row_spec(tm, d), out_shape=jax.ShapeDtypeStruct((s, d), F32),
        compiler_params=_params(1, VMEM_LIMIT),
    )(y, w, res)


def outproj_bwd(dx, w):
    s, d = dx.shape
    tm = min(ROW_TILE, s)

    def body(dx_ref, w_ref, o_ref):
        o_ref[...] = dot_nt(dx_ref[...], w_ref[...])

    return pl.pallas_call(
        body, name="outproj_bwd", grid=(s // tm,),
        in_specs=[_row_spec(tm, d), _full_spec(w.shape)],
        out_specs=_row_spec(tm, w.shape[0]), out_shape=jax.ShapeDtypeStruct((s, w.shape[0]), F32),
        compiler_params=_params(1, VMEM_LIMIT),
    )(dx, w)


def ffn_fwd(x, nw, wg, wu, wd):
    s, d = x.shape
    f = wg.shape[0]
    tm = min(ROW_TILE, s)

    def body(x_ref, nw_ref, wg_ref, wu_ref, wd_ref, o_ref, g_ref, u_ref):
        xv = x_ref[...]
        h = _rms_fwd(xv, nw_ref[...]).astype(BF16)
        g = dot_nt(h, wg_ref[...])
        u = dot_nt(h, wu_ref[...])
        g_ref[...] = g.astype(BF16)
        u_ref[...] = u.astype(BF16)
        o_ref[...] = xv + dot_nn(g * _sigmoid(g) * u, wd_ref[...])

    return pl.pallas_call(
        body, name="ffn_fwd", grid=(s // tm,),
        in_specs=[_row_spec(tm, d), _full_spec((1, d)), _full_spec(wg.shape), _full_spec(wu.shape),
                  _full_spec(wd.shape)],
        out_specs=[_row_spec(tm, d), _row_spec(tm, f), _row_spec(tm, f)],
        out_shape=[jax.ShapeDtypeStruct((s, d), F32), jax.ShapeDtypeStruct((s, f), BF16),
                   jax.ShapeDtypeStruct((s, f), BF16)],
        compiler_params=_params(1, VMEM_LIMIT),
    )(x, nw, wg, wu, wd)


def ffn_bwd(dxo, x, g, u, nw, wg, wu, wd):
    s, d = x.shape
    f = wg.shape[0]
    tm = min(ROW_TILE, s)

    def body(dxo_ref, x_ref, g_ref, u_ref, nw_ref, wg_ref, wu_ref, wd_ref, dx_ref, dnw_ref, a_ref, dg_ref,
             du_ref, h_ref):
        dxo_v = dxo_ref[...]
        xv = x_ref[...]
        da = dot_nt(dxo_v, wd_ref[...])
        gv = g_ref[...].astype(F32)
        uv = u_ref[...].astype(F32)
        sg = _sigmoid(gv)
        sl = gv * sg
        a_ref[...] = (sl * uv).astype(BF16)
        dg = (da * uv * (sg * (1.0 + gv * (1.0 - sg)))).astype(BF16)
        du = (da * sl).astype(BF16)
        dg_ref[...] = dg
        du_ref[...] = du
        dh = dot_nn(dg, wg_ref[...]) + dot_nn(du, wu_ref[...])
        h_ref[...] = _rms_fwd(xv, nw_ref[...]).astype(BF16)
        dx, dnw = _rms_bwd(xv, nw_ref[...], dh)
        dx_ref[...] = dxo_v + dx
        _acc(dnw_ref, pl.program_id(0) == 0, dnw)

    return pl.pallas_call(
        body, name="ffn_bwd", grid=(s // tm,),
        in_specs=[_row_spec(tm, d), _row_spec(tm, d), _row_spec(tm, f), _row_spec(tm, f), _full_spec((1, d)),
                  _full_spec(wg.shape), _full_spec(wu.shape), _full_spec(wd.shape)],
        out_specs=[_row_spec(tm, d), _full_spec((1, d)), _row_spec(tm, f), _row_spec(tm, f), _row_spec(tm, f),
                   _row_spec(tm, d)],
        out_shape=[jax.ShapeDtypeStruct((s, d), F32), jax.ShapeDtypeStruct((1, d), F32),
                   jax.ShapeDtypeStruct((s, f), BF16), jax.ShapeDtypeStruct((s, f), BF16),
                   jax.ShapeDtypeStruct((s, f), BF16), jax.ShapeDtypeStruct((s, d), BF16)],
        compiler_params=_params(1, VMEM_LIMIT),
    )(dxo, x, g, u, nw, wg, wu, wd)


def _tile(n, cap=256):
    best = LANE
    for t in range(LANE, cap + 1, LANE):
        if n % t == 0:
            best = t
    return best


def mm_tn(a, b):
    s, k = a.shape
    n = b.shape[1]
    tk = _tile(k)

    def body(a_ref, b_ref, o_ref):
        o_ref[...] = dot_nn(a_ref[...].astype(BF16).T, b_ref[...])

    return pl.pallas_call(
        body, name="mm_tn", grid=(k // tk,),
        in_specs=[pl.BlockSpec((s, tk), lambda i: (0, i)), _full_spec((s, n))],
        out_specs=pl.BlockSpec((tk, n), lambda i: (i, 0)), out_shape=jax.ShapeDtypeStruct((k, n), F32),
        compiler_params=_params(1, VMEM_LIMIT),
    )(a, b)


def head_loss(x, fw, tgt):
    s, d = x.shape
    tm = min(ROW_TILE, s)

    def body(x_ref, fw_ref, t_ref, loss_ref, dx_ref, dfw_ref):
        xv = x_ref[...]
        err = _rms_fwd(xv, fw_ref[...]) - t_ref[...]
        part = jnp.zeros((1, LANE), F32) + 0.5 * jnp.sum(err * err) / d
        dx, dfw = _rms_bwd(xv, fw_ref[...], err / d)
        dx_ref[...] = dx
        first = pl.program_id(0) == 0
        _acc(loss_ref, first, part)
        _acc(dfw_ref, first, dfw)

    return pl.pallas_call(
        body, name="head_loss", grid=(s // tm,),
        in_specs=[_row_spec(tm, d), _full_spec((1, d)), _row_spec(tm, d)],
        out_specs=[_full_spec((1, LANE)), _row_spec(tm, d), _full_spec((1, d))],
        out_shape=[jax.ShapeDtypeStruct((1, LANE), F32), jax.ShapeDtypeStruct((s, d), F32),
                   jax.ShapeDtypeStruct((1, d), F32)],
        compiler_params=_params(1),
    )(x, fw, tgt)


def _conv_pre(ext, cw_ref, cb_ref):
    shifted = [pltpu.roll(ext, CONV_WIDTH - 1 - i, 0)[BLK:] if i < CONV_WIDTH - 1 else ext[BLK:]
               for i in range(CONV_WIDTH)]
    acc = cb_ref[...] + sum(cw_ref[i:i + 1, :] * shifted[i] for i in range(CONV_WIDTH))
    return acc, shifted


def conv_fwd(xbc, cw, cb):
    s, n = xbc.shape

    def body(cur_ref, prev_ref, cw_ref, cb_ref, o_ref):
        prev = jnp.where(pl.program_id(0) > 0, prev_ref[...], 0.0)
        acc, _ = _conv_pre(jnp.concatenate([prev, cur_ref[...]], axis=0), cw_ref, cb_ref)
        o_ref[...] = acc * _sigmoid(acc)

    return pl.pallas_call(
        body, name="conv_fwd", grid=(s // BLK,),
        in_specs=[pl.BlockSpec((BLK, n), lambda c: (c, 0)), pl.BlockSpec((BLK, n), lambda c: (jnp.maximum(c - 1, 0), 0)),
                  _full_spec(cw.shape), _full_spec((1, n))],
        out_specs=pl.BlockSpec((BLK, n), lambda c: (c, 0)), out_shape=jax.ShapeDtypeStruct((s, n), F32),
        compiler_params=_params(1),
    )(xbc, xbc, cw, cb)


def conv_bwd(xbc, cw, cb, du):
    s, n = xbc.shape
    nb = s // BLK

    def body(cur_ref, prev_ref, cw_ref, cb_ref, du_ref, dx_ref, dcw_ref, dcb_ref, nxt_ref):
        i = pl.program_id(0)
        c = nb - 1 - i
        prev = jnp.where(c > 0, prev_ref[...], 0.0)
        acc, shifted = _conv_pre(jnp.concatenate([prev, cur_ref[...]], axis=0), cw_ref, cb_ref)
        sg = _sigmoid(acc)
        dacc = du_ref[...] * (sg * (1.0 + acc * (1.0 - sg)))

        @pl.when(i == 0)
        def _():
            nxt_ref[...] = jnp.zeros_like(nxt_ref)
            dcw_ref[...] = jnp.zeros_like(dcw_ref)
            dcb_ref[...] = jnp.zeros_like(dcb_ref)

        dcb_ref[...] += jnp.sum(dacc, axis=0, keepdims=True)
        for t in range(CONV_WIDTH):
            dcw_ref[t:t + 1, :] += jnp.sum(dacc * shifted[t], axis=0, keepdims=True)
        ext = jnp.concatenate([dacc, nxt_ref[...]], axis=0)
        dx = cw_ref[CONV_WIDTH - 1:CONV_WIDTH, :] * dacc
        for t in range(CONV_WIDTH - 1):
            dx += cw_ref[t:t + 1, :] * pltpu.roll(ext, 2 * BLK - (CONV_WIDTH - 1 - t), 0)[:BLK]
        dx_ref[...] = dx
        nxt_ref[...] = dacc

    rev = lambda i: (nb - 1 - i, 0)
    return pl.pallas_call(
        body, name="conv_bwd", grid=(nb,),
        in_specs=[pl.BlockSpec((BLK, n), rev), pl.BlockSpec((BLK, n), lambda i: (jnp.maximum(nb - 2 - i, 0), 0)),
                  _full_spec(cw.shape), _full_spec((1, n)), pl.BlockSpec((BLK, n), rev)],
        out_specs=[pl.BlockSpec((BLK, n), rev), _full_spec((8, n)), _full_spec((1, n))],
        out_shape=[jax.ShapeDtypeStruct((s, n), F32), jax.ShapeDtypeStruct((8, n), F32),
                   jax.ShapeDtypeStruct((1, n), F32)],
        scratch_shapes=[pltpu.VMEM((BLK, n), F32)],
        compiler_params=_params(1),
    )(xbc, xbc, cw, cb, du)


N_PAIR = SSD_HEADS // 2
B_LO = SSD_WIDTH
C_LO = SSD_WIDTH + 2 * D_STATE


def _softplus(x):
    return jnp.maximum(x, 0.0) + jnp.log(1.0 + jnp.exp(-jnp.abs(x)))


def _ssd_chunk(u_ref, dt_ref, dtb_ref, alog_ref):
    shape = (BLK, BLK)
    tri = _iota(shape, 1) <= _iota(shape, 0)
    pre = dt_ref[...] + dtb_ref[...]
    dt = _softplus(pre)
    a = -jnp.exp(alog_ref[...])
    acum = dot_nn(tri.astype(F32), dt * a, exact=True)
    acum_t = acum.T
    last = _sub_row(acum, BLK - 1)
    heads = []
    for h in range(SSD_HEADS):
        col = _lane_col(acum, h)
        seg = jnp.where(tri, col - _sub_row(acum_t, h), -1e30)
        heads.append(dict(col=col, dm=jnp.exp(seg), dt=_lane_col(dt, h), last=_lane_col(last, h)))
    return tri, pre, dt, a, heads


def _pair_mix(lo_mask, v0, v1):
    return jnp.where(lo_mask, v0, v1)


def ssd_fwd(u, z, dtr, dtb, alog, dsk, nw):
    s = u.shape[0]
    nc = s // BLK

    def body(u_ref, z_ref, dt_ref, dtb_ref, alog_ref, dsk_ref, nw_ref, y_ref, st_ref, s_ref):
        @pl.when(pl.program_id(0) == 0)
        def _():
            s_ref[...] = jnp.zeros_like(s_ref)

        _, _, _, _, heads = _ssd_chunk(u_ref, dt_ref, dtb_ref, alog_ref)
        lo_lane = _iota((BLK, LANE), 1) < HEAD_DIM
        lo_sub = _iota((BLK, LANE), 0) < HEAD_DIM
        ys = []
        for p in range(N_PAIR):
            g = p // 2
            h0, h1 = heads[2 * p], heads[2 * p + 1]
            bg = u_ref[:, B_LO + g * D_STATE:B_LO + (g + 1) * D_STATE]
            cg = u_ref[:, C_LO + g * D_STATE:C_LO + (g + 1) * D_STATE]
            xs = u_ref[:, p * LANE:(p + 1) * LANE]
            xp = xs * _pair_mix(lo_lane, h0["dt"], h1["dt"])
            gm = dot_nt(cg, bg)
            yd = _pair_mix(lo_lane, dot_nn(gm * h0["dm"], xp), dot_nn(gm * h1["dm"], xp))
            sp = s_ref[p]
            st_ref[0, p] = sp
            yo = _pair_mix(lo_lane, jnp.exp(h0["col"]), jnp.exp(h1["col"])) * dot_nt(cg, sp)
            dskp = _pair_mix(lo_lane, _lane_col(dsk_ref[...], 2 * p), _lane_col(dsk_ref[...], 2 * p + 1))
            ys.append(yd + yo + xs * dskp)
            wp = _pair_mix(lo_lane, jnp.exp(h0["last"] - h0["col"]), jnp.exp(h1["last"] - h1["col"]))
            el = _pair_mix(lo_sub, jnp.exp(h0["last"]), jnp.exp(h1["last"]))
            s_ref[p] = el * sp + dot_tn(wp * xp, bg)
        y = jnp.concatenate(ys, axis=1)
        zv = z_ref[...]
        y_ref[...] = _rms_fwd(y * zv * _sigmoid(zv), nw_ref[...])

    vec = _full_spec((1, LANE))
    return pl.pallas_call(
        body, name="ssd_fwd", grid=(nc,),
        in_specs=[_row_spec(BLK, CONV_DIM), _row_spec(BLK, SSD_WIDTH), _row_spec(BLK, LANE), vec, vec, vec,
                  _full_spec((1, SSD_WIDTH))],
        out_specs=[_row_spec(BLK, SSD_WIDTH), pl.BlockSpec((1, N_PAIR, LANE, D_STATE), lambda c: (c, 0, 0, 0))],
        out_shape=[jax.ShapeDtypeStruct((s, SSD_WIDTH), F32), jax.ShapeDtypeStruct((nc, N_PAIR, LANE, D_STATE), F32)],
        scratch_shapes=[pltpu.VMEM((N_PAIR, LANE, D_STATE), F32)],
        compiler_params=_params(1),
    )(u, z, dtr, dtb, alog, dsk, nw)


def ssd_bwd(u, z, dtr, st, dyo, dtb, alog, dsk, nw):
    s = u.shape[0]
    nc = s // BLK

    def body(u_ref, z_ref, dt_ref, st_ref, dyo_ref, dtb_ref, alog_ref, dsk_ref, nw_ref,
             du_ref, dz_ref, ddt_ref, ddtb_ref, dalog_ref, ddsk_ref, dnw_ref, ds_ref):
        first = pl.program_id(0) == 0

        @pl.when(first)
        def _():
            ds_ref[...] = jnp.zeros_like(ds_ref)

        tri, pre, dt, a, heads = _ssd_chunk(u_ref, dt_ref, dtb_ref, alog_ref)
        shape = (BLK, LANE)
        lane = _iota(shape, 1)
        lo_lane = lane < HEAD_DIM
        lo_sub = _iota(shape, 0) < HEAD_DIM
        pairs = []
        ys = []
        for p in range(N_PAIR):
            g = p // 2
            h0, h1 = heads[2 * p], heads[2 * p + 1]
            bg = u_ref[:, B_LO + g * D_STATE:B_LO + (g + 1) * D_STATE]
            cg = u_ref[:, C_LO + g * D_STATE:C_LO + (g + 1) * D_STATE]
            xs = u_ref[:, p * LANE:(p + 1) * LANE]
            dtp = _pair_mix(lo_lane, h0["dt"], h1["dt"])
            xp = xs * dtp
            gm = dot_nt(cg, bg)
            m0, m1 = gm * h0["dm"], gm * h1["dm"]
            sp = st_ref[0, p]
            eap = _pair_mix(lo_lane, jnp.exp(h0["col"]), jnp.exp(h1["col"]))
            yo = eap * dot_nt(cg, sp)
            dskp = _pair_mix(lo_lane, _lane_col(dsk_ref[...], 2 * p), _lane_col(dsk_ref[...], 2 * p + 1))
            ys.append(_pair_mix(lo_lane, dot_nn(m0, xp), dot_nn(m1, xp)) + yo + xs * dskp)
            pairs.append(dict(bg=bg, cg=cg, xs=xs, dtp=dtp, xp=xp, gm=gm, m=(m0, m1), sp=sp, eap=eap, yo=yo, dskp=dskp))
        y = jnp.concatenate(ys, axis=1)
        zv = z_ref[...]
        sz = _sigmoid(zv)
        gate = zv * sz
        dyg, dnw = _rms_bwd(y * gate, nw_ref[...], dyo_ref[...])
        _acc(dnw_ref, first, dnw)
        dy = dyg * gate
        dz_ref[...] = dyg * y * (sz * (1.0 + zv * (1.0 - sz)))

        zeros = jnp.zeros(shape, F32)
        dacum_col = zeros
        dacum_row = zeros
        ddt = zeros
        ddsk = jnp.zeros((1, LANE), F32)
        dlast = jnp.zeros((1, LANE), F32)
        head_row = _iota((1, LANE), 1)
        sub = _iota(shape, 0)
        db = [zeros, zeros]
        dc = [zeros, zeros]
        for p in range(N_PAIR):
            g = p // 2
            q = pairs[p]
            dyp = dy[:, p * LANE:(p + 1) * LANE]
            dsn = ds_ref[p]
            t = dyp * q["xs"]
            dxs = dyp * q["dskp"]
            dcs = dyp * q["eap"]
            dc[g] = dc[g] + dot_nn(dcs, q["sp"])
            dsp = dot_tn(dcs, q["cg"])
            dea = dyp * q["yo"]
            elp = _pair_mix(lo_sub, jnp.exp(heads[2 * p]["last"]), jnp.exp(heads[2 * p + 1]["last"]))
            dsp = dsp + elp * dsn
            dels = dsn * q["sp"] * elp
            wp = _pair_mix(lo_lane, jnp.exp(heads[2 * p]["last"] - heads[2 * p]["col"]),
                           jnp.exp(heads[2 * p + 1]["last"] - heads[2 * p + 1]["col"]))
            dv = dot_nt(q["bg"], dsn)
            db[g] = db[g] + dot_nn(wp * q["xp"], dsn)
            dxp = dv * wp
            dwv = dv * q["xp"] * wp
            dgm = zeros
            for k in range(2):
                h = 2 * p + k
                mine = lo_lane if k == 0 else jnp.logical_not(lo_lane)
                mine_sub = lo_sub if k == 0 else jnp.logical_not(lo_sub)
                dyh = jnp.where(mine, dyp, 0.0)
                dm = dot_nt(dyh, q["xp"])
                dxp = dxp + dot_tn(q["m"][k], dyh)
                dgm = dgm + dm * heads[h]["dm"]
                e = dm * q["m"][k]
                onehot = lane == h
                dw_col = jnp.sum(jnp.where(mine, dwv, 0.0), axis=1, keepdims=True)
                col = (jnp.sum(e, axis=1, keepdims=True) + jnp.sum(jnp.where(mine, dea, 0.0), axis=1, keepdims=True)
                       - dw_col)
                dacum_col = dacum_col + jnp.where(onehot, col, 0.0)
                dacum_row = dacum_row - jnp.where(sub == h, jnp.sum(e, axis=0, keepdims=True), 0.0)
                dl = jnp.sum(dw_col) + jnp.sum(jnp.where(mine_sub, dels, 0.0))
                dlast = dlast + jnp.where(head_row == h, dl, 0.0)
                ddsk = ddsk + jnp.where(head_row == h, jnp.sum(jnp.where(mine, t, 0.0)), 0.0)
            dc[g] = dc[g] + dot_nn(dgm, q["bg"])
            db[g] = db[g] + dot_tn(dgm, q["cg"])
            dxs = dxs + dxp * q["dtp"]
            tt = dxp * q["xs"]
            for k in range(2):
                mine = lo_lane if k == 0 else jnp.logical_not(lo_lane)
                ddt = ddt + jnp.where(lane == 2 * p + k, jnp.sum(jnp.where(mine, tt, 0.0), axis=1, keepdims=True), 0.0)
            du_ref[:, p * LANE:(p + 1) * LANE] = dxs
            ds_ref[p] = dsp
        for g in range(2):
            du_ref[:, B_LO + g * D_STATE:B_LO + (g + 1) * D_STATE] = db[g]
            du_ref[:, C_LO + g * D_STATE:C_LO + (g + 1) * D_STATE] = dc[g]
        dacum = dacum_col + dacum_row.T + jnp.where(sub == BLK - 1, dlast, 0.0)
        dda = dot_tn(tri.astype(F32), dacum, exact=True)
        ddt = ddt + dda * a
        _acc(dalog_ref, first, jnp.sum(dda * dt, axis=0, keepdims=True) * a)
        dpre = ddt * _sigmoid(pre)
        ddt_ref[...] = dpre
        _acc(ddtb_ref, first, jnp.sum(dpre, axis=0, keepdims=True))
        _acc(ddsk_ref, first, ddsk)

    rev = lambda i: (nc - 1 - i, 0)
    vec = _full_spec((1, LANE))
    rows = lambda n: pl.BlockSpec((BLK, n), rev)
    return pl.pallas_call(
        body, name="ssd_bwd", grid=(nc,),
        in_specs=[rows(CONV_DIM), rows(SSD_WIDTH), rows(LANE),
                  pl.BlockSpec((1, N_PAIR, LANE, D_STATE), lambda i: (nc - 1 - i, 0, 0, 0)), rows(SSD_WIDTH),
                  vec, vec, vec, _full_spec((1, SSD_WIDTH))],
        out_specs=[rows(CONV_DIM), rows(SSD_WIDTH), rows(LANE), vec, vec, vec, _full_spec((1, SSD_WIDTH))],
        out_shape=[jax.ShapeDtypeStruct((s, CONV_DIM), F32), jax.ShapeDtypeStruct((s, SSD_WIDTH), F32),
                   jax.ShapeDtypeStruct((s, LANE), F32)] + [jax.ShapeDtypeStruct((1, LANE), F32)] * 3
        + [jax.ShapeDtypeStruct((1, SSD_WIDTH), F32)],
        scratch_shapes=[pltpu.VMEM((N_PAIR, LANE, D_STATE), F32)],
        compiler_params=_params(1),
    )(u, z, dtr, st, dyo, dtb, alog, dsk, nw)


SB_PAIRS = SB_WIDTH // LANE
SB_SCALE = HEAD_DIM ** -0.5


SB_TQ = 256


def _sb_tq(s):
    return min(SB_TQ, s)


def _sb_stack(x):
    lo_lane = _iota(x.shape, 1) < HEAD_DIM
    return jnp.concatenate([jnp.where(lo_lane, x, 0.0), jnp.where(lo_lane, 0.0, x)], axis=0)


def _sb_unstack(x2):
    tq = x2.shape[0] // 2
    lo_lane = _iota((tq, LANE), 1) < HEAD_DIM
    return jnp.where(lo_lane, x2[:tq], x2[tq:])


def _sb_logits(q2, kj, row0, col0):
    shape = (q2.shape[0], BLK)
    tq = shape[0] // 2
    z = dot_nt(q2, kj)
    row = _iota(shape, 0)
    valid = (col0 + _iota(shape, 1)) < (row0 + jnp.where(row < tq, row, row - tq))
    t = jnp.log(1.0 + jnp.exp(-jnp.abs(z)))
    ls = jnp.minimum(z, 0.0) - t
    lk = jnp.where(valid, jnp.minimum(-z, 0.0) - t, 0.0)
    return valid, ls, lk


def _sums(x, mask2, parts):
    acc = None
    rest = x
    for _ in range(parts):
        term = rest.astype(BF16)
        rest = rest - term.astype(F32)
        d = lax.dot_general(term, mask2, (((1,), (0,)), ((), ())), preferred_element_type=F32)
        acc = d if acc is None else acc + d
    return acc[:, :BLK], acc[:, BLK:]


def _mask2(cond):
    return jnp.concatenate([cond.astype(BF16), jnp.ones(cond.shape, BF16)], axis=1)


def _sb_specs(s):
    tq = _sb_tq(s)
    qspec = pl.BlockSpec((tq, LANE), lambda p, i: (i, p))
    kspec = pl.BlockSpec((s, LANE), lambda p, i: (0, SB_PAIRS + p))
    vspec = pl.BlockSpec((s, LANE), lambda p, i: (0, 2 * SB_PAIRS + p))
    return qspec, kspec, vspec


def sb_fwd(qkv):
    s = qkv.shape[0]
    tq = _sb_tq(s)
    kpq = tq // BLK

    def body(q_ref, k_ref, v_ref, o_ref, t_ref, acc_ref):
        qi = pl.program_id(1)
        q2 = _sb_stack(q_ref[...] * SB_SCALE).astype(BF16)
        later = _mask2(_iota((BLK, BLK), 0) > _iota((BLK, BLK), 1))
        acc_ref[...] = jnp.zeros_like(acc_ref)

        def step(jj, r):
            j = kpq * (qi + 1) - 1 - jj
            rows = pl.ds(pl.multiple_of(j * BLK, BLK), BLK)
            valid, ls, lk = _sb_logits(q2, k_ref[rows, :], qi * tq, j * BLK)
            after, total = _sums(lk, later, 3)
            w = jnp.where(valid, jnp.exp(ls + r + after), 0.0)
            acc_ref[...] += dot_nn(w, v_ref[rows, :])
            return r + total

        r = lax.fori_loop(0, kpq * (qi + 1), step, jnp.zeros((2 * tq, LANE), F32))
        o_ref[...] = _sb_unstack(acc_ref[...])
        t_ref[...] = jnp.concatenate([r[:tq], r[tq:]], axis=1)

    return pl.pallas_call(
        body, name="sb_fwd", grid=(SB_PAIRS, s // tq),
        in_specs=list(_sb_specs(s)),
        out_specs=[pl.BlockSpec((tq, LANE), lambda p, i: (i, p)), pl.BlockSpec((tq, 2 * LANE), lambda p, i: (i, p))],
        out_shape=[jax.ShapeDtypeStruct((s, SB_WIDTH), F32), jax.ShapeDtypeStruct((s, 2 * SB_WIDTH), F32)],
        scratch_shapes=[pltpu.VMEM((2 * tq, LANE), F32)],
        compiler_params=_params(2),
    )(qkv, qkv, qkv)


def sb_bwd(qkv, tot, do, do_col=0):
    s = qkv.shape[0]
    tq = _sb_tq(s)
    kpq = tq // BLK

    def body(q_ref, k_ref, v_ref, t_ref, do_ref, dq_ref, dk_ref, dv_ref, acc_ref):
        qi = pl.program_id(1)
        q2 = _sb_stack(q_ref[...] * SB_SCALE).astype(BF16)
        do2 = _sb_stack(do_ref[...]).astype(BF16)
        tot2 = jnp.concatenate([t_ref[:, :LANE], t_ref[:, LANE:]], axis=0)
        sq = (BLK, BLK)
        upto = _mask2(_iota(sq, 0) <= _iota(sq, 1))
        before = _mask2(_iota(sq, 0) < _iota(sq, 1))
        acc_ref[...] = jnp.zeros_like(acc_ref)

        @pl.when(qi == 0)
        def _():
            dk_ref[...] = jnp.zeros_like(dk_ref)
            dv_ref[...] = jnp.zeros_like(dv_ref)

        def step(j, carry):
            pc, fc = carry
            rows = pl.ds(pl.multiple_of(j * BLK, BLK), BLK)
            kj = k_ref[rows, :]
            vj = v_ref[rows, :]
            valid, ls, lk = _sb_logits(q2, kj, qi * tq, j * BLK)
            p_in, p_tot = _sums(lk, upto, 3)
            w = jnp.where(valid, jnp.exp(ls + (tot2 - (pc + p_in))), 0.0)
            e = w * dot_nt(do2, vj)
            f_in, f_tot = _sums(e, before, 2)
            sg = jnp.exp(ls)
            dz = jnp.where(valid, e * (1.0 - sg) - (fc + f_in) * sg, 0.0)
            acc_ref[...] += dot_nn(dz, kj)
            dk_ref[rows, :] += dot_tn(dz, q2)
            dv_ref[rows, :] += dot_tn(w, do2)
            return pc + p_tot, fc + f_tot

        zero = jnp.zeros((2 * tq, LANE), F32)
        lax.fori_loop(0, kpq * (qi + 1), step, (zero, zero))
        dq_ref[...] = SB_SCALE * _sb_unstack(acc_ref[...])

    qspec, kspec, vspec = _sb_specs(s)
    blk = pl.BlockSpec((tq, LANE), lambda p, i: (i, p))
    acc = pl.BlockSpec((s, LANE), lambda p, i: (0, p))
    return pl.pallas_call(
        body, name="sb_bwd", grid=(SB_PAIRS, s // tq),
        in_specs=[qspec, kspec, vspec, pl.BlockSpec((tq, 2 * LANE), lambda p, i: (i, p)),
                  pl.BlockSpec((tq, LANE), lambda p, i: (i, do_col + p))],
        out_specs=[blk, acc, acc],
        out_shape=[jax.ShapeDtypeStruct((s, SB_WIDTH), F32)] * 3,
        scratch_shapes=[pltpu.VMEM((2 * tq, LANE), F32)],
        compiler_params=_params(2),
    )(qkv, qkv, qkv, tot, do)


POOL_GROUP_DIM = POOL_WIDTH // len(POOL_WINDOWS)


def _pool_consts(c):
    t = _iota((BLK, 2 * BLK), 0)
    j = _iota((BLK, 2 * BLK), 1) - BLK
    bands = [jnp.logical_and(j <= t, j > t - win).astype(F32) for win in POOL_WINDOWS]
    group = _iota((BLK, POOL_WIDTH), 1) // POOL_GROUP_DIM
    pos = c * BLK + _iota((BLK, POOL_WIDTH), 0)
    win = jnp.zeros((BLK, POOL_WIDTH), jnp.int32)
    for gi, wn in enumerate(POOL_WINDOWS):
        win = jnp.where(group == gi, wn, win)
    inv = 1.0 / jnp.minimum(pos + 1, win).astype(F32)
    return bands, group, inv


def _pool_pooled(ext, cur, bands, group, inv):
    sums = jnp.zeros(cur.shape, F32)
    for gi, band in enumerate(bands):
        sums = jnp.where(group == gi, dot_nn(band, ext, exact=True), sums)
    return sums * inv - cur


def pool_fwd(p, wblk, pb, ps):
    s, n = p.shape

    def body(cur_ref, prev_ref, w_ref, pb_ref, ps_ref, o_ref):
        c = pl.program_id(0)
        cur = cur_ref[...]
        prev = jnp.where(c > 0, prev_ref[...], 0.0)
        bands, group, inv = _pool_consts(c)
        pooled = _pool_pooled(jnp.concatenate([prev, cur], axis=0), cur, bands, group, inv)
        o_ref[...] = (dot_nn(pooled, w_ref[...]) + pb_ref[...]) * ps_ref[...]

    return pl.pallas_call(
        body, name="pool_fwd", grid=(s // BLK,),
        in_specs=[pl.BlockSpec((BLK, n), lambda c: (c, 0)), pl.BlockSpec((BLK, n), lambda c: (jnp.maximum(c - 1, 0), 0)),
                  _full_spec((n, n)), _full_spec((1, n)), _full_spec((1, n))],
        out_specs=pl.BlockSpec((BLK, n), lambda c: (c, 0)), out_shape=jax.ShapeDtypeStruct((s, n), F32),
        compiler_params=_params(1),
    )(p, p, wblk, pb, ps)


def pool_bwd(p, wblk, pb, ps, dout, do_col=0):
    s, n = p.shape
    nb = s // BLK

    def body(cur_ref, prev_ref, w_ref, pb_ref, ps_ref, do_ref, dp_ref, dw_ref, dpb_ref, dps_ref, carry_ref):
        i = pl.program_id(0)
        c = nb - 1 - i
        first = i == 0
        cur = cur_ref[...]
        prev = jnp.where(c > 0, prev_ref[...], 0.0)
        bands, group, inv = _pool_consts(c)
        pooled = _pool_pooled(jnp.concatenate([prev, cur], axis=0), cur, bands, group, inv)
        mixed = dot_nn(pooled, w_ref[...]) + pb_ref[...]
        dov = do_ref[...]
        dmixed = dov * ps_ref[...]
        _acc(dps_ref, first, jnp.sum(dov * mixed, axis=0, keepdims=True))
        _acc(dpb_ref, first, jnp.sum(dmixed, axis=0, keepdims=True))
        _acc(dw_ref, first, dot_tn(pooled, dmixed))
        dpooled = dot_nt(dmixed, w_ref[...])
        dsums = dpooled * inv
        dext = jnp.zeros((2 * BLK, n), F32)
        for gi, band in enumerate(bands):
            dext = dext + dot_tn(band, jnp.where(group == gi, dsums, 0.0), exact=True)

        @pl.when(first)
        def _():
            carry_ref[...] = jnp.zeros_like(carry_ref)

        dp_ref[...] = dext[BLK:] - dpooled + carry_ref[...]
        carry_ref[...] = dext[:BLK]

    rev = lambda i: (nb - 1 - i, 0)
    return pl.pallas_call(
        body, name="pool_bwd", grid=(nb,),
        in_specs=[pl.BlockSpec((BLK, n), rev), pl.BlockSpec((BLK, n), lambda i: (jnp.maximum(nb - 2 - i, 0), 0)),
                  _full_spec((n, n)), _full_spec((1, n)), _full_spec((1, n)),
                  pl.BlockSpec((BLK, n), lambda i: (nb - 1 - i, do_col))],
        out_specs=[pl.BlockSpec((BLK, n), rev), _full_spec((n, n)), _full_spec((1, n)), _full_spec((1, n))],
        out_shape=[jax.ShapeDtypeStruct((s, n), F32), jax.ShapeDtypeStruct((n, n), F32),
                   jax.ShapeDtypeStruct((1, n), F32), jax.ShapeDtypeStruct((1, n), F32)],
        scratch_shapes=[pltpu.VMEM((BLK, n), F32)],
        compiler_params=_params(1),
    )(p, p, wblk, pb, ps, dout)


def _row_tile(rows):
    for t in (256, 128, 64, 32, 16, 8):
        if rows % t == 0:
            return t
    return rows


def adamw(w, g, m, v):
    n, rows, cols = w.shape
    tr = _row_tile(rows)

    def body(w_ref, g_ref, m_ref, v_ref, d_ref, nm_ref, nv_ref):
        gv = g_ref[...]
        nm = ADAM_B1 * m_ref[...] + (1.0 - ADAM_B1) * gv
        nv = ADAM_B2 * v_ref[...] + (1.0 - ADAM_B2) * (gv * gv)
        m_hat = nm / (1.0 - ADAM_B1 ** ADAM_STEP)
        v_hat = nv / (1.0 - ADAM_B2 ** ADAM_STEP)
        d_ref[...] = -ADAM_LR * (m_hat / (jnp.sqrt(v_hat) + ADAM_EPS) + ADAM_WD * w_ref[...])
        nm_ref[...] = nm
        nv_ref[...] = nv

    spec = pl.BlockSpec((1, tr, cols), lambda i, j: (i, j, 0))
    return pl.pallas_call(
        body, name="adamw", grid=(n, rows // tr), in_specs=[spec] * 4, out_specs=[spec] * 3,
        out_shape=[jax.ShapeDtypeStruct(w.shape, F32)] * 3, compiler_params=_params(2),
    )(w, g, m, v)


def slab_sum(srcs, n_out, out_dtype):
    _, rows, cols = srcs[0][0].shape
    tr = _row_tile(rows)
    sel = jnp.stack([jnp.asarray(base, jnp.int32) for _, base, _ in srcs])

    def body(sel_ref, *refs):
        acc = refs[0][...].astype(F32)
        for r in refs[1:-1]:
            acc = acc + r[...].astype(F32)
        refs[-1][...] = acc.astype(out_dtype)

    def in_spec(k, step):
        return pl.BlockSpec((None, tr, cols), lambda o, i, sel_ref: (sel_ref[k] + step * o, i, 0))

    return pl.pallas_call(
        body, name="slab_sum",
        grid_spec=pltpu.PrefetchScalarGridSpec(
            num_scalar_prefetch=1, grid=(n_out, rows // tr),
            in_specs=[in_spec(k, step) for k, (_, _, step) in enumerate(srcs)],
            out_specs=pl.BlockSpec((None, tr, cols), lambda o, i, sel_ref: (o, i, 0))),
        out_shape=jax.ShapeDtypeStruct((n_out, rows, cols), out_dtype), compiler_params=_params(2),
    )(sel, *[a for a, _, _ in srcs])


ICI_FLIPS = ((1, 0, 0), (0, 1, 0), (1, 1, 0))
D2D_FLIPS = ((0, 0, 1),)
ANY = pl.BlockSpec(memory_space=pl.ANY)


def _me():
    return lax.axis_index("x"), lax.axis_index("y"), lax.axis_index("c")


def _flipped(me, flip):
    return tuple(1 - m if f else m for m, f in zip(me, flip))


def _chip(dev):
    return 2 * dev[0] + dev[1]


def _dev(dev):
    return 4 * dev[0] + 2 * dev[1] + dev[2]


N_CHIP = 4
D2D = (0, 0, 1)


def exchange(xs, n_out, copies, own, name, in_place=False):
    n_arr, n_cp = len(xs), len(copies)

    def body(*refs):
        x_refs, o_refs = refs[:n_arr], refs[n_arr:2 * n_arr]
        send_sems, recv_sems, own_sems = refs[2 * n_arr:]
        me = _me()
        mine = []
        if own is not None:
            for a in range(n_arr):
                mine.append(pltpu.make_async_copy(x_refs[a].at[own[0](me)], o_refs[a].at[own[1](me)], own_sems.at[a]))
                mine[-1].start()

        def copy(a, j, sender):
            flip, src_slot, dst_slot = copies[j]
            return pltpu.make_async_remote_copy(
                src_ref=x_refs[a].at[src_slot(me)], dst_ref=o_refs[a].at[dst_slot(sender)],
                send_sem=send_sems.at[a * n_cp + j], recv_sem=recv_sems.at[a * n_cp + j],
                device_id=_flipped(me, flip), device_id_type=MESH)

        sends = [copy(a, j, me) for j in range(n_cp) for a in range(n_arr)]
        for cp in sends:
            cp.start()
        for j in range(n_cp):
            for a in range(n_arr):
                copy(a, j, _flipped(me, copies[j][0])).wait_recv()
        for cp in sends:
            cp.wait_send()
        for cp in mine:
            cp.wait()

    return pl.pallas_call(
        body, name=name, in_specs=[ANY] * n_arr, out_specs=[ANY] * n_arr,
        out_shape=[jax.ShapeDtypeStruct((n_out,) + x.shape[1:], x.dtype) for x in xs],
        input_output_aliases={a: a for a in range(n_arr)} if in_place else {},
        scratch_shapes=[pltpu.SemaphoreType.DMA((n_arr * n_cp,)), pltpu.SemaphoreType.DMA((n_arr * n_cp,)),
                        pltpu.SemaphoreType.DMA((n_arr,))],
    )(*xs)


def all_gather(shards):
    ici = [(f, lambda me: 0, _dev) for f in ICI_FLIPS]
    outs = exchange([s[None] for s in shards], N_DEV, ici, (lambda me: 0, _dev), "gather_ici")
    d2d = [(D2D, (lambda me, k=k: 2 * k + me[2]), (lambda sender, k=k: 2 * k + sender[2])) for k in range(N_CHIP)]
    outs = exchange(outs, N_DEV, d2d, None, "gather_d2d", in_place=True)
    return [o.reshape(N_DEV * s.shape[0], s.shape[1]) for o, s in zip(outs, shards)]


def reduce_scatter(parts):
    x, y, c = _me()
    d2d = [(D2D, (lambda me, k=k: 2 * k + 1 - me[2]), (lambda sender, k=k: k)) for k in range(N_CHIP)]
    sib = exchange(parts, N_CHIP, d2d, None, "scatter_d2d")
    chip_sums = [slab_sum([(p, c, 2), (s, 0, 1)], N_CHIP, BF16) for p, s in zip(parts, sib)]
    ici = [(f, (lambda me, f=f: _chip(_flipped(me, f))), (lambda sender, i=i: i)) for i, f in enumerate(ICI_FLIPS)]
    got = exchange(chip_sums, len(ICI_FLIPS), ici, None, "scatter_ici")
    return [slab_sum([(cs, 2 * x + y, 0)] + [(g, i, 0) for i in range(len(ICI_FLIPS))], 1, F32)[0]
            for cs, g in zip(chip_sums, got)]


def all_reduce_small(v):
    flips = D2D_FLIPS + ICI_FLIPS[:2]

    def body(v_ref, o_ref, got_ref, send_sems, recv_sems):
        me = _me()
        o_ref[...] = v_ref[...]
        for i, flip in enumerate(flips):
            cp = pltpu.make_async_remote_copy(
                src_ref=o_ref, dst_ref=got_ref.at[i], send_sem=send_sems.at[i], recv_sem=recv_sems.at[i],
                device_id=_flipped(me, flip), device_id_type=MESH)
            cp.start()
            cp.wait()
            o_ref[...] = o_ref[...] + got_ref[i]

    vm = pl.BlockSpec(memory_space=pltpu.VMEM)
    return pl.pallas_call(
        body, name="all_reduce_small", in_specs=[vm], out_specs=vm, out_shape=jax.ShapeDtypeStruct(v.shape, F32),
        scratch_shapes=[pltpu.VMEM((len(flips),) + v.shape, F32), pltpu.SemaphoreType.DMA((len(flips),)),
                        pltpu.SemaphoreType.DMA((len(flips),))],
    )(v)


def _perm_rows(wt):
    pad = jnp.zeros((D_IN_PAD - D_IN_PROJ, wt.shape[1]), wt.dtype)
    return jnp.concatenate([wt[:DT_LO], wt[DT_HI:], wt[DT_LO:DT_HI], pad], axis=0)


def _unperm_rows(dwt):
    n = D_IN_PROJ - (DT_HI - DT_LO)
    return jnp.concatenate([dwt[:DT_LO], dwt[n:D_IN_PROJ], dwt[DT_LO:n]], axis=0)


def _pad_lanes(v):
    return jnp.pad(v, ((0, 0), (0, LANE - v.shape[1])))[:, None]


def _block_diag(w):
    l, g, n, _ = w.shape
    out = jnp.zeros((l, g * n, g * n), w.dtype)
    for i in range(g):
        out = out.at[:, i * n:(i + 1) * n, i * n:(i + 1) * n].set(w[:, i])
    return out


def _pack(groups):
    flat = []
    for grp in groups:
        parts = [a.reshape(-1) for a in (grp if isinstance(grp, (list, tuple)) else [grp])]
        n = sum(p.shape[0] for p in parts)
        if -n % LANE:
            parts.append(jnp.zeros((-n % LANE,), parts[0].dtype))
        flat += parts
    return jnp.concatenate(flat).reshape(-1, LANE)


def _unpack(buf, shapes):
    out = []
    lo = 0
    buf = buf.reshape(-1)
    for shp in shapes:
        n = 1
        for k in shp:
            n *= k
        out.append(buf[lo:lo + n].reshape(shp))
        lo += n + (-n % LANE)
    return out


def small_params(w, conv_w_full):
    return dict(
        n1w=w["norm1_w"][:, None], cw=jnp.pad(conv_w_full, ((0, 0), (0, 8 - CONV_WIDTH), (0, 0))),
        cb=w["conv_b"][:, None], dtb=_pad_lanes(w["dt_bias"]), alog=_pad_lanes(w["a_log"]), dsk=_pad_lanes(w["d_skip"]),
        snw=w["ssd_norm_w"][:, None], wblk=_block_diag(w["pool_w"]), pb=w["pool_b"].reshape(-1, 1, POOL_WIDTH),
        ps=w["pool_scale"][:, None], n2w=w["norm2_w"][:, None])


BIG = ("w_in", "w_out", "w_gate", "w_up", "w_down")


def layer_params(small, big, l):
    p = {k: v[l] for k, v in small.items()}
    p.update(w_in=_perm_rows(big[0]), w_out=big[1], wg=big[2], wu=big[3], wd=big[4])
    return p


def _layer_fwd(x, p):
    z, xbc, qkv, pp, dtr, h1 = inproj_fwd(x, p["n1w"], p["w_in"])
    u = conv_fwd(xbc, p["cw"], p["cb"])
    y_ssd, st = ssd_fwd(u, z, dtr, p["dtb"], p["alog"], p["dsk"], p["snw"])
    o, tot = sb_fwd(qkv)
    yp = pool_fwd(pp, p["wblk"], p["pb"], p["ps"])
    ycat = jnp.concatenate([y_ssd, o, yp], axis=1)
    x_mid = outproj_fwd(ycat, p["w_out"], x)
    x_out, g, uu = ffn_fwd(x_mid, p["n2w"], p["wg"], p["wu"], p["wd"])
    return x_out, dict(x=x, z=z, xbc=xbc, qkv=qkv, pp=pp, dtr=dtr, h1=h1, u=u, st=st, tot=tot, ycat=ycat, x_mid=x_mid,
                       g=g, uu=uu)


def _layer_bwd(dxo, sv, p):
    dx_mid, dn2w, a, dg, du, h2 = ffn_bwd(dxo, sv["x_mid"], sv["g"], sv["uu"], p["n2w"], p["wg"], p["wu"], p["wd"])
    gr = dict(norm2_w=dn2w[0], w_down=mm_tn(a, dxo), w_gate=mm_tn(dg, h2), w_up=mm_tn(du, h2))
    dycat = outproj_bwd(dx_mid, p["w_out"])
    gr["w_out"] = mm_tn(sv["ycat"], dx_mid)
    dp, dwblk, dpb, dps = pool_bwd(sv["pp"], p["wblk"], p["pb"], p["ps"], dycat,
                                   (SSD_WIDTH + SB_WIDTH) // POOL_WIDTH)
    n = POOL_GROUP_DIM
    gr["pool_w"] = jnp.stack([dwblk[i * n:(i + 1) * n, i * n:(i + 1) * n] for i in range(len(POOL_WINDOWS))])
    gr["pool_b"] = dpb.reshape(len(POOL_WINDOWS), n)
    gr["pool_scale"] = dps[0]
    dq, dk, dv = sb_bwd(sv["qkv"], sv["tot"], dycat, SSD_WIDTH // LANE)
    du_, dz, ddtr, ddtb, dalog, ddsk, dsnw = ssd_bwd(sv["u"], sv["z"], sv["dtr"], sv["st"], dycat, p["dtb"], p["alog"],
                                                     p["dsk"], p["snw"])
    gr.update(dt_bias=ddtb[0, :SSD_HEADS], a_log=dalog[0, :SSD_HEADS], d_skip=ddsk[0, :SSD_HEADS], ssd_norm_w=dsnw[0])
    dxbc, dcw, dcb = conv_bwd(sv["xbc"], p["cw"], p["cb"], du_)
    gr.update(conv_w=dcw[:CONV_WIDTH], conv_b=dcb[0])
    dproj = jnp.concatenate([dz, dxbc, dq, dk, dv, dp, ddtr], axis=1)
    dx, dn1w = inproj_bwd(dproj, p["w_in"], sv["x"], p["n1w"], dx_mid)
    gr.update(norm1_w=dn1w[0], w_in=_unperm_rows(mm_tn(dproj, sv["h1"])))
    return dx, gr


def local_step(x, tgt, params, final_w):
    saved = []
    for p in params:
        x, sv = _layer_fwd(x, p)
        saved.append(sv)
    loss, dx, dfw = head_loss(x, final_w[None], tgt)
    grads = []
    for p, sv in zip(reversed(params), reversed(saved)):
        dx, gr = _layer_bwd(dx, sv, p)
        grads.append(gr)
    grads.reverse()
    return loss, dx, dfw[0], grads


WEIGHTS = ("norm1_w", "w_in", "conv_w", "conv_b", "dt_bias", "a_log", "d_skip", "ssd_norm_w", "pool_w", "pool_b",
           "pool_scale", "w_out", "norm2_w", "w_gate", "w_up", "w_down", "final_norm_w")
COL_SHARDED = ("w_in", "w_gate", "w_up")
ROW_SHARDED = ("w_out", "w_down")
SMALL = tuple(k for k in WEIGHTS if k not in COL_SHARDED + ROW_SHARDED)


def kernel(x, norm1_w, w_in, conv_w, conv_b, dt_bias, a_log, d_skip, ssd_norm_w, pool_w, pool_b, pool_scale, w_out, norm2_w, w_gate, w_up, w_down, final_norm_w, loss_target, m_norm1_w, m_w_in, m_conv_w, m_conv_b, m_dt_bias, m_a_log, m_d_skip, m_ssd_norm_w, m_pool_w, m_pool_b, m_pool_scale, m_w_out, m_norm2_w, m_w_gate, m_w_up, m_w_down, m_final_norm_w, v_norm1_w, v_w_in, v_conv_w, v_conv_b, v_dt_bias, v_a_log, v_d_skip, v_ssd_norm_w, v_pool_w, v_pool_b, v_pool_scale, v_w_out, v_norm2_w, v_w_gate, v_w_up, v_w_down, v_final_norm_w):
    args = dict(locals())
    w = {k: args[k] for k in WEIGHTS}
    m = {k: args["m_" + k] for k in WEIGHTS}
    v = {k: args["v_" + k] for k in WEIGHTS}
    depth = w_in.shape[0]
    dev = _dev(_me())
    n_cw = conv_w.shape[-1]

    shards = {k: (jnp.swapaxes(w[k], 1, 2) if k in COL_SHARDED else w[k]).astype(BF16) for k in BIG}
    conv_w_full = all_gather([jnp.swapaxes(conv_w, 0, 2).reshape(n_cw, -1)])[0]
    conv_w_full = jnp.swapaxes(conv_w_full.reshape(N_DEV * n_cw, CONV_WIDTH, depth), 0, 2)
    small = small_params(w, conv_w_full)
    xs = x[0]
    params, saved = [], []
    for l in range(depth):
        params.append(layer_params(small, all_gather([shards[k][l] for k in BIG]), l))
        xs, sv = _layer_fwd(xs, params[-1])
        saved.append(sv)
    loss, dx, dfw = head_loss(xs, final_norm_w[None], loss_target[0])
    layer_grads = [None] * depth
    big_grads = [None] * depth
    for l in reversed(range(depth)):
        dx, layer_grads[l] = _layer_bwd(dx, saved[l], params[l])
        big_grads[l] = reduce_scatter([layer_grads[l][k].reshape(N_DEV, -1, D_MODEL) for k in BIG])

    grads = {}
    for i, k in enumerate(BIG):
        stacked = jnp.stack([big_grads[l][i] for l in range(depth)])
        grads[k] = jnp.swapaxes(stacked, 1, 2) if k in COL_SHARDED else stacked
    layered = [k for k in SMALL if k != "final_norm_w"]
    small_shapes = [(1, LANE)] + [(depth,) + layer_grads[0][k].shape for k in layered] + [dfw[0].shape]
    packed = _pack([loss] + [[layer_grads[l][k] for l in range(depth)] for k in layered] + [dfw[0]])
    summed = _unpack(all_reduce_small(packed), small_shapes)
    loss = summed[0][0, 0]
    grads.update(zip(layered + ["final_norm_w"], summed[1:]))
    grads["conv_w"] = lax.dynamic_slice_in_dim(grads["conv_w"], dev * n_cw, n_cw, axis=2)

    delta, new_m, new_v = {}, {}, {}
    for k in COL_SHARDED + ROW_SHARDED:
        delta[k], new_m[k], new_v[k] = adamw(w[k], grads[k], m[k], v[k])
    shapes = [w[k].shape for k in SMALL]
    packed = [_pack([t[k] for k in SMALL])[None] for t in (w, grads, m, v)]
    for dst, buf in zip((delta, new_m, new_v), adamw(*packed)):
        dst.update(zip(SMALL, _unpack(buf, shapes)))
    return (loss, dx[None], *[grads[k] for k in WEIGHTS], *[delta[k] for k in WEIGHTS],
            *[new_m[k] for k in WEIGHTS], *[new_v[k] for k in WEIGHTS])
```

```python
import functools

import jax
import jax.numpy as jnp
from jax import lax
from jax.experimental import pallas as pl
from jax.experimental.pallas import tpu as pltpu

F32 = jnp.float32
BF16 = jnp.bfloat16
HIGHEST = lax.Precision.HIGHEST
MESH = pl.DeviceIdType.MESH

EPS = 1e-6
D_MODEL = 1024
SSD_WIDTH = 512
SSD_HEADS = 8
HEAD_DIM = 64
D_STATE = 128
CONV_WIDTH = 4
CONV_DIM = 1024
SB_WIDTH = 256
POOL_WIDTH = 256
POOL_WINDOWS = (2, 4, 8, 16)
D_IN_PROJ = 2568
D_FF = 2816
N_DEV = 8
DEPTH = 4
SEG = (512, 1024, 768, 256, 128)
D_IN_PAD = sum(SEG)
DT_LO, DT_HI = 1536, 1544

LANE = 128
BLK = 128
ROW_TILE = 256
VMEM_LIMIT = 56 * 2**20

ADAM_LR, ADAM_B1, ADAM_B2, ADAM_EPS, ADAM_WD, ADAM_STEP = 0.001, 0.9, 0.999, 1e-08, 0.01, 10


def _params(n_axes=1, vmem=None):
    return pltpu.CompilerParams(dimension_semantics=("arbitrary",) * n_axes, vmem_limit_bytes=vmem)


def _dot(a, b, dims, exact=False):
    if exact:
        return lax.dot_general(a.astype(F32), b.astype(F32), (dims, ((), ())), precision=HIGHEST,
                               preferred_element_type=F32)
    return lax.dot_general(a.astype(BF16), b.astype(BF16), (dims, ((), ())), preferred_element_type=F32)


def dot_nn(a, b, exact=False):
    return _dot(a, b, ((1,), (0,)), exact)


def dot_nt(a, b, exact=False):
    return _dot(a, b, ((1,), (1,)), exact)


def dot_tn(a, b, exact=False):
    return _dot(a, b, ((0,), (0,)), exact)


def _iota(shape, axis):
    return lax.broadcasted_iota(jnp.int32, shape, axis)


def _lane_col(x, h):
    return jnp.sum(jnp.where(_iota(x.shape, 1) == h, x, 0.0), axis=1, keepdims=True)


def _sub_row(x, h):
    return jnp.sum(jnp.where(_iota(x.shape, 0) == h, x, 0.0), axis=0, keepdims=True)


def _sigmoid(x):
    return 1.0 / (1.0 + jnp.exp(-x))


def _rms_fwd(x, w):
    r = lax.rsqrt(jnp.mean(x * x, axis=-1, keepdims=True) + EPS)
    return x * r * w


def _rms_bwd(x, w, dy):
    r = lax.rsqrt(jnp.mean(x * x, axis=-1, keepdims=True) + EPS)
    xh = x * r
    dxh = dy * w
    dx = r * (dxh - xh * jnp.mean(dxh * xh, axis=-1, keepdims=True))
    return dx, jnp.sum(dy * xh, axis=0, keepdims=True)


def _acc(ref, first, val):
    @pl.when(first)
    def _():
        ref[...] = val

    @pl.when(jnp.logical_not(first))
    def _():
        ref[...] += val


def _row_spec(tm, n):
    return pl.BlockSpec((tm, n), lambda i: (i, 0))


def _full_spec(shape):
    return pl.BlockSpec(shape, lambda *_: (0,) * len(shape))


def inproj_fwd(x, nw, w):
    s, d = x.shape
    tm = min(ROW_TILE, s)

    def body(x_ref, nw_ref, w_ref, z_ref, xbc_ref, qkv_ref, p_ref, dt_ref, h_ref):
        h = _rms_fwd(x_ref[...], nw_ref[...]).astype(BF16)
        h_ref[...] = h
        lo = 0
        for ref, n in zip((z_ref, xbc_ref, qkv_ref, p_ref, dt_ref), SEG):
            ref[...] = dot_nt(h, w_ref[lo:lo + n, :])
            lo += n

    return pl.pallas_call(
        body, name="inproj_fwd", grid=(s // tm,),
        in_specs=[_row_spec(tm, d), _full_spec((1, d)), _full_spec(w.shape)],
        out_specs=[_row_spec(tm, n) for n in SEG] + [_row_spec(tm, d)],
        out_shape=[jax.ShapeDtypeStruct((s, n), F32) for n in SEG] + [jax.ShapeDtypeStruct((s, d), BF16)],
        compiler_params=_params(1, VMEM_LIMIT),
    )(x, nw, w)


def inproj_bwd(dproj, w, x, nw, dres):
    s, d = x.shape
    tm = min(ROW_TILE, s)

    def body(dp_ref, w_ref, x_ref, nw_ref, dres_ref, dx_ref, dnw_ref):
        dh = dot_nn(dp_ref[...], w_ref[...])
        dx, dnw = _rms_bwd(x_ref[...], nw_ref[...], dh)
        dx_ref[...] = dres_ref[...] + dx
        _acc(dnw_ref, pl.program_id(0) == 0, dnw)

    return pl.pallas_call(
        body, name="inproj_bwd", grid=(s // tm,),
        in_specs=[_row_spec(tm, dproj.shape[1]), _full_spec(w.shape), _row_spec(tm, d), _full_spec((1, d)),
                  _row_spec(tm, d)],
        out_specs=[_row_spec(tm, d), _full_spec((1, d))],
        out_shape=[jax.ShapeDtypeStruct((s, d), F32), jax.ShapeDtypeStruct((1, d), F32)],
        compiler_params=_params(1, VMEM_LIMIT),
    )(dproj, w, x, nw, dres)


def outproj_fwd(y, w, res):
    s, d = res.shape
    tm = min(ROW_TILE, s)

    def body(y_ref, w_ref, r_ref, o_ref):
        o_ref[...] = r_ref[...] + dot_nn(y_ref[...], w_ref[...])

    return pl.pallas_call(
        body, name="outproj_fwd", grid=(s // tm,),
        in_specs=[_row_spec(tm, y.shape[1]), _full_spec(w.shape), _row_spec(tm, d)],
        out_specs=_row_spec(tm, d), out_shape=jax.ShapeDtypeStruct((s, d), F32),
        compiler_params=_params(1, VMEM_LIMIT),
    )(y, w, res)


def outproj_bwd(dx, w):
    s, d = dx.shape
    tm = min(ROW_TILE, s)

    def body(dx_ref, w_ref, o_ref):
        o_ref[...] = dot_nt(dx_ref[...], w_ref[...])

    return pl.pallas_call(
        body, name="outproj_bwd", grid=(s // tm,),
        in_specs=[_row_spec(tm, d), _full_spec(w.shape)],
        out_specs=_row_spec(tm, w.shape[0]), out_shape=jax.ShapeDtypeStruct((s, w.shape[0]), F32),
        compiler_params=_params(1, VMEM_LIMIT),
    )(dx, w)


def ffn_fwd(x, nw, wg, wu, wd, side=None):
    s, d = x.shape
    f = wg.shape[0]
    tm = min(ROW_TILE, s)

    def body(x_ref, nw_ref, wg_ref, wu_ref, wd_ref, o_ref, g_ref, u_ref):
        xv = x_ref[...]
        h = _rms_fwd(xv, nw_ref[...]).astype(BF16)
        g = dot_nt(h, wg_ref[...])
        u = dot_nt(h, wu_ref[...])
        g_ref[...] = g.astype(BF16)
        u_ref[...] = u.astype(BF16)
        o_ref[...] = xv + dot_nn(g * _sigmoid(g) * u, wd_ref[...])

    return hosted_call(
        body, side, name="ffn_fwd", grid=(s // tm,),
        in_specs=[_row_spec(tm, d), _full_spec((1, d)), _full_spec(wg.shape), _full_spec(wu.shape),
                  _full_spec(wd.shape)],
        out_specs=[_row_spec(tm, d), _row_spec(tm, f), _row_spec(tm, f)],
        out_shape=[jax.ShapeDtypeStruct((s, d), F32), jax.ShapeDtypeStruct((s, f), BF16),
                   jax.ShapeDtypeStruct((s, f), BF16)],
        scratch_shapes=[], compiler_params=_params(1, VMEM_LIMIT), operands=(x, nw, wg, wu, wd))


def ffn_bwd(dxo, x, g, u, nw, wg, wu, wd, side=None):
    s, d = x.shape
    f = wg.shape[0]
    tm = min(ROW_TILE, s)

    def body(dxo_ref, x_ref, g_ref, u_ref, nw_ref, wg_ref, wu_ref, wd_ref, dx_ref, dnw_ref, a_ref, dg_ref,
             du_ref, h_ref):
        dxo_v = dxo_ref[...]
        xv = x_ref[...]
        da = dot_nt(dxo_v, wd_ref[...])
        gv = g_ref[...].astype(F32)
        uv = u_ref[...].astype(F32)
        sg = _sigmoid(gv)
        sl = gv * sg
        a_ref[...] = (sl * uv).astype(BF16)
        dg = (da * uv * (sg * (1.0 + gv * (1.0 - sg)))).astype(BF16)
        du = (da * sl).astype(BF16)
        dg_ref[...] = dg
        du_ref[...] = du
        dh = dot_nn(dg, wg_ref[...]) + dot_nn(du, wu_ref[...])
        h_ref[...] = _rms_fwd(xv, nw_ref[...]).astype(BF16)
        dx, dnw = _rms_bwd(xv, nw_ref[...], dh)
        dx_ref[...] = dxo_v + dx
        _acc(dnw_ref, pl.program_id(0) == 0, dnw)

    return hosted_call(
        body, side, name="ffn_bwd", grid=(s // tm,),
        in_specs=[_row_spec(tm, d), _row_spec(tm, d), _row_spec(tm, f), _row_spec(tm, f), _full_spec((1, d)),
                  _full_spec(wg.shape), _full_spec(wu.shape), _full_spec(wd.shape)],
        out_specs=[_row_spec(tm, d), _full_spec((1, d)), _row_spec(tm, f), _row_spec(tm, f), _row_spec(tm, f),
                   _row_spec(tm, d)],
        out_shape=[jax.ShapeDtypeStruct((s, d), F32), jax.ShapeDtypeStruct((1, d), F32),
                   jax.ShapeDtypeStruct((s, f), BF16), jax.ShapeDtypeStruct((s, f), BF16),
                   jax.ShapeDtypeStruct((s, f), BF16), jax.ShapeDtypeStruct((s, d), BF16)],
        scratch_shapes=[], compiler_params=_params(1, VMEM_LIMIT), operands=(dxo, x, g, u, nw, wg, wu, wd))


def _tile(n, cap=256):
    best = LANE
    for t in range(LANE, cap + 1, LANE):
        if n % t == 0:
            best = t
    return best


def mm_tn(a, b):
    s, k = a.shape
    n = b.shape[1]
    tk = _tile(k)

    def body(a_ref, b_ref, o_ref):
        o_ref[...] = dot_nn(a_ref[...].astype(BF16).T, b_ref[...])

    return pl.pallas_call(
        body, name="mm_tn", grid=(k // tk,),
        in_specs=[pl.BlockSpec((s, tk), lambda i: (0, i)), _full_spec((s, n))],
        out_specs=pl.BlockSpec((tk, n), lambda i: (i, 0)), out_shape=jax.ShapeDtypeStruct((k, n), F32),
        compiler_params=_params(1, VMEM_LIMIT),
    )(a, b)


def head_loss(x, fw, tgt):
    s, d = x.shape
    tm = min(ROW_TILE, s)

    def body(x_ref, fw_ref, t_ref, loss_ref, dx_ref, dfw_ref):
        xv = x_ref[...]
        err = _rms_fwd(xv, fw_ref[...]) - t_ref[...]
        part = jnp.zeros((1, LANE), F32) + 0.5 * jnp.sum(err * err) / d
        dx, dfw = _rms_bwd(xv, fw_ref[...], err / d)
        dx_ref[...] = dx
        first = pl.program_id(0) == 0
        _acc(loss_ref, first, part)
        _acc(dfw_ref, first, dfw)

    return pl.pallas_call(
        body, name="head_loss", grid=(s // tm,),
        in_specs=[_row_spec(tm, d), _full_spec((1, d)), _row_spec(tm, d)],
        out_specs=[_full_spec((1, LANE)), _row_spec(tm, d), _full_spec((1, d))],
        out_shape=[jax.ShapeDtypeStruct((1, LANE), F32), jax.ShapeDtypeStruct((s, d), F32),
                   jax.ShapeDtypeStruct((1, d), F32)],
        compiler_params=_params(1),
    )(x, fw, tgt)


def _conv_pre(ext, cw_ref, cb_ref):
    shifted = [pltpu.roll(ext, CONV_WIDTH - 1 - i, 0)[BLK:] if i < CONV_WIDTH - 1 else ext[BLK:]
               for i in range(CONV_WIDTH)]
    acc = cb_ref[...] + sum(cw_ref[i:i + 1, :] * shifted[i] for i in range(CONV_WIDTH))
    return acc, shifted


def conv_fwd(xbc, cw, cb):
    s, n = xbc.shape

    def body(cur_ref, prev_ref, cw_ref, cb_ref, o_ref):
        prev = jnp.where(pl.program_id(0) > 0, prev_ref[...], 0.0)
        acc, _ = _conv_pre(jnp.concatenate([prev, cur_ref[...]], axis=0), cw_ref, cb_ref)
        o_ref[...] = acc * _sigmoid(acc)

    return pl.pallas_call(
        body, name="conv_fwd", grid=(s // BLK,),
        in_specs=[pl.BlockSpec((BLK, n), lambda c: (c, 0)), pl.BlockSpec((BLK, n), lambda c: (jnp.maximum(c - 1, 0), 0)),
                  _full_spec(cw.shape), _full_spec((1, n))],
        out_specs=pl.BlockSpec((BLK, n), lambda c: (c, 0)), out_shape=jax.ShapeDtypeStruct((s, n), F32),
        compiler_params=_params(1),
    )(xbc, xbc, cw, cb)


def conv_bwd(xbc, cw, cb, du):
    s, n = xbc.shape
    nb = s // BLK

    def body(cur_ref, prev_ref, cw_ref, cb_ref, du_ref, dx_ref, dcw_ref, dcb_ref, nxt_ref):
        i = pl.program_id(0)
        c = nb - 1 - i
        prev = jnp.where(c > 0, prev_ref[...], 0.0)
        acc, shifted = _conv_pre(jnp.concatenate([prev, cur_ref[...]], axis=0), cw_ref, cb_ref)
        sg = _sigmoid(acc)
        dacc = du_ref[...] * (sg * (1.0 + acc * (1.0 - sg)))

        @pl.when(i == 0)
        def _():
            nxt_ref[...] = jnp.zeros_like(nxt_ref)
            dcw_ref[...] = jnp.zeros_like(dcw_ref)
            dcb_ref[...] = jnp.zeros_like(dcb_ref)

        dcb_ref[...] += jnp.sum(dacc, axis=0, keepdims=True)
        for t in range(CONV_WIDTH):
            dcw_ref[t:t + 1, :] += jnp.sum(dacc * shifted[t], axis=0, keepdims=True)
        ext = jnp.concatenate([dacc, nxt_ref[...]], axis=0)
        dx = cw_ref[CONV_WIDTH - 1:CONV_WIDTH, :] * dacc
        for t in range(CONV_WIDTH - 1):
            dx += cw_ref[t:t + 1, :] * pltpu.roll(ext, 2 * BLK - (CONV_WIDTH - 1 - t), 0)[:BLK]
        dx_ref[...] = dx
        nxt_ref[...] = dacc

    rev = lambda i: (nb - 1 - i, 0)
    return pl.pallas_call(
        body, name="conv_bwd", grid=(nb,),
        in_specs=[pl.BlockSpec((BLK, n), rev), pl.BlockSpec((BLK, n), lambda i: (jnp.maximum(nb - 2 - i, 0), 0)),
                  _full_spec(cw.shape), _full_spec((1, n)), pl.BlockSpec((BLK, n), rev)],
        out_specs=[pl.BlockSpec((BLK, n), rev), _full_spec((8, n)), _full_spec((1, n))],
        out_shape=[jax.ShapeDtypeStruct((s, n), F32), jax.ShapeDtypeStruct((8, n), F32),
                   jax.ShapeDtypeStruct((1, n), F32)],
        scratch_shapes=[pltpu.VMEM((BLK, n), F32)],
        compiler_params=_params(1),
    )(xbc, xbc, cw, cb, du)


N_PAIR = SSD_HEADS // 2
B_LO = SSD_WIDTH
C_LO = SSD_WIDTH + 2 * D_STATE


def _softplus(x):
    return jnp.maximum(x, 0.0) + jnp.log(1.0 + jnp.exp(-jnp.abs(x)))


def _ssd_chunk(u_ref, dt_ref, dtb_ref, alog_ref):
    shape = (BLK, BLK)
    tri = _iota(shape, 1) <= _iota(shape, 0)
    pre = dt_ref[...] + dtb_ref[...]
    dt = _softplus(pre)
    a = -jnp.exp(alog_ref[...])
    acum = dot_nn(tri.astype(F32), dt * a, exact=True)
    acum_t = acum.T
    last = _sub_row(acum, BLK - 1)
    heads = []
    for h in range(SSD_HEADS):
        col = _lane_col(acum, h)
        seg = jnp.where(tri, col - _sub_row(acum_t, h), -1e30)
        heads.append(dict(col=col, dm=jnp.exp(seg), dt=_lane_col(dt, h), last=_lane_col(last, h)))
    return tri, pre, dt, a, heads


def _pair_mix(lo_mask, v0, v1):
    return jnp.where(lo_mask, v0, v1)


def ssd_fwd(u, z, dtr, dtb, alog, dsk, nw):
    s = u.shape[0]
    nc = s // BLK

    def body(u_ref, z_ref, dt_ref, dtb_ref, alog_ref, dsk_ref, nw_ref, y_ref, st_ref, s_ref):
        @pl.when(pl.program_id(0) == 0)
        def _():
            s_ref[...] = jnp.zeros_like(s_ref)

        _, _, _, _, heads = _ssd_chunk(u_ref, dt_ref, dtb_ref, alog_ref)
        lo_lane = _iota((BLK, LANE), 1) < HEAD_DIM
        lo_sub = _iota((BLK, LANE), 0) < HEAD_DIM
        ys = []
        for p in range(N_PAIR):
            g = p // 2
            h0, h1 = heads[2 * p], heads[2 * p + 1]
            bg = u_ref[:, B_LO + g * D_STATE:B_LO + (g + 1) * D_STATE]
            cg = u_ref[:, C_LO + g * D_STATE:C_LO + (g + 1) * D_STATE]
            xs = u_ref[:, p * LANE:(p + 1) * LANE]
            xp = xs * _pair_mix(lo_lane, h0["dt"], h1["dt"])
            gm = dot_nt(cg, bg)
            yd = _pair_mix(lo_lane, dot_nn(gm * h0["dm"], xp), dot_nn(gm * h1["dm"], xp))
            sp = s_ref[p]
            st_ref[0, p] = sp
            yo = _pair_mix(lo_lane, jnp.exp(h0["col"]), jnp.exp(h1["col"])) * dot_nt(cg, sp)
            dskp = _pair_mix(lo_lane, _lane_col(dsk_ref[...], 2 * p), _lane_col(dsk_ref[...], 2 * p + 1))
            ys.append(yd + yo + xs * dskp)
            wp = _pair_mix(lo_lane, jnp.exp(h0["last"] - h0["col"]), jnp.exp(h1["last"] - h1["col"]))
            el = _pair_mix(lo_sub, jnp.exp(h0["last"]), jnp.exp(h1["last"]))
            s_ref[p] = el * sp + dot_tn(wp * xp, bg)
        y = jnp.concatenate(ys, axis=1)
        zv = z_ref[...]
        y_ref[...] = _rms_fwd(y * zv * _sigmoid(zv), nw_ref[...])

    vec = _full_spec((1, LANE))
    return pl.pallas_call(
        body, name="ssd_fwd", grid=(nc,),
        in_specs=[_row_spec(BLK, CONV_DIM), _row_spec(BLK, SSD_WIDTH), _row_spec(BLK, LANE), vec, vec, vec,
                  _full_spec((1, SSD_WIDTH))],
        out_specs=[_row_spec(BLK, SSD_WIDTH), pl.BlockSpec((1, N_PAIR, LANE, D_STATE), lambda c: (c, 0, 0, 0))],
        out_shape=[jax.ShapeDtypeStruct((s, SSD_WIDTH), F32), jax.ShapeDtypeStruct((nc, N_PAIR, LANE, D_STATE), F32)],
        scratch_shapes=[pltpu.VMEM((N_PAIR, LANE, D_STATE), F32)],
        compiler_params=_params(1),
    )(u, z, dtr, dtb, alog, dsk, nw)


def ssd_bwd(u, z, dtr, st, dyo, dtb, alog, dsk, nw):
    s = u.shape[0]
    nc = s // BLK

    def body(u_ref, z_ref, dt_ref, st_ref, dyo_ref, dtb_ref, alog_ref, dsk_ref, nw_ref,
             du_ref, dz_ref, ddt_ref, ddtb_ref, dalog_ref, ddsk_ref, dnw_ref, ds_ref):
        first = pl.program_id(0) == 0

        @pl.when(first)
        def _():
            ds_ref[...] = jnp.zeros_like(ds_ref)

        tri, pre, dt, a, heads = _ssd_chunk(u_ref, dt_ref, dtb_ref, alog_ref)
        shape = (BLK, LANE)
        lane = _iota(shape, 1)
        lo_lane = lane < HEAD_DIM
        lo_sub = _iota(shape, 0) < HEAD_DIM
        pairs = []
        ys = []
        for p in range(N_PAIR):
            g = p // 2
            h0, h1 = heads[2 * p], heads[2 * p + 1]
            bg = u_ref[:, B_LO + g * D_STATE:B_LO + (g + 1) * D_STATE]
            cg = u_ref[:, C_LO + g * D_STATE:C_LO + (g + 1) * D_STATE]
            xs = u_ref[:, p * LANE:(p + 1) * LANE]
            dtp = _pair_mix(lo_lane, h0["dt"], h1["dt"])
            xp = xs * dtp
            gm = dot_nt(cg, bg)
            m0, m1 = gm * h0["dm"], gm * h1["dm"]
            sp = st_ref[0, p]
            eap = _pair_mix(lo_lane, jnp.exp(h0["col"]), jnp.exp(h1["col"]))
            yo = eap * dot_nt(cg, sp)
            dskp = _pair_mix(lo_lane, _lane_col(dsk_ref[...], 2 * p), _lane_col(dsk_ref[...], 2 * p + 1))
            ys.append(_pair_mix(lo_lane, dot_nn(m0, xp), dot_nn(m1, xp)) + yo + xs * dskp)
            pairs.append(dict(bg=bg, cg=cg, xs=xs, dtp=dtp, xp=xp, gm=gm, m=(m0, m1), sp=sp, eap=eap, yo=yo, dskp=dskp))
        y = jnp.concatenate(ys, axis=1)
        zv = z_ref[...]
        sz = _sigmoid(zv)
        gate = zv * sz
        dyg, dnw = _rms_bwd(y * gate, nw_ref[...], dyo_ref[...])
        _acc(dnw_ref, first, dnw)
        dy = dyg * gate
        dz_ref[...] = dyg * y * (sz * (1.0 + zv * (1.0 - sz)))

        zeros = jnp.zeros(shape, F32)
        dacum_col = zeros
        dacum_row = zeros
        ddt = zeros
        ddsk = jnp.zeros((1, LANE), F32)
        dlast = jnp.zeros((1, LANE), F32)
        head_row = _iota((1, LANE), 1)
        sub = _iota(shape, 0)
        db = [zeros, zeros]
        dc = [zeros, zeros]
        for p in range(N_PAIR):
            g = p // 2
            q = pairs[p]
            dyp = dy[:, p * LANE:(p + 1) * LANE]
            dsn = ds_ref[p]
            t = dyp * q["xs"]
            dxs = dyp * q["dskp"]
            dcs = dyp * q["eap"]
            dc[g] = dc[g] + dot_nn(dcs, q["sp"])
            dsp = dot_tn(dcs, q["cg"])
            dea = dyp * q["yo"]
            elp = _pair_mix(lo_sub, jnp.exp(heads[2 * p]["last"]), jnp.exp(heads[2 * p + 1]["last"]))
            dsp = dsp + elp * dsn
            dels = dsn * q["sp"] * elp
            wp = _pair_mix(lo_lane, jnp.exp(heads[2 * p]["last"] - heads[2 * p]["col"]),
                           jnp.exp(heads[2 * p + 1]["last"] - heads[2 * p + 1]["col"]))
            dv = dot_nt(q["bg"], dsn)
            db[g] = db[g] + dot_nn(wp * q["xp"], dsn)
            dxp = dv * wp
            dwv = dv * q["xp"] * wp
            dgm = zeros
            for k in range(2):
                h = 2 * p + k
                mine = lo_lane if k == 0 else jnp.logical_not(lo_lane)
                mine_sub = lo_sub if k == 0 else jnp.logical_not(lo_sub)
                dyh = jnp.where(mine, dyp, 0.0)
                dm = dot_nt(dyh, q["xp"])
                dxp = dxp + dot_tn(q["m"][k], dyh)
                dgm = dgm + dm * heads[h]["dm"]
                e = dm * q["m"][k]
                onehot = lane == h
                dw_col = jnp.sum(jnp.where(mine, dwv, 0.0), axis=1, keepdims=True)
                col = (jnp.sum(e, axis=1, keepdims=True) + jnp.sum(jnp.where(mine, dea, 0.0), axis=1, keepdims=True)
                       - dw_col)
                dacum_col = dacum_col + jnp.where(onehot, col, 0.0)
                dacum_row = dacum_row - jnp.where(sub == h, jnp.sum(e, axis=0, keepdims=True), 0.0)
                dl = jnp.sum(dw_col) + jnp.sum(jnp.where(mine_sub, dels, 0.0))
                dlast = dlast + jnp.where(head_row == h, dl, 0.0)
                ddsk = ddsk + jnp.where(head_row == h, jnp.sum(jnp.where(mine, t, 0.0)), 0.0)
            dc[g] = dc[g] + dot_nn(dgm, q["bg"])
            db[g] = db[g] + dot_tn(dgm, q["cg"])
            dxs = dxs + dxp * q["dtp"]
            tt = dxp * q["xs"]
            for k in range(2):
                mine = lo_lane if k == 0 else jnp.logical_not(lo_lane)
                ddt = ddt + jnp.where(lane == 2 * p + k, jnp.sum(jnp.where(mine, tt, 0.0), axis=1, keepdims=True), 0.0)
            du_ref[:, p * LANE:(p + 1) * LANE] = dxs
            ds_ref[p] = dsp
        for g in range(2):
            du_ref[:, B_LO + g * D_STATE:B_LO + (g + 1) * D_STATE] = db[g]
            du_ref[:, C_LO + g * D_STATE:C_LO + (g + 1) * D_STATE] = dc[g]
        dacum = dacum_col + dacum_row.T + jnp.where(sub == BLK - 1, dlast, 0.0)
        dda = dot_tn(tri.astype(F32), dacum, exact=True)
        ddt = ddt + dda * a
        _acc(dalog_ref, first, jnp.sum(dda * dt, axis=0, keepdims=True) * a)
        dpre = ddt * _sigmoid(pre)
        ddt_ref[...] = dpre
        _acc(ddtb_ref, first, jnp.sum(dpre, axis=0, keepdims=True))
        _acc(ddsk_ref, first, ddsk)

    rev = lambda i: (nc - 1 - i, 0)
    vec = _full_spec((1, LANE))
    rows = lambda n: pl.BlockSpec((BLK, n), rev)
    return pl.pallas_call(
        body, name="ssd_bwd", grid=(nc,),
        in_specs=[rows(CONV_DIM), rows(SSD_WIDTH), rows(LANE),
                  pl.BlockSpec((1, N_PAIR, LANE, D_STATE), lambda i: (nc - 1 - i, 0, 0, 0)), rows(SSD_WIDTH),
                  vec, vec, vec, _full_spec((1, SSD_WIDTH))],
        out_specs=[rows(CONV_DIM), rows(SSD_WIDTH), rows(LANE), vec, vec, vec, _full_spec((1, SSD_WIDTH))],
        out_shape=[jax.ShapeDtypeStruct((s, CONV_DIM), F32), jax.ShapeDtypeStruct((s, SSD_WIDTH), F32),
                   jax.ShapeDtypeStruct((s, LANE), F32)] + [jax.ShapeDtypeStruct((1, LANE), F32)] * 3
        + [jax.ShapeDtypeStruct((1, SSD_WIDTH), F32)],
        scratch_shapes=[pltpu.VMEM((N_PAIR, LANE, D_STATE), F32)],
        compiler_params=_params(1),
    )(u, z, dtr, st, dyo, dtb, alog, dsk, nw)


SB_PAIRS = SB_WIDTH // LANE
SB_SCALE = HEAD_DIM ** -0.5


SB_TQ = 256


def _sb_tq(s):
    return min(SB_TQ, s)


def _sb_stack(x):
    lo_lane = _iota(x.shape, 1) < HEAD_DIM
    return jnp.concatenate([jnp.where(lo_lane, x, 0.0), jnp.where(lo_lane, 0.0, x)], axis=0)


def _sb_unstack(x2):
    tq = x2.shape[0] // 2
    lo_lane = _iota((tq, LANE), 1) < HEAD_DIM
    return jnp.where(lo_lane, x2[:tq], x2[tq:])


def _sb_logits(q2, kj, row0, col0):
    shape = (q2.shape[0], BLK)
    tq = shape[0] // 2
    z = dot_nt(q2, kj)
    row = _iota(shape, 0)
    valid = (col0 + _iota(shape, 1)) < (row0 + jnp.where(row < tq, row, row - tq))
    t = jnp.log(1.0 + jnp.exp(-jnp.abs(z)))
    ls = jnp.minimum(z, 0.0) - t
    lk = jnp.where(valid, jnp.minimum(-z, 0.0) - t, 0.0)
    return valid, ls, lk


def _sums(x, mask2, parts):
    acc = None
    rest = x
    for _ in range(parts):
        term = rest.astype(BF16)
        rest = rest - term.astype(F32)
        d = lax.dot_general(term, mask2, (((1,), (0,)), ((), ())), preferred_element_type=F32)
        acc = d if acc is None else acc + d
    return acc[:, :BLK], acc[:, BLK:]


def _mask2(cond):
    return jnp.concatenate([cond.astype(BF16), jnp.ones(cond.shape, BF16)], axis=1)


def _sb_specs(s):
    tq = _sb_tq(s)
    qspec = pl.BlockSpec((tq, LANE), lambda p, i: (i, p))
    kspec = pl.BlockSpec((s, LANE), lambda p, i: (0, SB_PAIRS + p))
    vspec = pl.BlockSpec((s, LANE), lambda p, i: (0, 2 * SB_PAIRS + p))
    return qspec, kspec, vspec


def sb_fwd(qkv, side=None):
    s = qkv.shape[0]
    tq = _sb_tq(s)
    kpq = tq // BLK

    def body(q_ref, k_ref, v_ref, o_ref, t_ref, acc_ref):
        qi = pl.program_id(1)
        q2 = _sb_stack(q_ref[...] * SB_SCALE).astype(BF16)
        later = _mask2(_iota((BLK, BLK), 0) > _iota((BLK, BLK), 1))
        acc_ref[...] = jnp.zeros_like(acc_ref)

        def step(jj, r):
            j = kpq * (qi + 1) - 1 - jj
            rows = pl.ds(pl.multiple_of(j * BLK, BLK), BLK)
            valid, ls, lk = _sb_logits(q2, k_ref[rows, :], qi * tq, j * BLK)
            after, total = _sums(lk, later, 3)
            w = jnp.where(valid, jnp.exp(ls + r + after), 0.0)
            acc_ref[...] += dot_nn(w, v_ref[rows, :])
            return r + total

        r = lax.fori_loop(0, kpq * (qi + 1), step, jnp.zeros((2 * tq, LANE), F32))
        o_ref[...] = _sb_unstack(acc_ref[...])
        t_ref[...] = jnp.concatenate([r[:tq], r[tq:]], axis=1)

    return hosted_call(
        body, side, name="sb_fwd", grid=(SB_PAIRS, s // tq),
        in_specs=list(_sb_specs(s)),
        out_specs=[pl.BlockSpec((tq, LANE), lambda p, i: (i, p)), pl.BlockSpec((tq, 2 * LANE), lambda p, i: (i, p))],
        out_shape=[jax.ShapeDtypeStruct((s, SB_WIDTH), F32), jax.ShapeDtypeStruct((s, 2 * SB_WIDTH), F32)],
        scratch_shapes=[pltpu.VMEM((2 * tq, LANE), F32)],
        compiler_params=_params(2), operands=(qkv, qkv, qkv))


def sb_bwd(qkv, tot, do, do_col=0, side=None):
    s = qkv.shape[0]
    tq = _sb_tq(s)
    kpq = tq // BLK

    def body(q_ref, k_ref, v_ref, t_ref, do_ref, dq_ref, dk_ref, dv_ref, acc_ref):
        qi = pl.program_id(1)
        q2 = _sb_stack(q_ref[...] * SB_SCALE).astype(BF16)
        do2 = _sb_stack(do_ref[...]).astype(BF16)
        tot2 = jnp.concatenate([t_ref[:, :LANE], t_ref[:, LANE:]], axis=0)
        sq = (BLK, BLK)
        upto = _mask2(_iota(sq, 0) <= _iota(sq, 1))
        before = _mask2(_iota(sq, 0) < _iota(sq, 1))
        acc_ref[...] = jnp.zeros_like(acc_ref)

        @pl.when(qi == 0)
        def _():
            dk_ref[...] = jnp.zeros_like(dk_ref)
            dv_ref[...] = jnp.zeros_like(dv_ref)

        def step(j, carry):
            pc, fc = carry
            rows = pl.ds(pl.multiple_of(j * BLK, BLK), BLK)
            kj = k_ref[rows, :]
            vj = v_ref[rows, :]
            valid, ls, lk = _sb_logits(q2, kj, qi * tq, j * BLK)
            p_in, p_tot = _sums(lk, upto, 3)
            w = jnp.where(valid, jnp.exp(ls + (tot2 - (pc + p_in))), 0.0)
            e = w * dot_nt(do2, vj)
            f_in, f_tot = _sums(e, before, 2)
            sg = jnp.exp(ls)
            dz = jnp.where(valid, e * (1.0 - sg) - (fc + f_in) * sg, 0.0)
            acc_ref[...] += dot_nn(dz, kj)
            dk_ref[rows, :] += dot_tn(dz, q2)
            dv_ref[rows, :] += dot_tn(w, do2)
            return pc + p_tot, fc + f_tot

        zero = jnp.zeros((2 * tq, LANE), F32)
        lax.fori_loop(0, kpq * (qi + 1), step, (zero, zero))
        dq_ref[...] = SB_SCALE * _sb_unstack(acc_ref[...])

    qspec, kspec, vspec = _sb_specs(s)
    blk = pl.BlockSpec((tq, LANE), lambda p, i: (i, p))
    acc = pl.BlockSpec((s, LANE), lambda p, i: (0, p))
    return hosted_call(
        body, side, name="sb_bwd", grid=(SB_PAIRS, s // tq),
        in_specs=[qspec, kspec, vspec, pl.BlockSpec((tq, 2 * LANE), lambda p, i: (i, p)),
                  pl.BlockSpec((tq, LANE), lambda p, i: (i, do_col + p))],
        out_specs=[blk, acc, acc],
        out_shape=[jax.ShapeDtypeStruct((s, SB_WIDTH), F32)] * 3,
        scratch_shapes=[pltpu.VMEM((2 * tq, LANE), F32)],
        compiler_params=_params(2), operands=(qkv, qkv, qkv, tot, do))


POOL_GROUP_DIM = POOL_WIDTH // len(POOL_WINDOWS)


def _pool_consts(c):
    t = _iota((BLK, 2 * BLK), 0)
    j = _iota((BLK, 2 * BLK), 1) - BLK
    bands = [jnp.logical_and(j <= t, j > t - win).astype(F32) for win in POOL_WINDOWS]
    group = _iota((BLK, POOL_WIDTH), 1) // POOL_GROUP_DIM
    pos = c * BLK + _iota((BLK, POOL_WIDTH), 0)
    win = jnp.zeros((BLK, POOL_WIDTH), jnp.int32)
    for gi, wn in enumerate(POOL_WINDOWS):
        win = jnp.where(group == gi, wn, win)
    inv = 1.0 / jnp.minimum(pos + 1, win).astype(F32)
    return bands, group, inv


def _pool_pooled(ext, cur, bands, group, inv):
    sums = jnp.zeros(cur.shape, F32)
    for gi, band in enumerate(bands):
        sums = jnp.where(group == gi, dot_nn(band, ext, exact=True), sums)
    return sums * inv - cur


def pool_fwd(p, wblk, pb, ps):
    s, n = p.shape

    def body(cur_ref, prev_ref, w_ref, pb_ref, ps_ref, o_ref):
        c = pl.program_id(0)
        cur = cur_ref[...]
        prev = jnp.where(c > 0, prev_ref[...], 0.0)
        bands, group, inv = _pool_consts(c)
        pooled = _pool_pooled(jnp.concatenate([prev, cur], axis=0), cur, bands, group, inv)
        o_ref[...] = (dot_nn(pooled, w_ref[...]) + pb_ref[...]) * ps_ref[...]

    return pl.pallas_call(
        body, name="pool_fwd", grid=(s // BLK,),
        in_specs=[pl.BlockSpec((BLK, n), lambda c: (c, 0)), pl.BlockSpec((BLK, n), lambda c: (jnp.maximum(c - 1, 0), 0)),
                  _full_spec((n, n)), _full_spec((1, n)), _full_spec((1, n))],
        out_specs=pl.BlockSpec((BLK, n), lambda c: (c, 0)), out_shape=jax.ShapeDtypeStruct((s, n), F32),
        compiler_params=_params(1),
    )(p, p, wblk, pb, ps)


def pool_bwd(p, wblk, pb, ps, dout, do_col=0):
    s, n = p.shape
    nb = s // BLK

    def body(cur_ref, prev_ref, w_ref, pb_ref, ps_ref, do_ref, dp_ref, dw_ref, dpb_ref, dps_ref, carry_ref):
        i = pl.program_id(0)
        c = nb - 1 - i
        first = i == 0
        cur = cur_ref[...]
        prev = jnp.where(c > 0, prev_ref[...], 0.0)
        bands, group, inv = _pool_consts(c)
        pooled = _pool_pooled(jnp.concatenate([prev, cur], axis=0), cur, bands, group, inv)
        mixed = dot_nn(pooled, w_ref[...]) + pb_ref[...]
        dov = do_ref[...]
        dmixed = dov * ps_ref[...]
        _acc(dps_ref, first, jnp.sum(dov * mixed, axis=0, keepdims=True))
        _acc(dpb_ref, first, jnp.sum(dmixed, axis=0, keepdims=True))
        _acc(dw_ref, first, dot_tn(pooled, dmixed))
        dpooled = dot_nt(dmixed, w_ref[...])
        dsums = dpooled * inv
        dext = jnp.zeros((2 * BLK, n), F32)
        for gi, band in enumerate(bands):
            dext = dext + dot_tn(band, jnp.where(group == gi, dsums, 0.0), exact=True)

        @pl.when(first)
        def _():
            carry_ref[...] = jnp.zeros_like(carry_ref)

        dp_ref[...] = dext[BLK:] - dpooled + carry_ref[...]
        carry_ref[...] = dext[:BLK]

    rev = lambda i: (nb - 1 - i, 0)
    return pl.pallas_call(
        body, name="pool_bwd", grid=(nb,),
        in_specs=[pl.BlockSpec((BLK, n), rev), pl.BlockSpec((BLK, n), lambda i: (jnp.maximum(nb - 2 - i, 0), 0)),
                  _full_spec((n, n)), _full_spec((1, n)), _full_spec((1, n)),
                  pl.BlockSpec((BLK, n), lambda i: (nb - 1 - i, do_col))],
        out_specs=[pl.BlockSpec((BLK, n), rev), _full_spec((n, n)), _full_spec((1, n)), _full_spec((1, n))],
        out_shape=[jax.ShapeDtypeStruct((s, n), F32), jax.ShapeDtypeStruct((n, n), F32),
                   jax.ShapeDtypeStruct((1, n), F32), jax.ShapeDtypeStruct((1, n), F32)],
        scratch_shapes=[pltpu.VMEM((BLK, n), F32)],
        compiler_params=_params(1),
    )(p, p, wblk, pb, ps, dout)


def _row_tile(rows):
    if rows <= 512:
        return rows
    for t in (512, 256, 128, 64, 32, 16, 8):
        if rows % t == 0:
            return t
    return rows


def adamw(w, g, m, v):
    n, rows, cols = w.shape
    tr = _row_tile(rows)

    def body(w_ref, g_ref, m_ref, v_ref, d_ref, nm_ref, nv_ref):
        gv = g_ref[...]
        nm = ADAM_B1 * m_ref[...] + (1.0 - ADAM_B1) * gv
        nv = ADAM_B2 * v_ref[...] + (1.0 - ADAM_B2) * (gv * gv)
        m_hat = nm / (1.0 - ADAM_B1 ** ADAM_STEP)
        v_hat = nv / (1.0 - ADAM_B2 ** ADAM_STEP)
        d_ref[...] = -ADAM_LR * (m_hat / (jnp.sqrt(v_hat) + ADAM_EPS) + ADAM_WD * w_ref[...])
        nm_ref[...] = nm
        nv_ref[...] = nv

    spec = pl.BlockSpec((1, tr, cols), lambda i, j: (i, j, 0))
    return pl.pallas_call(
        body, name="adamw", grid=(n, rows // tr), in_specs=[spec] * 4, out_specs=[spec] * 3,
        out_shape=[jax.ShapeDtypeStruct(w.shape, F32)] * 3, compiler_params=_params(2),
    )(w, g, m, v)


def slab_sum(srcs, n_out, out_dtype):
    _, rows, cols = srcs[0][0].shape
    tr = _row_tile(rows)
    sel = jnp.stack([jnp.asarray(base, jnp.int32) for _, base, _ in srcs])

    def body(sel_ref, *refs):
        acc = refs[0][...].astype(F32)
        for r in refs[1:-1]:
            acc = acc + r[...].astype(F32)
        refs[-1][...] = acc.astype(out_dtype)

    def in_spec(k, step):
        return pl.BlockSpec((None, tr, cols), lambda o, i, sel_ref: (sel_ref[k] + step * o, i, 0))

    return pl.pallas_call(
        body, name="slab_sum",
        grid_spec=pltpu.PrefetchScalarGridSpec(
            num_scalar_prefetch=1, grid=(n_out, rows // tr),
            in_specs=[in_spec(k, step) for k, (_, _, step) in enumerate(srcs)],
            out_specs=pl.BlockSpec((None, tr, cols), lambda o, i, sel_ref: (o, i, 0))),
        out_shape=jax.ShapeDtypeStruct((n_out, rows, cols), out_dtype), compiler_params=_params(2),
    )(sel, *[a for a, _, _ in srcs])


ICI_FLIPS = ((1, 0, 0), (0, 1, 0), (1, 1, 0))
D2D_FLIPS = ((0, 0, 1),)
ANY = pl.BlockSpec(memory_space=pl.ANY)


def _me():
    return lax.axis_index("x"), lax.axis_index("y"), lax.axis_index("c")


def _flipped(me, flip):
    return tuple(1 - m if f else m for m, f in zip(me, flip))


def _chip(dev):
    return 2 * dev[0] + dev[1]


def _dev(dev):
    return 4 * dev[0] + 2 * dev[1] + dev[2]


N_CHIP = 4
D2D = (0, 0, 1)


class Exchange:
    def __init__(self, xs, n_out, copies, own=None, in_place=False):
        self.xs, self.copies, self.own, self.in_place = list(xs), copies, own, in_place
        self.n_arr, self.n_cp = len(self.xs), len(copies)
        self.out_shape = [jax.ShapeDtypeStruct((n_out,) + x.shape[1:], x.dtype) for x in self.xs]
        self.scratch = [pltpu.SemaphoreType.DMA((self.n_arr * self.n_cp,)),
                        pltpu.SemaphoreType.DMA((self.n_arr * self.n_cp,)), pltpu.SemaphoreType.DMA((self.n_arr,))]

    def _own(self, x_refs, o_refs, sems, me):
        if self.own is None:
            return []
        return [pltpu.make_async_copy(x_refs[a].at[self.own[0](me)], o_refs[a].at[self.own[1](me)], sems[2].at[a])
                for a in range(self.n_arr)]

    def _copy(self, x_refs, o_refs, sems, me, a, j, sender):
        flip, src_slot, dst_slot = self.copies[j]
        k = a * self.n_cp + j
        return pltpu.make_async_remote_copy(
            src_ref=x_refs[a].at[src_slot(me)], dst_ref=o_refs[a].at[dst_slot(sender)],
            send_sem=sems[0].at[k], recv_sem=sems[1].at[k], device_id=_flipped(me, flip), device_id_type=MESH)

    def start(self, x_refs, o_refs, sems):
        me = _me()
        for cp in self._own(x_refs, o_refs, sems, me):
            cp.start()
        for j in range(self.n_cp):
            for a in range(self.n_arr):
                self._copy(x_refs, o_refs, sems, me, a, j, me).start()

    def wait(self, x_refs, o_refs, sems):
        me = _me()
        for j in range(self.n_cp):
            for a in range(self.n_arr):
                self._copy(x_refs, o_refs, sems, me, a, j, _flipped(me, self.copies[j][0])).wait_recv()
        for j in range(self.n_cp):
            for a in range(self.n_arr):
                self._copy(x_refs, o_refs, sems, me, a, j, me).wait_send()
        for cp in self._own(x_refs, o_refs, sems, me):
            cp.wait()

    def run(self, name):
        n = self.n_arr

        def body(*refs):
            self.start(refs[:n], refs[n:2 * n], refs[2 * n:])
            self.wait(refs[:n], refs[n:2 * n], refs[2 * n:])

        return pl.pallas_call(
            body, name=name, in_specs=[ANY] * n, out_specs=[ANY] * n, out_shape=self.out_shape,
            input_output_aliases={a: a for a in range(n)} if self.in_place else {}, scratch_shapes=self.scratch,
        )(*self.xs)


def hosted_call(body, side, *, name, grid, in_specs, out_specs, out_shape, scratch_shapes, compiler_params, operands):
    n_in, n_out, n_scr = len(in_specs), len(out_specs), len(scratch_shapes)
    if side is None:
        outs = pl.pallas_call(body, name=name, grid=grid, in_specs=in_specs, out_specs=out_specs, out_shape=out_shape,
                              scratch_shapes=scratch_shapes, compiler_params=compiler_params)(*operands)
        return outs, None
    n = side.n_arr

    def full_body(*refs):
        ins, sx = refs[:n_in], refs[n_in:n_in + n]
        outs, so = refs[n_in + n:n_in + n + n_out], refs[n_in + n + n_out:n_in + 2 * n + n_out]
        scr, sems = refs[n_in + 2 * n + n_out:n_in + 2 * n + n_out + n_scr], refs[n_in + 2 * n + n_out + n_scr:]
        first = functools.reduce(jnp.logical_and, [pl.program_id(a) == 0 for a in range(len(grid))])
        last = functools.reduce(jnp.logical_and, [pl.program_id(a) == g - 1 for a, g in enumerate(grid)])

        @pl.when(first)
        def _():
            side.start(sx, so, sems)

        body(*ins, *outs, *scr)

        @pl.when(last)
        def _():
            side.wait(sx, so, sems)

    outs = pl.pallas_call(
        full_body, name=name + "_x", grid=grid, in_specs=list(in_specs) + [ANY] * n,
        out_specs=list(out_specs) + [ANY] * n, out_shape=list(out_shape) + side.out_shape,
        input_output_aliases={n_in + a: n_out + a for a in range(n)} if side.in_place else {},
        scratch_shapes=list(scratch_shapes) + side.scratch, compiler_params=compiler_params,
    )(*operands, *side.xs)
    return outs[:n_out], outs[n_out:]


def gather_ici(shards):
    ici = [(f, lambda me: 0, _dev) for f in ICI_FLIPS]
    return Exchange([s[None] for s in shards], N_DEV, ici, (lambda me: 0, _dev))


def gather_d2d(blocks):
    d2d = [(D2D, (lambda me, k=k: 2 * k + me[2]), (lambda sender, k=k: 2 * k + sender[2])) for k in range(N_CHIP)]
    return Exchange(blocks, N_DEV, d2d, None, in_place=True)


def gathered(blocks):
    return [b.reshape(-1, b.shape[2]) for b in blocks]


def scatter_d2d(parts):
    d2d = [(D2D, (lambda me, k=k: 2 * k + 1 - me[2]), (lambda sender, k=k: k)) for k in range(N_CHIP)]
    return Exchange(parts, N_CHIP, d2d)


def chip_sums(parts, sib):
    c = _me()[2]
    return [slab_sum([(p, c, 2), (s, 0, 1)], N_CHIP, BF16) for p, s in zip(parts, sib)]


def scatter_ici(sums):
    ici = [(f, (lambda me, f=f: _chip(_flipped(me, f))), (lambda sender, i=i: i)) for i, f in enumerate(ICI_FLIPS)]
    return Exchange(sums, len(ICI_FLIPS), ici)


def device_sums(sums, got):
    x, y, _ = _me()
    return [slab_sum([(cs, 2 * x + y, 0)] + [(g, i, 0) for i in range(len(ICI_FLIPS))], 1, F32)[0]
            for cs, g in zip(sums, got)]


def all_gather(shards):
    blocks = gather_ici(shards).run("gather_ici")
    return gathered(gather_d2d(blocks).run("gather_d2d"))


def reduce_scatter(parts):
    sums = chip_sums(parts, scatter_d2d(parts).run("scatter_d2d"))
    return device_sums(sums, scatter_ici(sums).run("scatter_ici"))


def all_reduce_small(v):
    flips = D2D_FLIPS + ICI_FLIPS[:2]

    def body(v_ref, o_ref, got_ref, send_sems, recv_sems):
        me = _me()
        o_ref[...] = v_ref[...]
        for i, flip in enumerate(flips):
            cp = pltpu.make_async_remote_copy(
                src_ref=o_ref, dst_ref=got_ref.at[i], send_sem=send_sems.at[i], recv_sem=recv_sems.at[i],
                device_id=_flipped(me, flip), device_id_type=MESH)
            cp.start()
            cp.wait()
            o_ref[...] = o_ref[...] + got_ref[i]

    vm = pl.BlockSpec(memory_space=pltpu.VMEM)
    return pl.pallas_call(
        body, name="all_reduce_small", in_specs=[vm], out_specs=vm, out_shape=jax.ShapeDtypeStruct(v.shape, F32),
        scratch_shapes=[pltpu.VMEM((len(flips),) + v.shape, F32), pltpu.SemaphoreType.DMA((len(flips),)),
                        pltpu.SemaphoreType.DMA((len(flips),))],
    )(v)


def _perm_rows(wt):
    pad = jnp.zeros((D_IN_PAD - D_IN_PROJ, wt.shape[1]), wt.dtype)
    return jnp.concatenate([wt[:DT_LO], wt[DT_HI:], wt[DT_LO:DT_HI], pad], axis=0)


def _unperm_rows(dwt):
    n = D_IN_PROJ - (DT_HI - DT_LO)
    return jnp.concatenate([dwt[:DT_LO], dwt[n:D_IN_PROJ], dwt[DT_LO:n]], axis=0)


def _pad_lanes(v):
    return jnp.pad(v, ((0, 0), (0, LANE - v.shape[1])))[:, None]


def _block_diag(w):
    l, g, n, _ = w.shape
    out = jnp.zeros((l, g * n, g * n), w.dtype)
    for i in range(g):
        out = out.at[:, i * n:(i + 1) * n, i * n:(i + 1) * n].set(w[:, i])
    return out


def _pack(groups):
    flat = []
    for grp in groups:
        parts = [a.reshape(-1) for a in (grp if isinstance(grp, (list, tuple)) else [grp])]
        n = sum(p.shape[0] for p in parts)
        if -n % LANE:
            parts.append(jnp.zeros((-n % LANE,), parts[0].dtype))
        flat += parts
    return jnp.concatenate(flat).reshape(-1, LANE)


def _unpack(buf, shapes):
    out = []
    lo = 0
    buf = buf.reshape(-1)
    for shp in shapes:
        n = 1
        for k in shp:
            n *= k
        out.append(buf[lo:lo + n].reshape(shp))
        lo += n + (-n % LANE)
    return out


def small_params(w, conv_w_full):
    return dict(
        n1w=w["norm1_w"][:, None], cw=jnp.pad(conv_w_full, ((0, 0), (0, 8 - CONV_WIDTH), (0, 0))),
        cb=w["conv_b"][:, None], dtb=_pad_lanes(w["dt_bias"]), alog=_pad_lanes(w["a_log"]), dsk=_pad_lanes(w["d_skip"]),
        snw=w["ssd_norm_w"][:, None], wblk=_block_diag(w["pool_w"]), pb=w["pool_b"].reshape(-1, 1, POOL_WIDTH),
        ps=w["pool_scale"][:, None], n2w=w["norm2_w"][:, None])


BIG = ("w_in", "w_out", "w_gate", "w_up", "w_down")


def layer_params(small, big, l):
    p = {k: v[l] for k, v in small.items()}
    p.update(w_in=_perm_rows(big[0]), w_out=big[1], wg=big[2], wu=big[3], wd=big[4])
    return p


def _layer_fwd(x, p, next_shards=None):
    z, xbc, qkv, pp, dtr, h1 = inproj_fwd(x, p["n1w"], p["w_in"])
    u = conv_fwd(xbc, p["cw"], p["cb"])
    y_ssd, st = ssd_fwd(u, z, dtr, p["dtb"], p["alog"], p["dsk"], p["snw"])
    (o, tot), blocks = sb_fwd(qkv, gather_ici(next_shards) if next_shards is not None else None)
    yp = pool_fwd(pp, p["wblk"], p["pb"], p["ps"])
    ycat = jnp.concatenate([y_ssd, o, yp], axis=1)
    x_mid = outproj_fwd(ycat, p["w_out"], x)
    (x_out, g, uu), blocks = ffn_fwd(x_mid, p["n2w"], p["wg"], p["wu"], p["wd"],
                                     gather_d2d(blocks) if blocks is not None else None)
    sv = dict(x=x, z=z, xbc=xbc, qkv=qkv, pp=pp, dtr=dtr, h1=h1, u=u, st=st, tot=tot, ycat=ycat, x_mid=x_mid, g=g, uu=uu)
    return x_out, sv, gathered(blocks) if blocks is not None else None


def _layer_bwd(dxo, sv, p, pending=None):
    (dx_mid, dn2w, a, dg, du, h2), sib = ffn_bwd(dxo, sv["x_mid"], sv["g"], sv["uu"], p["n2w"], p["wg"], p["wu"],
                                                 p["wd"], scatter_d2d(pending) if pending is not None else None)
    sums = chip_sums(pending, sib) if pending is not None else None
    gr = dict(norm2_w=dn2w[0], w_down=mm_tn(a, dxo), w_gate=mm_tn(dg, h2), w_up=mm_tn(du, h2))
    dycat = outproj_bwd(dx_mid, p["w_out"])
    gr["w_out"] = mm_tn(sv["ycat"], dx_mid)
    dp, dwblk, dpb, dps = pool_bwd(sv["pp"], p["wblk"], p["pb"], p["ps"], dycat,
                                   (SSD_WIDTH + SB_WIDTH) // POOL_WIDTH)
    n = POOL_GROUP_DIM
    gr["pool_w"] = jnp.stack([dwblk[i * n:(i + 1) * n, i * n:(i + 1) * n] for i in range(len(POOL_WINDOWS))])
    gr["pool_b"] = dpb.reshape(len(POOL_WINDOWS), n)
    gr["pool_scale"] = dps[0]
    (dq, dk, dv), got = sb_bwd(sv["qkv"], sv["tot"], dycat, SSD_WIDTH // LANE,
                               scatter_ici(sums) if sums is not None else None)
    done = device_sums(sums, got) if sums is not None else None
    du_, dz, ddtr, ddtb, dalog, ddsk, dsnw = ssd_bwd(sv["u"], sv["z"], sv["dtr"], sv["st"], dycat, p["dtb"], p["alog"],
                                                     p["dsk"], p["snw"])
    gr.update(dt_bias=ddtb[0, :SSD_HEADS], a_log=dalog[0, :SSD_HEADS], d_skip=ddsk[0, :SSD_HEADS], ssd_norm_w=dsnw[0])
    dxbc, dcw, dcb = conv_bwd(sv["xbc"], p["cw"], p["cb"], du_)
    gr.update(conv_w=dcw[:CONV_WIDTH], conv_b=dcb[0])
    dproj = jnp.concatenate([dz, dxbc, dq, dk, dv, dp, ddtr], axis=1)
    dx, dn1w = inproj_bwd(dproj, p["w_in"], sv["x"], p["n1w"], dx_mid)
    gr.update(norm1_w=dn1w[0], w_in=_unperm_rows(mm_tn(dproj, sv["h1"])))
    return dx, gr, done


def local_step(x, tgt, params, final_w):
    saved = []
    for p in params:
        x, sv, _ = _layer_fwd(x, p)
        saved.append(sv)
    loss, dx, dfw = head_loss(x, final_w[None], tgt)
    grads = []
    for p, sv in zip(reversed(params), reversed(saved)):
        dx, gr, _ = _layer_bwd(dx, sv, p)
        grads.append(gr)
    grads.reverse()
    return loss, dx, dfw[0], grads


WEIGHTS = ("norm1_w", "w_in", "conv_w", "conv_b", "dt_bias", "a_log", "d_skip", "ssd_norm_w", "pool_w", "pool_b",
           "pool_scale", "w_out", "norm2_w", "w_gate", "w_up", "w_down", "final_norm_w")
COL_SHARDED = ("w_in", "w_gate", "w_up")
ROW_SHARDED = ("w_out", "w_down")
SMALL = tuple(k for k in WEIGHTS if k not in COL_SHARDED + ROW_SHARDED)


def kernel(x, norm1_w, w_in, conv_w, conv_b, dt_bias, a_log, d_skip, ssd_norm_w, pool_w, pool_b, pool_scale, w_out, norm2_w, w_gate, w_up, w_down, final_norm_w, loss_target, m_norm1_w, m_w_in, m_conv_w, m_conv_b, m_dt_bias, m_a_log, m_d_skip, m_ssd_norm_w, m_pool_w, m_pool_b, m_pool_scale, m_w_out, m_norm2_w, m_w_gate, m_w_up, m_w_down, m_final_norm_w, v_norm1_w, v_w_in, v_conv_w, v_conv_b, v_dt_bias, v_a_log, v_d_skip, v_ssd_norm_w, v_pool_w, v_pool_b, v_pool_scale, v_w_out, v_norm2_w, v_w_gate, v_w_up, v_w_down, v_final_norm_w):
    args = dict(locals())
    w = {k: args[k] for k in WEIGHTS}
    m = {k: args["m_" + k] for k in WEIGHTS}
    v = {k: args["v_" + k] for k in WEIGHTS}
    depth = w_in.shape[0]
    dev = _dev(_me())
    n_cw = conv_w.shape[-1]

    shards = {k: (jnp.swapaxes(w[k], 1, 2) if k in COL_SHARDED else w[k]).astype(BF16) for k in BIG}
    conv_w_full = all_gather([jnp.swapaxes(conv_w, 0, 2).reshape(n_cw, -1)])[0]
    conv_w_full = jnp.swapaxes(conv_w_full.reshape(N_DEV * n_cw, CONV_WIDTH, depth), 0, 2)
    small = small_params(w, conv_w_full)
    xs = x[0]
    params, saved = [], []
    big = all_gather([shards[k][0] for k in BIG])
    for l in range(depth):
        params.append(layer_params(small, big, l))
        xs, sv, big = _layer_fwd(xs, params[-1], [shards[k][l + 1] for k in BIG] if l + 1 < depth else None)
        saved.append(sv)
    loss, dx, dfw = head_loss(xs, final_norm_w[None], loss_target[0])
    layer_grads = [None] * depth
    big_grads = [None] * depth
    pending = None
    for l in reversed(range(depth)):
        dx, layer_grads[l], done = _layer_bwd(dx, saved[l], params[l], pending)
        if pending is not None:
            big_grads[l + 1] = done
        pending = [layer_grads[l][k].reshape(N_DEV, -1, D_MODEL) for k in BIG]
    big_grads[0] = reduce_scatter(pending)

    grads = {}
    for i, k in enumerate(BIG):
        stacked = jnp.stack([big_grads[l][i] for l in range(depth)])
        grads[k] = jnp.swapaxes(stacked, 1, 2) if k in COL_SHARDED else stacked
    layered = [k for k in SMALL if k != "final_norm_w"]
    small_shapes = [(1, LANE)] + [(depth,) + layer_grads[0][k].shape for k in layered] + [dfw[0].shape]
    packed = _pack([loss] + [[layer_grads[l][k] for l in range(depth)] for k in layered] + [dfw[0]])
    summed = _unpack(all_reduce_small(packed), small_shapes)
    loss = summed[0][0, 0]
    grads.update(zip(layered + ["final_norm_w"], summed[1:]))
    grads["conv_w"] = lax.dynamic_slice_in_dim(grads["conv_w"], dev * n_cw, n_cw, axis=2)

    delta, new_m, new_v = {}, {}, {}
    for k in COL_SHARDED + ROW_SHARDED:
        delta[k], new_m[k], new_v[k] = adamw(w[k], grads[k], m[k], v[k])
    shapes = [w[k].shape for k in SMALL]
    packed = [_pack([t[k] for k in SMALL])[None] for t in (w, grads, m, v)]
    for dst, buf in zip((delta, new_m, new_v), adamw(*packed)):
        dst.update(zip(SMALL, _unpack(buf, shapes)))
    return (loss, dx[None], *[grads[k] for k in WEIGHTS], *[delta[k] for k in WEIGHTS],
            *[new_m[k] for k in WEIGHTS], *[new_v[k] for k in WEIGHTS])
```

```python
import functools

import jax
import jax.numpy as jnp
from jax import lax
from jax.experimental import pallas as pl
from jax.experimental.pallas import tpu as pltpu

F32 = jnp.float32
BF16 = jnp.bfloat16
HIGHEST = lax.Precision.HIGHEST
MESH = pl.DeviceIdType.MESH

EPS = 1e-6
D_MODEL = 1024
SSD_WIDTH = 512
SSD_HEADS = 8
HEAD_DIM = 64
D_STATE = 128
CONV_WIDTH = 4
CONV_DIM = 1024
SB_WIDTH = 256
POOL_WIDTH = 256
POOL_WINDOWS = (2, 4, 8, 16)
D_IN_PROJ = 2568
D_FF = 2816
N_DEV = 8
DEPTH = 4
SEG = (512, 1024, 768, 256, 128)
D_IN_PAD = sum(SEG)
DT_LO, DT_HI = 1536, 1544

LANE = 128
BLK = 128
ROW_TILE = 256
VMEM_LIMIT = 56 * 2**20

ADAM_LR, ADAM_B1, ADAM_B2, ADAM_EPS, ADAM_WD, ADAM_STEP = 0.001, 0.9, 0.999, 1e-08, 0.01, 10


def _params(n_axes=1, vmem=None):
    return pltpu.CompilerParams(dimension_semantics=("arbitrary",) * n_axes, vmem_limit_bytes=vmem)


def _dot(a, b, dims, exact=False):
    if exact:
        return lax.dot_general(a.astype(F32), b.astype(F32), (dims, ((), ())), precision=HIGHEST,
                               preferred_element_type=F32)
    return lax.dot_general(a.astype(BF16), b.astype(BF16), (dims, ((), ())), preferred_element_type=F32)


def dot_nn(a, b, exact=False):
    return _dot(a, b, ((1,), (0,)), exact)


def dot_nt(a, b, exact=False):
    return _dot(a, b, ((1,), (1,)), exact)


def dot_tn(a, b, exact=False):
    return _dot(a, b, ((0,), (0,)), exact)


def _iota(shape, axis):
    return lax.broadcasted_iota(jnp.int32, shape, axis)


def _lane_col(x, h):
    return jnp.sum(jnp.where(_iota(x.shape, 1) == h, x, 0.0), axis=1, keepdims=True)


def _sub_row(x, h):
    return jnp.sum(jnp.where(_iota(x.shape, 0) == h, x, 0.0), axis=0, keepdims=True)


def _sigmoid(x):
    return 1.0 / (1.0 + jnp.exp(-x))


def _rms_fwd(x, w):
    r = lax.rsqrt(jnp.mean(x * x, axis=-1, keepdims=True) + EPS)
    return x * r * w


def _rms_bwd(x, w, dy):
    r = lax.rsqrt(jnp.mean(x * x, axis=-1, keepdims=True) + EPS)
    xh = x * r
    dxh = dy * w
    dx = r * (dxh - xh * jnp.mean(dxh * xh, axis=-1, keepdims=True))
    return dx, jnp.sum(dy * xh, axis=0, keepdims=True)


def _acc(ref, first, val):
    @pl.when(first)
    def _():
        ref[...] = val

    @pl.when(jnp.logical_not(first))
    def _():
        ref[...] += val


def _row_spec(tm, n):
    return pl.BlockSpec((tm, n), lambda i: (i, 0))


def _full_spec(shape):
    return pl.BlockSpec(shape, lambda *_: (0,) * len(shape))


def inproj_fwd(x, nw, w):
    s, d = x.shape
    tm = min(ROW_TILE, s)

    def body(x_ref, nw_ref, w_ref, z_ref, xbc_ref, qkv_ref, p_ref, dt_ref, h_ref):
        h = _rms_fwd(x_ref[...], nw_ref[...]).astype(BF16)
        h_ref[...] = h
        lo = 0
        for ref, n in zip((z_ref, xbc_ref, qkv_ref, p_ref, dt_ref), SEG):
            ref[...] = dot_nt(h, w_ref[lo:lo + n, :])
            lo += n

    return pl.pallas_call(
        body, name="inproj_fwd", grid=(s // tm,),
        in_specs=[_row_spec(tm, d), _full_spec((1, d)), _full_spec(w.shape)],
        out_specs=[_row_spec(tm, n) for n in SEG] + [_row_spec(tm, d)],
        out_shape=[jax.ShapeDtypeStruct((s, n), F32) for n in SEG] + [jax.ShapeDtypeStruct((s, d), BF16)],
        compiler_params=_params(1, VMEM_LIMIT),
    )(x, nw, w)


def inproj_bwd(dproj, w, x, nw, dres):
    s, d = x.shape
    tm = min(ROW_TILE, s)

    def body(dp_ref, w_ref, x_ref, nw_ref, dres_ref, dx_ref, dnw_ref):
        dh = dot_nn(dp_ref[...], w_ref[...])
        dx, dnw = _rms_bwd(x_ref[...], nw_ref[...], dh)
        dx_ref[...] = dres_ref[...] + dx
        _acc(dnw_ref, pl.program_id(0) == 0, dnw)

    return pl.pallas_call(
        body, name="inproj_bwd", grid=(s // tm,),
        in_specs=[_row_spec(tm, dproj.shape[1]), _full_spec(w.shape), _row_spec(tm, d), _full_spec((1, d)),
                  _row_spec(tm, d)],
        out_specs=[_row_spec(tm, d), _full_spec((1, d))],
        out_shape=[jax.ShapeDtypeStruct((s, d), F32), jax.ShapeDtypeStruct((1, d), F32)],
        compiler_params=_params(1, VMEM_LIMIT),
    )(dproj, w, x, nw, dres)


def outproj_fwd(y, w, res):
    s, d = res.shape
    tm = min(ROW_TILE, s)

    def body(y_ref, w_ref, r_ref, o_ref):
        o_ref[...] = r_ref[...] + dot_nn(y_ref[...], w_ref[...])

    return pl.pallas_call(
        body, name="outproj_fwd", grid=(s // tm,),
        in_specs=[_row_spec(tm, y.shape[1]), _full_spec(w.shape), _row_spec(tm, d)],
        out_specs=_row_spec(tm, d), out_shape=jax.ShapeDtypeStruct((s, d), F32),
        compiler_params=_params(1, VMEM_LIMIT),
    )(y, w, res)


def outproj_bwd(dx, w):
    s, d = dx.shape
    tm = min(ROW_TILE, s)

    def body(dx_ref, w_ref, o_ref):
        o_ref[...] = dot_nt(dx_ref[...], w_ref[...])

    return pl.pallas_call(
        body, name="outproj_bwd", grid=(s // tm,),
        in_specs=[_row_spec(tm, d), _full_spec(w.shape)],
        out_specs=_row_spec(tm, w.shape[0]), out_shape=jax.ShapeDtypeStruct((s, w.shape[0]), F32),
        compiler_params=_params(1, VMEM_LIMIT),
    )(dx, w)


def ffn_fwd(x, nw, wg, wu, wd, side=None):
    s, d = x.shape
    f = wg.shape[0]
    tm = min(ROW_TILE, s)

    def body(x_ref, nw_ref, wg_ref, wu_ref, wd_ref, o_ref, g_ref, u_ref):
        xv = x_ref[...]
        h = _rms_fwd(xv, nw_ref[...]).astype(BF16)
        g = dot_nt(h, wg_ref[...])
        u = dot_nt(h, wu_ref[...])
        g_ref[...] = g.astype(BF16)
        u_ref[...] = u.astype(BF16)
        o_ref[...] = xv + dot_nn(g * _sigmoid(g) * u, wd_ref[...])

    return hosted_call(
        body, side, name="ffn_fwd", grid=(s // tm,),
        in_specs=[_row_spec(tm, d), _full_spec((1, d)), _full_spec(wg.shape), _full_spec(wu.shape),
                  _full_spec(wd.shape)],
        out_specs=[_row_spec(tm, d), _row_spec(tm, f), _row_spec(tm, f)],
        out_shape=[jax.ShapeDtypeStruct((s, d), F32), jax.ShapeDtypeStruct((s, f), BF16),
                   jax.ShapeDtypeStruct((s, f), BF16)],
        scratch_shapes=[], compiler_params=_params(1, VMEM_LIMIT), operands=(x, nw, wg, wu, wd))


def ffn_bwd(dxo, x, g, u, nw, wg, wu, wd, side=None):
    s, d = x.shape
    f = wg.shape[0]
    tm = min(ROW_TILE, s)

    def body(dxo_ref, x_ref, g_ref, u_ref, nw_ref, wg_ref, wu_ref, wd_ref, dx_ref, dnw_ref, a_ref, dg_ref,
             du_ref, h_ref):
        dxo_v = dxo_ref[...]
        xv = x_ref[...]
        da = dot_nt(dxo_v, wd_ref[...])
        gv = g_ref[...].astype(F32)
        uv = u_ref[...].astype(F32)
        sg = _sigmoid(gv)
        sl = gv * sg
        a_ref[...] = (sl * uv).astype(BF16)
        dg = (da * uv * (sg * (1.0 + gv * (1.0 - sg)))).astype(BF16)
        du = (da * sl).astype(BF16)
        dg_ref[...] = dg
        du_ref[...] = du
        dh = dot_nn(dg, wg_ref[...]) + dot_nn(du, wu_ref[...])
        h_ref[...] = _rms_fwd(xv, nw_ref[...]).astype(BF16)
        dx, dnw = _rms_bwd(xv, nw_ref[...], dh)
        dx_ref[...] = dxo_v + dx
        _acc(dnw_ref, pl.program_id(0) == 0, dnw)

    return hosted_call(
        body, side, name="ffn_bwd", grid=(s // tm,),
        in_specs=[_row_spec(tm, d), _row_spec(tm, d), _row_spec(tm, f), _row_spec(tm, f), _full_spec((1, d)),
                  _full_spec(wg.shape), _full_spec(wu.shape), _full_spec(wd.shape)],
        out_specs=[_row_spec(tm, d), _full_spec((1, d)), _row_spec(tm, f), _row_spec(tm, f), _row_spec(tm, f),
                   _row_spec(tm, d)],
        out_shape=[jax.ShapeDtypeStruct((s, d), F32), jax.ShapeDtypeStruct((1, d), F32),
                   jax.ShapeDtypeStruct((s, f), BF16), jax.ShapeDtypeStruct((s, f), BF16),
                   jax.ShapeDtypeStruct((s, f), BF16), jax.ShapeDtypeStruct((s, d), BF16)],
        scratch_shapes=[], compiler_params=_params(1, VMEM_LIMIT), operands=(dxo, x, g, u, nw, wg, wu, wd))


def _tile(n, cap=256):
    best = LANE
    for t in range(LANE, cap + 1, LANE):
        if n % t == 0:
            best = t
    return best


def mm_tn(a, b):
    s, k = a.shape
    n = b.shape[1]
    tk = _tile(k)

    def body(a_ref, b_ref, o_ref):
        o_ref[...] = dot_nn(a_ref[...].astype(BF16).T, b_ref[...]).astype(BF16)

    return pl.pallas_call(
        body, name="mm_tn", grid=(k // tk,),
        in_specs=[pl.BlockSpec((s, tk), lambda i: (0, i)), _full_spec((s, n))],
        out_specs=pl.BlockSpec((tk, n), lambda i: (i, 0)), out_shape=jax.ShapeDtypeStruct((k, n), BF16),
        compiler_params=_params(1, VMEM_LIMIT),
    )(a, b)


def head_loss(x, fw, tgt):
    s, d = x.shape
    tm = min(ROW_TILE, s)

    def body(x_ref, fw_ref, t_ref, loss_ref, dx_ref, dfw_ref):
        xv = x_ref[...]
        err = _rms_fwd(xv, fw_ref[...]) - t_ref[...]
        part = jnp.zeros((1, LANE), F32) + 0.5 * jnp.sum(err * err) / d
        dx, dfw = _rms_bwd(xv, fw_ref[...], err / d)
        dx_ref[...] = dx
        first = pl.program_id(0) == 0
        _acc(loss_ref, first, part)
        _acc(dfw_ref, first, dfw)

    return pl.pallas_call(
        body, name="head_loss", grid=(s // tm,),
        in_specs=[_row_spec(tm, d), _full_spec((1, d)), _row_spec(tm, d)],
        out_specs=[_full_spec((1, LANE)), _row_spec(tm, d), _full_spec((1, d))],
        out_shape=[jax.ShapeDtypeStruct((1, LANE), F32), jax.ShapeDtypeStruct((s, d), F32),
                   jax.ShapeDtypeStruct((1, d), F32)],
        compiler_params=_params(1),
    )(x, fw, tgt)


def _conv_pre(ext, cw_ref, cb_ref):
    shifted = [pltpu.roll(ext, CONV_WIDTH - 1 - i, 0)[BLK:] if i < CONV_WIDTH - 1 else ext[BLK:]
               for i in range(CONV_WIDTH)]
    acc = cb_ref[...] + sum(cw_ref[i:i + 1, :] * shifted[i] for i in range(CONV_WIDTH))
    return acc, shifted


def conv_fwd(xbc, cw, cb):
    s, n = xbc.shape

    def body(cur_ref, prev_ref, cw_ref, cb_ref, o_ref):
        prev = jnp.where(pl.program_id(0) > 0, prev_ref[...], 0.0)
        acc, _ = _conv_pre(jnp.concatenate([prev, cur_ref[...]], axis=0), cw_ref, cb_ref)
        o_ref[...] = acc * _sigmoid(acc)

    return pl.pallas_call(
        body, name="conv_fwd", grid=(s // BLK,),
        in_specs=[pl.BlockSpec((BLK, n), lambda c: (c, 0)), pl.BlockSpec((BLK, n), lambda c: (jnp.maximum(c - 1, 0), 0)),
                  _full_spec(cw.shape), _full_spec((1, n))],
        out_specs=pl.BlockSpec((BLK, n), lambda c: (c, 0)), out_shape=jax.ShapeDtypeStruct((s, n), F32),
        compiler_params=_params(1),
    )(xbc, xbc, cw, cb)


def conv_bwd(xbc, cw, cb, du):
    s, n = xbc.shape
    nb = s // BLK

    def body(cur_ref, prev_ref, cw_ref, cb_ref, du_ref, dx_ref, dcw_ref, dcb_ref, nxt_ref):
        i = pl.program_id(0)
        c = nb - 1 - i
        prev = jnp.where(c > 0, prev_ref[...], 0.0)
        acc, shifted = _conv_pre(jnp.concatenate([prev, cur_ref[...]], axis=0), cw_ref, cb_ref)
        sg = _sigmoid(acc)
        dacc = du_ref[...] * (sg * (1.0 + acc * (1.0 - sg)))

        @pl.when(i == 0)
        def _():
            nxt_ref[...] = jnp.zeros_like(nxt_ref)
            dcw_ref[...] = jnp.zeros_like(dcw_ref)
            dcb_ref[...] = jnp.zeros_like(dcb_ref)

        dcb_ref[...] += jnp.sum(dacc, axis=0, keepdims=True)
        for t in range(CONV_WIDTH):
            dcw_ref[t:t + 1, :] += jnp.sum(dacc * shifted[t], axis=0, keepdims=True)
        ext = jnp.concatenate([dacc, nxt_ref[...]], axis=0)
        dx = cw_ref[CONV_WIDTH - 1:CONV_WIDTH, :] * dacc
        for t in range(CONV_WIDTH - 1):
            dx += cw_ref[t:t + 1, :] * pltpu.roll(ext, 2 * BLK - (CONV_WIDTH - 1 - t), 0)[:BLK]
        dx_ref[...] = dx
        nxt_ref[...] = dacc

    rev = lambda i: (nb - 1 - i, 0)
    return pl.pallas_call(
        body, name="conv_bwd", grid=(nb,),
        in_specs=[pl.BlockSpec((BLK, n), rev), pl.BlockSpec((BLK, n), lambda i: (jnp.maximum(nb - 2 - i, 0), 0)),
                  _full_spec(cw.shape), _full_spec((1, n)), pl.BlockSpec((BLK, n), rev)],
        out_specs=[pl.BlockSpec((BLK, n), rev), _full_spec((8, n)), _full_spec((1, n))],
        out_shape=[jax.ShapeDtypeStruct((s, n), F32), jax.ShapeDtypeStruct((8, n), F32),
                   jax.ShapeDtypeStruct((1, n), F32)],
        scratch_shapes=[pltpu.VMEM((BLK, n), F32)],
        compiler_params=_params(1),
    )(xbc, xbc, cw, cb, du)


N_PAIR = SSD_HEADS // 2
B_LO = SSD_WIDTH
C_LO = SSD_WIDTH + 2 * D_STATE


def _softplus(x):
    return jnp.maximum(x, 0.0) + jnp.log(1.0 + jnp.exp(-jnp.abs(x)))


def _ssd_chunk(u_ref, dt_ref, dtb_ref, alog_ref):
    shape = (BLK, BLK)
    tri = _iota(shape, 1) <= _iota(shape, 0)
    pre = dt_ref[...] + dtb_ref[...]
    dt = _softplus(pre)
    a = -jnp.exp(alog_ref[...])
    acum = dot_nn(tri.astype(F32), dt * a, exact=True)
    acum_t = acum.T
    last = _sub_row(acum, BLK - 1)
    heads = []
    for h in range(SSD_HEADS):
        col = _lane_col(acum, h)
        seg = jnp.where(tri, col - _sub_row(acum_t, h), -1e30)
        heads.append(dict(col=col, dm=jnp.exp(seg), dt=_lane_col(dt, h), last=_lane_col(last, h)))
    return tri, pre, dt, a, heads


def _pair_mix(lo_mask, v0, v1):
    return jnp.where(lo_mask, v0, v1)


def ssd_fwd(u, z, dtr, dtb, alog, dsk, nw):
    s = u.shape[0]
    nc = s // BLK

    def body(u_ref, z_ref, dt_ref, dtb_ref, alog_ref, dsk_ref, nw_ref, y_ref, st_ref, s_ref):
        @pl.when(pl.program_id(0) == 0)
        def _():
            s_ref[...] = jnp.zeros_like(s_ref)

        _, _, _, _, heads = _ssd_chunk(u_ref, dt_ref, dtb_ref, alog_ref)
        lo_lane = _iota((BLK, LANE), 1) < HEAD_DIM
        lo_sub = _iota((BLK, LANE), 0) < HEAD_DIM
        ys = []
        for p in range(N_PAIR):
            g = p // 2
            h0, h1 = heads[2 * p], heads[2 * p + 1]
            bg = u_ref[:, B_LO + g * D_STATE:B_LO + (g + 1) * D_STATE]
            cg = u_ref[:, C_LO + g * D_STATE:C_LO + (g + 1) * D_STATE]
            xs = u_ref[:, p * LANE:(p + 1) * LANE]
            xp = xs * _pair_mix(lo_lane, h0["dt"], h1["dt"])
            gm = dot_nt(cg, bg)
            yd = _pair_mix(lo_lane, dot_nn(gm * h0["dm"], xp), dot_nn(gm * h1["dm"], xp))
            sp = s_ref[p]
            st_ref[0, p] = sp
            yo = _pair_mix(lo_lane, jnp.exp(h0["col"]), jnp.exp(h1["col"])) * dot_nt(cg, sp)
            dskp = _pair_mix(lo_lane, _lane_col(dsk_ref[...], 2 * p), _lane_col(dsk_ref[...], 2 * p + 1))
            ys.append(yd + yo + xs * dskp)
            wp = _pair_mix(lo_lane, jnp.exp(h0["last"] - h0["col"]), jnp.exp(h1["last"] - h1["col"]))
            el = _pair_mix(lo_sub, jnp.exp(h0["last"]), jnp.exp(h1["last"]))
            s_ref[p] = el * sp + dot_tn(wp * xp, bg)
        y = jnp.concatenate(ys, axis=1)
        zv = z_ref[...]
        y_ref[...] = _rms_fwd(y * zv * _sigmoid(zv), nw_ref[...])

    vec = _full_spec((1, LANE))
    return pl.pallas_call(
        body, name="ssd_fwd", grid=(nc,),
        in_specs=[_row_spec(BLK, CONV_DIM), _row_spec(BLK, SSD_WIDTH), _row_spec(BLK, LANE), vec, vec, vec,
                  _full_spec((1, SSD_WIDTH))],
        out_specs=[_row_spec(BLK, SSD_WIDTH), pl.BlockSpec((1, N_PAIR, LANE, D_STATE), lambda c: (c, 0, 0, 0))],
        out_shape=[jax.ShapeDtypeStruct((s, SSD_WIDTH), F32), jax.ShapeDtypeStruct((nc, N_PAIR, LANE, D_STATE), F32)],
        scratch_shapes=[pltpu.VMEM((N_PAIR, LANE, D_STATE), F32)],
        compiler_params=_params(1),
    )(u, z, dtr, dtb, alog, dsk, nw)


def ssd_bwd(u, z, dtr, st, dyo, dtb, alog, dsk, nw):
    s = u.shape[0]
    nc = s // BLK

    def body(u_ref, z_ref, dt_ref, st_ref, dyo_ref, dtb_ref, alog_ref, dsk_ref, nw_ref,
             du_ref, dz_ref, ddt_ref, ddtb_ref, dalog_ref, ddsk_ref, dnw_ref, ds_ref):
        first = pl.program_id(0) == 0

        @pl.when(first)
        def _():
            ds_ref[...] = jnp.zeros_like(ds_ref)

        tri, pre, dt, a, heads = _ssd_chunk(u_ref, dt_ref, dtb_ref, alog_ref)
        shape = (BLK, LANE)
        lane = _iota(shape, 1)
        lo_lane = lane < HEAD_DIM
        lo_sub = _iota(shape, 0) < HEAD_DIM
        pairs = []
        ys = []
        for p in range(N_PAIR):
            g = p // 2
            h0, h1 = heads[2 * p], heads[2 * p + 1]
            bg = u_ref[:, B_LO + g * D_STATE:B_LO + (g + 1) * D_STATE]
            cg = u_ref[:, C_LO + g * D_STATE:C_LO + (g + 1) * D_STATE]
            xs = u_ref[:, p * LANE:(p + 1) * LANE]
            dtp = _pair_mix(lo_lane, h0["dt"], h1["dt"])
            xp = xs * dtp
            gm = dot_nt(cg, bg)
            m0, m1 = gm * h0["dm"], gm * h1["dm"]
            sp = st_ref[0, p]
            eap = _pair_mix(lo_lane, jnp.exp(h0["col"]), jnp.exp(h1["col"]))
            yo = eap * dot_nt(cg, sp)
            dskp = _pair_mix(lo_lane, _lane_col(dsk_ref[...], 2 * p), _lane_col(dsk_ref[...], 2 * p + 1))
            ys.append(_pair_mix(lo_lane, dot_nn(m0, xp), dot_nn(m1, xp)) + yo + xs * dskp)
            pairs.append(dict(bg=bg, cg=cg, xs=xs, dtp=dtp, xp=xp, gm=gm, m=(m0, m1), sp=sp, eap=eap, yo=yo, dskp=dskp))
        y = jnp.concatenate(ys, axis=1)
        zv = z_ref[...]
        sz = _sigmoid(zv)
        gate = zv * sz
        dyg, dnw = _rms_bwd(y * gate, nw_ref[...], dyo_ref[...])
        _acc(dnw_ref, first, dnw)
        dy = dyg * gate
        dz_ref[...] = dyg * y * (sz * (1.0 + zv * (1.0 - sz)))

        zeros = jnp.zeros(shape, F32)
        dacum_col = zeros
        dacum_row = zeros
        ddt = zeros
        ddsk = jnp.zeros((1, LANE), F32)
        dlast = jnp.zeros((1, LANE), F32)
        head_row = _iota((1, LANE), 1)
        sub = _iota(shape, 0)
        db = [zeros, zeros]
        dc = [zeros, zeros]
        for p in range(N_PAIR):
            g = p // 2
            q = pairs[p]
            dyp = dy[:, p * LANE:(p + 1) * LANE]
            dsn = ds_ref[p]
            t = dyp * q["xs"]
            dxs = dyp * q["dskp"]
            dcs = dyp * q["eap"]
            dc[g] = dc[g] + dot_nn(dcs, q["sp"])
            dsp = dot_tn(dcs, q["cg"])
            dea = dyp * q["yo"]
            elp = _pair_mix(lo_sub, jnp.exp(heads[2 * p]["last"]), jnp.exp(heads[2 * p + 1]["last"]))
            dsp = dsp + elp * dsn
            dels = dsn * q["sp"] * elp
            wp = _pair_mix(lo_lane, jnp.exp(heads[2 * p]["last"] - heads[2 * p]["col"]),
                           jnp.exp(heads[2 * p + 1]["last"] - heads[2 * p + 1]["col"]))
            dv = dot_nt(q["bg"], dsn)
            db[g] = db[g] + dot_nn(wp * q["xp"], dsn)
            dxp = dv * wp
            dwv = dv * q["xp"] * wp
            dgm = zeros
            for k in range(2):
                h = 2 * p + k
                mine = lo_lane if k == 0 else jnp.logical_not(lo_lane)
                mine_sub = lo_sub if k == 0 else jnp.logical_not(lo_sub)
                dyh = jnp.where(mine, dyp, 0.0)
                dm = dot_nt(dyh, q["xp"])
                dxp = dxp + dot_tn(q["m"][k], dyh)
                dgm = dgm + dm * heads[h]["dm"]
                e = dm * q["m"][k]
                onehot = lane == h
                dw_col = jnp.sum(jnp.where(mine, dwv, 0.0), axis=1, keepdims=True)
                col = (jnp.sum(e, axis=1, keepdims=True) + jnp.sum(jnp.where(mine, dea, 0.0), axis=1, keepdims=True)
                       - dw_col)
                dacum_col = dacum_col + jnp.where(onehot, col, 0.0)
                dacum_row = dacum_row - jnp.where(sub == h, jnp.sum(e, axis=0, keepdims=True), 0.0)
                dl = jnp.sum(dw_col) + jnp.sum(jnp.where(mine_sub, dels, 0.0))
                dlast = dlast + jnp.where(head_row == h, dl, 0.0)
                ddsk = ddsk + jnp.where(head_row == h, jnp.sum(jnp.where(mine, t, 0.0)), 0.0)
            dc[g] = dc[g] + dot_nn(dgm, q["bg"])
            db[g] = db[g] + dot_tn(dgm, q["cg"])
            dxs = dxs + dxp * q["dtp"]
            tt = dxp * q["xs"]
            for k in range(2):
                mine = lo_lane if k == 0 else jnp.logical_not(lo_lane)
                ddt = ddt + jnp.where(lane == 2 * p + k, jnp.sum(jnp.where(mine, tt, 0.0), axis=1, keepdims=True), 0.0)
            du_ref[:, p * LANE:(p + 1) * LANE] = dxs
            ds_ref[p] = dsp
        for g in range(2):
            du_ref[:, B_LO + g * D_STATE:B_LO + (g + 1) * D_STATE] = db[g]
            du_ref[:, C_LO + g * D_STATE:C_LO + (g + 1) * D_STATE] = dc[g]
        dacum = dacum_col + dacum_row.T + jnp.where(sub == BLK - 1, dlast, 0.0)
        dda = dot_tn(tri.astype(F32), dacum, exact=True)
        ddt = ddt + dda * a
        _acc(dalog_ref, first, jnp.sum(dda * dt, axis=0, keepdims=True) * a)
        dpre = ddt * _sigmoid(pre)
        ddt_ref[...] = dpre
        _acc(ddtb_ref, first, jnp.sum(dpre, axis=0, keepdims=True))
        _acc(ddsk_ref, first, ddsk)

    rev = lambda i: (nc - 1 - i, 0)
    vec = _full_spec((1, LANE))
    rows = lambda n: pl.BlockSpec((BLK, n), rev)
    return pl.pallas_call(
        body, name="ssd_bwd", grid=(nc,),
        in_specs=[rows(CONV_DIM), rows(SSD_WIDTH), rows(LANE),
                  pl.BlockSpec((1, N_PAIR, LANE, D_STATE), lambda i: (nc - 1 - i, 0, 0, 0)), rows(SSD_WIDTH),
                  vec, vec, vec, _full_spec((1, SSD_WIDTH))],
        out_specs=[rows(CONV_DIM), rows(SSD_WIDTH), rows(LANE), vec, vec, vec, _full_spec((1, SSD_WIDTH))],
        out_shape=[jax.ShapeDtypeStruct((s, CONV_DIM), F32), jax.ShapeDtypeStruct((s, SSD_WIDTH), F32),
                   jax.ShapeDtypeStruct((s, LANE), F32)] + [jax.ShapeDtypeStruct((1, LANE), F32)] * 3
        + [jax.ShapeDtypeStruct((1, SSD_WIDTH), F32)],
        scratch_shapes=[pltpu.VMEM((N_PAIR, LANE, D_STATE), F32)],
        compiler_params=_params(1),
    )(u, z, dtr, st, dyo, dtb, alog, dsk, nw)


SB_PAIRS = SB_WIDTH // LANE
SB_SCALE = HEAD_DIM ** -0.5


SB_TQ = 256


def _sb_tq(s):
    return min(SB_TQ, s)


def _sb_stack(x):
    lo_lane = _iota(x.shape, 1) < HEAD_DIM
    return jnp.concatenate([jnp.where(lo_lane, x, 0.0), jnp.where(lo_lane, 0.0, x)], axis=0)


def _sb_unstack(x2):
    tq = x2.shape[0] // 2
    lo_lane = _iota((tq, LANE), 1) < HEAD_DIM
    return jnp.where(lo_lane, x2[:tq], x2[tq:])


def _sb_logits(q2, kj, row0, col0, masked):
    shape = (q2.shape[0], BLK)
    tq = shape[0] // 2
    z = dot_nt(q2, kj)
    t = jnp.log(1.0 + jnp.exp(-jnp.abs(z)))
    ls = jnp.minimum(z, 0.0) - t
    lk = jnp.minimum(-z, 0.0) - t
    if not masked:
        return None, ls, lk
    row = _iota(shape, 0)
    valid = (col0 + _iota(shape, 1)) < (row0 + jnp.where(row < tq, row, row - tq))
    return valid, ls, jnp.where(valid, lk, 0.0)


def _sb_where(valid, x):
    return x if valid is None else jnp.where(valid, x, 0.0)


def _sums(x, mask2, parts):
    acc = None
    rest = x
    for _ in range(parts):
        term = rest.astype(BF16)
        rest = rest - term.astype(F32)
        d = lax.dot_general(term, mask2, (((1,), (0,)), ((), ())), preferred_element_type=F32)
        acc = d if acc is None else acc + d
    return acc[:, :BLK], acc[:, BLK:]


def _mask2(cond):
    return jnp.concatenate([cond.astype(BF16), jnp.ones(cond.shape, BF16)], axis=1)


def _sb_specs(s):
    tq = _sb_tq(s)
    qspec = pl.BlockSpec((tq, LANE), lambda p, i: (i, p))
    kspec = pl.BlockSpec((s, LANE), lambda p, i: (0, SB_PAIRS + p))
    vspec = pl.BlockSpec((s, LANE), lambda p, i: (0, 2 * SB_PAIRS + p))
    return qspec, kspec, vspec


def sb_fwd(qkv, side=None):
    s = qkv.shape[0]
    tq = _sb_tq(s)
    kpq = tq // BLK

    def body(q_ref, k_ref, v_ref, o_ref, t_ref, acc_ref):
        qi = pl.program_id(1)
        q2 = _sb_stack(q_ref[...] * SB_SCALE).astype(BF16)
        later = _mask2(_iota((BLK, BLK), 0) > _iota((BLK, BLK), 1))
        acc_ref[...] = jnp.zeros_like(acc_ref)

        def step(j, r, masked):
            rows = pl.ds(pl.multiple_of(j * BLK, BLK), BLK)
            valid, ls, lk = _sb_logits(q2, k_ref[rows, :], qi * tq, j * BLK, masked)
            after, total = _sums(lk, later, 2)
            w = _sb_where(valid, jnp.exp(ls + r + after))
            acc_ref[...] += dot_nn(w, v_ref[rows, :])
            return r + total

        r = jnp.zeros((2 * tq, LANE), F32)
        for d in reversed(range(kpq)):
            r = step(kpq * qi + d, r, True)

        def tile(g, r):
            for d in reversed(range(kpq)):
                r = step(kpq * (qi - 1 - g) + d, r, False)
            return r

        r = lax.fori_loop(0, qi, tile, r)
        o_ref[...] = _sb_unstack(acc_ref[...])
        t_ref[...] = jnp.concatenate([r[:tq], r[tq:]], axis=1)

    return hosted_call(
        body, side, name="sb_fwd", grid=(SB_PAIRS, s // tq),
        in_specs=list(_sb_specs(s)),
        out_specs=[pl.BlockSpec((tq, LANE), lambda p, i: (i, p)), pl.BlockSpec((tq, 2 * LANE), lambda p, i: (i, p))],
        out_shape=[jax.ShapeDtypeStruct((s, SB_WIDTH), F32), jax.ShapeDtypeStruct((s, 2 * SB_WIDTH), F32)],
        scratch_shapes=[pltpu.VMEM((2 * tq, LANE), F32)],
        compiler_params=_params(2), operands=(qkv, qkv, qkv))


def sb_bwd(qkv, tot, do, do_col=0, side=None):
    s = qkv.shape[0]
    tq = _sb_tq(s)
    kpq = tq // BLK

    def body(q_ref, k_ref, v_ref, t_ref, do_ref, dq_ref, dk_ref, dv_ref, acc_ref):
        qi = pl.program_id(1)
        q2 = _sb_stack(q_ref[...] * SB_SCALE).astype(BF16)
        do2 = _sb_stack(do_ref[...]).astype(BF16)
        tot2 = jnp.concatenate([t_ref[:, :LANE], t_ref[:, LANE:]], axis=0)
        sq = (BLK, BLK)
        later = _mask2(_iota(sq, 0) > _iota(sq, 1))
        before = _mask2(_iota(sq, 0) < _iota(sq, 1))
        acc_ref[...] = jnp.zeros_like(acc_ref)

        @pl.when(qi == 0)
        def _():
            dk_ref[...] = jnp.zeros_like(dk_ref)
            dv_ref[...] = jnp.zeros_like(dv_ref)

        def step(j, carry, masked):
            rc, fc = carry
            rows = pl.ds(pl.multiple_of(j * BLK, BLK), BLK)
            kj = k_ref[rows, :]
            vj = v_ref[rows, :]
            valid, ls, lk = _sb_logits(q2, kj, qi * tq, j * BLK, masked)
            after, total = _sums(lk, later, 2)
            rc = rc - total
            w = _sb_where(valid, jnp.exp(ls + rc + after))
            e = w * dot_nt(do2, vj)
            f_in, f_tot = _sums(e, before, 2)
            sg = jnp.exp(ls)
            dz = _sb_where(valid, e * (1.0 - sg) - (fc + f_in) * sg)
            acc_ref[...] += dot_nn(dz, kj)
            dk_ref[rows, :] += dot_tn(dz, q2)
            dv_ref[rows, :] += dot_tn(w, do2)
            return rc, fc + f_tot

        def tile(g, carry):
            for d in range(kpq):
                carry = step(kpq * g + d, carry, False)
            return carry

        carry = lax.fori_loop(0, qi, tile, (tot2, jnp.zeros((2 * tq, LANE), F32)))
        for d in range(kpq):
            carry = step(kpq * qi + d, carry, True)
        dq_ref[...] = SB_SCALE * _sb_unstack(acc_ref[...])

    qspec, kspec, vspec = _sb_specs(s)
    blk = pl.BlockSpec((tq, LANE), lambda p, i: (i, p))
    acc = pl.BlockSpec((s, LANE), lambda p, i: (0, p))
    return hosted_call(
        body, side, name="sb_bwd", grid=(SB_PAIRS, s // tq),
        in_specs=[qspec, kspec, vspec, pl.BlockSpec((tq, 2 * LANE), lambda p, i: (i, p)),
                  pl.BlockSpec((tq, LANE), lambda p, i: (i, do_col + p))],
        out_specs=[blk, acc, acc],
        out_shape=[jax.ShapeDtypeStruct((s, SB_WIDTH), F32)] * 3,
        scratch_shapes=[pltpu.VMEM((2 * tq, LANE), F32)],
        compiler_params=_params(2), operands=(qkv, qkv, qkv, tot, do))


POOL_GROUP_DIM = POOL_WIDTH // len(POOL_WINDOWS)


assert all(w == 2 ** (i + 1) for i, w in enumerate(POOL_WINDOWS))


def _pool_inv(c):
    group = _iota((BLK, POOL_WIDTH), 1) // POOL_GROUP_DIM
    pos = c * BLK + _iota((BLK, POOL_WIDTH), 0)
    win = jnp.zeros((BLK, POOL_WIDTH), jnp.int32)
    for gi, wn in enumerate(POOL_WINDOWS):
        win = jnp.where(group == gi, wn, win)
    return 1.0 / jnp.minimum(pos + 1, win).astype(F32)


def _window_sums(ext, trailing):
    group = _iota(ext.shape, 1) // POOL_GROUP_DIM
    acc = ext
    out = None
    for gi in range(len(POOL_WINDOWS)):
        shift = 2 ** gi
        acc = acc + pltpu.roll(acc, shift if trailing else ext.shape[0] - shift, 0)
        out = acc if out is None else jnp.where(group == gi, acc, out)
    return out


def _pool_pooled(ext, cur, inv):
    return _window_sums(ext, True)[BLK:] * inv - cur


def pool_fwd(p, wblk, pb, ps):
    s, n = p.shape

    def body(cur_ref, prev_ref, w_ref, pb_ref, ps_ref, o_ref):
        c = pl.program_id(0)
        cur = cur_ref[...]
        prev = jnp.where(c > 0, prev_ref[...], 0.0)
        pooled = _pool_pooled(jnp.concatenate([prev, cur], axis=0), cur, _pool_inv(c))
        o_ref[...] = (dot_nn(pooled, w_ref[...]) + pb_ref[...]) * ps_ref[...]

    return pl.pallas_call(
        body, name="pool_fwd", grid=(s // BLK,),
        in_specs=[pl.BlockSpec((BLK, n), lambda c: (c, 0)), pl.BlockSpec((BLK, n), lambda c: (jnp.maximum(c - 1, 0), 0)),
                  _full_spec((n, n)), _full_spec((1, n)), _full_spec((1, n))],
        out_specs=pl.BlockSpec((BLK, n), lambda c: (c, 0)), out_shape=jax.ShapeDtypeStruct((s, n), F32),
        compiler_params=_params(1),
    )(p, p, wblk, pb, ps)


def pool_bwd(p, wblk, pb, ps, dout, do_col=0):
    s, n = p.shape
    nb = s // BLK

    def body(cur_ref, prev_ref, w_ref, pb_ref, ps_ref, do_ref, dp_ref, dw_ref, dpb_ref, dps_ref, carry_ref):
        i = pl.program_id(0)
        c = nb - 1 - i
        first = i == 0
        cur = cur_ref[...]
        prev = jnp.where(c > 0, prev_ref[...], 0.0)
        inv = _pool_inv(c)
        pooled = _pool_pooled(jnp.concatenate([prev, cur], axis=0), cur, inv)
        mixed = dot_nn(pooled, w_ref[...]) + pb_ref[...]
        dov = do_ref[...]
        dmixed = dov * ps_ref[...]
        _acc(dps_ref, first, jnp.sum(dov * mixed, axis=0, keepdims=True))
        _acc(dpb_ref, first, jnp.sum(dmixed, axis=0, keepdims=True))
        _acc(dw_ref, first, dot_tn(pooled, dmixed))
        dpooled = dot_nt(dmixed, w_ref[...])
        dext = _window_sums(jnp.concatenate([jnp.zeros((BLK, n), F32), dpooled * inv], axis=0), False)

        @pl.when(first)
        def _():
            carry_ref[...] = jnp.zeros_like(carry_ref)

        dp_ref[...] = dext[BLK:] - dpooled + carry_ref[...]
        carry_ref[...] = dext[:BLK]

    rev = lambda i: (nb - 1 - i, 0)
    return pl.pallas_call(
        body, name="pool_bwd", grid=(nb,),
        in_specs=[pl.BlockSpec((BLK, n), rev), pl.BlockSpec((BLK, n), lambda i: (jnp.maximum(nb - 2 - i, 0), 0)),
                  _full_spec((n, n)), _full_spec((1, n)), _full_spec((1, n)),
                  pl.BlockSpec((BLK, n), lambda i: (nb - 1 - i, do_col))],
        out_specs=[pl.BlockSpec((BLK, n), rev), _full_spec((n, n)), _full_spec((1, n)), _full_spec((1, n))],
        out_shape=[jax.ShapeDtypeStruct((s, n), F32), jax.ShapeDtypeStruct((n, n), F32),
                   jax.ShapeDtypeStruct((1, n), F32), jax.ShapeDtypeStruct((1, n), F32)],
        scratch_shapes=[pltpu.VMEM((BLK, n), F32)],
        compiler_params=_params(1),
    )(p, p, wblk, pb, ps, dout)


def _row_tile(rows):
    if rows <= 512:
        return rows
    for t in (512, 256, 128, 64, 32, 16, 8):
        if rows % t == 0:
            return t
    return rows


def adamw(w, g, m, v):
    n, rows, cols = w.shape
    tr = _row_tile(rows)

    def body(w_ref, g_ref, m_ref, v_ref, d_ref, nm_ref, nv_ref):
        gv = g_ref[...]
        nm = ADAM_B1 * m_ref[...] + (1.0 - ADAM_B1) * gv
        nv = ADAM_B2 * v_ref[...] + (1.0 - ADAM_B2) * (gv * gv)
        m_hat = nm / (1.0 - ADAM_B1 ** ADAM_STEP)
        v_hat = nv / (1.0 - ADAM_B2 ** ADAM_STEP)
        d_ref[...] = -ADAM_LR * (m_hat / (jnp.sqrt(v_hat) + ADAM_EPS) + ADAM_WD * w_ref[...])
        nm_ref[...] = nm
        nv_ref[...] = nv

    spec = pl.BlockSpec((1, tr, cols), lambda i, j: (i, j, 0))
    return pl.pallas_call(
        body, name="adamw", grid=(n, rows // tr), in_specs=[spec] * 4, out_specs=[spec] * 3,
        out_shape=[jax.ShapeDtypeStruct(w.shape, F32)] * 3, compiler_params=_params(2),
    )(w, g, m, v)


def slab_sum(srcs, n_out, out_dtype):
    _, rows, cols = srcs[0][0].shape
    tr = _row_tile(rows)
    sel = jnp.stack([jnp.asarray(base, jnp.int32) for _, base, _ in srcs])

    def body(sel_ref, *refs):
        acc = refs[0][...].astype(F32)
        for r in refs[1:-1]:
            acc = acc + r[...].astype(F32)
        refs[-1][...] = acc.astype(out_dtype)

    def in_spec(k, step):
        return pl.BlockSpec((None, tr, cols), lambda o, i, sel_ref: (sel_ref[k] + step * o, i, 0))

    return pl.pallas_call(
        body, name="slab_sum",
        grid_spec=pltpu.PrefetchScalarGridSpec(
            num_scalar_prefetch=1, grid=(n_out, rows // tr),
            in_specs=[in_spec(k, step) for k, (_, _, step) in enumerate(srcs)],
            out_specs=pl.BlockSpec((None, tr, cols), lambda o, i, sel_ref: (o, i, 0))),
        out_shape=jax.ShapeDtypeStruct((n_out, rows, cols), out_dtype), compiler_params=_params(2),
    )(sel, *[a for a, _, _ in srcs])


ICI_FLIPS = ((1, 0, 0), (0, 1, 0), (1, 1, 0))
D2D_FLIPS = ((0, 0, 1),)
ANY = pl.BlockSpec(memory_space=pl.ANY)


def _me():
    return lax.axis_index("x"), lax.axis_index("y"), lax.axis_index("c")


def _flipped(me, flip):
    return tuple(1 - m if f else m for m, f in zip(me, flip))


def _chip(dev):
    return 2 * dev[0] + dev[1]


def _dev(dev):
    return 4 * dev[0] + 2 * dev[1] + dev[2]


N_CHIP = 4
D2D = (0, 0, 1)


class Exchange:
    def __init__(self, xs, n_out, copies, own=None, in_place=False):
        self.xs, self.copies, self.own, self.in_place = list(xs), copies, own, in_place
        self.n_arr, self.n_cp = len(self.xs), len(copies)
        self.out_shape = [jax.ShapeDtypeStruct((n_out,) + x.shape[1:], x.dtype) for x in self.xs]
        self.scratch = [pltpu.SemaphoreType.DMA((self.n_arr * self.n_cp,)),
                        pltpu.SemaphoreType.DMA((self.n_arr * self.n_cp,)), pltpu.SemaphoreType.DMA((self.n_arr,))]

    def _own(self, x_refs, o_refs, sems, me):
        if self.own is None:
            return []
        return [pltpu.make_async_copy(x_refs[a].at[self.own[0](me)], o_refs[a].at[self.own[1](me)], sems[2].at[a])
                for a in range(self.n_arr)]

    def _copy(self, x_refs, o_refs, sems, me, a, j, sender):
        flip, src_slot, dst_slot = self.copies[j]
        k = a * self.n_cp + j
        return pltpu.make_async_remote_copy(
            src_ref=x_refs[a].at[src_slot(me)], dst_ref=o_refs[a].at[dst_slot(sender)],
            send_sem=sems[0].at[k], recv_sem=sems[1].at[k], device_id=_flipped(me, flip), device_id_type=MESH)

    def start(self, x_refs, o_refs, sems):
        me = _me()
        for cp in self._own(x_refs, o_refs, sems, me):
            cp.start()
        for j in range(self.n_cp):
            for a in range(self.n_arr):
                self._copy(x_refs, o_refs, sems, me, a, j, me).start()

    def wait(self, x_refs, o_refs, sems):
        me = _me()
        for j in range(self.n_cp):
            for a in range(self.n_arr):
                self._copy(x_refs, o_refs, sems, me, a, j, _flipped(me, self.copies[j][0])).wait_recv()
        for j in range(self.n_cp):
            for a in range(self.n_arr):
                self._copy(x_refs, o_refs, sems, me, a, j, me).wait_send()
        for cp in self._own(x_refs, o_refs, sems, me):
            cp.wait()

    def run(self, name):
        n = self.n_arr

        def body(*refs):
            self.start(refs[:n], refs[n:2 * n], refs[2 * n:])
            self.wait(refs[:n], refs[n:2 * n], refs[2 * n:])

        return pl.pallas_call(
            body, name=name, in_specs=[ANY] * n, out_specs=[ANY] * n, out_shape=self.out_shape,
            input_output_aliases={a: a for a in range(n)} if self.in_place else {}, scratch_shapes=self.scratch,
        )(*self.xs)


def hosted_call(body, side, *, name, grid, in_specs, out_specs, out_shape, scratch_shapes, compiler_params, operands):
    n_in, n_out, n_scr = len(in_specs), len(out_specs), len(scratch_shapes)
    if side is None:
        outs = pl.pallas_call(body, name=name, grid=grid, in_specs=in_specs, out_specs=out_specs, out_shape=out_shape,
                              scratch_shapes=scratch_shapes, compiler_params=compiler_params)(*operands)
        return outs, None
    n = side.n_arr

    def full_body(*refs):
        ins, sx = refs[:n_in], refs[n_in:n_in + n]
        outs, so = refs[n_in + n:n_in + n + n_out], refs[n_in + n + n_out:n_in + 2 * n + n_out]
        scr, sems = refs[n_in + 2 * n + n_out:n_in + 2 * n + n_out + n_scr], refs[n_in + 2 * n + n_out + n_scr:]
        first = functools.reduce(jnp.logical_and, [pl.program_id(a) == 0 for a in range(len(grid))])
        last = functools.reduce(jnp.logical_and, [pl.program_id(a) == g - 1 for a, g in enumerate(grid)])

        @pl.when(first)
        def _():
            side.start(sx, so, sems)

        body(*ins, *outs, *scr)

        @pl.when(last)
        def _():
            side.wait(sx, so, sems)

    outs = pl.pallas_call(
        full_body, name=name + "_x", grid=grid, in_specs=list(in_specs) + [ANY] * n,
        out_specs=list(out_specs) + [ANY] * n, out_shape=list(out_shape) + side.out_shape,
        input_output_aliases={n_in + a: n_out + a for a in range(n)} if side.in_place else {},
        scratch_shapes=list(scratch_shapes) + side.scratch, compiler_params=compiler_params,
    )(*operands, *side.xs)
    return outs[:n_out], outs[n_out:]


def gather_ici(shards):
    ici = [(f, lambda me: 0, _dev) for f in ICI_FLIPS]
    return Exchange([s[None] for s in shards], N_DEV, ici, (lambda me: 0, _dev))


def gather_d2d(blocks):
    d2d = [(D2D, (lambda me, k=k: 2 * k + me[2]), (lambda sender, k=k: 2 * k + sender[2])) for k in range(N_CHIP)]
    return Exchange(blocks, N_DEV, d2d, None, in_place=True)


def gathered(blocks):
    return [b.reshape(-1, b.shape[2]) for b in blocks]


def scatter_d2d(parts):
    d2d = [(D2D, (lambda me, k=k: 2 * k + 1 - me[2]), (lambda sender, k=k: k)) for k in range(N_CHIP)]
    return Exchange(parts, N_CHIP, d2d)


def chip_sums(parts, sib):
    c = _me()[2]
    return [slab_sum([(p, c, 2), (s, 0, 1)], N_CHIP, BF16) for p, s in zip(parts, sib)]


def scatter_ici(sums):
    ici = [(f, (lambda me, f=f: _chip(_flipped(me, f))), (lambda sender, i=i: i)) for i, f in enumerate(ICI_FLIPS)]
    return Exchange(sums, len(ICI_FLIPS), ici)


def device_sums(sums, got):
    x, y, _ = _me()
    return [slab_sum([(cs, 2 * x + y, 0)] + [(g, i, 0) for i in range(len(ICI_FLIPS))], 1, F32)[0]
            for cs, g in zip(sums, got)]


def all_gather(shards):
    blocks = gather_ici(shards).run("gather_ici")
    return gathered(gather_d2d(blocks).run("gather_d2d"))


def reduce_scatter(parts):
    sums = chip_sums(parts, scatter_d2d(parts).run("scatter_d2d"))
    return device_sums(sums, scatter_ici(sums).run("scatter_ici"))


def all_reduce_small(v):
    flips = D2D_FLIPS + ICI_FLIPS[:2]

    def body(v_ref, o_ref, got_ref, send_sems, recv_sems):
        me = _me()
        o_ref[...] = v_ref[...]
        for i, flip in enumerate(flips):
            cp = pltpu.make_async_remote_copy(
                src_ref=o_ref, dst_ref=got_ref.at[i], send_sem=send_sems.at[i], recv_sem=recv_sems.at[i],
                device_id=_flipped(me, flip), device_id_type=MESH)
            cp.start()
            cp.wait()
            o_ref[...] = o_ref[...] + got_ref[i]

    vm = pl.BlockSpec(memory_space=pltpu.VMEM)
    return pl.pallas_call(
        body, name="all_reduce_small", in_specs=[vm], out_specs=vm, out_shape=jax.ShapeDtypeStruct(v.shape, F32),
        scratch_shapes=[pltpu.VMEM((len(flips),) + v.shape, F32), pltpu.SemaphoreType.DMA((len(flips),)),
                        pltpu.SemaphoreType.DMA((len(flips),))],
    )(v)


def _perm_rows(wt):
    pad = jnp.zeros((D_IN_PAD - D_IN_PROJ, wt.shape[1]), wt.dtype)
    return jnp.concatenate([wt[:DT_LO], wt[DT_HI:], wt[DT_LO:DT_HI], pad], axis=0)


def _unperm_rows(dwt):
    n = D_IN_PROJ - (DT_HI - DT_LO)
    return jnp.concatenate([dwt[:DT_LO], dwt[n:D_IN_PROJ], dwt[DT_LO:n]], axis=0)


def _pad_lanes(v):
    return jnp.pad(v, ((0, 0), (0, LANE - v.shape[1])))[:, None]


def _block_diag(w):
    l, g, n, _ = w.shape
    out = jnp.zeros((l, g * n, g * n), w.dtype)
    for i in range(g):
        out = out.at[:, i * n:(i + 1) * n, i * n:(i + 1) * n].set(w[:, i])
    return out


def _pack(groups):
    flat = []
    for grp in groups:
        parts = [a.reshape(-1) for a in (grp if isinstance(grp, (list, tuple)) else [grp])]
        n = sum(p.shape[0] for p in parts)
        if -n % LANE:
            parts.append(jnp.zeros((-n % LANE,), parts[0].dtype))
        flat += parts
    return jnp.concatenate(flat).reshape(-1, LANE)


def _unpack(buf, shapes):
    out = []
    lo = 0
    buf = buf.reshape(-1)
    for shp in shapes:
        n = 1
        for k in shp:
            n *= k
        out.append(buf[lo:lo + n].reshape(shp))
        lo += n + (-n % LANE)
    return out


def small_params(w, conv_w_full):
    return dict(
        n1w=w["norm1_w"][:, None], cw=jnp.pad(conv_w_full, ((0, 0), (0, 8 - CONV_WIDTH), (0, 0))),
        cb=w["conv_b"][:, None], dtb=_pad_lanes(w["dt_bias"]), alog=_pad_lanes(w["a_log"]), dsk=_pad_lanes(w["d_skip"]),
        snw=w["ssd_norm_w"][:, None], wblk=_block_diag(w["pool_w"]), pb=w["pool_b"].reshape(-1, 1, POOL_WIDTH),
        ps=w["pool_scale"][:, None], n2w=w["norm2_w"][:, None])


BIG = ("w_in", "w_out", "w_gate", "w_up", "w_down")


def layer_params(small, big, l):
    p = {k: v[l] for k, v in small.items()}
    p.update(w_in=_perm_rows(big[0]), w_out=big[1], wg=big[2], wu=big[3], wd=big[4])
    return p


def _layer_fwd(x, p, next_shards=None):
    z, xbc, qkv, pp, dtr, h1 = inproj_fwd(x, p["n1w"], p["w_in"])
    u = conv_fwd(xbc, p["cw"], p["cb"])
    y_ssd, st = ssd_fwd(u, z, dtr, p["dtb"], p["alog"], p["dsk"], p["snw"])
    (o, tot), blocks = sb_fwd(qkv, gather_ici(next_shards) if next_shards is not None else None)
    yp = pool_fwd(pp, p["wblk"], p["pb"], p["ps"])
    ycat = jnp.concatenate([y_ssd, o, yp], axis=1)
    x_mid = outproj_fwd(ycat, p["w_out"], x)
    (x_out, g, uu), blocks = ffn_fwd(x_mid, p["n2w"], p["wg"], p["wu"], p["wd"],
                                     gather_d2d(blocks) if blocks is not None else None)
    sv = dict(x=x, z=z, xbc=xbc, qkv=qkv, pp=pp, dtr=dtr, h1=h1, u=u, st=st, tot=tot, ycat=ycat, x_mid=x_mid, g=g, uu=uu)
    return x_out, sv, gathered(blocks) if blocks is not None else None


def _layer_bwd(dxo, sv, p, pending=None):
    (dx_mid, dn2w, a, dg, du, h2), sib = ffn_bwd(dxo, sv["x_mid"], sv["g"], sv["uu"], p["n2w"], p["wg"], p["wu"],
                                                 p["wd"], scatter_d2d(pending) if pending is not None else None)
    sums = chip_sums(pending, sib) if pending is not None else None
    gr = dict(norm2_w=dn2w[0], w_down=mm_tn(a, dxo), w_gate=mm_tn(dg, h2), w_up=mm_tn(du, h2))
    dycat = outproj_bwd(dx_mid, p["w_out"])
    gr["w_out"] = mm_tn(sv["ycat"], dx_mid)
    dp, dwblk, dpb, dps = pool_bwd(sv["pp"], p["wblk"], p["pb"], p["ps"], dycat,
                                   (SSD_WIDTH + SB_WIDTH) // POOL_WIDTH)
    n = POOL_GROUP_DIM
    gr["pool_w"] = jnp.stack([dwblk[i * n:(i + 1) * n, i * n:(i + 1) * n] for i in range(len(POOL_WINDOWS))])
    gr["pool_b"] = dpb.reshape(len(POOL_WINDOWS), n)
    gr["pool_scale"] = dps[0]
    (dq, dk, dv), got = sb_bwd(sv["qkv"], sv["tot"], dycat, SSD_WIDTH // LANE,
                               scatter_ici(sums) if sums is not None else None)
    done = device_sums(sums, got) if sums is not None else None
    du_, dz, ddtr, ddtb, dalog, ddsk, dsnw = ssd_bwd(sv["u"], sv["z"], sv["dtr"], sv["st"], dycat, p["dtb"], p["alog"],
                                                     p["dsk"], p["snw"])
    gr.update(dt_bias=ddtb[0, :SSD_HEADS], a_log=dalog[0, :SSD_HEADS], d_skip=ddsk[0, :SSD_HEADS], ssd_norm_w=dsnw[0])
    dxbc, dcw, dcb = conv_bwd(sv["xbc"], p["cw"], p["cb"], du_)
    gr.update(conv_w=dcw[:CONV_WIDTH], conv_b=dcb[0])
    dproj = jnp.concatenate([dz, dxbc, dq, dk, dv, dp, ddtr], axis=1)
    dx, dn1w = inproj_bwd(dproj, p["w_in"], sv["x"], p["n1w"], dx_mid)
    gr.update(norm1_w=dn1w[0], w_in=_unperm_rows(mm_tn(dproj, sv["h1"])))
    return dx, gr, done


def local_step(x, tgt, params, final_w):
    saved = []
    for p in params:
        x, sv, _ = _layer_fwd(x, p)
        saved.append(sv)
    loss, dx, dfw = head_loss(x, final_w[None], tgt)
    grads = []
    for p, sv in zip(reversed(params), reversed(saved)):
        dx, gr, _ = _layer_bwd(dx, sv, p)
        grads.append(gr)
    grads.reverse()
    return loss, dx, dfw[0], grads


WEIGHTS = ("norm1_w", "w_in", "conv_w", "conv_b", "dt_bias", "a_log", "d_skip", "ssd_norm_w", "pool_w", "pool_b",
           "pool_scale", "w_out", "norm2_w", "w_gate", "w_up", "w_down", "final_norm_w")
COL_SHARDED = ("w_in", "w_gate", "w_up")
ROW_SHARDED = ("w_out", "w_down")
SMALL = tuple(k for k in WEIGHTS if k not in COL_SHARDED + ROW_SHARDED)


def kernel(x, norm1_w, w_in, conv_w, conv_b, dt_bias, a_log, d_skip, ssd_norm_w, pool_w, pool_b, pool_scale, w_out, norm2_w, w_gate, w_up, w_down, final_norm_w, loss_target, m_norm1_w, m_w_in, m_conv_w, m_conv_b, m_dt_bias, m_a_log, m_d_skip, m_ssd_norm_w, m_pool_w, m_pool_b, m_pool_scale, m_w_out, m_norm2_w, m_w_gate, m_w_up, m_w_down, m_final_norm_w, v_norm1_w, v_w_in, v_conv_w, v_conv_b, v_dt_bias, v_a_log, v_d_skip, v_ssd_norm_w, v_pool_w, v_pool_b, v_pool_scale, v_w_out, v_norm2_w, v_w_gate, v_w_up, v_w_down, v_final_norm_w):
    args = dict(locals())
    w = {k: args[k] for k in WEIGHTS}
    m = {k: args["m_" + k] for k in WEIGHTS}
    v = {k: args["v_" + k] for k in WEIGHTS}
    depth = w_in.shape[0]
    dev = _dev(_me())
    n_cw = conv_w.shape[-1]

    shards = {k: (jnp.swapaxes(w[k], 1, 2) if k in COL_SHARDED else w[k]).astype(BF16) for k in BIG}
    conv_w_full = all_gather([jnp.swapaxes(conv_w, 0, 2).reshape(n_cw, -1)])[0]
    conv_w_full = jnp.swapaxes(conv_w_full.reshape(N_DEV * n_cw, CONV_WIDTH, depth), 0, 2)
    small = small_params(w, conv_w_full)
    xs = x[0]
    params, saved = [], []
    big = all_gather([shards[k][0] for k in BIG])
    for l in range(depth):
        params.append(layer_params(small, big, l))
        xs, sv, big = _layer_fwd(xs, params[-1], [shards[k][l + 1] for k in BIG] if l + 1 < depth else None)
        saved.append(sv)
    loss, dx, dfw = head_loss(xs, final_norm_w[None], loss_target[0])
    layer_grads = [None] * depth
    big_grads = [None] * depth
    pending = None
    for l in reversed(range(depth)):
        dx, layer_grads[l], done = _layer_bwd(dx, saved[l], params[l], pending)
        if pending is not None:
            big_grads[l + 1] = done
        pending = [layer_grads[l][k].reshape(N_DEV, -1, D_MODEL) for k in BIG]
    big_grads[0] = reduce_scatter(pending)

    grads = {}
    for i, k in enumerate(BIG):
        stacked = jnp.stack([big_grads[l][i] for l in range(depth)])
        grads[k] = jnp.swapaxes(stacked, 1, 2) if k in COL_SHARDED else stacked
    layered = [k for k in SMALL if k != "final_norm_w"]
    small_shapes = [(1, LANE)] + [(depth,) + layer_grads[0][k].shape for k in layered] + [dfw[0].shape]
    packed = _pack([loss] + [[layer_grads[l][k] for l in range(depth)] for k in layered] + [dfw[0]])
    summed = _unpack(all_reduce_small(packed), small_shapes)
    loss = summed[0][0, 0]
    grads.update(zip(layered + ["final_norm_w"], summed[1:]))
    grads["conv_w"] = lax.dynamic_slice_in_dim(grads["conv_w"], dev * n_cw, n_cw, axis=2)

    delta, new_m, new_v = {}, {}, {}
    for k in COL_SHARDED + ROW_SHARDED:
        delta[k], new_m[k], new_v[k] = adamw(w[k], grads[k], m[k], v[k])
    shapes = [w[k].shape for k in SMALL]
    packed = [_pack([t[k] for k in SMALL])[None] for t in (w, grads, m, v)]
    for dst, buf in zip((delta, new_m, new_v), adamw(*packed)):
        dst.update(zip(SMALL, _unpack(buf, shapes)))
    return (loss, dx[None], *[grads[k] for k in WEIGHTS], *[delta[k] for k in WEIGHTS],
            *[new_m[k] for k in WEIGHTS], *[new_v[k] for k in WEIGHTS])
```

```python
import functools

import jax
import jax.numpy as jnp
from jax import lax
from jax.experimental import pallas as pl
from jax.experimental.pallas import tpu as pltpu

F32 = jnp.float32
BF16 = jnp.bfloat16
HIGHEST = lax.Precision.HIGHEST
MESH = pl.DeviceIdType.MESH

EPS = 1e-6
D_MODEL = 1024
SSD_WIDTH = 512
SSD_HEADS = 8
HEAD_DIM = 64
D_STATE = 128
CONV_WIDTH = 4
CONV_DIM = 1024
SB_WIDTH = 256
POOL_WIDTH = 256
POOL_WINDOWS = (2, 4, 8, 16)
D_IN_PROJ = 2568
D_FF = 2816
N_DEV = 8
DEPTH = 4
SEG = (512, 1024, 768, 256, 128)
D_IN_PAD = sum(SEG)
DT_LO, DT_HI = 1536, 1544

LANE = 128
BLK = 128
ROW_TILE = 256
VMEM_LIMIT = 56 * 2**20

ADAM_LR, ADAM_B1, ADAM_B2, ADAM_EPS, ADAM_WD, ADAM_STEP = 0.001, 0.9, 0.999, 1e-08, 0.01, 10


def _params(n_axes=1, vmem=None):
    return pltpu.CompilerParams(dimension_semantics=("arbitrary",) * n_axes, vmem_limit_bytes=vmem)


def _dot(a, b, dims, exact=False):
    if exact:
        return lax.dot_general(a.astype(F32), b.astype(F32), (dims, ((), ())), precision=HIGHEST,
                               preferred_element_type=F32)
    return lax.dot_general(a.astype(BF16), b.astype(BF16), (dims, ((), ())), preferred_element_type=F32)


def dot_nn(a, b, exact=False):
    return _dot(a, b, ((1,), (0,)), exact)


def dot_nt(a, b, exact=False):
    return _dot(a, b, ((1,), (1,)), exact)


def dot_tn(a, b, exact=False):
    return _dot(a, b, ((0,), (0,)), exact)


def _iota(shape, axis):
    return lax.broadcasted_iota(jnp.int32, shape, axis)


def _lane_col(x, h):
    return jnp.sum(jnp.where(_iota(x.shape, 1) == h, x, 0.0), axis=1, keepdims=True)


def _sub_row(x, h):
    return jnp.sum(jnp.where(_iota(x.shape, 0) == h, x, 0.0), axis=0, keepdims=True)


def _sigmoid(x):
    return 1.0 / (1.0 + jnp.exp(-x))


def _rms_fwd(x, w):
    r = lax.rsqrt(jnp.mean(x * x, axis=-1, keepdims=True) + EPS)
    return x * r * w


def _rms_bwd(x, w, dy):
    r = lax.rsqrt(jnp.mean(x * x, axis=-1, keepdims=True) + EPS)
    xh = x * r
    dxh = dy * w
    dx = r * (dxh - xh * jnp.mean(dxh * xh, axis=-1, keepdims=True))
    return dx, jnp.sum(dy * xh, axis=0, keepdims=True)


def _acc(ref, first, val):
    @pl.when(first)
    def _():
        ref[...] = val

    @pl.when(jnp.logical_not(first))
    def _():
        ref[...] += val


def _row_spec(tm, n):
    return pl.BlockSpec((tm, n), lambda i: (i, 0))


def _full_spec(shape):
    return pl.BlockSpec(shape, lambda *_: (0,) * len(shape))


def inproj_fwd(x, nw, w):
    s, d = x.shape
    tm = min(ROW_TILE, s)

    def body(x_ref, nw_ref, w_ref, z_ref, xbc_ref, qkv_ref, p_ref, dt_ref, h_ref):
        h = _rms_fwd(x_ref[...], nw_ref[...]).astype(BF16)
        h_ref[...] = h
        lo = 0
        for ref, n in zip((z_ref, xbc_ref, qkv_ref, p_ref, dt_ref), SEG):
            ref[...] = dot_nt(h, w_ref[lo:lo + n, :])
            lo += n

    return pl.pallas_call(
        body, name="inproj_fwd", grid=(s // tm,),
        in_specs=[_row_spec(tm, d), _full_spec((1, d)), _full_spec(w.shape)],
        out_specs=[_row_spec(tm, n) for n in SEG] + [_row_spec(tm, d)],
        out_shape=[jax.ShapeDtypeStruct((s, n), F32) for n in SEG] + [jax.ShapeDtypeStruct((s, d), BF16)],
        compiler_params=_params(1, VMEM_LIMIT),
    )(x, nw, w)


def inproj_bwd(dproj, w, x, nw, dres):
    s, d = x.shape
    tm = min(ROW_TILE, s)

    def body(dp_ref, w_ref, x_ref, nw_ref, dres_ref, dx_ref, dnw_ref):
        dh = dot_nn(dp_ref[...], w_ref[...])
        dx, dnw = _rms_bwd(x_ref[...], nw_ref[...], dh)
        dx_ref[...] = dres_ref[...] + dx
        _acc(dnw_ref, pl.program_id(0) == 0, dnw)

    return pl.pallas_call(
        body, name="inproj_bwd", grid=(s // tm,),
        in_specs=[_row_spec(tm, dproj.shape[1]), _full_spec(w.shape), _row_spec(tm, d), _full_spec((1, d)),
                  _row_spec(tm, d)],
        out_specs=[_row_spec(tm, d), _full_spec((1, d))],
        out_shape=[jax.ShapeDtypeStruct((s, d), F32), jax.ShapeDtypeStruct((1, d), F32)],
        compiler_params=_params(1, VMEM_LIMIT),
    )(dproj, w, x, nw, dres)


def outproj_fwd(y, w, res, sides=()):
    s, d = res.shape
    tm = min(ROW_TILE, s)

    def body(y_ref, w_ref, r_ref, o_ref):
        o_ref[...] = r_ref[...] + dot_nn(y_ref[...], w_ref[...])

    return hosted_call(
        body, sides, name="outproj_fwd", grid=(s // tm,),
        in_specs=[_row_spec(tm, y.shape[1]), _full_spec(w.shape), _row_spec(tm, d)],
        out_specs=[_row_spec(tm, d)], out_shape=[jax.ShapeDtypeStruct((s, d), F32)], scratch_shapes=[],
        compiler_params=_params(1, VMEM_LIMIT), operands=(y, w, res))


def outproj_bwd(dx, w, sides=()):
    s, d = dx.shape
    tm = min(ROW_TILE, s)

    def body(dx_ref, w_ref, o_ref):
        o_ref[...] = dot_nt(dx_ref[...], w_ref[...])

    return hosted_call(
        body, sides, name="outproj_bwd", grid=(s // tm,),
        in_specs=[_row_spec(tm, d), _full_spec(w.shape)],
        out_specs=[_row_spec(tm, w.shape[0])], out_shape=[jax.ShapeDtypeStruct((s, w.shape[0]), F32)],
        scratch_shapes=[], compiler_params=_params(1, VMEM_LIMIT), operands=(dx, w))


def ffn_fwd(x, nw, wg, wu, wd, sides=()):
    s, d = x.shape
    f = wg.shape[0]
    tm = min(ROW_TILE, s)

    def body(x_ref, nw_ref, wg_ref, wu_ref, wd_ref, o_ref, g_ref, u_ref):
        xv = x_ref[...]
        h = _rms_fwd(xv, nw_ref[...]).astype(BF16)
        g = dot_nt(h, wg_ref[...])
        u = dot_nt(h, wu_ref[...])
        g_ref[...] = g.astype(BF16)
        u_ref[...] = u.astype(BF16)
        o_ref[...] = xv + dot_nn(g * _sigmoid(g) * u, wd_ref[...])

    return hosted_call(
        body, sides, name="ffn_fwd", grid=(s // tm,),
        in_specs=[_row_spec(tm, d), _full_spec((1, d)), _full_spec(wg.shape), _full_spec(wu.shape),
                  _full_spec(wd.shape)],
        out_specs=[_row_spec(tm, d), _row_spec(tm, f), _row_spec(tm, f)],
        out_shape=[jax.ShapeDtypeStruct((s, d), F32), jax.ShapeDtypeStruct((s, f), BF16),
                   jax.ShapeDtypeStruct((s, f), BF16)],
        scratch_shapes=[], compiler_params=_params(1, VMEM_LIMIT), operands=(x, nw, wg, wu, wd))


def ffn_bwd(dxo, x, g, u, nw, wg, wu, wd, sides=()):
    s, d = x.shape
    f = wg.shape[0]
    tm = min(ROW_TILE, s)

    def body(dxo_ref, x_ref, g_ref, u_ref, nw_ref, wg_ref, wu_ref, wd_ref, dx_ref, dnw_ref, a_ref, dg_ref,
             du_ref, h_ref):
        dxo_v = dxo_ref[...]
        xv = x_ref[...]
        da = dot_nt(dxo_v, wd_ref[...])
        gv = g_ref[...].astype(F32)
        uv = u_ref[...].astype(F32)
        sg = _sigmoid(gv)
        sl = gv * sg
        a_ref[...] = (sl * uv).astype(BF16)
        dg = (da * uv * (sg * (1.0 + gv * (1.0 - sg)))).astype(BF16)
        du = (da * sl).astype(BF16)
        dg_ref[...] = dg
        du_ref[...] = du
        dh = dot_nn(dg, wg_ref[...]) + dot_nn(du, wu_ref[...])
        h_ref[...] = _rms_fwd(xv, nw_ref[...]).astype(BF16)
        dx, dnw = _rms_bwd(xv, nw_ref[...], dh)
        dx_ref[...] = dxo_v + dx
        _acc(dnw_ref, pl.program_id(0) == 0, dnw)

    return hosted_call(
        body, sides, name="ffn_bwd", grid=(s // tm,),
        in_specs=[_row_spec(tm, d), _row_spec(tm, d), _row_spec(tm, f), _row_spec(tm, f), _full_spec((1, d)),
                  _full_spec(wg.shape), _full_spec(wu.shape), _full_spec(wd.shape)],
        out_specs=[_row_spec(tm, d), _full_spec((1, d)), _row_spec(tm, f), _row_spec(tm, f), _row_spec(tm, f),
                   _row_spec(tm, d)],
        out_shape=[jax.ShapeDtypeStruct((s, d), F32), jax.ShapeDtypeStruct((1, d), F32),
                   jax.ShapeDtypeStruct((s, f), BF16), jax.ShapeDtypeStruct((s, f), BF16),
                   jax.ShapeDtypeStruct((s, f), BF16), jax.ShapeDtypeStruct((s, d), BF16)],
        scratch_shapes=[], compiler_params=_params(1, VMEM_LIMIT), operands=(dxo, x, g, u, nw, wg, wu, wd))


def _tile(n, cap=256):
    best = LANE
    for t in range(LANE, cap + 1, LANE):
        if n % t == 0:
            best = t
    return best


def mm_tn(a, b):
    s, k = a.shape
    n = b.shape[1]
    tk = _tile(k)

    def body(a_ref, b_ref, o_ref):
        o_ref[...] = dot_nn(a_ref[...].astype(BF16).T, b_ref[...]).astype(BF16)

    return pl.pallas_call(
        body, name="mm_tn", grid=(k // tk,),
        in_specs=[pl.BlockSpec((s, tk), lambda i: (0, i)), _full_spec((s, n))],
        out_specs=pl.BlockSpec((tk, n), lambda i: (i, 0)), out_shape=jax.ShapeDtypeStruct((k, n), BF16),
        compiler_params=_params(1, VMEM_LIMIT),
    )(a, b)


def head_loss(x, fw, tgt):
    s, d = x.shape
    tm = min(ROW_TILE, s)

    def body(x_ref, fw_ref, t_ref, loss_ref, dx_ref, dfw_ref):
        xv = x_ref[...]
        err = _rms_fwd(xv, fw_ref[...]) - t_ref[...]
        part = jnp.zeros((1, LANE), F32) + 0.5 * jnp.sum(err * err) / d
        dx, dfw = _rms_bwd(xv, fw_ref[...], err / d)
        dx_ref[...] = dx
        first = pl.program_id(0) == 0
        _acc(loss_ref, first, part)
        _acc(dfw_ref, first, dfw)

    return pl.pallas_call(
        body, name="head_loss", grid=(s // tm,),
        in_specs=[_row_spec(tm, d), _full_spec((1, d)), _row_spec(tm, d)],
        out_specs=[_full_spec((1, LANE)), _row_spec(tm, d), _full_spec((1, d))],
        out_shape=[jax.ShapeDtypeStruct((1, LANE), F32), jax.ShapeDtypeStruct((s, d), F32),
                   jax.ShapeDtypeStruct((1, d), F32)],
        compiler_params=_params(1),
    )(x, fw, tgt)


def _conv_pre(ext, cw_ref, cb_ref):
    shifted = [pltpu.roll(ext, CONV_WIDTH - 1 - i, 0)[BLK:] if i < CONV_WIDTH - 1 else ext[BLK:]
               for i in range(CONV_WIDTH)]
    acc = cb_ref[...] + sum(cw_ref[i:i + 1, :] * shifted[i] for i in range(CONV_WIDTH))
    return acc, shifted


def conv_fwd(xbc, cw, cb):
    s, n = xbc.shape

    def body(cur_ref, prev_ref, cw_ref, cb_ref, o_ref):
        prev = jnp.where(pl.program_id(0) > 0, prev_ref[...], 0.0)
        acc, _ = _conv_pre(jnp.concatenate([prev, cur_ref[...]], axis=0), cw_ref, cb_ref)
        o_ref[...] = acc * _sigmoid(acc)

    return pl.pallas_call(
        body, name="conv_fwd", grid=(s // BLK,),
        in_specs=[pl.BlockSpec((BLK, n), lambda c: (c, 0)), pl.BlockSpec((BLK, n), lambda c: (jnp.maximum(c - 1, 0), 0)),
                  _full_spec(cw.shape), _full_spec((1, n))],
        out_specs=pl.BlockSpec((BLK, n), lambda c: (c, 0)), out_shape=jax.ShapeDtypeStruct((s, n), F32),
        compiler_params=_params(1),
    )(xbc, xbc, cw, cb)


def conv_bwd(xbc, cw, cb, du):
    s, n = xbc.shape
    nb = s // BLK

    def body(cur_ref, prev_ref, cw_ref, cb_ref, du_ref, dx_ref, dcw_ref, dcb_ref, nxt_ref):
        i = pl.program_id(0)
        c = nb - 1 - i
        prev = jnp.where(c > 0, prev_ref[...], 0.0)
        acc, shifted = _conv_pre(jnp.concatenate([prev, cur_ref[...]], axis=0), cw_ref, cb_ref)
        sg = _sigmoid(acc)
        dacc = du_ref[...] * (sg * (1.0 + acc * (1.0 - sg)))

        @pl.when(i == 0)
        def _():
            nxt_ref[...] = jnp.zeros_like(nxt_ref)
            dcw_ref[...] = jnp.zeros_like(dcw_ref)
            dcb_ref[...] = jnp.zeros_like(dcb_ref)

        dcb_ref[...] += jnp.sum(dacc, axis=0, keepdims=True)
        for t in range(CONV_WIDTH):
            dcw_ref[t:t + 1, :] += jnp.sum(dacc * shifted[t], axis=0, keepdims=True)
        ext = jnp.concatenate([dacc, nxt_ref[...]], axis=0)
        dx = cw_ref[CONV_WIDTH - 1:CONV_WIDTH, :] * dacc
        for t in range(CONV_WIDTH - 1):
            dx += cw_ref[t:t + 1, :] * pltpu.roll(ext, 2 * BLK - (CONV_WIDTH - 1 - t), 0)[:BLK]
        dx_ref[...] = dx
        nxt_ref[...] = dacc

    rev = lambda i: (nb - 1 - i, 0)
    return pl.pallas_call(
        body, name="conv_bwd", grid=(nb,),
        in_specs=[pl.BlockSpec((BLK, n), rev), pl.BlockSpec((BLK, n), lambda i: (jnp.maximum(nb - 2 - i, 0), 0)),
                  _full_spec(cw.shape), _full_spec((1, n)), pl.BlockSpec((BLK, n), rev)],
        out_specs=[pl.BlockSpec((BLK, n), rev), _full_spec((8, n)), _full_spec((1, n))],
        out_shape=[jax.ShapeDtypeStruct((s, n), F32), jax.ShapeDtypeStruct((8, n), F32),
                   jax.ShapeDtypeStruct((1, n), F32)],
        scratch_shapes=[pltpu.VMEM((BLK, n), F32)],
        compiler_params=_params(1),
    )(xbc, xbc, cw, cb, du)


N_PAIR = SSD_HEADS // 2
B_LO = SSD_WIDTH
C_LO = SSD_WIDTH + 2 * D_STATE


def _softplus(x):
    return jnp.maximum(x, 0.0) + jnp.log(1.0 + jnp.exp(-jnp.abs(x)))


def _ssd_chunk(u_ref, dt_ref, dtb_ref, alog_ref):
    shape = (BLK, BLK)
    tri = _iota(shape, 1) <= _iota(shape, 0)
    pre = dt_ref[...] + dtb_ref[...]
    dt = _softplus(pre)
    a = -jnp.exp(alog_ref[...])
    acum = dot_nn(tri.astype(F32), dt * a, exact=True)
    acum_t = acum.T
    last = _sub_row(acum, BLK - 1)
    heads = []
    for h in range(SSD_HEADS):
        col = _lane_col(acum, h)
        seg = jnp.where(tri, col - _sub_row(acum_t, h), -1e30)
        heads.append(dict(col=col, dm=jnp.exp(seg), dt=_lane_col(dt, h), last=_lane_col(last, h)))
    return tri, pre, dt, a, heads


def _pair_mix(lo_mask, v0, v1):
    return jnp.where(lo_mask, v0, v1)


def ssd_fwd(u, z, dtr, dtb, alog, dsk, nw):
    s = u.shape[0]
    nc = s // BLK

    def body(u_ref, z_ref, dt_ref, dtb_ref, alog_ref, dsk_ref, nw_ref, y_ref, st_ref, s_ref):
        @pl.when(pl.program_id(0) == 0)
        def _():
            s_ref[...] = jnp.zeros_like(s_ref)

        _, _, _, _, heads = _ssd_chunk(u_ref, dt_ref, dtb_ref, alog_ref)
        lo_lane = _iota((BLK, LANE), 1) < HEAD_DIM
        lo_sub = _iota((BLK, LANE), 0) < HEAD_DIM
        ys = []
        for p in range(N_PAIR):
            g = p // 2
            h0, h1 = heads[2 * p], heads[2 * p + 1]
            bg = u_ref[:, B_LO + g * D_STATE:B_LO + (g + 1) * D_STATE]
            cg = u_ref[:, C_LO + g * D_STATE:C_LO + (g + 1) * D_STATE]
            xs = u_ref[:, p * LANE:(p + 1) * LANE]
            xp = xs * _pair_mix(lo_lane, h0["dt"], h1["dt"])
            gm = dot_nt(cg, bg)
            yd = _pair_mix(lo_lane, dot_nn(gm * h0["dm"], xp), dot_nn(gm * h1["dm"], xp))
            sp = s_ref[p]
            st_ref[0, p] = sp
            yo = _pair_mix(lo_lane, jnp.exp(h0["col"]), jnp.exp(h1["col"])) * dot_nt(cg, sp)
            dskp = _pair_mix(lo_lane, _lane_col(dsk_ref[...], 2 * p), _lane_col(dsk_ref[...], 2 * p + 1))
            ys.append(yd + yo + xs * dskp)
            wp = _pair_mix(lo_lane, jnp.exp(h0["last"] - h0["col"]), jnp.exp(h1["last"] - h1["col"]))
            el = _pair_mix(lo_sub, jnp.exp(h0["last"]), jnp.exp(h1["last"]))
            s_ref[p] = el * sp + dot_tn(wp * xp, bg)
        y = jnp.concatenate(ys, axis=1)
        zv = z_ref[...]
        y_ref[...] = _rms_fwd(y * zv * _sigmoid(zv), nw_ref[...])

    vec = _full_spec((1, LANE))
    return pl.pallas_call(
        body, name="ssd_fwd", grid=(nc,),
        in_specs=[_row_spec(BLK, CONV_DIM), _row_spec(BLK, SSD_WIDTH), _row_spec(BLK, LANE), vec, vec, vec,
                  _full_spec((1, SSD_WIDTH))],
        out_specs=[_row_spec(BLK, SSD_WIDTH), pl.BlockSpec((1, N_PAIR, LANE, D_STATE), lambda c: (c, 0, 0, 0))],
        out_shape=[jax.ShapeDtypeStruct((s, SSD_WIDTH), F32), jax.ShapeDtypeStruct((nc, N_PAIR, LANE, D_STATE), F32)],
        scratch_shapes=[pltpu.VMEM((N_PAIR, LANE, D_STATE), F32)],
        compiler_params=_params(1),
    )(u, z, dtr, dtb, alog, dsk, nw)


def ssd_bwd(u, z, dtr, st, dyo, dtb, alog, dsk, nw):
    s = u.shape[0]
    nc = s // BLK

    def body(u_ref, z_ref, dt_ref, st_ref, dyo_ref, dtb_ref, alog_ref, dsk_ref, nw_ref,
             du_ref, dz_ref, ddt_ref, ddtb_ref, dalog_ref, ddsk_ref, dnw_ref, ds_ref):
        first = pl.program_id(0) == 0

        @pl.when(first)
        def _():
            ds_ref[...] = jnp.zeros_like(ds_ref)

        tri, pre, dt, a, heads = _ssd_chunk(u_ref, dt_ref, dtb_ref, alog_ref)
        shape = (BLK, LANE)
        lane = _iota(shape, 1)
        lo_lane = lane < HEAD_DIM
        lo_sub = _iota(shape, 0) < HEAD_DIM
        pairs = []
        ys = []
        for p in range(N_PAIR):
            g = p // 2
            h0, h1 = heads[2 * p], heads[2 * p + 1]
            bg = u_ref[:, B_LO + g * D_STATE:B_LO + (g + 1) * D_STATE]
            cg = u_ref[:, C_LO + g * D_STATE:C_LO + (g + 1) * D_STATE]
            xs = u_ref[:, p * LANE:(p + 1) * LANE]
            dtp = _pair_mix(lo_lane, h0["dt"], h1["dt"])
            xp = xs * dtp
            gm = dot_nt(cg, bg)
            m0, m1 = gm * h0["dm"], gm * h1["dm"]
            sp = st_ref[0, p]
            eap = _pair_mix(lo_lane, jnp.exp(h0["col"]), jnp.exp(h1["col"]))
            yo = eap * dot_nt(cg, sp)
            dskp = _pair_mix(lo_lane, _lane_col(dsk_ref[...], 2 * p), _lane_col(dsk_ref[...], 2 * p + 1))
            ys.append(_pair_mix(lo_lane, dot_nn(m0, xp), dot_nn(m1, xp)) + yo + xs * dskp)
            pairs.append(dict(bg=bg, cg=cg, xs=xs, dtp=dtp, xp=xp, gm=gm, m=(m0, m1), sp=sp, eap=eap, yo=yo, dskp=dskp))
        y = jnp.concatenate(ys, axis=1)
        zv = z_ref[...]
        sz = _sigmoid(zv)
        gate = zv * sz
        dyg, dnw = _rms_bwd(y * gate, nw_ref[...], dyo_ref[...])
        _acc(dnw_ref, first, dnw)
        dy = dyg * gate
        dz_ref[...] = dyg * y * (sz * (1.0 + zv * (1.0 - sz)))

        zeros = jnp.zeros(shape, F32)
        dacum_col = zeros
        dacum_row = zeros
        ddt = zeros
        ddsk = jnp.zeros((1, LANE), F32)
        dlast = jnp.zeros((1, LANE), F32)
        head_row = _iota((1, LANE), 1)
        sub = _iota(shape, 0)
        db = [zeros, zeros]
        dc = [zeros, zeros]
        for p in range(N_PAIR):
            g = p // 2
            q = pairs[p]
            dyp = dy[:, p * LANE:(p + 1) * LANE]
            dsn = ds_ref[p]
            t = dyp * q["xs"]
            dxs = dyp * q["dskp"]
            dcs = dyp * q["eap"]
            dc[g] = dc[g] + dot_nn(dcs, q["sp"])
            dsp = dot_tn(dcs, q["cg"])
            dea = dyp * q["yo"]
            elp = _pair_mix(lo_sub, jnp.exp(heads[2 * p]["last"]), jnp.exp(heads[2 * p + 1]["last"]))
            dsp = dsp + elp * dsn
            dels = dsn * q["sp"] * elp
            wp = _pair_mix(lo_lane, jnp.exp(heads[2 * p]["last"] - heads[2 * p]["col"]),
                           jnp.exp(heads[2 * p + 1]["last"] - heads[2 * p + 1]["col"]))
            dv = dot_nt(q["bg"], dsn)
            db[g] = db[g] + dot_nn(wp * q["xp"], dsn)
            dxp = dv * wp
            dwv = dv * q["xp"] * wp
            dgm = zeros
            for k in range(2):
                h = 2 * p + k
                mine = lo_lane if k == 0 else jnp.logical_not(lo_lane)
                mine_sub = lo_sub if k == 0 else jnp.logical_not(lo_sub)
                dyh = jnp.where(mine, dyp, 0.0)
                dm = dot_nt(dyh, q["xp"])
                dxp = dxp + dot_tn(q["m"][k], dyh)
                dgm = dgm + dm * heads[h]["dm"]
                e = dm * q["m"][k]
                onehot = lane == h
                dw_col = jnp.sum(jnp.where(mine, dwv, 0.0), axis=1, keepdims=True)
                col = (jnp.sum(e, axis=1, keepdims=True) + jnp.sum(jnp.where(mine, dea, 0.0), axis=1, keepdims=True)
                       - dw_col)
                dacum_col = dacum_col + jnp.where(onehot, col, 0.0)
                dacum_row = dacum_row - jnp.where(sub == h, jnp.sum(e, axis=0, keepdims=True), 0.0)
                dl = jnp.sum(dw_col) + jnp.sum(jnp.where(mine_sub, dels, 0.0))
                dlast = dlast + jnp.where(head_row == h, dl, 0.0)
                ddsk = ddsk + jnp.where(head_row == h, jnp.sum(jnp.where(mine, t, 0.0)), 0.0)
            dc[g] = dc[g] + dot_nn(dgm, q["bg"])
            db[g] = db[g] + dot_tn(dgm, q["cg"])
            dxs = dxs + dxp * q["dtp"]
            tt = dxp * q["xs"]
            for k in range(2):
                mine = lo_lane if k == 0 else jnp.logical_not(lo_lane)
                ddt = ddt + jnp.where(lane == 2 * p + k, jnp.sum(jnp.where(mine, tt, 0.0), axis=1, keepdims=True), 0.0)
            du_ref[:, p * LANE:(p + 1) * LANE] = dxs
            ds_ref[p] = dsp
        for g in range(2):
            du_ref[:, B_LO + g * D_STATE:B_LO + (g + 1) * D_STATE] = db[g]
            du_ref[:, C_LO + g * D_STATE:C_LO + (g + 1) * D_STATE] = dc[g]
        dacum = dacum_col + dacum_row.T + jnp.where(sub == BLK - 1, dlast, 0.0)
        dda = dot_tn(tri.astype(F32), dacum, exact=True)
        ddt = ddt + dda * a
        _acc(dalog_ref, first, jnp.sum(dda * dt, axis=0, keepdims=True) * a)
        dpre = ddt * _sigmoid(pre)
        ddt_ref[...] = dpre
        _acc(ddtb_ref, first, jnp.sum(dpre, axis=0, keepdims=True))
        _acc(ddsk_ref, first, ddsk)

    rev = lambda i: (nc - 1 - i, 0)
    vec = _full_spec((1, LANE))
    rows = lambda n: pl.BlockSpec((BLK, n), rev)
    return pl.pallas_call(
        body, name="ssd_bwd", grid=(nc,),
        in_specs=[rows(CONV_DIM), rows(SSD_WIDTH), rows(LANE),
                  pl.BlockSpec((1, N_PAIR, LANE, D_STATE), lambda i: (nc - 1 - i, 0, 0, 0)), rows(SSD_WIDTH),
                  vec, vec, vec, _full_spec((1, SSD_WIDTH))],
        out_specs=[rows(CONV_DIM), rows(SSD_WIDTH), rows(LANE), vec, vec, vec, _full_spec((1, SSD_WIDTH))],
        out_shape=[jax.ShapeDtypeStruct((s, CONV_DIM), F32), jax.ShapeDtypeStruct((s, SSD_WIDTH), F32),
                   jax.ShapeDtypeStruct((s, LANE), F32)] + [jax.ShapeDtypeStruct((1, LANE), F32)] * 3
        + [jax.ShapeDtypeStruct((1, SSD_WIDTH), F32)],
        scratch_shapes=[pltpu.VMEM((N_PAIR, LANE, D_STATE), F32)],
        compiler_params=_params(1),
    )(u, z, dtr, st, dyo, dtb, alog, dsk, nw)


SB_PAIRS = SB_WIDTH // LANE
SB_SCALE = HEAD_DIM ** -0.5


SB_TQ = 256


def _sb_tq(s):
    return min(SB_TQ, s)


def _sb_stack(x):
    lo_lane = _iota(x.shape, 1) < HEAD_DIM
    return jnp.concatenate([jnp.where(lo_lane, x, 0.0), jnp.where(lo_lane, 0.0, x)], axis=0)


def _sb_unstack(x2):
    tq = x2.shape[0] // 2
    lo_lane = _iota((tq, LANE), 1) < HEAD_DIM
    return jnp.where(lo_lane, x2[:tq], x2[tq:])


def _sb_logits(q2, kj, row0, col0, masked):
    shape = (q2.shape[0], BLK)
    tq = shape[0] // 2
    z = dot_nt(q2, kj)
    t = jnp.log(1.0 + jnp.exp(-jnp.abs(z)))
    ls = jnp.minimum(z, 0.0) - t
    lk = jnp.minimum(-z, 0.0) - t
    if not masked:
        return None, ls, lk
    row = _iota(shape, 0)
    valid = (col0 + _iota(shape, 1)) < (row0 + jnp.where(row < tq, row, row - tq))
    return valid, ls, jnp.where(valid, lk, 0.0)


def _sb_where(valid, x):
    return x if valid is None else jnp.where(valid, x, 0.0)


def _sums(x, mask2, parts):
    acc = None
    rest = x
    for _ in range(parts):
        term = rest.astype(BF16)
        rest = rest - term.astype(F32)
        d = lax.dot_general(term, mask2, (((1,), (0,)), ((), ())), preferred_element_type=F32)
        acc = d if acc is None else acc + d
    return acc[:, :BLK], acc[:, BLK:]


def _mask2(cond):
    return jnp.concatenate([cond.astype(BF16), jnp.ones(cond.shape, BF16)], axis=1)


def _sb_specs(s):
    tq = _sb_tq(s)
    qspec = pl.BlockSpec((tq, LANE), lambda p, i: (i, p))
    kspec = pl.BlockSpec((s, LANE), lambda p, i: (0, SB_PAIRS + p))
    vspec = pl.BlockSpec((s, LANE), lambda p, i: (0, 2 * SB_PAIRS + p))
    return qspec, kspec, vspec


def sb_fwd(qkv, sides=()):
    s = qkv.shape[0]
    tq = _sb_tq(s)
    kpq = tq // BLK

    def body(q_ref, k_ref, v_ref, o_ref, t_ref, acc_ref):
        qi = pl.program_id(1)
        q2 = _sb_stack(q_ref[...] * SB_SCALE).astype(BF16)
        later = _mask2(_iota((BLK, BLK), 0) > _iota((BLK, BLK), 1))
        acc_ref[...] = jnp.zeros_like(acc_ref)

        def step(j, r, masked):
            rows = pl.ds(pl.multiple_of(j * BLK, BLK), BLK)
            valid, ls, lk = _sb_logits(q2, k_ref[rows, :], qi * tq, j * BLK, masked)
            after, total = _sums(lk, later, 2)
            w = _sb_where(valid, jnp.exp(ls + r + after))
            acc_ref[...] += dot_nn(w, v_ref[rows, :])
            return r + total

        r = jnp.zeros((2 * tq, LANE), F32)
        for d in reversed(range(kpq)):
            r = step(kpq * qi + d, r, True)

        def tile(g, r):
            for d in reversed(range(kpq)):
                r = step(kpq * (qi - 1 - g) + d, r, False)
            return r

        r = lax.fori_loop(0, qi, tile, r)
        o_ref[...] = _sb_unstack(acc_ref[...])
        t_ref[...] = jnp.concatenate([r[:tq], r[tq:]], axis=1)

    return hosted_call(
        body, sides, name="sb_fwd", grid=(SB_PAIRS, s // tq),
        in_specs=list(_sb_specs(s)),
        out_specs=[pl.BlockSpec((tq, LANE), lambda p, i: (i, p)), pl.BlockSpec((tq, 2 * LANE), lambda p, i: (i, p))],
        out_shape=[jax.ShapeDtypeStruct((s, SB_WIDTH), F32), jax.ShapeDtypeStruct((s, 2 * SB_WIDTH), F32)],
        scratch_shapes=[pltpu.VMEM((2 * tq, LANE), F32)],
        compiler_params=_params(2), operands=(qkv, qkv, qkv))


def sb_bwd(qkv, tot, do, do_col=0, sides=()):
    s = qkv.shape[0]
    tq = _sb_tq(s)
    kpq = tq // BLK

    def body(q_ref, k_ref, v_ref, t_ref, do_ref, dq_ref, dk_ref, dv_ref, acc_ref):
        qi = pl.program_id(1)
        q2 = _sb_stack(q_ref[...] * SB_SCALE).astype(BF16)
        do2 = _sb_stack(do_ref[...]).astype(BF16)
        tot2 = jnp.concatenate([t_ref[:, :LANE], t_ref[:, LANE:]], axis=0)
        sq = (BLK, BLK)
        later = _mask2(_iota(sq, 0) > _iota(sq, 1))
        before = _mask2(_iota(sq, 0) < _iota(sq, 1))
        acc_ref[...] = jnp.zeros_like(acc_ref)

        @pl.when(qi == 0)
        def _():
            dk_ref[...] = jnp.zeros_like(dk_ref)
            dv_ref[...] = jnp.zeros_like(dv_ref)

        def step(j, carry, masked):
            rc, fc = carry
            rows = pl.ds(pl.multiple_of(j * BLK, BLK), BLK)
            kj = k_ref[rows, :]
            vj = v_ref[rows, :]
            valid, ls, lk = _sb_logits(q2, kj, qi * tq, j * BLK, masked)
            after, total = _sums(lk, later, 2)
            rc = rc - total
            w = _sb_where(valid, jnp.exp(ls + rc + after))
            e = w * dot_nt(do2, vj)
            f_in, f_tot = _sums(e, before, 2)
            sg = jnp.exp(ls)
            dz = _sb_where(valid, e * (1.0 - sg) - (fc + f_in) * sg)
            acc_ref[...] += dot_nn(dz, kj)
            dk_ref[rows, :] += dot_tn(dz, q2)
            dv_ref[rows, :] += dot_tn(w, do2)
            return rc, fc + f_tot

        def tile(g, carry):
            for d in range(kpq):
                carry = step(kpq * g + d, carry, False)
            return carry

        carry = lax.fori_loop(0, qi, tile, (tot2, jnp.zeros((2 * tq, LANE), F32)))
        for d in range(kpq):
            carry = step(kpq * qi + d, carry, True)
        dq_ref[...] = SB_SCALE * _sb_unstack(acc_ref[...])

    qspec, kspec, vspec = _sb_specs(s)
    blk = pl.BlockSpec((tq, LANE), lambda p, i: (i, p))
    acc = pl.BlockSpec((s, LANE), lambda p, i: (0, p))
    return hosted_call(
        body, sides, name="sb_bwd", grid=(SB_PAIRS, s // tq),
        in_specs=[qspec, kspec, vspec, pl.BlockSpec((tq, 2 * LANE), lambda p, i: (i, p)),
                  pl.BlockSpec((tq, LANE), lambda p, i: (i, do_col + p))],
        out_specs=[blk, acc, acc],
        out_shape=[jax.ShapeDtypeStruct((s, SB_WIDTH), F32)] * 3,
        scratch_shapes=[pltpu.VMEM((2 * tq, LANE), F32)],
        compiler_params=_params(2), operands=(qkv, qkv, qkv, tot, do))


POOL_GROUP_DIM = POOL_WIDTH // len(POOL_WINDOWS)


assert all(w == 2 ** (i + 1) for i, w in enumerate(POOL_WINDOWS))


def _pool_inv(c):
    group = _iota((BLK, POOL_WIDTH), 1) // POOL_GROUP_DIM
    pos = c * BLK + _iota((BLK, POOL_WIDTH), 0)
    win = jnp.zeros((BLK, POOL_WIDTH), jnp.int32)
    for gi, wn in enumerate(POOL_WINDOWS):
        win = jnp.where(group == gi, wn, win)
    return 1.0 / jnp.minimum(pos + 1, win).astype(F32)


def _window_sums(ext, trailing):
    group = _iota(ext.shape, 1) // POOL_GROUP_DIM
    acc = ext
    out = None
    for gi in range(len(POOL_WINDOWS)):
        shift = 2 ** gi
        acc = acc + pltpu.roll(acc, shift if trailing else ext.shape[0] - shift, 0)
        out = acc if out is None else jnp.where(group == gi, acc, out)
    return out


def _pool_pooled(ext, cur, inv):
    return _window_sums(ext, True)[BLK:] * inv - cur


def pool_fwd(p, wblk, pb, ps):
    s, n = p.shape

    def body(cur_ref, prev_ref, w_ref, pb_ref, ps_ref, o_ref):
        c = pl.program_id(0)
        cur = cur_ref[...]
        prev = jnp.where(c > 0, prev_ref[...], 0.0)
        pooled = _pool_pooled(jnp.concatenate([prev, cur], axis=0), cur, _pool_inv(c))
        o_ref[...] = (dot_nn(pooled, w_ref[...]) + pb_ref[...]) * ps_ref[...]

    return pl.pallas_call(
        body, name="pool_fwd", grid=(s // BLK,),
        in_specs=[pl.BlockSpec((BLK, n), lambda c: (c, 0)), pl.BlockSpec((BLK, n), lambda c: (jnp.maximum(c - 1, 0), 0)),
                  _full_spec((n, n)), _full_spec((1, n)), _full_spec((1, n))],
        out_specs=pl.BlockSpec((BLK, n), lambda c: (c, 0)), out_shape=jax.ShapeDtypeStruct((s, n), F32),
        compiler_params=_params(1),
    )(p, p, wblk, pb, ps)


def pool_bwd(p, wblk, pb, ps, dout, do_col=0):
    s, n = p.shape
    nb = s // BLK

    def body(cur_ref, prev_ref, w_ref, pb_ref, ps_ref, do_ref, dp_ref, dw_ref, dpb_ref, dps_ref, carry_ref):
        i = pl.program_id(0)
        c = nb - 1 - i
        first = i == 0
        cur = cur_ref[...]
        prev = jnp.where(c > 0, prev_ref[...], 0.0)
        inv = _pool_inv(c)
        pooled = _pool_pooled(jnp.concatenate([prev, cur], axis=0), cur, inv)
        mixed = dot_nn(pooled, w_ref[...]) + pb_ref[...]
        dov = do_ref[...]
        dmixed = dov * ps_ref[...]
        _acc(dps_ref, first, jnp.sum(dov * mixed, axis=0, keepdims=True))
        _acc(dpb_ref, first, jnp.sum(dmixed, axis=0, keepdims=True))
        _acc(dw_ref, first, dot_tn(pooled, dmixed))
        dpooled = dot_nt(dmixed, w_ref[...])
        dext = _window_sums(jnp.concatenate([jnp.zeros((BLK, n), F32), dpooled * inv], axis=0), False)

        @pl.when(first)
        def _():
            carry_ref[...] = jnp.zeros_like(carry_ref)

        dp_ref[...] = dext[BLK:] - dpooled + carry_ref[...]
        carry_ref[...] = dext[:BLK]

    rev = lambda i: (nb - 1 - i, 0)
    return pl.pallas_call(
        body, name="pool_bwd", grid=(nb,),
        in_specs=[pl.BlockSpec((BLK, n), rev), pl.BlockSpec((BLK, n), lambda i: (jnp.maximum(nb - 2 - i, 0), 0)),
                  _full_spec((n, n)), _full_spec((1, n)), _full_spec((1, n)),
                  pl.BlockSpec((BLK, n), lambda i: (nb - 1 - i, do_col))],
        out_specs=[pl.BlockSpec((BLK, n), rev), _full_spec((n, n)), _full_spec((1, n)), _full_spec((1, n))],
        out_shape=[jax.ShapeDtypeStruct((s, n), F32), jax.ShapeDtypeStruct((n, n), F32),
                   jax.ShapeDtypeStruct((1, n), F32), jax.ShapeDtypeStruct((1, n), F32)],
        scratch_shapes=[pltpu.VMEM((BLK, n), F32)],
        compiler_params=_params(1),
    )(p, p, wblk, pb, ps, dout)


def _row_tile(rows):
    if rows <= 512:
        return rows
    for t in (512, 256, 128, 64, 32, 16, 8):
        if rows % t == 0:
            return t
    return rows


def adamw(w, g, m, v):
    n, rows, cols = w.shape
    tr = _row_tile(rows)

    def body(w_ref, g_ref, m_ref, v_ref, d_ref, nm_ref, nv_ref):
        d_ref[...], nm_ref[...], nv_ref[...] = _adamw_math(w_ref[...], g_ref[...], m_ref[...], v_ref[...])

    spec = pl.BlockSpec((1, tr, cols), lambda i, j: (i, j, 0))
    return pl.pallas_call(
        body, name="adamw", grid=(n, rows // tr), in_specs=[spec] * 4, out_specs=[spec] * 3,
        out_shape=[jax.ShapeDtypeStruct(w.shape, F32)] * 3, compiler_params=_params(2),
    )(w, g, m, v)


def _adamw_math(w, g, m, v):
    nm = ADAM_B1 * m + (1.0 - ADAM_B1) * g
    nv = ADAM_B2 * v + (1.0 - ADAM_B2) * (g * g)
    m_hat = nm / (1.0 - ADAM_B1 ** ADAM_STEP)
    v_hat = nv / (1.0 - ADAM_B2 ** ADAM_STEP)
    return -ADAM_LR * (m_hat / (jnp.sqrt(v_hat) + ADAM_EPS) + ADAM_WD * w), nm, nv


def adamw_small(ws, gs, ms, vs):
    n = len(ws)

    def body(*refs):
        for i in range(n):
            outs = _adamw_math(*[refs[k * n + i][...] for k in range(4)])
            for k in range(3):
                refs[(4 + k) * n + i][...] = outs[k]

    vm = pl.BlockSpec(memory_space=pltpu.VMEM)
    outs = pl.pallas_call(
        body, name="adamw_small", in_specs=[vm] * (4 * n), out_specs=[vm] * (3 * n),
        out_shape=[jax.ShapeDtypeStruct(w.shape, F32) for w in ws] * 3,
    )(*ws, *gs, *ms, *vs)
    return outs[:n], outs[n:2 * n], outs[2 * n:]


def slab_sum(srcs, n_out, out_dtype):
    _, rows, cols = srcs[0][0].shape
    tr = _row_tile(rows)
    sel = jnp.stack([jnp.asarray(base, jnp.int32) for _, base, _ in srcs])

    def body(sel_ref, *refs):
        acc = refs[0][...].astype(F32)
        for r in refs[1:-1]:
            acc = acc + r[...].astype(F32)
        refs[-1][...] = acc.astype(out_dtype)

    def in_spec(k, step):
        return pl.BlockSpec((None, tr, cols), lambda o, i, sel_ref: (sel_ref[k] + step * o, i, 0))

    return pl.pallas_call(
        body, name="slab_sum",
        grid_spec=pltpu.PrefetchScalarGridSpec(
            num_scalar_prefetch=1, grid=(n_out, rows // tr),
            in_specs=[in_spec(k, step) for k, (_, _, step) in enumerate(srcs)],
            out_specs=pl.BlockSpec((None, tr, cols), lambda o, i, sel_ref: (o, i, 0))),
        out_shape=jax.ShapeDtypeStruct((n_out, rows, cols), out_dtype), compiler_params=_params(2),
    )(sel, *[a for a, _, _ in srcs])


ICI_FLIPS = ((1, 0, 0), (0, 1, 0), (1, 1, 0))
D2D_FLIPS = ((0, 0, 1),)
ANY = pl.BlockSpec(memory_space=pl.ANY)


def _me():
    return lax.axis_index("x"), lax.axis_index("y"), lax.axis_index("c")


def _flipped(me, flip):
    return tuple(1 - m if f else m for m, f in zip(me, flip))


def _chip(dev):
    return 2 * dev[0] + dev[1]


def _dev(dev):
    return 4 * dev[0] + 2 * dev[1] + dev[2]


N_CHIP = 4
D2D = (0, 0, 1)


class Exchange:
    def __init__(self, xs, n_out, copies, own=None, in_place=False):
        self.xs, self.copies, self.own, self.in_place = list(xs), copies, own, in_place
        self.n_arr, self.n_cp = len(self.xs), len(copies)
        self.out_shape = [jax.ShapeDtypeStruct((n_out,) + x.shape[1:], x.dtype) for x in self.xs]
        self.scratch = [pltpu.SemaphoreType.DMA((self.n_arr * self.n_cp,)),
                        pltpu.SemaphoreType.DMA((self.n_arr * self.n_cp,)), pltpu.SemaphoreType.DMA((self.n_arr,))]

    def _own(self, x_refs, o_refs, sems, me):
        if self.own is None:
            return []
        return [pltpu.make_async_copy(x_refs[a].at[self.own[0](me)], o_refs[a].at[self.own[1](me)], sems[2].at[a])
                for a in range(self.n_arr)]

    def _copy(self, x_refs, o_refs, sems, me, a, j, sender):
        flip, src_slot, dst_slot = self.copies[j]
        k = a * self.n_cp + j
        return pltpu.make_async_remote_copy(
            src_ref=x_refs[a].at[src_slot(me)], dst_ref=o_refs[a].at[dst_slot(sender)],
            send_sem=sems[0].at[k], recv_sem=sems[1].at[k], device_id=_flipped(me, flip), device_id_type=MESH)

    def start(self, x_refs, o_refs, sems):
        me = _me()
        for cp in self._own(x_refs, o_refs, sems, me):
            cp.start()
        for j in range(self.n_cp):
            for a in range(self.n_arr):
                self._copy(x_refs, o_refs, sems, me, a, j, me).start()

    def wait(self, x_refs, o_refs, sems):
        me = _me()
        for j in range(self.n_cp):
            for a in range(self.n_arr):
                self._copy(x_refs, o_refs, sems, me, a, j, _flipped(me, self.copies[j][0])).wait_recv()
        for j in range(self.n_cp):
            for a in range(self.n_arr):
                self._copy(x_refs, o_refs, sems, me, a, j, me).wait_send()
        for cp in self._own(x_refs, o_refs, sems, me):
            cp.wait()

    def run(self, name):
        n = self.n_arr

        def body(*refs):
            self.start(refs[:n], refs[n:2 * n], refs[2 * n:])
            self.wait(refs[:n], refs[n:2 * n], refs[2 * n:])

        return pl.pallas_call(
            body, name=name, in_specs=[ANY] * n, out_specs=[ANY] * n, out_shape=self.out_shape,
            input_output_aliases={a: a for a in range(n)} if self.in_place else {}, scratch_shapes=self.scratch,
        )(*self.xs)


def hosted_call(body, sides, *, name, grid, in_specs, out_specs, out_shape, scratch_shapes, compiler_params, operands):
    n_in, n_out, n_scr = len(in_specs), len(out_specs), len(scratch_shapes)
    live = [s for s in sides if s is not None]
    if not live:
        outs = pl.pallas_call(body, name=name, grid=grid, in_specs=in_specs, out_specs=out_specs, out_shape=out_shape,
                              scratch_shapes=scratch_shapes, compiler_params=compiler_params)(*operands)
        return outs, [None] * len(sides)
    n = sum(s.n_arr for s in live)
    lo = [sum(s.n_arr for s in live[:i]) for i in range(len(live))]

    def full_body(*refs):
        ins, sx = refs[:n_in], refs[n_in:n_in + n]
        outs, so = refs[n_in + n:n_in + n + n_out], refs[n_in + n + n_out:n_in + 2 * n + n_out]
        scr, sems = refs[n_in + 2 * n + n_out:n_in + 2 * n + n_out + n_scr], refs[n_in + 2 * n + n_out + n_scr:]
        first = functools.reduce(jnp.logical_and, [pl.program_id(a) == 0 for a in range(len(grid))])
        last = functools.reduce(jnp.logical_and, [pl.program_id(a) == g - 1 for a, g in enumerate(grid)])
        parts = [(s, sx[l:l + s.n_arr], so[l:l + s.n_arr], sems[3 * i:3 * i + 3]) for i, (s, l) in enumerate(zip(live, lo))]

        @pl.when(first)
        def _():
            for s, x, o, m in parts:
                s.start(x, o, m)

        body(*ins, *outs, *scr)

        @pl.when(last)
        def _():
            for s, x, o, m in parts:
                s.wait(x, o, m)

    aliases = {n_in + l + a: n_out + l + a for s, l in zip(live, lo) if s.in_place for a in range(s.n_arr)}
    outs = pl.pallas_call(
        full_body, name=name + "_x", grid=grid, in_specs=list(in_specs) + [ANY] * n,
        out_specs=list(out_specs) + [ANY] * n, out_shape=list(out_shape) + [o for s in live for o in s.out_shape],
        input_output_aliases=aliases,
        scratch_shapes=list(scratch_shapes) + [m for s in live for m in s.scratch], compiler_params=compiler_params,
    )(*operands, *[x for s in live for x in s.xs])
    side_outs = iter([outs[n_out + l:n_out + l + s.n_arr] for s, l in zip(live, lo)])
    return outs[:n_out], [next(side_outs) if s is not None else None for s in sides]


def gather_ici(shards):
    ici = [(f, lambda me: 0, _dev) for f in ICI_FLIPS]
    return Exchange([s[None] for s in shards], N_DEV, ici, (lambda me: 0, _dev))


def gather_d2d(blocks):
    d2d = [(D2D, (lambda me, k=k: 2 * k + me[2]), (lambda sender, k=k: 2 * k + sender[2])) for k in range(N_CHIP)]
    return Exchange(blocks, N_DEV, d2d, None, in_place=True)


def gathered(blocks):
    return [b.reshape(-1, b.shape[2]) for b in blocks]


def scatter_d2d(parts):
    d2d = [(D2D, (lambda me, k=k: 2 * k + 1 - me[2]), (lambda sender, k=k: k)) for k in range(N_CHIP)]
    return Exchange(parts, N_CHIP, d2d)


def chip_sums(parts, sib):
    c = _me()[2]
    return [slab_sum([(p, c, 2), (s, 0, 1)], N_CHIP, BF16) for p, s in zip(parts, sib)]


def scatter_ici(sums):
    ici = [(f, (lambda me, f=f: _chip(_flipped(me, f))), (lambda sender, i=i: i)) for i, f in enumerate(ICI_FLIPS)]
    return Exchange(sums, len(ICI_FLIPS), ici)


def device_sums(sums, got):
    x, y, _ = _me()
    return [slab_sum([(cs, 2 * x + y, 0)] + [(g, i, 0) for i in range(len(ICI_FLIPS))], 1, F32)[0]
            for cs, g in zip(sums, got)]


def all_gather(shards):
    blocks = gather_ici(shards).run("gather_ici")
    return gathered(gather_d2d(blocks).run("gather_d2d"))


def reduce_scatter(parts):
    sums = chip_sums(parts, scatter_d2d(parts).run("scatter_d2d"))
    return device_sums(sums, scatter_ici(sums).run("scatter_ici"))


def all_reduce_small(v):
    flips = D2D_FLIPS + ICI_FLIPS[:2]

    def body(v_ref, o_ref, got_ref, send_sems, recv_sems):
        me = _me()
        o_ref[...] = v_ref[...]
        for i, flip in enumerate(flips):
            cp = pltpu.make_async_remote_copy(
                src_ref=o_ref, dst_ref=got_ref.at[i], send_sem=send_sems.at[i], recv_sem=recv_sems.at[i],
                device_id=_flipped(me, flip), device_id_type=MESH)
            cp.start()
            cp.wait()
            o_ref[...] = o_ref[...] + got_ref[i]

    vm = pl.BlockSpec(memory_space=pltpu.VMEM)
    return pl.pallas_call(
        body, name="all_reduce_small", in_specs=[vm], out_specs=vm, out_shape=jax.ShapeDtypeStruct(v.shape, F32),
        scratch_shapes=[pltpu.VMEM((len(flips),) + v.shape, F32), pltpu.SemaphoreType.DMA((len(flips),)),
                        pltpu.SemaphoreType.DMA((len(flips),))],
    )(v)


def _perm_rows(wt):
    pad = jnp.zeros((D_IN_PAD - D_IN_PROJ, wt.shape[1]), wt.dtype)
    return jnp.concatenate([wt[:DT_LO], wt[DT_HI:], wt[DT_LO:DT_HI], pad], axis=0)


def _unperm_rows(dwt):
    n = D_IN_PROJ - (DT_HI - DT_LO)
    return jnp.concatenate([dwt[:DT_LO], dwt[n:D_IN_PROJ], dwt[DT_LO:n]], axis=0)


def _pad_lanes(v):
    return jnp.pad(v, ((0, 0), (0, LANE - v.shape[1])))[:, None]


def _block_diag(w):
    l, g, n, _ = w.shape
    out = jnp.zeros((l, g * n, g * n), w.dtype)
    for i in range(g):
        out = out.at[:, i * n:(i + 1) * n, i * n:(i + 1) * n].set(w[:, i])
    return out


def _pack(groups):
    flat = []
    for grp in groups:
        parts = [a.reshape(-1) for a in (grp if isinstance(grp, (list, tuple)) else [grp])]
        n = sum(p.shape[0] for p in parts)
        if -n % LANE:
            parts.append(jnp.zeros((-n % LANE,), parts[0].dtype))
        flat += parts
    return jnp.concatenate(flat).reshape(-1, LANE)


def _unpack(buf, shapes):
    out = []
    lo = 0
    buf = buf.reshape(-1)
    for shp in shapes:
        n = 1
        for k in shp:
            n *= k
        out.append(buf[lo:lo + n].reshape(shp))
        lo += n + (-n % LANE)
    return out


def small_params(w, conv_w_full):
    return dict(
        n1w=w["norm1_w"][:, None], cw=jnp.pad(conv_w_full, ((0, 0), (0, 8 - CONV_WIDTH), (0, 0))),
        cb=w["conv_b"][:, None], dtb=_pad_lanes(w["dt_bias"]), alog=_pad_lanes(w["a_log"]), dsk=_pad_lanes(w["d_skip"]),
        snw=w["ssd_norm_w"][:, None], wblk=_block_diag(w["pool_w"]), pb=w["pool_b"].reshape(-1, 1, POOL_WIDTH),
        ps=w["pool_scale"][:, None], n2w=w["norm2_w"][:, None])


MIX = ("w_in", "w_out")
FFN = ("w_gate", "w_up", "w_down")


def layer_params(small, l):
    return {k: v[l] for k, v in small.items()}


def mix_weights(whole):
    return _perm_rows(whole[0]), whole[1]


def _slabs(g):
    return g.reshape(N_DEV, -1, g.shape[-1])


def _layer_fwd(x, p, mix, ffn=None, ffn_shards=None, next_mix_shards=None):
    z, xbc, qkv, pp, dtr, h1 = inproj_fwd(x, p["n1w"], mix[0])
    u = conv_fwd(xbc, p["cw"], p["cb"])
    y_ssd, st = ssd_fwd(u, z, dtr, p["dtb"], p["alog"], p["dsk"], p["snw"])
    (o, tot), (blocks,) = sb_fwd(qkv, [gather_ici(ffn_shards) if ffn_shards is not None else None])
    yp = pool_fwd(pp, p["wblk"], p["pb"], p["ps"])
    ycat = jnp.concatenate([y_ssd, o, yp], axis=1)
    (x_mid,), (blocks,) = outproj_fwd(ycat, mix[1], x, [gather_d2d(blocks) if blocks is not None else None])
    if blocks is not None:
        ffn = gathered(blocks)
    (x_out, g, uu), (nxt,) = ffn_fwd(x_mid, p["n2w"], *ffn,
                                     [gather_ici(next_mix_shards) if next_mix_shards is not None else None])
    sv = dict(x=x, z=z, xbc=xbc, qkv=qkv, pp=pp, dtr=dtr, h1=h1, u=u, st=st, tot=tot, ycat=ycat, x_mid=x_mid, g=g, uu=uu,
              w_in=mix[0], w_out=mix[1], wg=ffn[0], wu=ffn[1], wd=ffn[2])
    return x_out, sv, nxt


def _layer_bwd(dxo, sv, p, pending_mix=None, exchange=False):
    (dx_mid, dn2w, a, dg, du, h2), (sib,) = ffn_bwd(
        dxo, sv["x_mid"], sv["g"], sv["uu"], p["n2w"], sv["wg"], sv["wu"], sv["wd"],
        [scatter_d2d(pending_mix) if pending_mix is not None else None])
    sums_mix = chip_sums(pending_mix, sib) if pending_mix is not None else None
    gr = dict(norm2_w=dn2w[0], w_gate=mm_tn(dg, h2), w_up=mm_tn(du, h2), w_down=mm_tn(a, dxo))
    parts = [_slabs(gr[k]) for k in FFN] if exchange else None
    (dycat,), (sib,) = outproj_bwd(dx_mid, sv["w_out"], [scatter_d2d(parts) if exchange else None])
    sums_ffn = chip_sums(parts, sib) if exchange else None
    gr["w_out"] = mm_tn(sv["ycat"], dx_mid)
    dp, dwblk, dpb, dps = pool_bwd(sv["pp"], p["wblk"], p["pb"], p["ps"], dycat,
                                   (SSD_WIDTH + SB_WIDTH) // POOL_WIDTH)
    n = POOL_GROUP_DIM
    gr["pool_w"] = jnp.stack([dwblk[i * n:(i + 1) * n, i * n:(i + 1) * n] for i in range(len(POOL_WINDOWS))])
    gr["pool_b"] = dpb.reshape(len(POOL_WINDOWS), n)
    gr["pool_scale"] = dps[0]
    (dq, dk, dv), (got_ffn, got_mix) = sb_bwd(
        sv["qkv"], sv["tot"], dycat, SSD_WIDTH // LANE,
        [scatter_ici(sums_ffn) if exchange else None, scatter_ici(sums_mix) if sums_mix is not None else None])
    done_ffn = device_sums(sums_ffn, got_ffn) if exchange else None
    done_mix = device_sums(sums_mix, got_mix) if sums_mix is not None else None
    du_, dz, ddtr, ddtb, dalog, ddsk, dsnw = ssd_bwd(sv["u"], sv["z"], sv["dtr"], sv["st"], dycat, p["dtb"], p["alog"],
                                                     p["dsk"], p["snw"])
    gr.update(dt_bias=ddtb[0, :SSD_HEADS], a_log=dalog[0, :SSD_HEADS], d_skip=ddsk[0, :SSD_HEADS], ssd_norm_w=dsnw[0])
    dxbc, dcw, dcb = conv_bwd(sv["xbc"], p["cw"], p["cb"], du_)
    gr.update(conv_w=dcw[:CONV_WIDTH], conv_b=dcb[0])
    dproj = jnp.concatenate([dz, dxbc, dq, dk, dv, dp, ddtr], axis=1)
    dx, dn1w = inproj_bwd(dproj, sv["w_in"], sv["x"], p["n1w"], dx_mid)
    gr.update(norm1_w=dn1w[0], w_in=_unperm_rows(mm_tn(dproj, sv["h1"])))
    return dx, gr, done_ffn, done_mix


def local_step(x, tgt, params, weights, final_w):
    saved = []
    for p, (mix, ffn) in zip(params, weights):
        x, sv, _ = _layer_fwd(x, p, mix, ffn)
        saved.append(sv)
    loss, dx, dfw = head_loss(x, final_w[None], tgt)
    grads = []
    for p, sv in zip(reversed(params), reversed(saved)):
        dx, gr, _, _ = _layer_bwd(dx, sv, p)
        grads.append(gr)
    grads.reverse()
    return loss, dx, dfw[0], grads


WEIGHTS = ("norm1_w", "w_in", "conv_w", "conv_b", "dt_bias", "a_log", "d_skip", "ssd_norm_w", "pool_w", "pool_b",
           "pool_scale", "w_out", "norm2_w", "w_gate", "w_up", "w_down", "final_norm_w")
COL_SHARDED = ("w_in", "w_gate", "w_up")
ROW_SHARDED = ("w_out", "w_down")
SMALL = tuple(k for k in WEIGHTS if k not in COL_SHARDED + ROW_SHARDED)


def kernel(x, norm1_w, w_in, conv_w, conv_b, dt_bias, a_log, d_skip, ssd_norm_w, pool_w, pool_b, pool_scale, w_out, norm2_w, w_gate, w_up, w_down, final_norm_w, loss_target, m_norm1_w, m_w_in, m_conv_w, m_conv_b, m_dt_bias, m_a_log, m_d_skip, m_ssd_norm_w, m_pool_w, m_pool_b, m_pool_scale, m_w_out, m_norm2_w, m_w_gate, m_w_up, m_w_down, m_final_norm_w, v_norm1_w, v_w_in, v_conv_w, v_conv_b, v_dt_bias, v_a_log, v_d_skip, v_ssd_norm_w, v_pool_w, v_pool_b, v_pool_scale, v_w_out, v_norm2_w, v_w_gate, v_w_up, v_w_down, v_final_norm_w):
    args = dict(locals())
    w = {k: args[k] for k in WEIGHTS}
    m = {k: args["m_" + k] for k in WEIGHTS}
    v = {k: args["v_" + k] for k in WEIGHTS}
    depth = w_in.shape[0]
    dev = _dev(_me())
    n_cw = conv_w.shape[-1]

    shards = {k: (jnp.swapaxes(w[k], 1, 2) if k in COL_SHARDED else w[k]).astype(BF16) for k in MIX + FFN}
    conv_w_full = all_gather([jnp.swapaxes(conv_w, 0, 2).reshape(n_cw, -1)])[0]
    conv_w_full = jnp.swapaxes(conv_w_full.reshape(N_DEV * n_cw, CONV_WIDTH, depth), 0, 2)
    small = small_params(w, conv_w_full)
    xs = x[0]
    params, saved = [layer_params(small, l) for l in range(depth)], []
    mix = mix_weights(all_gather([shards[k][0] for k in MIX]))
    for l in range(depth):
        xs, sv, nxt = _layer_fwd(xs, params[l], mix, ffn_shards=[shards[k][l] for k in FFN],
                                 next_mix_shards=[shards[k][l + 1] for k in MIX] if l + 1 < depth else None)
        saved.append(sv)
        if nxt is not None:
            mix = mix_weights(gathered(gather_d2d(nxt).run("gather_d2d")))
    loss, dx, dfw = head_loss(xs, final_norm_w[None], loss_target[0])
    layer_grads = [None] * depth
    reduced = [dict() for _ in range(depth)]
    pending = None
    for l in reversed(range(depth)):
        dx, layer_grads[l], done_ffn, done_mix = _layer_bwd(dx, saved[l], params[l], pending, exchange=True)
        reduced[l].update(zip(FFN, done_ffn))
        if pending is not None:
            reduced[l + 1].update(zip(MIX, done_mix))
        pending = [_slabs(layer_grads[l][k]) for k in MIX]
    reduced[0].update(zip(MIX, reduce_scatter(pending)))

    grads, native = {}, {}
    for k in MIX + FFN:
        native[k] = jnp.stack([reduced[l][k] for l in range(depth)])
        grads[k] = jnp.swapaxes(native[k], 1, 2) if k in COL_SHARDED else native[k]
    layered = [k for k in SMALL if k != "final_norm_w"]
    small_shapes = [(1, LANE)] + [(depth,) + layer_grads[0][k].shape for k in layered] + [dfw[0].shape]
    packed = _pack([loss] + [[layer_grads[l][k] for l in range(depth)] for k in layered] + [dfw[0]])
    summed = _unpack(all_reduce_small(packed), small_shapes)
    loss = summed[0][0, 0]
    grads.update(zip(layered + ["final_norm_w"], summed[1:]))
    grads["conv_w"] = lax.dynamic_slice_in_dim(grads["conv_w"], dev * n_cw, n_cw, axis=2)

    delta, new_m, new_v = {}, {}, {}
    for k in MIX + FFN:
        if k in ("w_gate", "w_up"):
            wt, mt, vt = (jnp.swapaxes(t, 1, 2) for t in (w[k], m[k], v[k]))
            delta[k], new_m[k], new_v[k] = [jnp.swapaxes(o, 1, 2) for o in adamw(wt, native[k], mt, vt)]
        else:
            delta[k], new_m[k], new_v[k] = adamw(w[k], grads[k], m[k], v[k])
    two_d = lambda a: a.reshape(1, -1) if a.ndim == 1 else a
    outs = adamw_small(*[[two_d(t[k]) for k in SMALL] for t in (w, grads, m, v)])
    for dst, arrs in zip((delta, new_m, new_v), outs):
        dst.update({k: a.reshape(w[k].shape) for k, a in zip(SMALL, arrs)})
    return (loss, dx[None], *[grads[k] for k in WEIGHTS], *[delta[k] for k in WEIGHTS],
            *[new_m[k] for k in WEIGHTS], *[new_v[k] for k in WEIGHTS])
```

```python
import functools

import jax
import jax.numpy as jnp
from jax import lax
from jax.experimental import pallas as pl
from jax.experimental.pallas import tpu as pltpu

F32 = jnp.float32
BF16 = jnp.bfloat16
HIGHEST = lax.Precision.HIGHEST
MESH = pl.DeviceIdType.MESH

EPS = 1e-6
D_MODEL = 1024
SSD_WIDTH = 512
SSD_HEADS = 8
HEAD_DIM = 64
D_STATE = 128
CONV_WIDTH = 4
CONV_DIM = 1024
SB_WIDTH = 256
POOL_WIDTH = 256
POOL_WINDOWS = (2, 4, 8, 16)
D_IN_PROJ = 2568
D_FF = 2816
N_DEV = 8
DEPTH = 4
SEG = (512, 1024, 768, 256, 128)
D_IN_PAD = sum(SEG)
DT_LO, DT_HI = 1536, 1544

LANE = 128
BLK = 128
ROW_TILE = 256
VMEM_LIMIT = 56 * 2**20

ADAM_LR, ADAM_B1, ADAM_B2, ADAM_EPS, ADAM_WD, ADAM_STEP = 0.001, 0.9, 0.999, 1e-08, 0.01, 10


def _params(n_axes=1, vmem=None):
    return pltpu.CompilerParams(dimension_semantics=("arbitrary",) * n_axes, vmem_limit_bytes=vmem)


def _dot(a, b, dims, exact=False):
    if exact:
        return lax.dot_general(a.astype(F32), b.astype(F32), (dims, ((), ())), precision=HIGHEST,
                               preferred_element_type=F32)
    return lax.dot_general(a.astype(BF16), b.astype(BF16), (dims, ((), ())), preferred_element_type=F32)


def dot_nn(a, b, exact=False):
    return _dot(a, b, ((1,), (0,)), exact)


def dot_nt(a, b, exact=False):
    return _dot(a, b, ((1,), (1,)), exact)


def dot_tn(a, b, exact=False):
    return _dot(a, b, ((0,), (0,)), exact)


def _iota(shape, axis):
    return lax.broadcasted_iota(jnp.int32, shape, axis)


def _lane_col(x, h):
    return jnp.sum(jnp.where(_iota(x.shape, 1) == h, x, 0.0), axis=1, keepdims=True)


def _sub_row(x, h):
    return jnp.sum(jnp.where(_iota(x.shape, 0) == h, x, 0.0), axis=0, keepdims=True)


def _sigmoid(x):
    return 1.0 / (1.0 + jnp.exp(-x))


def _rms_fwd(x, w):
    r = lax.rsqrt(jnp.mean(x * x, axis=-1, keepdims=True) + EPS)
    return x * r * w


def _rms_bwd(x, w, dy):
    r = lax.rsqrt(jnp.mean(x * x, axis=-1, keepdims=True) + EPS)
    xh = x * r
    dxh = dy * w
    dx = r * (dxh - xh * jnp.mean(dxh * xh, axis=-1, keepdims=True))
    return dx, jnp.sum(dy * xh, axis=0, keepdims=True)


def _acc(ref, first, val):
    @pl.when(first)
    def _():
        ref[...] = val

    @pl.when(jnp.logical_not(first))
    def _():
        ref[...] += val


def _row_spec(tm, n):
    return pl.BlockSpec((tm, n), lambda i: (i, 0))


def _full_spec(shape):
    return pl.BlockSpec(shape, lambda *_: (0,) * len(shape))


def inproj_fwd(x, nw, w):
    s, d = x.shape
    tm = min(ROW_TILE, s)

    def body(x_ref, nw_ref, w_ref, z_ref, xbc_ref, qkv_ref, p_ref, dt_ref, h_ref):
        h = _rms_fwd(x_ref[...], nw_ref[...]).astype(BF16)
        h_ref[...] = h
        lo = 0
        for ref, n in zip((z_ref, xbc_ref, qkv_ref, p_ref, dt_ref), SEG):
            ref[...] = dot_nt(h, w_ref[lo:lo + n, :])
            lo += n

    return pl.pallas_call(
        body, name="inproj_fwd", grid=(s // tm,),
        in_specs=[_row_spec(tm, d), _full_spec((1, d)), _full_spec(w.shape)],
        out_specs=[_row_spec(tm, n) for n in SEG] + [_row_spec(tm, d)],
        out_shape=[jax.ShapeDtypeStruct((s, n), F32) for n in SEG] + [jax.ShapeDtypeStruct((s, d), BF16)],
        compiler_params=_params(1, VMEM_LIMIT),
    )(x, nw, w)


def inproj_bwd(dproj, w, x, nw, dres):
    s, d = x.shape
    tm = min(ROW_TILE, s)

    def body(dp_ref, w_ref, x_ref, nw_ref, dres_ref, dx_ref, dnw_ref):
        dh = dot_nn(dp_ref[...], w_ref[...])
        dx, dnw = _rms_bwd(x_ref[...], nw_ref[...], dh)
        dx_ref[...] = dres_ref[...] + dx
        _acc(dnw_ref, pl.program_id(0) == 0, dnw)

    return pl.pallas_call(
        body, name="inproj_bwd", grid=(s // tm,),
        in_specs=[_row_spec(tm, dproj.shape[1]), _full_spec(w.shape), _row_spec(tm, d), _full_spec((1, d)),
                  _row_spec(tm, d)],
        out_specs=[_row_spec(tm, d), _full_spec((1, d))],
        out_shape=[jax.ShapeDtypeStruct((s, d), F32), jax.ShapeDtypeStruct((1, d), F32)],
        compiler_params=_params(1, VMEM_LIMIT),
    )(dproj, w, x, nw, dres)


def outproj_fwd(y, w, res, sides=()):
    s, d = res.shape
    tm = min(ROW_TILE, s)

    def body(y_ref, w_ref, r_ref, o_ref):
        o_ref[...] = r_ref[...] + dot_nn(y_ref[...], w_ref[...])

    return hosted_call(
        body, sides, name="outproj_fwd", grid=(s // tm,),
        in_specs=[_row_spec(tm, y.shape[1]), _full_spec(w.shape), _row_spec(tm, d)],
        out_specs=[_row_spec(tm, d)], out_shape=[jax.ShapeDtypeStruct((s, d), F32)], scratch_shapes=[],
        compiler_params=_params(1, VMEM_LIMIT), operands=(y, w, res))


def outproj_bwd(dx, w, sides=()):
    s, d = dx.shape
    tm = min(ROW_TILE, s)

    def body(dx_ref, w_ref, o_ref):
        o_ref[...] = dot_nt(dx_ref[...], w_ref[...])

    return hosted_call(
        body, sides, name="outproj_bwd", grid=(s // tm,),
        in_specs=[_row_spec(tm, d), _full_spec(w.shape)],
        out_specs=[_row_spec(tm, w.shape[0])], out_shape=[jax.ShapeDtypeStruct((s, w.shape[0]), F32)],
        scratch_shapes=[], compiler_params=_params(1, VMEM_LIMIT), operands=(dx, w))


def ffn_fwd(x, nw, wg, wu, wd, sides=()):
    s, d = x.shape
    f = wg.shape[0]
    tm = min(ROW_TILE, s)

    def body(x_ref, nw_ref, wg_ref, wu_ref, wd_ref, o_ref, g_ref, u_ref):
        xv = x_ref[...]
        h = _rms_fwd(xv, nw_ref[...]).astype(BF16)
        g = dot_nt(h, wg_ref[...])
        u = dot_nt(h, wu_ref[...])
        g_ref[...] = g.astype(BF16)
        u_ref[...] = u.astype(BF16)
        o_ref[...] = xv + dot_nn(g * _sigmoid(g) * u, wd_ref[...])

    return hosted_call(
        body, sides, name="ffn_fwd", grid=(s // tm,),
        in_specs=[_row_spec(tm, d), _full_spec((1, d)), _full_spec(wg.shape), _full_spec(wu.shape),
                  _full_spec(wd.shape)],
        out_specs=[_row_spec(tm, d), _row_spec(tm, f), _row_spec(tm, f)],
        out_shape=[jax.ShapeDtypeStruct((s, d), F32), jax.ShapeDtypeStruct((s, f), BF16),
                   jax.ShapeDtypeStruct((s, f), BF16)],
        scratch_shapes=[], compiler_params=_params(1, VMEM_LIMIT), operands=(x, nw, wg, wu, wd))


def ffn_bwd(dxo, x, g, u, nw, wg, wu, wd, sides=()):
    s, d = x.shape
    f = wg.shape[0]
    tm = min(ROW_TILE, s)

    def body(dxo_ref, x_ref, g_ref, u_ref, nw_ref, wg_ref, wu_ref, wd_ref, dx_ref, dnw_ref, a_ref, dg_ref,
             du_ref, h_ref):
        dxo_v = dxo_ref[...]
        xv = x_ref[...]
        da = dot_nt(dxo_v, wd_ref[...])
        gv = g_ref[...].astype(F32)
        uv = u_ref[...].astype(F32)
        sg = _sigmoid(gv)
        sl = gv * sg
        a_ref[...] = (sl * uv).astype(BF16)
        dg = (da * uv * (sg * (1.0 + gv * (1.0 - sg)))).astype(BF16)
        du = (da * sl).astype(BF16)
        dg_ref[...] = dg
        du_ref[...] = du
        dh = dot_nn(dg, wg_ref[...]) + dot_nn(du, wu_ref[...])
        h_ref[...] = _rms_fwd(xv, nw_ref[...]).astype(BF16)
        dx, dnw = _rms_bwd(xv, nw_ref[...], dh)
        dx_ref[...] = dxo_v + dx
        _acc(dnw_ref, pl.program_id(0) == 0, dnw)

    return hosted_call(
        body, sides, name="ffn_bwd", grid=(s // tm,),
        in_specs=[_row_spec(tm, d), _row_spec(tm, d), _row_spec(tm, f), _row_spec(tm, f), _full_spec((1, d)),
                  _full_spec(wg.shape), _full_spec(wu.shape), _full_spec(wd.shape)],
        out_specs=[_row_spec(tm, d), _full_spec((1, d)), _row_spec(tm, f), _row_spec(tm, f), _row_spec(tm, f),
                   _row_spec(tm, d)],
        out_shape=[jax.ShapeDtypeStruct((s, d), F32), jax.ShapeDtypeStruct((1, d), F32),
                   jax.ShapeDtypeStruct((s, f), BF16), jax.ShapeDtypeStruct((s, f), BF16),
                   jax.ShapeDtypeStruct((s, f), BF16), jax.ShapeDtypeStruct((s, d), BF16)],
        scratch_shapes=[], compiler_params=_params(1, VMEM_LIMIT), operands=(dxo, x, g, u, nw, wg, wu, wd))


def _tile(n, cap=256):
    best = LANE
    for t in range(LANE, cap + 1, LANE):
        if n % t == 0:
            best = t
    return best


def mm_tn(a, b):
    s, k = a.shape
    n = b.shape[1]
    tk = _tile(k)

    def body(a_ref, b_ref, o_ref):
        o_ref[...] = dot_nn(a_ref[...].astype(BF16).T, b_ref[...]).astype(BF16)

    return pl.pallas_call(
        body, name="mm_tn", grid=(k // tk,),
        in_specs=[pl.BlockSpec((s, tk), lambda i: (0, i)), _full_spec((s, n))],
        out_specs=pl.BlockSpec((tk, n), lambda i: (i, 0)), out_shape=jax.ShapeDtypeStruct((k, n), BF16),
        compiler_params=_params(1, VMEM_LIMIT),
    )(a, b)


def head_loss(x, fw, tgt):
    s, d = x.shape
    tm = min(ROW_TILE, s)

    def body(x_ref, fw_ref, t_ref, loss_ref, dx_ref, dfw_ref):
        xv = x_ref[...]
        err = _rms_fwd(xv, fw_ref[...]) - t_ref[...]
        part = jnp.zeros((1, LANE), F32) + 0.5 * jnp.sum(err * err) / d
        dx, dfw = _rms_bwd(xv, fw_ref[...], err / d)
        dx_ref[...] = dx
        first = pl.program_id(0) == 0
        _acc(loss_ref, first, part)
        _acc(dfw_ref, first, dfw)

    return pl.pallas_call(
        body, name="head_loss", grid=(s // tm,),
        in_specs=[_row_spec(tm, d), _full_spec((1, d)), _row_spec(tm, d)],
        out_specs=[_full_spec((1, LANE)), _row_spec(tm, d), _full_spec((1, d))],
        out_shape=[jax.ShapeDtypeStruct((1, LANE), F32), jax.ShapeDtypeStruct((s, d), F32),
                   jax.ShapeDtypeStruct((1, d), F32)],
        compiler_params=_params(1),
    )(x, fw, tgt)


def _conv_pre(ext, cw_ref, cb_ref):
    shifted = [pltpu.roll(ext, CONV_WIDTH - 1 - i, 0)[BLK:] if i < CONV_WIDTH - 1 else ext[BLK:]
               for i in range(CONV_WIDTH)]
    acc = cb_ref[...] + sum(cw_ref[i:i + 1, :] * shifted[i] for i in range(CONV_WIDTH))
    return acc, shifted


def conv_fwd(xbc, cw, cb):
    s, n = xbc.shape

    def body(cur_ref, prev_ref, cw_ref, cb_ref, o_ref):
        prev = jnp.where(pl.program_id(0) > 0, prev_ref[...], 0.0)
        acc, _ = _conv_pre(jnp.concatenate([prev, cur_ref[...]], axis=0), cw_ref, cb_ref)
        o_ref[...] = acc * _sigmoid(acc)

    return pl.pallas_call(
        body, name="conv_fwd", grid=(s // BLK,),
        in_specs=[pl.BlockSpec((BLK, n), lambda c: (c, 0)), pl.BlockSpec((BLK, n), lambda c: (jnp.maximum(c - 1, 0), 0)),
                  _full_spec(cw.shape), _full_spec((1, n))],
        out_specs=pl.BlockSpec((BLK, n), lambda c: (c, 0)), out_shape=jax.ShapeDtypeStruct((s, n), F32),
        compiler_params=_params(1),
    )(xbc, xbc, cw, cb)


def conv_bwd(xbc, cw, cb, du):
    s, n = xbc.shape
    nb = s // BLK

    def body(cur_ref, prev_ref, cw_ref, cb_ref, du_ref, dx_ref, dcw_ref, dcb_ref, nxt_ref):
        i = pl.program_id(0)
        c = nb - 1 - i
        prev = jnp.where(c > 0, prev_ref[...], 0.0)
        acc, shifted = _conv_pre(jnp.concatenate([prev, cur_ref[...]], axis=0), cw_ref, cb_ref)
        sg = _sigmoid(acc)
        dacc = du_ref[...] * (sg * (1.0 + acc * (1.0 - sg)))

        @pl.when(i == 0)
        def _():
            nxt_ref[...] = jnp.zeros_like(nxt_ref)
            dcw_ref[...] = jnp.zeros_like(dcw_ref)
            dcb_ref[...] = jnp.zeros_like(dcb_ref)

        dcb_ref[...] += jnp.sum(dacc, axis=0, keepdims=True)
        for t in range(CONV_WIDTH):
            dcw_ref[t:t + 1, :] += jnp.sum(dacc * shifted[t], axis=0, keepdims=True)
        ext = jnp.concatenate([dacc, nxt_ref[...]], axis=0)
        dx = cw_ref[CONV_WIDTH - 1:CONV_WIDTH, :] * dacc
        for t in range(CONV_WIDTH - 1):
            dx += cw_ref[t:t + 1, :] * pltpu.roll(ext, 2 * BLK - (CONV_WIDTH - 1 - t), 0)[:BLK]
        dx_ref[...] = dx
        nxt_ref[...] = dacc

    rev = lambda i: (nb - 1 - i, 0)
    return pl.pallas_call(
        body, name="conv_bwd", grid=(nb,),
        in_specs=[pl.BlockSpec((BLK, n), rev), pl.BlockSpec((BLK, n), lambda i: (jnp.maximum(nb - 2 - i, 0), 0)),
                  _full_spec(cw.shape), _full_spec((1, n)), pl.BlockSpec((BLK, n), rev)],
        out_specs=[pl.BlockSpec((BLK, n), rev), _full_spec((8, n)), _full_spec((1, n))],
        out_shape=[jax.ShapeDtypeStruct((s, n), F32), jax.ShapeDtypeStruct((8, n), F32),
                   jax.ShapeDtypeStruct((1, n), F32)],
        scratch_shapes=[pltpu.VMEM((BLK, n), F32)],
        compiler_params=_params(1),
    )(xbc, xbc, cw, cb, du)


N_PAIR = SSD_HEADS // 2
B_LO = SSD_WIDTH
C_LO = SSD_WIDTH + 2 * D_STATE


def _softplus(x):
    return jnp.maximum(x, 0.0) + jnp.log(1.0 + jnp.exp(-jnp.abs(x)))


def _ssd_chunk(u_ref, dt_ref, dtb_ref, alog_ref):
    shape = (BLK, BLK)
    tri = _iota(shape, 1) <= _iota(shape, 0)
    pre = dt_ref[...] + dtb_ref[...]
    dt = _softplus(pre)
    a = -jnp.exp(alog_ref[...])
    acum = dot_nn(tri.astype(F32), dt * a, exact=True)
    acum_t = acum.T
    last = _sub_row(acum, BLK - 1)
    heads = []
    for h in range(SSD_HEADS):
        col = _lane_col(acum, h)
        seg = jnp.where(tri, col - _sub_row(acum_t, h), -1e30)
        heads.append(dict(col=col, dm=jnp.exp(seg), dt=_lane_col(dt, h), last=_lane_col(last, h)))
    return tri, pre, dt, a, heads


def _pair_mix(lo_mask, v0, v1):
    return jnp.where(lo_mask, v0, v1)


def ssd_fwd(u, z, dtr, dtb, alog, dsk, nw):
    s = u.shape[0]
    nc = s // BLK

    def body(u_ref, z_ref, dt_ref, dtb_ref, alog_ref, dsk_ref, nw_ref, y_ref, st_ref, s_ref):
        @pl.when(pl.program_id(0) == 0)
        def _():
            s_ref[...] = jnp.zeros_like(s_ref)

        _, _, _, _, heads = _ssd_chunk(u_ref, dt_ref, dtb_ref, alog_ref)
        lo_lane = _iota((BLK, LANE), 1) < HEAD_DIM
        lo_sub = _iota((BLK, LANE), 0) < HEAD_DIM
        ys = []
        for p in range(N_PAIR):
            g = p // 2
            h0, h1 = heads[2 * p], heads[2 * p + 1]
            bg = u_ref[:, B_LO + g * D_STATE:B_LO + (g + 1) * D_STATE]
            cg = u_ref[:, C_LO + g * D_STATE:C_LO + (g + 1) * D_STATE]
            xs = u_ref[:, p * LANE:(p + 1) * LANE]
            xp = xs * _pair_mix(lo_lane, h0["dt"], h1["dt"])
            gm = dot_nt(cg, bg)
            yd = _pair_mix(lo_lane, dot_nn(gm * h0["dm"], xp), dot_nn(gm * h1["dm"], xp))
            sp = s_ref[p]
            st_ref[0, p] = sp
            yo = _pair_mix(lo_lane, jnp.exp(h0["col"]), jnp.exp(h1["col"])) * dot_nt(cg, sp)
            dskp = _pair_mix(lo_lane, _lane_col(dsk_ref[...], 2 * p), _lane_col(dsk_ref[...], 2 * p + 1))
            ys.append(yd + yo + xs * dskp)
            wp = _pair_mix(lo_lane, jnp.exp(h0["last"] - h0["col"]), jnp.exp(h1["last"] - h1["col"]))
            el = _pair_mix(lo_sub, jnp.exp(h0["last"]), jnp.exp(h1["last"]))
            s_ref[p] = el * sp + dot_tn(wp * xp, bg)
        y = jnp.concatenate(ys, axis=1)
        zv = z_ref[...]
        y_ref[...] = _rms_fwd(y * zv * _sigmoid(zv), nw_ref[...])

    vec = _full_spec((1, LANE))
    return pl.pallas_call(
        body, name="ssd_fwd", grid=(nc,),
        in_specs=[_row_spec(BLK, CONV_DIM), _row_spec(BLK, SSD_WIDTH), _row_spec(BLK, LANE), vec, vec, vec,
                  _full_spec((1, SSD_WIDTH))],
        out_specs=[_row_spec(BLK, SSD_WIDTH), pl.BlockSpec((1, N_PAIR, LANE, D_STATE), lambda c: (c, 0, 0, 0))],
        out_shape=[jax.ShapeDtypeStruct((s, SSD_WIDTH), F32), jax.ShapeDtypeStruct((nc, N_PAIR, LANE, D_STATE), F32)],
        scratch_shapes=[pltpu.VMEM((N_PAIR, LANE, D_STATE), F32)],
        compiler_params=_params(1),
    )(u, z, dtr, dtb, alog, dsk, nw)


def ssd_bwd(u, z, dtr, st, dyo, dtb, alog, dsk, nw):
    s = u.shape[0]
    nc = s // BLK

    def body(u_ref, z_ref, dt_ref, st_ref, dyo_ref, dtb_ref, alog_ref, dsk_ref, nw_ref,
             du_ref, dz_ref, ddt_ref, ddtb_ref, dalog_ref, ddsk_ref, dnw_ref, ds_ref):
        first = pl.program_id(0) == 0

        @pl.when(first)
        def _():
            ds_ref[...] = jnp.zeros_like(ds_ref)

        tri, pre, dt, a, heads = _ssd_chunk(u_ref, dt_ref, dtb_ref, alog_ref)
        shape = (BLK, LANE)
        lane = _iota(shape, 1)
        lo_lane = lane < HEAD_DIM
        lo_sub = _iota(shape, 0) < HEAD_DIM
        pairs = []
        ys = []
        for p in range(N_PAIR):
            g = p // 2
            h0, h1 = heads[2 * p], heads[2 * p + 1]
            bg = u_ref[:, B_LO + g * D_STATE:B_LO + (g + 1) * D_STATE]
            cg = u_ref[:, C_LO + g * D_STATE:C_LO + (g + 1) * D_STATE]
            xs = u_ref[:, p * LANE:(p + 1) * LANE]
            dtp = _pair_mix(lo_lane, h0["dt"], h1["dt"])
            xp = xs * dtp
            gm = dot_nt(cg, bg)
            m0, m1 = gm * h0["dm"], gm * h1["dm"]
            sp = st_ref[0, p]
            eap = _pair_mix(lo_lane, jnp.exp(h0["col"]), jnp.exp(h1["col"]))
            yo = eap * dot_nt(cg, sp)
            dskp = _pair_mix(lo_lane, _lane_col(dsk_ref[...], 2 * p), _lane_col(dsk_ref[...], 2 * p + 1))
            ys.append(_pair_mix(lo_lane, dot_nn(m0, xp), dot_nn(m1, xp)) + yo + xs * dskp)
            pairs.append(dict(bg=bg, cg=cg, xs=xs, dtp=dtp, xp=xp, gm=gm, m=(m0, m1), sp=sp, eap=eap, yo=yo, dskp=dskp))
        y = jnp.concatenate(ys, axis=1)
        zv = z_ref[...]
        sz = _sigmoid(zv)
        gate = zv * sz
        dyg, dnw = _rms_bwd(y * gate, nw_ref[...], dyo_ref[...])
        _acc(dnw_ref, first, dnw)
        dy = dyg * gate
        dz_ref[...] = dyg * y * (sz * (1.0 + zv * (1.0 - sz)))

        zeros = jnp.zeros(shape, F32)
        dacum_col = zeros
        dacum_row = zeros
        ddt = zeros
        ddsk = jnp.zeros((1, LANE), F32)
        dlast = jnp.zeros((1, LANE), F32)
        head_row = _iota((1, LANE), 1)
        sub = _iota(shape, 0)
        db = [zeros, zeros]
        dc = [zeros, zeros]
        for p in range(N_PAIR):
            g = p // 2
            q = pairs[p]
            dyp = dy[:, p * LANE:(p + 1) * LANE]
            dsn = ds_ref[p]
            t = dyp * q["xs"]
            dxs = dyp * q["dskp"]
            dcs = dyp * q["eap"]
            dc[g] = dc[g] + dot_nn(dcs, q["sp"])
            dsp = dot_tn(dcs, q["cg"])
            dea = dyp * q["yo"]
            elp = _pair_mix(lo_sub, jnp.exp(heads[2 * p]["last"]), jnp.exp(heads[2 * p + 1]["last"]))
            dsp = dsp + elp * dsn
            dels = dsn * q["sp"] * elp
            wp = _pair_mix(lo_lane, jnp.exp(heads[2 * p]["last"] - heads[2 * p]["col"]),
                           jnp.exp(heads[2 * p + 1]["last"] - heads[2 * p + 1]["col"]))
            dv = dot_nt(q["bg"], dsn)
            db[g] = db[g] + dot_nn(wp * q["xp"], dsn)
            dxp = dv * wp
            dwv = dv * q["xp"] * wp
            dgm = zeros
            for k in range(2):
                h = 2 * p + k
                mine = lo_lane if k == 0 else jnp.logical_not(lo_lane)
                mine_sub = lo_sub if k == 0 else jnp.logical_not(lo_sub)
                dyh = jnp.where(mine, dyp, 0.0)
                dm = dot_nt(dyh, q["xp"])
                dxp = dxp + dot_tn(q["m"][k], dyh)
                dgm = dgm + dm * heads[h]["dm"]
                e = dm * q["m"][k]
                onehot = lane == h
                dw_col = jnp.sum(jnp.where(mine, dwv, 0.0), axis=1, keepdims=True)
                col = (jnp.sum(e, axis=1, keepdims=True) + jnp.sum(jnp.where(mine, dea, 0.0), axis=1, keepdims=True)
                       - dw_col)
                dacum_col = dacum_col + jnp.where(onehot, col, 0.0)
                dacum_row = dacum_row - jnp.where(sub == h, jnp.sum(e, axis=0, keepdims=True), 0.0)
                dl = jnp.sum(dw_col) + jnp.sum(jnp.where(mine_sub, dels, 0.0))
                dlast = dlast + jnp.where(head_row == h, dl, 0.0)
                ddsk = ddsk + jnp.where(head_row == h, jnp.sum(jnp.where(mine, t, 0.0)), 0.0)
            dc[g] = dc[g] + dot_nn(dgm, q["bg"])
            db[g] = db[g] + dot_tn(dgm, q["cg"])
            dxs = dxs + dxp * q["dtp"]
            tt = dxp * q["xs"]
            for k in range(2):
                mine = lo_lane if k == 0 else jnp.logical_not(lo_lane)
                ddt = ddt + jnp.where(lane == 2 * p + k, jnp.sum(jnp.where(mine, tt, 0.0), axis=1, keepdims=True), 0.0)
            du_ref[:, p * LANE:(p + 1) * LANE] = dxs
            ds_ref[p] = dsp
        for g in range(2):
            du_ref[:, B_LO + g * D_STATE:B_LO + (g + 1) * D_STATE] = db[g]
            du_ref[:, C_LO + g * D_STATE:C_LO + (g + 1) * D_STATE] = dc[g]
        dacum = dacum_col + dacum_row.T + jnp.where(sub == BLK - 1, dlast, 0.0)
        dda = dot_tn(tri.astype(F32), dacum, exact=True)
        ddt = ddt + dda * a
        _acc(dalog_ref, first, jnp.sum(dda * dt, axis=0, keepdims=True) * a)
        dpre = ddt * _sigmoid(pre)
        ddt_ref[...] = dpre
        _acc(ddtb_ref, first, jnp.sum(dpre, axis=0, keepdims=True))
        _acc(ddsk_ref, first, ddsk)

    rev = lambda i: (nc - 1 - i, 0)
    vec = _full_spec((1, LANE))
    rows = lambda n: pl.BlockSpec((BLK, n), rev)
    return pl.pallas_call(
        body, name="ssd_bwd", grid=(nc,),
        in_specs=[rows(CONV_DIM), rows(SSD_WIDTH), rows(LANE),
                  pl.BlockSpec((1, N_PAIR, LANE, D_STATE), lambda i: (nc - 1 - i, 0, 0, 0)), rows(SSD_WIDTH),
                  vec, vec, vec, _full_spec((1, SSD_WIDTH))],
        out_specs=[rows(CONV_DIM), rows(SSD_WIDTH), rows(LANE), vec, vec, vec, _full_spec((1, SSD_WIDTH))],
        out_shape=[jax.ShapeDtypeStruct((s, CONV_DIM), F32), jax.ShapeDtypeStruct((s, SSD_WIDTH), F32),
                   jax.ShapeDtypeStruct((s, LANE), F32)] + [jax.ShapeDtypeStruct((1, LANE), F32)] * 3
        + [jax.ShapeDtypeStruct((1, SSD_WIDTH), F32)],
        scratch_shapes=[pltpu.VMEM((N_PAIR, LANE, D_STATE), F32)],
        compiler_params=_params(1),
    )(u, z, dtr, st, dyo, dtb, alog, dsk, nw)


SB_PAIRS = SB_WIDTH // LANE
SB_SCALE = HEAD_DIM ** -0.5


SB_TQ = 256


def _sb_tq(s):
    return min(SB_TQ, s)


def _sb_stack(x):
    lo_lane = _iota(x.shape, 1) < HEAD_DIM
    return jnp.concatenate([jnp.where(lo_lane, x, 0.0), jnp.where(lo_lane, 0.0, x)], axis=0)


def _sb_unstack(x2):
    tq = x2.shape[0] // 2
    lo_lane = _iota((tq, LANE), 1) < HEAD_DIM
    return jnp.where(lo_lane, x2[:tq], x2[tq:])


def _sb_logits(q2, kj, row0, col0, masked):
    shape = (q2.shape[0], BLK)
    tq = shape[0] // 2
    z = dot_nt(q2, kj)
    t = jnp.log(1.0 + jnp.exp(-jnp.abs(z)))
    ls = jnp.minimum(z, 0.0) - t
    lk = jnp.minimum(-z, 0.0) - t
    if not masked:
        return None, ls, lk
    row = _iota(shape, 0)
    valid = (col0 + _iota(shape, 1)) < (row0 + jnp.where(row < tq, row, row - tq))
    return valid, ls, jnp.where(valid, lk, 0.0)


def _sb_where(valid, x):
    return x if valid is None else jnp.where(valid, x, 0.0)


def _sums(x, mask2, parts):
    acc = None
    rest = x
    for _ in range(parts):
        term = rest.astype(BF16)
        rest = rest - term.astype(F32)
        d = lax.dot_general(term, mask2, (((1,), (0,)), ((), ())), preferred_element_type=F32)
        acc = d if acc is None else acc + d
    return acc[:, :BLK], acc[:, BLK:]


def _mask2(cond):
    return jnp.concatenate([cond.astype(BF16), jnp.ones(cond.shape, BF16)], axis=1)


def _sb_specs(s):
    tq = _sb_tq(s)
    qspec = pl.BlockSpec((tq, LANE), lambda p, i: (i, p))
    kspec = pl.BlockSpec((s, LANE), lambda p, i: (0, SB_PAIRS + p))
    vspec = pl.BlockSpec((s, LANE), lambda p, i: (0, 2 * SB_PAIRS + p))
    return qspec, kspec, vspec


SB_FLOOR = -104.0


def sb_fwd(qkv, sides=()):
    s = qkv.shape[0]
    tq = _sb_tq(s)
    kpq = tq // BLK

    def body(q_ref, k_ref, v_ref, o_ref, t_ref, n_ref, acc_ref):
        qi = pl.program_id(1)
        q2 = _sb_stack(q_ref[...] * SB_SCALE).astype(BF16)
        later = _mask2(_iota((BLK, BLK), 0) > _iota((BLK, BLK), 1))
        acc_ref[...] = jnp.zeros_like(acc_ref)

        def step(j, r, masked):
            rows = pl.ds(pl.multiple_of(j * BLK, BLK), BLK)
            valid, ls, lk = _sb_logits(q2, k_ref[rows, :], qi * tq, j * BLK, masked)
            after, total = _sums(lk, later, 2)
            w = _sb_where(valid, jnp.exp(ls + r + after))
            acc_ref[...] += dot_nn(w, v_ref[rows, :])
            return r + total

        r = jnp.zeros((2 * tq, LANE), F32)
        for d in reversed(range(kpq)):
            r = step(kpq * qi + d, r, True)

        def tile(g, r):
            for d in reversed(range(kpq)):
                r = step(kpq * (qi - 1 - g) + d, r, False)
            return r

        n, r = lax.while_loop(lambda c: jnp.logical_and(c[0] < qi, jnp.max(c[1]) > SB_FLOOR),
                              lambda c: (c[0] + 1, tile(c[0], c[1])), (jnp.int32(0), r))
        o_ref[...] = _sb_unstack(acc_ref[...])
        t_ref[...] = jnp.concatenate([r[:tq], r[tq:]], axis=1)
        n_ref[...] = jnp.zeros(n_ref.shape, F32) + n.astype(F32)

    return hosted_call(
        body, sides, name="sb_fwd", grid=(SB_PAIRS, s // tq),
        in_specs=list(_sb_specs(s)),
        out_specs=[pl.BlockSpec((tq, LANE), lambda p, i: (i, p)), pl.BlockSpec((tq, 2 * LANE), lambda p, i: (i, p)),
                   pl.BlockSpec((None, None, 8, LANE), lambda p, i: (p, i, 0, 0))],
        out_shape=[jax.ShapeDtypeStruct((s, SB_WIDTH), F32), jax.ShapeDtypeStruct((s, 2 * SB_WIDTH), F32),
                   jax.ShapeDtypeStruct((SB_PAIRS, s // tq, 8, LANE), F32)],
        scratch_shapes=[pltpu.VMEM((2 * tq, LANE), F32)],
        compiler_params=_params(2), operands=(qkv, qkv, qkv))


def sb_bwd(qkv, tot, swept, do, do_col=0, sides=()):
    s = qkv.shape[0]
    tq = _sb_tq(s)
    kpq = tq // BLK

    def body(q_ref, k_ref, v_ref, t_ref, n_ref, do_ref, dq_ref, dk_ref, dv_ref, acc_ref):
        qi = pl.program_id(1)
        n = jnp.clip(jnp.max(n_ref[...]).astype(jnp.int32), 0, qi)
        q2 = _sb_stack(q_ref[...] * SB_SCALE).astype(BF16)
        do2 = _sb_stack(do_ref[...]).astype(BF16)
        tot2 = jnp.concatenate([t_ref[:, :LANE], t_ref[:, LANE:]], axis=0)
        sq = (BLK, BLK)
        later = _mask2(_iota(sq, 0) > _iota(sq, 1))
        before = _mask2(_iota(sq, 0) < _iota(sq, 1))
        acc_ref[...] = jnp.zeros_like(acc_ref)

        @pl.when(qi == 0)
        def _():
            dk_ref[...] = jnp.zeros_like(dk_ref)
            dv_ref[...] = jnp.zeros_like(dv_ref)

        def step(j, carry, masked):
            rc, fc = carry
            rows = pl.ds(pl.multiple_of(j * BLK, BLK), BLK)
            kj = k_ref[rows, :]
            vj = v_ref[rows, :]
            valid, ls, lk = _sb_logits(q2, kj, qi * tq, j * BLK, masked)
            after, total = _sums(lk, later, 2)
            rc = rc - total
            w = _sb_where(valid, jnp.exp(ls + rc + after))
            e = w * dot_nt(do2, vj)
            f_in, f_tot = _sums(e, before, 2)
            sg = jnp.exp(ls)
            dz = _sb_where(valid, e * (1.0 - sg) - (fc + f_in) * sg)
            acc_ref[...] += dot_nn(dz, kj)
            dk_ref[rows, :] += dot_tn(dz, q2)
            dv_ref[rows, :] += dot_tn(w, do2)
            return rc, fc + f_tot

        def tile(g, carry):
            for d in range(kpq):
                carry = step(kpq * g + d, carry, False)
            return carry

        carry = lax.fori_loop(qi - n, qi, tile, (tot2, jnp.zeros((2 * tq, LANE), F32)))
        for d in range(kpq):
            carry = step(kpq * qi + d, carry, True)
        dq_ref[...] = SB_SCALE * _sb_unstack(acc_ref[...])

    qspec, kspec, vspec = _sb_specs(s)
    blk = pl.BlockSpec((tq, LANE), lambda p, i: (i, p))
    acc = pl.BlockSpec((s, LANE), lambda p, i: (0, p))
    return hosted_call(
        body, sides, name="sb_bwd", grid=(SB_PAIRS, s // tq),
        in_specs=[qspec, kspec, vspec, pl.BlockSpec((tq, 2 * LANE), lambda p, i: (i, p)),
                  pl.BlockSpec((None, None, 8, LANE), lambda p, i: (p, i, 0, 0)),
                  pl.BlockSpec((tq, LANE), lambda p, i: (i, do_col + p))],
        out_specs=[blk, acc, acc],
        out_shape=[jax.ShapeDtypeStruct((s, SB_WIDTH), F32)] * 3,
        scratch_shapes=[pltpu.VMEM((2 * tq, LANE), F32)],
        compiler_params=_params(2), operands=(qkv, qkv, qkv, tot, swept, do))


POOL_GROUP_DIM = POOL_WIDTH // len(POOL_WINDOWS)


assert all(w == 2 ** (i + 1) for i, w in enumerate(POOL_WINDOWS))


def _pool_inv(c):
    group = _iota((BLK, POOL_WIDTH), 1) // POOL_GROUP_DIM
    pos = c * BLK + _iota((BLK, POOL_WIDTH), 0)
    win = jnp.zeros((BLK, POOL_WIDTH), jnp.int32)
    for gi, wn in enumerate(POOL_WINDOWS):
        win = jnp.where(group == gi, wn, win)
    return 1.0 / jnp.minimum(pos + 1, win).astype(F32)


def _window_sums(ext, trailing):
    group = _iota(ext.shape, 1) // POOL_GROUP_DIM
    acc = ext
    out = None
    for gi in range(len(POOL_WINDOWS)):
        shift = 2 ** gi
        acc = acc + pltpu.roll(acc, shift if trailing else ext.shape[0] - shift, 0)
        out = acc if out is None else jnp.where(group == gi, acc, out)
    return out


def _pool_pooled(ext, cur, inv):
    return _window_sums(ext, True)[BLK:] * inv - cur


def pool_fwd(p, wblk, pb, ps):
    s, n = p.shape

    def body(cur_ref, prev_ref, w_ref, pb_ref, ps_ref, o_ref):
        c = pl.program_id(0)
        cur = cur_ref[...]
        prev = jnp.where(c > 0, prev_ref[...], 0.0)
        pooled = _pool_pooled(jnp.concatenate([prev, cur], axis=0), cur, _pool_inv(c))
        o_ref[...] = (dot_nn(pooled, w_ref[...]) + pb_ref[...]) * ps_ref[...]

    return pl.pallas_call(
        body, name="pool_fwd", grid=(s // BLK,),
        in_specs=[pl.BlockSpec((BLK, n), lambda c: (c, 0)), pl.BlockSpec((BLK, n), lambda c: (jnp.maximum(c - 1, 0), 0)),
                  _full_spec((n, n)), _full_spec((1, n)), _full_spec((1, n))],
        out_specs=pl.BlockSpec((BLK, n), lambda c: (c, 0)), out_shape=jax.ShapeDtypeStruct((s, n), F32),
        compiler_params=_params(1),
    )(p, p, wblk, pb, ps)


def pool_bwd(p, wblk, pb, ps, dout, do_col=0):
    s, n = p.shape
    nb = s // BLK

    def body(cur_ref, prev_ref, w_ref, pb_ref, ps_ref, do_ref, dp_ref, dw_ref, dpb_ref, dps_ref, carry_ref):
        i = pl.program_id(0)
        c = nb - 1 - i
        first = i == 0
        cur = cur_ref[...]
        prev = jnp.where(c > 0, prev_ref[...], 0.0)
        inv = _pool_inv(c)
        pooled = _pool_pooled(jnp.concatenate([prev, cur], axis=0), cur, inv)
        mixed = dot_nn(pooled, w_ref[...]) + pb_ref[...]
        dov = do_ref[...]
        dmixed = dov * ps_ref[...]
        _acc(dps_ref, first, jnp.sum(dov * mixed, axis=0, keepdims=True))
        _acc(dpb_ref, first, jnp.sum(dmixed, axis=0, keepdims=True))
        _acc(dw_ref, first, dot_tn(pooled, dmixed))
        dpooled = dot_nt(dmixed, w_ref[...])
        dext = _window_sums(jnp.concatenate([jnp.zeros((BLK, n), F32), dpooled * inv], axis=0), False)

        @pl.when(first)
        def _():
            carry_ref[...] = jnp.zeros_like(carry_ref)

        dp_ref[...] = dext[BLK:] - dpooled + carry_ref[...]
        carry_ref[...] = dext[:BLK]

    rev = lambda i: (nb - 1 - i, 0)
    return pl.pallas_call(
        body, name="pool_bwd", grid=(nb,),
        in_specs=[pl.BlockSpec((BLK, n), rev), pl.BlockSpec((BLK, n), lambda i: (jnp.maximum(nb - 2 - i, 0), 0)),
                  _full_spec((n, n)), _full_spec((1, n)), _full_spec((1, n)),
                  pl.BlockSpec((BLK, n), lambda i: (nb - 1 - i, do_col))],
        out_specs=[pl.BlockSpec((BLK, n), rev), _full_spec((n, n)), _full_spec((1, n)), _full_spec((1, n))],
        out_shape=[jax.ShapeDtypeStruct((s, n), F32), jax.ShapeDtypeStruct((n, n), F32),
                   jax.ShapeDtypeStruct((1, n), F32), jax.ShapeDtypeStruct((1, n), F32)],
        scratch_shapes=[pltpu.VMEM((BLK, n), F32)],
        compiler_params=_params(1),
    )(p, p, wblk, pb, ps, dout)


def _row_tile(rows):
    if rows <= 512:
        return rows
    for t in (512, 256, 128, 64, 32, 16, 8):
        if rows % t == 0:
            return t
    return rows


def adamw(w, g, m, v):
    n, rows, cols = w.shape
    tr = _row_tile(rows)

    def body(w_ref, g_ref, m_ref, v_ref, d_ref, nm_ref, nv_ref):
        d_ref[...], nm_ref[...], nv_ref[...] = _adamw_math(w_ref[...], g_ref[...], m_ref[...], v_ref[...])

    spec = pl.BlockSpec((1, tr, cols), lambda i, j: (i, j, 0))
    return pl.pallas_call(
        body, name="adamw", grid=(n, rows // tr), in_specs=[spec] * 4, out_specs=[spec] * 3,
        out_shape=[jax.ShapeDtypeStruct(w.shape, F32)] * 3, compiler_params=_params(2),
    )(w, g, m, v)


def _adamw_math(w, g, m, v):
    nm = ADAM_B1 * m + (1.0 - ADAM_B1) * g
    nv = ADAM_B2 * v + (1.0 - ADAM_B2) * (g * g)
    m_hat = nm / (1.0 - ADAM_B1 ** ADAM_STEP)
    v_hat = nv / (1.0 - ADAM_B2 ** ADAM_STEP)
    return -ADAM_LR * (m_hat / (jnp.sqrt(v_hat) + ADAM_EPS) + ADAM_WD * w), nm, nv


def adamw_small(ws, gs, ms, vs):
    n = len(ws)

    def body(*refs):
        for i in range(n):
            outs = _adamw_math(*[refs[k * n + i][...] for k in range(4)])
            for k in range(3):
                refs[(4 + k) * n + i][...] = outs[k]

    vm = pl.BlockSpec(memory_space=pltpu.VMEM)
    outs = pl.pallas_call(
        body, name="adamw_small", in_specs=[vm] * (4 * n), out_specs=[vm] * (3 * n),
        out_shape=[jax.ShapeDtypeStruct(w.shape, F32) for w in ws] * 3,
    )(*ws, *gs, *ms, *vs)
    return outs[:n], outs[n:2 * n], outs[2 * n:]


def slab_sum(srcs, n_out, out_dtype):
    _, rows, cols = srcs[0][0].shape
    tr = _row_tile(rows)
    sel = jnp.stack([jnp.asarray(base, jnp.int32) for _, base, _ in srcs])

    def body(sel_ref, *refs):
        acc = refs[0][...].astype(F32)
        for r in refs[1:-1]:
            acc = acc + r[...].astype(F32)
        refs[-1][...] = acc.astype(out_dtype)

    def in_spec(k, step):
        return pl.BlockSpec((None, tr, cols), lambda o, i, sel_ref: (sel_ref[k] + step * o, i, 0))

    return pl.pallas_call(
        body, name="slab_sum",
        grid_spec=pltpu.PrefetchScalarGridSpec(
            num_scalar_prefetch=1, grid=(n_out, rows // tr),
            in_specs=[in_spec(k, step) for k, (_, _, step) in enumerate(srcs)],
            out_specs=pl.BlockSpec((None, tr, cols), lambda o, i, sel_ref: (o, i, 0))),
        out_shape=jax.ShapeDtypeStruct((n_out, rows, cols), out_dtype), compiler_params=_params(2),
    )(sel, *[a for a, _, _ in srcs])


ICI_FLIPS = ((1, 0, 0), (0, 1, 0), (1, 1, 0))
D2D_FLIPS = ((0, 0, 1),)
ANY = pl.BlockSpec(memory_space=pl.ANY)


def _me():
    return lax.axis_index("x"), lax.axis_index("y"), lax.axis_index("c")


def _flipped(me, flip):
    return tuple(1 - m if f else m for m, f in zip(me, flip))


def _chip(dev):
    return 2 * dev[0] + dev[1]


def _dev(dev):
    return 4 * dev[0] + 2 * dev[1] + dev[2]


N_CHIP = 4
D2D = (0, 0, 1)


class Exchange:
    def __init__(self, xs, n_out, copies, own=None, in_place=False):
        self.xs, self.copies, self.own, self.in_place = list(xs), copies, own, in_place
        self.n_arr, self.n_cp = len(self.xs), len(copies)
        self.out_shape = [jax.ShapeDtypeStruct((n_out,) + x.shape[1:], x.dtype) for x in self.xs]
        self.scratch = [pltpu.SemaphoreType.DMA((self.n_arr * self.n_cp,)),
                        pltpu.SemaphoreType.DMA((self.n_arr * self.n_cp,)), pltpu.SemaphoreType.DMA((self.n_arr,))]

    def _own(self, x_refs, o_refs, sems, me):
        if self.own is None:
            return []
        return [pltpu.make_async_copy(x_refs[a].at[self.own[0](me)], o_refs[a].at[self.own[1](me)], sems[2].at[a])
                for a in range(self.n_arr)]

    def _copy(self, x_refs, o_refs, sems, me, a, j, sender):
        flip, src_slot, dst_slot = self.copies[j]
        k = a * self.n_cp + j
        return pltpu.make_async_remote_copy(
            src_ref=x_refs[a].at[src_slot(me)], dst_ref=o_refs[a].at[dst_slot(sender)],
            send_sem=sems[0].at[k], recv_sem=sems[1].at[k], device_id=_flipped(me, flip), device_id_type=MESH)

    def start(self, x_refs, o_refs, sems):
        me = _me()
        for cp in self._own(x_refs, o_refs, sems, me):
            cp.start()
        for j in range(self.n_cp):
            for a in range(self.n_arr):
                self._copy(x_refs, o_refs, sems, me, a, j, me).start()

    def wait(self, x_refs, o_refs, sems):
        me = _me()
        for j in range(self.n_cp):
            for a in range(self.n_arr):
                self._copy(x_refs, o_refs, sems, me, a, j, _flipped(me, self.copies[j][0])).wait_recv()
        for j in range(self.n_cp):
            for a in range(self.n_arr):
                self._copy(x_refs, o_refs, sems, me, a, j, me).wait_send()
        for cp in self._own(x_refs, o_refs, sems, me):
            cp.wait()

    def run(self, name):
        n = self.n_arr

        def body(*refs):
            self.start(refs[:n], refs[n:2 * n], refs[2 * n:])
            self.wait(refs[:n], refs[n:2 * n], refs[2 * n:])

        return pl.pallas_call(
            body, name=name, in_specs=[ANY] * n, out_specs=[ANY] * n, out_shape=self.out_shape,
            input_output_aliases={a: a for a in range(n)} if self.in_place else {}, scratch_shapes=self.scratch,
        )(*self.xs)


def hosted_call(body, sides, *, name, grid, in_specs, out_specs, out_shape, scratch_shapes, compiler_params, operands):
    n_in, n_out, n_scr = len(in_specs), len(out_specs), len(scratch_shapes)
    live = [s for s in sides if s is not None]
    if not live:
        outs = pl.pallas_call(body, name=name, grid=grid, in_specs=in_specs, out_specs=out_specs, out_shape=out_shape,
                              scratch_shapes=scratch_shapes, compiler_params=compiler_params)(*operands)
        return outs, [None] * len(sides)
    n = sum(s.n_arr for s in live)
    lo = [sum(s.n_arr for s in live[:i]) for i in range(len(live))]

    def full_body(*refs):
        ins, sx = refs[:n_in], refs[n_in:n_in + n]
        outs, so = refs[n_in + n:n_in + n + n_out], refs[n_in + n + n_out:n_in + 2 * n + n_out]
        scr, sems = refs[n_in + 2 * n + n_out:n_in + 2 * n + n_out + n_scr], refs[n_in + 2 * n + n_out + n_scr:]
        first = functools.reduce(jnp.logical_and, [pl.program_id(a) == 0 for a in range(len(grid))])
        last = functools.reduce(jnp.logical_and, [pl.program_id(a) == g - 1 for a, g in enumerate(grid)])
        parts = [(s, sx[l:l + s.n_arr], so[l:l + s.n_arr], sems[3 * i:3 * i + 3]) for i, (s, l) in enumerate(zip(live, lo))]

        @pl.when(first)
        def _():
            for s, x, o, m in parts:
                s.start(x, o, m)

        body(*ins, *outs, *scr)

        @pl.when(last)
        def _():
            for s, x, o, m in parts:
                s.wait(x, o, m)

    aliases = {n_in + l + a: n_out + l + a for s, l in zip(live, lo) if s.in_place for a in range(s.n_arr)}
    outs = pl.pallas_call(
        full_body, name=name + "_x", grid=grid, in_specs=list(in_specs) + [ANY] * n,
        out_specs=list(out_specs) + [ANY] * n, out_shape=list(out_shape) + [o for s in live for o in s.out_shape],
        input_output_aliases=aliases,
        scratch_shapes=list(scratch_shapes) + [m for s in live for m in s.scratch], compiler_params=compiler_params,
    )(*operands, *[x for s in live for x in s.xs])
    side_outs = iter([outs[n_out + l:n_out + l + s.n_arr] for s, l in zip(live, lo)])
    return outs[:n_out], [next(side_outs) if s is not None else None for s in sides]


def gather_ici(shards):
    ici = [(f, lambda me: 0, _dev) for f in ICI_FLIPS]
    return Exchange([s[None] for s in shards], N_DEV, ici, (lambda me: 0, _dev))


def gather_d2d(blocks):
    d2d = [(D2D, (lambda me, k=k: 2 * k + me[2]), (lambda sender, k=k: 2 * k + sender[2])) for k in range(N_CHIP)]
    return Exchange(blocks, N_DEV, d2d, None, in_place=True)


def gathered(blocks):
    return [b.reshape(-1, b.shape[2]) for b in blocks]


def scatter_d2d(parts):
    d2d = [(D2D, (lambda me, k=k: 2 * k + 1 - me[2]), (lambda sender, k=k: k)) for k in range(N_CHIP)]
    return Exchange(parts, N_CHIP, d2d)


def chip_sums(parts, sib):
    c = _me()[2]
    return [slab_sum([(p, c, 2), (s, 0, 1)], N_CHIP, BF16) for p, s in zip(parts, sib)]


def scatter_ici(sums):
    ici = [(f, (lambda me, f=f: _chip(_flipped(me, f))), (lambda sender, i=i: i)) for i, f in enumerate(ICI_FLIPS)]
    return Exchange(sums, len(ICI_FLIPS), ici)


def device_sums(sums, got):
    x, y, _ = _me()
    return [slab_sum([(cs, 2 * x + y, 0)] + [(g, i, 0) for i in range(len(ICI_FLIPS))], 1, F32)[0]
            for cs, g in zip(sums, got)]


def all_gather(shards):
    blocks = gather_ici(shards).run("gather_ici")
    return gathered(gather_d2d(blocks).run("gather_d2d"))


def reduce_scatter(parts):
    sums = chip_sums(parts, scatter_d2d(parts).run("scatter_d2d"))
    return device_sums(sums, scatter_ici(sums).run("scatter_ici"))


def all_reduce_small(v):
    flips = D2D_FLIPS + ICI_FLIPS[:2]

    def body(v_ref, o_ref, got_ref, send_sems, recv_sems):
        me = _me()
        o_ref[...] = v_ref[...]
        for i, flip in enumerate(flips):
            cp = pltpu.make_async_remote_copy(
                src_ref=o_ref, dst_ref=got_ref.at[i], send_sem=send_sems.at[i], recv_sem=recv_sems.at[i],
                device_id=_flipped(me, flip), device_id_type=MESH)
            cp.start()
            cp.wait()
            o_ref[...] = o_ref[...] + got_ref[i]

    vm = pl.BlockSpec(memory_space=pltpu.VMEM)
    return pl.pallas_call(
        body, name="all_reduce_small", in_specs=[vm], out_specs=vm, out_shape=jax.ShapeDtypeStruct(v.shape, F32),
        scratch_shapes=[pltpu.VMEM((len(flips),) + v.shape, F32), pltpu.SemaphoreType.DMA((len(flips),)),
                        pltpu.SemaphoreType.DMA((len(flips),))],
    )(v)


def _perm_rows(wt):
    pad = jnp.zeros((D_IN_PAD - D_IN_PROJ, wt.shape[1]), wt.dtype)
    return jnp.concatenate([wt[:DT_LO], wt[DT_HI:], wt[DT_LO:DT_HI], pad], axis=0)


def _unperm_rows(dwt):
    n = D_IN_PROJ - (DT_HI - DT_LO)
    return jnp.concatenate([dwt[:DT_LO], dwt[n:D_IN_PROJ], dwt[DT_LO:n]], axis=0)


def _pad_lanes(v):
    return jnp.pad(v, ((0, 0), (0, LANE - v.shape[1])))[:, None]


def _block_diag(w):
    l, g, n, _ = w.shape
    out = jnp.zeros((l, g * n, g * n), w.dtype)
    for i in range(g):
        out = out.at[:, i * n:(i + 1) * n, i * n:(i + 1) * n].set(w[:, i])
    return out


def _pack(groups):
    flat = []
    for grp in groups:
        parts = [a.reshape(-1) for a in (grp if isinstance(grp, (list, tuple)) else [grp])]
        n = sum(p.shape[0] for p in parts)
        if -n % LANE:
            parts.append(jnp.zeros((-n % LANE,), parts[0].dtype))
        flat += parts
    return jnp.concatenate(flat).reshape(-1, LANE)


def _unpack(buf, shapes):
    out = []
    lo = 0
    buf = buf.reshape(-1)
    for shp in shapes:
        n = 1
        for k in shp:
            n *= k
        out.append(buf[lo:lo + n].reshape(shp))
        lo += n + (-n % LANE)
    return out


def small_params(w, conv_w_full):
    return dict(
        n1w=w["norm1_w"][:, None], cw=jnp.pad(conv_w_full, ((0, 0), (0, 8 - CONV_WIDTH), (0, 0))),
        cb=w["conv_b"][:, None], dtb=_pad_lanes(w["dt_bias"]), alog=_pad_lanes(w["a_log"]), dsk=_pad_lanes(w["d_skip"]),
        snw=w["ssd_norm_w"][:, None], wblk=_block_diag(w["pool_w"]), pb=w["pool_b"].reshape(-1, 1, POOL_WIDTH),
        ps=w["pool_scale"][:, None], n2w=w["norm2_w"][:, None])


MIX = ("w_in", "w_out")
FFN = ("w_gate", "w_up", "w_down")


def layer_params(small, l):
    return {k: v[l] for k, v in small.items()}


def mix_weights(whole):
    return _perm_rows(whole[0]), whole[1]


def _slabs(g):
    return g.reshape(N_DEV, -1, g.shape[-1])


def _layer_fwd(x, p, mix, ffn=None, ffn_shards=None, next_mix_shards=None):
    z, xbc, qkv, pp, dtr, h1 = inproj_fwd(x, p["n1w"], mix[0])
    u = conv_fwd(xbc, p["cw"], p["cb"])
    y_ssd, st = ssd_fwd(u, z, dtr, p["dtb"], p["alog"], p["dsk"], p["snw"])
    (o, tot, swept), (blocks,) = sb_fwd(qkv, [gather_ici(ffn_shards) if ffn_shards is not None else None])
    yp = pool_fwd(pp, p["wblk"], p["pb"], p["ps"])
    ycat = jnp.concatenate([y_ssd, o, yp], axis=1)
    (x_mid,), (blocks,) = outproj_fwd(ycat, mix[1], x, [gather_d2d(blocks) if blocks is not None else None])
    if blocks is not None:
        ffn = gathered(blocks)
    (x_out, g, uu), (nxt,) = ffn_fwd(x_mid, p["n2w"], *ffn,
                                     [gather_ici(next_mix_shards) if next_mix_shards is not None else None])
    sv = dict(x=x, z=z, xbc=xbc, qkv=qkv, pp=pp, dtr=dtr, h1=h1, u=u, st=st, tot=tot, swept=swept, ycat=ycat,
              x_mid=x_mid, g=g, uu=uu, w_in=mix[0], w_out=mix[1], wg=ffn[0], wu=ffn[1], wd=ffn[2])
    return x_out, sv, nxt


def _layer_bwd(dxo, sv, p, pending_mix=None, exchange=False):
    (dx_mid, dn2w, a, dg, du, h2), (sib,) = ffn_bwd(
        dxo, sv["x_mid"], sv["g"], sv["uu"], p["n2w"], sv["wg"], sv["wu"], sv["wd"],
        [scatter_d2d(pending_mix) if pending_mix is not None else None])
    sums_mix = chip_sums(pending_mix, sib) if pending_mix is not None else None
    gr = dict(norm2_w=dn2w[0], w_gate=mm_tn(dg, h2), w_up=mm_tn(du, h2), w_down=mm_tn(a, dxo))
    parts = [_slabs(gr[k]) for k in FFN] if exchange else None
    (dycat,), (sib,) = outproj_bwd(dx_mid, sv["w_out"], [scatter_d2d(parts) if exchange else None])
    sums_ffn = chip_sums(parts, sib) if exchange else None
    gr["w_out"] = mm_tn(sv["ycat"], dx_mid)
    dp, dwblk, dpb, dps = pool_bwd(sv["pp"], p["wblk"], p["pb"], p["ps"], dycat,
                                   (SSD_WIDTH + SB_WIDTH) // POOL_WIDTH)
    n = POOL_GROUP_DIM
    gr["pool_w"] = jnp.stack([dwblk[i * n:(i + 1) * n, i * n:(i + 1) * n] for i in range(len(POOL_WINDOWS))])
    gr["pool_b"] = dpb.reshape(len(POOL_WINDOWS), n)
    gr["pool_scale"] = dps[0]
    (dq, dk, dv), (got_ffn, got_mix) = sb_bwd(
        sv["qkv"], sv["tot"], sv["swept"], dycat, SSD_WIDTH // LANE,
        [scatter_ici(sums_ffn) if exchange else None, scatter_ici(sums_mix) if sums_mix is not None else None])
    done_ffn = device_sums(sums_ffn, got_ffn) if exchange else None
    done_mix = device_sums(sums_mix, got_mix) if sums_mix is not None else None
    du_, dz, ddtr, ddtb, dalog, ddsk, dsnw = ssd_bwd(sv["u"], sv["z"], sv["dtr"], sv["st"], dycat, p["dtb"], p["alog"],
                                                     p["dsk"], p["snw"])
    gr.update(dt_bias=ddtb[0, :SSD_HEADS], a_log=dalog[0, :SSD_HEADS], d_skip=ddsk[0, :SSD_HEADS], ssd_norm_w=dsnw[0])
    dxbc, dcw, dcb = conv_bwd(sv["xbc"], p["cw"], p["cb"], du_)
    gr.update(conv_w=dcw[:CONV_WIDTH], conv_b=dcb[0])
    dproj = jnp.concatenate([dz, dxbc, dq, dk, dv, dp, ddtr], axis=1)
    dx, dn1w = inproj_bwd(dproj, sv["w_in"], sv["x"], p["n1w"], dx_mid)
    gr.update(norm1_w=dn1w[0], w_in=_unperm_rows(mm_tn(dproj, sv["h1"])))
    return dx, gr, done_ffn, done_mix


def local_step(x, tgt, params, weights, final_w):
    saved = []
    for p, (mix, ffn) in zip(params, weights):
        x, sv, _ = _layer_fwd(x, p, mix, ffn)
        saved.append(sv)
    loss, dx, dfw = head_loss(x, final_w[None], tgt)
    grads = []
    for p, sv in zip(reversed(params), reversed(saved)):
        dx, gr, _, _ = _layer_bwd(dx, sv, p)
        grads.append(gr)
    grads.reverse()
    return loss, dx, dfw[0], grads


WEIGHTS = ("norm1_w", "w_in", "conv_w", "conv_b", "dt_bias", "a_log", "d_skip", "ssd_norm_w", "pool_w", "pool_b",
           "pool_scale", "w_out", "norm2_w", "w_gate", "w_up", "w_down", "final_norm_w")
COL_SHARDED = ("w_in", "w_gate", "w_up")
ROW_SHARDED = ("w_out", "w_down")
SMALL = tuple(k for k in WEIGHTS if k not in COL_SHARDED + ROW_SHARDED)


def kernel(x, norm1_w, w_in, conv_w, conv_b, dt_bias, a_log, d_skip, ssd_norm_w, pool_w, pool_b, pool_scale, w_out, norm2_w, w_gate, w_up, w_down, final_norm_w, loss_target, m_norm1_w, m_w_in, m_conv_w, m_conv_b, m_dt_bias, m_a_log, m_d_skip, m_ssd_norm_w, m_pool_w, m_pool_b, m_pool_scale, m_w_out, m_norm2_w, m_w_gate, m_w_up, m_w_down, m_final_norm_w, v_norm1_w, v_w_in, v_conv_w, v_conv_b, v_dt_bias, v_a_log, v_d_skip, v_ssd_norm_w, v_pool_w, v_pool_b, v_pool_scale, v_w_out, v_norm2_w, v_w_gate, v_w_up, v_w_down, v_final_norm_w):
    args = dict(locals())
    w = {k: args[k] for k in WEIGHTS}
    m = {k: args["m_" + k] for k in WEIGHTS}
    v = {k: args["v_" + k] for k in WEIGHTS}
    depth = w_in.shape[0]
    dev = _dev(_me())
    n_cw = conv_w.shape[-1]

    shards = {k: (jnp.swapaxes(w[k], 1, 2) if k in COL_SHARDED else w[k]).astype(BF16) for k in MIX + FFN}
    whole = all_gather([jnp.swapaxes(conv_w, 0, 2).reshape(n_cw, -1)] + [shards[k][0] for k in MIX])
    conv_w_full = jnp.swapaxes(whole[0].reshape(N_DEV * n_cw, CONV_WIDTH, depth), 0, 2)
    small = small_params(w, conv_w_full)
    xs = x[0]
    params, saved = [layer_params(small, l) for l in range(depth)], []
    mix = mix_weights(whole[1:])
    for l in range(depth):
        xs, sv, nxt = _layer_fwd(xs, params[l], mix, ffn_shards=[shards[k][l] for k in FFN],
                                 next_mix_shards=[shards[k][l + 1] for k in MIX] if l + 1 < depth else None)
        saved.append(sv)
        if nxt is not None:
            mix = mix_weights(gathered(gather_d2d(nxt).run("gather_d2d")))
    loss, dx, dfw = head_loss(xs, final_norm_w[None], loss_target[0])
    layer_grads = [None] * depth
    reduced = [dict() for _ in range(depth)]
    pending = None
    for l in reversed(range(depth)):
        dx, layer_grads[l], done_ffn, done_mix = _layer_bwd(dx, saved[l], params[l], pending, exchange=True)
        reduced[l].update(zip(FFN, done_ffn))
        if pending is not None:
            reduced[l + 1].update(zip(MIX, done_mix))
        pending = [_slabs(layer_grads[l][k]) for k in MIX]
    reduced[0].update(zip(MIX, reduce_scatter(pending)))

    grads, native = {}, {}
    for k in MIX + FFN:
        native[k] = jnp.stack([reduced[l][k] for l in range(depth)])
        grads[k] = jnp.swapaxes(native[k], 1, 2) if k in COL_SHARDED else native[k]
    layered = [k for k in SMALL if k != "final_norm_w"]
    small_shapes = [(1, LANE)] + [(depth,) + layer_grads[0][k].shape for k in layered] + [dfw[0].shape]
    packed = _pack([loss] + [[layer_grads[l][k] for l in range(depth)] for k in layered] + [dfw[0]])
    summed = _unpack(all_reduce_small(packed), small_shapes)
    loss = summed[0][0, 0]
    grads.update(zip(layered + ["final_norm_w"], summed[1:]))
    grads["conv_w"] = lax.dynamic_slice_in_dim(grads["conv_w"], dev * n_cw, n_cw, axis=2)

    delta, new_m, new_v = {}, {}, {}
    for k in MIX + FFN:
        if k in ("w_gate", "w_up"):
            wt, mt, vt = (jnp.swapaxes(t, 1, 2) for t in (w[k], m[k], v[k]))
            delta[k], new_m[k], new_v[k] = [jnp.swapaxes(o, 1, 2) for o in adamw(wt, native[k], mt, vt)]
        else:
            delta[k], new_m[k], new_v[k] = adamw(w[k], grads[k], m[k], v[k])
    two_d = lambda a: a.reshape(1, -1) if a.ndim == 1 else a
    outs = adamw_small(*[[two_d(t[k]) for k in SMALL] for t in (w, grads, m, v)])
    for dst, arrs in zip((delta, new_m, new_v), outs):
        dst.update({k: a.reshape(w[k].shape) for k, a in zip(SMALL, arrs)})
    return (loss, dx[None], *[grads[k] for k in WEIGHTS], *[delta[k] for k in WEIGHTS],
            *[new_m[k] for k in WEIGHTS], *[new_v[k] for k in WEIGHTS])
```

```python
import functools

import jax
import jax.numpy as jnp
from jax import lax
from jax.experimental import pallas as pl
from jax.experimental.pallas import tpu as pltpu

F32 = jnp.float32
BF16 = jnp.bfloat16
HIGHEST = lax.Precision.HIGHEST
MESH = pl.DeviceIdType.MESH

EPS = 1e-6
D_MODEL = 1024
SSD_WIDTH = 512
SSD_HEADS = 8
HEAD_DIM = 64
D_STATE = 128
CONV_WIDTH = 4
CONV_DIM = 1024
SB_WIDTH = 256
POOL_WIDTH = 256
POOL_WINDOWS = (2, 4, 8, 16)
D_IN_PROJ = 2568
D_FF = 2816
N_DEV = 8
DEPTH = 4
SEG = (512, 1024, 768, 256, 128)
D_IN_PAD = sum(SEG)
DT_LO, DT_HI = 1536, 1544

LANE = 128
BLK = 128
ROW_TILE = 256
VMEM_LIMIT = 56 * 2**20

ADAM_LR, ADAM_B1, ADAM_B2, ADAM_EPS, ADAM_WD, ADAM_STEP = 0.001, 0.9, 0.999, 1e-08, 0.01, 10


def _params(n_axes=1, vmem=None):
    return pltpu.CompilerParams(dimension_semantics=("arbitrary",) * n_axes, vmem_limit_bytes=vmem)


def _dot(a, b, dims, exact=False):
    if exact:
        return lax.dot_general(a.astype(F32), b.astype(F32), (dims, ((), ())), precision=HIGHEST,
                               preferred_element_type=F32)
    return lax.dot_general(a.astype(BF16), b.astype(BF16), (dims, ((), ())), preferred_element_type=F32)


def dot_nn(a, b, exact=False):
    return _dot(a, b, ((1,), (0,)), exact)


def dot_nt(a, b, exact=False):
    return _dot(a, b, ((1,), (1,)), exact)


def dot_tn(a, b, exact=False):
    return _dot(a, b, ((0,), (0,)), exact)


def _iota(shape, axis):
    return lax.broadcasted_iota(jnp.int32, shape, axis)


def _lane_col(x, h):
    return jnp.sum(jnp.where(_iota(x.shape, 1) == h, x, 0.0), axis=1, keepdims=True)


def _sub_row(x, h):
    return jnp.sum(jnp.where(_iota(x.shape, 0) == h, x, 0.0), axis=0, keepdims=True)


def _sigmoid(x):
    return 1.0 / (1.0 + jnp.exp(-x))


def _rms_fwd(x, w):
    r = lax.rsqrt(jnp.mean(x * x, axis=-1, keepdims=True) + EPS)
    return x * r * w


def _rms_bwd(x, w, dy):
    r = lax.rsqrt(jnp.mean(x * x, axis=-1, keepdims=True) + EPS)
    xh = x * r
    dxh = dy * w
    dx = r * (dxh - xh * jnp.mean(dxh * xh, axis=-1, keepdims=True))
    return dx, jnp.sum(dy * xh, axis=0, keepdims=True)


def _acc(ref, first, val):
    @pl.when(first)
    def _():
        ref[...] = val

    @pl.when(jnp.logical_not(first))
    def _():
        ref[...] += val


def _row_spec(tm, n):
    return pl.BlockSpec((tm, n), lambda i: (i, 0))


def _full_spec(shape):
    return pl.BlockSpec(shape, lambda *_: (0,) * len(shape))


def inproj_fwd(x, nw, w):
    s, d = x.shape
    tm = min(ROW_TILE, s)

    def body(x_ref, nw_ref, w_ref, z_ref, xbc_ref, qkv_ref, p_ref, dt_ref, h_ref):
        h = _rms_fwd(x_ref[...], nw_ref[...]).astype(BF16)
        h_ref[...] = h
        lo = 0
        for ref, n in zip((z_ref, xbc_ref, qkv_ref, p_ref, dt_ref), SEG):
            ref[...] = dot_nt(h, w_ref[lo:lo + n, :])
            lo += n

    return pl.pallas_call(
        body, name="inproj_fwd", grid=(s // tm,),
        in_specs=[_row_spec(tm, d), _full_spec((1, d)), _full_spec(w.shape)],
        out_specs=[_row_spec(tm, n) for n in SEG] + [_row_spec(tm, d)],
        out_shape=[jax.ShapeDtypeStruct((s, n), F32) for n in SEG] + [jax.ShapeDtypeStruct((s, d), BF16)],
        compiler_params=_params(1, VMEM_LIMIT),
    )(x, nw, w)


def inproj_bwd(dproj, w, x, nw, dres):
    s, d = x.shape
    tm = min(ROW_TILE, s)

    def body(dp_ref, w_ref, x_ref, nw_ref, dres_ref, dx_ref, dnw_ref):
        dh = dot_nn(dp_ref[...], w_ref[...])
        dx, dnw = _rms_bwd(x_ref[...], nw_ref[...], dh)
        dx_ref[...] = dres_ref[...] + dx
        _acc(dnw_ref, pl.program_id(0) == 0, dnw)

    return pl.pallas_call(
        body, name="inproj_bwd", grid=(s // tm,),
        in_specs=[_row_spec(tm, dproj.shape[1]), _full_spec(w.shape), _row_spec(tm, d), _full_spec((1, d)),
                  _row_spec(tm, d)],
        out_specs=[_row_spec(tm, d), _full_spec((1, d))],
        out_shape=[jax.ShapeDtypeStruct((s, d), F32), jax.ShapeDtypeStruct((1, d), F32)],
        compiler_params=_params(1, VMEM_LIMIT),
    )(dproj, w, x, nw, dres)


def outproj_fwd(y, w, res, sides=()):
    s, d = res.shape
    tm = min(ROW_TILE, s)

    def body(y_ref, w_ref, r_ref, o_ref):
        o_ref[...] = r_ref[...] + dot_nn(y_ref[...], w_ref[...])

    return hosted_call(
        body, sides, name="outproj_fwd", grid=(s // tm,),
        in_specs=[_row_spec(tm, y.shape[1]), _full_spec(w.shape), _row_spec(tm, d)],
        out_specs=[_row_spec(tm, d)], out_shape=[jax.ShapeDtypeStruct((s, d), F32)], scratch_shapes=[],
        compiler_params=_params(1, VMEM_LIMIT), operands=(y, w, res))


def outproj_bwd(dx, w, sides=()):
    s, d = dx.shape
    tm = min(ROW_TILE, s)

    def body(dx_ref, w_ref, o_ref):
        o_ref[...] = dot_nt(dx_ref[...], w_ref[...])

    return hosted_call(
        body, sides, name="outproj_bwd", grid=(s // tm,),
        in_specs=[_row_spec(tm, d), _full_spec(w.shape)],
        out_specs=[_row_spec(tm, w.shape[0])], out_shape=[jax.ShapeDtypeStruct((s, w.shape[0]), F32)],
        scratch_shapes=[], compiler_params=_params(1, VMEM_LIMIT), operands=(dx, w))


def ffn_fwd(x, nw, wg, wu, wd, sides=()):
    s, d = x.shape
    f = wg.shape[0]
    tm = min(ROW_TILE, s)

    def body(x_ref, nw_ref, wg_ref, wu_ref, wd_ref, o_ref, g_ref, u_ref):
        xv = x_ref[...]
        h = _rms_fwd(xv, nw_ref[...]).astype(BF16)
        g = dot_nt(h, wg_ref[...])
        u = dot_nt(h, wu_ref[...])
        g_ref[...] = g.astype(BF16)
        u_ref[...] = u.astype(BF16)
        o_ref[...] = xv + dot_nn(g * _sigmoid(g) * u, wd_ref[...])

    return hosted_call(
        body, sides, name="ffn_fwd", grid=(s // tm,),
        in_specs=[_row_spec(tm, d), _full_spec((1, d)), _full_spec(wg.shape), _full_spec(wu.shape),
                  _full_spec(wd.shape)],
        out_specs=[_row_spec(tm, d), _row_spec(tm, f), _row_spec(tm, f)],
        out_shape=[jax.ShapeDtypeStruct((s, d), F32), jax.ShapeDtypeStruct((s, f), BF16),
                   jax.ShapeDtypeStruct((s, f), BF16)],
        scratch_shapes=[], compiler_params=_params(1, VMEM_LIMIT), operands=(x, nw, wg, wu, wd))


def ffn_bwd(dxo, x, g, u, nw, wg, wu, wd, sides=()):
    s, d = x.shape
    f = wg.shape[0]
    tm = min(ROW_TILE, s)

    def body(dxo_ref, x_ref, g_ref, u_ref, nw_ref, wg_ref, wu_ref, wd_ref, dx_ref, dnw_ref, a_ref, dg_ref,
             du_ref, h_ref):
        dxo_v = dxo_ref[...]
        xv = x_ref[...]
        da = dot_nt(dxo_v, wd_ref[...])
        gv = g_ref[...].astype(F32)
        uv = u_ref[...].astype(F32)
        sg = _sigmoid(gv)
        sl = gv * sg
        a_ref[...] = (sl * uv).astype(BF16)
        dg = (da * uv * (sg * (1.0 + gv * (1.0 - sg)))).astype(BF16)
        du = (da * sl).astype(BF16)
        dg_ref[...] = dg
        du_ref[...] = du
        dh = dot_nn(dg, wg_ref[...]) + dot_nn(du, wu_ref[...])
        h_ref[...] = _rms_fwd(xv, nw_ref[...]).astype(BF16)
        dx, dnw = _rms_bwd(xv, nw_ref[...], dh)
        dx_ref[...] = dxo_v + dx
        _acc(dnw_ref, pl.program_id(0) == 0, dnw)

    return hosted_call(
        body, sides, name="ffn_bwd", grid=(s // tm,),
        in_specs=[_row_spec(tm, d), _row_spec(tm, d), _row_spec(tm, f), _row_spec(tm, f), _full_spec((1, d)),
                  _full_spec(wg.shape), _full_spec(wu.shape), _full_spec(wd.shape)],
        out_specs=[_row_spec(tm, d), _full_spec((1, d)), _row_spec(tm, f), _row_spec(tm, f), _row_spec(tm, f),
                   _row_spec(tm, d)],
        out_shape=[jax.ShapeDtypeStruct((s, d), F32), jax.ShapeDtypeStruct((1, d), F32),
                   jax.ShapeDtypeStruct((s, f), BF16), jax.ShapeDtypeStruct((s, f), BF16),
                   jax.ShapeDtypeStruct((s, f), BF16), jax.ShapeDtypeStruct((s, d), BF16)],
        scratch_shapes=[], compiler_params=_params(1, VMEM_LIMIT), operands=(dxo, x, g, u, nw, wg, wu, wd))


def _tile(n, cap=256):
    best = LANE
    for t in range(LANE, cap + 1, LANE):
        if n % t == 0:
            best = t
    return best


def mm_tn(a, b):
    s, k = a.shape
    n = b.shape[1]
    tk = _tile(k)

    def body(a_ref, b_ref, o_ref):
        o_ref[...] = dot_nn(a_ref[...].astype(BF16).T, b_ref[...]).astype(BF16)

    return pl.pallas_call(
        body, name="mm_tn", grid=(k // tk,),
        in_specs=[pl.BlockSpec((s, tk), lambda i: (0, i)), _full_spec((s, n))],
        out_specs=pl.BlockSpec((tk, n), lambda i: (i, 0)), out_shape=jax.ShapeDtypeStruct((k, n), BF16),
        compiler_params=_params(1, VMEM_LIMIT),
    )(a, b)


def head_loss(x, fw, tgt):
    s, d = x.shape
    tm = min(ROW_TILE, s)

    def body(x_ref, fw_ref, t_ref, loss_ref, dx_ref, dfw_ref):
        xv = x_ref[...]
        err = _rms_fwd(xv, fw_ref[...]) - t_ref[...]
        part = jnp.zeros((1, LANE), F32) + 0.5 * jnp.sum(err * err) / d
        dx, dfw = _rms_bwd(xv, fw_ref[...], err / d)
        dx_ref[...] = dx
        first = pl.program_id(0) == 0
        _acc(loss_ref, first, part)
        _acc(dfw_ref, first, dfw)

    return pl.pallas_call(
        body, name="head_loss", grid=(s // tm,),
        in_specs=[_row_spec(tm, d), _full_spec((1, d)), _row_spec(tm, d)],
        out_specs=[_full_spec((1, LANE)), _row_spec(tm, d), _full_spec((1, d))],
        out_shape=[jax.ShapeDtypeStruct((1, LANE), F32), jax.ShapeDtypeStruct((s, d), F32),
                   jax.ShapeDtypeStruct((1, d), F32)],
        compiler_params=_params(1),
    )(x, fw, tgt)


def _conv_pre(ext, cw_ref, cb_ref):
    shifted = [pltpu.roll(ext, CONV_WIDTH - 1 - i, 0)[BLK:] if i < CONV_WIDTH - 1 else ext[BLK:]
               for i in range(CONV_WIDTH)]
    acc = cb_ref[...] + sum(cw_ref[i:i + 1, :] * shifted[i] for i in range(CONV_WIDTH))
    return acc, shifted


def conv_fwd(xbc, cw, cb, sides=()):
    s, n = xbc.shape

    def body(cur_ref, prev_ref, cw_ref, cb_ref, o_ref):
        prev = jnp.where(pl.program_id(0) > 0, prev_ref[...], 0.0)
        acc, _ = _conv_pre(jnp.concatenate([prev, cur_ref[...]], axis=0), cw_ref, cb_ref)
        o_ref[...] = acc * _sigmoid(acc)

    return hosted_call(
        body, sides, name="conv_fwd", grid=(s // BLK,),
        in_specs=[pl.BlockSpec((BLK, n), lambda c: (c, 0)), pl.BlockSpec((BLK, n), lambda c: (jnp.maximum(c - 1, 0), 0)),
                  _full_spec(cw.shape), _full_spec((1, n))],
        out_specs=[pl.BlockSpec((BLK, n), lambda c: (c, 0))], out_shape=[jax.ShapeDtypeStruct((s, n), F32)],
        scratch_shapes=[], compiler_params=_params(1), operands=(xbc, xbc, cw, cb))


def conv_bwd(xbc, cw, cb, du):
    s, n = xbc.shape
    nb = s // BLK

    def body(cur_ref, prev_ref, cw_ref, cb_ref, du_ref, dx_ref, dcw_ref, dcb_ref, nxt_ref):
        i = pl.program_id(0)
        c = nb - 1 - i
        prev = jnp.where(c > 0, prev_ref[...], 0.0)
        acc, shifted = _conv_pre(jnp.concatenate([prev, cur_ref[...]], axis=0), cw_ref, cb_ref)
        sg = _sigmoid(acc)
        dacc = du_ref[...] * (sg * (1.0 + acc * (1.0 - sg)))

        @pl.when(i == 0)
        def _():
            nxt_ref[...] = jnp.zeros_like(nxt_ref)
            dcw_ref[...] = jnp.zeros_like(dcw_ref)
            dcb_ref[...] = jnp.zeros_like(dcb_ref)

        dcb_ref[...] += jnp.sum(dacc, axis=0, keepdims=True)
        for t in range(CONV_WIDTH):
            dcw_ref[t:t + 1, :] += jnp.sum(dacc * shifted[t], axis=0, keepdims=True)
        ext = jnp.concatenate([dacc, nxt_ref[...]], axis=0)
        dx = cw_ref[CONV_WIDTH - 1:CONV_WIDTH, :] * dacc
        for t in range(CONV_WIDTH - 1):
            dx += cw_ref[t:t + 1, :] * pltpu.roll(ext, 2 * BLK - (CONV_WIDTH - 1 - t), 0)[:BLK]
        dx_ref[...] = dx
        nxt_ref[...] = dacc

    rev = lambda i: (nb - 1 - i, 0)
    return pl.pallas_call(
        body, name="conv_bwd", grid=(nb,),
        in_specs=[pl.BlockSpec((BLK, n), rev), pl.BlockSpec((BLK, n), lambda i: (jnp.maximum(nb - 2 - i, 0), 0)),
                  _full_spec(cw.shape), _full_spec((1, n)), pl.BlockSpec((BLK, n), rev)],
        out_specs=[pl.BlockSpec((BLK, n), rev), _full_spec((8, n)), _full_spec((1, n))],
        out_shape=[jax.ShapeDtypeStruct((s, n), F32), jax.ShapeDtypeStruct((8, n), F32),
                   jax.ShapeDtypeStruct((1, n), F32)],
        scratch_shapes=[pltpu.VMEM((BLK, n), F32)],
        compiler_params=_params(1),
    )(xbc, xbc, cw, cb, du)


N_PAIR = SSD_HEADS // 2
B_LO = SSD_WIDTH
C_LO = SSD_WIDTH + 2 * D_STATE


def _softplus(x):
    return jnp.maximum(x, 0.0) + jnp.log(1.0 + jnp.exp(-jnp.abs(x)))


def _ssd_chunk(u_ref, dt_ref, dtb_ref, alog_ref):
    shape = (BLK, BLK)
    tri = _iota(shape, 1) <= _iota(shape, 0)
    pre = dt_ref[...] + dtb_ref[...]
    dt = _softplus(pre)
    a = -jnp.exp(alog_ref[...])
    acum = dot_nn(tri.astype(F32), dt * a, exact=True)
    acum_t = acum.T
    last = _sub_row(acum, BLK - 1)
    heads = []
    for h in range(SSD_HEADS):
        col = _lane_col(acum, h)
        seg = jnp.where(tri, col - _sub_row(acum_t, h), -1e30)
        heads.append(dict(col=col, dm=jnp.exp(seg), dt=_lane_col(dt, h), last=_lane_col(last, h)))
    return tri, pre, dt, a, heads


def _pair_mix(lo_mask, v0, v1):
    return jnp.where(lo_mask, v0, v1)


def ssd_fwd(u, z, dtr, dtb, alog, dsk, nw, sides=()):
    s = u.shape[0]
    nc = s // BLK

    def body(u_ref, z_ref, dt_ref, dtb_ref, alog_ref, dsk_ref, nw_ref, y_ref, st_ref, s_ref):
        @pl.when(pl.program_id(0) == 0)
        def _():
            s_ref[...] = jnp.zeros_like(s_ref)

        _, _, _, _, heads = _ssd_chunk(u_ref, dt_ref, dtb_ref, alog_ref)
        lo_lane = _iota((BLK, LANE), 1) < HEAD_DIM
        lo_sub = _iota((BLK, LANE), 0) < HEAD_DIM
        ys = []
        for p in range(N_PAIR):
            g = p // 2
            h0, h1 = heads[2 * p], heads[2 * p + 1]
            bg = u_ref[:, B_LO + g * D_STATE:B_LO + (g + 1) * D_STATE]
            cg = u_ref[:, C_LO + g * D_STATE:C_LO + (g + 1) * D_STATE]
            xs = u_ref[:, p * LANE:(p + 1) * LANE]
            xp = xs * _pair_mix(lo_lane, h0["dt"], h1["dt"])
            gm = dot_nt(cg, bg)
            yd = _pair_mix(lo_lane, dot_nn(gm * h0["dm"], xp), dot_nn(gm * h1["dm"], xp))
            sp = s_ref[p]
            st_ref[0, p] = sp
            yo = _pair_mix(lo_lane, jnp.exp(h0["col"]), jnp.exp(h1["col"])) * dot_nt(cg, sp)
            dskp = _pair_mix(lo_lane, _lane_col(dsk_ref[...], 2 * p), _lane_col(dsk_ref[...], 2 * p + 1))
            ys.append(yd + yo + xs * dskp)
            wp = _pair_mix(lo_lane, jnp.exp(h0["last"] - h0["col"]), jnp.exp(h1["last"] - h1["col"]))
            el = _pair_mix(lo_sub, jnp.exp(h0["last"]), jnp.exp(h1["last"]))
            s_ref[p] = el * sp + dot_tn(wp * xp, bg)
        y = jnp.concatenate(ys, axis=1)
        zv = z_ref[...]
        y_ref[...] = _rms_fwd(y * zv * _sigmoid(zv), nw_ref[...])

    vec = _full_spec((1, LANE))
    return hosted_call(
        body, sides, name="ssd_fwd", grid=(nc,),
        in_specs=[_row_spec(BLK, CONV_DIM), _row_spec(BLK, SSD_WIDTH), _row_spec(BLK, LANE), vec, vec, vec,
                  _full_spec((1, SSD_WIDTH))],
        out_specs=[_row_spec(BLK, SSD_WIDTH), pl.BlockSpec((1, N_PAIR, LANE, D_STATE), lambda c: (c, 0, 0, 0))],
        out_shape=[jax.ShapeDtypeStruct((s, SSD_WIDTH), F32), jax.ShapeDtypeStruct((nc, N_PAIR, LANE, D_STATE), F32)],
        scratch_shapes=[pltpu.VMEM((N_PAIR, LANE, D_STATE), F32)],
        compiler_params=_params(1), operands=(u, z, dtr, dtb, alog, dsk, nw))


def ssd_bwd(u, z, dtr, st, dyo, dtb, alog, dsk, nw, sides=()):
    s = u.shape[0]
    nc = s // BLK

    def body(u_ref, z_ref, dt_ref, st_ref, dyo_ref, dtb_ref, alog_ref, dsk_ref, nw_ref,
             du_ref, dz_ref, ddt_ref, ddtb_ref, dalog_ref, ddsk_ref, dnw_ref, ds_ref):
        first = pl.program_id(0) == 0

        @pl.when(first)
        def _():
            ds_ref[...] = jnp.zeros_like(ds_ref)

        tri, pre, dt, a, heads = _ssd_chunk(u_ref, dt_ref, dtb_ref, alog_ref)
        shape = (BLK, LANE)
        lane = _iota(shape, 1)
        lo_lane = lane < HEAD_DIM
        lo_sub = _iota(shape, 0) < HEAD_DIM
        pairs = []
        ys = []
        for p in range(N_PAIR):
            g = p // 2
            h0, h1 = heads[2 * p], heads[2 * p + 1]
            bg = u_ref[:, B_LO + g * D_STATE:B_LO + (g + 1) * D_STATE]
            cg = u_ref[:, C_LO + g * D_STATE:C_LO + (g + 1) * D_STATE]
            xs = u_ref[:, p * LANE:(p + 1) * LANE]
            dtp = _pair_mix(lo_lane, h0["dt"], h1["dt"])
            xp = xs * dtp
            gm = dot_nt(cg, bg)
            m0, m1 = gm * h0["dm"], gm * h1["dm"]
            sp = st_ref[0, p]
            eap = _pair_mix(lo_lane, jnp.exp(h0["col"]), jnp.exp(h1["col"]))
            yo = eap * dot_nt(cg, sp)
            dskp = _pair_mix(lo_lane, _lane_col(dsk_ref[...], 2 * p), _lane_col(dsk_ref[...], 2 * p + 1))
            ys.append(_pair_mix(lo_lane, dot_nn(m0, xp), dot_nn(m1, xp)) + yo + xs * dskp)
            pairs.append(dict(bg=bg, cg=cg, xs=xs, dtp=dtp, xp=xp, gm=gm, m=(m0, m1), sp=sp, eap=eap, yo=yo, dskp=dskp))
        y = jnp.concatenate(ys, axis=1)
        zv = z_ref[...]
        sz = _sigmoid(zv)
        gate = zv * sz
        dyg, dnw = _rms_bwd(y * gate, nw_ref[...], dyo_ref[...])
        _acc(dnw_ref, first, dnw)
        dy = dyg * gate
        dz_ref[...] = dyg * y * (sz * (1.0 + zv * (1.0 - sz)))

        zeros = jnp.zeros(shape, F32)
        dacum_col = zeros
        dacum_row = zeros
        ddt = zeros
        ddsk = jnp.zeros((1, LANE), F32)
        dlast = jnp.zeros((1, LANE), F32)
        head_row = _iota((1, LANE), 1)
        sub = _iota(shape, 0)
        db = [zeros, zeros]
        dc = [zeros, zeros]
        for p in range(N_PAIR):
            g = p // 2
            q = pairs[p]
            dyp = dy[:, p * LANE:(p + 1) * LANE]
            dsn = ds_ref[p]
            t = dyp * q["xs"]
            dxs = dyp * q["dskp"]
            dcs = dyp * q["eap"]
            dc[g] = dc[g] + dot_nn(dcs, q["sp"])
            dsp = dot_tn(dcs, q["cg"])
            dea = dyp * q["yo"]
            elp = _pair_mix(lo_sub, jnp.exp(heads[2 * p]["last"]), jnp.exp(heads[2 * p + 1]["last"]))
            dsp = dsp + elp * dsn
            dels = dsn * q["sp"] * elp
            wp = _pair_mix(lo_lane, jnp.exp(heads[2 * p]["last"] - heads[2 * p]["col"]),
                           jnp.exp(heads[2 * p + 1]["last"] - heads[2 * p + 1]["col"]))
            dv = dot_nt(q["bg"], dsn)
            db[g] = db[g] + dot_nn(wp * q["xp"], dsn)
            dxp = dv * wp
            dwv = dv * q["xp"] * wp
            dgm = zeros
            for k in range(2):
                h = 2 * p + k
                mine = lo_lane if k == 0 else jnp.logical_not(lo_lane)
                mine_sub = lo_sub if k == 0 else jnp.logical_not(lo_sub)
                dyh = jnp.where(mine, dyp, 0.0)
                dm = dot_nt(dyh, q["xp"])
                dxp = dxp + dot_tn(q["m"][k], dyh)
                dgm = dgm + dm * heads[h]["dm"]
                e = dm * q["m"][k]
                onehot = lane == h
                dw_col = jnp.sum(jnp.where(mine, dwv, 0.0), axis=1, keepdims=True)
                col = (jnp.sum(e, axis=1, keepdims=True) + jnp.sum(jnp.where(mine, dea, 0.0), axis=1, keepdims=True)
                       - dw_col)
                dacum_col = dacum_col + jnp.where(onehot, col, 0.0)
                dacum_row = dacum_row - jnp.where(sub == h, jnp.sum(e, axis=0, keepdims=True), 0.0)
                dl = jnp.sum(dw_col) + jnp.sum(jnp.where(mine_sub, dels, 0.0))
                dlast = dlast + jnp.where(head_row == h, dl, 0.0)
                ddsk = ddsk + jnp.where(head_row == h, jnp.sum(jnp.where(mine, t, 0.0)), 0.0)
            dc[g] = dc[g] + dot_nn(dgm, q["bg"])
            db[g] = db[g] + dot_tn(dgm, q["cg"])
            dxs = dxs + dxp * q["dtp"]
            tt = dxp * q["xs"]
            for k in range(2):
                mine = lo_lane if k == 0 else jnp.logical_not(lo_lane)
                ddt = ddt + jnp.where(lane == 2 * p + k, jnp.sum(jnp.where(mine, tt, 0.0), axis=1, keepdims=True), 0.0)
            du_ref[:, p * LANE:(p + 1) * LANE] = dxs
            ds_ref[p] = dsp
        for g in range(2):
            du_ref[:, B_LO + g * D_STATE:B_LO + (g + 1) * D_STATE] = db[g]
            du_ref[:, C_LO + g * D_STATE:C_LO + (g + 1) * D_STATE] = dc[g]
        dacum = dacum_col + dacum_row.T + jnp.where(sub == BLK - 1, dlast, 0.0)
        dda = dot_tn(tri.astype(F32), dacum, exact=True)
        ddt = ddt + dda * a
        _acc(dalog_ref, first, jnp.sum(dda * dt, axis=0, keepdims=True) * a)
        dpre = ddt * _sigmoid(pre)
        ddt_ref[...] = dpre
        _acc(ddtb_ref, first, jnp.sum(dpre, axis=0, keepdims=True))
        _acc(ddsk_ref, first, ddsk)

    rev = lambda i: (nc - 1 - i, 0)
    vec = _full_spec((1, LANE))
    rows = lambda n: pl.BlockSpec((BLK, n), rev)
    return hosted_call(
        body, sides, name="ssd_bwd", grid=(nc,),
        in_specs=[rows(CONV_DIM), rows(SSD_WIDTH), rows(LANE),
                  pl.BlockSpec((1, N_PAIR, LANE, D_STATE), lambda i: (nc - 1 - i, 0, 0, 0)), rows(SSD_WIDTH),
                  vec, vec, vec, _full_spec((1, SSD_WIDTH))],
        out_specs=[rows(CONV_DIM), rows(SSD_WIDTH), rows(LANE), vec, vec, vec, _full_spec((1, SSD_WIDTH))],
        out_shape=[jax.ShapeDtypeStruct((s, CONV_DIM), F32), jax.ShapeDtypeStruct((s, SSD_WIDTH), F32),
                   jax.ShapeDtypeStruct((s, LANE), F32)] + [jax.ShapeDtypeStruct((1, LANE), F32)] * 3
        + [jax.ShapeDtypeStruct((1, SSD_WIDTH), F32)],
        scratch_shapes=[pltpu.VMEM((N_PAIR, LANE, D_STATE), F32)],
        compiler_params=_params(1), operands=(u, z, dtr, st, dyo, dtb, alog, dsk, nw))


SB_PAIRS = SB_WIDTH // LANE
SB_SCALE = HEAD_DIM ** -0.5


SB_TQ = 256


def _sb_tq(s):
    return min(SB_TQ, s)


def _sb_stack(x):
    lo_lane = _iota(x.shape, 1) < HEAD_DIM
    return jnp.concatenate([jnp.where(lo_lane, x, 0.0), jnp.where(lo_lane, 0.0, x)], axis=0)


def _sb_unstack(x2):
    tq = x2.shape[0] // 2
    lo_lane = _iota((tq, LANE), 1) < HEAD_DIM
    return jnp.where(lo_lane, x2[:tq], x2[tq:])


def _sb_logits(q2, kj, row0, col0, masked):
    shape = (q2.shape[0], BLK)
    tq = shape[0] // 2
    z = dot_nt(q2, kj)
    t = jnp.log(1.0 + jnp.exp(-jnp.abs(z)))
    ls = jnp.minimum(z, 0.0) - t
    lk = jnp.minimum(-z, 0.0) - t
    if not masked:
        return None, ls, lk
    row = _iota(shape, 0)
    valid = (col0 + _iota(shape, 1)) < (row0 + jnp.where(row < tq, row, row - tq))
    return valid, ls, jnp.where(valid, lk, 0.0)


def _sb_where(valid, x):
    return x if valid is None else jnp.where(valid, x, 0.0)


def _sums(x, mask2, parts):
    acc = None
    rest = x
    for _ in range(parts):
        term = rest.astype(BF16)
        rest = rest - term.astype(F32)
        d = lax.dot_general(term, mask2, (((1,), (0,)), ((), ())), preferred_element_type=F32)
        acc = d if acc is None else acc + d
    return acc[:, :BLK], acc[:, BLK:]


def _mask2(cond):
    return jnp.concatenate([cond.astype(BF16), jnp.ones(cond.shape, BF16)], axis=1)


def _sb_specs(s):
    tq = _sb_tq(s)
    qspec = pl.BlockSpec((tq, LANE), lambda p, i: (i, p))
    kspec = pl.BlockSpec((s, LANE), lambda p, i: (0, SB_PAIRS + p))
    vspec = pl.BlockSpec((s, LANE), lambda p, i: (0, 2 * SB_PAIRS + p))
    return qspec, kspec, vspec


SB_FLOOR = -104.0


def sb_fwd(qkv, sides=()):
    s = qkv.shape[0]
    tq = _sb_tq(s)
    kpq = tq // BLK

    def body(q_ref, k_ref, v_ref, o_ref, t_ref, n_ref, acc_ref):
        qi = pl.program_id(1)
        q2 = _sb_stack(q_ref[...] * SB_SCALE).astype(BF16)
        later = _mask2(_iota((BLK, BLK), 0) > _iota((BLK, BLK), 1))
        acc_ref[...] = jnp.zeros_like(acc_ref)

        def step(j, r, masked):
            rows = pl.ds(pl.multiple_of(j * BLK, BLK), BLK)
            valid, ls, lk = _sb_logits(q2, k_ref[rows, :], qi * tq, j * BLK, masked)
            after, total = _sums(lk, later, 2)
            w = _sb_where(valid, jnp.exp(ls + r + after))
            acc_ref[...] += dot_nn(w, v_ref[rows, :])
            return r + total

        r = jnp.zeros((2 * tq, LANE), F32)
        for d in reversed(range(kpq)):
            r = step(kpq * qi + d, r, True)

        def tile(g, r):
            for d in reversed(range(kpq)):
                r = step(kpq * (qi - 1 - g) + d, r, False)
            return r

        n, r = lax.while_loop(lambda c: jnp.logical_and(c[0] < qi, jnp.max(c[1]) > SB_FLOOR),
                              lambda c: (c[0] + 1, tile(c[0], c[1])), (jnp.int32(0), r))
        o_ref[...] = _sb_unstack(acc_ref[...])
        t_ref[...] = jnp.concatenate([r[:tq], r[tq:]], axis=1)
        n_ref[...] = jnp.zeros(n_ref.shape, F32) + n.astype(F32)

    return hosted_call(
        body, sides, name="sb_fwd", grid=(SB_PAIRS, s // tq),
        in_specs=list(_sb_specs(s)),
        out_specs=[pl.BlockSpec((tq, LANE), lambda p, i: (i, p)), pl.BlockSpec((tq, 2 * LANE), lambda p, i: (i, p)),
                   pl.BlockSpec((None, None, 8, LANE), lambda p, i: (p, i, 0, 0))],
        out_shape=[jax.ShapeDtypeStruct((s, SB_WIDTH), F32), jax.ShapeDtypeStruct((s, 2 * SB_WIDTH), F32),
                   jax.ShapeDtypeStruct((SB_PAIRS, s // tq, 8, LANE), F32)],
        scratch_shapes=[pltpu.VMEM((2 * tq, LANE), F32)],
        compiler_params=_params(2), operands=(qkv, qkv, qkv))


def sb_bwd(qkv, tot, swept, do, do_col=0, sides=()):
    s = qkv.shape[0]
    tq = _sb_tq(s)
    kpq = tq // BLK

    def body(q_ref, k_ref, v_ref, t_ref, n_ref, do_ref, dq_ref, dk_ref, dv_ref, acc_ref):
        qi = pl.program_id(1)
        n = jnp.clip(jnp.max(n_ref[...]).astype(jnp.int32), 0, qi)
        q2 = _sb_stack(q_ref[...] * SB_SCALE).astype(BF16)
        do2 = _sb_stack(do_ref[...]).astype(BF16)
        tot2 = jnp.concatenate([t_ref[:, :LANE], t_ref[:, LANE:]], axis=0)
        sq = (BLK, BLK)
        later = _mask2(_iota(sq, 0) > _iota(sq, 1))
        before = _mask2(_iota(sq, 0) < _iota(sq, 1))
        acc_ref[...] = jnp.zeros_like(acc_ref)

        @pl.when(qi == 0)
        def _():
            dk_ref[...] = jnp.zeros_like(dk_ref)
            dv_ref[...] = jnp.zeros_like(dv_ref)

        def step(j, carry, masked):
            rc, fc = carry
            rows = pl.ds(pl.multiple_of(j * BLK, BLK), BLK)
            kj = k_ref[rows, :]
            vj = v_ref[rows, :]
            valid, ls, lk = _sb_logits(q2, kj, qi * tq, j * BLK, masked)
            after, total = _sums(lk, later, 2)
            rc = rc - total
            w = _sb_where(valid, jnp.exp(ls + rc + after))
            e = w * dot_nt(do2, vj)
            f_in, f_tot = _sums(e, before, 2)
            sg = jnp.exp(ls)
            dz = _sb_where(valid, e * (1.0 - sg) - (fc + f_in) * sg)
            acc_ref[...] += dot_nn(dz, kj)
            dk_ref[rows, :] += dot_tn(dz, q2)
            dv_ref[rows, :] += dot_tn(w, do2)
            return rc, fc + f_tot

        def tile(g, carry):
            for d in range(kpq):
                carry = step(kpq * g + d, carry, False)
            return carry

        carry = lax.fori_loop(qi - n, qi, tile, (tot2, jnp.zeros((2 * tq, LANE), F32)))
        for d in range(kpq):
            carry = step(kpq * qi + d, carry, True)
        dq_ref[...] = SB_SCALE * _sb_unstack(acc_ref[...])

    qspec, kspec, vspec = _sb_specs(s)
    blk = pl.BlockSpec((tq, LANE), lambda p, i: (i, p))
    acc = pl.BlockSpec((s, LANE), lambda p, i: (0, p))
    return hosted_call(
        body, sides, name="sb_bwd", grid=(SB_PAIRS, s // tq),
        in_specs=[qspec, kspec, vspec, pl.BlockSpec((tq, 2 * LANE), lambda p, i: (i, p)),
                  pl.BlockSpec((None, None, 8, LANE), lambda p, i: (p, i, 0, 0)),
                  pl.BlockSpec((tq, LANE), lambda p, i: (i, do_col + p))],
        out_specs=[blk, acc, acc],
        out_shape=[jax.ShapeDtypeStruct((s, SB_WIDTH), F32)] * 3,
        scratch_shapes=[pltpu.VMEM((2 * tq, LANE), F32)],
        compiler_params=_params(2), operands=(qkv, qkv, qkv, tot, swept, do))


POOL_GROUP_DIM = POOL_WIDTH // len(POOL_WINDOWS)


assert all(w == 2 ** (i + 1) for i, w in enumerate(POOL_WINDOWS))


def _pool_inv(c):
    group = _iota((BLK, POOL_WIDTH), 1) // POOL_GROUP_DIM
    pos = c * BLK + _iota((BLK, POOL_WIDTH), 0)
    win = jnp.zeros((BLK, POOL_WIDTH), jnp.int32)
    for gi, wn in enumerate(POOL_WINDOWS):
        win = jnp.where(group == gi, wn, win)
    return 1.0 / jnp.minimum(pos + 1, win).astype(F32)


def _window_sums(ext, trailing):
    group = _iota(ext.shape, 1) // POOL_GROUP_DIM
    acc = ext
    out = None
    for gi in range(len(POOL_WINDOWS)):
        shift = 2 ** gi
        acc = acc + pltpu.roll(acc, shift if trailing else ext.shape[0] - shift, 0)
        out = acc if out is None else jnp.where(group == gi, acc, out)
    return out


def _pool_pooled(ext, cur, inv):
    return _window_sums(ext, True)[BLK:] * inv - cur


def pool_fwd(p, wblk, pb, ps):
    s, n = p.shape

    def body(cur_ref, prev_ref, w_ref, pb_ref, ps_ref, o_ref):
        c = pl.program_id(0)
        cur = cur_ref[...]
        prev = jnp.where(c > 0, prev_ref[...], 0.0)
        pooled = _pool_pooled(jnp.concatenate([prev, cur], axis=0), cur, _pool_inv(c))
        o_ref[...] = (dot_nn(pooled, w_ref[...]) + pb_ref[...]) * ps_ref[...]

    return pl.pallas_call(
        body, name="pool_fwd", grid=(s // BLK,),
        in_specs=[pl.BlockSpec((BLK, n), lambda c: (c, 0)), pl.BlockSpec((BLK, n), lambda c: (jnp.maximum(c - 1, 0), 0)),
                  _full_spec((n, n)), _full_spec((1, n)), _full_spec((1, n))],
        out_specs=pl.BlockSpec((BLK, n), lambda c: (c, 0)), out_shape=jax.ShapeDtypeStruct((s, n), F32),
        compiler_params=_params(1),
    )(p, p, wblk, pb, ps)


def pool_bwd(p, wblk, pb, ps, dout, do_col=0, sides=()):
    s, n = p.shape
    nb = s // BLK

    def body(cur_ref, prev_ref, w_ref, pb_ref, ps_ref, do_ref, dp_ref, dw_ref, dpb_ref, dps_ref, carry_ref):
        i = pl.program_id(0)
        c = nb - 1 - i
        first = i == 0
        cur = cur_ref[...]
        prev = jnp.where(c > 0, prev_ref[...], 0.0)
        inv = _pool_inv(c)
        pooled = _pool_pooled(jnp.concatenate([prev, cur], axis=0), cur, inv)
        mixed = dot_nn(pooled, w_ref[...]) + pb_ref[...]
        dov = do_ref[...]
        dmixed = dov * ps_ref[...]
        _acc(dps_ref, first, jnp.sum(dov * mixed, axis=0, keepdims=True))
        _acc(dpb_ref, first, jnp.sum(dmixed, axis=0, keepdims=True))
        _acc(dw_ref, first, dot_tn(pooled, dmixed))
        dpooled = dot_nt(dmixed, w_ref[...])
        dext = _window_sums(jnp.concatenate([jnp.zeros((BLK, n), F32), dpooled * inv], axis=0), False)

        @pl.when(first)
        def _():
            carry_ref[...] = jnp.zeros_like(carry_ref)

        dp_ref[...] = dext[BLK:] - dpooled + carry_ref[...]
        carry_ref[...] = dext[:BLK]

    rev = lambda i: (nb - 1 - i, 0)
    return hosted_call(
        body, sides, name="pool_bwd", grid=(nb,),
        in_specs=[pl.BlockSpec((BLK, n), rev), pl.BlockSpec((BLK, n), lambda i: (jnp.maximum(nb - 2 - i, 0), 0)),
                  _full_spec((n, n)), _full_spec((1, n)), _full_spec((1, n)),
                  pl.BlockSpec((BLK, n), lambda i: (nb - 1 - i, do_col))],
        out_specs=[pl.BlockSpec((BLK, n), rev), _full_spec((n, n)), _full_spec((1, n)), _full_spec((1, n))],
        out_shape=[jax.ShapeDtypeStruct((s, n), F32), jax.ShapeDtypeStruct((n, n), F32),
                   jax.ShapeDtypeStruct((1, n), F32), jax.ShapeDtypeStruct((1, n), F32)],
        scratch_shapes=[pltpu.VMEM((BLK, n), F32)],
        compiler_params=_params(1), operands=(p, p, wblk, pb, ps, dout))


def _row_tile(rows):
    if rows <= 512:
        return rows
    for t in (512, 256, 128, 64, 32, 16, 8):
        if rows % t == 0:
            return t
    return rows


def adamw(w, g, m, v):
    n, rows, cols = w.shape
    tr = _row_tile(rows)

    def body(w_ref, g_ref, m_ref, v_ref, d_ref, nm_ref, nv_ref):
        d_ref[...], nm_ref[...], nv_ref[...] = _adamw_math(w_ref[...], g_ref[...], m_ref[...], v_ref[...])

    spec = pl.BlockSpec((1, tr, cols), lambda i, j: (i, j, 0))
    return pl.pallas_call(
        body, name="adamw", grid=(n, rows // tr), in_specs=[spec] * 4, out_specs=[spec] * 3,
        out_shape=[jax.ShapeDtypeStruct(w.shape, F32)] * 3, compiler_params=_params(2),
    )(w, g, m, v)


def _adamw_math(w, g, m, v):
    nm = ADAM_B1 * m + (1.0 - ADAM_B1) * g
    nv = ADAM_B2 * v + (1.0 - ADAM_B2) * (g * g)
    m_hat = nm / (1.0 - ADAM_B1 ** ADAM_STEP)
    v_hat = nv / (1.0 - ADAM_B2 ** ADAM_STEP)
    return -ADAM_LR * (m_hat / (jnp.sqrt(v_hat) + ADAM_EPS) + ADAM_WD * w), nm, nv


def adamw_small(ws, gs, ms, vs):
    n = len(ws)

    def body(*refs):
        for i in range(n):
            outs = _adamw_math(*[refs[k * n + i][...] for k in range(4)])
            for k in range(3):
                refs[(4 + k) * n + i][...] = outs[k]

    vm = pl.BlockSpec(memory_space=pltpu.VMEM)
    outs = pl.pallas_call(
        body, name="adamw_small", in_specs=[vm] * (4 * n), out_specs=[vm] * (3 * n),
        out_shape=[jax.ShapeDtypeStruct(w.shape, F32) for w in ws] * 3,
    )(*ws, *gs, *ms, *vs)
    return outs[:n], outs[n:2 * n], outs[2 * n:]


def slab_sum(srcs, n_out, out_dtype, into=None, slot=0):
    _, rows, cols = srcs[0][0].shape
    tr = _row_tile(rows)
    n_src = len(srcs)
    sel = jnp.stack([jnp.asarray(base, jnp.int32) for _, base, _ in srcs])

    def body(sel_ref, *refs):
        acc = refs[0][...].astype(F32)
        for r in refs[1:n_src]:
            acc = acc + r[...].astype(F32)
        refs[-1][...] = acc.astype(out_dtype)

    def in_spec(k, step):
        return pl.BlockSpec((None, tr, cols), lambda o, i, sel_ref: (sel_ref[k] + step * o, i, 0))

    shape = (n_out, rows, cols) if into is None else into.shape
    return pl.pallas_call(
        body, name="slab_sum",
        grid_spec=pltpu.PrefetchScalarGridSpec(
            num_scalar_prefetch=1, grid=(n_out, rows // tr),
            in_specs=[in_spec(k, step) for k, (_, _, step) in enumerate(srcs)] + ([] if into is None else [ANY]),
            out_specs=pl.BlockSpec((None, tr, cols), lambda o, i, sel_ref: (slot + o, i, 0))),
        out_shape=jax.ShapeDtypeStruct(shape, out_dtype), compiler_params=_params(2),
        input_output_aliases={} if into is None else {1 + n_src: 0},
    )(sel, *[a for a, _, _ in srcs], *([] if into is None else [into]))


ICI_FLIPS = ((1, 0, 0), (0, 1, 0), (1, 1, 0))
D2D_FLIPS = ((0, 0, 1),)
ANY = pl.BlockSpec(memory_space=pl.ANY)


def _me():
    return lax.axis_index("x"), lax.axis_index("y"), lax.axis_index("c")


def _flipped(me, flip):
    return tuple(1 - m if f else m for m, f in zip(me, flip))


def _chip(dev):
    return 2 * dev[0] + dev[1]


def _dev(dev):
    return 4 * dev[0] + 2 * dev[1] + dev[2]


N_CHIP = 4
D2D = (0, 0, 1)


class Exchange:
    def __init__(self, xs, n_out, copies, own=None, in_place=False):
        self.xs, self.copies, self.own, self.in_place = list(xs), copies, own, in_place
        self.n_arr, self.n_cp = len(self.xs), len(copies)
        self.out_shape = [jax.ShapeDtypeStruct((n_out,) + x.shape[1:], x.dtype) for x in self.xs]
        self.scratch = [pltpu.SemaphoreType.DMA((self.n_arr * self.n_cp,)),
                        pltpu.SemaphoreType.DMA((self.n_arr * self.n_cp,)), pltpu.SemaphoreType.DMA((self.n_arr,))]

    def _own(self, x_refs, o_refs, sems, me):
        if self.own is None:
            return []
        return [pltpu.make_async_copy(x_refs[a].at[self.own[0](me)], o_refs[a].at[self.own[1](me)], sems[2].at[a])
                for a in range(self.n_arr)]

    def _copy(self, x_refs, o_refs, sems, me, a, j, sender):
        flip, src_slot, dst_slot = self.copies[j]
        k = a * self.n_cp + j
        return pltpu.make_async_remote_copy(
            src_ref=x_refs[a].at[src_slot(me)], dst_ref=o_refs[a].at[dst_slot(sender)],
            send_sem=sems[0].at[k], recv_sem=sems[1].at[k], device_id=_flipped(me, flip), device_id_type=MESH)

    def start(self, x_refs, o_refs, sems):
        me = _me()
        for cp in self._own(x_refs, o_refs, sems, me):
            cp.start()
        for j in range(self.n_cp):
            for a in range(self.n_arr):
                self._copy(x_refs, o_refs, sems, me, a, j, me).start()

    def wait(self, x_refs, o_refs, sems):
        me = _me()
        for j in range(self.n_cp):
            for a in range(self.n_arr):
                self._copy(x_refs, o_refs, sems, me, a, j, _flipped(me, self.copies[j][0])).wait_recv()
        for j in range(self.n_cp):
            for a in range(self.n_arr):
                self._copy(x_refs, o_refs, sems, me, a, j, me).wait_send()
        for cp in self._own(x_refs, o_refs, sems, me):
            cp.wait()

    def run(self, name):
        n = self.n_arr

        def body(*refs):
            self.start(refs[:n], refs[n:2 * n], refs[2 * n:])
            self.wait(refs[:n], refs[n:2 * n], refs[2 * n:])

        return pl.pallas_call(
            body, name=name, in_specs=[ANY] * n, out_specs=[ANY] * n, out_shape=self.out_shape,
            input_output_aliases={a: a for a in range(n)} if self.in_place else {}, scratch_shapes=self.scratch,
        )(*self.xs)


def hosted_call(body, sides, *, name, grid, in_specs, out_specs, out_shape, scratch_shapes, compiler_params, operands):
    n_in, n_out, n_scr = len(in_specs), len(out_specs), len(scratch_shapes)
    live = [s for s in sides if s is not None]
    if not live:
        outs = pl.pallas_call(body, name=name, grid=grid, in_specs=in_specs, out_specs=out_specs, out_shape=out_shape,
                              scratch_shapes=scratch_shapes, compiler_params=compiler_params)(*operands)
        return outs, [None] * len(sides)
    n = sum(s.n_arr for s in live)
    lo = [sum(s.n_arr for s in live[:i]) for i in range(len(live))]

    def full_body(*refs):
        ins, sx = refs[:n_in], refs[n_in:n_in + n]
        outs, so = refs[n_in + n:n_in + n + n_out], refs[n_in + n + n_out:n_in + 2 * n + n_out]
        scr, sems = refs[n_in + 2 * n + n_out:n_in + 2 * n + n_out + n_scr], refs[n_in + 2 * n + n_out + n_scr:]
        first = functools.reduce(jnp.logical_and, [pl.program_id(a) == 0 for a in range(len(grid))])
        last = functools.reduce(jnp.logical_and, [pl.program_id(a) == g - 1 for a, g in enumerate(grid)])
        parts = [(s, sx[l:l + s.n_arr], so[l:l + s.n_arr], sems[3 * i:3 * i + 3]) for i, (s, l) in enumerate(zip(live, lo))]

        @pl.when(first)
        def _():
            for s, x, o, m in parts:
                s.start(x, o, m)

        body(*ins, *outs, *scr)

        @pl.when(last)
        def _():
            for s, x, o, m in parts:
                s.wait(x, o, m)

    aliases = {n_in + l + a: n_out + l + a for s, l in zip(live, lo) if s.in_place for a in range(s.n_arr)}
    outs = pl.pallas_call(
        full_body, name=name + "_x", grid=grid, in_specs=list(in_specs) + [ANY] * n,
        out_specs=list(out_specs) + [ANY] * n, out_shape=list(out_shape) + [o for s in live for o in s.out_shape],
        input_output_aliases=aliases,
        scratch_shapes=list(scratch_shapes) + [m for s in live for m in s.scratch], compiler_params=compiler_params,
    )(*operands, *[x for s in live for x in s.xs])
    side_outs = iter([outs[n_out + l:n_out + l + s.n_arr] for s, l in zip(live, lo)])
    return outs[:n_out], [next(side_outs) if s is not None else None for s in sides]


def gather_ici(shards):
    ici = [(f, lambda me: 0, _dev) for f in ICI_FLIPS]
    return Exchange([s[None] for s in shards], N_DEV, ici, (lambda me: 0, _dev))


def gather_d2d(blocks):
    d2d = [(D2D, (lambda me, k=k: 2 * k + me[2]), (lambda sender, k=k: 2 * k + sender[2])) for k in range(N_CHIP)]
    return Exchange(blocks, N_DEV, d2d, None, in_place=True)


def gathered(blocks):
    return [b.reshape(-1, b.shape[2]) for b in blocks]


def scatter_d2d(parts):
    d2d = [(D2D, (lambda me, k=k: 2 * k + 1 - me[2]), (lambda sender, k=k: k)) for k in range(N_CHIP)]
    return Exchange(parts, N_CHIP, d2d)


def chip_sums(parts, sib):
    c = _me()[2]
    return [slab_sum([(p, c, 2), (s, 0, 1)], N_CHIP, BF16) for p, s in zip(parts, sib)]


def scatter_ici(sums):
    ici = [(f, (lambda me, f=f: _chip(_flipped(me, f))), (lambda sender, i=i: i)) for i, f in enumerate(ICI_FLIPS)]
    return Exchange(sums, len(ICI_FLIPS), ici)


def device_sums(sums, got, into=None, slot=0):
    x, y, _ = _me()
    outs = [slab_sum([(cs, 2 * x + y, 0)] + [(g, i, 0) for i in range(len(ICI_FLIPS))], 1, F32,
                     None if into is None else into[a], slot) for a, (cs, g) in enumerate(zip(sums, got))]
    return outs if into is not None else [o[0] for o in outs]


def all_gather(shards):
    blocks = gather_ici(shards).run("gather_ici")
    return gathered(gather_d2d(blocks).run("gather_d2d"))


def reduce_scatter(parts, *into):
    sums = chip_sums(parts, scatter_d2d(parts).run("scatter_d2d"))
    return device_sums(sums, scatter_ici(sums).run("scatter_ici"), *into)


def all_reduce_small(v):
    flips = D2D_FLIPS + ICI_FLIPS[:2]

    def body(v_ref, o_ref, got_ref, send_sems, recv_sems):
        me = _me()
        o_ref[...] = v_ref[...]
        for i, flip in enumerate(flips):
            cp = pltpu.make_async_remote_copy(
                src_ref=o_ref, dst_ref=got_ref.at[i], send_sem=send_sems.at[i], recv_sem=recv_sems.at[i],
                device_id=_flipped(me, flip), device_id_type=MESH)
            cp.start()
            cp.wait()
            o_ref[...] = o_ref[...] + got_ref[i]

    vm = pl.BlockSpec(memory_space=pltpu.VMEM)
    return pl.pallas_call(
        body, name="all_reduce_small", in_specs=[vm], out_specs=vm, out_shape=jax.ShapeDtypeStruct(v.shape, F32),
        scratch_shapes=[pltpu.VMEM((len(flips),) + v.shape, F32), pltpu.SemaphoreType.DMA((len(flips),)),
                        pltpu.SemaphoreType.DMA((len(flips),))],
    )(v)


def _perm_rows(wt):
    pad = jnp.zeros((D_IN_PAD - D_IN_PROJ, wt.shape[1]), wt.dtype)
    return jnp.concatenate([wt[:DT_LO], wt[DT_HI:], wt[DT_LO:DT_HI], pad], axis=0)


def _unperm_rows(dwt):
    n = D_IN_PROJ - (DT_HI - DT_LO)
    return jnp.concatenate([dwt[:DT_LO], dwt[n:D_IN_PROJ], dwt[DT_LO:n]], axis=0)


def _pad_lanes(v):
    return jnp.pad(v, ((0, 0), (0, LANE - v.shape[1])))[:, None]


def _block_diag(w):
    l, g, n, _ = w.shape
    out = jnp.zeros((l, g * n, g * n), w.dtype)
    for i in range(g):
        out = out.at[:, i * n:(i + 1) * n, i * n:(i + 1) * n].set(w[:, i])
    return out


def _pack(groups):
    flat = []
    for grp in groups:
        parts = [a.reshape(-1) for a in (grp if isinstance(grp, (list, tuple)) else [grp])]
        n = sum(p.shape[0] for p in parts)
        if -n % LANE:
            parts.append(jnp.zeros((-n % LANE,), parts[0].dtype))
        flat += parts
    return jnp.concatenate(flat).reshape(-1, LANE)


def _unpack(buf, shapes):
    out = []
    lo = 0
    buf = buf.reshape(-1)
    for shp in shapes:
        n = 1
        for k in shp:
            n *= k
        out.append(buf[lo:lo + n].reshape(shp))
        lo += n + (-n % LANE)
    return out


def small_params(w, conv_w_full):
    return dict(
        n1w=w["norm1_w"][:, None], cw=jnp.pad(conv_w_full, ((0, 0), (0, 8 - CONV_WIDTH), (0, 0))),
        cb=w["conv_b"][:, None], dtb=_pad_lanes(w["dt_bias"]), alog=_pad_lanes(w["a_log"]), dsk=_pad_lanes(w["d_skip"]),
        snw=w["ssd_norm_w"][:, None], wblk=_block_diag(w["pool_w"]), pb=w["pool_b"].reshape(-1, 1, POOL_WIDTH),
        ps=w["pool_scale"][:, None], n2w=w["norm2_w"][:, None])


MIX = ("w_in", "w_out")
FFN = ("w_gate", "w_up", "w_down")


def layer_params(small, l):
    return {k: v[l] for k, v in small.items()}


def mix_weights(whole):
    return _perm_rows(whole[0]), whole[1]


def _slabs(g):
    return g.reshape(N_DEV, -1, g.shape[-1])


def _layer_fwd(x, p, mix, ffn=None, ffn_shards=None, next_mix_shards=None):
    ici = [gather_ici([sh]) for sh in ffn_shards] if ffn_shards is not None else [None] * len(FFN)
    z, xbc, qkv, pp, dtr, h1 = inproj_fwd(x, p["n1w"], mix[0])
    (u,), (blk_g,) = conv_fwd(xbc, p["cw"], p["cb"], [ici[0]])
    (y_ssd, st), (blk_u,) = ssd_fwd(u, z, dtr, p["dtb"], p["alog"], p["dsk"], p["snw"], [ici[1]])
    (o, tot, swept), (blk_d,) = sb_fwd(qkv, [ici[2]])
    blocks = blk_g + blk_u + blk_d if ffn_shards is not None else None
    yp = pool_fwd(pp, p["wblk"], p["pb"], p["ps"])
    ycat = jnp.concatenate([y_ssd, o, yp], axis=1)
    (x_mid,), (blocks,) = outproj_fwd(ycat, mix[1], x, [gather_d2d(blocks) if blocks is not None else None])
    if blocks is not None:
        ffn = gathered(blocks)
    (x_out, g, uu), (nxt,) = ffn_fwd(x_mid, p["n2w"], *ffn,
                                     [gather_ici(next_mix_shards) if next_mix_shards is not None else None])
    sv = dict(x=x, z=z, xbc=xbc, qkv=qkv, pp=pp, dtr=dtr, h1=h1, u=u, st=st, tot=tot, swept=swept, ycat=ycat,
              x_mid=x_mid, g=g, uu=uu, w_in=mix[0], w_out=mix[1], wg=ffn[0], wu=ffn[1], wd=ffn[2])
    return x_out, sv, nxt


def _layer_bwd(dxo, sv, p, pending_mix=None, exchange=False, into_ffn=(), into_mix=()):
    (dx_mid, dn2w, a, dg, du, h2), (sib,) = ffn_bwd(
        dxo, sv["x_mid"], sv["g"], sv["uu"], p["n2w"], sv["wg"], sv["wu"], sv["wd"],
        [scatter_d2d(pending_mix) if pending_mix is not None else None])
    sums_mix = chip_sums(pending_mix, sib) if pending_mix is not None else None
    gr = dict(norm2_w=dn2w[0], w_gate=mm_tn(dg, h2), w_up=mm_tn(du, h2), w_down=mm_tn(a, dxo))
    parts = [_slabs(gr[k]) for k in FFN] if exchange else None
    (dycat,), (sib,) = outproj_bwd(dx_mid, sv["w_out"], [scatter_d2d(parts) if exchange else None])
    sums_ffn = chip_sums(parts, sib) if exchange else None
    gr["w_out"] = mm_tn(sv["ycat"], dx_mid)
    (dp, dwblk, dpb, dps), (got_g,) = pool_bwd(sv["pp"], p["wblk"], p["pb"], p["ps"], dycat,
                                               (SSD_WIDTH + SB_WIDTH) // POOL_WIDTH,
                                               [scatter_ici(sums_ffn[:1]) if exchange else None])
    n = POOL_GROUP_DIM
    gr["pool_w"] = jnp.stack([dwblk[i * n:(i + 1) * n, i * n:(i + 1) * n] for i in range(len(POOL_WINDOWS))])
    gr["pool_b"] = dpb.reshape(len(POOL_WINDOWS), n)
    gr["pool_scale"] = dps[0]
    (dq, dk, dv), (got_ud,) = sb_bwd(sv["qkv"], sv["tot"], sv["swept"], dycat, SSD_WIDTH // LANE,
                                     [scatter_ici(sums_ffn[1:]) if exchange else None])
    done_ffn = device_sums(sums_ffn, got_g + got_ud, *into_ffn) if exchange else None
    (du_, dz, ddtr, ddtb, dalog, ddsk, dsnw), (got_mix,) = ssd_bwd(
        sv["u"], sv["z"], sv["dtr"], sv["st"], dycat, p["dtb"], p["alog"], p["dsk"], p["snw"],
        [scatter_ici(sums_mix) if sums_mix is not None else None])
    done_mix = device_sums(sums_mix, got_mix, *into_mix) if sums_mix is not None else None
    gr.update(dt_bias=ddtb[0, :SSD_HEADS], a_log=dalog[0, :SSD_HEADS], d_skip=ddsk[0, :SSD_HEADS], ssd_norm_w=dsnw[0])
    dxbc, dcw, dcb = conv_bwd(sv["xbc"], p["cw"], p["cb"], du_)
    gr.update(conv_w=dcw[:CONV_WIDTH], conv_b=dcb[0])
    dproj = jnp.concatenate([dz, dxbc, dq, dk, dv, dp, ddtr], axis=1)
    dx, dn1w = inproj_bwd(dproj, sv["w_in"], sv["x"], p["n1w"], dx_mid)
    gr.update(norm1_w=dn1w[0], w_in=_unperm_rows(mm_tn(dproj, sv["h1"])))
    return dx, gr, done_ffn, done_mix


def local_step(x, tgt, params, weights, final_w):
    saved = []
    for p, (mix, ffn) in zip(params, weights):
        x, sv, _ = _layer_fwd(x, p, mix, ffn)
        saved.append(sv)
    loss, dx, dfw = head_loss(x, final_w[None], tgt)
    grads = []
    for p, sv in zip(reversed(params), reversed(saved)):
        dx, gr, _, _ = _layer_bwd(dx, sv, p)
        grads.append(gr)
    grads.reverse()
    return loss, dx, dfw[0], grads


WEIGHTS = ("norm1_w", "w_in", "conv_w", "conv_b", "dt_bias", "a_log", "d_skip", "ssd_norm_w", "pool_w", "pool_b",
           "pool_scale", "w_out", "norm2_w", "w_gate", "w_up", "w_down", "final_norm_w")
COL_SHARDED = ("w_in", "w_gate", "w_up")
ROW_SHARDED = ("w_out", "w_down")
SMALL = tuple(k for k in WEIGHTS if k not in COL_SHARDED + ROW_SHARDED)


def kernel(x, norm1_w, w_in, conv_w, conv_b, dt_bias, a_log, d_skip, ssd_norm_w, pool_w, pool_b, pool_scale, w_out, norm2_w, w_gate, w_up, w_down, final_norm_w, loss_target, m_norm1_w, m_w_in, m_conv_w, m_conv_b, m_dt_bias, m_a_log, m_d_skip, m_ssd_norm_w, m_pool_w, m_pool_b, m_pool_scale, m_w_out, m_norm2_w, m_w_gate, m_w_up, m_w_down, m_final_norm_w, v_norm1_w, v_w_in, v_conv_w, v_conv_b, v_dt_bias, v_a_log, v_d_skip, v_ssd_norm_w, v_pool_w, v_pool_b, v_pool_scale, v_w_out, v_norm2_w, v_w_gate, v_w_up, v_w_down, v_final_norm_w):
    args = dict(locals())
    w = {k: args[k] for k in WEIGHTS}
    m = {k: args["m_" + k] for k in WEIGHTS}
    v = {k: args["v_" + k] for k in WEIGHTS}
    depth = w_in.shape[0]
    dev = _dev(_me())
    n_cw = conv_w.shape[-1]

    shards = {k: (jnp.swapaxes(w[k], 1, 2) if k in COL_SHARDED else w[k]).astype(BF16) for k in MIX + FFN}
    whole = all_gather([jnp.swapaxes(conv_w, 0, 2).reshape(n_cw, -1)] + [shards[k][0] for k in MIX])
    conv_w_full = jnp.swapaxes(whole[0].reshape(N_DEV * n_cw, CONV_WIDTH, depth), 0, 2)
    small = small_params(w, conv_w_full)
    xs = x[0]
    params, saved = [layer_params(small, l) for l in range(depth)], []
    mix = mix_weights(whole[1:])
    for l in range(depth):
        xs, sv, nxt = _layer_fwd(xs, params[l], mix, ffn_shards=[shards[k][l] for k in FFN],
                                 next_mix_shards=[shards[k][l + 1] for k in MIX] if l + 1 < depth else None)
        saved.append(sv)
        if nxt is not None:
            mix = mix_weights(gathered(gather_d2d(nxt).run("gather_d2d")))
    loss, dx, dfw = head_loss(xs, final_norm_w[None], loss_target[0])
    layer_grads = [None] * depth
    native = {k: lax.empty((depth,) + shards[k].shape[1:], F32) for k in MIX + FFN}
    pending = None
    for l in reversed(range(depth)):
        dx, layer_grads[l], done_ffn, done_mix = _layer_bwd(
            dx, saved[l], params[l], pending, exchange=True, into_ffn=([native[k] for k in FFN], l),
            into_mix=([native[k] for k in MIX], l + 1))
        native.update(zip(FFN, done_ffn))
        if pending is not None:
            native.update(zip(MIX, done_mix))
        pending = [_slabs(layer_grads[l][k]) for k in MIX]
    native.update(zip(MIX, reduce_scatter(pending, [native[k] for k in MIX], 0)))
    grads = {k: jnp.swapaxes(native[k], 1, 2) if k in COL_SHARDED else native[k] for k in MIX + FFN}
    layered = [k for k in SMALL if k != "final_norm_w"]
    small_shapes = [(1, LANE)] + [(depth,) + layer_grads[0][k].shape for k in layered] + [dfw[0].shape]
    packed = _pack([loss] + [[layer_grads[l][k] for l in range(depth)] for k in layered] + [dfw[0]])
    summed = _unpack(all_reduce_small(packed), small_shapes)
    loss = summed[0][0, 0]
    grads.update(zip(layered + ["final_norm_w"], summed[1:]))
    grads["conv_w"] = lax.dynamic_slice_in_dim(grads["conv_w"], dev * n_cw, n_cw, axis=2)

    delta, new_m, new_v = {}, {}, {}
    for k in MIX + FFN:
        if k in ("w_gate", "w_up"):
            wt, mt, vt = (jnp.swapaxes(t, 1, 2) for t in (w[k], m[k], v[k]))
            delta[k], new_m[k], new_v[k] = [jnp.swapaxes(o, 1, 2) for o in adamw(wt, native[k], mt, vt)]
        else:
            delta[k], new_m[k], new_v[k] = adamw(w[k], grads[k], m[k], v[k])
    two_d = lambda a: a.reshape(1, -1) if a.ndim == 1 else a
    outs = adamw_small(*[[two_d(t[k]) for k in SMALL] for t in (w, grads, m, v)])
    for dst, arrs in zip((delta, new_m, new_v), outs):
        dst.update({k: a.reshape(w[k].shape) for k, a in zip(SMALL, arrs)})
    return (loss, dx[None], *[grads[k] for k in WEIGHTS], *[delta[k] for k in WEIGHTS],
            *[new_m[k] for k in WEIGHTS], *[new_v[k] for k in WEIGHTS])
```

```python
import functools

import jax
import jax.numpy as jnp
from jax import lax
from jax.experimental import pallas as pl
from jax.experimental.pallas import tpu as pltpu

F32 = jnp.float32
BF16 = jnp.bfloat16
HIGHEST = lax.Precision.HIGHEST
MESH = pl.DeviceIdType.MESH

EPS = 1e-6
D_MODEL = 1024
SSD_WIDTH = 512
SSD_HEADS = 8
HEAD_DIM = 64
D_STATE = 128
CONV_WIDTH = 4
CONV_DIM = 1024
SB_WIDTH = 256
POOL_WIDTH = 256
POOL_WINDOWS = (2, 4, 8, 16)
D_IN_PROJ = 2568
D_FF = 2816
N_DEV = 8
DEPTH = 4
SEG = (512, 1024, 768, 256, 128)
D_IN_PAD = sum(SEG)
DT_LO, DT_HI = 1536, 1544

LANE = 128
BLK = 128
ROW_TILE = 256
VMEM_LIMIT = 56 * 2**20

ADAM_LR, ADAM_B1, ADAM_B2, ADAM_EPS, ADAM_WD, ADAM_STEP = 0.001, 0.9, 0.999, 1e-08, 0.01, 10


def _params(n_axes=1, vmem=None):
    return pltpu.CompilerParams(dimension_semantics=("arbitrary",) * n_axes, vmem_limit_bytes=vmem)


def _dot(a, b, dims, exact=False):
    if exact:
        return lax.dot_general(a.astype(F32), b.astype(F32), (dims, ((), ())), precision=HIGHEST,
                               preferred_element_type=F32)
    return lax.dot_general(a.astype(BF16), b.astype(BF16), (dims, ((), ())), preferred_element_type=F32)


def dot_nn(a, b, exact=False):
    return _dot(a, b, ((1,), (0,)), exact)


def dot_nt(a, b, exact=False):
    return _dot(a, b, ((1,), (1,)), exact)


def dot_tn(a, b, exact=False):
    return _dot(a, b, ((0,), (0,)), exact)


def _iota(shape, axis):
    return lax.broadcasted_iota(jnp.int32, shape, axis)


def _lane_col(x, h):
    return jnp.sum(jnp.where(_iota(x.shape, 1) == h, x, 0.0), axis=1, keepdims=True)


def _sub_row(x, h):
    return jnp.sum(jnp.where(_iota(x.shape, 0) == h, x, 0.0), axis=0, keepdims=True)


def _sigmoid(x):
    return 1.0 / (1.0 + jnp.exp(-x))


def _rms_fwd(x, w):
    r = lax.rsqrt(jnp.mean(x * x, axis=-1, keepdims=True) + EPS)
    return x * r * w


def _rms_bwd(x, w, dy):
    r = lax.rsqrt(jnp.mean(x * x, axis=-1, keepdims=True) + EPS)
    xh = x * r
    dxh = dy * w
    dx = r * (dxh - xh * jnp.mean(dxh * xh, axis=-1, keepdims=True))
    return dx, jnp.sum(dy * xh, axis=0, keepdims=True)


def _acc(ref, first, val):
    @pl.when(first)
    def _():
        ref[...] = val

    @pl.when(jnp.logical_not(first))
    def _():
        ref[...] += val


def _row_spec(tm, n):
    return pl.BlockSpec((tm, n), lambda i: (i, 0))


def _full_spec(shape):
    return pl.BlockSpec(shape, lambda *_: (0,) * len(shape))


def inproj_fwd(x, nw, w):
    s, d = x.shape
    tm = min(ROW_TILE, s)

    def body(x_ref, nw_ref, w_ref, z_ref, xbc_ref, qkv_ref, p_ref, dt_ref, h_ref):
        h = _rms_fwd(x_ref[...], nw_ref[...]).astype(BF16)
        h_ref[...] = h
        lo = 0
        for ref, n in zip((z_ref, xbc_ref, qkv_ref, p_ref, dt_ref), SEG):
            ref[...] = dot_nt(h, w_ref[lo:lo + n, :])
            lo += n

    return pl.pallas_call(
        body, name="inproj_fwd", grid=(s // tm,),
        in_specs=[_row_spec(tm, d), _full_spec((1, d)), _full_spec(w.shape)],
        out_specs=[_row_spec(tm, n) for n in SEG] + [_row_spec(tm, d)],
        out_shape=[jax.ShapeDtypeStruct((s, n), F32) for n in SEG] + [jax.ShapeDtypeStruct((s, d), BF16)],
        compiler_params=_params(1, VMEM_LIMIT),
    )(x, nw, w)


def inproj_bwd(pieces, w, x, nw, dres):
    s, d = x.shape
    tm = min(ROW_TILE, s)
    n_p = len(pieces)
    widths = [p.shape[1] for p in pieces]

    def body(*refs):
        w_ref, x_ref, nw_ref, dres_ref, dx_ref, dnw_ref, dp_ref = refs[n_p:]
        dh = None
        lo = 0
        for ref, n in zip(refs[:n_p], widths):
            piece = ref[...].astype(BF16)
            dp_ref[:, lo:lo + n] = piece
            part = dot_nn(piece, w_ref[lo:lo + n, :])
            dh = part if dh is None else dh + part
            lo += n
        dx, dnw = _rms_bwd(x_ref[...], nw_ref[...], dh)
        dx_ref[...] = dres_ref[...] + dx
        _acc(dnw_ref, pl.program_id(0) == 0, dnw)

    return pl.pallas_call(
        body, name="inproj_bwd", grid=(s // tm,),
        in_specs=[_row_spec(tm, n) for n in widths] + [_full_spec(w.shape), _row_spec(tm, d), _full_spec((1, d)),
                                                       _row_spec(tm, d)],
        out_specs=[_row_spec(tm, d), _full_spec((1, d)), _row_spec(tm, sum(widths))],
        out_shape=[jax.ShapeDtypeStruct((s, d), F32), jax.ShapeDtypeStruct((1, d), F32),
                   jax.ShapeDtypeStruct((s, sum(widths)), BF16)],
        compiler_params=_params(1, VMEM_LIMIT),
    )(*pieces, w, x, nw, dres)


def outproj_fwd(pieces, w, res, sides=()):
    s, d = res.shape
    tm = min(ROW_TILE, s)
    n_p = len(pieces)
    widths = [p.shape[1] for p in pieces]

    def body(*refs):
        w_ref, r_ref, o_ref, y_ref = refs[n_p:]
        acc = r_ref[...]
        lo = 0
        for ref, n in zip(refs[:n_p], widths):
            piece = ref[...].astype(BF16)
            y_ref[:, lo:lo + n] = piece
            acc = acc + dot_nn(piece, w_ref[lo:lo + n, :])
            lo += n
        o_ref[...] = acc

    return hosted_call(
        body, sides, name="outproj_fwd", grid=(s // tm,),
        in_specs=[_row_spec(tm, n) for n in widths] + [_full_spec(w.shape), _row_spec(tm, d)],
        out_specs=[_row_spec(tm, d), _row_spec(tm, sum(widths))],
        out_shape=[jax.ShapeDtypeStruct((s, d), F32), jax.ShapeDtypeStruct((s, sum(widths)), BF16)],
        scratch_shapes=[], compiler_params=_params(1, VMEM_LIMIT), operands=(*pieces, w, res))


def outproj_bwd(dx, w, sides=()):
    s, d = dx.shape
    tm = min(ROW_TILE, s)

    def body(dx_ref, w_ref, o_ref):
        o_ref[...] = dot_nt(dx_ref[...], w_ref[...])

    return hosted_call(
        body, sides, name="outproj_bwd", grid=(s // tm,),
        in_specs=[_row_spec(tm, d), _full_spec(w.shape)],
        out_specs=[_row_spec(tm, w.shape[0])], out_shape=[jax.ShapeDtypeStruct((s, w.shape[0]), F32)],
        scratch_shapes=[], compiler_params=_params(1, VMEM_LIMIT), operands=(dx, w))


def ffn_fwd(x, nw, wg, wu, wd, sides=()):
    s, d = x.shape
    f = wg.shape[0]
    tm = min(ROW_TILE, s)

    def body(x_ref, nw_ref, wg_ref, wu_ref, wd_ref, o_ref, g_ref, u_ref):
        xv = x_ref[...]
        h = _rms_fwd(xv, nw_ref[...]).astype(BF16)
        g = dot_nt(h, wg_ref[...])
        u = dot_nt(h, wu_ref[...])
        g_ref[...] = g.astype(BF16)
        u_ref[...] = u.astype(BF16)
        o_ref[...] = xv + dot_nn(g * _sigmoid(g) * u, wd_ref[...])

    return hosted_call(
        body, sides, name="ffn_fwd", grid=(s // tm,),
        in_specs=[_row_spec(tm, d), _full_spec((1, d)), _full_spec(wg.shape), _full_spec(wu.shape),
                  _full_spec(wd.shape)],
        out_specs=[_row_spec(tm, d), _row_spec(tm, f), _row_spec(tm, f)],
        out_shape=[jax.ShapeDtypeStruct((s, d), F32), jax.ShapeDtypeStruct((s, f), BF16),
                   jax.ShapeDtypeStruct((s, f), BF16)],
        scratch_shapes=[], compiler_params=_params(1, VMEM_LIMIT), operands=(x, nw, wg, wu, wd))


def ffn_bwd(dxo, x, g, u, nw, wg, wu, wd, sides=()):
    s, d = x.shape
    f = wg.shape[0]
    tm = min(ROW_TILE, s)

    def body(dxo_ref, x_ref, g_ref, u_ref, nw_ref, wg_ref, wu_ref, wd_ref, dx_ref, dnw_ref, a_ref, dg_ref,
             du_ref, h_ref):
        dxo_v = dxo_ref[...]
        xv = x_ref[...]
        da = dot_nt(dxo_v, wd_ref[...])
        gv = g_ref[...].astype(F32)
        uv = u_ref[...].astype(F32)
        sg = _sigmoid(gv)
        sl = gv * sg
        a_ref[...] = (sl * uv).astype(BF16)
        dg = (da * uv * (sg * (1.0 + gv * (1.0 - sg)))).astype(BF16)
        du = (da * sl).astype(BF16)
        dg_ref[...] = dg
        du_ref[...] = du
        dh = dot_nn(dg, wg_ref[...]) + dot_nn(du, wu_ref[...])
        h_ref[...] = _rms_fwd(xv, nw_ref[...]).astype(BF16)
        dx, dnw = _rms_bwd(xv, nw_ref[...], dh)
        dx_ref[...] = dxo_v + dx
        _acc(dnw_ref, pl.program_id(0) == 0, dnw)

    return hosted_call(
        body, sides, name="ffn_bwd", grid=(s // tm,),
        in_specs=[_row_spec(tm, d), _row_spec(tm, d), _row_spec(tm, f), _row_spec(tm, f), _full_spec((1, d)),
                  _full_spec(wg.shape), _full_spec(wu.shape), _full_spec(wd.shape)],
        out_specs=[_row_spec(tm, d), _full_spec((1, d)), _row_spec(tm, f), _row_spec(tm, f), _row_spec(tm, f),
                   _row_spec(tm, d)],
        out_shape=[jax.ShapeDtypeStruct((s, d), F32), jax.ShapeDtypeStruct((1, d), F32),
                   jax.ShapeDtypeStruct((s, f), BF16), jax.ShapeDtypeStruct((s, f), BF16),
                   jax.ShapeDtypeStruct((s, f), BF16), jax.ShapeDtypeStruct((s, d), BF16)],
        scratch_shapes=[], compiler_params=_params(1, VMEM_LIMIT), operands=(dxo, x, g, u, nw, wg, wu, wd))


def _tile(n, cap=256):
    best = LANE
    for t in range(LANE, cap + 1, LANE):
        if n % t == 0:
            best = t
    return best


def mm_tn(a, b):
    s, k = a.shape
    n = b.shape[1]
    tk = _tile(k)

    def body(a_ref, b_ref, o_ref):
        o_ref[...] = dot_nn(a_ref[...].astype(BF16).T, b_ref[...]).astype(BF16)

    return pl.pallas_call(
        body, name="mm_tn", grid=(k // tk,),
        in_specs=[pl.BlockSpec((s, tk), lambda i: (0, i)), _full_spec((s, n))],
        out_specs=pl.BlockSpec((tk, n), lambda i: (i, 0)), out_shape=jax.ShapeDtypeStruct((k, n), BF16),
        compiler_params=_params(1, VMEM_LIMIT),
    )(a, b)


def head_loss(x, fw, tgt):
    s, d = x.shape
    tm = min(ROW_TILE, s)

    def body(x_ref, fw_ref, t_ref, loss_ref, dx_ref, dfw_ref):
        xv = x_ref[...]
        err = _rms_fwd(xv, fw_ref[...]) - t_ref[...]
        part = jnp.zeros((1, LANE), F32) + 0.5 * jnp.sum(err * err) / d
        dx, dfw = _rms_bwd(xv, fw_ref[...], err / d)
        dx_ref[...] = dx
        first = pl.program_id(0) == 0
        _acc(loss_ref, first, part)
        _acc(dfw_ref, first, dfw)

    return pl.pallas_call(
        body, name="head_loss", grid=(s // tm,),
        in_specs=[_row_spec(tm, d), _full_spec((1, d)), _row_spec(tm, d)],
        out_specs=[_full_spec((1, LANE)), _row_spec(tm, d), _full_spec((1, d))],
        out_shape=[jax.ShapeDtypeStruct((1, LANE), F32), jax.ShapeDtypeStruct((s, d), F32),
                   jax.ShapeDtypeStruct((1, d), F32)],
        compiler_params=_params(1),
    )(x, fw, tgt)


def _conv_pre(ext, cw_ref, cb_ref):
    shifted = [pltpu.roll(ext, CONV_WIDTH - 1 - i, 0)[BLK:] if i < CONV_WIDTH - 1 else ext[BLK:]
               for i in range(CONV_WIDTH)]
    acc = cb_ref[...] + sum(cw_ref[i:i + 1, :] * shifted[i] for i in range(CONV_WIDTH))
    return acc, shifted


def conv_fwd(xbc, cw, cb, sides=()):
    s, n = xbc.shape

    def body(cur_ref, prev_ref, cw_ref, cb_ref, o_ref):
        prev = jnp.where(pl.program_id(0) > 0, prev_ref[...], 0.0)
        acc, _ = _conv_pre(jnp.concatenate([prev, cur_ref[...]], axis=0), cw_ref, cb_ref)
        o_ref[...] = acc * _sigmoid(acc)

    return hosted_call(
        body, sides, name="conv_fwd", grid=(s // BLK,),
        in_specs=[pl.BlockSpec((BLK, n), lambda c: (c, 0)), pl.BlockSpec((BLK, n), lambda c: (jnp.maximum(c - 1, 0), 0)),
                  _full_spec(cw.shape), _full_spec((1, n))],
        out_specs=[pl.BlockSpec((BLK, n), lambda c: (c, 0))], out_shape=[jax.ShapeDtypeStruct((s, n), F32)],
        scratch_shapes=[], compiler_params=_params(1), operands=(xbc, xbc, cw, cb))


def conv_bwd(xbc, cw, cb, du):
    s, n = xbc.shape
    nb = s // BLK

    def body(cur_ref, prev_ref, cw_ref, cb_ref, du_ref, dx_ref, dcw_ref, dcb_ref, nxt_ref):
        i = pl.program_id(0)
        c = nb - 1 - i
        prev = jnp.where(c > 0, prev_ref[...], 0.0)
        acc, shifted = _conv_pre(jnp.concatenate([prev, cur_ref[...]], axis=0), cw_ref, cb_ref)
        sg = _sigmoid(acc)
        dacc = du_ref[...] * (sg * (1.0 + acc * (1.0 - sg)))

        @pl.when(i == 0)
        def _():
            nxt_ref[...] = jnp.zeros_like(nxt_ref)
            dcw_ref[...] = jnp.zeros_like(dcw_ref)
            dcb_ref[...] = jnp.zeros_like(dcb_ref)

        dcb_ref[...] += jnp.sum(dacc, axis=0, keepdims=True)
        for t in range(CONV_WIDTH):
            dcw_ref[t:t + 1, :] += jnp.sum(dacc * shifted[t], axis=0, keepdims=True)
        ext = jnp.concatenate([dacc, nxt_ref[...]], axis=0)
        dx = cw_ref[CONV_WIDTH - 1:CONV_WIDTH, :] * dacc
        for t in range(CONV_WIDTH - 1):
            dx += cw_ref[t:t + 1, :] * pltpu.roll(ext, 2 * BLK - (CONV_WIDTH - 1 - t), 0)[:BLK]
        dx_ref[...] = dx
        nxt_ref[...] = dacc

    rev = lambda i: (nb - 1 - i, 0)
    return pl.pallas_call(
        body, name="conv_bwd", grid=(nb,),
        in_specs=[pl.BlockSpec((BLK, n), rev), pl.BlockSpec((BLK, n), lambda i: (jnp.maximum(nb - 2 - i, 0), 0)),
                  _full_spec(cw.shape), _full_spec((1, n)), pl.BlockSpec((BLK, n), rev)],
        out_specs=[pl.BlockSpec((BLK, n), rev), _full_spec((8, n)), _full_spec((1, n))],
        out_shape=[jax.ShapeDtypeStruct((s, n), F32), jax.ShapeDtypeStruct((8, n), F32),
                   jax.ShapeDtypeStruct((1, n), F32)],
        scratch_shapes=[pltpu.VMEM((BLK, n), F32)],
        compiler_params=_params(1),
    )(xbc, xbc, cw, cb, du)


N_PAIR = SSD_HEADS // 2
B_LO = SSD_WIDTH
C_LO = SSD_WIDTH + 2 * D_STATE


def _softplus(x):
    return jnp.maximum(x, 0.0) + jnp.log(1.0 + jnp.exp(-jnp.abs(x)))


def _ssd_chunk(u_ref, dt_ref, dtb_ref, alog_ref):
    shape = (BLK, BLK)
    tri = _iota(shape, 1) <= _iota(shape, 0)
    pre = dt_ref[...] + dtb_ref[...]
    dt = _softplus(pre)
    a = -jnp.exp(alog_ref[...])
    acum = dot_nn(tri.astype(F32), dt * a, exact=True)
    acum_t = acum.T
    last = _sub_row(acum, BLK - 1)
    heads = []
    for h in range(SSD_HEADS):
        col = _lane_col(acum, h)
        seg = jnp.where(tri, col - _sub_row(acum_t, h), -1e30)
        heads.append(dict(col=col, dm=jnp.exp(seg), dt=_lane_col(dt, h), last=_lane_col(last, h)))
    return tri, pre, dt, a, heads


def _pair_mix(lo_mask, v0, v1):
    return jnp.where(lo_mask, v0, v1)


def ssd_fwd(u, z, dtr, dtb, alog, dsk, nw, sides=()):
    s = u.shape[0]
    nc = s // BLK

    def body(u_ref, z_ref, dt_ref, dtb_ref, alog_ref, dsk_ref, nw_ref, y_ref, st_ref, s_ref):
        @pl.when(pl.program_id(0) == 0)
        def _():
            s_ref[...] = jnp.zeros_like(s_ref)

        _, _, _, _, heads = _ssd_chunk(u_ref, dt_ref, dtb_ref, alog_ref)
        lo_lane = _iota((BLK, LANE), 1) < HEAD_DIM
        lo_sub = _iota((BLK, LANE), 0) < HEAD_DIM
        ys = []
        for p in range(N_PAIR):
            g = p // 2
            h0, h1 = heads[2 * p], heads[2 * p + 1]
            bg = u_ref[:, B_LO + g * D_STATE:B_LO + (g + 1) * D_STATE]
            cg = u_ref[:, C_LO + g * D_STATE:C_LO + (g + 1) * D_STATE]
            xs = u_ref[:, p * LANE:(p + 1) * LANE]
            xp = xs * _pair_mix(lo_lane, h0["dt"], h1["dt"])
            gm = dot_nt(cg, bg)
            yd = _pair_mix(lo_lane, dot_nn(gm * h0["dm"], xp), dot_nn(gm * h1["dm"], xp))
            sp = s_ref[p]
            st_ref[0, p] = sp
            yo = _pair_mix(lo_lane, jnp.exp(h0["col"]), jnp.exp(h1["col"])) * dot_nt(cg, sp)
            dskp = _pair_mix(lo_lane, _lane_col(dsk_ref[...], 2 * p), _lane_col(dsk_ref[...], 2 * p + 1))
            ys.append(yd + yo + xs * dskp)
            wp = _pair_mix(lo_lane, jnp.exp(h0["last"] - h0["col"]), jnp.exp(h1["last"] - h1["col"]))
            el = _pair_mix(lo_sub, jnp.exp(h0["last"]), jnp.exp(h1["last"]))
            s_ref[p] = el * sp + dot_tn(wp * xp, bg)
        y = jnp.concatenate(ys, axis=1)
        zv = z_ref[...]
        y_ref[...] = _rms_fwd(y * zv * _sigmoid(zv), nw_ref[...])

    vec = _full_spec((1, LANE))
    return hosted_call(
        body, sides, name="ssd_fwd", grid=(nc,),
        in_specs=[_row_spec(BLK, CONV_DIM), _row_spec(BLK, SSD_WIDTH), _row_spec(BLK, LANE), vec, vec, vec,
                  _full_spec((1, SSD_WIDTH))],
        out_specs=[_row_spec(BLK, SSD_WIDTH), pl.BlockSpec((1, N_PAIR, LANE, D_STATE), lambda c: (c, 0, 0, 0))],
        out_shape=[jax.ShapeDtypeStruct((s, SSD_WIDTH), F32), jax.ShapeDtypeStruct((nc, N_PAIR, LANE, D_STATE), F32)],
        scratch_shapes=[pltpu.VMEM((N_PAIR, LANE, D_STATE), F32)],
        compiler_params=_params(1), operands=(u, z, dtr, dtb, alog, dsk, nw))


def ssd_bwd(u, z, dtr, st, dyo, dtb, alog, dsk, nw, sides=()):
    s = u.shape[0]
    nc = s // BLK

    def body(u_ref, z_ref, dt_ref, st_ref, dyo_ref, dtb_ref, alog_ref, dsk_ref, nw_ref,
             du_ref, dz_ref, ddt_ref, ddtb_ref, dalog_ref, ddsk_ref, dnw_ref, ds_ref):
        first = pl.program_id(0) == 0

        @pl.when(first)
        def _():
            ds_ref[...] = jnp.zeros_like(ds_ref)

        tri, pre, dt, a, heads = _ssd_chunk(u_ref, dt_ref, dtb_ref, alog_ref)
        shape = (BLK, LANE)
        lane = _iota(shape, 1)
        lo_lane = lane < HEAD_DIM
        lo_sub = _iota(shape, 0) < HEAD_DIM
        pairs = []
        ys = []
        for p in range(N_PAIR):
            g = p // 2
            h0, h1 = heads[2 * p], heads[2 * p + 1]
            bg = u_ref[:, B_LO + g * D_STATE:B_LO + (g + 1) * D_STATE]
            cg = u_ref[:, C_LO + g * D_STATE:C_LO + (g + 1) * D_STATE]
            xs = u_ref[:, p * LANE:(p + 1) * LANE]
            dtp = _pair_mix(lo_lane, h0["dt"], h1["dt"])
            xp = xs * dtp
            gm = dot_nt(cg, bg)
            m0, m1 = gm * h0["dm"], gm * h1["dm"]
            sp = st_ref[0, p]
            eap = _pair_mix(lo_lane, jnp.exp(h0["col"]), jnp.exp(h1["col"]))
            yo = eap * dot_nt(cg, sp)
            dskp = _pair_mix(lo_lane, _lane_col(dsk_ref[...], 2 * p), _lane_col(dsk_ref[...], 2 * p + 1))
            ys.append(_pair_mix(lo_lane, dot_nn(m0, xp), dot_nn(m1, xp)) + yo + xs * dskp)
            pairs.append(dict(bg=bg, cg=cg, xs=xs, dtp=dtp, xp=xp, gm=gm, m=(m0, m1), sp=sp, eap=eap, yo=yo, dskp=dskp))
        y = jnp.concatenate(ys, axis=1)
        zv = z_ref[...]
        sz = _sigmoid(zv)
        gate = zv * sz
        dyg, dnw = _rms_bwd(y * gate, nw_ref[...], dyo_ref[...])
        _acc(dnw_ref, first, dnw)
        dy = dyg * gate
        dz_ref[...] = dyg * y * (sz * (1.0 + zv * (1.0 - sz)))

        zeros = jnp.zeros(shape, F32)
        dacum_col = zeros
        dacum_row = zeros
        ddt = zeros
        ddsk = jnp.zeros((1, LANE), F32)
        dlast = jnp.zeros((1, LANE), F32)
        head_row = _iota((1, LANE), 1)
        sub = _iota(shape, 0)
        db = [zeros, zeros]
        dc = [zeros, zeros]
        for p in range(N_PAIR):
            g = p // 2
            q = pairs[p]
            dyp = dy[:, p * LANE:(p + 1) * LANE]
            dsn = ds_ref[p]
            t = dyp * q["xs"]
            dxs = dyp * q["dskp"]
            dcs = dyp * q["eap"]
            dc[g] = dc[g] + dot_nn(dcs, q["sp"])
            dsp = dot_tn(dcs, q["cg"])
            dea = dyp * q["yo"]
            elp = _pair_mix(lo_sub, jnp.exp(heads[2 * p]["last"]), jnp.exp(heads[2 * p + 1]["last"]))
            dsp = dsp + elp * dsn
            dels = dsn * q["sp"] * elp
            wp = _pair_mix(lo_lane, jnp.exp(heads[2 * p]["last"] - heads[2 * p]["col"]),
                           jnp.exp(heads[2 * p + 1]["last"] - heads[2 * p + 1]["col"]))
            dv = dot_nt(q["bg"], dsn)
            db[g] = db[g] + dot_nn(wp * q["xp"], dsn)
            dxp = dv * wp
            dwv = dv * q["xp"] * wp
            dgm = zeros
            for k in range(2):
                h = 2 * p + k
                mine = lo_lane if k == 0 else jnp.logical_not(lo_lane)
                mine_sub = lo_sub if k == 0 else jnp.logical_not(lo_sub)
                dyh = jnp.where(mine, dyp, 0.0)
                dm = dot_nt(dyh, q["xp"])
                dxp = dxp + dot_tn(q["m"][k], dyh)
                dgm = dgm + dm * heads[h]["dm"]
                e = dm * q["m"][k]
                onehot = lane == h
                dw_col = jnp.sum(jnp.where(mine, dwv, 0.0), axis=1, keepdims=True)
                col = (jnp.sum(e, axis=1, keepdims=True) + jnp.sum(jnp.where(mine, dea, 0.0), axis=1, keepdims=True)
                       - dw_col)
                dacum_col = dacum_col + jnp.where(onehot, col, 0.0)
                dacum_row = dacum_row - jnp.where(sub == h, jnp.sum(e, axis=0, keepdims=True), 0.0)
                dl = jnp.sum(dw_col) + jnp.sum(jnp.where(mine_sub, dels, 0.0))
                dlast = dlast + jnp.where(head_row == h, dl, 0.0)
                ddsk = ddsk + jnp.where(head_row == h, jnp.sum(jnp.where(mine, t, 0.0)), 0.0)
            dc[g] = dc[g] + dot_nn(dgm, q["bg"])
            db[g] = db[g] + dot_tn(dgm, q["cg"])
            dxs = dxs + dxp * q["dtp"]
            tt = dxp * q["xs"]
            for k in range(2):
                mine = lo_lane if k == 0 else jnp.logical_not(lo_lane)
                ddt = ddt + jnp.where(lane == 2 * p + k, jnp.sum(jnp.where(mine, tt, 0.0), axis=1, keepdims=True), 0.0)
            du_ref[:, p * LANE:(p + 1) * LANE] = dxs
            ds_ref[p] = dsp
        for g in range(2):
            du_ref[:, B_LO + g * D_STATE:B_LO + (g + 1) * D_STATE] = db[g]
            du_ref[:, C_LO + g * D_STATE:C_LO + (g + 1) * D_STATE] = dc[g]
        dacum = dacum_col + dacum_row.T + jnp.where(sub == BLK - 1, dlast, 0.0)
        dda = dot_tn(tri.astype(F32), dacum, exact=True)
        ddt = ddt + dda * a
        _acc(dalog_ref, first, jnp.sum(dda * dt, axis=0, keepdims=True) * a)
        dpre = ddt * _sigmoid(pre)
        ddt_ref[...] = dpre
        _acc(ddtb_ref, first, jnp.sum(dpre, axis=0, keepdims=True))
        _acc(ddsk_ref, first, ddsk)

    rev = lambda i: (nc - 1 - i, 0)
    vec = _full_spec((1, LANE))
    rows = lambda n: pl.BlockSpec((BLK, n), rev)
    return hosted_call(
        body, sides, name="ssd_bwd", grid=(nc,),
        in_specs=[rows(CONV_DIM), rows(SSD_WIDTH), rows(LANE),
                  pl.BlockSpec((1, N_PAIR, LANE, D_STATE), lambda i: (nc - 1 - i, 0, 0, 0)), rows(SSD_WIDTH),
                  vec, vec, vec, _full_spec((1, SSD_WIDTH))],
        out_specs=[rows(CONV_DIM), rows(SSD_WIDTH), rows(LANE), vec, vec, vec, _full_spec((1, SSD_WIDTH))],
        out_shape=[jax.ShapeDtypeStruct((s, CONV_DIM), F32), jax.ShapeDtypeStruct((s, SSD_WIDTH), F32),
                   jax.ShapeDtypeStruct((s, LANE), F32)] + [jax.ShapeDtypeStruct((1, LANE), F32)] * 3
        + [jax.ShapeDtypeStruct((1, SSD_WIDTH), F32)],
        scratch_shapes=[pltpu.VMEM((N_PAIR, LANE, D_STATE), F32)],
        compiler_params=_params(1), operands=(u, z, dtr, st, dyo, dtb, alog, dsk, nw))


SB_PAIRS = SB_WIDTH // LANE
SB_SCALE = HEAD_DIM ** -0.5


SB_TQ = 256


def _sb_tq(s):
    return min(SB_TQ, s)


def _sb_stack(x):
    lo_lane = _iota(x.shape, 1) < HEAD_DIM
    return jnp.concatenate([jnp.where(lo_lane, x, 0.0), jnp.where(lo_lane, 0.0, x)], axis=0)


def _sb_unstack(x2):
    tq = x2.shape[0] // 2
    lo_lane = _iota((tq, LANE), 1) < HEAD_DIM
    return jnp.where(lo_lane, x2[:tq], x2[tq:])


def _sb_logits(q2, kj, row0, col0, masked):
    shape = (q2.shape[0], BLK)
    tq = shape[0] // 2
    z = dot_nt(q2, kj)
    t = jnp.log(1.0 + jnp.exp(-jnp.abs(z)))
    ls = jnp.minimum(z, 0.0) - t
    lk = jnp.minimum(-z, 0.0) - t
    if not masked:
        return None, ls, lk
    row = _iota(shape, 0)
    valid = (col0 + _iota(shape, 1)) < (row0 + jnp.where(row < tq, row, row - tq))
    return valid, ls, jnp.where(valid, lk, 0.0)


def _sb_where(valid, x):
    return x if valid is None else jnp.where(valid, x, 0.0)


def _sums(x, mask2, parts):
    acc = None
    rest = x
    for _ in range(parts):
        term = rest.astype(BF16)
        rest = rest - term.astype(F32)
        d = lax.dot_general(term, mask2, (((1,), (0,)), ((), ())), preferred_element_type=F32)
        acc = d if acc is None else acc + d
    return acc[:, :BLK], acc[:, BLK:]


def _mask2(cond):
    return jnp.concatenate([cond.astype(BF16), jnp.ones(cond.shape, BF16)], axis=1)


def _sb_specs(s):
    tq = _sb_tq(s)
    qspec = pl.BlockSpec((tq, LANE), lambda p, i: (i, p))
    kspec = pl.BlockSpec((s, LANE), lambda p, i: (0, SB_PAIRS + p))
    vspec = pl.BlockSpec((s, LANE), lambda p, i: (0, 2 * SB_PAIRS + p))
    return qspec, kspec, vspec


SB_FLOOR = -104.0


def sb_fwd(qkv, sides=()):
    s = qkv.shape[0]
    tq = _sb_tq(s)
    kpq = tq // BLK

    def body(q_ref, k_ref, v_ref, o_ref, t_ref, n_ref, acc_ref):
        qi = pl.program_id(1)
        q2 = _sb_stack(q_ref[...] * SB_SCALE).astype(BF16)
        later = _mask2(_iota((BLK, BLK), 0) > _iota((BLK, BLK), 1))
        acc_ref[...] = jnp.zeros_like(acc_ref)

        def step(j, r, masked):
            rows = pl.ds(pl.multiple_of(j * BLK, BLK), BLK)
            valid, ls, lk = _sb_logits(q2, k_ref[rows, :], qi * tq, j * BLK, masked)
            after, total = _sums(lk, later, 2)
            w = _sb_where(valid, jnp.exp(ls + r + after))
            acc_ref[...] += dot_nn(w, v_ref[rows, :])
            return r + total

        r = jnp.zeros((2 * tq, LANE), F32)
        for d in reversed(range(kpq)):
            r = step(kpq * qi + d, r, True)

        def tile(g, r):
            for d in reversed(range(kpq)):
                r = step(kpq * (qi - 1 - g) + d, r, False)
            return r

        n, r = lax.while_loop(lambda c: jnp.logical_and(c[0] < qi, jnp.max(c[1]) > SB_FLOOR),
                              lambda c: (c[0] + 1, tile(c[0], c[1])), (jnp.int32(0), r))
        o_ref[...] = _sb_unstack(acc_ref[...])
        t_ref[...] = jnp.concatenate([r[:tq], r[tq:]], axis=1)
        n_ref[...] = jnp.zeros(n_ref.shape, F32) + n.astype(F32)

    return hosted_call(
        body, sides, name="sb_fwd", grid=(SB_PAIRS, s // tq),
        in_specs=list(_sb_specs(s)),
        out_specs=[pl.BlockSpec((tq, LANE), lambda p, i: (i, p)), pl.BlockSpec((tq, 2 * LANE), lambda p, i: (i, p)),
                   pl.BlockSpec((None, None, 8, LANE), lambda p, i: (p, i, 0, 0))],
        out_shape=[jax.ShapeDtypeStruct((s, SB_WIDTH), F32), jax.ShapeDtypeStruct((s, 2 * SB_WIDTH), F32),
                   jax.ShapeDtypeStruct((SB_PAIRS, s // tq, 8, LANE), F32)],
        scratch_shapes=[pltpu.VMEM((2 * tq, LANE), F32)],
        compiler_params=_params(2), operands=(qkv, qkv, qkv))


def sb_bwd(qkv, tot, swept, do, do_col=0, sides=()):
    s = qkv.shape[0]
    tq = _sb_tq(s)
    kpq = tq // BLK

    def body(q_ref, k_ref, v_ref, t_ref, n_ref, do_ref, dq_ref, dk_ref, dv_ref, acc_ref):
        qi = pl.program_id(1)
        n = jnp.clip(jnp.max(n_ref[...]).astype(jnp.int32), 0, qi)
        q2 = _sb_stack(q_ref[...] * SB_SCALE).astype(BF16)
        do2 = _sb_stack(do_ref[...]).astype(BF16)
        tot2 = jnp.concatenate([t_ref[:, :LANE], t_ref[:, LANE:]], axis=0)
        sq = (BLK, BLK)
        later = _mask2(_iota(sq, 0) > _iota(sq, 1))
        before = _mask2(_iota(sq, 0) < _iota(sq, 1))
        acc_ref[...] = jnp.zeros_like(acc_ref)

        @pl.when(qi == 0)
        def _():
            dk_ref[...] = jnp.zeros_like(dk_ref)
            dv_ref[...] = jnp.zeros_like(dv_ref)

        def step(j, carry, masked):
            rc, fc = carry
            rows = pl.ds(pl.multiple_of(j * BLK, BLK), BLK)
            kj = k_ref[rows, :]
            vj = v_ref[rows, :]
            valid, ls, lk = _sb_logits(q2, kj, qi * tq, j * BLK, masked)
            after, total = _sums(lk, later, 2)
            rc = rc - total
            w = _sb_where(valid, jnp.exp(ls + rc + after))
            e = w * dot_nt(do2, vj)
            f_in, f_tot = _sums(e, before, 2)
            sg = jnp.exp(ls)
            dz = _sb_where(valid, e * (1.0 - sg) - (fc + f_in) * sg)
            acc_ref[...] += dot_nn(dz, kj)
            dk_ref[rows, :] += dot_tn(dz, q2)
            dv_ref[rows, :] += dot_tn(w, do2)
            return rc, fc + f_tot

        def tile(g, carry):
            for d in range(kpq):
                carry = step(kpq * g + d, carry, False)
            return carry

        carry = lax.fori_loop(qi - n, qi, tile, (tot2, jnp.zeros((2 * tq, LANE), F32)))
        for d in range(kpq):
            carry = step(kpq * qi + d, carry, True)
        dq_ref[...] = SB_SCALE * _sb_unstack(acc_ref[...])

    qspec, kspec, vspec = _sb_specs(s)
    blk = pl.BlockSpec((tq, LANE), lambda p, i: (i, p))
    acc = pl.BlockSpec((s, LANE), lambda p, i: (0, p))
    return hosted_call(
        body, sides, name="sb_bwd", grid=(SB_PAIRS, s // tq),
        in_specs=[qspec, kspec, vspec, pl.BlockSpec((tq, 2 * LANE), lambda p, i: (i, p)),
                  pl.BlockSpec((None, None, 8, LANE), lambda p, i: (p, i, 0, 0)),
                  pl.BlockSpec((tq, LANE), lambda p, i: (i, do_col + p))],
        out_specs=[blk, acc, acc],
        out_shape=[jax.ShapeDtypeStruct((s, SB_WIDTH), F32)] * 3,
        scratch_shapes=[pltpu.VMEM((2 * tq, LANE), F32)],
        compiler_params=_params(2), operands=(qkv, qkv, qkv, tot, swept, do))


POOL_GROUP_DIM = POOL_WIDTH // len(POOL_WINDOWS)


assert all(w == 2 ** (i + 1) for i, w in enumerate(POOL_WINDOWS))


def _pool_inv(c):
    group = _iota((BLK, POOL_WIDTH), 1) // POOL_GROUP_DIM
    pos = c * BLK + _iota((BLK, POOL_WIDTH), 0)
    win = jnp.zeros((BLK, POOL_WIDTH), jnp.int32)
    for gi, wn in enumerate(POOL_WINDOWS):
        win = jnp.where(group == gi, wn, win)
    return 1.0 / jnp.minimum(pos + 1, win).astype(F32)


def _window_sums(ext, trailing):
    group = _iota(ext.shape, 1) // POOL_GROUP_DIM
    acc = ext
    out = None
    for gi in range(len(POOL_WINDOWS)):
        shift = 2 ** gi
        acc = acc + pltpu.roll(acc, shift if trailing else ext.shape[0] - shift, 0)
        out = acc if out is None else jnp.where(group == gi, acc, out)
    return out


def _pool_pooled(ext, cur, inv):
    return _window_sums(ext, True)[BLK:] * inv - cur


def pool_fwd(p, wblk, pb, ps):
    s, n = p.shape

    def body(cur_ref, prev_ref, w_ref, pb_ref, ps_ref, o_ref):
        c = pl.program_id(0)
        cur = cur_ref[...]
        prev = jnp.where(c > 0, prev_ref[...], 0.0)
        pooled = _pool_pooled(jnp.concatenate([prev, cur], axis=0), cur, _pool_inv(c))
        o_ref[...] = (dot_nn(pooled, w_ref[...]) + pb_ref[...]) * ps_ref[...]

    return pl.pallas_call(
        body, name="pool_fwd", grid=(s // BLK,),
        in_specs=[pl.BlockSpec((BLK, n), lambda c: (c, 0)), pl.BlockSpec((BLK, n), lambda c: (jnp.maximum(c - 1, 0), 0)),
                  _full_spec((n, n)), _full_spec((1, n)), _full_spec((1, n))],
        out_specs=pl.BlockSpec((BLK, n), lambda c: (c, 0)), out_shape=jax.ShapeDtypeStruct((s, n), F32),
        compiler_params=_params(1),
    )(p, p, wblk, pb, ps)


def pool_bwd(p, wblk, pb, ps, dout, do_col=0, sides=()):
    s, n = p.shape
    nb = s // BLK

    def body(cur_ref, prev_ref, w_ref, pb_ref, ps_ref, do_ref, dp_ref, dw_ref, dpb_ref, dps_ref, carry_ref):
        i = pl.program_id(0)
        c = nb - 1 - i
        first = i == 0
        cur = cur_ref[...]
        prev = jnp.where(c > 0, prev_ref[...], 0.0)
        inv = _pool_inv(c)
        pooled = _pool_pooled(jnp.concatenate([prev, cur], axis=0), cur, inv)
        mixed = dot_nn(pooled, w_ref[...]) + pb_ref[...]
        dov = do_ref[...]
        dmixed = dov * ps_ref[...]
        _acc(dps_ref, first, jnp.sum(dov * mixed, axis=0, keepdims=True))
        _acc(dpb_ref, first, jnp.sum(dmixed, axis=0, keepdims=True))
        _acc(dw_ref, first, dot_tn(pooled, dmixed))
        dpooled = dot_nt(dmixed, w_ref[...])
        dext = _window_sums(jnp.concatenate([jnp.zeros((BLK, n), F32), dpooled * inv], axis=0), False)

        @pl.when(first)
        def _():
            carry_ref[...] = jnp.zeros_like(carry_ref)

        dp_ref[...] = dext[BLK:] - dpooled + carry_ref[...]
        carry_ref[...] = dext[:BLK]

    rev = lambda i: (nb - 1 - i, 0)
    return hosted_call(
        body, sides, name="pool_bwd", grid=(nb,),
        in_specs=[pl.BlockSpec((BLK, n), rev), pl.BlockSpec((BLK, n), lambda i: (jnp.maximum(nb - 2 - i, 0), 0)),
                  _full_spec((n, n)), _full_spec((1, n)), _full_spec((1, n)),
                  pl.BlockSpec((BLK, n), lambda i: (nb - 1 - i, do_col))],
        out_specs=[pl.BlockSpec((BLK, n), rev), _full_spec((n, n)), _full_spec((1, n)), _full_spec((1, n))],
        out_shape=[jax.ShapeDtypeStruct((s, n), F32), jax.ShapeDtypeStruct((n, n), F32),
                   jax.ShapeDtypeStruct((1, n), F32), jax.ShapeDtypeStruct((1, n), F32)],
        scratch_shapes=[pltpu.VMEM((BLK, n), F32)],
        compiler_params=_params(1), operands=(p, p, wblk, pb, ps, dout))


def _row_tile(rows):
    if rows <= 512:
        return rows
    for t in (512, 256, 128, 64, 32, 16, 8):
        if rows % t == 0:
            return t
    return rows


def adamw(w, g, m, v):
    n, rows, cols = w.shape
    tr = _row_tile(rows)

    def body(w_ref, g_ref, m_ref, v_ref, d_ref, nm_ref, nv_ref):
        d_ref[...], nm_ref[...], nv_ref[...] = _adamw_math(w_ref[...], g_ref[...], m_ref[...], v_ref[...])

    spec = pl.BlockSpec((1, tr, cols), lambda i, j: (i, j, 0))
    return pl.pallas_call(
        body, name="adamw", grid=(n, rows // tr), in_specs=[spec] * 4, out_specs=[spec] * 3,
        out_shape=[jax.ShapeDtypeStruct(w.shape, F32)] * 3, compiler_params=_params(2),
    )(w, g, m, v)


def _adamw_math(w, g, m, v):
    nm = ADAM_B1 * m + (1.0 - ADAM_B1) * g
    nv = ADAM_B2 * v + (1.0 - ADAM_B2) * (g * g)
    m_hat = nm / (1.0 - ADAM_B1 ** ADAM_STEP)
    v_hat = nv / (1.0 - ADAM_B2 ** ADAM_STEP)
    return -ADAM_LR * (m_hat / (jnp.sqrt(v_hat) + ADAM_EPS) + ADAM_WD * w), nm, nv


def adamw_small(ws, gs, ms, vs):
    n = len(ws)

    def body(*refs):
        for i in range(n):
            outs = _adamw_math(*[refs[k * n + i][...] for k in range(4)])
            for k in range(3):
                refs[(4 + k) * n + i][...] = outs[k]

    vm = pl.BlockSpec(memory_space=pltpu.VMEM)
    outs = pl.pallas_call(
        body, name="adamw_small", in_specs=[vm] * (4 * n), out_specs=[vm] * (3 * n),
        out_shape=[jax.ShapeDtypeStruct(w.shape, F32) for w in ws] * 3,
    )(*ws, *gs, *ms, *vs)
    return outs[:n], outs[n:2 * n], outs[2 * n:]


def slab_sum(srcs, n_out, out_dtype, into=None, slot=0):
    _, rows, cols = srcs[0][0].shape
    tr = _row_tile(rows)
    n_src = len(srcs)
    sel = jnp.stack([jnp.asarray(base, jnp.int32) for _, base, _ in srcs])

    def body(sel_ref, *refs):
        acc = refs[0][...].astype(F32)
        for r in refs[1:n_src]:
            acc = acc + r[...].astype(F32)
        refs[-1][...] = acc.astype(out_dtype)

    def in_spec(k, step):
        return pl.BlockSpec((None, tr, cols), lambda o, i, sel_ref: (sel_ref[k] + step * o, i, 0))

    shape = (n_out, rows, cols) if into is None else into.shape
    return pl.pallas_call(
        body, name="slab_sum",
        grid_spec=pltpu.PrefetchScalarGridSpec(
            num_scalar_prefetch=1, grid=(n_out, rows // tr),
            in_specs=[in_spec(k, step) for k, (_, _, step) in enumerate(srcs)] + ([] if into is None else [ANY]),
            out_specs=pl.BlockSpec((None, tr, cols), lambda o, i, sel_ref: (slot + o, i, 0))),
        out_shape=jax.ShapeDtypeStruct(shape, out_dtype), compiler_params=_params(2),
        input_output_aliases={} if into is None else {1 + n_src: 0},
    )(sel, *[a for a, _, _ in srcs], *([] if into is None else [into]))


ICI_FLIPS = ((1, 0, 0), (0, 1, 0), (1, 1, 0))
D2D_FLIPS = ((0, 0, 1),)
ANY = pl.BlockSpec(memory_space=pl.ANY)


def _me():
    return lax.axis_index("x"), lax.axis_index("y"), lax.axis_index("c")


def _flipped(me, flip):
    return tuple(1 - m if f else m for m, f in zip(me, flip))


def _chip(dev):
    return 2 * dev[0] + dev[1]


def _dev(dev):
    return 4 * dev[0] + 2 * dev[1] + dev[2]


N_CHIP = 4
D2D = (0, 0, 1)


class Exchange:
    def __init__(self, xs, n_out, copies, own=None, in_place=False):
        self.xs, self.copies, self.own, self.in_place = list(xs), copies, own, in_place
        self.n_arr, self.n_cp = len(self.xs), len(copies)
        self.out_shape = [jax.ShapeDtypeStruct((n_out,) + x.shape[1:], x.dtype) for x in self.xs]
        self.scratch = [pltpu.SemaphoreType.DMA((self.n_arr * self.n_cp,)),
                        pltpu.SemaphoreType.DMA((self.n_arr * self.n_cp,)), pltpu.SemaphoreType.DMA((self.n_arr,))]

    def _own(self, x_refs, o_refs, sems, me):
        if self.own is None:
            return []
        return [pltpu.make_async_copy(x_refs[a].at[self.own[0](me)], o_refs[a].at[self.own[1](me)], sems[2].at[a])
                for a in range(self.n_arr)]

    def _copy(self, x_refs, o_refs, sems, me, a, j, sender):
        flip, src_slot, dst_slot = self.copies[j]
        k = a * self.n_cp + j
        return pltpu.make_async_remote_copy(
            src_ref=x_refs[a].at[src_slot(me)], dst_ref=o_refs[a].at[dst_slot(sender)],
            send_sem=sems[0].at[k], recv_sem=sems[1].at[k], device_id=_flipped(me, flip), device_id_type=MESH)

    def start(self, x_refs, o_refs, sems):
        me = _me()
        for cp in self._own(x_refs, o_refs, sems, me):
            cp.start()
        for j in range(self.n_cp):
            for a in range(self.n_arr):
                self._copy(x_refs, o_refs, sems, me, a, j, me).start()

    def wait(self, x_refs, o_refs, sems):
        me = _me()
        for j in range(self.n_cp):
            for a in range(self.n_arr):
                self._copy(x_refs, o_refs, sems, me, a, j, _flipped(me, self.copies[j][0])).wait_recv()
        for j in range(self.n_cp):
            for a in range(self.n_arr):
                self._copy(x_refs, o_refs, sems, me, a, j, me).wait_send()
        for cp in self._own(x_refs, o_refs, sems, me):
            cp.wait()

    def run(self, name):
        n = self.n_arr

        def body(*refs):
            self.start(refs[:n], refs[n:2 * n], refs[2 * n:])
            self.wait(refs[:n], refs[n:2 * n], refs[2 * n:])

        return pl.pallas_call(
            body, name=name, in_specs=[ANY] * n, out_specs=[ANY] * n, out_shape=self.out_shape,
            input_output_aliases={a: a for a in range(n)} if self.in_place else {}, scratch_shapes=self.scratch,
        )(*self.xs)


def hosted_call(body, sides, *, name, grid, in_specs, out_specs, out_shape, scratch_shapes, compiler_params, operands):
    n_in, n_out, n_scr = len(in_specs), len(out_specs), len(scratch_shapes)
    live = [s for s in sides if s is not None]
    if not live:
        outs = pl.pallas_call(body, name=name, grid=grid, in_specs=in_specs, out_specs=out_specs, out_shape=out_shape,
                              scratch_shapes=scratch_shapes, compiler_params=compiler_params)(*operands)
        return outs, [None] * len(sides)
    n = sum(s.n_arr for s in live)
    lo = [sum(s.n_arr for s in live[:i]) for i in range(len(live))]

    def full_body(*refs):
        ins, sx = refs[:n_in], refs[n_in:n_in + n]
        outs, so = refs[n_in + n:n_in + n + n_out], refs[n_in + n + n_out:n_in + 2 * n + n_out]
        scr, sems = refs[n_in + 2 * n + n_out:n_in + 2 * n + n_out + n_scr], refs[n_in + 2 * n + n_out + n_scr:]
        first = functools.reduce(jnp.logical_and, [pl.program_id(a) == 0 for a in range(len(grid))])
        last = functools.reduce(jnp.logical_and, [pl.program_id(a) == g - 1 for a, g in enumerate(grid)])
        parts = [(s, sx[l:l + s.n_arr], so[l:l + s.n_arr], sems[3 * i:3 * i + 3]) for i, (s, l) in enumerate(zip(live, lo))]

        @pl.when(first)
        def _():
            for s, x, o, m in parts:
                s.start(x, o, m)

        body(*ins, *outs, *scr)

        @pl.when(last)
        def _():
            for s, x, o, m in parts:
                s.wait(x, o, m)

    aliases = {n_in + l + a: n_out + l + a for s, l in zip(live, lo) if s.in_place for a in range(s.n_arr)}
    outs = pl.pallas_call(
        full_body, name=name + "_x", grid=grid, in_specs=list(in_specs) + [ANY] * n,
        out_specs=list(out_specs) + [ANY] * n, out_shape=list(out_shape) + [o for s in live for o in s.out_shape],
        input_output_aliases=aliases,
        scratch_shapes=list(scratch_shapes) + [m for s in live for m in s.scratch], compiler_params=compiler_params,
    )(*operands, *[x for s in live for x in s.xs])
    side_outs = iter([outs[n_out + l:n_out + l + s.n_arr] for s, l in zip(live, lo)])
    return outs[:n_out], [next(side_outs) if s is not None else None for s in sides]


def gather_ici(shards):
    ici = [(f, lambda me: 0, _dev) for f in ICI_FLIPS]
    return Exchange([s[None] for s in shards], N_DEV, ici, (lambda me: 0, _dev))


def gather_d2d(blocks):
    d2d = [(D2D, (lambda me, k=k: 2 * k + me[2]), (lambda sender, k=k: 2 * k + sender[2])) for k in range(N_CHIP)]
    return Exchange(blocks, N_DEV, d2d, None, in_place=True)


def gathered(blocks):
    return [b.reshape(-1, b.shape[2]) for b in blocks]


def scatter_d2d(parts):
    d2d = [(D2D, (lambda me, k=k: 2 * k + 1 - me[2]), (lambda sender, k=k: k)) for k in range(N_CHIP)]
    return Exchange(parts, N_CHIP, d2d)


def chip_sums(parts, sib):
    c = _me()[2]
    return [slab_sum([(p, c, 2), (s, 0, 1)], N_CHIP, BF16) for p, s in zip(parts, sib)]


def scatter_ici(sums):
    ici = [(f, (lambda me, f=f: _chip(_flipped(me, f))), (lambda sender, i=i: i)) for i, f in enumerate(ICI_FLIPS)]
    return Exchange(sums, len(ICI_FLIPS), ici)


def device_sums(sums, got, into=None, slot=0):
    x, y, _ = _me()
    outs = [slab_sum([(cs, 2 * x + y, 0)] + [(g, i, 0) for i in range(len(ICI_FLIPS))], 1, F32,
                     None if into is None else into[a], slot) for a, (cs, g) in enumerate(zip(sums, got))]
    return outs if into is not None else [o[0] for o in outs]


def all_gather(shards):
    blocks = gather_ici(shards).run("gather_ici")
    return gathered(gather_d2d(blocks).run("gather_d2d"))


def reduce_scatter(parts, *into):
    sums = chip_sums(parts, scatter_d2d(parts).run("scatter_d2d"))
    return device_sums(sums, scatter_ici(sums).run("scatter_ici"), *into)


def all_reduce_small(v):
    flips = D2D_FLIPS + ICI_FLIPS[:2]

    def body(v_ref, o_ref, got_ref, send_sems, recv_sems):
        me = _me()
        o_ref[...] = v_ref[...]
        for i, flip in enumerate(flips):
            cp = pltpu.make_async_remote_copy(
                src_ref=o_ref, dst_ref=got_ref.at[i], send_sem=send_sems.at[i], recv_sem=recv_sems.at[i],
                device_id=_flipped(me, flip), device_id_type=MESH)
            cp.start()
            cp.wait()
            o_ref[...] = o_ref[...] + got_ref[i]

    vm = pl.BlockSpec(memory_space=pltpu.VMEM)
    return pl.pallas_call(
        body, name="all_reduce_small", in_specs=[vm], out_specs=vm, out_shape=jax.ShapeDtypeStruct(v.shape, F32),
        scratch_shapes=[pltpu.VMEM((len(flips),) + v.shape, F32), pltpu.SemaphoreType.DMA((len(flips),)),
                        pltpu.SemaphoreType.DMA((len(flips),))],
    )(v)


def _perm_rows(wt):
    pad = jnp.zeros((D_IN_PAD - D_IN_PROJ, wt.shape[1]), wt.dtype)
    return jnp.concatenate([wt[:DT_LO], wt[DT_HI:], wt[DT_LO:DT_HI], pad], axis=0)


def _unperm_rows(dwt):
    n = D_IN_PROJ - (DT_HI - DT_LO)
    return jnp.concatenate([dwt[:DT_LO], dwt[n:D_IN_PROJ], dwt[DT_LO:n]], axis=0)


def _pad_lanes(v):
    return jnp.pad(v, ((0, 0), (0, LANE - v.shape[1])))[:, None]


def _block_diag(w):
    l, g, n, _ = w.shape
    out = jnp.zeros((l, g * n, g * n), w.dtype)
    for i in range(g):
        out = out.at[:, i * n:(i + 1) * n, i * n:(i + 1) * n].set(w[:, i])
    return out


def _pack(groups):
    flat = []
    for grp in groups:
        parts = [a.reshape(-1) for a in (grp if isinstance(grp, (list, tuple)) else [grp])]
        n = sum(p.shape[0] for p in parts)
        if -n % LANE:
            parts.append(jnp.zeros((-n % LANE,), parts[0].dtype))
        flat += parts
    return jnp.concatenate(flat).reshape(-1, LANE)


def _unpack(buf, shapes):
    out = []
    lo = 0
    buf = buf.reshape(-1)
    for shp in shapes:
        n = 1
        for k in shp:
            n *= k
        out.append(buf[lo:lo + n].reshape(shp))
        lo += n + (-n % LANE)
    return out


def small_params(w, conv_w_full):
    return dict(
        n1w=w["norm1_w"][:, None], cw=jnp.pad(conv_w_full, ((0, 0), (0, 8 - CONV_WIDTH), (0, 0))),
        cb=w["conv_b"][:, None], dtb=_pad_lanes(w["dt_bias"]), alog=_pad_lanes(w["a_log"]), dsk=_pad_lanes(w["d_skip"]),
        snw=w["ssd_norm_w"][:, None], wblk=_block_diag(w["pool_w"]), pb=w["pool_b"].reshape(-1, 1, POOL_WIDTH),
        ps=w["pool_scale"][:, None], n2w=w["norm2_w"][:, None])


MIX = ("w_in", "w_out")
FFN = ("w_gate", "w_up", "w_down")


def layer_params(small, l):
    return {k: v[l] for k, v in small.items()}


def mix_weights(whole):
    return _perm_rows(whole[0]), whole[1]


def _slabs(g):
    return g.reshape(N_DEV, -1, g.shape[-1])


def _layer_fwd(x, p, mix, ffn=None, ffn_shards=None, next_mix_shards=None):
    ici = [gather_ici([sh]) for sh in ffn_shards] if ffn_shards is not None else [None] * len(FFN)
    z, xbc, qkv, pp, dtr, h1 = inproj_fwd(x, p["n1w"], mix[0])
    (u,), (blk_g,) = conv_fwd(xbc, p["cw"], p["cb"], [ici[0]])
    (y_ssd, st), (blk_u,) = ssd_fwd(u, z, dtr, p["dtb"], p["alog"], p["dsk"], p["snw"], [ici[1]])
    (o, tot, swept), (blk_d,) = sb_fwd(qkv, [ici[2]])
    blocks = blk_g + blk_u + blk_d if ffn_shards is not None else None
    yp = pool_fwd(pp, p["wblk"], p["pb"], p["ps"])
    (x_mid, ycat), (blocks,) = outproj_fwd([y_ssd, o, yp], mix[1], x,
                                           [gather_d2d(blocks) if blocks is not None else None])
    if blocks is not None:
        ffn = gathered(blocks)
    (x_out, g, uu), (nxt,) = ffn_fwd(x_mid, p["n2w"], *ffn,
                                     [gather_ici(next_mix_shards) if next_mix_shards is not None else None])
    sv = dict(x=x, z=z, xbc=xbc, qkv=qkv, pp=pp, dtr=dtr, h1=h1, u=u, st=st, tot=tot, swept=swept, ycat=ycat,
              x_mid=x_mid, g=g, uu=uu, w_in=mix[0], w_out=mix[1], wg=ffn[0], wu=ffn[1], wd=ffn[2])
    return x_out, sv, nxt


def _layer_bwd(dxo, sv, p, pending_mix=None, exchange=False, into_ffn=(), into_mix=()):
    (dx_mid, dn2w, a, dg, du, h2), (sib,) = ffn_bwd(
        dxo, sv["x_mid"], sv["g"], sv["uu"], p["n2w"], sv["wg"], sv["wu"], sv["wd"],
        [scatter_d2d(pending_mix) if pending_mix is not None else None])
    sums_mix = chip_sums(pending_mix, sib) if pending_mix is not None else None
    gr = dict(norm2_w=dn2w[0], w_gate=mm_tn(dg, h2), w_up=mm_tn(du, h2), w_down=mm_tn(a, dxo))
    parts = [_slabs(gr[k]) for k in FFN] if exchange else None
    (dycat,), (sib,) = outproj_bwd(dx_mid, sv["w_out"], [scatter_d2d(parts) if exchange else None])
    sums_ffn = chip_sums(parts, sib) if exchange else None
    gr["w_out"] = mm_tn(sv["ycat"], dx_mid)
    (dp, dwblk, dpb, dps), (got_g,) = pool_bwd(sv["pp"], p["wblk"], p["pb"], p["ps"], dycat,
                                               (SSD_WIDTH + SB_WIDTH) // POOL_WIDTH,
                                               [scatter_ici(sums_ffn[:1]) if exchange else None])
    n = POOL_GROUP_DIM
    gr["pool_w"] = jnp.stack([dwblk[i * n:(i + 1) * n, i * n:(i + 1) * n] for i in range(len(POOL_WINDOWS))])
    gr["pool_b"] = dpb.reshape(len(POOL_WINDOWS), n)
    gr["pool_scale"] = dps[0]
    (dq, dk, dv), (got_ud,) = sb_bwd(sv["qkv"], sv["tot"], sv["swept"], dycat, SSD_WIDTH // LANE,
                                     [scatter_ici(sums_ffn[1:]) if exchange else None])
    done_ffn = device_sums(sums_ffn, got_g + got_ud, *into_ffn) if exchange else None
    (du_, dz, ddtr, ddtb, dalog, ddsk, dsnw), (got_mix,) = ssd_bwd(
        sv["u"], sv["z"], sv["dtr"], sv["st"], dycat, p["dtb"], p["alog"], p["dsk"], p["snw"],
        [scatter_ici(sums_mix) if sums_mix is not None else None])
    done_mix = device_sums(sums_mix, got_mix, *into_mix) if sums_mix is not None else None
    gr.update(dt_bias=ddtb[0, :SSD_HEADS], a_log=dalog[0, :SSD_HEADS], d_skip=ddsk[0, :SSD_HEADS], ssd_norm_w=dsnw[0])
    dxbc, dcw, dcb = conv_bwd(sv["xbc"], p["cw"], p["cb"], du_)
    gr.update(conv_w=dcw[:CONV_WIDTH], conv_b=dcb[0])
    dx, dn1w, dproj = inproj_bwd([dz, dxbc, dq, dk, dv, dp, ddtr], sv["w_in"], sv["x"], p["n1w"], dx_mid)
    gr.update(norm1_w=dn1w[0], w_in=_unperm_rows(mm_tn(dproj, sv["h1"])))
    return dx, gr, done_ffn, done_mix


def local_step(x, tgt, params, weights, final_w):
    saved = []
    for p, (mix, ffn) in zip(params, weights):
        x, sv, _ = _layer_fwd(x, p, mix, ffn)
        saved.append(sv)
    loss, dx, dfw = head_loss(x, final_w[None], tgt)
    grads = []
    for p, sv in zip(reversed(params), reversed(saved)):
        dx, gr, _, _ = _layer_bwd(dx, sv, p)
        grads.append(gr)
    grads.reverse()
    return loss, dx, dfw[0], grads


WEIGHTS = ("norm1_w", "w_in", "conv_w", "conv_b", "dt_bias", "a_log", "d_skip", "ssd_norm_w", "pool_w", "pool_b",
           "pool_scale", "w_out", "norm2_w", "w_gate", "w_up", "w_down", "final_norm_w")
COL_SHARDED = ("w_in", "w_gate", "w_up")
ROW_SHARDED = ("w_out", "w_down")
SMALL = tuple(k for k in WEIGHTS if k not in COL_SHARDED + ROW_SHARDED)


def kernel(x, norm1_w, w_in, conv_w, conv_b, dt_bias, a_log, d_skip, ssd_norm_w, pool_w, pool_b, pool_scale, w_out, norm2_w, w_gate, w_up, w_down, final_norm_w, loss_target, m_norm1_w, m_w_in, m_conv_w, m_conv_b, m_dt_bias, m_a_log, m_d_skip, m_ssd_norm_w, m_pool_w, m_pool_b, m_pool_scale, m_w_out, m_norm2_w, m_w_gate, m_w_up, m_w_down, m_final_norm_w, v_norm1_w, v_w_in, v_conv_w, v_conv_b, v_dt_bias, v_a_log, v_d_skip, v_ssd_norm_w, v_pool_w, v_pool_b, v_pool_scale, v_w_out, v_norm2_w, v_w_gate, v_w_up, v_w_down, v_final_norm_w):
    args = dict(locals())
    w = {k: args[k] for k in WEIGHTS}
    m = {k: args["m_" + k] for k in WEIGHTS}
    v = {k: args["v_" + k] for k in WEIGHTS}
    depth = w_in.shape[0]
    dev = _dev(_me())
    n_cw = conv_w.shape[-1]

    shards = {k: (jnp.swapaxes(w[k], 1, 2) if k in COL_SHARDED else w[k]).astype(BF16) for k in MIX + FFN}
    whole = all_gather([jnp.swapaxes(conv_w, 0, 2).reshape(n_cw, -1)] + [shards[k][0] for k in MIX])
    conv_w_full = jnp.swapaxes(whole[0].reshape(N_DEV * n_cw, CONV_WIDTH, depth), 0, 2)
    small = small_params(w, conv_w_full)
    xs = x[0]
    params, saved = [layer_params(small, l) for l in range(depth)], []
    mix = mix_weights(whole[1:])
    for l in range(depth):
        xs, sv, nxt = _layer_fwd(xs, params[l], mix, ffn_shards=[shards[k][l] for k in FFN],
                                 next_mix_shards=[shards[k][l + 1] for k in MIX] if l + 1 < depth else None)
        saved.append(sv)
        if nxt is not None:
            mix = mix_weights(gathered(gather_d2d(nxt).run("gather_d2d")))
    loss, dx, dfw = head_loss(xs, final_norm_w[None], loss_target[0])
    layer_grads = [None] * depth
    native = {k: lax.empty((depth,) + shards[k].shape[1:], F32) for k in MIX + FFN}
    pending = None
    for l in reversed(range(depth)):
        dx, layer_grads[l], done_ffn, done_mix = _layer_bwd(
            dx, saved[l], params[l], pending, exchange=True, into_ffn=([native[k] for k in FFN], l),
            into_mix=([native[k] for k in MIX], l + 1))
        native.update(zip(FFN, done_ffn))
        if pending is not None:
            native.update(zip(MIX, done_mix))
        pending = [_slabs(layer_grads[l][k]) for k in MIX]
    native.update(zip(MIX, reduce_scatter(pending, [native[k] for k in MIX], 0)))
    grads = {k: jnp.swapaxes(native[k], 1, 2) if k in COL_SHARDED else native[k] for k in MIX + FFN}
    layered = [k for k in SMALL if k != "final_norm_w"]
    small_shapes = [(1, LANE)] + [(depth,) + layer_grads[0][k].shape for k in layered] + [dfw[0].shape]
    packed = _pack([loss] + [[layer_grads[l][k] for l in range(depth)] for k in layered] + [dfw[0]])
    summed = _unpack(all_reduce_small(packed), small_shapes)
    loss = summed[0][0, 0]
    grads.update(zip(layered + ["final_norm_w"], summed[1:]))
    grads["conv_w"] = lax.dynamic_slice_in_dim(grads["conv_w"], dev * n_cw, n_cw, axis=2)

    delta, new_m, new_v = {}, {}, {}
    for k in MIX + FFN:
        if k in ("w_gate", "w_up"):
            wt, mt, vt = (jnp.swapaxes(t, 1, 2) for t in (w[k], m[k], v[k]))
            delta[k], new_m[k], new_v[k] = [jnp.swapaxes(o, 1, 2) for o in adamw(wt, native[k], mt, vt)]
        else:
            delta[k], new_m[k], new_v[k] = adamw(w[k], grads[k], m[k], v[k])
    two_d = lambda a: a.reshape(1, -1) if a.ndim == 1 else a
    outs = adamw_small(*[[two_d(t[k]) for k in SMALL] for t in (w, grads, m, v)])
    for dst, arrs in zip((delta, new_m, new_v), outs):
        dst.update({k: a.reshape(w[k].shape) for k, a in zip(SMALL, arrs)})
    return (loss, dx[None], *[grads[k] for k in WEIGHTS], *[delta[k] for k in WEIGHTS],
            *[new_m[k] for k in WEIGHTS], *[new_v[k] for k in WEIGHTS])
```

```python
import functools

import jax
import jax.numpy as jnp
from jax import lax
from jax.experimental import pallas as pl
from jax.experimental.pallas import tpu as pltpu

F32 = jnp.float32
BF16 = jnp.bfloat16
HIGHEST = lax.Precision.HIGHEST
MESH = pl.DeviceIdType.MESH

EPS = 1e-6
D_MODEL = 1024
SSD_WIDTH = 512
SSD_HEADS = 8
HEAD_DIM = 64
D_STATE = 128
CONV_WIDTH = 4
CONV_DIM = 1024
SB_WIDTH = 256
POOL_WIDTH = 256
POOL_WINDOWS = (2, 4, 8, 16)
D_IN_PROJ = 2568
D_FF = 2816
N_DEV = 8
DEPTH = 4
SEG = (512, 1024, 768, 256, 128)
D_IN_PAD = sum(SEG)
DT_LO, DT_HI = 1536, 1544

LANE = 128
BLK = 128
ROW_TILE = 256
VMEM_LIMIT = 56 * 2**20

ADAM_LR, ADAM_B1, ADAM_B2, ADAM_EPS, ADAM_WD, ADAM_STEP = 0.001, 0.9, 0.999, 1e-08, 0.01, 10


def _params(n_axes=1, vmem=None):
    return pltpu.CompilerParams(dimension_semantics=("arbitrary",) * n_axes, vmem_limit_bytes=vmem)


def _dot(a, b, dims, exact=False):
    if exact:
        return lax.dot_general(a.astype(F32), b.astype(F32), (dims, ((), ())), precision=HIGHEST,
                               preferred_element_type=F32)
    return lax.dot_general(a.astype(BF16), b.astype(BF16), (dims, ((), ())), preferred_element_type=F32)


def dot_nn(a, b, exact=False):
    return _dot(a, b, ((1,), (0,)), exact)


def dot_nt(a, b, exact=False):
    return _dot(a, b, ((1,), (1,)), exact)


def dot_tn(a, b, exact=False):
    return _dot(a, b, ((0,), (0,)), exact)


def _iota(shape, axis):
    return lax.broadcasted_iota(jnp.int32, shape, axis)


def _lane_col(x, h):
    return jnp.sum(jnp.where(_iota(x.shape, 1) == h, x, 0.0), axis=1, keepdims=True)


def _sub_row(x, h):
    return jnp.sum(jnp.where(_iota(x.shape, 0) == h, x, 0.0), axis=0, keepdims=True)


def _sigmoid(x):
    return 1.0 / (1.0 + jnp.exp(-x))


def _rms_fwd(x, w):
    r = lax.rsqrt(jnp.mean(x * x, axis=-1, keepdims=True) + EPS)
    return x * r * w


def _rms_bwd(x, w, dy):
    r = lax.rsqrt(jnp.mean(x * x, axis=-1, keepdims=True) + EPS)
    xh = x * r
    dxh = dy * w
    dx = r * (dxh - xh * jnp.mean(dxh * xh, axis=-1, keepdims=True))
    return dx, jnp.sum(dy * xh, axis=0, keepdims=True)


def _acc(ref, first, val):
    @pl.when(first)
    def _():
        ref[...] = val

    @pl.when(jnp.logical_not(first))
    def _():
        ref[...] += val


def _row_spec(tm, n):
    return pl.BlockSpec((tm, n), lambda i: (i, 0))


def _full_spec(shape):
    return pl.BlockSpec(shape, lambda *_: (0,) * len(shape))


def inproj_fwd(x, nw, w, sides=()):
    s, d = x.shape
    tm = min(ROW_TILE, s)

    def body(x_ref, nw_ref, w_ref, z_ref, xbc_ref, qkv_ref, p_ref, dt_ref, h_ref):
        h = _rms_fwd(x_ref[...], nw_ref[...]).astype(BF16)
        h_ref[...] = h
        lo = 0
        for ref, n in zip((z_ref, xbc_ref, qkv_ref, p_ref, dt_ref), SEG):
            ref[...] = dot_nt(h, w_ref[lo:lo + n, :])
            lo += n

    return hosted_call(
        body, sides, name="inproj_fwd", grid=(s // tm,),
        in_specs=[_row_spec(tm, d), _full_spec((1, d)), _full_spec(w.shape)],
        out_specs=[_row_spec(tm, n) for n in SEG] + [_row_spec(tm, d)],
        out_shape=[jax.ShapeDtypeStruct((s, n), F32) for n in SEG] + [jax.ShapeDtypeStruct((s, d), BF16)],
        scratch_shapes=[], compiler_params=_params(1, VMEM_LIMIT), operands=(x, nw, w))


def inproj_bwd(pieces, w, x, nw, dres):
    s, d = x.shape
    tm = min(ROW_TILE, s)
    n_p = len(pieces)
    widths = [p.shape[1] for p in pieces]

    def body(*refs):
        w_ref, x_ref, nw_ref, dres_ref, dx_ref, dnw_ref, dp_ref = refs[n_p:]
        dh = None
        lo = 0
        for ref, n in zip(refs[:n_p], widths):
            piece = ref[...].astype(BF16)
            dp_ref[:, lo:lo + n] = piece
            part = dot_nn(piece, w_ref[lo:lo + n, :])
            dh = part if dh is None else dh + part
            lo += n
        dx, dnw = _rms_bwd(x_ref[...], nw_ref[...], dh)
        dx_ref[...] = dres_ref[...] + dx
        _acc(dnw_ref, pl.program_id(0) == 0, dnw)

    return pl.pallas_call(
        body, name="inproj_bwd", grid=(s // tm,),
        in_specs=[_row_spec(tm, n) for n in widths] + [_full_spec(w.shape), _row_spec(tm, d), _full_spec((1, d)),
                                                       _row_spec(tm, d)],
        out_specs=[_row_spec(tm, d), _full_spec((1, d)), _row_spec(tm, sum(widths))],
        out_shape=[jax.ShapeDtypeStruct((s, d), F32), jax.ShapeDtypeStruct((1, d), F32),
                   jax.ShapeDtypeStruct((s, sum(widths)), BF16)],
        compiler_params=_params(1, VMEM_LIMIT),
    )(*pieces, w, x, nw, dres)


def outproj_fwd(pieces, w, res, sides=()):
    s, d = res.shape
    tm = min(ROW_TILE, s)
    n_p = len(pieces)
    widths = [p.shape[1] for p in pieces]

    def body(*refs):
        w_ref, r_ref, o_ref, y_ref = refs[n_p:]
        acc = r_ref[...]
        lo = 0
        for ref, n in zip(refs[:n_p], widths):
            piece = ref[...].astype(BF16)
            y_ref[:, lo:lo + n] = piece
            acc = acc + dot_nn(piece, w_ref[lo:lo + n, :])
            lo += n
        o_ref[...] = acc

    return hosted_call(
        body, sides, name="outproj_fwd", grid=(s // tm,),
        in_specs=[_row_spec(tm, n) for n in widths] + [_full_spec(w.shape), _row_spec(tm, d)],
        out_specs=[_row_spec(tm, d), _row_spec(tm, sum(widths))],
        out_shape=[jax.ShapeDtypeStruct((s, d), F32), jax.ShapeDtypeStruct((s, sum(widths)), BF16)],
        scratch_shapes=[], compiler_params=_params(1, VMEM_LIMIT), operands=(*pieces, w, res))


def outproj_bwd(dx, w, sides=()):
    s, d = dx.shape
    tm = min(ROW_TILE, s)

    def body(dx_ref, w_ref, o_ref):
        o_ref[...] = dot_nt(dx_ref[...], w_ref[...])

    return hosted_call(
        body, sides, name="outproj_bwd", grid=(s // tm,),
        in_specs=[_row_spec(tm, d), _full_spec(w.shape)],
        out_specs=[_row_spec(tm, w.shape[0])], out_shape=[jax.ShapeDtypeStruct((s, w.shape[0]), F32)],
        scratch_shapes=[], compiler_params=_params(1, VMEM_LIMIT), operands=(dx, w))


def ffn_fwd(x, nw, wg, wu, wd, sides=()):
    s, d = x.shape
    f = wg.shape[0]
    tm = min(ROW_TILE, s)

    def body(x_ref, nw_ref, wg_ref, wu_ref, wd_ref, o_ref, g_ref, u_ref):
        xv = x_ref[...]
        h = _rms_fwd(xv, nw_ref[...]).astype(BF16)
        g = dot_nt(h, wg_ref[...])
        u = dot_nt(h, wu_ref[...])
        g_ref[...] = g.astype(BF16)
        u_ref[...] = u.astype(BF16)
        o_ref[...] = xv + dot_nn(g * _sigmoid(g) * u, wd_ref[...])

    return hosted_call(
        body, sides, name="ffn_fwd", grid=(s // tm,),
        in_specs=[_row_spec(tm, d), _full_spec((1, d)), _full_spec(wg.shape), _full_spec(wu.shape),
                  _full_spec(wd.shape)],
        out_specs=[_row_spec(tm, d), _row_spec(tm, f), _row_spec(tm, f)],
        out_shape=[jax.ShapeDtypeStruct((s, d), F32), jax.ShapeDtypeStruct((s, f), BF16),
                   jax.ShapeDtypeStruct((s, f), BF16)],
        scratch_shapes=[], compiler_params=_params(1, VMEM_LIMIT), operands=(x, nw, wg, wu, wd))


def ffn_bwd(dxo, x, g, u, nw, wg, wu, wd, sides=()):
    s, d = x.shape
    f = wg.shape[0]
    tm = min(ROW_TILE, s)

    def body(dxo_ref, x_ref, g_ref, u_ref, nw_ref, wg_ref, wu_ref, wd_ref, dx_ref, dnw_ref, a_ref, dg_ref,
             du_ref, h_ref):
        dxo_v = dxo_ref[...]
        xv = x_ref[...]
        da = dot_nt(dxo_v, wd_ref[...])
        gv = g_ref[...].astype(F32)
        uv = u_ref[...].astype(F32)
        sg = _sigmoid(gv)
        sl = gv * sg
        a_ref[...] = (sl * uv).astype(BF16)
        dg = (da * uv * (sg * (1.0 + gv * (1.0 - sg)))).astype(BF16)
        du = (da * sl).astype(BF16)
        dg_ref[...] = dg
        du_ref[...] = du
        dh = dot_nn(dg, wg_ref[...]) + dot_nn(du, wu_ref[...])
        h_ref[...] = _rms_fwd(xv, nw_ref[...]).astype(BF16)
        dx, dnw = _rms_bwd(xv, nw_ref[...], dh)
        dx_ref[...] = dxo_v + dx
        _acc(dnw_ref, pl.program_id(0) == 0, dnw)

    return hosted_call(
        body, sides, name="ffn_bwd", grid=(s // tm,),
        in_specs=[_row_spec(tm, d), _row_spec(tm, d), _row_spec(tm, f), _row_spec(tm, f), _full_spec((1, d)),
                  _full_spec(wg.shape), _full_spec(wu.shape), _full_spec(wd.shape)],
        out_specs=[_row_spec(tm, d), _full_spec((1, d)), _row_spec(tm, f), _row_spec(tm, f), _row_spec(tm, f),
                   _row_spec(tm, d)],
        out_shape=[jax.ShapeDtypeStruct((s, d), F32), jax.ShapeDtypeStruct((1, d), F32),
                   jax.ShapeDtypeStruct((s, f), BF16), jax.ShapeDtypeStruct((s, f), BF16),
                   jax.ShapeDtypeStruct((s, f), BF16), jax.ShapeDtypeStruct((s, d), BF16)],
        scratch_shapes=[], compiler_params=_params(1, VMEM_LIMIT), operands=(dxo, x, g, u, nw, wg, wu, wd))


def _tile(n, cap=256):
    best = LANE
    for t in range(LANE, cap + 1, LANE):
        if n % t == 0:
            best = t
    return best


def mm_tn(a, b):
    s, k = a.shape
    n = b.shape[1]
    tk = _tile(k)

    def body(a_ref, b_ref, o_ref):
        o_ref[...] = dot_nn(a_ref[...].astype(BF16).T, b_ref[...]).astype(BF16)

    return pl.pallas_call(
        body, name="mm_tn", grid=(k // tk,),
        in_specs=[pl.BlockSpec((s, tk), lambda i: (0, i)), _full_spec((s, n))],
        out_specs=pl.BlockSpec((tk, n), lambda i: (i, 0)), out_shape=jax.ShapeDtypeStruct((k, n), BF16),
        compiler_params=_params(1, VMEM_LIMIT),
    )(a, b)


def head_loss(x, fw, tgt):
    s, d = x.shape
    tm = min(ROW_TILE, s)

    def body(x_ref, fw_ref, t_ref, loss_ref, dx_ref, dfw_ref):
        xv = x_ref[...]
        err = _rms_fwd(xv, fw_ref[...]) - t_ref[...]
        part = jnp.zeros((1, LANE), F32) + 0.5 * jnp.sum(err * err) / d
        dx, dfw = _rms_bwd(xv, fw_ref[...], err / d)
        dx_ref[...] = dx
        first = pl.program_id(0) == 0
        _acc(loss_ref, first, part)
        _acc(dfw_ref, first, dfw)

    return pl.pallas_call(
        body, name="head_loss", grid=(s // tm,),
        in_specs=[_row_spec(tm, d), _full_spec((1, d)), _row_spec(tm, d)],
        out_specs=[_full_spec((1, LANE)), _row_spec(tm, d), _full_spec((1, d))],
        out_shape=[jax.ShapeDtypeStruct((1, LANE), F32), jax.ShapeDtypeStruct((s, d), F32),
                   jax.ShapeDtypeStruct((1, d), F32)],
        compiler_params=_params(1),
    )(x, fw, tgt)


def _conv_pre(ext, cw_ref, cb_ref):
    shifted = [pltpu.roll(ext, CONV_WIDTH - 1 - i, 0)[BLK:] if i < CONV_WIDTH - 1 else ext[BLK:]
               for i in range(CONV_WIDTH)]
    acc = cb_ref[...] + sum(cw_ref[i:i + 1, :] * shifted[i] for i in range(CONV_WIDTH))
    return acc, shifted


def conv_fwd(xbc, cw, cb, sides=()):
    s, n = xbc.shape

    def body(cur_ref, prev_ref, cw_ref, cb_ref, o_ref):
        prev = jnp.where(pl.program_id(0) > 0, prev_ref[...], 0.0)
        acc, _ = _conv_pre(jnp.concatenate([prev, cur_ref[...]], axis=0), cw_ref, cb_ref)
        o_ref[...] = acc * _sigmoid(acc)

    return hosted_call(
        body, sides, name="conv_fwd", grid=(s // BLK,),
        in_specs=[pl.BlockSpec((BLK, n), lambda c: (c, 0)), pl.BlockSpec((BLK, n), lambda c: (jnp.maximum(c - 1, 0), 0)),
                  _full_spec(cw.shape), _full_spec((1, n))],
        out_specs=[pl.BlockSpec((BLK, n), lambda c: (c, 0))], out_shape=[jax.ShapeDtypeStruct((s, n), F32)],
        scratch_shapes=[], compiler_params=_params(1), operands=(xbc, xbc, cw, cb))


def conv_bwd(xbc, cw, cb, du, sides=()):
    s, n = xbc.shape
    nb = s // BLK

    def body(cur_ref, prev_ref, cw_ref, cb_ref, du_ref, dx_ref, dcw_ref, dcb_ref, nxt_ref):
        i = pl.program_id(0)
        c = nb - 1 - i
        prev = jnp.where(c > 0, prev_ref[...], 0.0)
        acc, shifted = _conv_pre(jnp.concatenate([prev, cur_ref[...]], axis=0), cw_ref, cb_ref)
        sg = _sigmoid(acc)
        dacc = du_ref[...] * (sg * (1.0 + acc * (1.0 - sg)))

        @pl.when(i == 0)
        def _():
            nxt_ref[...] = jnp.zeros_like(nxt_ref)
            dcw_ref[...] = jnp.zeros_like(dcw_ref)
            dcb_ref[...] = jnp.zeros_like(dcb_ref)

        dcb_ref[...] += jnp.sum(dacc, axis=0, keepdims=True)
        for t in range(CONV_WIDTH):
            dcw_ref[t:t + 1, :] += jnp.sum(dacc * shifted[t], axis=0, keepdims=True)
        ext = jnp.concatenate([dacc, nxt_ref[...]], axis=0)
        dx = cw_ref[CONV_WIDTH - 1:CONV_WIDTH, :] * dacc
        for t in range(CONV_WIDTH - 1):
            dx += cw_ref[t:t + 1, :] * pltpu.roll(ext, 2 * BLK - (CONV_WIDTH - 1 - t), 0)[:BLK]
        dx_ref[...] = dx.astype(dx_ref.dtype)
        nxt_ref[...] = dacc

    rev = lambda i: (nb - 1 - i, 0)
    return hosted_call(
        body, sides, name="conv_bwd", grid=(nb,),
        in_specs=[pl.BlockSpec((BLK, n), rev), pl.BlockSpec((BLK, n), lambda i: (jnp.maximum(nb - 2 - i, 0), 0)),
                  _full_spec(cw.shape), _full_spec((1, n)), pl.BlockSpec((BLK, n), rev)],
        out_specs=[pl.BlockSpec((BLK, n), rev), _full_spec((8, n)), _full_spec((1, n))],
        out_shape=[jax.ShapeDtypeStruct((s, n), BF16), jax.ShapeDtypeStruct((8, n), F32),
                   jax.ShapeDtypeStruct((1, n), F32)],
        scratch_shapes=[pltpu.VMEM((BLK, n), F32)],
        compiler_params=_params(1), operands=(xbc, xbc, cw, cb, du))


N_PAIR = SSD_HEADS // 2
B_LO = SSD_WIDTH
C_LO = SSD_WIDTH + 2 * D_STATE


def _softplus(x):
    return jnp.maximum(x, 0.0) + jnp.log(1.0 + jnp.exp(-jnp.abs(x)))


def _ssd_chunk(u_ref, dt_ref, dtb_ref, alog_ref):
    shape = (BLK, BLK)
    tri = _iota(shape, 1) <= _iota(shape, 0)
    pre = dt_ref[...] + dtb_ref[...]
    dt = _softplus(pre)
    a = -jnp.exp(alog_ref[...])
    acum = dot_nn(tri.astype(F32), dt * a, exact=True)
    acum_t = acum.T
    last = _sub_row(acum, BLK - 1)
    heads = []
    for h in range(SSD_HEADS):
        col = _lane_col(acum, h)
        seg = jnp.where(tri, col - _sub_row(acum_t, h), -1e30)
        heads.append(dict(col=col, dm=jnp.exp(seg), dt=_lane_col(dt, h), last=_lane_col(last, h)))
    return tri, pre, dt, a, heads


def _pair_mix(lo_mask, v0, v1):
    return jnp.where(lo_mask, v0, v1)


def ssd_fwd(u, z, dtr, dtb, alog, dsk, nw, sides=()):
    s = u.shape[0]
    nc = s // BLK

    def body(u_ref, z_ref, dt_ref, dtb_ref, alog_ref, dsk_ref, nw_ref, y_ref, st_ref, s_ref):
        @pl.when(pl.program_id(0) == 0)
        def _():
            s_ref[...] = jnp.zeros_like(s_ref)

        _, _, _, _, heads = _ssd_chunk(u_ref, dt_ref, dtb_ref, alog_ref)
        lo_lane = _iota((BLK, LANE), 1) < HEAD_DIM
        lo_sub = _iota((BLK, LANE), 0) < HEAD_DIM
        ys = []
        for p in range(N_PAIR):
            g = p // 2
            h0, h1 = heads[2 * p], heads[2 * p + 1]
            bg = u_ref[:, B_LO + g * D_STATE:B_LO + (g + 1) * D_STATE]
            cg = u_ref[:, C_LO + g * D_STATE:C_LO + (g + 1) * D_STATE]
            xs = u_ref[:, p * LANE:(p + 1) * LANE]
            xp = xs * _pair_mix(lo_lane, h0["dt"], h1["dt"])
            gm = dot_nt(cg, bg)
            yd = _pair_mix(lo_lane, dot_nn(gm * h0["dm"], xp), dot_nn(gm * h1["dm"], xp))
            sp = s_ref[p]
            st_ref[0, p] = sp
            yo = _pair_mix(lo_lane, jnp.exp(h0["col"]), jnp.exp(h1["col"])) * dot_nt(cg, sp)
            dskp = _pair_mix(lo_lane, _lane_col(dsk_ref[...], 2 * p), _lane_col(dsk_ref[...], 2 * p + 1))
            ys.append(yd + yo + xs * dskp)
            wp = _pair_mix(lo_lane, jnp.exp(h0["last"] - h0["col"]), jnp.exp(h1["last"] - h1["col"]))
            el = _pair_mix(lo_sub, jnp.exp(h0["last"]), jnp.exp(h1["last"]))
            s_ref[p] = el * sp + dot_tn(wp * xp, bg)
        y = jnp.concatenate(ys, axis=1)
        zv = z_ref[...]
        y_ref[...] = _rms_fwd(y * zv * _sigmoid(zv), nw_ref[...])

    vec = _full_spec((1, LANE))
    return hosted_call(
        body, sides, name="ssd_fwd", grid=(nc,),
        in_specs=[_row_spec(BLK, CONV_DIM), _row_spec(BLK, SSD_WIDTH), _row_spec(BLK, LANE), vec, vec, vec,
                  _full_spec((1, SSD_WIDTH))],
        out_specs=[_row_spec(BLK, SSD_WIDTH), pl.BlockSpec((1, N_PAIR, LANE, D_STATE), lambda c: (c, 0, 0, 0))],
        out_shape=[jax.ShapeDtypeStruct((s, SSD_WIDTH), F32), jax.ShapeDtypeStruct((nc, N_PAIR, LANE, D_STATE), F32)],
        scratch_shapes=[pltpu.VMEM((N_PAIR, LANE, D_STATE), F32)],
        compiler_params=_params(1), operands=(u, z, dtr, dtb, alog, dsk, nw))


def ssd_bwd(u, z, dtr, st, dyo, dtb, alog, dsk, nw, sides=()):
    s = u.shape[0]
    nc = s // BLK

    def body(u_ref, z_ref, dt_ref, st_ref, dyo_ref, dtb_ref, alog_ref, dsk_ref, nw_ref,
             du_ref, dz_ref, ddt_ref, ddtb_ref, dalog_ref, ddsk_ref, dnw_ref, ds_ref):
        first = pl.program_id(0) == 0

        @pl.when(first)
        def _():
            ds_ref[...] = jnp.zeros_like(ds_ref)

        tri, pre, dt, a, heads = _ssd_chunk(u_ref, dt_ref, dtb_ref, alog_ref)
        shape = (BLK, LANE)
        lane = _iota(shape, 1)
        lo_lane = lane < HEAD_DIM
        lo_sub = _iota(shape, 0) < HEAD_DIM
        pairs = []
        ys = []
        for p in range(N_PAIR):
            g = p // 2
            h0, h1 = heads[2 * p], heads[2 * p + 1]
            bg = u_ref[:, B_LO + g * D_STATE:B_LO + (g + 1) * D_STATE]
            cg = u_ref[:, C_LO + g * D_STATE:C_LO + (g + 1) * D_STATE]
            xs = u_ref[:, p * LANE:(p + 1) * LANE]
            dtp = _pair_mix(lo_lane, h0["dt"], h1["dt"])
            xp = xs * dtp
            gm = dot_nt(cg, bg)
            m0, m1 = gm * h0["dm"], gm * h1["dm"]
            sp = st_ref[0, p]
            eap = _pair_mix(lo_lane, jnp.exp(h0["col"]), jnp.exp(h1["col"]))
            yo = eap * dot_nt(cg, sp)
            dskp = _pair_mix(lo_lane, _lane_col(dsk_ref[...], 2 * p), _lane_col(dsk_ref[...], 2 * p + 1))
            ys.append(_pair_mix(lo_lane, dot_nn(m0, xp), dot_nn(m1, xp)) + yo + xs * dskp)
            pairs.append(dict(bg=bg, cg=cg, xs=xs, dtp=dtp, xp=xp, gm=gm, m=(m0, m1), sp=sp, eap=eap, yo=yo, dskp=dskp))
        y = jnp.concatenate(ys, axis=1)
        zv = z_ref[...]
        sz = _sigmoid(zv)
        gate = zv * sz
        dyg, dnw = _rms_bwd(y * gate, nw_ref[...], dyo_ref[...])
        _acc(dnw_ref, first, dnw)
        dy = dyg * gate
        dz_ref[...] = (dyg * y * (sz * (1.0 + zv * (1.0 - sz)))).astype(dz_ref.dtype)

        zeros = jnp.zeros(shape, F32)
        dacum_col = zeros
        dacum_row = zeros
        ddt = zeros
        ddsk = jnp.zeros((1, LANE), F32)
        dlast = jnp.zeros((1, LANE), F32)
        head_row = _iota((1, LANE), 1)
        sub = _iota(shape, 0)
        db = [zeros, zeros]
        dc = [zeros, zeros]
        for p in range(N_PAIR):
            g = p // 2
            q = pairs[p]
            dyp = dy[:, p * LANE:(p + 1) * LANE]
            dsn = ds_ref[p]
            t = dyp * q["xs"]
            dxs = dyp * q["dskp"]
            dcs = dyp * q["eap"]
            dc[g] = dc[g] + dot_nn(dcs, q["sp"])
            dsp = dot_tn(dcs, q["cg"])
            dea = dyp * q["yo"]
            elp = _pair_mix(lo_sub, jnp.exp(heads[2 * p]["last"]), jnp.exp(heads[2 * p + 1]["last"]))
            dsp = dsp + elp * dsn
            dels = dsn * q["sp"] * elp
            wp = _pair_mix(lo_lane, jnp.exp(heads[2 * p]["last"] - heads[2 * p]["col"]),
                           jnp.exp(heads[2 * p + 1]["last"] - heads[2 * p + 1]["col"]))
            dv = dot_nt(q["bg"], dsn)
            db[g] = db[g] + dot_nn(wp * q["xp"], dsn)
            dxp = dv * wp
            dwv = dv * q["xp"] * wp
            dgm = zeros
            for k in range(2):
                h = 2 * p + k
                mine = lo_lane if k == 0 else jnp.logical_not(lo_lane)
                mine_sub = lo_sub if k == 0 else jnp.logical_not(lo_sub)
                dyh = jnp.where(mine, dyp, 0.0)
                dm = dot_nt(dyh, q["xp"])
                dxp = dxp + dot_tn(q["m"][k], dyh)
                dgm = dgm + dm * heads[h]["dm"]
                e = dm * q["m"][k]
                onehot = lane == h
                dw_col = jnp.sum(jnp.where(mine, dwv, 0.0), axis=1, keepdims=True)
                col = (jnp.sum(e, axis=1, keepdims=True) + jnp.sum(jnp.where(mine, dea, 0.0), axis=1, keepdims=True)
                       - dw_col)
                dacum_col = dacum_col + jnp.where(onehot, col, 0.0)
                dacum_row = dacum_row - jnp.where(sub == h, jnp.sum(e, axis=0, keepdims=True), 0.0)
                dl = jnp.sum(dw_col) + jnp.sum(jnp.where(mine_sub, dels, 0.0))
                dlast = dlast + jnp.where(head_row == h, dl, 0.0)
                ddsk = ddsk + jnp.where(head_row == h, jnp.sum(jnp.where(mine, t, 0.0)), 0.0)
            dc[g] = dc[g] + dot_nn(dgm, q["bg"])
            db[g] = db[g] + dot_tn(dgm, q["cg"])
            dxs = dxs + dxp * q["dtp"]
            tt = dxp * q["xs"]
            for k in range(2):
                mine = lo_lane if k == 0 else jnp.logical_not(lo_lane)
                ddt = ddt + jnp.where(lane == 2 * p + k, jnp.sum(jnp.where(mine, tt, 0.0), axis=1, keepdims=True), 0.0)
            du_ref[:, p * LANE:(p + 1) * LANE] = dxs
            ds_ref[p] = dsp
        for g in range(2):
            du_ref[:, B_LO + g * D_STATE:B_LO + (g + 1) * D_STATE] = db[g]
            du_ref[:, C_LO + g * D_STATE:C_LO + (g + 1) * D_STATE] = dc[g]
        dacum = dacum_col + dacum_row.T + jnp.where(sub == BLK - 1, dlast, 0.0)
        dda = dot_tn(tri.astype(F32), dacum, exact=True)
        ddt = ddt + dda * a
        _acc(dalog_ref, first, jnp.sum(dda * dt, axis=0, keepdims=True) * a)
        dpre = ddt * _sigmoid(pre)
        ddt_ref[...] = dpre.astype(ddt_ref.dtype)
        _acc(ddtb_ref, first, jnp.sum(dpre, axis=0, keepdims=True))
        _acc(ddsk_ref, first, ddsk)

    rev = lambda i: (nc - 1 - i, 0)
    vec = _full_spec((1, LANE))
    rows = lambda n: pl.BlockSpec((BLK, n), rev)
    return hosted_call(
        body, sides, name="ssd_bwd", grid=(nc,),
        in_specs=[rows(CONV_DIM), rows(SSD_WIDTH), rows(LANE),
                  pl.BlockSpec((1, N_PAIR, LANE, D_STATE), lambda i: (nc - 1 - i, 0, 0, 0)), rows(SSD_WIDTH),
                  vec, vec, vec, _full_spec((1, SSD_WIDTH))],
        out_specs=[rows(CONV_DIM), rows(SSD_WIDTH), rows(LANE), vec, vec, vec, _full_spec((1, SSD_WIDTH))],
        out_shape=[jax.ShapeDtypeStruct((s, CONV_DIM), F32), jax.ShapeDtypeStruct((s, SSD_WIDTH), BF16),
                   jax.ShapeDtypeStruct((s, LANE), BF16)] + [jax.ShapeDtypeStruct((1, LANE), F32)] * 3
        + [jax.ShapeDtypeStruct((1, SSD_WIDTH), F32)],
        scratch_shapes=[pltpu.VMEM((N_PAIR, LANE, D_STATE), F32)],
        compiler_params=_params(1), operands=(u, z, dtr, st, dyo, dtb, alog, dsk, nw))


SB_PAIRS = SB_WIDTH // LANE
SB_SCALE = HEAD_DIM ** -0.5


SB_TQ = 256


def _sb_tq(s):
    return min(SB_TQ, s)


def _sb_stack(x):
    lo_lane = _iota(x.shape, 1) < HEAD_DIM
    return jnp.concatenate([jnp.where(lo_lane, x, 0.0), jnp.where(lo_lane, 0.0, x)], axis=0)


def _sb_unstack(x2):
    tq = x2.shape[0] // 2
    lo_lane = _iota((tq, LANE), 1) < HEAD_DIM
    return jnp.where(lo_lane, x2[:tq], x2[tq:])


def _sb_logits(q2, kj, row0, col0, masked):
    shape = (q2.shape[0], BLK)
    tq = shape[0] // 2
    z = dot_nt(q2, kj)
    t = jnp.log(1.0 + jnp.exp(-jnp.abs(z)))
    ls = jnp.minimum(z, 0.0) - t
    lk = jnp.minimum(-z, 0.0) - t
    if not masked:
        return None, ls, lk
    row = _iota(shape, 0)
    valid = (col0 + _iota(shape, 1)) < (row0 + jnp.where(row < tq, row, row - tq))
    return valid, ls, jnp.where(valid, lk, 0.0)


def _sb_where(valid, x):
    return x if valid is None else jnp.where(valid, x, 0.0)


def _sums(x, mask2, parts):
    acc = None
    rest = x
    for _ in range(parts):
        term = rest.astype(BF16)
        rest = rest - term.astype(F32)
        d = lax.dot_general(term, mask2, (((1,), (0,)), ((), ())), preferred_element_type=F32)
        acc = d if acc is None else acc + d
    return acc[:, :BLK], acc[:, BLK:]


def _mask2(cond):
    return jnp.concatenate([cond.astype(BF16), jnp.ones(cond.shape, BF16)], axis=1)


def _sb_specs(s):
    tq = _sb_tq(s)
    qspec = pl.BlockSpec((tq, LANE), lambda p, i: (i, p))
    kspec = pl.BlockSpec((s, LANE), lambda p, i: (0, SB_PAIRS + p))
    vspec = pl.BlockSpec((s, LANE), lambda p, i: (0, 2 * SB_PAIRS + p))
    return qspec, kspec, vspec


SB_FLOOR = -104.0


def sb_fwd(qkv, sides=()):
    s = qkv.shape[0]
    tq = _sb_tq(s)
    kpq = tq // BLK

    def body(q_ref, k_ref, v_ref, o_ref, t_ref, n_ref, acc_ref):
        qi = pl.program_id(1)
        q2 = _sb_stack(q_ref[...] * SB_SCALE).astype(BF16)
        later = _mask2(_iota((BLK, BLK), 0) > _iota((BLK, BLK), 1))
        acc_ref[...] = jnp.zeros_like(acc_ref)

        def step(j, r, masked):
            rows = pl.ds(pl.multiple_of(j * BLK, BLK), BLK)
            valid, ls, lk = _sb_logits(q2, k_ref[rows, :], qi * tq, j * BLK, masked)
            after, total = _sums(lk, later, 2)
            w = _sb_where(valid, jnp.exp(ls + r + after))
            acc_ref[...] += dot_nn(w, v_ref[rows, :])
            return r + total

        r = jnp.zeros((2 * tq, LANE), F32)
        for d in reversed(range(kpq)):
            r = step(kpq * qi + d, r, True)

        def tile(g, r):
            for d in reversed(range(kpq)):
                r = step(kpq * (qi - 1 - g) + d, r, False)
            return r

        n, r = lax.while_loop(lambda c: jnp.logical_and(c[0] < qi, jnp.max(c[1]) > SB_FLOOR),
                              lambda c: (c[0] + 1, tile(c[0], c[1])), (jnp.int32(0), r))
        o_ref[...] = _sb_unstack(acc_ref[...])
        t_ref[...] = jnp.concatenate([r[:tq], r[tq:]], axis=1)
        n_ref[...] = jnp.zeros(n_ref.shape, F32) + n.astype(F32)

    return hosted_call(
        body, sides, name="sb_fwd", grid=(SB_PAIRS, s // tq),
        in_specs=list(_sb_specs(s)),
        out_specs=[pl.BlockSpec((tq, LANE), lambda p, i: (i, p)), pl.BlockSpec((tq, 2 * LANE), lambda p, i: (i, p)),
                   pl.BlockSpec((None, None, 8, LANE), lambda p, i: (p, i, 0, 0))],
        out_shape=[jax.ShapeDtypeStruct((s, SB_WIDTH), F32), jax.ShapeDtypeStruct((s, 2 * SB_WIDTH), F32),
                   jax.ShapeDtypeStruct((SB_PAIRS, s // tq, 8, LANE), F32)],
        scratch_shapes=[pltpu.VMEM((2 * tq, LANE), F32)],
        compiler_params=_params(2), operands=(qkv, qkv, qkv))


def sb_bwd(qkv, tot, swept, do, do_col=0, sides=()):
    s = qkv.shape[0]
    tq = _sb_tq(s)
    kpq = tq // BLK

    def body(q_ref, k_ref, v_ref, t_ref, n_ref, do_ref, dq_ref, dk_ref, dv_ref, acc_ref):
        qi = pl.program_id(1)
        n = jnp.clip(jnp.max(n_ref[...]).astype(jnp.int32), 0, qi)
        q2 = _sb_stack(q_ref[...] * SB_SCALE).astype(BF16)
        do2 = _sb_stack(do_ref[...]).astype(BF16)
        tot2 = jnp.concatenate([t_ref[:, :LANE], t_ref[:, LANE:]], axis=0)
        sq = (BLK, BLK)
        later = _mask2(_iota(sq, 0) > _iota(sq, 1))
        before = _mask2(_iota(sq, 0) < _iota(sq, 1))
        acc_ref[...] = jnp.zeros_like(acc_ref)

        @pl.when(qi == 0)
        def _():
            dk_ref[...] = jnp.zeros_like(dk_ref)
            dv_ref[...] = jnp.zeros_like(dv_ref)

        def step(j, carry, masked):
            rc, fc = carry
            rows = pl.ds(pl.multiple_of(j * BLK, BLK), BLK)
            kj = k_ref[rows, :]
            vj = v_ref[rows, :]
            valid, ls, lk = _sb_logits(q2, kj, qi * tq, j * BLK, masked)
            after, total = _sums(lk, later, 2)
            rc = rc - total
            w = _sb_where(valid, jnp.exp(ls + rc + after))
            e = w * dot_nt(do2, vj)
            f_in, f_tot = _sums(e, before, 2)
            sg = jnp.exp(ls)
            dz = _sb_where(valid, e * (1.0 - sg) - (fc + f_in) * sg)
            acc_ref[...] += dot_nn(dz, kj)
            dk_ref[rows, :] += dot_tn(dz, q2)
            dv_ref[rows, :] += dot_tn(w, do2)
            return rc, fc + f_tot

        def tile(g, carry):
            for d in range(kpq):
                carry = step(kpq * g + d, carry, False)
            return carry

        carry = lax.fori_loop(qi - n, qi, tile, (tot2, jnp.zeros((2 * tq, LANE), F32)))
        for d in range(kpq):
            carry = step(kpq * qi + d, carry, True)
        dq_ref[...] = (SB_SCALE * _sb_unstack(acc_ref[...])).astype(dq_ref.dtype)

    qspec, kspec, vspec = _sb_specs(s)
    blk = pl.BlockSpec((tq, LANE), lambda p, i: (i, p))
    acc = pl.BlockSpec((s, LANE), lambda p, i: (0, p))
    return hosted_call(
        body, sides, name="sb_bwd", grid=(SB_PAIRS, s // tq),
        in_specs=[qspec, kspec, vspec, pl.BlockSpec((tq, 2 * LANE), lambda p, i: (i, p)),
                  pl.BlockSpec((None, None, 8, LANE), lambda p, i: (p, i, 0, 0)),
                  pl.BlockSpec((tq, LANE), lambda p, i: (i, do_col + p))],
        out_specs=[blk, acc, acc],
        out_shape=[jax.ShapeDtypeStruct((s, SB_WIDTH), BF16)] + [jax.ShapeDtypeStruct((s, SB_WIDTH), F32)] * 2,
        scratch_shapes=[pltpu.VMEM((2 * tq, LANE), F32)],
        compiler_params=_params(2), operands=(qkv, qkv, qkv, tot, swept, do))


POOL_GROUP_DIM = POOL_WIDTH // len(POOL_WINDOWS)


assert all(w == 2 ** (i + 1) for i, w in enumerate(POOL_WINDOWS))


def _pool_inv(c):
    group = _iota((BLK, POOL_WIDTH), 1) // POOL_GROUP_DIM
    pos = c * BLK + _iota((BLK, POOL_WIDTH), 0)
    win = jnp.zeros((BLK, POOL_WIDTH), jnp.int32)
    for gi, wn in enumerate(POOL_WINDOWS):
        win = jnp.where(group == gi, wn, win)
    return 1.0 / jnp.minimum(pos + 1, win).astype(F32)


def _window_sums(ext, trailing):
    group = _iota(ext.shape, 1) // POOL_GROUP_DIM
    acc = ext
    out = None
    for gi in range(len(POOL_WINDOWS)):
        shift = 2 ** gi
        acc = acc + pltpu.roll(acc, shift if trailing else ext.shape[0] - shift, 0)
        out = acc if out is None else jnp.where(group == gi, acc, out)
    return out


def _pool_pooled(ext, cur, inv):
    return _window_sums(ext, True)[BLK:] * inv - cur


def pool_fwd(p, wblk, pb, ps):
    s, n = p.shape

    def body(cur_ref, prev_ref, w_ref, pb_ref, ps_ref, o_ref):
        c = pl.program_id(0)
        cur = cur_ref[...]
        prev = jnp.where(c > 0, prev_ref[...], 0.0)
        pooled = _pool_pooled(jnp.concatenate([prev, cur], axis=0), cur, _pool_inv(c))
        o_ref[...] = (dot_nn(pooled, w_ref[...]) + pb_ref[...]) * ps_ref[...]

    return pl.pallas_call(
        body, name="pool_fwd", grid=(s // BLK,),
        in_specs=[pl.BlockSpec((BLK, n), lambda c: (c, 0)), pl.BlockSpec((BLK, n), lambda c: (jnp.maximum(c - 1, 0), 0)),
                  _full_spec((n, n)), _full_spec((1, n)), _full_spec((1, n))],
        out_specs=pl.BlockSpec((BLK, n), lambda c: (c, 0)), out_shape=jax.ShapeDtypeStruct((s, n), F32),
        compiler_params=_params(1),
    )(p, p, wblk, pb, ps)


def pool_bwd(p, wblk, pb, ps, dout, do_col=0, sides=()):
    s, n = p.shape
    nb = s // BLK

    def body(cur_ref, prev_ref, w_ref, pb_ref, ps_ref, do_ref, dp_ref, dw_ref, dpb_ref, dps_ref, carry_ref):
        i = pl.program_id(0)
        c = nb - 1 - i
        first = i == 0
        cur = cur_ref[...]
        prev = jnp.where(c > 0, prev_ref[...], 0.0)
        inv = _pool_inv(c)
        pooled = _pool_pooled(jnp.concatenate([prev, cur], axis=0), cur, inv)
        mixed = dot_nn(pooled, w_ref[...]) + pb_ref[...]
        dov = do_ref[...]
        dmixed = dov * ps_ref[...]
        _acc(dps_ref, first, jnp.sum(dov * mixed, axis=0, keepdims=True))
        _acc(dpb_ref, first, jnp.sum(dmixed, axis=0, keepdims=True))
        _acc(dw_ref, first, dot_tn(pooled, dmixed))
        dpooled = dot_nt(dmixed, w_ref[...])
        dext = _window_sums(jnp.concatenate([jnp.zeros((BLK, n), F32), dpooled * inv], axis=0), False)

        @pl.when(first)
        def _():
            carry_ref[...] = jnp.zeros_like(carry_ref)

        dp_ref[...] = (dext[BLK:] - dpooled + carry_ref[...]).astype(dp_ref.dtype)
        carry_ref[...] = dext[:BLK]

    rev = lambda i: (nb - 1 - i, 0)
    return hosted_call(
        body, sides, name="pool_bwd", grid=(nb,),
        in_specs=[pl.BlockSpec((BLK, n), rev), pl.BlockSpec((BLK, n), lambda i: (jnp.maximum(nb - 2 - i, 0), 0)),
                  _full_spec((n, n)), _full_spec((1, n)), _full_spec((1, n)),
                  pl.BlockSpec((BLK, n), lambda i: (nb - 1 - i, do_col))],
        out_specs=[pl.BlockSpec((BLK, n), rev), _full_spec((n, n)), _full_spec((1, n)), _full_spec((1, n))],
        out_shape=[jax.ShapeDtypeStruct((s, n), BF16), jax.ShapeDtypeStruct((n, n), F32),
                   jax.ShapeDtypeStruct((1, n), F32), jax.ShapeDtypeStruct((1, n), F32)],
        scratch_shapes=[pltpu.VMEM((BLK, n), F32)],
        compiler_params=_params(1), operands=(p, p, wblk, pb, ps, dout))


def _row_tile(rows):
    if rows <= 512:
        return rows
    for t in (512, 256, 128, 64, 32, 16, 8):
        if rows % t == 0:
            return t
    return rows


def adamw(w, g, m, v):
    n, rows, cols = w.shape
    tr = _row_tile(rows)

    def body(w_ref, g_ref, m_ref, v_ref, d_ref, nm_ref, nv_ref):
        d_ref[...], nm_ref[...], nv_ref[...] = _adamw_math(w_ref[...], g_ref[...], m_ref[...], v_ref[...])

    spec = pl.BlockSpec((1, tr, cols), lambda i, j: (i, j, 0))
    return pl.pallas_call(
        body, name="adamw", grid=(n, rows // tr), in_specs=[spec] * 4, out_specs=[spec] * 3,
        out_shape=[jax.ShapeDtypeStruct(w.shape, F32)] * 3, compiler_params=_params(2),
    )(w, g, m, v)


def _adamw_math(w, g, m, v):
    nm = ADAM_B1 * m + (1.0 - ADAM_B1) * g
    nv = ADAM_B2 * v + (1.0 - ADAM_B2) * (g * g)
    m_hat = nm / (1.0 - ADAM_B1 ** ADAM_STEP)
    v_hat = nv / (1.0 - ADAM_B2 ** ADAM_STEP)
    return -ADAM_LR * (m_hat / (jnp.sqrt(v_hat) + ADAM_EPS) + ADAM_WD * w), nm, nv


def adamw_small(ws, gs, ms, vs):
    n = len(ws)

    def body(*refs):
        for i in range(n):
            outs = _adamw_math(*[refs[k * n + i][...] for k in range(4)])
            for k in range(3):
                refs[(4 + k) * n + i][...] = outs[k]

    vm = pl.BlockSpec(memory_space=pltpu.VMEM)
    outs = pl.pallas_call(
        body, name="adamw_small", in_specs=[vm] * (4 * n), out_specs=[vm] * (3 * n),
        out_shape=[jax.ShapeDtypeStruct(w.shape, F32) for w in ws] * 3,
    )(*ws, *gs, *ms, *vs)
    return outs[:n], outs[n:2 * n], outs[2 * n:]


def slab_sum(srcs, n_out, out_dtype, into=None, slot=0):
    _, rows, cols = srcs[0][0].shape
    tr = _row_tile(rows)
    n_src = len(srcs)
    sel = jnp.stack([jnp.asarray(base, jnp.int32) for _, base, _ in srcs])

    def body(sel_ref, *refs):
        acc = refs[0][...].astype(F32)
        for r in refs[1:n_src]:
            acc = acc + r[...].astype(F32)
        refs[-1][...] = acc.astype(out_dtype)

    def in_spec(k, step):
        return pl.BlockSpec((None, tr, cols), lambda o, i, sel_ref: (sel_ref[k] + step * o, i, 0))

    shape = (n_out, rows, cols) if into is None else into.shape
    return pl.pallas_call(
        body, name="slab_sum",
        grid_spec=pltpu.PrefetchScalarGridSpec(
            num_scalar_prefetch=1, grid=(n_out, rows // tr),
            in_specs=[in_spec(k, step) for k, (_, _, step) in enumerate(srcs)] + ([] if into is None else [ANY]),
            out_specs=pl.BlockSpec((None, tr, cols), lambda o, i, sel_ref: (slot + o, i, 0))),
        out_shape=jax.ShapeDtypeStruct(shape, out_dtype), compiler_params=_params(2),
        input_output_aliases={} if into is None else {1 + n_src: 0},
    )(sel, *[a for a, _, _ in srcs], *([] if into is None else [into]))


ICI_FLIPS = ((1, 0, 0), (0, 1, 0), (1, 1, 0))
D2D_FLIPS = ((0, 0, 1),)
ANY = pl.BlockSpec(memory_space=pl.ANY)


def _me():
    return lax.axis_index("x"), lax.axis_index("y"), lax.axis_index("c")


def _flipped(me, flip):
    return tuple(1 - m if f else m for m, f in zip(me, flip))


def _chip(dev):
    return 2 * dev[0] + dev[1]


def _dev(dev):
    return 4 * dev[0] + 2 * dev[1] + dev[2]


N_CHIP = 4
D2D = (0, 0, 1)


class Exchange:
    def __init__(self, xs, n_out, copies, own=None, in_place=False):
        self.xs, self.copies, self.own, self.in_place = list(xs), copies, own, in_place
        self.n_arr, self.n_cp = len(self.xs), len(copies)
        self.out_shape = [jax.ShapeDtypeStruct((n_out,) + x.shape[1:], x.dtype) for x in self.xs]
        self.scratch = [pltpu.SemaphoreType.DMA((self.n_arr * self.n_cp,)),
                        pltpu.SemaphoreType.DMA((self.n_arr * self.n_cp,)), pltpu.SemaphoreType.DMA((self.n_arr,))]

    def _own(self, x_refs, o_refs, sems, me):
        if self.own is None:
            return []
        return [pltpu.make_async_copy(x_refs[a].at[self.own[0](me)], o_refs[a].at[self.own[1](me)], sems[2].at[a])
                for a in range(self.n_arr)]

    def _copy(self, x_refs, o_refs, sems, me, a, j, sender):
        flip, src_slot, dst_slot = self.copies[j]
        k = a * self.n_cp + j
        return pltpu.make_async_remote_copy(
            src_ref=x_refs[a].at[src_slot(me)], dst_ref=o_refs[a].at[dst_slot(sender)],
            send_sem=sems[0].at[k], recv_sem=sems[1].at[k], device_id=_flipped(me, flip), device_id_type=MESH)

    def start(self, x_refs, o_refs, sems):
        me = _me()
        for cp in self._own(x_refs, o_refs, sems, me):
            cp.start()
        for j in range(self.n_cp):
            for a in range(self.n_arr):
                self._copy(x_refs, o_refs, sems, me, a, j, me).start()

    def wait(self, x_refs, o_refs, sems):
        me = _me()
        for j in range(self.n_cp):
            for a in range(self.n_arr):
                self._copy(x_refs, o_refs, sems, me, a, j, _flipped(me, self.copies[j][0])).wait_recv()
        for j in range(self.n_cp):
            for a in range(self.n_arr):
                self._copy(x_refs, o_refs, sems, me, a, j, me).wait_send()
        for cp in self._own(x_refs, o_refs, sems, me):
            cp.wait()

    def run(self, name):
        n = self.n_arr

        def body(*refs):
            self.start(refs[:n], refs[n:2 * n], refs[2 * n:])
            self.wait(refs[:n], refs[n:2 * n], refs[2 * n:])

        return pl.pallas_call(
            body, name=name, in_specs=[ANY] * n, out_specs=[ANY] * n, out_shape=self.out_shape,
            input_output_aliases={a: a for a in range(n)} if self.in_place else {}, scratch_shapes=self.scratch,
        )(*self.xs)


def hosted_call(body, sides, *, name, grid, in_specs, out_specs, out_shape, scratch_shapes, compiler_params, operands):
    n_in, n_out, n_scr = len(in_specs), len(out_specs), len(scratch_shapes)
    live = [s for s in sides if s is not None]
    if not live:
        outs = pl.pallas_call(body, name=name, grid=grid, in_specs=in_specs, out_specs=out_specs, out_shape=out_shape,
                              scratch_shapes=scratch_shapes, compiler_params=compiler_params)(*operands)
        return outs, [None] * len(sides)
    n = sum(s.n_arr for s in live)
    lo = [sum(s.n_arr for s in live[:i]) for i in range(len(live))]

    def full_body(*refs):
        ins, sx = refs[:n_in], refs[n_in:n_in + n]
        outs, so = refs[n_in + n:n_in + n + n_out], refs[n_in + n + n_out:n_in + 2 * n + n_out]
        scr, sems = refs[n_in + 2 * n + n_out:n_in + 2 * n + n_out + n_scr], refs[n_in + 2 * n + n_out + n_scr:]
        first = functools.reduce(jnp.logical_and, [pl.program_id(a) == 0 for a in range(len(grid))])
        last = functools.reduce(jnp.logical_and, [pl.program_id(a) == g - 1 for a, g in enumerate(grid)])
        parts = [(s, sx[l:l + s.n_arr], so[l:l + s.n_arr], sems[3 * i:3 * i + 3]) for i, (s, l) in enumerate(zip(live, lo))]

        @pl.when(first)
        def _():
            for s, x, o, m in parts:
                s.start(x, o, m)

        body(*ins, *outs, *scr)

        @pl.when(last)
        def _():
            for s, x, o, m in parts:
                s.wait(x, o, m)

    aliases = {n_in + l + a: n_out + l + a for s, l in zip(live, lo) if s.in_place for a in range(s.n_arr)}
    outs = pl.pallas_call(
        full_body, name=name + "_x", grid=grid, in_specs=list(in_specs) + [ANY] * n,
        out_specs=list(out_specs) + [ANY] * n, out_shape=list(out_shape) + [o for s in live for o in s.out_shape],
        input_output_aliases=aliases,
        scratch_shapes=list(scratch_shapes) + [m for s in live for m in s.scratch], compiler_params=compiler_params,
    )(*operands, *[x for s in live for x in s.xs])
    side_outs = iter([outs[n_out + l:n_out + l + s.n_arr] for s, l in zip(live, lo)])
    return outs[:n_out], [next(side_outs) if s is not None else None for s in sides]


def gather_ici(shards):
    ici = [(f, lambda me: 0, _dev) for f in ICI_FLIPS]
    return Exchange([s[None] for s in shards], N_DEV, ici, (lambda me: 0, _dev))


def gather_d2d(blocks):
    d2d = [(D2D, (lambda me, k=k: 2 * k + me[2]), (lambda sender, k=k: 2 * k + sender[2])) for k in range(N_CHIP)]
    return Exchange(blocks, N_DEV, d2d, None, in_place=True)


def gathered(blocks):
    return [b.reshape(-1, b.shape[2]) for b in blocks]


def scatter_d2d(parts):
    d2d = [(D2D, (lambda me, k=k: 2 * k + 1 - me[2]), (lambda sender, k=k: k)) for k in range(N_CHIP)]
    return Exchange(parts, N_CHIP, d2d)


def chip_sums(parts, sib):
    c = _me()[2]
    return [slab_sum([(p, c, 2), (s, 0, 1)], N_CHIP, BF16) for p, s in zip(parts, sib)]


def scatter_ici(sums):
    ici = [(f, (lambda me, f=f: _chip(_flipped(me, f))), (lambda sender, i=i: i)) for i, f in enumerate(ICI_FLIPS)]
    return Exchange(sums, len(ICI_FLIPS), ici)


def device_sums(sums, got, into=None, slot=0):
    x, y, _ = _me()
    outs = [slab_sum([(cs, 2 * x + y, 0)] + [(g, i, 0) for i in range(len(ICI_FLIPS))], 1, F32,
                     None if into is None else into[a], slot) for a, (cs, g) in enumerate(zip(sums, got))]
    return outs if into is not None else [o[0] for o in outs]


def all_gather(shards):
    blocks = gather_ici(shards).run("gather_ici")
    return gathered(gather_d2d(blocks).run("gather_d2d"))


def reduce_scatter(parts, *into):
    sums = chip_sums(parts, scatter_d2d(parts).run("scatter_d2d"))
    return device_sums(sums, scatter_ici(sums).run("scatter_ici"), *into)


def all_reduce_small(v):
    flips = D2D_FLIPS + ICI_FLIPS[:2]

    def body(v_ref, o_ref, got_ref, send_sems, recv_sems):
        me = _me()
        o_ref[...] = v_ref[...]
        for i, flip in enumerate(flips):
            cp = pltpu.make_async_remote_copy(
                src_ref=o_ref, dst_ref=got_ref.at[i], send_sem=send_sems.at[i], recv_sem=recv_sems.at[i],
                device_id=_flipped(me, flip), device_id_type=MESH)
            cp.start()
            cp.wait()
            o_ref[...] = o_ref[...] + got_ref[i]

    vm = pl.BlockSpec(memory_space=pltpu.VMEM)
    return pl.pallas_call(
        body, name="all_reduce_small", in_specs=[vm], out_specs=vm, out_shape=jax.ShapeDtypeStruct(v.shape, F32),
        scratch_shapes=[pltpu.VMEM((len(flips),) + v.shape, F32), pltpu.SemaphoreType.DMA((len(flips),)),
                        pltpu.SemaphoreType.DMA((len(flips),))],
    )(v)


def _perm_rows(wt):
    pad = jnp.zeros((D_IN_PAD - D_IN_PROJ, wt.shape[1]), wt.dtype)
    return jnp.concatenate([wt[:DT_LO], wt[DT_HI:], wt[DT_LO:DT_HI], pad], axis=0)


def _unperm_rows(dwt):
    n = D_IN_PROJ - (DT_HI - DT_LO)
    return jnp.concatenate([dwt[:DT_LO], dwt[n:D_IN_PROJ], dwt[DT_LO:n]], axis=0)


def _pad_lanes(v):
    return jnp.pad(v, ((0, 0), (0, LANE - v.shape[1])))[:, None]


def _block_diag(w):
    l, g, n, _ = w.shape
    out = jnp.zeros((l, g * n, g * n), w.dtype)
    for i in range(g):
        out = out.at[:, i * n:(i + 1) * n, i * n:(i + 1) * n].set(w[:, i])
    return out


def _pack(groups):
    flat = []
    for grp in groups:
        parts = [a.reshape(-1) for a in (grp if isinstance(grp, (list, tuple)) else [grp])]
        n = sum(p.shape[0] for p in parts)
        if -n % LANE:
            parts.append(jnp.zeros((-n % LANE,), parts[0].dtype))
        flat += parts
    return jnp.concatenate(flat).reshape(-1, LANE)


def _unpack(buf, shapes):
    out = []
    lo = 0
    buf = buf.reshape(-1)
    for shp in shapes:
        n = 1
        for k in shp:
            n *= k
        out.append(buf[lo:lo + n].reshape(shp))
        lo += n + (-n % LANE)
    return out


def small_params(w, conv_w_full):
    return dict(
        n1w=w["norm1_w"][:, None], cw=jnp.pad(conv_w_full, ((0, 0), (0, 8 - CONV_WIDTH), (0, 0))),
        cb=w["conv_b"][:, None], dtb=_pad_lanes(w["dt_bias"]), alog=_pad_lanes(w["a_log"]), dsk=_pad_lanes(w["d_skip"]),
        snw=w["ssd_norm_w"][:, None], wblk=_block_diag(w["pool_w"]), pb=w["pool_b"].reshape(-1, 1, POOL_WIDTH),
        ps=w["pool_scale"][:, None], n2w=w["norm2_w"][:, None])


MIX = ("w_in", "w_out")
FFN = ("w_gate", "w_up", "w_down")


def layer_params(small, l):
    return {k: v[l] for k, v in small.items()}


def mix_weights(whole):
    return _perm_rows(whole[0]), whole[1]


def _slabs(g):
    return g.reshape(N_DEV, -1, g.shape[-1])


def _layer_fwd(x, p, mix, ffn=None, ffn_shards=None, next_mix_shards=None):
    ici = [gather_ici([sh]) for sh in ffn_shards] if ffn_shards is not None else [None] * len(FFN)
    (z, xbc, qkv, pp, dtr, h1), (blk_g,) = inproj_fwd(x, p["n1w"], mix[0], [ici[0]])
    (u,), _ = conv_fwd(xbc, p["cw"], p["cb"])
    (y_ssd, st), (blk_u,) = ssd_fwd(u, z, dtr, p["dtb"], p["alog"], p["dsk"], p["snw"], [ici[1]])
    (o, tot, swept), (blk_d,) = sb_fwd(qkv, [ici[2]])
    blocks = blk_g + blk_u + blk_d if ffn_shards is not None else None
    yp = pool_fwd(pp, p["wblk"], p["pb"], p["ps"])
    (x_mid, ycat), (blocks,) = outproj_fwd([y_ssd, o, yp], mix[1], x,
                                           [gather_d2d(blocks) if blocks is not None else None])
    if blocks is not None:
        ffn = gathered(blocks)
    (x_out, g, uu), (nxt,) = ffn_fwd(x_mid, p["n2w"], *ffn,
                                     [gather_ici(next_mix_shards) if next_mix_shards is not None else None])
    sv = dict(x=x, z=z, xbc=xbc, qkv=qkv, pp=pp, dtr=dtr, h1=h1, u=u, st=st, tot=tot, swept=swept, ycat=ycat,
              x_mid=x_mid, g=g, uu=uu, w_in=mix[0], w_out=mix[1], wg=ffn[0], wu=ffn[1], wd=ffn[2])
    return x_out, sv, nxt


def _layer_bwd(dxo, sv, p, pending_mix=None, exchange=False, into_ffn=(), into_mix=()):
    (dx_mid, dn2w, a, dg, du, h2), (sib,) = ffn_bwd(
        dxo, sv["x_mid"], sv["g"], sv["uu"], p["n2w"], sv["wg"], sv["wu"], sv["wd"],
        [scatter_d2d(pending_mix) if pending_mix is not None else None])
    sums_mix = chip_sums(pending_mix, sib) if pending_mix is not None else None
    gr = dict(norm2_w=dn2w[0], w_gate=mm_tn(dg, h2), w_up=mm_tn(du, h2), w_down=mm_tn(a, dxo))
    parts = [_slabs(gr[k]) for k in FFN] if exchange else None
    (dycat,), (sib,) = outproj_bwd(dx_mid, sv["w_out"], [scatter_d2d(parts) if exchange else None])
    sums_ffn = chip_sums(parts, sib) if exchange else None
    gr["w_out"] = mm_tn(sv["ycat"], dx_mid)
    (dp, dwblk, dpb, dps), _ = pool_bwd(sv["pp"], p["wblk"], p["pb"], p["ps"], dycat,
                                        (SSD_WIDTH + SB_WIDTH) // POOL_WIDTH)
    n = POOL_GROUP_DIM
    gr["pool_w"] = jnp.stack([dwblk[i * n:(i + 1) * n, i * n:(i + 1) * n] for i in range(len(POOL_WINDOWS))])
    gr["pool_b"] = dpb.reshape(len(POOL_WINDOWS), n)
    gr["pool_scale"] = dps[0]
    (dq, dk, dv), (got_ud,) = sb_bwd(sv["qkv"], sv["tot"], sv["swept"], dycat, SSD_WIDTH // LANE,
                                     [scatter_ici(sums_ffn[1:]) if exchange else None])
    (du_, dz, ddtr, ddtb, dalog, ddsk, dsnw), (got_mix,) = ssd_bwd(
        sv["u"], sv["z"], sv["dtr"], sv["st"], dycat, p["dtb"], p["alog"], p["dsk"], p["snw"],
        [scatter_ici(sums_mix) if sums_mix is not None else None])
    done_mix = device_sums(sums_mix, got_mix, *into_mix) if sums_mix is not None else None
    gr.update(dt_bias=ddtb[0, :SSD_HEADS], a_log=dalog[0, :SSD_HEADS], d_skip=ddsk[0, :SSD_HEADS], ssd_norm_w=dsnw[0])
    (dxbc, dcw, dcb), (got_g,) = conv_bwd(sv["xbc"], p["cw"], p["cb"], du_,
                                          [scatter_ici(sums_ffn[:1]) if exchange else None])
    done_ffn = device_sums(sums_ffn, got_g + got_ud, *into_ffn) if exchange else None
    gr.update(conv_w=dcw[:CONV_WIDTH], conv_b=dcb[0])
    dx, dn1w, dproj = inproj_bwd([dz, dxbc, dq, dk, dv, dp, ddtr], sv["w_in"], sv["x"], p["n1w"], dx_mid)
    gr.update(norm1_w=dn1w[0], w_in=_unperm_rows(mm_tn(dproj, sv["h1"])))
    return dx, gr, done_ffn, done_mix


def local_step(x, tgt, params, weights, final_w):
    saved = []
    for p, (mix, ffn) in zip(params, weights):
        x, sv, _ = _layer_fwd(x, p, mix, ffn)
        saved.append(sv)
    loss, dx, dfw = head_loss(x, final_w[None], tgt)
    grads = []
    for p, sv in zip(reversed(params), reversed(saved)):
        dx, gr, _, _ = _layer_bwd(dx, sv, p)
        grads.append(gr)
    grads.reverse()
    return loss, dx, dfw[0], grads


WEIGHTS = ("norm1_w", "w_in", "conv_w", "conv_b", "dt_bias", "a_log", "d_skip", "ssd_norm_w", "pool_w", "pool_b",
           "pool_scale", "w_out", "norm2_w", "w_gate", "w_up", "w_down", "final_norm_w")
COL_SHARDED = ("w_in", "w_gate", "w_up")
ROW_SHARDED = ("w_out", "w_down")
SMALL = tuple(k for k in WEIGHTS if k not in COL_SHARDED + ROW_SHARDED)


def kernel(x, norm1_w, w_in, conv_w, conv_b, dt_bias, a_log, d_skip, ssd_norm_w, pool_w, pool_b, pool_scale, w_out, norm2_w, w_gate, w_up, w_down, final_norm_w, loss_target, m_norm1_w, m_w_in, m_conv_w, m_conv_b, m_dt_bias, m_a_log, m_d_skip, m_ssd_norm_w, m_pool_w, m_pool_b, m_pool_scale, m_w_out, m_norm2_w, m_w_gate, m_w_up, m_w_down, m_final_norm_w, v_norm1_w, v_w_in, v_conv_w, v_conv_b, v_dt_bias, v_a_log, v_d_skip, v_ssd_norm_w, v_pool_w, v_pool_b, v_pool_scale, v_w_out, v_norm2_w, v_w_gate, v_w_up, v_w_down, v_final_norm_w):
    args = dict(locals())
    w = {k: args[k] for k in WEIGHTS}
    m = {k: args["m_" + k] for k in WEIGHTS}
    v = {k: args["v_" + k] for k in WEIGHTS}
    depth = w_in.shape[0]
    dev = _dev(_me())
    n_cw = conv_w.shape[-1]

    shards = {k: (jnp.swapaxes(w[k], 1, 2) if k in COL_SHARDED else w[k]).astype(BF16) for k in MIX + FFN}
    whole = all_gather([jnp.swapaxes(conv_w, 0, 2).reshape(n_cw, -1)] + [shards[k][0] for k in MIX])
    conv_w_full = jnp.swapaxes(whole[0].reshape(N_DEV * n_cw, CONV_WIDTH, depth), 0, 2)
    small = small_params(w, conv_w_full)
    xs = x[0]
    params, saved = [layer_params(small, l) for l in range(depth)], []
    mix = mix_weights(whole[1:])
    for l in range(depth):
        xs, sv, nxt = _layer_fwd(xs, params[l], mix, ffn_shards=[shards[k][l] for k in FFN],
                                 next_mix_shards=[shards[k][l + 1] for k in MIX] if l + 1 < depth else None)
        saved.append(sv)
        if nxt is not None:
            mix = mix_weights(gathered(gather_d2d(nxt).run("gather_d2d")))
    loss, dx, dfw = head_loss(xs, final_norm_w[None], loss_target[0])
    layer_grads = [None] * depth
    native = {k: lax.empty((depth,) + shards[k].shape[1:], F32) for k in MIX + FFN}
    pending = None
    for l in reversed(range(depth)):
        dx, layer_grads[l], done_ffn, done_mix = _layer_bwd(
            dx, saved[l], params[l], pending, exchange=True, into_ffn=([native[k] for k in FFN], l),
            into_mix=([native[k] for k in MIX], l + 1))
        native.update(zip(FFN, done_ffn))
        if pending is not None:
            native.update(zip(MIX, done_mix))
        pending = [_slabs(layer_grads[l][k]) for k in MIX]
    native.update(zip(MIX, reduce_scatter(pending, [native[k] for k in MIX], 0)))
    grads = {k: jnp.swapaxes(native[k], 1, 2) if k in COL_SHARDED else native[k] for k in MIX + FFN}
    layered = [k for k in SMALL if k != "final_norm_w"]
    small_shapes = [(1, LANE)] + [(depth,) + layer_grads[0][k].shape for k in layered] + [dfw[0].shape]
    packed = _pack([loss] + [[layer_grads[l][k] for l in range(depth)] for k in layered] + [dfw[0]])
    summed = _unpack(all_reduce_small(packed), small_shapes)
    loss = summed[0][0, 0]
    grads.update(zip(layered + ["final_norm_w"], summed[1:]))
    grads["conv_w"] = lax.dynamic_slice_in_dim(grads["conv_w"], dev * n_cw, n_cw, axis=2)

    delta, new_m, new_v = {}, {}, {}
    for k in MIX + FFN:
        if k in ("w_gate", "w_up"):
            wt, mt, vt = (jnp.swapaxes(t, 1, 2) for t in (w[k], m[k], v[k]))
            delta[k], new_m[k], new_v[k] = [jnp.swapaxes(o, 1, 2) for o in adamw(wt, native[k], mt, vt)]
        else:
            delta[k], new_m[k], new_v[k] = adamw(w[k], grads[k], m[k], v[k])
    two_d = lambda a: a.reshape(1, -1) if a.ndim == 1 else a
    outs = adamw_small(*[[two_d(t[k]) for k in SMALL] for t in (w, grads, m, v)])
    for dst, arrs in zip((delta, new_m, new_v), outs):
        dst.update({k: a.reshape(w[k].shape) for k, a in zip(SMALL, arrs)})
    return (loss, dx[None], *[grads[k] for k in WEIGHTS], *[delta[k] for k in WEIGHTS],
            *[new_m[k] for k in WEIGHTS], *[new_v[k] for k in WEIGHTS])
```

```python
import functools

import jax
import jax.numpy as jnp
from jax import lax
from jax.experimental import pallas as pl
from jax.experimental.pallas import tpu as pltpu

F32 = jnp.float32
BF16 = jnp.bfloat16
HIGHEST = lax.Precision.HIGHEST
MESH = pl.DeviceIdType.MESH

EPS = 1e-6
D_MODEL = 1024
SSD_WIDTH = 512
SSD_HEADS = 8
HEAD_DIM = 64
D_STATE = 128
CONV_WIDTH = 4
CONV_DIM = 1024
SB_WIDTH = 256
POOL_WIDTH = 256
POOL_WINDOWS = (2, 4, 8, 16)
D_IN_PROJ = 2568
D_FF = 2816
N_DEV = 8
DEPTH = 4
SEG = (512, 1024, 768, 256, 128)
D_IN_PAD = sum(SEG)
DT_LO, DT_HI = 1536, 1544

LANE = 128
BLK = 128
ROW_TILE = 256
VMEM_LIMIT = 56 * 2**20

ADAM_LR, ADAM_B1, ADAM_B2, ADAM_EPS, ADAM_WD, ADAM_STEP = 0.001, 0.9, 0.999, 1e-08, 0.01, 10


def _params(n_axes=1, vmem=None):
    return pltpu.CompilerParams(dimension_semantics=("arbitrary",) * n_axes, vmem_limit_bytes=vmem)


def _dot(a, b, dims, exact=False):
    if exact:
        return lax.dot_general(a.astype(F32), b.astype(F32), (dims, ((), ())), precision=HIGHEST,
                               preferred_element_type=F32)
    return lax.dot_general(a.astype(BF16), b.astype(BF16), (dims, ((), ())), preferred_element_type=F32)


def dot_nn(a, b, exact=False):
    return _dot(a, b, ((1,), (0,)), exact)


def dot_nt(a, b, exact=False):
    return _dot(a, b, ((1,), (1,)), exact)


def dot_tn(a, b, exact=False):
    return _dot(a, b, ((0,), (0,)), exact)


def _iota(shape, axis):
    return lax.broadcasted_iota(jnp.int32, shape, axis)


def _lane_col(x, h):
    return jnp.sum(jnp.where(_iota(x.shape, 1) == h, x, 0.0), axis=1, keepdims=True)


def _sub_row(x, h):
    return jnp.sum(jnp.where(_iota(x.shape, 0) == h, x, 0.0), axis=0, keepdims=True)


def _sigmoid(x):
    return 1.0 / (1.0 + jnp.exp(-x))


def _rms_fwd(x, w):
    r = lax.rsqrt(jnp.mean(x * x, axis=-1, keepdims=True) + EPS)
    return x * r * w


def _rms_bwd(x, w, dy):
    r = lax.rsqrt(jnp.mean(x * x, axis=-1, keepdims=True) + EPS)
    xh = x * r
    dxh = dy * w
    dx = r * (dxh - xh * jnp.mean(dxh * xh, axis=-1, keepdims=True))
    return dx, jnp.sum(dy * xh, axis=0, keepdims=True)


def _acc(ref, first, val):
    @pl.when(first)
    def _():
        ref[...] = val

    @pl.when(jnp.logical_not(first))
    def _():
        ref[...] += val


def _row_spec(tm, n):
    return pl.BlockSpec((tm, n), lambda i: (i, 0))


def _full_spec(shape):
    return pl.BlockSpec(shape, lambda *_: (0,) * len(shape))


def inproj_fwd(x, nw, w, sides=()):
    s, d = x.shape
    tm = min(ROW_TILE, s)

    def body(x_ref, nw_ref, w_ref, z_ref, xbc_ref, qkv_ref, p_ref, dt_ref, h_ref):
        h = _rms_fwd(x_ref[...], nw_ref[...]).astype(BF16)
        h_ref[...] = h
        lo = 0
        for ref, n in zip((z_ref, xbc_ref, qkv_ref, p_ref, dt_ref), SEG):
            ref[...] = dot_nt(h, w_ref[lo:lo + n, :])
            lo += n

    return hosted_call(
        body, sides, name="inproj_fwd", grid=(s // tm,),
        in_specs=[_row_spec(tm, d), _full_spec((1, d)), _full_spec(w.shape)],
        out_specs=[_row_spec(tm, n) for n in SEG] + [_row_spec(tm, d)],
        out_shape=[jax.ShapeDtypeStruct((s, n), F32) for n in SEG] + [jax.ShapeDtypeStruct((s, d), BF16)],
        scratch_shapes=[], compiler_params=_params(1, VMEM_LIMIT), operands=(x, nw, w))


def inproj_bwd(pieces, w, x, nw, dres):
    s, d = x.shape
    tm = min(ROW_TILE, s)
    n_p = len(pieces)
    widths = [p.shape[1] for p in pieces]

    def body(*refs):
        w_ref, x_ref, nw_ref, dres_ref, dx_ref, dnw_ref, dp_ref = refs[n_p:]
        dh = None
        lo = 0
        for ref, n in zip(refs[:n_p], widths):
            piece = ref[...].astype(BF16)
            dp_ref[:, lo:lo + n] = piece
            part = dot_nn(piece, w_ref[lo:lo + n, :])
            dh = part if dh is None else dh + part
            lo += n
        dx, dnw = _rms_bwd(x_ref[...], nw_ref[...], dh)
        dx_ref[...] = dres_ref[...] + dx
        _acc(dnw_ref, pl.program_id(0) == 0, dnw)

    return pl.pallas_call(
        body, name="inproj_bwd", grid=(s // tm,),
        in_specs=[_row_spec(tm, n) for n in widths] + [_full_spec(w.shape), _row_spec(tm, d), _full_spec((1, d)),
                                                       _row_spec(tm, d)],
        out_specs=[_row_spec(tm, d), _full_spec((1, d)), _row_spec(tm, sum(widths))],
        out_shape=[jax.ShapeDtypeStruct((s, d), F32), jax.ShapeDtypeStruct((1, d), F32),
                   jax.ShapeDtypeStruct((s, sum(widths)), BF16)],
        compiler_params=_params(1, VMEM_LIMIT),
    )(*pieces, w, x, nw, dres)


def outproj_fwd(pieces, w, res, sides=()):
    s, d = res.shape
    tm = min(ROW_TILE, s)
    n_p = len(pieces)
    widths = [p.shape[1] for p in pieces]

    def body(*refs):
        w_ref, r_ref, o_ref, y_ref = refs[n_p:]
        acc = r_ref[...]
        lo = 0
        for ref, n in zip(refs[:n_p], widths):
            piece = ref[...].astype(BF16)
            y_ref[:, lo:lo + n] = piece
            acc = acc + dot_nn(piece, w_ref[lo:lo + n, :])
            lo += n
        o_ref[...] = acc

    return hosted_call(
        body, sides, name="outproj_fwd", grid=(s // tm,),
        in_specs=[_row_spec(tm, n) for n in widths] + [_full_spec(w.shape), _row_spec(tm, d)],
        out_specs=[_row_spec(tm, d), _row_spec(tm, sum(widths))],
        out_shape=[jax.ShapeDtypeStruct((s, d), F32), jax.ShapeDtypeStruct((s, sum(widths)), BF16)],
        scratch_shapes=[], compiler_params=_params(1, VMEM_LIMIT), operands=(*pieces, w, res))


def outproj_bwd(dx, w, sides=()):
    s, d = dx.shape
    tm = min(ROW_TILE, s)

    def body(dx_ref, w_ref, o_ref):
        o_ref[...] = dot_nt(dx_ref[...], w_ref[...])

    return hosted_call(
        body, sides, name="outproj_bwd", grid=(s // tm,),
        in_specs=[_row_spec(tm, d), _full_spec(w.shape)],
        out_specs=[_row_spec(tm, w.shape[0])], out_shape=[jax.ShapeDtypeStruct((s, w.shape[0]), F32)],
        scratch_shapes=[], compiler_params=_params(1, VMEM_LIMIT), operands=(dx, w))


def ffn_fwd(x, nw, wg, wu, wd, sides=()):
    s, d = x.shape
    f = wg.shape[0]
    tm = min(ROW_TILE, s)

    def body(x_ref, nw_ref, wg_ref, wu_ref, wd_ref, o_ref, g_ref, u_ref):
        xv = x_ref[...]
        h = _rms_fwd(xv, nw_ref[...]).astype(BF16)
        g = dot_nt(h, wg_ref[...])
        u = dot_nt(h, wu_ref[...])
        g_ref[...] = g.astype(BF16)
        u_ref[...] = u.astype(BF16)
        o_ref[...] = xv + dot_nn(g * _sigmoid(g) * u, wd_ref[...])

    return hosted_call(
        body, sides, name="ffn_fwd", grid=(s // tm,),
        in_specs=[_row_spec(tm, d), _full_spec((1, d)), _full_spec(wg.shape), _full_spec(wu.shape),
                  _full_spec(wd.shape)],
        out_specs=[_row_spec(tm, d), _row_spec(tm, f), _row_spec(tm, f)],
        out_shape=[jax.ShapeDtypeStruct((s, d), F32), jax.ShapeDtypeStruct((s, f), BF16),
                   jax.ShapeDtypeStruct((s, f), BF16)],
        scratch_shapes=[], compiler_params=_params(1, VMEM_LIMIT), operands=(x, nw, wg, wu, wd))


def ffn_bwd(dxo, x, g, u, nw, wg, wu, wd, sides=()):
    s, d = x.shape
    f = wg.shape[0]
    tm = min(ROW_TILE, s)

    def body(dxo_ref, x_ref, g_ref, u_ref, nw_ref, wg_ref, wu_ref, wd_ref, dx_ref, dnw_ref, a_ref, dg_ref,
             du_ref, h_ref):
        dxo_v = dxo_ref[...]
        xv = x_ref[...]
        da = dot_nt(dxo_v, wd_ref[...])
        gv = g_ref[...].astype(F32)
        uv = u_ref[...].astype(F32)
        sg = _sigmoid(gv)
        sl = gv * sg
        a_ref[...] = (sl * uv).astype(BF16)
        dg = (da * uv * (sg * (1.0 + gv * (1.0 - sg)))).astype(BF16)
        du = (da * sl).astype(BF16)
        dg_ref[...] = dg
        du_ref[...] = du
        dh = dot_nn(dg, wg_ref[...]) + dot_nn(du, wu_ref[...])
        h_ref[...] = _rms_fwd(xv, nw_ref[...]).astype(BF16)
        dx, dnw = _rms_bwd(xv, nw_ref[...], dh)
        dx_ref[...] = dxo_v + dx
        _acc(dnw_ref, pl.program_id(0) == 0, dnw)

    return hosted_call(
        body, sides, name="ffn_bwd", grid=(s // tm,),
        in_specs=[_row_spec(tm, d), _row_spec(tm, d), _row_spec(tm, f), _row_spec(tm, f), _full_spec((1, d)),
                  _full_spec(wg.shape), _full_spec(wu.shape), _full_spec(wd.shape)],
        out_specs=[_row_spec(tm, d), _full_spec((1, d)), _row_spec(tm, f), _row_spec(tm, f), _row_spec(tm, f),
                   _row_spec(tm, d)],
        out_shape=[jax.ShapeDtypeStruct((s, d), F32), jax.ShapeDtypeStruct((1, d), F32),
                   jax.ShapeDtypeStruct((s, f), BF16), jax.ShapeDtypeStruct((s, f), BF16),
                   jax.ShapeDtypeStruct((s, f), BF16), jax.ShapeDtypeStruct((s, d), BF16)],
        scratch_shapes=[], compiler_params=_params(1, VMEM_LIMIT), operands=(dxo, x, g, u, nw, wg, wu, wd))


def _tile(n, cap=256):
    best = LANE
    for t in range(LANE, cap + 1, LANE):
        if n % t == 0:
            best = t
    return best


def mm_tn(a, b):
    many = isinstance(a, (list, tuple))
    a_list = list(a) if many else [a]
    n_a = len(a_list)
    s, k = a_list[0].shape
    n = b.shape[1]
    tk = _tile(k)

    def body(*refs):
        b_val = refs[n_a][...]
        for a_ref, o_ref in zip(refs[:n_a], refs[n_a + 1:]):
            o_ref[...] = dot_nn(a_ref[...].astype(BF16).T, b_val).astype(BF16)

    outs = pl.pallas_call(
        body, name="mm_tn", grid=(k // tk,),
        in_specs=[pl.BlockSpec((s, tk), lambda i: (0, i))] * n_a + [_full_spec((s, n))],
        out_specs=[pl.BlockSpec((tk, n), lambda i: (i, 0))] * n_a,
        out_shape=[jax.ShapeDtypeStruct((k, n), BF16)] * n_a, compiler_params=_params(1, VMEM_LIMIT),
    )(*a_list, b)
    return outs if many else outs[0]


def head_loss(x, fw, tgt):
    s, d = x.shape
    tm = min(ROW_TILE, s)

    def body(x_ref, fw_ref, t_ref, loss_ref, dx_ref, dfw_ref):
        xv = x_ref[...]
        err = _rms_fwd(xv, fw_ref[...]) - t_ref[...]
        part = jnp.zeros((1, LANE), F32) + 0.5 * jnp.sum(err * err) / d
        dx, dfw = _rms_bwd(xv, fw_ref[...], err / d)
        dx_ref[...] = dx
        first = pl.program_id(0) == 0
        _acc(loss_ref, first, part)
        _acc(dfw_ref, first, dfw)

    return pl.pallas_call(
        body, name="head_loss", grid=(s // tm,),
        in_specs=[_row_spec(tm, d), _full_spec((1, d)), _row_spec(tm, d)],
        out_specs=[_full_spec((1, LANE)), _row_spec(tm, d), _full_spec((1, d))],
        out_shape=[jax.ShapeDtypeStruct((1, LANE), F32), jax.ShapeDtypeStruct((s, d), F32),
                   jax.ShapeDtypeStruct((1, d), F32)],
        compiler_params=_params(1),
    )(x, fw, tgt)


HALO = 8


def _conv_pre(ext, cw_ref, cb_ref):
    shifted = [pltpu.roll(ext, CONV_WIDTH - 1 - i, 0)[HALO:] if i < CONV_WIDTH - 1 else ext[HALO:]
               for i in range(CONV_WIDTH)]
    acc = cb_ref[...] + sum(cw_ref[i:i + 1, :] * shifted[i] for i in range(CONV_WIDTH))
    return acc, shifted


def _halo_spec(n, block_of_step):
    return pl.BlockSpec((HALO, n), lambda i: (jnp.maximum(block_of_step(i) * (BLK // HALO) - 1, 0), 0))


def conv_fwd(xbc, cw, cb, sides=()):
    s, n = xbc.shape

    def body(cur_ref, prev_ref, cw_ref, cb_ref, o_ref):
        prev = jnp.where(pl.program_id(0) > 0, prev_ref[...], 0.0)
        acc, _ = _conv_pre(jnp.concatenate([prev, cur_ref[...]], axis=0), cw_ref, cb_ref)
        o_ref[...] = acc * _sigmoid(acc)

    return hosted_call(
        body, sides, name="conv_fwd", grid=(s // BLK,),
        in_specs=[pl.BlockSpec((BLK, n), lambda c: (c, 0)), _halo_spec(n, lambda c: c),
                  _full_spec(cw.shape), _full_spec((1, n))],
        out_specs=[pl.BlockSpec((BLK, n), lambda c: (c, 0))], out_shape=[jax.ShapeDtypeStruct((s, n), F32)],
        scratch_shapes=[], compiler_params=_params(1), operands=(xbc, xbc, cw, cb))


def conv_bwd(xbc, cw, cb, du, sides=()):
    s, n = xbc.shape
    nb = s // BLK

    def body(cur_ref, prev_ref, cw_ref, cb_ref, du_ref, dx_ref, dcw_ref, dcb_ref, nxt_ref):
        i = pl.program_id(0)
        c = nb - 1 - i
        prev = jnp.where(c > 0, prev_ref[...], 0.0)
        acc, shifted = _conv_pre(jnp.concatenate([prev, cur_ref[...]], axis=0), cw_ref, cb_ref)
        sg = _sigmoid(acc)
        dacc = du_ref[...] * (sg * (1.0 + acc * (1.0 - sg)))

        @pl.when(i == 0)
        def _():
            nxt_ref[...] = jnp.zeros_like(nxt_ref)
            dcw_ref[...] = jnp.zeros_like(dcw_ref)
            dcb_ref[...] = jnp.zeros_like(dcb_ref)

        dcb_ref[...] += jnp.sum(dacc, axis=0, keepdims=True)
        for t in range(CONV_WIDTH):
            dcw_ref[t:t + 1, :] += jnp.sum(dacc * shifted[t], axis=0, keepdims=True)
        ext = jnp.concatenate([dacc, nxt_ref[...]], axis=0)
        dx = cw_ref[CONV_WIDTH - 1:CONV_WIDTH, :] * dacc
        for t in range(CONV_WIDTH - 1):
            dx += cw_ref[t:t + 1, :] * pltpu.roll(ext, BLK + HALO - (CONV_WIDTH - 1 - t), 0)[:BLK]
        dx_ref[...] = dx.astype(dx_ref.dtype)
        nxt_ref[...] = dacc[:HALO]

    rev = lambda i: (nb - 1 - i, 0)
    return hosted_call(
        body, sides, name="conv_bwd", grid=(nb,),
        in_specs=[pl.BlockSpec((BLK, n), rev), _halo_spec(n, lambda i: nb - 1 - i),
                  _full_spec(cw.shape), _full_spec((1, n)), pl.BlockSpec((BLK, n), rev)],
        out_specs=[pl.BlockSpec((BLK, n), rev), _full_spec((8, n)), _full_spec((1, n))],
        out_shape=[jax.ShapeDtypeStruct((s, n), BF16), jax.ShapeDtypeStruct((8, n), F32),
                   jax.ShapeDtypeStruct((1, n), F32)],
        scratch_shapes=[pltpu.VMEM((HALO, n), F32)],
        compiler_params=_params(1), operands=(xbc, xbc, cw, cb, du))


N_PAIR = SSD_HEADS // 2
B_LO = SSD_WIDTH
C_LO = SSD_WIDTH + 2 * D_STATE


def _softplus(x):
    return jnp.maximum(x, 0.0) + jnp.log(1.0 + jnp.exp(-jnp.abs(x)))


def _ssd_chunk(u_ref, dt_ref, dtb_ref, alog_ref):
    shape = (BLK, BLK)
    tri = _iota(shape, 1) <= _iota(shape, 0)
    pre = dt_ref[...] + dtb_ref[...]
    dt = _softplus(pre)
    a = -jnp.exp(alog_ref[...])
    acum = dot_nn(tri.astype(F32), dt * a, exact=True)
    acum_t = acum.T
    last = _sub_row(acum, BLK - 1)
    heads = []
    for h in range(SSD_HEADS):
        col = _lane_col(acum, h)
        seg = jnp.where(tri, col - _sub_row(acum_t, h), -1e30)
        heads.append(dict(col=col, dm=jnp.exp(seg), dt=_lane_col(dt, h), last=_lane_col(last, h)))
    return tri, pre, dt, a, heads


def _pair_mix(lo_mask, v0, v1):
    return jnp.where(lo_mask, v0, v1)


def ssd_fwd(u, z, dtr, dtb, alog, dsk, nw, sides=()):
    s = u.shape[0]
    nc = s // BLK

    def body(u_ref, z_ref, dt_ref, dtb_ref, alog_ref, dsk_ref, nw_ref, y_ref, st_ref, s_ref):
        @pl.when(pl.program_id(0) == 0)
        def _():
            s_ref[...] = jnp.zeros_like(s_ref)

        _, _, _, _, heads = _ssd_chunk(u_ref, dt_ref, dtb_ref, alog_ref)
        lo_lane = _iota((BLK, LANE), 1) < HEAD_DIM
        lo_sub = _iota((BLK, LANE), 0) < HEAD_DIM
        ys = []
        for p in range(N_PAIR):
            g = p // 2
            h0, h1 = heads[2 * p], heads[2 * p + 1]
            bg = u_ref[:, B_LO + g * D_STATE:B_LO + (g + 1) * D_STATE]
            cg = u_ref[:, C_LO + g * D_STATE:C_LO + (g + 1) * D_STATE]
            xs = u_ref[:, p * LANE:(p + 1) * LANE]
            xp = xs * _pair_mix(lo_lane, h0["dt"], h1["dt"])
            gm = dot_nt(cg, bg)
            yd = _pair_mix(lo_lane, dot_nn(gm * h0["dm"], xp), dot_nn(gm * h1["dm"], xp))
            sp = s_ref[p]
            st_ref[0, p] = sp
            yo = _pair_mix(lo_lane, jnp.exp(h0["col"]), jnp.exp(h1["col"])) * dot_nt(cg, sp)
            dskp = _pair_mix(lo_lane, _lane_col(dsk_ref[...], 2 * p), _lane_col(dsk_ref[...], 2 * p + 1))
            ys.append(yd + yo + xs * dskp)
            wp = _pair_mix(lo_lane, jnp.exp(h0["last"] - h0["col"]), jnp.exp(h1["last"] - h1["col"]))
            el = _pair_mix(lo_sub, jnp.exp(h0["last"]), jnp.exp(h1["last"]))
            s_ref[p] = el * sp + dot_tn(wp * xp, bg)
        y = jnp.concatenate(ys, axis=1)
        zv = z_ref[...]
        y_ref[...] = _rms_fwd(y * zv * _sigmoid(zv), nw_ref[...])

    vec = _full_spec((1, LANE))
    return hosted_call(
        body, sides, name="ssd_fwd", grid=(nc,),
        in_specs=[_row_spec(BLK, CONV_DIM), _row_spec(BLK, SSD_WIDTH), _row_spec(BLK, LANE), vec, vec, vec,
                  _full_spec((1, SSD_WIDTH))],
        out_specs=[_row_spec(BLK, SSD_WIDTH), pl.BlockSpec((1, N_PAIR, LANE, D_STATE), lambda c: (c, 0, 0, 0))],
        out_shape=[jax.ShapeDtypeStruct((s, SSD_WIDTH), F32), jax.ShapeDtypeStruct((nc, N_PAIR, LANE, D_STATE), F32)],
        scratch_shapes=[pltpu.VMEM((N_PAIR, LANE, D_STATE), F32)],
        compiler_params=_params(1), operands=(u, z, dtr, dtb, alog, dsk, nw))


def ssd_bwd(u, z, dtr, st, dyo, dtb, alog, dsk, nw, sides=()):
    s = u.shape[0]
    nc = s // BLK

    def body(u_ref, z_ref, dt_ref, st_ref, dyo_ref, dtb_ref, alog_ref, dsk_ref, nw_ref,
             du_ref, dz_ref, ddt_ref, ddtb_ref, dalog_ref, ddsk_ref, dnw_ref, ds_ref):
        first = pl.program_id(0) == 0

        @pl.when(first)
        def _():
            ds_ref[...] = jnp.zeros_like(ds_ref)

        tri, pre, dt, a, heads = _ssd_chunk(u_ref, dt_ref, dtb_ref, alog_ref)
        shape = (BLK, LANE)
        lane = _iota(shape, 1)
        lo_lane = lane < HEAD_DIM
        lo_sub = _iota(shape, 0) < HEAD_DIM
        pairs = []
        ys = []
        for p in range(N_PAIR):
            g = p // 2
            h0, h1 = heads[2 * p], heads[2 * p + 1]
            bg = u_ref[:, B_LO + g * D_STATE:B_LO + (g + 1) * D_STATE]
            cg = u_ref[:, C_LO + g * D_STATE:C_LO + (g + 1) * D_STATE]
            xs = u_ref[:, p * LANE:(p + 1) * LANE]
            dtp = _pair_mix(lo_lane, h0["dt"], h1["dt"])
            xp = xs * dtp
            gm = dot_nt(cg, bg)
            m0, m1 = gm * h0["dm"], gm * h1["dm"]
            sp = st_ref[0, p]
            eap = _pair_mix(lo_lane, jnp.exp(h0["col"]), jnp.exp(h1["col"]))
            yo = eap * dot_nt(cg, sp)
            dskp = _pair_mix(lo_lane, _lane_col(dsk_ref[...], 2 * p), _lane_col(dsk_ref[...], 2 * p + 1))
            ys.append(_pair_mix(lo_lane, dot_nn(m0, xp), dot_nn(m1, xp)) + yo + xs * dskp)
            pairs.append(dict(bg=bg, cg=cg, xs=xs, dtp=dtp, xp=xp, gm=gm, m=(m0, m1), sp=sp, eap=eap, yo=yo, dskp=dskp))
        y = jnp.concatenate(ys, axis=1)
        zv = z_ref[...]
        sz = _sigmoid(zv)
        gate = zv * sz
        dyg, dnw = _rms_bwd(y * gate, nw_ref[...], dyo_ref[...])
        _acc(dnw_ref, first, dnw)
        dy = dyg * gate
        dz_ref[...] = (dyg * y * (sz * (1.0 + zv * (1.0 - sz)))).astype(dz_ref.dtype)

        zeros = jnp.zeros(shape, F32)
        dacum_col = zeros
        dacum_row = zeros
        ddt = zeros
        ddsk = jnp.zeros((1, LANE), F32)
        dlast = jnp.zeros((1, LANE), F32)
        head_row = _iota((1, LANE), 1)
        sub = _iota(shape, 0)
        db = [zeros, zeros]
        dc = [zeros, zeros]
        for p in range(N_PAIR):
            g = p // 2
            q = pairs[p]
            dyp = dy[:, p * LANE:(p + 1) * LANE]
            dsn = ds_ref[p]
            t = dyp * q["xs"]
            dxs = dyp * q["dskp"]
            dcs = dyp * q["eap"]
            dc[g] = dc[g] + dot_nn(dcs, q["sp"])
            dsp = dot_tn(dcs, q["cg"])
            dea = dyp * q["yo"]
            elp = _pair_mix(lo_sub, jnp.exp(heads[2 * p]["last"]), jnp.exp(heads[2 * p + 1]["last"]))
            dsp = dsp + elp * dsn
            dels = dsn * q["sp"] * elp
            wp = _pair_mix(lo_lane, jnp.exp(heads[2 * p]["last"] - heads[2 * p]["col"]),
                           jnp.exp(heads[2 * p + 1]["last"] - heads[2 * p + 1]["col"]))
            dv = dot_nt(q["bg"], dsn)
            db[g] = db[g] + dot_nn(wp * q["xp"], dsn)
            dxp = dv * wp
            dwv = dv * q["xp"] * wp
            dgm = zeros
            for k in range(2):
                h = 2 * p + k
                mine = lo_lane if k == 0 else jnp.logical_not(lo_lane)
                mine_sub = lo_sub if k == 0 else jnp.logical_not(lo_sub)
                dyh = jnp.where(mine, dyp, 0.0)
                dm = dot_nt(dyh, q["xp"])
                dxp = dxp + dot_tn(q["m"][k], dyh)
                dgm = dgm + dm * heads[h]["dm"]
                e = dm * q["m"][k]
                onehot = lane == h
                dw_col = jnp.sum(jnp.where(mine, dwv, 0.0), axis=1, keepdims=True)
                col = (jnp.sum(e, axis=1, keepdims=True) + jnp.sum(jnp.where(mine, dea, 0.0), axis=1, keepdims=True)
                       - dw_col)
                dacum_col = dacum_col + jnp.where(onehot, col, 0.0)
                dacum_row = dacum_row - jnp.where(sub == h, jnp.sum(e, axis=0, keepdims=True), 0.0)
                dl = jnp.sum(dw_col) + jnp.sum(jnp.where(mine_sub, dels, 0.0))
                dlast = dlast + jnp.where(head_row == h, dl, 0.0)
                ddsk = ddsk + jnp.where(head_row == h, jnp.sum(jnp.where(mine, t, 0.0)), 0.0)
            dc[g] = dc[g] + dot_nn(dgm, q["bg"])
            db[g] = db[g] + dot_tn(dgm, q["cg"])
            dxs = dxs + dxp * q["dtp"]
            tt = dxp * q["xs"]
            for k in range(2):
                mine = lo_lane if k == 0 else jnp.logical_not(lo_lane)
                ddt = ddt + jnp.where(lane == 2 * p + k, jnp.sum(jnp.where(mine, tt, 0.0), axis=1, keepdims=True), 0.0)
            du_ref[:, p * LANE:(p + 1) * LANE] = dxs
            ds_ref[p] = dsp
        for g in range(2):
            du_ref[:, B_LO + g * D_STATE:B_LO + (g + 1) * D_STATE] = db[g]
            du_ref[:, C_LO + g * D_STATE:C_LO + (g + 1) * D_STATE] = dc[g]
        dacum = dacum_col + dacum_row.T + jnp.where(sub == BLK - 1, dlast, 0.0)
        dda = dot_tn(tri.astype(F32), dacum, exact=True)
        ddt = ddt + dda * a
        _acc(dalog_ref, first, jnp.sum(dda * dt, axis=0, keepdims=True) * a)
        dpre = ddt * _sigmoid(pre)
        ddt_ref[...] = dpre.astype(ddt_ref.dtype)
        _acc(ddtb_ref, first, jnp.sum(dpre, axis=0, keepdims=True))
        _acc(ddsk_ref, first, ddsk)

    rev = lambda i: (nc - 1 - i, 0)
    vec = _full_spec((1, LANE))
    rows = lambda n: pl.BlockSpec((BLK, n), rev)
    return hosted_call(
        body, sides, name="ssd_bwd", grid=(nc,),
        in_specs=[rows(CONV_DIM), rows(SSD_WIDTH), rows(LANE),
                  pl.BlockSpec((1, N_PAIR, LANE, D_STATE), lambda i: (nc - 1 - i, 0, 0, 0)), rows(SSD_WIDTH),
                  vec, vec, vec, _full_spec((1, SSD_WIDTH))],
        out_specs=[rows(CONV_DIM), rows(SSD_WIDTH), rows(LANE), vec, vec, vec, _full_spec((1, SSD_WIDTH))],
        out_shape=[jax.ShapeDtypeStruct((s, CONV_DIM), F32), jax.ShapeDtypeStruct((s, SSD_WIDTH), BF16),
                   jax.ShapeDtypeStruct((s, LANE), BF16)] + [jax.ShapeDtypeStruct((1, LANE), F32)] * 3
        + [jax.ShapeDtypeStruct((1, SSD_WIDTH), F32)],
        scratch_shapes=[pltpu.VMEM((N_PAIR, LANE, D_STATE), F32)],
        compiler_params=_params(1), operands=(u, z, dtr, st, dyo, dtb, alog, dsk, nw))


SB_PAIRS = SB_WIDTH // LANE
SB_SCALE = HEAD_DIM ** -0.5


SB_TQ = 256


def _sb_tq(s):
    return min(SB_TQ, s)


def _sb_stack(x):
    lo_lane = _iota(x.shape, 1) < HEAD_DIM
    return jnp.concatenate([jnp.where(lo_lane, x, 0.0), jnp.where(lo_lane, 0.0, x)], axis=0)


def _sb_unstack(x2):
    tq = x2.shape[0] // 2
    lo_lane = _iota((tq, LANE), 1) < HEAD_DIM
    return jnp.where(lo_lane, x2[:tq], x2[tq:])


def _sb_logits(q2, kj, row0, col0, masked):
    shape = (q2.shape[0], BLK)
    tq = shape[0] // 2
    z = dot_nt(q2, kj)
    t = jnp.log(1.0 + jnp.exp(-jnp.abs(z)))
    ls = jnp.minimum(z, 0.0) - t
    lk = jnp.minimum(-z, 0.0) - t
    if not masked:
        return None, ls, lk
    row = _iota(shape, 0)
    valid = (col0 + _iota(shape, 1)) < (row0 + jnp.where(row < tq, row, row - tq))
    return valid, ls, jnp.where(valid, lk, 0.0)


def _sb_where(valid, x):
    return x if valid is None else jnp.where(valid, x, 0.0)


def _sums(x, mask2, parts):
    acc = None
    rest = x
    for _ in range(parts):
        term = rest.astype(BF16)
        rest = rest - term.astype(F32)
        d = lax.dot_general(term, mask2, (((1,), (0,)), ((), ())), preferred_element_type=F32)
        acc = d if acc is None else acc + d
    return acc[:, :BLK], acc[:, BLK:]


def _mask2(cond):
    return jnp.concatenate([cond.astype(BF16), jnp.ones(cond.shape, BF16)], axis=1)


def _sb_specs(s):
    tq = _sb_tq(s)
    qspec = pl.BlockSpec((tq, LANE), lambda p, i: (i, p))
    kspec = pl.BlockSpec((s, LANE), lambda p, i: (0, SB_PAIRS + p))
    vspec = pl.BlockSpec((s, LANE), lambda p, i: (0, 2 * SB_PAIRS + p))
    return qspec, kspec, vspec


SB_FLOOR = -104.0


def sb_fwd(qkv, sides=()):
    s = qkv.shape[0]
    tq = _sb_tq(s)
    kpq = tq // BLK

    def body(q_ref, k_ref, v_ref, o_ref, t_ref, n_ref, acc_ref):
        qi = pl.program_id(1)
        q2 = _sb_stack(q_ref[...] * SB_SCALE).astype(BF16)
        later = _mask2(_iota((BLK, BLK), 0) > _iota((BLK, BLK), 1))
        acc_ref[...] = jnp.zeros_like(acc_ref)

        def step(j, r, masked):
            rows = pl.ds(pl.multiple_of(j * BLK, BLK), BLK)
            valid, ls, lk = _sb_logits(q2, k_ref[rows, :], qi * tq, j * BLK, masked)
            after, total = _sums(lk, later, 2)
            w = _sb_where(valid, jnp.exp(ls + r + after))
            acc_ref[...] += dot_nn(w, v_ref[rows, :])
            return r + total

        r = jnp.zeros((2 * tq, LANE), F32)
        for d in reversed(range(kpq)):
            r = step(kpq * qi + d, r, True)

        def tile(g, r):
            for d in reversed(range(kpq)):
                r = step(kpq * (qi - 1 - g) + d, r, False)
            return r

        n, r = lax.while_loop(lambda c: jnp.logical_and(c[0] < qi, jnp.max(c[1]) > SB_FLOOR),
                              lambda c: (c[0] + 1, tile(c[0], c[1])), (jnp.int32(0), r))
        o_ref[...] = _sb_unstack(acc_ref[...])
        t_ref[...] = jnp.concatenate([r[:tq], r[tq:]], axis=1)
        n_ref[...] = jnp.zeros(n_ref.shape, F32) + n.astype(F32)

    return hosted_call(
        body, sides, name="sb_fwd", grid=(SB_PAIRS, s // tq),
        in_specs=list(_sb_specs(s)),
        out_specs=[pl.BlockSpec((tq, LANE), lambda p, i: (i, p)), pl.BlockSpec((tq, 2 * LANE), lambda p, i: (i, p)),
                   pl.BlockSpec((None, None, 8, LANE), lambda p, i: (p, i, 0, 0))],
        out_shape=[jax.ShapeDtypeStruct((s, SB_WIDTH), F32), jax.ShapeDtypeStruct((s, 2 * SB_WIDTH), F32),
                   jax.ShapeDtypeStruct((SB_PAIRS, s // tq, 8, LANE), F32)],
        scratch_shapes=[pltpu.VMEM((2 * tq, LANE), F32)],
        compiler_params=_params(2), operands=(qkv, qkv, qkv))


def sb_bwd(qkv, tot, swept, do, do_col=0, sides=()):
    s = qkv.shape[0]
    tq = _sb_tq(s)
    kpq = tq // BLK

    def body(q_ref, k_ref, v_ref, t_ref, n_ref, do_ref, dq_ref, dk_ref, dv_ref, acc_ref):
        qi = pl.program_id(1)
        n = jnp.clip(jnp.max(n_ref[...]).astype(jnp.int32), 0, qi)
        q2 = _sb_stack(q_ref[...] * SB_SCALE).astype(BF16)
        do2 = _sb_stack(do_ref[...]).astype(BF16)
        tot2 = jnp.concatenate([t_ref[:, :LANE], t_ref[:, LANE:]], axis=0)
        sq = (BLK, BLK)
        later = _mask2(_iota(sq, 0) > _iota(sq, 1))
        before = _mask2(_iota(sq, 0) < _iota(sq, 1))
        acc_ref[...] = jnp.zeros_like(acc_ref)

        @pl.when(qi == 0)
        def _():
            dk_ref[...] = jnp.zeros_like(dk_ref)
            dv_ref[...] = jnp.zeros_like(dv_ref)

        def step(j, carry, masked):
            rc, fc = carry
            rows = pl.ds(pl.multiple_of(j * BLK, BLK), BLK)
            kj = k_ref[rows, :]
            vj = v_ref[rows, :]
            valid, ls, lk = _sb_logits(q2, kj, qi * tq, j * BLK, masked)
            after, total = _sums(lk, later, 2)
            rc = rc - total
            w = _sb_where(valid, jnp.exp(ls + rc + after))
            e = w * dot_nt(do2, vj)
            f_in, f_tot = _sums(e, before, 2)
            sg = jnp.exp(ls)
            dz = _sb_where(valid, e * (1.0 - sg) - (fc + f_in) * sg)
            acc_ref[...] += dot_nn(dz, kj)
            dk_ref[rows, :] += dot_tn(dz, q2)
            dv_ref[rows, :] += dot_tn(w, do2)
            return rc, fc + f_tot

        def tile(g, carry):
            for d in range(kpq):
                carry = step(kpq * g + d, carry, False)
            return carry

        carry = lax.fori_loop(qi - n, qi, tile, (tot2, jnp.zeros((2 * tq, LANE), F32)))
        for d in range(kpq):
            carry = step(kpq * qi + d, carry, True)
        dq_ref[...] = (SB_SCALE * _sb_unstack(acc_ref[...])).astype(dq_ref.dtype)

    qspec, kspec, vspec = _sb_specs(s)
    blk = pl.BlockSpec((tq, LANE), lambda p, i: (i, p))
    acc = pl.BlockSpec((s, LANE), lambda p, i: (0, p))
    return hosted_call(
        body, sides, name="sb_bwd", grid=(SB_PAIRS, s // tq),
        in_specs=[qspec, kspec, vspec, pl.BlockSpec((tq, 2 * LANE), lambda p, i: (i, p)),
                  pl.BlockSpec((None, None, 8, LANE), lambda p, i: (p, i, 0, 0)),
                  pl.BlockSpec((tq, LANE), lambda p, i: (i, do_col + p))],
        out_specs=[blk, acc, acc],
        out_shape=[jax.ShapeDtypeStruct((s, SB_WIDTH), BF16)] + [jax.ShapeDtypeStruct((s, SB_WIDTH), F32)] * 2,
        scratch_shapes=[pltpu.VMEM((2 * tq, LANE), F32)],
        compiler_params=_params(2), operands=(qkv, qkv, qkv, tot, swept, do))


POOL_GROUP_DIM = POOL_WIDTH // len(POOL_WINDOWS)


assert all(w == 2 ** (i + 1) for i, w in enumerate(POOL_WINDOWS))


def _pool_inv(c):
    group = _iota((BLK, POOL_WIDTH), 1) // POOL_GROUP_DIM
    pos = c * BLK + _iota((BLK, POOL_WIDTH), 0)
    win = jnp.zeros((BLK, POOL_WIDTH), jnp.int32)
    for gi, wn in enumerate(POOL_WINDOWS):
        win = jnp.where(group == gi, wn, win)
    return 1.0 / jnp.minimum(pos + 1, win).astype(F32)


def _window_sums(ext, trailing):
    group = _iota(ext.shape, 1) // POOL_GROUP_DIM
    acc = ext
    out = None
    for gi in range(len(POOL_WINDOWS)):
        shift = 2 ** gi
        acc = acc + pltpu.roll(acc, shift if trailing else ext.shape[0] - shift, 0)
        out = acc if out is None else jnp.where(group == gi, acc, out)
    return out


def _pool_pooled(ext, cur, inv):
    return _window_sums(ext, True)[BLK:] * inv - cur


def pool_fwd(p, wblk, pb, ps):
    s, n = p.shape

    def body(cur_ref, prev_ref, w_ref, pb_ref, ps_ref, o_ref):
        c = pl.program_id(0)
        cur = cur_ref[...]
        prev = jnp.where(c > 0, prev_ref[...], 0.0)
        pooled = _pool_pooled(jnp.concatenate([prev, cur], axis=0), cur, _pool_inv(c))
        o_ref[...] = (dot_nn(pooled, w_ref[...]) + pb_ref[...]) * ps_ref[...]

    return pl.pallas_call(
        body, name="pool_fwd", grid=(s // BLK,),
        in_specs=[pl.BlockSpec((BLK, n), lambda c: (c, 0)), pl.BlockSpec((BLK, n), lambda c: (jnp.maximum(c - 1, 0), 0)),
                  _full_spec((n, n)), _full_spec((1, n)), _full_spec((1, n))],
        out_specs=pl.BlockSpec((BLK, n), lambda c: (c, 0)), out_shape=jax.ShapeDtypeStruct((s, n), F32),
        compiler_params=_params(1),
    )(p, p, wblk, pb, ps)


def pool_bwd(p, wblk, pb, ps, dout, do_col=0, sides=()):
    s, n = p.shape
    nb = s // BLK

    def body(cur_ref, prev_ref, w_ref, pb_ref, ps_ref, do_ref, dp_ref, dw_ref, dpb_ref, dps_ref, carry_ref):
        i = pl.program_id(0)
        c = nb - 1 - i
        first = i == 0
        cur = cur_ref[...]
        prev = jnp.where(c > 0, prev_ref[...], 0.0)
        inv = _pool_inv(c)
        pooled = _pool_pooled(jnp.concatenate([prev, cur], axis=0), cur, inv)
        mixed = dot_nn(pooled, w_ref[...]) + pb_ref[...]
        dov = do_ref[...]
        dmixed = dov * ps_ref[...]
        _acc(dps_ref, first, jnp.sum(dov * mixed, axis=0, keepdims=True))
        _acc(dpb_ref, first, jnp.sum(dmixed, axis=0, keepdims=True))
        _acc(dw_ref, first, dot_tn(pooled, dmixed))
        dpooled = dot_nt(dmixed, w_ref[...])
        dext = _window_sums(jnp.concatenate([jnp.zeros((BLK, n), F32), dpooled * inv], axis=0), False)

        @pl.when(first)
        def _():
            carry_ref[...] = jnp.zeros_like(carry_ref)

        dp_ref[...] = (dext[BLK:] - dpooled + carry_ref[...]).astype(dp_ref.dtype)
        carry_ref[...] = dext[:BLK]

    rev = lambda i: (nb - 1 - i, 0)
    return hosted_call(
        body, sides, name="pool_bwd", grid=(nb,),
        in_specs=[pl.BlockSpec((BLK, n), rev), pl.BlockSpec((BLK, n), lambda i: (jnp.maximum(nb - 2 - i, 0), 0)),
                  _full_spec((n, n)), _full_spec((1, n)), _full_spec((1, n)),
                  pl.BlockSpec((BLK, n), lambda i: (nb - 1 - i, do_col))],
        out_specs=[pl.BlockSpec((BLK, n), rev), _full_spec((n, n)), _full_spec((1, n)), _full_spec((1, n))],
        out_shape=[jax.ShapeDtypeStruct((s, n), BF16), jax.ShapeDtypeStruct((n, n), F32),
                   jax.ShapeDtypeStruct((1, n), F32), jax.ShapeDtypeStruct((1, n), F32)],
        scratch_shapes=[pltpu.VMEM((BLK, n), F32)],
        compiler_params=_params(1), operands=(p, p, wblk, pb, ps, dout))


def _row_tile(rows):
    if rows <= 512:
        return rows
    for t in (512, 256, 128, 64, 32, 16, 8):
        if rows % t == 0:
            return t
    return rows


def adamw(w, g, m, v):
    n, rows, cols = w.shape
    tr = _row_tile(rows)

    def body(w_ref, g_ref, m_ref, v_ref, d_ref, nm_ref, nv_ref):
        d_ref[...], nm_ref[...], nv_ref[...] = _adamw_math(w_ref[...], g_ref[...], m_ref[...], v_ref[...])

    spec = pl.BlockSpec((1, tr, cols), lambda i, j: (i, j, 0))
    return pl.pallas_call(
        body, name="adamw", grid=(n, rows // tr), in_specs=[spec] * 4, out_specs=[spec] * 3,
        out_shape=[jax.ShapeDtypeStruct(w.shape, F32)] * 3, compiler_params=_params(2),
    )(w, g, m, v)


def _adamw_math(w, g, m, v):
    nm = ADAM_B1 * m + (1.0 - ADAM_B1) * g
    nv = ADAM_B2 * v + (1.0 - ADAM_B2) * (g * g)
    m_hat = nm / (1.0 - ADAM_B1 ** ADAM_STEP)
    v_hat = nv / (1.0 - ADAM_B2 ** ADAM_STEP)
    return -ADAM_LR * (m_hat / (jnp.sqrt(v_hat) + ADAM_EPS) + ADAM_WD * w), nm, nv


def adamw_small(ws, gs, ms, vs):
    n = len(ws)

    def body(*refs):
        for i in range(n):
            outs = _adamw_math(*[refs[k * n + i][...] for k in range(4)])
            for k in range(3):
                refs[(4 + k) * n + i][...] = outs[k]

    vm = pl.BlockSpec(memory_space=pltpu.VMEM)
    outs = pl.pallas_call(
        body, name="adamw_small", in_specs=[vm] * (4 * n), out_specs=[vm] * (3 * n),
        out_shape=[jax.ShapeDtypeStruct(w.shape, F32) for w in ws] * 3,
    )(*ws, *gs, *ms, *vs)
    return outs[:n], outs[n:2 * n], outs[2 * n:]


def slab_sum(srcs, n_out, out_dtype, into=None, slot=0):
    _, rows, cols = srcs[0][0].shape
    tr = _row_tile(rows)
    n_src = len(srcs)
    sel = jnp.stack([jnp.asarray(base, jnp.int32) for _, base, _ in srcs])

    def body(sel_ref, *refs):
        acc = refs[0][...].astype(F32)
        for r in refs[1:n_src]:
            acc = acc + r[...].astype(F32)
        refs[-1][...] = acc.astype(out_dtype)

    def in_spec(k, step):
        return pl.BlockSpec((None, tr, cols), lambda o, i, sel_ref: (sel_ref[k] + step * o, i, 0))

    shape = (n_out, rows, cols) if into is None else into.shape
    return pl.pallas_call(
        body, name="slab_sum",
        grid_spec=pltpu.PrefetchScalarGridSpec(
            num_scalar_prefetch=1, grid=(n_out, rows // tr),
            in_specs=[in_spec(k, step) for k, (_, _, step) in enumerate(srcs)] + ([] if into is None else [ANY]),
            out_specs=pl.BlockSpec((None, tr, cols), lambda o, i, sel_ref: (slot + o, i, 0))),
        out_shape=jax.ShapeDtypeStruct(shape, out_dtype), compiler_params=_params(2),
        input_output_aliases={} if into is None else {1 + n_src: 0},
    )(sel, *[a for a, _, _ in srcs], *([] if into is None else [into]))


ICI_FLIPS = ((1, 0, 0), (0, 1, 0), (1, 1, 0))
D2D_FLIPS = ((0, 0, 1),)
ANY = pl.BlockSpec(memory_space=pl.ANY)


def _me():
    return lax.axis_index("x"), lax.axis_index("y"), lax.axis_index("c")


def _flipped(me, flip):
    return tuple(1 - m if f else m for m, f in zip(me, flip))


def _chip(dev):
    return 2 * dev[0] + dev[1]


def _dev(dev):
    return 4 * dev[0] + 2 * dev[1] + dev[2]


N_CHIP = 4
D2D = (0, 0, 1)


class Exchange:
    def __init__(self, xs, n_out, copies, own=None, in_place=False):
        self.xs, self.copies, self.own, self.in_place = list(xs), copies, own, in_place
        self.n_arr, self.n_cp = len(self.xs), len(copies)
        self.out_shape = [jax.ShapeDtypeStruct((n_out,) + x.shape[1:], x.dtype) for x in self.xs]
        self.scratch = [pltpu.SemaphoreType.DMA((self.n_arr * self.n_cp,)),
                        pltpu.SemaphoreType.DMA((self.n_arr * self.n_cp,)), pltpu.SemaphoreType.DMA((self.n_arr,))]

    def _own(self, x_refs, o_refs, sems, me):
        if self.own is None:
            return []
        return [pltpu.make_async_copy(x_refs[a].at[self.own[0](me)], o_refs[a].at[self.own[1](me)], sems[2].at[a])
                for a in range(self.n_arr)]

    def _copy(self, x_refs, o_refs, sems, me, a, j, sender):
        flip, src_slot, dst_slot = self.copies[j]
        k = a * self.n_cp + j
        return pltpu.make_async_remote_copy(
            src_ref=x_refs[a].at[src_slot(me)], dst_ref=o_refs[a].at[dst_slot(sender)],
            send_sem=sems[0].at[k], recv_sem=sems[1].at[k], device_id=_flipped(me, flip), device_id_type=MESH)

    def start(self, x_refs, o_refs, sems):
        me = _me()
        for cp in self._own(x_refs, o_refs, sems, me):
            cp.start()
        for j in range(self.n_cp):
            for a in range(self.n_arr):
                self._copy(x_refs, o_refs, sems, me, a, j, me).start()

    def wait(self, x_refs, o_refs, sems):
        me = _me()
        for j in range(self.n_cp):
            for a in range(self.n_arr):
                self._copy(x_refs, o_refs, sems, me, a, j, _flipped(me, self.copies[j][0])).wait_recv()
        for j in range(self.n_cp):
            for a in range(self.n_arr):
                self._copy(x_refs, o_refs, sems, me, a, j, me).wait_send()
        for cp in self._own(x_refs, o_refs, sems, me):
            cp.wait()

    def run(self, name):
        n = self.n_arr

        def body(*refs):
            self.start(refs[:n], refs[n:2 * n], refs[2 * n:])
            self.wait(refs[:n], refs[n:2 * n], refs[2 * n:])

        return pl.pallas_call(
            body, name=name, in_specs=[ANY] * n, out_specs=[ANY] * n, out_shape=self.out_shape,
            input_output_aliases={a: a for a in range(n)} if self.in_place else {}, scratch_shapes=self.scratch,
        )(*self.xs)


def hosted_call(body, sides, *, name, grid, in_specs, out_specs, out_shape, scratch_shapes, compiler_params, operands):
    n_in, n_out, n_scr = len(in_specs), len(out_specs), len(scratch_shapes)
    live = [s for s in sides if s is not None]
    if not live:
        outs = pl.pallas_call(body, name=name, grid=grid, in_specs=in_specs, out_specs=out_specs, out_shape=out_shape,
                              scratch_shapes=scratch_shapes, compiler_params=compiler_params)(*operands)
        return outs, [None] * len(sides)
    n = sum(s.n_arr for s in live)
    lo = [sum(s.n_arr for s in live[:i]) for i in range(len(live))]

    def full_body(*refs):
        ins, sx = refs[:n_in], refs[n_in:n_in + n]
        outs, so = refs[n_in + n:n_in + n + n_out], refs[n_in + n + n_out:n_in + 2 * n + n_out]
        scr, sems = refs[n_in + 2 * n + n_out:n_in + 2 * n + n_out + n_scr], refs[n_in + 2 * n + n_out + n_scr:]
        first = functools.reduce(jnp.logical_and, [pl.program_id(a) == 0 for a in range(len(grid))])
        last = functools.reduce(jnp.logical_and, [pl.program_id(a) == g - 1 for a, g in enumerate(grid)])
        parts = [(s, sx[l:l + s.n_arr], so[l:l + s.n_arr], sems[3 * i:3 * i + 3]) for i, (s, l) in enumerate(zip(live, lo))]

        @pl.when(first)
        def _():
            for s, x, o, m in parts:
                s.start(x, o, m)

        body(*ins, *outs, *scr)

        @pl.when(last)
        def _():
            for s, x, o, m in parts:
                s.wait(x, o, m)

    aliases = {n_in + l + a: n_out + l + a for s, l in zip(live, lo) if s.in_place for a in range(s.n_arr)}
    outs = pl.pallas_call(
        full_body, name=name + "_x", grid=grid, in_specs=list(in_specs) + [ANY] * n,
        out_specs=list(out_specs) + [ANY] * n, out_shape=list(out_shape) + [o for s in live for o in s.out_shape],
        input_output_aliases=aliases,
        scratch_shapes=list(scratch_shapes) + [m for s in live for m in s.scratch], compiler_params=compiler_params,
    )(*operands, *[x for s in live for x in s.xs])
    side_outs = iter([outs[n_out + l:n_out + l + s.n_arr] for s, l in zip(live, lo)])
    return outs[:n_out], [next(side_outs) if s is not None else None for s in sides]


def gather_ici(shards):
    ici = [(f, lambda me: 0, _dev) for f in ICI_FLIPS]
    return Exchange([s[None] for s in shards], N_DEV, ici, (lambda me: 0, _dev))


def gather_d2d(blocks):
    d2d = [(D2D, (lambda me, k=k: 2 * k + me[2]), (lambda sender, k=k: 2 * k + sender[2])) for k in range(N_CHIP)]
    return Exchange(blocks, N_DEV, d2d, None, in_place=True)


def gathered(blocks):
    return [b.reshape(-1, b.shape[2]) for b in blocks]


def scatter_d2d(parts):
    d2d = [(D2D, (lambda me, k=k: 2 * k + 1 - me[2]), (lambda sender, k=k: k)) for k in range(N_CHIP)]
    return Exchange(parts, N_CHIP, d2d)


def chip_sums(parts, sib):
    c = _me()[2]
    return [slab_sum([(p, c, 2), (s, 0, 1)], N_CHIP, BF16) for p, s in zip(parts, sib)]


def scatter_ici(sums):
    ici = [(f, (lambda me, f=f: _chip(_flipped(me, f))), (lambda sender, i=i: i)) for i, f in enumerate(ICI_FLIPS)]
    return Exchange(sums, len(ICI_FLIPS), ici)


def device_sums(sums, got, into=None, slot=0):
    x, y, _ = _me()
    outs = [slab_sum([(cs, 2 * x + y, 0)] + [(g, i, 0) for i in range(len(ICI_FLIPS))], 1, F32,
                     None if into is None else into[a], slot) for a, (cs, g) in enumerate(zip(sums, got))]
    return outs if into is not None else [o[0] for o in outs]


def all_gather(shards):
    blocks = gather_ici(shards).run("gather_ici")
    return gathered(gather_d2d(blocks).run("gather_d2d"))


def reduce_scatter(parts, *into):
    sums = chip_sums(parts, scatter_d2d(parts).run("scatter_d2d"))
    return device_sums(sums, scatter_ici(sums).run("scatter_ici"), *into)


def all_reduce_small(v):
    flips = D2D_FLIPS + ICI_FLIPS[:2]

    def body(v_ref, o_ref, got_ref, send_sems, recv_sems):
        me = _me()
        o_ref[...] = v_ref[...]
        for i, flip in enumerate(flips):
            cp = pltpu.make_async_remote_copy(
                src_ref=o_ref, dst_ref=got_ref.at[i], send_sem=send_sems.at[i], recv_sem=recv_sems.at[i],
                device_id=_flipped(me, flip), device_id_type=MESH)
            cp.start()
            cp.wait()
            o_ref[...] = o_ref[...] + got_ref[i]

    vm = pl.BlockSpec(memory_space=pltpu.VMEM)
    return pl.pallas_call(
        body, name="all_reduce_small", in_specs=[vm], out_specs=vm, out_shape=jax.ShapeDtypeStruct(v.shape, F32),
        scratch_shapes=[pltpu.VMEM((len(flips),) + v.shape, F32), pltpu.SemaphoreType.DMA((len(flips),)),
                        pltpu.SemaphoreType.DMA((len(flips),))],
    )(v)


def _perm_rows(wt):
    pad = jnp.zeros((D_IN_PAD - D_IN_PROJ, wt.shape[1]), wt.dtype)
    return jnp.concatenate([wt[:DT_LO], wt[DT_HI:], wt[DT_LO:DT_HI], pad], axis=0)


def _unperm_rows(dwt):
    n = D_IN_PROJ - (DT_HI - DT_LO)
    return jnp.concatenate([dwt[:DT_LO], dwt[n:D_IN_PROJ], dwt[DT_LO:n]], axis=0)


def _pad_lanes(v):
    return jnp.pad(v, ((0, 0), (0, LANE - v.shape[1])))[:, None]


def _block_diag(w):
    l, g, n, _ = w.shape
    out = jnp.zeros((l, g * n, g * n), w.dtype)
    for i in range(g):
        out = out.at[:, i * n:(i + 1) * n, i * n:(i + 1) * n].set(w[:, i])
    return out


def _pack(groups):
    flat = []
    for grp in groups:
        parts = [a.reshape(-1) for a in (grp if isinstance(grp, (list, tuple)) else [grp])]
        n = sum(p.shape[0] for p in parts)
        if -n % LANE:
            parts.append(jnp.zeros((-n % LANE,), parts[0].dtype))
        flat += parts
    return jnp.concatenate(flat).reshape(-1, LANE)


def _unpack(buf, shapes):
    out = []
    lo = 0
    buf = buf.reshape(-1)
    for shp in shapes:
        n = 1
        for k in shp:
            n *= k
        out.append(buf[lo:lo + n].reshape(shp))
        lo += n + (-n % LANE)
    return out


def small_params(w, conv_w_full):
    return dict(
        n1w=w["norm1_w"][:, None], cw=jnp.pad(conv_w_full, ((0, 0), (0, 8 - CONV_WIDTH), (0, 0))),
        cb=w["conv_b"][:, None], dtb=_pad_lanes(w["dt_bias"]), alog=_pad_lanes(w["a_log"]), dsk=_pad_lanes(w["d_skip"]),
        snw=w["ssd_norm_w"][:, None], wblk=_block_diag(w["pool_w"]), pb=w["pool_b"].reshape(-1, 1, POOL_WIDTH),
        ps=w["pool_scale"][:, None], n2w=w["norm2_w"][:, None])


MIX = ("w_in", "w_out")
FFN = ("w_gate", "w_up", "w_down")


def layer_params(small, l):
    return {k: v[l] for k, v in small.items()}


def mix_weights(whole):
    return _perm_rows(whole[0]), whole[1]


def _slabs(g):
    return g.reshape(N_DEV, -1, g.shape[-1])


def _layer_fwd(x, p, mix, ffn=None, ffn_shards=None, next_mix_shards=None):
    ici = [gather_ici([sh]) for sh in ffn_shards] if ffn_shards is not None else [None] * len(FFN)
    (z, xbc, qkv, pp, dtr, h1), (blk_g,) = inproj_fwd(x, p["n1w"], mix[0], [ici[0]])
    (u,), _ = conv_fwd(xbc, p["cw"], p["cb"])
    (y_ssd, st), (blk_u,) = ssd_fwd(u, z, dtr, p["dtb"], p["alog"], p["dsk"], p["snw"], [ici[1]])
    (o, tot, swept), (blk_d,) = sb_fwd(qkv, [ici[2]])
    blocks = blk_g + blk_u + blk_d if ffn_shards is not None else None
    yp = pool_fwd(pp, p["wblk"], p["pb"], p["ps"])
    (x_mid, ycat), (blocks,) = outproj_fwd([y_ssd, o, yp], mix[1], x,
                                           [gather_d2d(blocks) if blocks is not None else None])
    if blocks is not None:
        ffn = gathered(blocks)
    (x_out, g, uu), (nxt,) = ffn_fwd(x_mid, p["n2w"], *ffn,
                                     [gather_ici(next_mix_shards) if next_mix_shards is not None else None])
    sv = dict(x=x, z=z, xbc=xbc, qkv=qkv, pp=pp, dtr=dtr, h1=h1, u=u, st=st, tot=tot, swept=swept, ycat=ycat,
              x_mid=x_mid, g=g, uu=uu, w_in=mix[0], w_out=mix[1], wg=ffn[0], wu=ffn[1], wd=ffn[2])
    return x_out, sv, nxt


def _layer_bwd(dxo, sv, p, pending_mix=None, exchange=False, into_ffn=(), into_mix=()):
    (dx_mid, dn2w, a, dg, du, h2), (sib,) = ffn_bwd(
        dxo, sv["x_mid"], sv["g"], sv["uu"], p["n2w"], sv["wg"], sv["wu"], sv["wd"],
        [scatter_d2d(pending_mix) if pending_mix is not None else None])
    sums_mix = chip_sums(pending_mix, sib) if pending_mix is not None else None
    dwg, dwu = mm_tn([dg, du], h2)
    gr = dict(norm2_w=dn2w[0], w_gate=dwg, w_up=dwu, w_down=mm_tn(a, dxo))
    parts = [_slabs(gr[k]) for k in FFN] if exchange else None
    (dycat,), (sib,) = outproj_bwd(dx_mid, sv["w_out"], [scatter_d2d(parts) if exchange else None])
    sums_ffn = chip_sums(parts, sib) if exchange else None
    gr["w_out"] = mm_tn(sv["ycat"], dx_mid)
    (dp, dwblk, dpb, dps), _ = pool_bwd(sv["pp"], p["wblk"], p["pb"], p["ps"], dycat,
                                        (SSD_WIDTH + SB_WIDTH) // POOL_WIDTH)
    n = POOL_GROUP_DIM
    gr["pool_w"] = jnp.stack([dwblk[i * n:(i + 1) * n, i * n:(i + 1) * n] for i in range(len(POOL_WINDOWS))])
    gr["pool_b"] = dpb.reshape(len(POOL_WINDOWS), n)
    gr["pool_scale"] = dps[0]
    (dq, dk, dv), (got_ud,) = sb_bwd(sv["qkv"], sv["tot"], sv["swept"], dycat, SSD_WIDTH // LANE,
                                     [scatter_ici(sums_ffn[1:]) if exchange else None])
    (du_, dz, ddtr, ddtb, dalog, ddsk, dsnw), (got_mix,) = ssd_bwd(
        sv["u"], sv["z"], sv["dtr"], sv["st"], dycat, p["dtb"], p["alog"], p["dsk"], p["snw"],
        [scatter_ici(sums_mix) if sums_mix is not None else None])
    done_mix = device_sums(sums_mix, got_mix, *into_mix) if sums_mix is not None else None
    gr.update(dt_bias=ddtb[0, :SSD_HEADS], a_log=dalog[0, :SSD_HEADS], d_skip=ddsk[0, :SSD_HEADS], ssd_norm_w=dsnw[0])
    (dxbc, dcw, dcb), (got_g,) = conv_bwd(sv["xbc"], p["cw"], p["cb"], du_,
                                          [scatter_ici(sums_ffn[:1]) if exchange else None])
    done_ffn = device_sums(sums_ffn, got_g + got_ud, *into_ffn) if exchange else None
    gr.update(conv_w=dcw[:CONV_WIDTH], conv_b=dcb[0])
    dx, dn1w, dproj = inproj_bwd([dz, dxbc, dq, dk, dv, dp, ddtr], sv["w_in"], sv["x"], p["n1w"], dx_mid)
    gr.update(norm1_w=dn1w[0], w_in=_unperm_rows(mm_tn(dproj, sv["h1"])))
    return dx, gr, done_ffn, done_mix


def local_step(x, tgt, params, weights, final_w):
    saved = []
    for p, (mix, ffn) in zip(params, weights):
        x, sv, _ = _layer_fwd(x, p, mix, ffn)
        saved.append(sv)
    loss, dx, dfw = head_loss(x, final_w[None], tgt)
    grads = []
    for p, sv in zip(reversed(params), reversed(saved)):
        dx, gr, _, _ = _layer_bwd(dx, sv, p)
        grads.append(gr)
    grads.reverse()
    return loss, dx, dfw[0], grads


WEIGHTS = ("norm1_w", "w_in", "conv_w", "conv_b", "dt_bias", "a_log", "d_skip", "ssd_norm_w", "pool_w", "pool_b",
           "pool_scale", "w_out", "norm2_w", "w_gate", "w_up", "w_down", "final_norm_w")
COL_SHARDED = ("w_in", "w_gate", "w_up")
ROW_SHARDED = ("w_out", "w_down")
SMALL = tuple(k for k in WEIGHTS if k not in COL_SHARDED + ROW_SHARDED)


def kernel(x, norm1_w, w_in, conv_w, conv_b, dt_bias, a_log, d_skip, ssd_norm_w, pool_w, pool_b, pool_scale, w_out, norm2_w, w_gate, w_up, w_down, final_norm_w, loss_target, m_norm1_w, m_w_in, m_conv_w, m_conv_b, m_dt_bias, m_a_log, m_d_skip, m_ssd_norm_w, m_pool_w, m_pool_b, m_pool_scale, m_w_out, m_norm2_w, m_w_gate, m_w_up, m_w_down, m_final_norm_w, v_norm1_w, v_w_in, v_conv_w, v_conv_b, v_dt_bias, v_a_log, v_d_skip, v_ssd_norm_w, v_pool_w, v_pool_b, v_pool_scale, v_w_out, v_norm2_w, v_w_gate, v_w_up, v_w_down, v_final_norm_w):
    args = dict(locals())
    w = {k: args[k] for k in WEIGHTS}
    m = {k: args["m_" + k] for k in WEIGHTS}
    v = {k: args["v_" + k] for k in WEIGHTS}
    depth = w_in.shape[0]
    dev = _dev(_me())
    n_cw = conv_w.shape[-1]

    shards = {k: (jnp.swapaxes(w[k], 1, 2) if k in COL_SHARDED else w[k]).astype(BF16) for k in MIX + FFN}
    whole = all_gather([jnp.swapaxes(conv_w, 0, 2).reshape(n_cw, -1)] + [shards[k][0] for k in MIX])
    conv_w_full = jnp.swapaxes(whole[0].reshape(N_DEV * n_cw, CONV_WIDTH, depth), 0, 2)
    small = small_params(w, conv_w_full)
    xs = x[0]
    params, saved = [layer_params(small, l) for l in range(depth)], []
    mix = mix_weights(whole[1:])
    for l in range(depth):
        xs, sv, nxt = _layer_fwd(xs, params[l], mix, ffn_shards=[shards[k][l] for k in FFN],
                                 next_mix_shards=[shards[k][l + 1] for k in MIX] if l + 1 < depth else None)
        saved.append(sv)
        if nxt is not None:
            mix = mix_weights(gathered(gather_d2d(nxt).run("gather_d2d")))
    loss, dx, dfw = head_loss(xs, final_norm_w[None], loss_target[0])
    layer_grads = [None] * depth
    native = {k: lax.empty((depth,) + shards[k].shape[1:], F32) for k in MIX + FFN}
    pending = None
    for l in reversed(range(depth)):
        dx, layer_grads[l], done_ffn, done_mix = _layer_bwd(
            dx, saved[l], params[l], pending, exchange=True, into_ffn=([native[k] for k in FFN], l),
            into_mix=([native[k] for k in MIX], l + 1))
        native.update(zip(FFN, done_ffn))
        if pending is not None:
            native.update(zip(MIX, done_mix))
        pending = [_slabs(layer_grads[l][k]) for k in MIX]
    native.update(zip(MIX, reduce_scatter(pending, [native[k] for k in MIX], 0)))
    grads = {k: jnp.swapaxes(native[k], 1, 2) if k in COL_SHARDED else native[k] for k in MIX + FFN}
    layered = [k for k in SMALL if k != "final_norm_w"]
    small_shapes = [(1, LANE)] + [(depth,) + layer_grads[0][k].shape for k in layered] + [dfw[0].shape]
    packed = _pack([loss] + [[layer_grads[l][k] for l in range(depth)] for k in layered] + [dfw[0]])
    summed = _unpack(all_reduce_small(packed), small_shapes)
    loss = summed[0][0, 0]
    grads.update(zip(layered + ["final_norm_w"], summed[1:]))
    grads["conv_w"] = lax.dynamic_slice_in_dim(grads["conv_w"], dev * n_cw, n_cw, axis=2)

    delta, new_m, new_v = {}, {}, {}
    for k in MIX + FFN:
        if k in ("w_gate", "w_up"):
            wt, mt, vt = (jnp.swapaxes(t, 1, 2) for t in (w[k], m[k], v[k]))
            delta[k], new_m[k], new_v[k] = [jnp.swapaxes(o, 1, 2) for o in adamw(wt, native[k], mt, vt)]
        else:
            delta[k], new_m[k], new_v[k] = adamw(w[k], grads[k], m[k], v[k])
    two_d = lambda a: a.reshape(1, -1) if a.ndim == 1 else a
    outs = adamw_small(*[[two_d(t[k]) for k in SMALL] for t in (w, grads, m, v)])
    for dst, arrs in zip((delta, new_m, new_v), outs):
        dst.update({k: a.reshape(w[k].shape) for k, a in zip(SMALL, arrs)})
    return (loss, dx[None], *[grads[k] for k in WEIGHTS], *[delta[k] for k in WEIGHTS],
            *[new_m[k] for k in WEIGHTS], *[new_v[k] for k in WEIGHTS])
```

```python
import functools

import jax
import jax.numpy as jnp
from jax import lax
from jax.experimental import pallas as pl
from jax.experimental.pallas import tpu as pltpu

F32 = jnp.float32
BF16 = jnp.bfloat16
HIGHEST = lax.Precision.HIGHEST
MESH = pl.DeviceIdType.MESH

EPS = 1e-6
D_MODEL = 1024
SSD_WIDTH = 512
SSD_HEADS = 8
HEAD_DIM = 64
D_STATE = 128
CONV_WIDTH = 4
CONV_DIM = 1024
SB_WIDTH = 256
POOL_WIDTH = 256
POOL_WINDOWS = (2, 4, 8, 16)
D_IN_PROJ = 2568
D_FF = 2816
N_DEV = 8
SEG = (512, 1024, 768, 256, 128)
D_IN_PAD = sum(SEG)
DT_LO, DT_HI = 1536, 1544

LANE = 128
BLK = 128
ROW_TILE = 256
VMEM_LIMIT = 56 * 2**20

ADAM_LR, ADAM_B1, ADAM_B2, ADAM_EPS, ADAM_WD, ADAM_STEP = 0.001, 0.9, 0.999, 1e-08, 0.01, 10


def _params(n_axes=1, vmem=None):
    return pltpu.CompilerParams(dimension_semantics=("arbitrary",) * n_axes, vmem_limit_bytes=vmem)


def _dot(a, b, dims, exact=False):
    if exact:
        return lax.dot_general(a.astype(F32), b.astype(F32), (dims, ((), ())), precision=HIGHEST,
                               preferred_element_type=F32)
    return lax.dot_general(a.astype(BF16), b.astype(BF16), (dims, ((), ())), preferred_element_type=F32)


def dot_nn(a, b, exact=False):
    return _dot(a, b, ((1,), (0,)), exact)


def dot_nt(a, b, exact=False):
    return _dot(a, b, ((1,), (1,)), exact)


def dot_tn(a, b, exact=False):
    return _dot(a, b, ((0,), (0,)), exact)


def _iota(shape, axis):
    return lax.broadcasted_iota(jnp.int32, shape, axis)


def _lane_col(x, h):
    return jnp.sum(jnp.where(_iota(x.shape, 1) == h, x, 0.0), axis=1, keepdims=True)


def _sub_row(x, h):
    return jnp.sum(jnp.where(_iota(x.shape, 0) == h, x, 0.0), axis=0, keepdims=True)


def _sigmoid(x):
    return 1.0 / (1.0 + jnp.exp(-x))


def _rms_fwd(x, w):
    r = lax.rsqrt(jnp.mean(x * x, axis=-1, keepdims=True) + EPS)
    return x * r * w


def _rms_bwd(x, w, dy):
    r = lax.rsqrt(jnp.mean(x * x, axis=-1, keepdims=True) + EPS)
    xh = x * r
    dxh = dy * w
    dx = r * (dxh - xh * jnp.mean(dxh * xh, axis=-1, keepdims=True))
    return dx, jnp.sum(dy * xh, axis=0, keepdims=True)


def _acc(ref, first, val):
    @pl.when(first)
    def _():
        ref[...] = val

    @pl.when(jnp.logical_not(first))
    def _():
        ref[...] += val


def _row_spec(tm, n):
    return pl.BlockSpec((tm, n), lambda i: (i, 0))


def _full_spec(shape):
    return pl.BlockSpec(shape, lambda *_: (0,) * len(shape))


def inproj_fwd(x, nw, w, sides=()):
    s, d = x.shape
    tm = min(ROW_TILE, s)
    seg_dtypes = (F32, F32, BF16, F32, F32)

    def body(x_ref, nw_ref, w_ref, z_ref, xbc_ref, qkv_ref, p_ref, dt_ref, h_ref):
        h = _rms_fwd(x_ref[...], nw_ref[...]).astype(BF16)
        h_ref[...] = h
        lo = 0
        for ref, n in zip((z_ref, xbc_ref, qkv_ref, p_ref, dt_ref), SEG):
            ref[...] = dot_nt(h, w_ref[lo:lo + n, :]).astype(ref.dtype)
            lo += n

    return hosted_call(
        body, sides, name="inproj_fwd", grid=(s // tm,),
        in_specs=[_row_spec(tm, d), _full_spec((1, d)), _full_spec(w.shape)],
        out_specs=[_row_spec(tm, n) for n in SEG] + [_row_spec(tm, d)],
        out_shape=[jax.ShapeDtypeStruct((s, n), t) for n, t in zip(SEG, seg_dtypes)]
        + [jax.ShapeDtypeStruct((s, d), BF16)],
        scratch_shapes=[], compiler_params=_params(1, VMEM_LIMIT), operands=(x, nw, w))


def inproj_bwd(pieces, w, x, nw, dres):
    s, d = x.shape
    tm = min(ROW_TILE, s)
    n_p = len(pieces)
    widths = [p.shape[1] for p in pieces]

    def body(*refs):
        w_ref, x_ref, nw_ref, dres_ref, dx_ref, dnw_ref, dp_ref = refs[n_p:]
        dh = None
        lo = 0
        for ref, n in zip(refs[:n_p], widths):
            piece = ref[...].astype(BF16)
            dp_ref[:, lo:lo + n] = piece
            part = dot_nn(piece, w_ref[lo:lo + n, :])
            dh = part if dh is None else dh + part
            lo += n
        dx, dnw = _rms_bwd(x_ref[...], nw_ref[...], dh)
        dx_ref[...] = dres_ref[...] + dx
        _acc(dnw_ref, pl.program_id(0) == 0, dnw)

    return pl.pallas_call(
        body, name="inproj_bwd", grid=(s // tm,),
        in_specs=[_row_spec(tm, n) for n in widths] + [_full_spec(w.shape), _row_spec(tm, d), _full_spec((1, d)),
                                                       _row_spec(tm, d)],
        out_specs=[_row_spec(tm, d), _full_spec((1, d)), _row_spec(tm, sum(widths))],
        out_shape=[jax.ShapeDtypeStruct((s, d), F32), jax.ShapeDtypeStruct((1, d), F32),
                   jax.ShapeDtypeStruct((s, sum(widths)), BF16)],
        compiler_params=_params(1, VMEM_LIMIT),
    )(*pieces, w, x, nw, dres)


def outproj_fwd(pieces, w, res, sides=()):
    s, d = res.shape
    tm = min(ROW_TILE, s)
    n_p = len(pieces)
    widths = [p.shape[1] for p in pieces]

    def body(*refs):
        w_ref, r_ref, o_ref, y_ref = refs[n_p:]
        acc = r_ref[...]
        lo = 0
        for ref, n in zip(refs[:n_p], widths):
            piece = ref[...].astype(BF16)
            y_ref[:, lo:lo + n] = piece
            acc = acc + dot_nn(piece, w_ref[lo:lo + n, :])
            lo += n
        o_ref[...] = acc

    return hosted_call(
        body, sides, name="outproj_fwd", grid=(s // tm,),
        in_specs=[_row_spec(tm, n) for n in widths] + [_full_spec(w.shape), _row_spec(tm, d)],
        out_specs=[_row_spec(tm, d), _row_spec(tm, sum(widths))],
        out_shape=[jax.ShapeDtypeStruct((s, d), F32), jax.ShapeDtypeStruct((s, sum(widths)), BF16)],
        scratch_shapes=[], compiler_params=_params(1, VMEM_LIMIT), operands=(*pieces, w, res))


def outproj_bwd(dx, w, sides=()):
    s, d = dx.shape
    tm = min(ROW_TILE, s)

    def body(dx_ref, w_ref, o_ref):
        o_ref[...] = dot_nt(dx_ref[...], w_ref[...])

    return hosted_call(
        body, sides, name="outproj_bwd", grid=(s // tm,),
        in_specs=[_row_spec(tm, d), _full_spec(w.shape)],
        out_specs=[_row_spec(tm, w.shape[0])], out_shape=[jax.ShapeDtypeStruct((s, w.shape[0]), F32)],
        scratch_shapes=[], compiler_params=_params(1, VMEM_LIMIT), operands=(dx, w))


def ffn_fwd(x, nw, wg, wu, wd, sides=()):
    s, d = x.shape
    f = wg.shape[0]
    tm = min(ROW_TILE, s)

    def body(x_ref, nw_ref, wg_ref, wu_ref, wd_ref, o_ref, g_ref, u_ref):
        xv = x_ref[...]
        h = _rms_fwd(xv, nw_ref[...]).astype(BF16)
        g = dot_nt(h, wg_ref[...])
        u = dot_nt(h, wu_ref[...])
        g_ref[...] = g.astype(BF16)
        u_ref[...] = u.astype(BF16)
        o_ref[...] = xv + dot_nn(g * _sigmoid(g) * u, wd_ref[...])

    return hosted_call(
        body, sides, name="ffn_fwd", grid=(s // tm,),
        in_specs=[_row_spec(tm, d), _full_spec((1, d)), _full_spec(wg.shape), _full_spec(wu.shape),
                  _full_spec(wd.shape)],
        out_specs=[_row_spec(tm, d), _row_spec(tm, f), _row_spec(tm, f)],
        out_shape=[jax.ShapeDtypeStruct((s, d), F32), jax.ShapeDtypeStruct((s, f), BF16),
                   jax.ShapeDtypeStruct((s, f), BF16)],
        scratch_shapes=[], compiler_params=_params(1, VMEM_LIMIT), operands=(x, nw, wg, wu, wd))


def ffn_bwd(dxo, x, g, u, nw, wg, wu, wd, sides=()):
    s, d = x.shape
    f = wg.shape[0]
    tm = min(ROW_TILE, s)

    def body(dxo_ref, x_ref, g_ref, u_ref, nw_ref, wg_ref, wu_ref, wd_ref, dx_ref, dnw_ref, a_ref, dg_ref,
             du_ref, h_ref):
        dxo_v = dxo_ref[...]
        xv = x_ref[...]
        da = dot_nt(dxo_v, wd_ref[...])
        gv = g_ref[...].astype(F32)
        uv = u_ref[...].astype(F32)
        sg = _sigmoid(gv)
        sl = gv * sg
        a_ref[...] = (sl * uv).astype(BF16)
        dg = (da * uv * (sg * (1.0 + gv * (1.0 - sg)))).astype(BF16)
        du = (da * sl).astype(BF16)
        dg_ref[...] = dg
        du_ref[...] = du
        dh = dot_nn(dg, wg_ref[...]) + dot_nn(du, wu_ref[...])
        h_ref[...] = _rms_fwd(xv, nw_ref[...]).astype(BF16)
        dx, dnw = _rms_bwd(xv, nw_ref[...], dh)
        dx_ref[...] = dxo_v + dx
        _acc(dnw_ref, pl.program_id(0) == 0, dnw)

    return hosted_call(
        body, sides, name="ffn_bwd", grid=(s // tm,),
        in_specs=[_row_spec(tm, d), _row_spec(tm, d), _row_spec(tm, f), _row_spec(tm, f), _full_spec((1, d)),
                  _full_spec(wg.shape), _full_spec(wu.shape), _full_spec(wd.shape)],
        out_specs=[_row_spec(tm, d), _full_spec((1, d)), _row_spec(tm, f), _row_spec(tm, f), _row_spec(tm, f),
                   _row_spec(tm, d)],
        out_shape=[jax.ShapeDtypeStruct((s, d), F32), jax.ShapeDtypeStruct((1, d), F32),
                   jax.ShapeDtypeStruct((s, f), BF16), jax.ShapeDtypeStruct((s, f), BF16),
                   jax.ShapeDtypeStruct((s, f), BF16), jax.ShapeDtypeStruct((s, d), BF16)],
        scratch_shapes=[], compiler_params=_params(1, VMEM_LIMIT), operands=(dxo, x, g, u, nw, wg, wu, wd))


def _tile(n, cap=256):
    best = LANE
    for t in range(LANE, cap + 1, LANE):
        if n % t == 0:
            best = t
    return best


def mm_tn(a, b):
    many = isinstance(a, (list, tuple))
    a_list = list(a) if many else [a]
    n_a = len(a_list)
    s, k = a_list[0].shape
    n = b.shape[1]
    tk = _tile(k)

    def body(*refs):
        b_val = refs[n_a][...]
        for a_ref, o_ref in zip(refs[:n_a], refs[n_a + 1:]):
            o_ref[...] = dot_nn(a_ref[...].astype(BF16).T, b_val).astype(BF16)

    outs = pl.pallas_call(
        body, name="mm_tn", grid=(k // tk,),
        in_specs=[pl.BlockSpec((s, tk), lambda i: (0, i))] * n_a + [_full_spec((s, n))],
        out_specs=[pl.BlockSpec((tk, n), lambda i: (i, 0))] * n_a,
        out_shape=[jax.ShapeDtypeStruct((k, n), BF16)] * n_a, compiler_params=_params(1, VMEM_LIMIT),
    )(*a_list, b)
    return outs if many else outs[0]


def head_loss(x, fw, tgt):
    s, d = x.shape
    tm = min(ROW_TILE, s)

    def body(x_ref, fw_ref, t_ref, loss_ref, dx_ref, dfw_ref):
        xv = x_ref[...]
        err = _rms_fwd(xv, fw_ref[...]) - t_ref[...]
        part = jnp.zeros((1, LANE), F32) + 0.5 * jnp.sum(err * err) / d
        dx, dfw = _rms_bwd(xv, fw_ref[...], err / d)
        dx_ref[...] = dx
        first = pl.program_id(0) == 0
        _acc(loss_ref, first, part)
        _acc(dfw_ref, first, dfw)

    return pl.pallas_call(
        body, name="head_loss", grid=(s // tm,),
        in_specs=[_row_spec(tm, d), _full_spec((1, d)), _row_spec(tm, d)],
        out_specs=[_full_spec((1, LANE)), _row_spec(tm, d), _full_spec((1, d))],
        out_shape=[jax.ShapeDtypeStruct((1, LANE), F32), jax.ShapeDtypeStruct((s, d), F32),
                   jax.ShapeDtypeStruct((1, d), F32)],
        compiler_params=_params(1),
    )(x, fw, tgt)


HALO = 8


def _conv_pre(ext, cw_ref, cb_ref):
    shifted = [pltpu.roll(ext, CONV_WIDTH - 1 - i, 0)[HALO:] if i < CONV_WIDTH - 1 else ext[HALO:]
               for i in range(CONV_WIDTH)]
    acc = cb_ref[...] + sum(cw_ref[i:i + 1, :] * shifted[i] for i in range(CONV_WIDTH))
    return acc, shifted


def _halo_spec(n, block_of_step):
    return pl.BlockSpec((HALO, n), lambda i: (jnp.maximum(block_of_step(i) * (BLK // HALO) - 1, 0), 0))


def conv_fwd(xbc, cw, cb, sides=()):
    s, n = xbc.shape

    def body(cur_ref, prev_ref, cw_ref, cb_ref, o_ref):
        prev = jnp.where(pl.program_id(0) > 0, prev_ref[...], 0.0)
        acc, _ = _conv_pre(jnp.concatenate([prev, cur_ref[...]], axis=0), cw_ref, cb_ref)
        o_ref[...] = acc * _sigmoid(acc)

    return hosted_call(
        body, sides, name="conv_fwd", grid=(s // BLK,),
        in_specs=[pl.BlockSpec((BLK, n), lambda c: (c, 0)), _halo_spec(n, lambda c: c),
                  _full_spec(cw.shape), _full_spec((1, n))],
        out_specs=[pl.BlockSpec((BLK, n), lambda c: (c, 0))], out_shape=[jax.ShapeDtypeStruct((s, n), F32)],
        scratch_shapes=[], compiler_params=_params(1), operands=(xbc, xbc, cw, cb))


def conv_bwd(xbc, cw, cb, du, sides=()):
    s, n = xbc.shape
    nb = s // BLK

    def body(cur_ref, prev_ref, cw_ref, cb_ref, du_ref, dx_ref, dcw_ref, dcb_ref, nxt_ref):
        i = pl.program_id(0)
        c = nb - 1 - i
        prev = jnp.where(c > 0, prev_ref[...], 0.0)
        acc, shifted = _conv_pre(jnp.concatenate([prev, cur_ref[...]], axis=0), cw_ref, cb_ref)
        sg = _sigmoid(acc)
        dacc = du_ref[...] * (sg * (1.0 + acc * (1.0 - sg)))

        @pl.when(i == 0)
        def _():
            nxt_ref[...] = jnp.zeros_like(nxt_ref)
            dcw_ref[...] = jnp.zeros_like(dcw_ref)
            dcb_ref[...] = jnp.zeros_like(dcb_ref)

        dcb_ref[...] += jnp.sum(dacc, axis=0, keepdims=True)
        for t in range(CONV_WIDTH):
            dcw_ref[t:t + 1, :] += jnp.sum(dacc * shifted[t], axis=0, keepdims=True)
        ext = jnp.concatenate([dacc, nxt_ref[...]], axis=0)
        dx = cw_ref[CONV_WIDTH - 1:CONV_WIDTH, :] * dacc
        for t in range(CONV_WIDTH - 1):
            dx += cw_ref[t:t + 1, :] * pltpu.roll(ext, BLK + HALO - (CONV_WIDTH - 1 - t), 0)[:BLK]
        dx_ref[...] = dx.astype(dx_ref.dtype)
        nxt_ref[...] = dacc[:HALO]

    rev = lambda i: (nb - 1 - i, 0)
    return hosted_call(
        body, sides, name="conv_bwd", grid=(nb,),
        in_specs=[pl.BlockSpec((BLK, n), rev), _halo_spec(n, lambda i: nb - 1 - i),
                  _full_spec(cw.shape), _full_spec((1, n)), pl.BlockSpec((BLK, n), rev)],
        out_specs=[pl.BlockSpec((BLK, n), rev), _full_spec((8, n)), _full_spec((1, n))],
        out_shape=[jax.ShapeDtypeStruct((s, n), BF16), jax.ShapeDtypeStruct((8, n), F32),
                   jax.ShapeDtypeStruct((1, n), F32)],
        scratch_shapes=[pltpu.VMEM((HALO, n), F32)],
        compiler_params=_params(1), operands=(xbc, xbc, cw, cb, du))


N_PAIR = SSD_HEADS // 2
B_LO = SSD_WIDTH
C_LO = SSD_WIDTH + 2 * D_STATE


def _softplus(x):
    return jnp.maximum(x, 0.0) + jnp.log(1.0 + jnp.exp(-jnp.abs(x)))


def _ssd_chunk(u_ref, dt_ref, dtb_ref, alog_ref):
    shape = (BLK, BLK)
    tri = _iota(shape, 1) <= _iota(shape, 0)
    pre = dt_ref[...] + dtb_ref[...]
    dt = _softplus(pre)
    a = -jnp.exp(alog_ref[...])
    acum = dot_nn(tri.astype(F32), dt * a, exact=True)
    acum_t = acum.T
    last = _sub_row(acum, BLK - 1)
    heads = []
    for h in range(SSD_HEADS):
        col = _lane_col(acum, h)
        seg = jnp.where(tri, col - _sub_row(acum_t, h), -1e30)
        heads.append(dict(col=col, dm=jnp.exp(seg), dt=_lane_col(dt, h), last=_lane_col(last, h)))
    return tri, pre, dt, a, heads


def _pair_mix(lo_mask, v0, v1):
    return jnp.where(lo_mask, v0, v1)


def ssd_fwd(u, z, dtr, dtb, alog, dsk, nw, sides=()):
    s = u.shape[0]
    nc = s // BLK

    def body(u_ref, z_ref, dt_ref, dtb_ref, alog_ref, dsk_ref, nw_ref, y_ref, st_ref, s_ref):
        @pl.when(pl.program_id(0) == 0)
        def _():
            s_ref[...] = jnp.zeros_like(s_ref)

        _, _, _, _, heads = _ssd_chunk(u_ref, dt_ref, dtb_ref, alog_ref)
        lo_lane = _iota((BLK, LANE), 1) < HEAD_DIM
        lo_sub = _iota((BLK, LANE), 0) < HEAD_DIM
        ys = []
        for p in range(N_PAIR):
            g = p // 2
            h0, h1 = heads[2 * p], heads[2 * p + 1]
            bg = u_ref[:, B_LO + g * D_STATE:B_LO + (g + 1) * D_STATE]
            cg = u_ref[:, C_LO + g * D_STATE:C_LO + (g + 1) * D_STATE]
            xs = u_ref[:, p * LANE:(p + 1) * LANE]
            xp = xs * _pair_mix(lo_lane, h0["dt"], h1["dt"])
            gm = dot_nt(cg, bg)
            yd = _pair_mix(lo_lane, dot_nn(gm * h0["dm"], xp), dot_nn(gm * h1["dm"], xp))
            sp = s_ref[p]
            st_ref[0, p] = sp
            yo = _pair_mix(lo_lane, jnp.exp(h0["col"]), jnp.exp(h1["col"])) * dot_nt(cg, sp)
            dskp = _pair_mix(lo_lane, _lane_col(dsk_ref[...], 2 * p), _lane_col(dsk_ref[...], 2 * p + 1))
            ys.append(yd + yo + xs * dskp)
            wp = _pair_mix(lo_lane, jnp.exp(h0["last"] - h0["col"]), jnp.exp(h1["last"] - h1["col"]))
            el = _pair_mix(lo_sub, jnp.exp(h0["last"]), jnp.exp(h1["last"]))
            s_ref[p] = el * sp + dot_tn(wp * xp, bg)
        y = jnp.concatenate(ys, axis=1)
        zv = z_ref[...]
        y_ref[...] = _rms_fwd(y * zv * _sigmoid(zv), nw_ref[...])

    vec = _full_spec((1, LANE))
    return hosted_call(
        body, sides, name="ssd_fwd", grid=(nc,),
        in_specs=[_row_spec(BLK, CONV_DIM), _row_spec(BLK, SSD_WIDTH), _row_spec(BLK, LANE), vec, vec, vec,
                  _full_spec((1, SSD_WIDTH))],
        out_specs=[_row_spec(BLK, SSD_WIDTH), pl.BlockSpec((1, N_PAIR, LANE, D_STATE), lambda c: (c, 0, 0, 0))],
        out_shape=[jax.ShapeDtypeStruct((s, SSD_WIDTH), F32), jax.ShapeDtypeStruct((nc, N_PAIR, LANE, D_STATE), F32)],
        scratch_shapes=[pltpu.VMEM((N_PAIR, LANE, D_STATE), F32)],
        compiler_params=_params(1), operands=(u, z, dtr, dtb, alog, dsk, nw))


def ssd_bwd(u, z, dtr, st, dyo, dtb, alog, dsk, nw, sides=()):
    s = u.shape[0]
    nc = s // BLK

    def body(u_ref, z_ref, dt_ref, st_ref, dyo_ref, dtb_ref, alog_ref, dsk_ref, nw_ref,
             du_ref, dz_ref, ddt_ref, ddtb_ref, dalog_ref, ddsk_ref, dnw_ref, ds_ref):
        first = pl.program_id(0) == 0

        @pl.when(first)
        def _():
            ds_ref[...] = jnp.zeros_like(ds_ref)

        tri, pre, dt, a, heads = _ssd_chunk(u_ref, dt_ref, dtb_ref, alog_ref)
        shape = (BLK, LANE)
        lane = _iota(shape, 1)
        lo_lane = lane < HEAD_DIM
        lo_sub = _iota(shape, 0) < HEAD_DIM
        pairs = []
        ys = []
        for p in range(N_PAIR):
            g = p // 2
            h0, h1 = heads[2 * p], heads[2 * p + 1]
            bg = u_ref[:, B_LO + g * D_STATE:B_LO + (g + 1) * D_STATE]
            cg = u_ref[:, C_LO + g * D_STATE:C_LO + (g + 1) * D_STATE]
            xs = u_ref[:, p * LANE:(p + 1) * LANE]
            dtp = _pair_mix(lo_lane, h0["dt"], h1["dt"])
            xp = xs * dtp
            gm = dot_nt(cg, bg)
            m0, m1 = gm * h0["dm"], gm * h1["dm"]
            sp = st_ref[0, p]
            eap = _pair_mix(lo_lane, jnp.exp(h0["col"]), jnp.exp(h1["col"]))
            yo = eap * dot_nt(cg, sp)
            dskp = _pair_mix(lo_lane, _lane_col(dsk_ref[...], 2 * p), _lane_col(dsk_ref[...], 2 * p + 1))
            ys.append(_pair_mix(lo_lane, dot_nn(m0, xp), dot_nn(m1, xp)) + yo + xs * dskp)
            pairs.append(dict(bg=bg, cg=cg, xs=xs, dtp=dtp, xp=xp, gm=gm, m=(m0, m1), sp=sp, eap=eap, yo=yo, dskp=dskp))
        y = jnp.concatenate(ys, axis=1)
        zv = z_ref[...]
        sz = _sigmoid(zv)
        gate = zv * sz
        dyg, dnw = _rms_bwd(y * gate, nw_ref[...], dyo_ref[...])
        _acc(dnw_ref, first, dnw)
        dy = dyg * gate
        dz_ref[...] = (dyg * y * (sz * (1.0 + zv * (1.0 - sz)))).astype(dz_ref.dtype)

        zeros = jnp.zeros(shape, F32)
        dacum_col = zeros
        dacum_row = zeros
        ddt = zeros
        ddsk = jnp.zeros((1, LANE), F32)
        dlast = jnp.zeros((1, LANE), F32)
        head_row = _iota((1, LANE), 1)
        sub = _iota(shape, 0)
        db = [zeros, zeros]
        dc = [zeros, zeros]
        for p in range(N_PAIR):
            g = p // 2
            q = pairs[p]
            dyp = dy[:, p * LANE:(p + 1) * LANE]
            dsn = ds_ref[p]
            t = dyp * q["xs"]
            dxs = dyp * q["dskp"]
            dcs = dyp * q["eap"]
            dc[g] = dc[g] + dot_nn(dcs, q["sp"])
            dsp = dot_tn(dcs, q["cg"])
            dea = dyp * q["yo"]
            elp = _pair_mix(lo_sub, jnp.exp(heads[2 * p]["last"]), jnp.exp(heads[2 * p + 1]["last"]))
            dsp = dsp + elp * dsn
            dels = dsn * q["sp"] * elp
            wp = _pair_mix(lo_lane, jnp.exp(heads[2 * p]["last"] - heads[2 * p]["col"]),
                           jnp.exp(heads[2 * p + 1]["last"] - heads[2 * p + 1]["col"]))
            dv = dot_nt(q["bg"], dsn)
            db[g] = db[g] + dot_nn(wp * q["xp"], dsn)
            dxp = dv * wp
            dwv = dv * q["xp"] * wp
            dgm = zeros
            for k in range(2):
                h = 2 * p + k
                mine = lo_lane if k == 0 else jnp.logical_not(lo_lane)
                mine_sub = lo_sub if k == 0 else jnp.logical_not(lo_sub)
                dyh = jnp.where(mine, dyp, 0.0)
                dm = dot_nt(dyh, q["xp"])
                dxp = dxp + dot_tn(q["m"][k], dyh)
                dgm = dgm + dm * heads[h]["dm"]
                e = dm * q["m"][k]
                onehot = lane == h
                dw_col = jnp.sum(jnp.where(mine, dwv, 0.0), axis=1, keepdims=True)
                col = (jnp.sum(e, axis=1, keepdims=True) + jnp.sum(jnp.where(mine, dea, 0.0), axis=1, keepdims=True)
                       - dw_col)
                dacum_col = dacum_col + jnp.where(onehot, col, 0.0)
                dacum_row = dacum_row - jnp.where(sub == h, jnp.sum(e, axis=0, keepdims=True), 0.0)
                dl = jnp.sum(dw_col) + jnp.sum(jnp.where(mine_sub, dels, 0.0))
                dlast = dlast + jnp.where(head_row == h, dl, 0.0)
                ddsk = ddsk + jnp.where(head_row == h, jnp.sum(jnp.where(mine, t, 0.0)), 0.0)
            dc[g] = dc[g] + dot_nn(dgm, q["bg"])
            db[g] = db[g] + dot_tn(dgm, q["cg"])
            dxs = dxs + dxp * q["dtp"]
            tt = dxp * q["xs"]
            for k in range(2):
                mine = lo_lane if k == 0 else jnp.logical_not(lo_lane)
                ddt = ddt + jnp.where(lane == 2 * p + k, jnp.sum(jnp.where(mine, tt, 0.0), axis=1, keepdims=True), 0.0)
            du_ref[:, p * LANE:(p + 1) * LANE] = dxs
            ds_ref[p] = dsp
        for g in range(2):
            du_ref[:, B_LO + g * D_STATE:B_LO + (g + 1) * D_STATE] = db[g]
            du_ref[:, C_LO + g * D_STATE:C_LO + (g + 1) * D_STATE] = dc[g]
        dacum = dacum_col + dacum_row.T + jnp.where(sub == BLK - 1, dlast, 0.0)
        dda = dot_tn(tri.astype(F32), dacum, exact=True)
        ddt = ddt + dda * a
        _acc(dalog_ref, first, jnp.sum(dda * dt, axis=0, keepdims=True) * a)
        dpre = ddt * _sigmoid(pre)
        ddt_ref[...] = dpre.astype(ddt_ref.dtype)
        _acc(ddtb_ref, first, jnp.sum(dpre, axis=0, keepdims=True))
        _acc(ddsk_ref, first, ddsk)

    rev = lambda i: (nc - 1 - i, 0)
    vec = _full_spec((1, LANE))
    rows = lambda n: pl.BlockSpec((BLK, n), rev)
    return hosted_call(
        body, sides, name="ssd_bwd", grid=(nc,),
        in_specs=[rows(CONV_DIM), rows(SSD_WIDTH), rows(LANE),
                  pl.BlockSpec((1, N_PAIR, LANE, D_STATE), lambda i: (nc - 1 - i, 0, 0, 0)), rows(SSD_WIDTH),
                  vec, vec, vec, _full_spec((1, SSD_WIDTH))],
        out_specs=[rows(CONV_DIM), rows(SSD_WIDTH), rows(LANE), vec, vec, vec, _full_spec((1, SSD_WIDTH))],
        out_shape=[jax.ShapeDtypeStruct((s, CONV_DIM), F32), jax.ShapeDtypeStruct((s, SSD_WIDTH), BF16),
                   jax.ShapeDtypeStruct((s, LANE), BF16)] + [jax.ShapeDtypeStruct((1, LANE), F32)] * 3
        + [jax.ShapeDtypeStruct((1, SSD_WIDTH), F32)],
        scratch_shapes=[pltpu.VMEM((N_PAIR, LANE, D_STATE), F32)],
        compiler_params=_params(1), operands=(u, z, dtr, st, dyo, dtb, alog, dsk, nw))


SB_PAIRS = SB_WIDTH // LANE
SB_SCALE = HEAD_DIM ** -0.5


SB_TQ = 256


def _sb_tq(s):
    return min(SB_TQ, s)


def _sb_stack(x):
    lo_lane = _iota(x.shape, 1) < HEAD_DIM
    return jnp.concatenate([jnp.where(lo_lane, x, 0.0), jnp.where(lo_lane, 0.0, x)], axis=0)


def _sb_unstack(x2):
    tq = x2.shape[0] // 2
    lo_lane = _iota((tq, LANE), 1) < HEAD_DIM
    return jnp.where(lo_lane, x2[:tq], x2[tq:])


def _sb_logits(q2, kj, row0, col0, masked):
    shape = (q2.shape[0], BLK)
    tq = shape[0] // 2
    z = dot_nt(q2, kj)
    t = jnp.log(1.0 + jnp.exp(-jnp.abs(z)))
    ls = jnp.minimum(z, 0.0) - t
    lk = jnp.minimum(-z, 0.0) - t
    if not masked:
        return None, ls, lk
    row = _iota(shape, 0)
    valid = (col0 + _iota(shape, 1)) < (row0 + jnp.where(row < tq, row, row - tq))
    return valid, ls, jnp.where(valid, lk, 0.0)


def _sb_where(valid, x):
    return x if valid is None else jnp.where(valid, x, 0.0)


def _sb_win(x2, lo):
    if lo == 0:
        return x2
    tq = x2.shape[0] // 2
    return jnp.concatenate([x2[lo:tq], x2[tq + lo:]], axis=0)


def _sb_unwin(x2, xw, lo):
    if lo == 0:
        return xw
    tq = x2.shape[0] // 2
    return jnp.concatenate([x2[:lo], xw[:tq - lo], x2[tq:tq + lo], xw[tq - lo:]], axis=0)


def _sb_add(ref, val, lo):
    if lo == 0:
        ref[...] += val
    else:
        tq = ref.shape[0] // 2
        ref[lo:tq, :] += val[:tq - lo]
        ref[tq + lo:, :] += val[tq - lo:]


def _sums(x, mask2, parts):
    acc = None
    rest = x
    for _ in range(parts):
        term = rest.astype(BF16)
        rest = rest - term.astype(F32)
        d = lax.dot_general(term, mask2, (((1,), (0,)), ((), ())), preferred_element_type=F32)
        acc = d if acc is None else acc + d
    return acc[:, :BLK], acc[:, BLK:]


def _mask2(cond):
    return jnp.concatenate([cond.astype(BF16), jnp.ones(cond.shape, BF16)], axis=1)


def _sb_specs(s):
    tq = _sb_tq(s)
    qspec = pl.BlockSpec((tq, LANE), lambda p, i: (i, p))
    kspec = pl.BlockSpec((s, LANE), lambda p, i: (0, SB_PAIRS + p))
    vspec = pl.BlockSpec((s, LANE), lambda p, i: (0, 2 * SB_PAIRS + p))
    return qspec, kspec, vspec


SB_FLOOR = -104.0


def sb_fwd(qkv, sides=()):
    s = qkv.shape[0]
    tq = _sb_tq(s)
    kpq = tq // BLK

    def body(q_ref, k_ref, v_ref, o_ref, t_ref, n_ref, acc_ref):
        qi = pl.program_id(1)
        q2 = _sb_stack(q_ref[...] * SB_SCALE).astype(BF16)
        later = _mask2(_iota((BLK, BLK), 0) > _iota((BLK, BLK), 1))
        acc_ref[...] = jnp.zeros_like(acc_ref)

        def step(j, r, masked, lo=0):
            rows = pl.ds(pl.multiple_of(j * BLK, BLK), BLK)
            rw = _sb_win(r, lo)
            valid, ls, lk = _sb_logits(_sb_win(q2, lo), k_ref[rows, :], qi * tq + lo, j * BLK, masked)
            after, total = _sums(lk, later, 2)
            w = _sb_where(valid, jnp.exp(ls + rw + after))
            _sb_add(acc_ref, dot_nn(w, v_ref[rows, :]), lo)
            return _sb_unwin(r, rw + total, lo)

        r = jnp.zeros((2 * tq, LANE), F32)
        for d in reversed(range(kpq)):
            r = step(kpq * qi + d, r, True, d * BLK)

        def tile(g, r):
            for d in reversed(range(kpq)):
                r = step(kpq * (qi - 1 - g) + d, r, False)
            return r

        n, r = lax.while_loop(lambda c: jnp.logical_and(c[0] < qi, jnp.max(c[1]) > SB_FLOOR),
                              lambda c: (c[0] + 1, tile(c[0], c[1])), (jnp.int32(0), r))
        o_ref[...] = _sb_unstack(acc_ref[...])
        t_ref[...] = jnp.concatenate([r[:tq], r[tq:]], axis=1)
        n_ref[...] = jnp.zeros(n_ref.shape, F32) + n.astype(F32)

    return hosted_call(
        body, sides, name="sb_fwd", grid=(SB_PAIRS, s // tq),
        in_specs=list(_sb_specs(s)),
        out_specs=[pl.BlockSpec((tq, LANE), lambda p, i: (i, p)), pl.BlockSpec((tq, 2 * LANE), lambda p, i: (i, p)),
                   pl.BlockSpec((None, None, 8, LANE), lambda p, i: (p, i, 0, 0))],
        out_shape=[jax.ShapeDtypeStruct((s, SB_WIDTH), F32), jax.ShapeDtypeStruct((s, 2 * SB_WIDTH), F32),
                   jax.ShapeDtypeStruct((SB_PAIRS, s // tq, 8, LANE), F32)],
        scratch_shapes=[pltpu.VMEM((2 * tq, LANE), F32)],
        compiler_params=_params(2), operands=(qkv, qkv, qkv))


def sb_bwd(qkv, tot, swept, do, do_col=0, sides=()):
    s = qkv.shape[0]
    tq = _sb_tq(s)
    kpq = tq // BLK

    def body(q_ref, k_ref, v_ref, t_ref, n_ref, do_ref, dq_ref, dk_ref, dv_ref, acc_ref):
        qi = pl.program_id(1)
        n = jnp.clip(jnp.max(n_ref[...]).astype(jnp.int32), 0, qi)
        q2 = _sb_stack(q_ref[...] * SB_SCALE).astype(BF16)
        do2 = _sb_stack(do_ref[...]).astype(BF16)
        tot2 = jnp.concatenate([t_ref[:, :LANE], t_ref[:, LANE:]], axis=0)
        sq = (BLK, BLK)
        later = _mask2(_iota(sq, 0) > _iota(sq, 1))
        before = _mask2(_iota(sq, 0) < _iota(sq, 1))
        acc_ref[...] = jnp.zeros_like(acc_ref)

        @pl.when(qi == 0)
        def _():
            dk_ref[...] = jnp.zeros_like(dk_ref)
            dv_ref[...] = jnp.zeros_like(dv_ref)

        def step(j, carry, masked, lo=0):
            rc, fc = carry
            rows = pl.ds(pl.multiple_of(j * BLK, BLK), BLK)
            kj = k_ref[rows, :]
            vj = v_ref[rows, :]
            qw, dow, fw = _sb_win(q2, lo), _sb_win(do2, lo), _sb_win(fc, lo)
            valid, ls, lk = _sb_logits(qw, kj, qi * tq + lo, j * BLK, masked)
            after, total = _sums(lk, later, 2)
            rw = _sb_win(rc, lo) - total
            w = _sb_where(valid, jnp.exp(ls + rw + after))
            e = w * dot_nt(dow, vj)
            f_in, f_tot = _sums(e, before, 2)
            sg = jnp.exp(ls)
            dz = _sb_where(valid, e * (1.0 - sg) - (fw + f_in) * sg)
            _sb_add(acc_ref, dot_nn(dz, kj), lo)
            dk_ref[rows, :] += dot_tn(dz, qw)
            dv_ref[rows, :] += dot_tn(w, dow)
            return _sb_unwin(rc, rw, lo), _sb_unwin(fc, fw + f_tot, lo)

        def tile(g, carry):
            for d in range(kpq):
                carry = step(kpq * g + d, carry, False)
            return carry

        carry = lax.fori_loop(qi - n, qi, tile, (tot2, jnp.zeros((2 * tq, LANE), F32)))
        for d in range(kpq):
            carry = step(kpq * qi + d, carry, True, d * BLK)
        dq_ref[...] = (SB_SCALE * _sb_unstack(acc_ref[...])).astype(dq_ref.dtype)

    qspec, kspec, vspec = _sb_specs(s)
    blk = pl.BlockSpec((tq, LANE), lambda p, i: (i, p))
    acc = pl.BlockSpec((s, LANE), lambda p, i: (0, p))
    return hosted_call(
        body, sides, name="sb_bwd", grid=(SB_PAIRS, s // tq),
        in_specs=[qspec, kspec, vspec, pl.BlockSpec((tq, 2 * LANE), lambda p, i: (i, p)),
                  pl.BlockSpec((None, None, 8, LANE), lambda p, i: (p, i, 0, 0)),
                  pl.BlockSpec((tq, LANE), lambda p, i: (i, do_col + p))],
        out_specs=[blk, acc, acc],
        out_shape=[jax.ShapeDtypeStruct((s, SB_WIDTH), BF16)] + [jax.ShapeDtypeStruct((s, SB_WIDTH), F32)] * 2,
        scratch_shapes=[pltpu.VMEM((2 * tq, LANE), F32)],
        compiler_params=_params(2), operands=(qkv, qkv, qkv, tot, swept, do))


POOL_GROUP_DIM = POOL_WIDTH // len(POOL_WINDOWS)


assert all(w == 2 ** (i + 1) for i, w in enumerate(POOL_WINDOWS))


def _pool_inv(c):
    group = _iota((BLK, POOL_WIDTH), 1) // POOL_GROUP_DIM
    pos = c * BLK + _iota((BLK, POOL_WIDTH), 0)
    win = jnp.zeros((BLK, POOL_WIDTH), jnp.int32)
    for gi, wn in enumerate(POOL_WINDOWS):
        win = jnp.where(group == gi, wn, win)
    return 1.0 / jnp.minimum(pos + 1, win).astype(F32)


def _window_sums(ext, trailing):
    group = _iota(ext.shape, 1) // POOL_GROUP_DIM
    acc = ext
    out = None
    for gi in range(len(POOL_WINDOWS)):
        shift = 2 ** gi
        acc = acc + pltpu.roll(acc, shift if trailing else ext.shape[0] - shift, 0)
        out = acc if out is None else jnp.where(group == gi, acc, out)
    return out


def _pool_pooled(ext, cur, inv):
    return _window_sums(ext, True)[BLK:] * inv - cur


def pool_fwd(p, wblk, pb, ps):
    s, n = p.shape

    def body(cur_ref, prev_ref, w_ref, pb_ref, ps_ref, o_ref):
        c = pl.program_id(0)
        cur = cur_ref[...]
        prev = jnp.where(c > 0, prev_ref[...], 0.0)
        pooled = _pool_pooled(jnp.concatenate([prev, cur], axis=0), cur, _pool_inv(c))
        o_ref[...] = (dot_nn(pooled, w_ref[...]) + pb_ref[...]) * ps_ref[...]

    return pl.pallas_call(
        body, name="pool_fwd", grid=(s // BLK,),
        in_specs=[pl.BlockSpec((BLK, n), lambda c: (c, 0)), pl.BlockSpec((BLK, n), lambda c: (jnp.maximum(c - 1, 0), 0)),
                  _full_spec((n, n)), _full_spec((1, n)), _full_spec((1, n))],
        out_specs=pl.BlockSpec((BLK, n), lambda c: (c, 0)), out_shape=jax.ShapeDtypeStruct((s, n), F32),
        compiler_params=_params(1),
    )(p, p, wblk, pb, ps)


def pool_bwd(p, wblk, pb, ps, dout, do_col=0, sides=()):
    s, n = p.shape
    nb = s // BLK

    def body(cur_ref, prev_ref, w_ref, pb_ref, ps_ref, do_ref, dp_ref, dw_ref, dpb_ref, dps_ref, carry_ref):
        i = pl.program_id(0)
        c = nb - 1 - i
        first = i == 0
        cur = cur_ref[...]
        prev = jnp.where(c > 0, prev_ref[...], 0.0)
        inv = _pool_inv(c)
        pooled = _pool_pooled(jnp.concatenate([prev, cur], axis=0), cur, inv)
        mixed = dot_nn(pooled, w_ref[...]) + pb_ref[...]
        dov = do_ref[...]
        dmixed = dov * ps_ref[...]
        _acc(dps_ref, first, jnp.sum(dov * mixed, axis=0, keepdims=True))
        _acc(dpb_ref, first, jnp.sum(dmixed, axis=0, keepdims=True))
        _acc(dw_ref, first, dot_tn(pooled, dmixed))
        dpooled = dot_nt(dmixed, w_ref[...])
        dext = _window_sums(jnp.concatenate([jnp.zeros((BLK, n), F32), dpooled * inv], axis=0), False)

        @pl.when(first)
        def _():
            carry_ref[...] = jnp.zeros_like(carry_ref)

        dp_ref[...] = (dext[BLK:] - dpooled + carry_ref[...]).astype(dp_ref.dtype)
        carry_ref[...] = dext[:BLK]

    rev = lambda i: (nb - 1 - i, 0)
    return hosted_call(
        body, sides, name="pool_bwd", grid=(nb,),
        in_specs=[pl.BlockSpec((BLK, n), rev), pl.BlockSpec((BLK, n), lambda i: (jnp.maximum(nb - 2 - i, 0), 0)),
                  _full_spec((n, n)), _full_spec((1, n)), _full_spec((1, n)),
                  pl.BlockSpec((BLK, n), lambda i: (nb - 1 - i, do_col))],
        out_specs=[pl.BlockSpec((BLK, n), rev), _full_spec((n, n)), _full_spec((1, n)), _full_spec((1, n))],
        out_shape=[jax.ShapeDtypeStruct((s, n), BF16), jax.ShapeDtypeStruct((n, n), F32),
                   jax.ShapeDtypeStruct((1, n), F32), jax.ShapeDtypeStruct((1, n), F32)],
        scratch_shapes=[pltpu.VMEM((BLK, n), F32)],
        compiler_params=_params(1), operands=(p, p, wblk, pb, ps, dout))


def _row_tile(rows):
    if rows <= 512:
        return rows
    for t in (512, 256, 128, 64, 32, 16, 8):
        if rows % t == 0:
            return t
    return rows


def adamw(w, g, m, v):
    n, rows, cols = w.shape
    tr = _row_tile(rows)

    def body(w_ref, g_ref, m_ref, v_ref, d_ref, nm_ref, nv_ref):
        d_ref[...], nm_ref[...], nv_ref[...] = _adamw_math(w_ref[...], g_ref[...], m_ref[...], v_ref[...])

    spec = pl.BlockSpec((1, tr, cols), lambda i, j: (i, j, 0))
    return pl.pallas_call(
        body, name="adamw", grid=(n, rows // tr), in_specs=[spec] * 4, out_specs=[spec] * 3,
        out_shape=[jax.ShapeDtypeStruct(w.shape, F32)] * 3, compiler_params=_params(2),
    )(w, g, m, v)


def _adamw_math(w, g, m, v):
    nm = ADAM_B1 * m + (1.0 - ADAM_B1) * g
    nv = ADAM_B2 * v + (1.0 - ADAM_B2) * (g * g)
    m_hat = nm / (1.0 - ADAM_B1 ** ADAM_STEP)
    v_hat = nv / (1.0 - ADAM_B2 ** ADAM_STEP)
    return -ADAM_LR * (m_hat / (jnp.sqrt(v_hat) + ADAM_EPS) + ADAM_WD * w), nm, nv


def adamw_small(ws, gs, ms, vs):
    n = len(ws)

    def body(*refs):
        for i in range(n):
            outs = _adamw_math(*[refs[k * n + i][...] for k in range(4)])
            for k in range(3):
                refs[(4 + k) * n + i][...] = outs[k]

    vm = pl.BlockSpec(memory_space=pltpu.VMEM)
    outs = pl.pallas_call(
        body, name="adamw_small", in_specs=[vm] * (4 * n), out_specs=[vm] * (3 * n),
        out_shape=[jax.ShapeDtypeStruct(w.shape, F32) for w in ws] * 3,
    )(*ws, *gs, *ms, *vs)
    return outs[:n], outs[n:2 * n], outs[2 * n:]


def slab_sum(srcs, n_out, out_dtype, into=None, slot=0):
    _, rows, cols = srcs[0][0].shape
    tr = _row_tile(rows)
    n_src = len(srcs)
    sel = jnp.stack([jnp.asarray(base, jnp.int32) for _, base, _ in srcs])

    def body(sel_ref, *refs):
        acc = refs[0][...].astype(F32)
        for r in refs[1:n_src]:
            acc = acc + r[...].astype(F32)
        refs[-1][...] = acc.astype(out_dtype)

    def in_spec(k, step):
        return pl.BlockSpec((None, tr, cols), lambda o, i, sel_ref: (sel_ref[k] + step * o, i, 0))

    shape = (n_out, rows, cols) if into is None else into.shape
    return pl.pallas_call(
        body, name="slab_sum",
        grid_spec=pltpu.PrefetchScalarGridSpec(
            num_scalar_prefetch=1, grid=(n_out, rows // tr),
            in_specs=[in_spec(k, step) for k, (_, _, step) in enumerate(srcs)] + ([] if into is None else [ANY]),
            out_specs=pl.BlockSpec((None, tr, cols), lambda o, i, sel_ref: (slot + o, i, 0))),
        out_shape=jax.ShapeDtypeStruct(shape, out_dtype), compiler_params=_params(2),
        input_output_aliases={} if into is None else {1 + n_src: 0},
    )(sel, *[a for a, _, _ in srcs], *([] if into is None else [into]))


ICI_FLIPS = ((1, 0, 0), (0, 1, 0), (1, 1, 0))
D2D_FLIPS = ((0, 0, 1),)
ANY = pl.BlockSpec(memory_space=pl.ANY)


def _me():
    return lax.axis_index("x"), lax.axis_index("y"), lax.axis_index("c")


def _flipped(me, flip):
    return tuple(1 - m if f else m for m, f in zip(me, flip))


def _chip(dev):
    return 2 * dev[0] + dev[1]


def _dev(dev):
    return 4 * dev[0] + 2 * dev[1] + dev[2]


N_CHIP = 4
D2D = (0, 0, 1)


class Exchange:
    def __init__(self, xs, n_out, copies, own=None, in_place=False):
        self.xs, self.copies, self.own, self.in_place = list(xs), copies, own, in_place
        self.n_arr, self.n_cp = len(self.xs), len(copies)
        self.out_shape = [jax.ShapeDtypeStruct((n_out,) + x.shape[1:], x.dtype) for x in self.xs]
        self.scratch = [pltpu.SemaphoreType.DMA((self.n_arr * self.n_cp,)),
                        pltpu.SemaphoreType.DMA((self.n_arr * self.n_cp,)), pltpu.SemaphoreType.DMA((self.n_arr,))]

    def _own(self, x_refs, o_refs, sems, me):
        if self.own is None:
            return []
        return [pltpu.make_async_copy(x_refs[a].at[self.own[0](me)], o_refs[a].at[self.own[1](me)], sems[2].at[a])
                for a in range(self.n_arr)]

    def _copy(self, x_refs, o_refs, sems, me, a, j, sender):
        flip, src_slot, dst_slot = self.copies[j]
        k = a * self.n_cp + j
        return pltpu.make_async_remote_copy(
            src_ref=x_refs[a].at[src_slot(me)], dst_ref=o_refs[a].at[dst_slot(sender)],
            send_sem=sems[0].at[k], recv_sem=sems[1].at[k], device_id=_flipped(me, flip), device_id_type=MESH)

    def start(self, x_refs, o_refs, sems):
        me = _me()
        for cp in self._own(x_refs, o_refs, sems, me):
            cp.start()
        for j in range(self.n_cp):
            for a in range(self.n_arr):
                self._copy(x_refs, o_refs, sems, me, a, j, me).start()

    def wait(self, x_refs, o_refs, sems):
        me = _me()
        for j in range(self.n_cp):
            for a in range(self.n_arr):
                self._copy(x_refs, o_refs, sems, me, a, j, _flipped(me, self.copies[j][0])).wait_recv()
        for j in range(self.n_cp):
            for a in range(self.n_arr):
                self._copy(x_refs, o_refs, sems, me, a, j, me).wait_send()
        for cp in self._own(x_refs, o_refs, sems, me):
            cp.wait()

    def run(self, name):
        n = self.n_arr

        def body(*refs):
            self.start(refs[:n], refs[n:2 * n], refs[2 * n:])
            self.wait(refs[:n], refs[n:2 * n], refs[2 * n:])

        return pl.pallas_call(
            body, name=name, in_specs=[ANY] * n, out_specs=[ANY] * n, out_shape=self.out_shape,
            input_output_aliases={a: a for a in range(n)} if self.in_place else {}, scratch_shapes=self.scratch,
        )(*self.xs)


def hosted_call(body, sides, *, name, grid, in_specs, out_specs, out_shape, scratch_shapes, compiler_params, operands):
    n_in, n_out, n_scr = len(in_specs), len(out_specs), len(scratch_shapes)
    live = [s for s in sides if s is not None]
    if not live:
        outs = pl.pallas_call(body, name=name, grid=grid, in_specs=in_specs, out_specs=out_specs, out_shape=out_shape,
                              scratch_shapes=scratch_shapes, compiler_params=compiler_params)(*operands)
        return outs, [None] * len(sides)
    n = sum(s.n_arr for s in live)
    lo = [sum(s.n_arr for s in live[:i]) for i in range(len(live))]

    def full_body(*refs):
        ins, sx = refs[:n_in], refs[n_in:n_in + n]
        outs, so = refs[n_in + n:n_in + n + n_out], refs[n_in + n + n_out:n_in + 2 * n + n_out]
        scr, sems = refs[n_in + 2 * n + n_out:n_in + 2 * n + n_out + n_scr], refs[n_in + 2 * n + n_out + n_scr:]
        first = functools.reduce(jnp.logical_and, [pl.program_id(a) == 0 for a in range(len(grid))])
        last = functools.reduce(jnp.logical_and, [pl.program_id(a) == g - 1 for a, g in enumerate(grid)])
        parts = [(s, sx[l:l + s.n_arr], so[l:l + s.n_arr], sems[3 * i:3 * i + 3]) for i, (s, l) in enumerate(zip(live, lo))]

        @pl.when(first)
        def _():
            for s, x, o, m in parts:
                s.start(x, o, m)

        body(*ins, *outs, *scr)

        @pl.when(last)
        def _():
            for s, x, o, m in parts:
                s.wait(x, o, m)

    aliases = {n_in + l + a: n_out + l + a for s, l in zip(live, lo) if s.in_place for a in range(s.n_arr)}
    outs = pl.pallas_call(
        full_body, name=name + "_x", grid=grid, in_specs=list(in_specs) + [ANY] * n,
        out_specs=list(out_specs) + [ANY] * n, out_shape=list(out_shape) + [o for s in live for o in s.out_shape],
        input_output_aliases=aliases,
        scratch_shapes=list(scratch_shapes) + [m for s in live for m in s.scratch], compiler_params=compiler_params,
    )(*operands, *[x for s in live for x in s.xs])
    side_outs = iter([outs[n_out + l:n_out + l + s.n_arr] for s, l in zip(live, lo)])
    return outs[:n_out], [next(side_outs) if s is not None else None for s in sides]


def gather_ici(shards):
    ici = [(f, lambda me: 0, _dev) for f in ICI_FLIPS]
    return Exchange([s[None] for s in shards], N_DEV, ici, (lambda me: 0, _dev))


def gather_d2d(blocks):
    d2d = [(D2D, (lambda me, k=k: 2 * k + me[2]), (lambda sender, k=k: 2 * k + sender[2])) for k in range(N_CHIP)]
    return Exchange(blocks, N_DEV, d2d, None, in_place=True)


def gathered(blocks):
    return [b.reshape(-1, b.shape[2]) for b in blocks]


def scatter_d2d(parts):
    d2d = [(D2D, (lambda me, k=k: 2 * k + 1 - me[2]), (lambda sender, k=k: k)) for k in range(N_CHIP)]
    return Exchange(parts, N_CHIP, d2d)


def chip_sums(parts, sib):
    c = _me()[2]
    return [slab_sum([(p, c, 2), (s, 0, 1)], N_CHIP, BF16) for p, s in zip(parts, sib)]


def scatter_ici(sums):
    ici = [(f, (lambda me, f=f: _chip(_flipped(me, f))), (lambda sender, i=i: i)) for i, f in enumerate(ICI_FLIPS)]
    return Exchange(sums, len(ICI_FLIPS), ici)


def device_sums(sums, got, into=None, slot=0):
    x, y, _ = _me()
    outs = [slab_sum([(cs, 2 * x + y, 0)] + [(g, i, 0) for i in range(len(ICI_FLIPS))], 1, F32,
                     None if into is None else into[a], slot) for a, (cs, g) in enumerate(zip(sums, got))]
    return outs if into is not None else [o[0] for o in outs]


def all_gather(shards):
    blocks = gather_ici(shards).run("gather_ici")
    return gathered(gather_d2d(blocks).run("gather_d2d"))


def reduce_scatter(parts, *into):
    sums = chip_sums(parts, scatter_d2d(parts).run("scatter_d2d"))
    return device_sums(sums, scatter_ici(sums).run("scatter_ici"), *into)


def all_reduce_small(v):
    flips = D2D_FLIPS + ICI_FLIPS[:2]

    def body(v_ref, o_ref, got_ref, send_sems, recv_sems):
        me = _me()
        o_ref[...] = v_ref[...]
        for i, flip in enumerate(flips):
            cp = pltpu.make_async_remote_copy(
                src_ref=o_ref, dst_ref=got_ref.at[i], send_sem=send_sems.at[i], recv_sem=recv_sems.at[i],
                device_id=_flipped(me, flip), device_id_type=MESH)
            cp.start()
            cp.wait()
            o_ref[...] = o_ref[...] + got_ref[i]

    vm = pl.BlockSpec(memory_space=pltpu.VMEM)
    return pl.pallas_call(
        body, name="all_reduce_small", in_specs=[vm], out_specs=vm, out_shape=jax.ShapeDtypeStruct(v.shape, F32),
        scratch_shapes=[pltpu.VMEM((len(flips),) + v.shape, F32), pltpu.SemaphoreType.DMA((len(flips),)),
                        pltpu.SemaphoreType.DMA((len(flips),))],
    )(v)


def _perm_rows(wt):
    pad = jnp.zeros((D_IN_PAD - D_IN_PROJ, wt.shape[1]), wt.dtype)
    return jnp.concatenate([wt[:DT_LO], wt[DT_HI:], wt[DT_LO:DT_HI], pad], axis=0)


def _unperm_rows(dwt):
    n = D_IN_PROJ - (DT_HI - DT_LO)
    return jnp.concatenate([dwt[:DT_LO], dwt[n:D_IN_PROJ], dwt[DT_LO:n]], axis=0)


def _pad_lanes(v):
    return jnp.pad(v, ((0, 0), (0, LANE - v.shape[1])))[:, None]


def _block_diag(w):
    l, g, n, _ = w.shape
    out = jnp.zeros((l, g * n, g * n), w.dtype)
    for i in range(g):
        out = out.at[:, i * n:(i + 1) * n, i * n:(i + 1) * n].set(w[:, i])
    return out


def _pack(groups):
    flat = []
    for grp in groups:
        parts = [a.reshape(-1) for a in (grp if isinstance(grp, (list, tuple)) else [grp])]
        n = sum(p.shape[0] for p in parts)
        if -n % LANE:
            parts.append(jnp.zeros((-n % LANE,), parts[0].dtype))
        flat += parts
    return jnp.concatenate(flat).reshape(-1, LANE)


def _unpack(buf, shapes):
    out = []
    lo = 0
    buf = buf.reshape(-1)
    for shp in shapes:
        n = 1
        for k in shp:
            n *= k
        out.append(buf[lo:lo + n].reshape(shp))
        lo += n + (-n % LANE)
    return out


def small_params(w, conv_w_full):
    return dict(
        n1w=w["norm1_w"][:, None], cw=jnp.pad(conv_w_full, ((0, 0), (0, 8 - CONV_WIDTH), (0, 0))),
        cb=w["conv_b"][:, None], dtb=_pad_lanes(w["dt_bias"]), alog=_pad_lanes(w["a_log"]), dsk=_pad_lanes(w["d_skip"]),
        snw=w["ssd_norm_w"][:, None], wblk=_block_diag(w["pool_w"]), pb=w["pool_b"].reshape(-1, 1, POOL_WIDTH),
        ps=w["pool_scale"][:, None], n2w=w["norm2_w"][:, None])


MIX = ("w_in", "w_out")
FFN = ("w_gate", "w_up", "w_down")


def layer_params(small, l):
    return {k: v[l] for k, v in small.items()}


def mix_weights(whole):
    return _perm_rows(whole[0]), whole[1]


def _slabs(g):
    return g.reshape(N_DEV, -1, g.shape[-1])


def _layer_fwd(x, p, mix, ffn=None, ffn_shards=None, next_mix_shards=None):
    ici = [gather_ici([sh]) for sh in ffn_shards] if ffn_shards is not None else [None] * len(FFN)
    (z, xbc, qkv, pp, dtr, h1), (blk_g,) = inproj_fwd(x, p["n1w"], mix[0], [ici[0]])
    (u,), _ = conv_fwd(xbc, p["cw"], p["cb"])
    (y_ssd, st), (blk_u,) = ssd_fwd(u, z, dtr, p["dtb"], p["alog"], p["dsk"], p["snw"], [ici[1]])
    (o, tot, swept), (blk_d,) = sb_fwd(qkv, [ici[2]])
    blocks = blk_g + blk_u + blk_d if ffn_shards is not None else None
    yp = pool_fwd(pp, p["wblk"], p["pb"], p["ps"])
    (x_mid, ycat), (blocks,) = outproj_fwd([y_ssd, o, yp], mix[1], x,
                                           [gather_d2d(blocks) if blocks is not None else None])
    if blocks is not None:
        ffn = gathered(blocks)
    (x_out, g, uu), (nxt,) = ffn_fwd(x_mid, p["n2w"], *ffn,
                                     [gather_ici(next_mix_shards) if next_mix_shards is not None else None])
    sv = dict(x=x, z=z, xbc=xbc, qkv=qkv, pp=pp, dtr=dtr, h1=h1, u=u, st=st, tot=tot, swept=swept, ycat=ycat,
              x_mid=x_mid, g=g, uu=uu, w_in=mix[0], w_out=mix[1], wg=ffn[0], wu=ffn[1], wd=ffn[2])
    return x_out, sv, nxt


def _layer_bwd(dxo, sv, p, pending_mix=None, exchange=False, into_ffn=(), into_mix=()):
    (dx_mid, dn2w, a, dg, du, h2), (sib,) = ffn_bwd(
        dxo, sv["x_mid"], sv["g"], sv["uu"], p["n2w"], sv["wg"], sv["wu"], sv["wd"],
        [scatter_d2d(pending_mix) if pending_mix is not None else None])
    sums_mix = chip_sums(pending_mix, sib) if pending_mix is not None else None
    dwg, dwu = mm_tn([dg, du], h2)
    gr = dict(norm2_w=dn2w[0], w_gate=dwg, w_up=dwu, w_down=mm_tn(a, dxo))
    parts = [_slabs(gr[k]) for k in FFN] if exchange else None
    (dycat,), (sib,) = outproj_bwd(dx_mid, sv["w_out"], [scatter_d2d(parts) if exchange else None])
    sums_ffn = chip_sums(parts, sib) if exchange else None
    gr["w_out"] = mm_tn(sv["ycat"], dx_mid)
    (dp, dwblk, dpb, dps), _ = pool_bwd(sv["pp"], p["wblk"], p["pb"], p["ps"], dycat,
                                        (SSD_WIDTH + SB_WIDTH) // POOL_WIDTH)
    n = POOL_GROUP_DIM
    gr["pool_w"] = jnp.stack([dwblk[i * n:(i + 1) * n, i * n:(i + 1) * n] for i in range(len(POOL_WINDOWS))])
    gr["pool_b"] = dpb.reshape(len(POOL_WINDOWS), n)
    gr["pool_scale"] = dps[0]
    (dq, dk, dv), (got_ud,) = sb_bwd(sv["qkv"], sv["tot"], sv["swept"], dycat, SSD_WIDTH // LANE,
                                     [scatter_ici(sums_ffn[1:]) if exchange else None])
    (du_, dz, ddtr, ddtb, dalog, ddsk, dsnw), (got_mix,) = ssd_bwd(
        sv["u"], sv["z"], sv["dtr"], sv["st"], dycat, p["dtb"], p["alog"], p["dsk"], p["snw"],
        [scatter_ici(sums_mix) if sums_mix is not None else None])
    done_mix = device_sums(sums_mix, got_mix, *into_mix) if sums_mix is not None else None
    gr.update(dt_bias=ddtb[0, :SSD_HEADS], a_log=dalog[0, :SSD_HEADS], d_skip=ddsk[0, :SSD_HEADS], ssd_norm_w=dsnw[0])
    (dxbc, dcw, dcb), (got_g,) = conv_bwd(sv["xbc"], p["cw"], p["cb"], du_,
                                          [scatter_ici(sums_ffn[:1]) if exchange else None])
    done_ffn = device_sums(sums_ffn, got_g + got_ud, *into_ffn) if exchange else None
    gr.update(conv_w=dcw[:CONV_WIDTH], conv_b=dcb[0])
    dx, dn1w, dproj = inproj_bwd([dz, dxbc, dq, dk, dv, dp, ddtr], sv["w_in"], sv["x"], p["n1w"], dx_mid)
    gr.update(norm1_w=dn1w[0], w_in=_unperm_rows(mm_tn(dproj, sv["h1"])))
    return dx, gr, done_ffn, done_mix


def local_step(x, tgt, params, weights, final_w):
    saved = []
    for p, (mix, ffn) in zip(params, weights):
        x, sv, _ = _layer_fwd(x, p, mix, ffn)
        saved.append(sv)
    loss, dx, dfw = head_loss(x, final_w[None], tgt)
    grads = []
    for p, sv in zip(reversed(params), reversed(saved)):
        dx, gr, _, _ = _layer_bwd(dx, sv, p)
        grads.append(gr)
    grads.reverse()
    return loss, dx, dfw[0], grads


WEIGHTS = ("norm1_w", "w_in", "conv_w", "conv_b", "dt_bias", "a_log", "d_skip", "ssd_norm_w", "pool_w", "pool_b",
           "pool_scale", "w_out", "norm2_w", "w_gate", "w_up", "w_down", "final_norm_w")
COL_SHARDED = ("w_in", "w_gate", "w_up")
ROW_SHARDED = ("w_out", "w_down")
SMALL = tuple(k for k in WEIGHTS if k not in COL_SHARDED + ROW_SHARDED)


def kernel(x, norm1_w, w_in, conv_w, conv_b, dt_bias, a_log, d_skip, ssd_norm_w, pool_w, pool_b, pool_scale, w_out, norm2_w, w_gate, w_up, w_down, final_norm_w, loss_target, m_norm1_w, m_w_in, m_conv_w, m_conv_b, m_dt_bias, m_a_log, m_d_skip, m_ssd_norm_w, m_pool_w, m_pool_b, m_pool_scale, m_w_out, m_norm2_w, m_w_gate, m_w_up, m_w_down, m_final_norm_w, v_norm1_w, v_w_in, v_conv_w, v_conv_b, v_dt_bias, v_a_log, v_d_skip, v_ssd_norm_w, v_pool_w, v_pool_b, v_pool_scale, v_w_out, v_norm2_w, v_w_gate, v_w_up, v_w_down, v_final_norm_w):
    args = dict(locals())
    w = {k: args[k] for k in WEIGHTS}
    m = {k: args["m_" + k] for k in WEIGHTS}
    v = {k: args["v_" + k] for k in WEIGHTS}
    depth = w_in.shape[0]
    dev = _dev(_me())
    n_cw = conv_w.shape[-1]

    shards = {k: (jnp.swapaxes(w[k], 1, 2) if k in COL_SHARDED else w[k]).astype(BF16) for k in MIX + FFN}
    whole = all_gather([jnp.swapaxes(conv_w, 0, 2).reshape(n_cw, -1)] + [shards[k][0] for k in MIX])
    conv_w_full = jnp.swapaxes(whole[0].reshape(N_DEV * n_cw, CONV_WIDTH, depth), 0, 2)
    small = small_params(w, conv_w_full)
    xs = x[0]
    params, saved = [layer_params(small, l) for l in range(depth)], []
    mix = mix_weights(whole[1:])
    for l in range(depth):
        xs, sv, nxt = _layer_fwd(xs, params[l], mix, ffn_shards=[shards[k][l] for k in FFN],
                                 next_mix_shards=[shards[k][l + 1] for k in MIX] if l + 1 < depth else None)
        saved.append(sv)
        if nxt is not None:
            mix = mix_weights(gathered(gather_d2d(nxt).run("gather_d2d")))
    loss, dx, dfw = head_loss(xs, final_norm_w[None], loss_target[0])
    layer_grads = [None] * depth
    native = {k: lax.empty((depth,) + shards[k].shape[1:], F32) for k in MIX + FFN}
    pending = None
    for l in reversed(range(depth)):
        dx, layer_grads[l], done_ffn, done_mix = _layer_bwd(
            dx, saved[l], params[l], pending, exchange=True, into_ffn=([native[k] for k in FFN], l),
            into_mix=([native[k] for k in MIX], l + 1))
        native.update(zip(FFN, done_ffn))
        if pending is not None:
            native.update(zip(MIX, done_mix))
        pending = [_slabs(layer_grads[l][k]) for k in MIX]
    native.update(zip(MIX, reduce_scatter(pending, [native[k] for k in MIX], 0)))
    grads = {k: jnp.swapaxes(native[k], 1, 2) if k in COL_SHARDED else native[k] for k in MIX + FFN}
    layered = [k for k in SMALL if k != "final_norm_w"]
    small_shapes = [(1, LANE)] + [(depth,) + layer_grads[0][k].shape for k in layered] + [dfw[0].shape]
    packed = _pack([loss] + [[layer_grads[l][k] for l in range(depth)] for k in layered] + [dfw[0]])
    summed = _unpack(all_reduce_small(packed), small_shapes)
    loss = summed[0][0, 0]
    grads.update(zip(layered + ["final_norm_w"], summed[1:]))
    grads["conv_w"] = lax.dynamic_slice_in_dim(grads["conv_w"], dev * n_cw, n_cw, axis=2)

    delta, new_m, new_v = {}, {}, {}
    for k in MIX + FFN:
        if k in ("w_gate", "w_up"):
            wt, mt, vt = (jnp.swapaxes(t, 1, 2) for t in (w[k], m[k], v[k]))
            delta[k], new_m[k], new_v[k] = [jnp.swapaxes(o, 1, 2) for o in adamw(wt, native[k], mt, vt)]
        else:
            delta[k], new_m[k], new_v[k] = adamw(w[k], grads[k], m[k], v[k])
    two_d = lambda a: a.reshape(1, -1) if a.ndim == 1 else a
    outs = adamw_small(*[[two_d(t[k]) for k in SMALL] for t in (w, grads, m, v)])
    for dst, arrs in zip((delta, new_m, new_v), outs):
        dst.update({k: a.reshape(w[k].shape) for k, a in zip(SMALL, arrs)})
    return (loss, dx[None], *[grads[k] for k in WEIGHTS], *[delta[k] for k in WEIGHTS],
            *[new_m[k] for k in WEIGHTS], *[new_v[k] for k in WEIGHTS])
```

```python
import functools

import jax
import jax.numpy as jnp
from jax import lax
from jax.experimental import pallas as pl
from jax.experimental.pallas import tpu as pltpu

F32 = jnp.float32
BF16 = jnp.bfloat16
HIGHEST = lax.Precision.HIGHEST
MESH = pl.DeviceIdType.MESH

EPS = 1e-6
D_MODEL = 1024
SSD_WIDTH = 512
SSD_HEADS = 8
HEAD_DIM = 64
D_STATE = 128
CONV_WIDTH = 4
CONV_DIM = 1024
SB_WIDTH = 256
POOL_WIDTH = 256
POOL_WINDOWS = (2, 4, 8, 16)
D_IN_PROJ = 2568
D_FF = 2816
N_DEV = 8
SEG = (512, 1024, 768, 256, 128)
D_IN_PAD = sum(SEG)
DT_LO, DT_HI = 1536, 1544

LANE = 128
BLK = 128
ROW_TILE = 256
VMEM_LIMIT = 56 * 2**20

ADAM_LR, ADAM_B1, ADAM_B2, ADAM_EPS, ADAM_WD, ADAM_STEP = 0.001, 0.9, 0.999, 1e-08, 0.01, 10


def _params(n_axes=1, vmem=None):
    return pltpu.CompilerParams(dimension_semantics=("arbitrary",) * n_axes, vmem_limit_bytes=vmem)


def _dot(a, b, dims, exact=False):
    if exact:
        return lax.dot_general(a.astype(F32), b.astype(F32), (dims, ((), ())), precision=HIGHEST,
                               preferred_element_type=F32)
    return lax.dot_general(a.astype(BF16), b.astype(BF16), (dims, ((), ())), preferred_element_type=F32)


def dot_nn(a, b, exact=False):
    return _dot(a, b, ((1,), (0,)), exact)


def dot_nt(a, b, exact=False):
    return _dot(a, b, ((1,), (1,)), exact)


def dot_tn(a, b, exact=False):
    return _dot(a, b, ((0,), (0,)), exact)


def _iota(shape, axis):
    return lax.broadcasted_iota(jnp.int32, shape, axis)


def _lane_col(x, h):
    return jnp.sum(jnp.where(_iota(x.shape, 1) == h, x, 0.0), axis=1, keepdims=True)


def _sub_row(x, h):
    return jnp.sum(jnp.where(_iota(x.shape, 0) == h, x, 0.0), axis=0, keepdims=True)


def _sigmoid(x):
    return 1.0 / (1.0 + jnp.exp(-x))


def _rms_fwd(x, w):
    r = lax.rsqrt(jnp.mean(x * x, axis=-1, keepdims=True) + EPS)
    return x * r * w


def _rms_bwd(x, w, dy):
    r = lax.rsqrt(jnp.mean(x * x, axis=-1, keepdims=True) + EPS)
    xh = x * r
    dxh = dy * w
    dx = r * (dxh - xh * jnp.mean(dxh * xh, axis=-1, keepdims=True))
    return dx, jnp.sum(dy * xh, axis=0, keepdims=True)


def _acc(ref, first, val):
    @pl.when(first)
    def _():
        ref[...] = val

    @pl.when(jnp.logical_not(first))
    def _():
        ref[...] += val


def _row_spec(tm, n):
    return pl.BlockSpec((tm, n), lambda i: (i, 0))


def _full_spec(shape):
    return pl.BlockSpec(shape, lambda *_: (0,) * len(shape))


def inproj_fwd(x, nw, w, sides=()):
    s, d = x.shape
    tm = min(ROW_TILE, s)
    seg_dtypes = (F32, F32, BF16, F32, F32)

    def body(x_ref, nw_ref, w_ref, z_ref, xbc_ref, qkv_ref, p_ref, dt_ref, h_ref):
        h = _rms_fwd(x_ref[...], nw_ref[...]).astype(BF16)
        h_ref[...] = h
        lo = 0
        for ref, n in zip((z_ref, xbc_ref, qkv_ref, p_ref, dt_ref), SEG):
            ref[...] = dot_nt(h, w_ref[lo:lo + n, :]).astype(ref.dtype)
            lo += n

    return hosted_call(
        body, sides, name="inproj_fwd", grid=(s // tm,),
        in_specs=[_row_spec(tm, d), _full_spec((1, d)), _full_spec(w.shape)],
        out_specs=[_row_spec(tm, n) for n in SEG] + [_row_spec(tm, d)],
        out_shape=[jax.ShapeDtypeStruct((s, n), t) for n, t in zip(SEG, seg_dtypes)]
        + [jax.ShapeDtypeStruct((s, d), BF16)],
        scratch_shapes=[], compiler_params=_params(1, VMEM_LIMIT), operands=(x, nw, w))


def inproj_bwd(pieces, w, x, nw, dres):
    s, d = x.shape
    tm = min(ROW_TILE, s)
    n_p = len(pieces)
    widths = [p.shape[1] for p in pieces]

    def body(*refs):
        w_ref, x_ref, nw_ref, dres_ref, dx_ref, dnw_ref, dp_ref = refs[n_p:]
        dh = None
        lo = 0
        for ref, n in zip(refs[:n_p], widths):
            piece = ref[...].astype(BF16)
            dp_ref[:, lo:lo + n] = piece
            part = dot_nn(piece, w_ref[lo:lo + n, :])
            dh = part if dh is None else dh + part
            lo += n
        dx, dnw = _rms_bwd(x_ref[...], nw_ref[...], dh)
        dx_ref[...] = dres_ref[...] + dx
        _acc(dnw_ref, pl.program_id(0) == 0, dnw)

    return pl.pallas_call(
        body, name="inproj_bwd", grid=(s // tm,),
        in_specs=[_row_spec(tm, n) for n in widths] + [_full_spec(w.shape), _row_spec(tm, d), _full_spec((1, d)),
                                                       _row_spec(tm, d)],
        out_specs=[_row_spec(tm, d), _full_spec((1, d)), _row_spec(tm, sum(widths))],
        out_shape=[jax.ShapeDtypeStruct((s, d), F32), jax.ShapeDtypeStruct((1, d), F32),
                   jax.ShapeDtypeStruct((s, sum(widths)), BF16)],
        compiler_params=_params(1, VMEM_LIMIT),
    )(*pieces, w, x, nw, dres)


def outproj_fwd(pieces, w, res, sides=()):
    s, d = res.shape
    tm = min(ROW_TILE, s)
    n_p = len(pieces)
    widths = [p.shape[1] for p in pieces]

    def body(*refs):
        w_ref, r_ref, o_ref, y_ref = refs[n_p:]
        acc = r_ref[...]
        lo = 0
        for ref, n in zip(refs[:n_p], widths):
            piece = ref[...].astype(BF16)
            y_ref[:, lo:lo + n] = piece
            acc = acc + dot_nn(piece, w_ref[lo:lo + n, :])
            lo += n
        o_ref[...] = acc

    return hosted_call(
        body, sides, name="outproj_fwd", grid=(s // tm,),
        in_specs=[_row_spec(tm, n) for n in widths] + [_full_spec(w.shape), _row_spec(tm, d)],
        out_specs=[_row_spec(tm, d), _row_spec(tm, sum(widths))],
        out_shape=[jax.ShapeDtypeStruct((s, d), F32), jax.ShapeDtypeStruct((s, sum(widths)), BF16)],
        scratch_shapes=[], compiler_params=_params(1, VMEM_LIMIT), operands=(*pieces, w, res))


def outproj_bwd(dx, w, sides=()):
    s, d = dx.shape
    tm = min(ROW_TILE, s)

    def body(dx_ref, w_ref, o_ref):
        o_ref[...] = dot_nt(dx_ref[...], w_ref[...])

    return hosted_call(
        body, sides, name="outproj_bwd", grid=(s // tm,),
        in_specs=[_row_spec(tm, d), _full_spec(w.shape)],
        out_specs=[_row_spec(tm, w.shape[0])], out_shape=[jax.ShapeDtypeStruct((s, w.shape[0]), F32)],
        scratch_shapes=[], compiler_params=_params(1, VMEM_LIMIT), operands=(dx, w))


def ffn_fwd(x, nw, wg, wu, wd, sides=()):
    s, d = x.shape
    f = wg.shape[0]
    tm = min(ROW_TILE, s)

    def body(x_ref, nw_ref, wg_ref, wu_ref, wd_ref, o_ref, g_ref, u_ref):
        xv = x_ref[...]
        h = _rms_fwd(xv, nw_ref[...]).astype(BF16)
        g = dot_nt(h, wg_ref[...])
        u = dot_nt(h, wu_ref[...])
        g_ref[...] = g.astype(BF16)
        u_ref[...] = u.astype(BF16)
        o_ref[...] = xv + dot_nn(g * _sigmoid(g) * u, wd_ref[...])

    return hosted_call(
        body, sides, name="ffn_fwd", grid=(s // tm,),
        in_specs=[_row_spec(tm, d), _full_spec((1, d)), _full_spec(wg.shape), _full_spec(wu.shape),
                  _full_spec(wd.shape)],
        out_specs=[_row_spec(tm, d), _row_spec(tm, f), _row_spec(tm, f)],
        out_shape=[jax.ShapeDtypeStruct((s, d), F32), jax.ShapeDtypeStruct((s, f), BF16),
                   jax.ShapeDtypeStruct((s, f), BF16)],
        scratch_shapes=[], compiler_params=_params(1, VMEM_LIMIT), operands=(x, nw, wg, wu, wd))


def ffn_bwd(dxo, x, g, u, nw, wg, wu, wd, sides=()):
    s, d = x.shape
    f = wg.shape[0]
    tm = min(ROW_TILE, s)

    def body(dxo_ref, x_ref, g_ref, u_ref, nw_ref, wg_ref, wu_ref, wd_ref, dx_ref, dnw_ref, a_ref, dg_ref,
             du_ref, h_ref):
        dxo_v = dxo_ref[...]
        xv = x_ref[...]
        da = dot_nt(dxo_v, wd_ref[...])
        gv = g_ref[...].astype(F32)
        uv = u_ref[...].astype(F32)
        sg = _sigmoid(gv)
        sl = gv * sg
        a_ref[...] = (sl * uv).astype(BF16)
        dg = (da * uv * (sg * (1.0 + gv * (1.0 - sg)))).astype(BF16)
        du = (da * sl).astype(BF16)
        dg_ref[...] = dg
        du_ref[...] = du
        dh = dot_nn(dg, wg_ref[...]) + dot_nn(du, wu_ref[...])
        h_ref[...] = _rms_fwd(xv, nw_ref[...]).astype(BF16)
        dx, dnw = _rms_bwd(xv, nw_ref[...], dh)
        dx_ref[...] = dxo_v + dx
        _acc(dnw_ref, pl.program_id(0) == 0, dnw)

    return hosted_call(
        body, sides, name="ffn_bwd", grid=(s // tm,),
        in_specs=[_row_spec(tm, d), _row_spec(tm, d), _row_spec(tm, f), _row_spec(tm, f), _full_spec((1, d)),
                  _full_spec(wg.shape), _full_spec(wu.shape), _full_spec(wd.shape)],
        out_specs=[_row_spec(tm, d), _full_spec((1, d)), _row_spec(tm, f), _row_spec(tm, f), _row_spec(tm, f),
                   _row_spec(tm, d)],
        out_shape=[jax.ShapeDtypeStruct((s, d), F32), jax.ShapeDtypeStruct((1, d), F32),
                   jax.ShapeDtypeStruct((s, f), BF16), jax.ShapeDtypeStruct((s, f), BF16),
                   jax.ShapeDtypeStruct((s, f), BF16), jax.ShapeDtypeStruct((s, d), BF16)],
        scratch_shapes=[], compiler_params=_params(1, VMEM_LIMIT), operands=(dxo, x, g, u, nw, wg, wu, wd))


def _tile(n, cap=256):
    best = LANE
    for t in range(LANE, cap + 1, LANE):
        if n % t == 0:
            best = t
    return best


def mm_tn(a, b):
    many = isinstance(a, (list, tuple))
    a_list = list(a) if many else [a]
    n_a = len(a_list)
    s, k = a_list[0].shape
    n = b.shape[1]
    tk = _tile(k)

    def body(*refs):
        b_val = refs[n_a][...]
        for a_ref, o_ref in zip(refs[:n_a], refs[n_a + 1:]):
            o_ref[...] = dot_nn(a_ref[...].astype(BF16).T, b_val).astype(BF16)

    outs = pl.pallas_call(
        body, name="mm_tn", grid=(k // tk,),
        in_specs=[pl.BlockSpec((s, tk), lambda i: (0, i))] * n_a + [_full_spec((s, n))],
        out_specs=[pl.BlockSpec((tk, n), lambda i: (i, 0))] * n_a,
        out_shape=[jax.ShapeDtypeStruct((k, n), BF16)] * n_a, compiler_params=_params(1, VMEM_LIMIT),
    )(*a_list, b)
    return outs if many else outs[0]


def head_loss(x, fw, tgt):
    s, d = x.shape
    tm = min(ROW_TILE, s)

    def body(x_ref, fw_ref, t_ref, loss_ref, dx_ref, dfw_ref):
        xv = x_ref[...]
        err = _rms_fwd(xv, fw_ref[...]) - t_ref[...]
        part = jnp.zeros((1, LANE), F32) + 0.5 * jnp.sum(err * err) / d
        dx, dfw = _rms_bwd(xv, fw_ref[...], err / d)
        dx_ref[...] = dx
        first = pl.program_id(0) == 0
        _acc(loss_ref, first, part)
        _acc(dfw_ref, first, dfw)

    return pl.pallas_call(
        body, name="head_loss", grid=(s // tm,),
        in_specs=[_row_spec(tm, d), _full_spec((1, d)), _row_spec(tm, d)],
        out_specs=[_full_spec((1, LANE)), _row_spec(tm, d), _full_spec((1, d))],
        out_shape=[jax.ShapeDtypeStruct((1, LANE), F32), jax.ShapeDtypeStruct((s, d), F32),
                   jax.ShapeDtypeStruct((1, d), F32)],
        compiler_params=_params(1),
    )(x, fw, tgt)


HALO = 8


def _conv_pre(ext, cw_ref, cb_ref):
    shifted = [pltpu.roll(ext, CONV_WIDTH - 1 - i, 0)[HALO:] if i < CONV_WIDTH - 1 else ext[HALO:]
               for i in range(CONV_WIDTH)]
    acc = cb_ref[...] + sum(cw_ref[i:i + 1, :] * shifted[i] for i in range(CONV_WIDTH))
    return acc, shifted


def _halo_spec(n, block_of_step):
    return pl.BlockSpec((HALO, n), lambda i: (jnp.maximum(block_of_step(i) * (BLK // HALO) - 1, 0), 0))


def conv_fwd(xbc, cw, cb, sides=()):
    s, n = xbc.shape

    def body(cur_ref, prev_ref, cw_ref, cb_ref, o_ref):
        prev = jnp.where(pl.program_id(0) > 0, prev_ref[...], 0.0)
        acc, _ = _conv_pre(jnp.concatenate([prev, cur_ref[...]], axis=0), cw_ref, cb_ref)
        o_ref[...] = acc * _sigmoid(acc)

    return hosted_call(
        body, sides, name="conv_fwd", grid=(s // BLK,),
        in_specs=[pl.BlockSpec((BLK, n), lambda c: (c, 0)), _halo_spec(n, lambda c: c),
                  _full_spec(cw.shape), _full_spec((1, n))],
        out_specs=[pl.BlockSpec((BLK, n), lambda c: (c, 0))], out_shape=[jax.ShapeDtypeStruct((s, n), F32)],
        scratch_shapes=[], compiler_params=_params(1), operands=(xbc, xbc, cw, cb))


def conv_bwd(xbc, cw, cb, du, sides=()):
    s, n = xbc.shape
    nb = s // BLK

    def body(cur_ref, prev_ref, cw_ref, cb_ref, du_ref, dx_ref, dcw_ref, dcb_ref, nxt_ref):
        i = pl.program_id(0)
        c = nb - 1 - i
        prev = jnp.where(c > 0, prev_ref[...], 0.0)
        acc, shifted = _conv_pre(jnp.concatenate([prev, cur_ref[...]], axis=0), cw_ref, cb_ref)
        sg = _sigmoid(acc)
        dacc = du_ref[...] * (sg * (1.0 + acc * (1.0 - sg)))

        @pl.when(i == 0)
        def _():
            nxt_ref[...] = jnp.zeros_like(nxt_ref)
            dcw_ref[...] = jnp.zeros_like(dcw_ref)
            dcb_ref[...] = jnp.zeros_like(dcb_ref)

        dcb_ref[...] += jnp.sum(dacc, axis=0, keepdims=True)
        for t in range(CONV_WIDTH):
            dcw_ref[t:t + 1, :] += jnp.sum(dacc * shifted[t], axis=0, keepdims=True)
        ext = jnp.concatenate([dacc, nxt_ref[...]], axis=0)
        dx = cw_ref[CONV_WIDTH - 1:CONV_WIDTH, :] * dacc
        for t in range(CONV_WIDTH - 1):
            dx += cw_ref[t:t + 1, :] * pltpu.roll(ext, BLK + HALO - (CONV_WIDTH - 1 - t), 0)[:BLK]
        dx_ref[...] = dx.astype(dx_ref.dtype)
        nxt_ref[...] = dacc[:HALO]

    rev = lambda i: (nb - 1 - i, 0)
    return hosted_call(
        body, sides, name="conv_bwd", grid=(nb,),
        in_specs=[pl.BlockSpec((BLK, n), rev), _halo_spec(n, lambda i: nb - 1 - i),
                  _full_spec(cw.shape), _full_spec((1, n)), pl.BlockSpec((BLK, n), rev)],
        out_specs=[pl.BlockSpec((BLK, n), rev), _full_spec((8, n)), _full_spec((1, n))],
        out_shape=[jax.ShapeDtypeStruct((s, n), BF16), jax.ShapeDtypeStruct((8, n), F32),
                   jax.ShapeDtypeStruct((1, n), F32)],
        scratch_shapes=[pltpu.VMEM((HALO, n), F32)],
        compiler_params=_params(1), operands=(xbc, xbc, cw, cb, du))


N_PAIR = SSD_HEADS // 2
B_LO = SSD_WIDTH
C_LO = SSD_WIDTH + 2 * D_STATE


def _softplus(x):
    return jnp.maximum(x, 0.0) + jnp.log(1.0 + jnp.exp(-jnp.abs(x)))


def _ssd_chunk(u_ref, dt_ref, dtb_ref, alog_ref):
    shape = (BLK, BLK)
    tri = _iota(shape, 1) <= _iota(shape, 0)
    pre = dt_ref[...] + dtb_ref[...]
    dt = _softplus(pre)
    a = -jnp.exp(alog_ref[...])
    acum = dot_nn(tri.astype(F32), dt * a, exact=True)
    acum_t = acum.T
    last = _sub_row(acum, BLK - 1)
    heads = []
    for h in range(SSD_HEADS):
        col = _lane_col(acum, h)
        seg = jnp.where(tri, col - _sub_row(acum_t, h), -1e30)
        heads.append(dict(col=col, dm=jnp.exp(seg), dt=_lane_col(dt, h), last=_lane_col(last, h)))
    return tri, pre, dt, a, heads


def _pair_mix(lo_mask, v0, v1):
    return jnp.where(lo_mask, v0, v1)


def ssd_fwd(u, z, dtr, dtb, alog, dsk, nw, sides=()):
    s = u.shape[0]
    nc = s // BLK

    def body(u_ref, z_ref, dt_ref, dtb_ref, alog_ref, dsk_ref, nw_ref, y_ref, st_ref, s_ref):
        @pl.when(pl.program_id(0) == 0)
        def _():
            s_ref[...] = jnp.zeros_like(s_ref)

        _, _, _, _, heads = _ssd_chunk(u_ref, dt_ref, dtb_ref, alog_ref)
        lo_lane = _iota((BLK, LANE), 1) < HEAD_DIM
        lo_sub = _iota((BLK, LANE), 0) < HEAD_DIM
        ys = []
        for p in range(N_PAIR):
            g = p // 2
            h0, h1 = heads[2 * p], heads[2 * p + 1]
            bg = u_ref[:, B_LO + g * D_STATE:B_LO + (g + 1) * D_STATE]
            cg = u_ref[:, C_LO + g * D_STATE:C_LO + (g + 1) * D_STATE]
            xs = u_ref[:, p * LANE:(p + 1) * LANE]
            xp = xs * _pair_mix(lo_lane, h0["dt"], h1["dt"])
            gm = dot_nt(cg, bg)
            yd = _pair_mix(lo_lane, dot_nn(gm * h0["dm"], xp), dot_nn(gm * h1["dm"], xp))
            sp = s_ref[p]
            st_ref[0, p] = sp
            yo = _pair_mix(lo_lane, jnp.exp(h0["col"]), jnp.exp(h1["col"])) * dot_nt(cg, sp)
            dskp = _pair_mix(lo_lane, _lane_col(dsk_ref[...], 2 * p), _lane_col(dsk_ref[...], 2 * p + 1))
            ys.append(yd + yo + xs * dskp)
            wp = _pair_mix(lo_lane, jnp.exp(h0["last"] - h0["col"]), jnp.exp(h1["last"] - h1["col"]))
            el = _pair_mix(lo_sub, jnp.exp(h0["last"]), jnp.exp(h1["last"]))
            s_ref[p] = el * sp + dot_tn(wp * xp, bg)
        y = jnp.concatenate(ys, axis=1)
        zv = z_ref[...]
        y_ref[...] = _rms_fwd(y * zv * _sigmoid(zv), nw_ref[...])

    vec = _full_spec((1, LANE))
    return hosted_call(
        body, sides, name="ssd_fwd", grid=(nc,),
        in_specs=[_row_spec(BLK, CONV_DIM), _row_spec(BLK, SSD_WIDTH), _row_spec(BLK, LANE), vec, vec, vec,
                  _full_spec((1, SSD_WIDTH))],
        out_specs=[_row_spec(BLK, SSD_WIDTH), pl.BlockSpec((1, N_PAIR, LANE, D_STATE), lambda c: (c, 0, 0, 0))],
        out_shape=[jax.ShapeDtypeStruct((s, SSD_WIDTH), F32), jax.ShapeDtypeStruct((nc, N_PAIR, LANE, D_STATE), F32)],
        scratch_shapes=[pltpu.VMEM((N_PAIR, LANE, D_STATE), F32)],
        compiler_params=_params(1), operands=(u, z, dtr, dtb, alog, dsk, nw))


def ssd_bwd(u, z, dtr, st, dyo, dtb, alog, dsk, nw, sides=()):
    s = u.shape[0]
    nc = s // BLK

    def body(u_ref, z_ref, dt_ref, st_ref, dyo_ref, dtb_ref, alog_ref, dsk_ref, nw_ref,
             du_ref, dz_ref, ddt_ref, ddtb_ref, dalog_ref, ddsk_ref, dnw_ref, ds_ref):
        first = pl.program_id(0) == 0

        @pl.when(first)
        def _():
            ds_ref[...] = jnp.zeros_like(ds_ref)

        tri, pre, dt, a, heads = _ssd_chunk(u_ref, dt_ref, dtb_ref, alog_ref)
        shape = (BLK, LANE)
        lane = _iota(shape, 1)
        lo_lane = lane < HEAD_DIM
        lo_sub = _iota(shape, 0) < HEAD_DIM
        pairs = []
        ys = []
        for p in range(N_PAIR):
            g = p // 2
            h0, h1 = heads[2 * p], heads[2 * p + 1]
            bg = u_ref[:, B_LO + g * D_STATE:B_LO + (g + 1) * D_STATE]
            cg = u_ref[:, C_LO + g * D_STATE:C_LO + (g + 1) * D_STATE]
            xs = u_ref[:, p * LANE:(p + 1) * LANE]
            dtp = _pair_mix(lo_lane, h0["dt"], h1["dt"])
            xp = xs * dtp
            gm = dot_nt(cg, bg)
            m0, m1 = gm * h0["dm"], gm * h1["dm"]
            sp = st_ref[0, p]
            eap = _pair_mix(lo_lane, jnp.exp(h0["col"]), jnp.exp(h1["col"]))
            yo = eap * dot_nt(cg, sp)
            dskp = _pair_mix(lo_lane, _lane_col(dsk_ref[...], 2 * p), _lane_col(dsk_ref[...], 2 * p + 1))
            ys.append(_pair_mix(lo_lane, dot_nn(m0, xp), dot_nn(m1, xp)) + yo + xs * dskp)
            pairs.append(dict(bg=bg, cg=cg, xs=xs, dtp=dtp, xp=xp, gm=gm, m=(m0, m1), sp=sp, eap=eap, yo=yo, dskp=dskp))
        y = jnp.concatenate(ys, axis=1)
        zv = z_ref[...]
        sz = _sigmoid(zv)
        gate = zv * sz
        dyg, dnw = _rms_bwd(y * gate, nw_ref[...], dyo_ref[...])
        _acc(dnw_ref, first, dnw)
        dy = dyg * gate
        dz_ref[...] = (dyg * y * (sz * (1.0 + zv * (1.0 - sz)))).astype(dz_ref.dtype)

        zeros = jnp.zeros(shape, F32)
        dacum_col = zeros
        dacum_row = zeros
        ddt = zeros
        ddsk = jnp.zeros((1, LANE), F32)
        dlast = jnp.zeros((1, LANE), F32)
        head_row = _iota((1, LANE), 1)
        sub = _iota(shape, 0)
        db = [zeros, zeros]
        dc = [zeros, zeros]
        for p in range(N_PAIR):
            g = p // 2
            q = pairs[p]
            dyp = dy[:, p * LANE:(p + 1) * LANE]
            dsn = ds_ref[p]
            t = dyp * q["xs"]
            dxs = dyp * q["dskp"]
            dcs = dyp * q["eap"]
            dc[g] = dc[g] + dot_nn(dcs, q["sp"])
            dsp = dot_tn(dcs, q["cg"])
            dea = dyp * q["yo"]
            elp = _pair_mix(lo_sub, jnp.exp(heads[2 * p]["last"]), jnp.exp(heads[2 * p + 1]["last"]))
            dsp = dsp + elp * dsn
            dels = dsn * q["sp"] * elp
            wp = _pair_mix(lo_lane, jnp.exp(heads[2 * p]["last"] - heads[2 * p]["col"]),
                           jnp.exp(heads[2 * p + 1]["last"] - heads[2 * p + 1]["col"]))
            dv = dot_nt(q["bg"], dsn)
            db[g] = db[g] + dot_nn(wp * q["xp"], dsn)
            dxp = dv * wp
            dwv = dv * q["xp"] * wp
            dgm = zeros
            for k in range(2):
                h = 2 * p + k
                mine = lo_lane if k == 0 else jnp.logical_not(lo_lane)
                mine_sub = lo_sub if k == 0 else jnp.logical_not(lo_sub)
                dyh = jnp.where(mine, dyp, 0.0)
                dm = dot_nt(dyh, q["xp"])
                dxp = dxp + dot_tn(q["m"][k], dyh)
                dgm = dgm + dm * heads[h]["dm"]
                e = dm * q["m"][k]
                onehot = lane == h
                dw_col = jnp.sum(jnp.where(mine, dwv, 0.0), axis=1, keepdims=True)
                col = (jnp.sum(e, axis=1, keepdims=True) + jnp.sum(jnp.where(mine, dea, 0.0), axis=1, keepdims=True)
                       - dw_col)
                dacum_col = dacum_col + jnp.where(onehot, col, 0.0)
                dacum_row = dacum_row - jnp.where(sub == h, jnp.sum(e, axis=0, keepdims=True), 0.0)
                dl = jnp.sum(dw_col) + jnp.sum(jnp.where(mine_sub, dels, 0.0))
                dlast = dlast + jnp.where(head_row == h, dl, 0.0)
                ddsk = ddsk + jnp.where(head_row == h, jnp.sum(jnp.where(mine, t, 0.0)), 0.0)
            dc[g] = dc[g] + dot_nn(dgm, q["bg"])
            db[g] = db[g] + dot_tn(dgm, q["cg"])
            dxs = dxs + dxp * q["dtp"]
            tt = dxp * q["xs"]
            for k in range(2):
                mine = lo_lane if k == 0 else jnp.logical_not(lo_lane)
                ddt = ddt + jnp.where(lane == 2 * p + k, jnp.sum(jnp.where(mine, tt, 0.0), axis=1, keepdims=True), 0.0)
            du_ref[:, p * LANE:(p + 1) * LANE] = dxs
            ds_ref[p] = dsp
        for g in range(2):
            du_ref[:, B_LO + g * D_STATE:B_LO + (g + 1) * D_STATE] = db[g]
            du_ref[:, C_LO + g * D_STATE:C_LO + (g + 1) * D_STATE] = dc[g]
        dacum = dacum_col + dacum_row.T + jnp.where(sub == BLK - 1, dlast, 0.0)
        dda = dot_tn(tri.astype(F32), dacum, exact=True)
        ddt = ddt + dda * a
        _acc(dalog_ref, first, jnp.sum(dda * dt, axis=0, keepdims=True) * a)
        dpre = ddt * _sigmoid(pre)
        ddt_ref[...] = dpre.astype(ddt_ref.dtype)
        _acc(ddtb_ref, first, jnp.sum(dpre, axis=0, keepdims=True))
        _acc(ddsk_ref, first, ddsk)

    rev = lambda i: (nc - 1 - i, 0)
    vec = _full_spec((1, LANE))
    rows = lambda n: pl.BlockSpec((BLK, n), rev)
    return hosted_call(
        body, sides, name="ssd_bwd", grid=(nc,),
        in_specs=[rows(CONV_DIM), rows(SSD_WIDTH), rows(LANE),
                  pl.BlockSpec((1, N_PAIR, LANE, D_STATE), lambda i: (nc - 1 - i, 0, 0, 0)), rows(SSD_WIDTH),
                  vec, vec, vec, _full_spec((1, SSD_WIDTH))],
        out_specs=[rows(CONV_DIM), rows(SSD_WIDTH), rows(LANE), vec, vec, vec, _full_spec((1, SSD_WIDTH))],
        out_shape=[jax.ShapeDtypeStruct((s, CONV_DIM), F32), jax.ShapeDtypeStruct((s, SSD_WIDTH), BF16),
                   jax.ShapeDtypeStruct((s, LANE), BF16)] + [jax.ShapeDtypeStruct((1, LANE), F32)] * 3
        + [jax.ShapeDtypeStruct((1, SSD_WIDTH), F32)],
        scratch_shapes=[pltpu.VMEM((N_PAIR, LANE, D_STATE), F32)],
        compiler_params=_params(1), operands=(u, z, dtr, st, dyo, dtb, alog, dsk, nw))


SB_PAIRS = SB_WIDTH // LANE
SB_SCALE = HEAD_DIM ** -0.5


SB_TQ = 256


def _sb_tq(s):
    return min(SB_TQ, s)


def _sb_stack(x):
    lo_lane = _iota(x.shape, 1) < HEAD_DIM
    return jnp.concatenate([jnp.where(lo_lane, x, 0.0), jnp.where(lo_lane, 0.0, x)], axis=0)


def _sb_unstack(x2):
    tq = x2.shape[0] // 2
    lo_lane = _iota((tq, LANE), 1) < HEAD_DIM
    return jnp.where(lo_lane, x2[:tq], x2[tq:])


def _sb_logits(q2, kj, row0, col0, masked):
    shape = (q2.shape[0], BLK)
    tq = shape[0] // 2
    z = dot_nt(q2, kj)
    t = jnp.log(1.0 + jnp.exp(-jnp.abs(z)))
    ls = jnp.minimum(z, 0.0) - t
    lk = jnp.minimum(-z, 0.0) - t
    if not masked:
        return None, ls, lk
    row = _iota(shape, 0)
    valid = (col0 + _iota(shape, 1)) < (row0 + jnp.where(row < tq, row, row - tq))
    return valid, ls, jnp.where(valid, lk, 0.0)


def _sb_where(valid, x):
    return x if valid is None else jnp.where(valid, x, 0.0)


def _sb_win(x2, lo):
    if lo == 0:
        return x2
    tq = x2.shape[0] // 2
    return jnp.concatenate([x2[lo:tq], x2[tq + lo:]], axis=0)


def _sb_unwin(x2, xw, lo):
    if lo == 0:
        return xw
    tq = x2.shape[0] // 2
    return jnp.concatenate([x2[:lo], xw[:tq - lo], x2[tq:tq + lo], xw[tq - lo:]], axis=0)


def _sb_add(ref, val, lo):
    if lo == 0:
        ref[...] += val
    else:
        tq = ref.shape[0] // 2
        ref[lo:tq, :] += val[:tq - lo]
        ref[tq + lo:, :] += val[tq - lo:]


def _sums(x, mask2, parts):
    acc = None
    rest = x
    for _ in range(parts):
        term = rest.astype(BF16)
        rest = rest - term.astype(F32)
        d = lax.dot_general(term, mask2, (((1,), (0,)), ((), ())), preferred_element_type=F32)
        acc = d if acc is None else acc + d
    return acc[:, :BLK], acc[:, BLK:]


def _mask2(cond):
    return jnp.concatenate([cond.astype(BF16), jnp.ones(cond.shape, BF16)], axis=1)


def _sb_specs(s):
    tq = _sb_tq(s)
    qspec = pl.BlockSpec((tq, LANE), lambda p, i: (i, p))
    kspec = pl.BlockSpec((s, LANE), lambda p, i: (0, SB_PAIRS + p))
    vspec = pl.BlockSpec((s, LANE), lambda p, i: (0, 2 * SB_PAIRS + p))
    return qspec, kspec, vspec


SB_FLOOR = -104.0


def sb_fwd(qkv, sides=()):
    s = qkv.shape[0]
    tq = _sb_tq(s)
    kpq = tq // BLK

    def body(q_ref, k_ref, v_ref, o_ref, t_ref, n_ref, acc_ref):
        qi = pl.program_id(1)
        q2 = _sb_stack(q_ref[...] * SB_SCALE).astype(BF16)
        later = _mask2(_iota((BLK, BLK), 0) > _iota((BLK, BLK), 1))
        acc_ref[...] = jnp.zeros_like(acc_ref)

        def step(j, r, masked, lo=0):
            rows = pl.ds(pl.multiple_of(j * BLK, BLK), BLK)
            rw = _sb_win(r, lo)
            valid, ls, lk = _sb_logits(_sb_win(q2, lo), k_ref[rows, :], qi * tq + lo, j * BLK, masked)
            after, total = _sums(lk, later, 2)
            w = _sb_where(valid, jnp.exp(ls + rw + after))
            _sb_add(acc_ref, dot_nn(w, v_ref[rows, :]), lo)
            return _sb_unwin(r, rw + total, lo)

        r = jnp.zeros((2 * tq, LANE), F32)
        for d in reversed(range(kpq)):
            r = step(kpq * qi + d, r, True, d * BLK)

        def tile(g, r):
            for d in reversed(range(kpq)):
                r = step(kpq * (qi - 1 - g) + d, r, False)
            return r

        n, r = lax.while_loop(lambda c: jnp.logical_and(c[0] < qi, jnp.max(c[1]) > SB_FLOOR),
                              lambda c: (c[0] + 1, tile(c[0], c[1])), (jnp.int32(0), r))
        o_ref[...] = _sb_unstack(acc_ref[...])
        t_ref[...] = jnp.concatenate([r[:tq], r[tq:]], axis=1)
        n_ref[...] = jnp.zeros(n_ref.shape, F32) + n.astype(F32)

    return hosted_call(
        body, sides, name="sb_fwd", grid=(SB_PAIRS, s // tq),
        in_specs=list(_sb_specs(s)),
        out_specs=[pl.BlockSpec((tq, LANE), lambda p, i: (i, p)), pl.BlockSpec((tq, 2 * LANE), lambda p, i: (i, p)),
                   pl.BlockSpec((None, None, 8, LANE), lambda p, i: (p, i, 0, 0))],
        out_shape=[jax.ShapeDtypeStruct((s, SB_WIDTH), F32), jax.ShapeDtypeStruct((s, 2 * SB_WIDTH), F32),
                   jax.ShapeDtypeStruct((SB_PAIRS, s // tq, 8, LANE), F32)],
        scratch_shapes=[pltpu.VMEM((2 * tq, LANE), F32)],
        compiler_params=_params(2), operands=(qkv, qkv, qkv))


def sb_bwd(qkv, tot, swept, do, do_col=0, sides=()):
    s = qkv.shape[0]
    tq = _sb_tq(s)
    kpq = tq // BLK

    def body(q_ref, k_ref, v_ref, t_ref, n_ref, do_ref, dq_ref, dk_ref, dv_ref, acc_ref):
        qi = pl.program_id(1)
        n = jnp.clip(jnp.max(n_ref[...]).astype(jnp.int32), 0, qi)
        q2 = _sb_stack(q_ref[...] * SB_SCALE).astype(BF16)
        do2 = _sb_stack(do_ref[...]).astype(BF16)
        tot2 = jnp.concatenate([t_ref[:, :LANE], t_ref[:, LANE:]], axis=0)
        sq = (BLK, BLK)
        later = _mask2(_iota(sq, 0) > _iota(sq, 1))
        before = _mask2(_iota(sq, 0) < _iota(sq, 1))
        acc_ref[...] = jnp.zeros_like(acc_ref)

        @pl.when(qi == 0)
        def _():
            dk_ref[...] = jnp.zeros_like(dk_ref)
            dv_ref[...] = jnp.zeros_like(dv_ref)

        def step(j, carry, masked, lo=0):
            rc, fc = carry
            rows = pl.ds(pl.multiple_of(j * BLK, BLK), BLK)
            kj = k_ref[rows, :]
            vj = v_ref[rows, :]
            qw, dow, fw = _sb_win(q2, lo), _sb_win(do2, lo), _sb_win(fc, lo)
            valid, ls, lk = _sb_logits(qw, kj, qi * tq + lo, j * BLK, masked)
            after, total = _sums(lk, later, 2)
            rw = _sb_win(rc, lo) - total
            w = _sb_where(valid, jnp.exp(ls + rw + after))
            e = w * dot_nt(dow, vj)
            f_in, f_tot = _sums(e, before, 2)
            sg = jnp.exp(ls)
            dz = _sb_where(valid, e * (1.0 - sg) - (fw + f_in) * sg)
            _sb_add(acc_ref, dot_nn(dz, kj), lo)
            dk_ref[rows, :] += dot_tn(dz, qw)
            dv_ref[rows, :] += dot_tn(w, dow)
            return _sb_unwin(rc, rw, lo), _sb_unwin(fc, fw + f_tot, lo)

        def tile(g, carry):
            for d in range(kpq):
                carry = step(kpq * g + d, carry, False)
            return carry

        carry = lax.fori_loop(qi - n, qi, tile, (tot2, jnp.zeros((2 * tq, LANE), F32)))
        for d in range(kpq):
            carry = step(kpq * qi + d, carry, True, d * BLK)
        dq_ref[...] = (SB_SCALE * _sb_unstack(acc_ref[...])).astype(dq_ref.dtype)

    qspec, kspec, vspec = _sb_specs(s)
    blk = pl.BlockSpec((tq, LANE), lambda p, i: (i, p))
    acc = pl.BlockSpec((s, LANE), lambda p, i: (0, p))
    return hosted_call(
        body, sides, name="sb_bwd", grid=(SB_PAIRS, s // tq),
        in_specs=[qspec, kspec, vspec, pl.BlockSpec((tq, 2 * LANE), lambda p, i: (i, p)),
                  pl.BlockSpec((None, None, 8, LANE), lambda p, i: (p, i, 0, 0)),
                  pl.BlockSpec((tq, LANE), lambda p, i: (i, do_col + p))],
        out_specs=[blk, acc, acc],
        out_shape=[jax.ShapeDtypeStruct((s, SB_WIDTH), BF16)] + [jax.ShapeDtypeStruct((s, SB_WIDTH), F32)] * 2,
        scratch_shapes=[pltpu.VMEM((2 * tq, LANE), F32)],
        compiler_params=_params(2), operands=(qkv, qkv, qkv, tot, swept, do))


POOL_GROUP_DIM = POOL_WIDTH // len(POOL_WINDOWS)


assert all(w == 2 ** (i + 1) for i, w in enumerate(POOL_WINDOWS))


def _pool_inv(c):
    group = _iota((BLK, POOL_WIDTH), 1) // POOL_GROUP_DIM
    pos = c * BLK + _iota((BLK, POOL_WIDTH), 0)
    win = jnp.zeros((BLK, POOL_WIDTH), jnp.int32)
    for gi, wn in enumerate(POOL_WINDOWS):
        win = jnp.where(group == gi, wn, win)
    return 1.0 / jnp.minimum(pos + 1, win).astype(F32)


def _window_sums(ext, trailing):
    group = _iota(ext.shape, 1) // POOL_GROUP_DIM
    acc = ext
    out = None
    for gi in range(len(POOL_WINDOWS)):
        shift = 2 ** gi
        acc = acc + pltpu.roll(acc, shift if trailing else ext.shape[0] - shift, 0)
        out = acc if out is None else jnp.where(group == gi, acc, out)
    return out


def _pool_pooled(ext, cur, inv):
    return _window_sums(ext, True)[BLK:] * inv - cur


def pool_fwd(p, wblk, pb, ps):
    s, n = p.shape

    def body(cur_ref, prev_ref, w_ref, pb_ref, ps_ref, o_ref):
        c = pl.program_id(0)
        cur = cur_ref[...]
        prev = jnp.where(c > 0, prev_ref[...], 0.0)
        pooled = _pool_pooled(jnp.concatenate([prev, cur], axis=0), cur, _pool_inv(c))
        o_ref[...] = (dot_nn(pooled, w_ref[...]) + pb_ref[...]) * ps_ref[...]

    return pl.pallas_call(
        body, name="pool_fwd", grid=(s // BLK,),
        in_specs=[pl.BlockSpec((BLK, n), lambda c: (c, 0)), pl.BlockSpec((BLK, n), lambda c: (jnp.maximum(c - 1, 0), 0)),
                  _full_spec((n, n)), _full_spec((1, n)), _full_spec((1, n))],
        out_specs=pl.BlockSpec((BLK, n), lambda c: (c, 0)), out_shape=jax.ShapeDtypeStruct((s, n), F32),
        compiler_params=_params(1),
    )(p, p, wblk, pb, ps)


def pool_bwd(p, wblk, pb, ps, dout, do_col=0, sides=()):
    s, n = p.shape
    nb = s // BLK

    def body(cur_ref, prev_ref, w_ref, pb_ref, ps_ref, do_ref, dp_ref, dw_ref, dpb_ref, dps_ref, carry_ref):
        i = pl.program_id(0)
        c = nb - 1 - i
        first = i == 0
        cur = cur_ref[...]
        prev = jnp.where(c > 0, prev_ref[...], 0.0)
        inv = _pool_inv(c)
        pooled = _pool_pooled(jnp.concatenate([prev, cur], axis=0), cur, inv)
        mixed = dot_nn(pooled, w_ref[...]) + pb_ref[...]
        dov = do_ref[...]
        dmixed = dov * ps_ref[...]
        _acc(dps_ref, first, jnp.sum(dov * mixed, axis=0, keepdims=True))
        _acc(dpb_ref, first, jnp.sum(dmixed, axis=0, keepdims=True))
        _acc(dw_ref, first, dot_tn(pooled, dmixed))
        dpooled = dot_nt(dmixed, w_ref[...])
        dext = _window_sums(jnp.concatenate([jnp.zeros((BLK, n), F32), dpooled * inv], axis=0), False)

        @pl.when(first)
        def _():
            carry_ref[...] = jnp.zeros_like(carry_ref)

        dp_ref[...] = (dext[BLK:] - dpooled + carry_ref[...]).astype(dp_ref.dtype)
        carry_ref[...] = dext[:BLK]

    rev = lambda i: (nb - 1 - i, 0)
    return hosted_call(
        body, sides, name="pool_bwd", grid=(nb,),
        in_specs=[pl.BlockSpec((BLK, n), rev), pl.BlockSpec((BLK, n), lambda i: (jnp.maximum(nb - 2 - i, 0), 0)),
                  _full_spec((n, n)), _full_spec((1, n)), _full_spec((1, n)),
                  pl.BlockSpec((BLK, n), lambda i: (nb - 1 - i, do_col))],
        out_specs=[pl.BlockSpec((BLK, n), rev), _full_spec((n, n)), _full_spec((1, n)), _full_spec((1, n))],
        out_shape=[jax.ShapeDtypeStruct((s, n), BF16), jax.ShapeDtypeStruct((n, n), F32),
                   jax.ShapeDtypeStruct((1, n), F32), jax.ShapeDtypeStruct((1, n), F32)],
        scratch_shapes=[pltpu.VMEM((BLK, n), F32)],
        compiler_params=_params(1), operands=(p, p, wblk, pb, ps, dout))


def _row_tile(rows):
    if rows <= 512:
        return rows
    for t in (512, 256, 128, 64, 32, 16, 8):
        if rows % t == 0:
            return t
    return rows


def adamw(w, g, m, v):
    (d, nm, nv), _ = adamw_many([w], [g], [m], [v])
    return d[0], nm[0], nv[0]


def adamw_many(ws, gs, ms, vs, sides=()):
    k = len(ws)
    n, rows, cols = ws[0].shape
    tr = _row_tile(rows)
    while 7 * k * 2 * tr * cols * 4 > VMEM_LIMIT // 2 and tr % 16 == 0:
        tr //= 2

    def body(*refs):
        for i in range(k):
            outs = _adamw_math(*[refs[j * k + i][...] for j in range(4)])
            for j in range(3):
                refs[(4 + j) * k + i][...] = outs[j]

    spec = pl.BlockSpec((1, tr, cols), lambda i, j: (i, j, 0))
    outs, side_outs = hosted_call(
        body, sides, name="adamw", grid=(n, rows // tr), in_specs=[spec] * (4 * k), out_specs=[spec] * (3 * k),
        out_shape=[jax.ShapeDtypeStruct(ws[0].shape, F32)] * (3 * k), scratch_shapes=[],
        compiler_params=_params(2), operands=(*ws, *gs, *ms, *vs))
    return (outs[:k], outs[k:2 * k], outs[2 * k:]), side_outs


def _adamw_math(w, g, m, v):
    nm = ADAM_B1 * m + (1.0 - ADAM_B1) * g
    nv = ADAM_B2 * v + (1.0 - ADAM_B2) * (g * g)
    m_hat = nm / (1.0 - ADAM_B1 ** ADAM_STEP)
    v_hat = nv / (1.0 - ADAM_B2 ** ADAM_STEP)
    return -ADAM_LR * (m_hat / (jnp.sqrt(v_hat) + ADAM_EPS) + ADAM_WD * w), nm, nv


def adamw_small(ws, gs, ms, vs):
    n = len(ws)

    def body(*refs):
        for i in range(n):
            outs = _adamw_math(*[refs[k * n + i][...] for k in range(4)])
            for k in range(3):
                refs[(4 + k) * n + i][...] = outs[k]

    vm = pl.BlockSpec(memory_space=pltpu.VMEM)
    outs = pl.pallas_call(
        body, name="adamw_small", in_specs=[vm] * (4 * n), out_specs=[vm] * (3 * n),
        out_shape=[jax.ShapeDtypeStruct(w.shape, F32) for w in ws] * 3,
    )(*ws, *gs, *ms, *vs)
    return outs[:n], outs[n:2 * n], outs[2 * n:]


def slab_sum(srcs, n_out, out_dtype, into=None, slot=0):
    _, rows, cols = srcs[0][0].shape
    tr = _row_tile(rows)
    n_src = len(srcs)
    sel = jnp.stack([jnp.asarray(base, jnp.int32) for _, base, _ in srcs])

    def body(sel_ref, *refs):
        acc = refs[0][...].astype(F32)
        for r in refs[1:n_src]:
            acc = acc + r[...].astype(F32)
        refs[-1][...] = acc.astype(out_dtype)

    def in_spec(k, step):
        return pl.BlockSpec((None, tr, cols), lambda o, i, sel_ref: (sel_ref[k] + step * o, i, 0))

    shape = (n_out, rows, cols) if into is None else into.shape
    return pl.pallas_call(
        body, name="slab_sum",
        grid_spec=pltpu.PrefetchScalarGridSpec(
            num_scalar_prefetch=1, grid=(n_out, rows // tr),
            in_specs=[in_spec(k, step) for k, (_, _, step) in enumerate(srcs)] + ([] if into is None else [ANY]),
            out_specs=pl.BlockSpec((None, tr, cols), lambda o, i, sel_ref: (slot + o, i, 0))),
        out_shape=jax.ShapeDtypeStruct(shape, out_dtype), compiler_params=_params(2),
        input_output_aliases={} if into is None else {1 + n_src: 0},
    )(sel, *[a for a, _, _ in srcs], *([] if into is None else [into]))


ICI_FLIPS = ((1, 0, 0), (0, 1, 0), (1, 1, 0))
D2D_FLIPS = ((0, 0, 1),)
ANY = pl.BlockSpec(memory_space=pl.ANY)


def _me():
    return lax.axis_index("x"), lax.axis_index("y"), lax.axis_index("c")


def _flipped(me, flip):
    return tuple(1 - m if f else m for m, f in zip(me, flip))


def _chip(dev):
    return 2 * dev[0] + dev[1]


def _dev(dev):
    return 4 * dev[0] + 2 * dev[1] + dev[2]


N_CHIP = 4
D2D = (0, 0, 1)


class Exchange:
    def __init__(self, xs, n_out, copies, own=None, in_place=False):
        self.xs, self.copies, self.own, self.in_place = list(xs), copies, own, in_place
        self.n_arr, self.n_cp = len(self.xs), len(copies)
        self.out_shape = [jax.ShapeDtypeStruct((n_out,) + x.shape[1:], x.dtype) for x in self.xs]
        self.scratch = [pltpu.SemaphoreType.DMA((self.n_arr * self.n_cp,)),
                        pltpu.SemaphoreType.DMA((self.n_arr * self.n_cp,)), pltpu.SemaphoreType.DMA((self.n_arr,))]

    def _own(self, x_refs, o_refs, sems, me):
        if self.own is None:
            return []
        return [pltpu.make_async_copy(x_refs[a].at[self.own[0](me)], o_refs[a].at[self.own[1](me)], sems[2].at[a])
                for a in range(self.n_arr)]

    def _copy(self, x_refs, o_refs, sems, me, a, j, sender):
        flip, src_slot, dst_slot = self.copies[j]
        k = a * self.n_cp + j
        return pltpu.make_async_remote_copy(
            src_ref=x_refs[a].at[src_slot(me)], dst_ref=o_refs[a].at[dst_slot(sender)],
            send_sem=sems[0].at[k], recv_sem=sems[1].at[k], device_id=_flipped(me, flip), device_id_type=MESH)

    def start(self, x_refs, o_refs, sems):
        me = _me()
        for cp in self._own(x_refs, o_refs, sems, me):
            cp.start()
        for j in range(self.n_cp):
            for a in range(self.n_arr):
                self._copy(x_refs, o_refs, sems, me, a, j, me).start()

    def wait(self, x_refs, o_refs, sems):
        me = _me()
        for j in range(self.n_cp):
            for a in range(self.n_arr):
                self._copy(x_refs, o_refs, sems, me, a, j, _flipped(me, self.copies[j][0])).wait_recv()
        for j in range(self.n_cp):
            for a in range(self.n_arr):
                self._copy(x_refs, o_refs, sems, me, a, j, me).wait_send()
        for cp in self._own(x_refs, o_refs, sems, me):
            cp.wait()

    def run(self, name):
        n = self.n_arr

        def body(*refs):
            self.start(refs[:n], refs[n:2 * n], refs[2 * n:])
            self.wait(refs[:n], refs[n:2 * n], refs[2 * n:])

        return pl.pallas_call(
            body, name=name, in_specs=[ANY] * n, out_specs=[ANY] * n, out_shape=self.out_shape,
            input_output_aliases={a: a for a in range(n)} if self.in_place else {}, scratch_shapes=self.scratch,
        )(*self.xs)


def hosted_call(body, sides, *, name, grid, in_specs, out_specs, out_shape, scratch_shapes, compiler_params, operands):
    n_in, n_out, n_scr = len(in_specs), len(out_specs), len(scratch_shapes)
    live = [s for s in sides if s is not None]
    if not live:
        outs = pl.pallas_call(body, name=name, grid=grid, in_specs=in_specs, out_specs=out_specs, out_shape=out_shape,
                              scratch_shapes=scratch_shapes, compiler_params=compiler_params)(*operands)
        return outs, [None] * len(sides)
    n = sum(s.n_arr for s in live)
    lo = [sum(s.n_arr for s in live[:i]) for i in range(len(live))]

    def full_body(*refs):
        ins, sx = refs[:n_in], refs[n_in:n_in + n]
        outs, so = refs[n_in + n:n_in + n + n_out], refs[n_in + n + n_out:n_in + 2 * n + n_out]
        scr, sems = refs[n_in + 2 * n + n_out:n_in + 2 * n + n_out + n_scr], refs[n_in + 2 * n + n_out + n_scr:]
        first = functools.reduce(jnp.logical_and, [pl.program_id(a) == 0 for a in range(len(grid))])
        last = functools.reduce(jnp.logical_and, [pl.program_id(a) == g - 1 for a, g in enumerate(grid)])
        parts = [(s, sx[l:l + s.n_arr], so[l:l + s.n_arr], sems[3 * i:3 * i + 3]) for i, (s, l) in enumerate(zip(live, lo))]

        @pl.when(first)
        def _():
            for s, x, o, m in parts:
                s.start(x, o, m)

        body(*ins, *outs, *scr)

        @pl.when(last)
        def _():
            for s, x, o, m in parts:
                s.wait(x, o, m)

    aliases = {n_in + l + a: n_out + l + a for s, l in zip(live, lo) if s.in_place for a in range(s.n_arr)}
    outs = pl.pallas_call(
        full_body, name=name + "_x", grid=grid, in_specs=list(in_specs) + [ANY] * n,
        out_specs=list(out_specs) + [ANY] * n, out_shape=list(out_shape) + [o for s in live for o in s.out_shape],
        input_output_aliases=aliases,
        scratch_shapes=list(scratch_shapes) + [m for s in live for m in s.scratch], compiler_params=compiler_params,
    )(*operands, *[x for s in live for x in s.xs])
    side_outs = iter([outs[n_out + l:n_out + l + s.n_arr] for s, l in zip(live, lo)])
    return outs[:n_out], [next(side_outs) if s is not None else None for s in sides]


def gather_ici(shards):
    ici = [(f, lambda me: 0, _dev) for f in ICI_FLIPS]
    return Exchange([s[None] for s in shards], N_DEV, ici, (lambda me: 0, _dev))


def gather_d2d(blocks):
    d2d = [(D2D, (lambda me, k=k: 2 * k + me[2]), (lambda sender, k=k: 2 * k + sender[2])) for k in range(N_CHIP)]
    return Exchange(blocks, N_DEV, d2d, None, in_place=True)


def gathered(blocks):
    return [b.reshape(-1, b.shape[2]) for b in blocks]


def scatter_d2d(parts):
    d2d = [(D2D, (lambda me, k=k: 2 * k + 1 - me[2]), (lambda sender, k=k: k)) for k in range(N_CHIP)]
    return Exchange(parts, N_CHIP, d2d)


def chip_sums(parts, sib):
    c = _me()[2]
    return [slab_sum([(p, c, 2), (s, 0, 1)], N_CHIP, BF16) for p, s in zip(parts, sib)]


def scatter_ici(sums):
    ici = [(f, (lambda me, f=f: _chip(_flipped(me, f))), (lambda sender, i=i: i)) for i, f in enumerate(ICI_FLIPS)]
    return Exchange(sums, len(ICI_FLIPS), ici)


def device_sums(sums, got, into=None, slot=0):
    x, y, _ = _me()
    outs = [slab_sum([(cs, 2 * x + y, 0)] + [(g, i, 0) for i in range(len(ICI_FLIPS))], 1, F32,
                     None if into is None else into[a], slot) for a, (cs, g) in enumerate(zip(sums, got))]
    return outs if into is not None else [o[0] for o in outs]


def all_gather(shards):
    blocks = gather_ici(shards).run("gather_ici")
    return gathered(gather_d2d(blocks).run("gather_d2d"))


def all_reduce_small(v):
    flips = D2D_FLIPS + ICI_FLIPS[:2]

    def body(v_ref, o_ref, got_ref, send_sems, recv_sems):
        me = _me()
        o_ref[...] = v_ref[...]
        for i, flip in enumerate(flips):
            cp = pltpu.make_async_remote_copy(
                src_ref=o_ref, dst_ref=got_ref.at[i], send_sem=send_sems.at[i], recv_sem=recv_sems.at[i],
                device_id=_flipped(me, flip), device_id_type=MESH)
            cp.start()
            cp.wait()
            o_ref[...] = o_ref[...] + got_ref[i]

    vm = pl.BlockSpec(memory_space=pltpu.VMEM)
    return pl.pallas_call(
        body, name="all_reduce_small", in_specs=[vm], out_specs=vm, out_shape=jax.ShapeDtypeStruct(v.shape, F32),
        scratch_shapes=[pltpu.VMEM((len(flips),) + v.shape, F32), pltpu.SemaphoreType.DMA((len(flips),)),
                        pltpu.SemaphoreType.DMA((len(flips),))],
    )(v)


def _perm_rows(wt):
    pad = jnp.zeros((D_IN_PAD - D_IN_PROJ, wt.shape[1]), wt.dtype)
    return jnp.concatenate([wt[:DT_LO], wt[DT_HI:], wt[DT_LO:DT_HI], pad], axis=0)


def _unperm_rows(dwt):
    n = D_IN_PROJ - (DT_HI - DT_LO)
    return jnp.concatenate([dwt[:DT_LO], dwt[n:D_IN_PROJ], dwt[DT_LO:n]], axis=0)


def _pad_lanes(v):
    return jnp.pad(v, ((0, 0), (0, LANE - v.shape[1])))[:, None]


def _block_diag(w):
    l, g, n, _ = w.shape
    out = jnp.zeros((l, g * n, g * n), w.dtype)
    for i in range(g):
        out = out.at[:, i * n:(i + 1) * n, i * n:(i + 1) * n].set(w[:, i])
    return out


def _pack(groups):
    flat = []
    for grp in groups:
        parts = [a.reshape(-1) for a in (grp if isinstance(grp, (list, tuple)) else [grp])]
        n = sum(p.shape[0] for p in parts)
        if -n % LANE:
            parts.append(jnp.zeros((-n % LANE,), parts[0].dtype))
        flat += parts
    return jnp.concatenate(flat).reshape(-1, LANE)


def _unpack(buf, shapes):
    out = []
    lo = 0
    buf = buf.reshape(-1)
    for shp in shapes:
        n = 1
        for k in shp:
            n *= k
        out.append(buf[lo:lo + n].reshape(shp))
        lo += n + (-n % LANE)
    return out


def small_params(w, conv_w_full):
    return dict(
        n1w=w["norm1_w"][:, None], cw=jnp.pad(conv_w_full, ((0, 0), (0, 8 - CONV_WIDTH), (0, 0))),
        cb=w["conv_b"][:, None], dtb=_pad_lanes(w["dt_bias"]), alog=_pad_lanes(w["a_log"]), dsk=_pad_lanes(w["d_skip"]),
        snw=w["ssd_norm_w"][:, None], wblk=_block_diag(w["pool_w"]), pb=w["pool_b"].reshape(-1, 1, POOL_WIDTH),
        ps=w["pool_scale"][:, None], n2w=w["norm2_w"][:, None])


MIX = ("w_in", "w_out")
FFN = ("w_gate", "w_up", "w_down")


def layer_params(small, l):
    return {k: v[l] for k, v in small.items()}


def mix_weights(whole):
    return _perm_rows(whole[0]), whole[1]


def _slabs(g):
    return g.reshape(N_DEV, -1, g.shape[-1])


def _layer_fwd(x, p, mix, ffn=None, ffn_shards=None, next_mix_shards=None):
    ici = [gather_ici([sh]) for sh in ffn_shards] if ffn_shards is not None else [None] * len(FFN)
    (z, xbc, qkv, pp, dtr, h1), (blk_g,) = inproj_fwd(x, p["n1w"], mix[0], [ici[0]])
    (u,), _ = conv_fwd(xbc, p["cw"], p["cb"])
    (y_ssd, st), (blk_u,) = ssd_fwd(u, z, dtr, p["dtb"], p["alog"], p["dsk"], p["snw"], [ici[1]])
    (o, tot, swept), (blk_d,) = sb_fwd(qkv, [ici[2]])
    blocks = blk_g + blk_u + blk_d if ffn_shards is not None else None
    yp = pool_fwd(pp, p["wblk"], p["pb"], p["ps"])
    (x_mid, ycat), (blocks,) = outproj_fwd([y_ssd, o, yp], mix[1], x,
                                           [gather_d2d(blocks) if blocks is not None else None])
    if blocks is not None:
        ffn = gathered(blocks)
    (x_out, g, uu), (nxt,) = ffn_fwd(x_mid, p["n2w"], *ffn,
                                     [gather_ici(next_mix_shards) if next_mix_shards is not None else None])
    sv = dict(x=x, z=z, xbc=xbc, qkv=qkv, pp=pp, dtr=dtr, h1=h1, u=u, st=st, tot=tot, swept=swept, ycat=ycat,
              x_mid=x_mid, g=g, uu=uu, w_in=mix[0], w_out=mix[1], wg=ffn[0], wu=ffn[1], wd=ffn[2])
    return x_out, sv, nxt


def _layer_bwd(dxo, sv, p, pending_mix=None, exchange=False, into_ffn=(), into_mix=()):
    (dx_mid, dn2w, a, dg, du, h2), (sib,) = ffn_bwd(
        dxo, sv["x_mid"], sv["g"], sv["uu"], p["n2w"], sv["wg"], sv["wu"], sv["wd"],
        [scatter_d2d(pending_mix) if pending_mix is not None else None])
    sums_mix = chip_sums(pending_mix, sib) if pending_mix is not None else None
    dwg, dwu = mm_tn([dg, du], h2)
    gr = dict(norm2_w=dn2w[0], w_gate=dwg, w_up=dwu, w_down=mm_tn(a, dxo))
    parts = [_slabs(gr[k]) for k in FFN] if exchange else None
    (dycat,), (sib,) = outproj_bwd(dx_mid, sv["w_out"], [scatter_d2d(parts) if exchange else None])
    sums_ffn = chip_sums(parts, sib) if exchange else None
    gr["w_out"] = mm_tn(sv["ycat"], dx_mid)
    (dp, dwblk, dpb, dps), _ = pool_bwd(sv["pp"], p["wblk"], p["pb"], p["ps"], dycat,
                                        (SSD_WIDTH + SB_WIDTH) // POOL_WIDTH)
    n = POOL_GROUP_DIM
    gr["pool_w"] = jnp.stack([dwblk[i * n:(i + 1) * n, i * n:(i + 1) * n] for i in range(len(POOL_WINDOWS))])
    gr["pool_b"] = dpb.reshape(len(POOL_WINDOWS), n)
    gr["pool_scale"] = dps[0]
    (dq, dk, dv), (got_ud,) = sb_bwd(sv["qkv"], sv["tot"], sv["swept"], dycat, SSD_WIDTH // LANE,
                                     [scatter_ici(sums_ffn[1:]) if exchange else None])
    (du_, dz, ddtr, ddtb, dalog, ddsk, dsnw), (got_mix,) = ssd_bwd(
        sv["u"], sv["z"], sv["dtr"], sv["st"], dycat, p["dtb"], p["alog"], p["dsk"], p["snw"],
        [scatter_ici(sums_mix) if sums_mix is not None else None])
    done_mix = device_sums(sums_mix, got_mix, *into_mix) if sums_mix is not None else None
    gr.update(dt_bias=ddtb[0, :SSD_HEADS], a_log=dalog[0, :SSD_HEADS], d_skip=ddsk[0, :SSD_HEADS], ssd_norm_w=dsnw[0])
    (dxbc, dcw, dcb), (got_g,) = conv_bwd(sv["xbc"], p["cw"], p["cb"], du_,
                                          [scatter_ici(sums_ffn[:1]) if exchange else None])
    done_ffn = device_sums(sums_ffn, got_g + got_ud, *into_ffn) if exchange else None
    gr.update(conv_w=dcw[:CONV_WIDTH], conv_b=dcb[0])
    dx, dn1w, dproj = inproj_bwd([dz, dxbc, dq, dk, dv, dp, ddtr], sv["w_in"], sv["x"], p["n1w"], dx_mid)
    gr.update(norm1_w=dn1w[0], w_in=_unperm_rows(mm_tn(dproj, sv["h1"])))
    return dx, gr, done_ffn, done_mix


def local_step(x, tgt, params, weights, final_w):
    saved = []
    for p, (mix, ffn) in zip(params, weights):
        x, sv, _ = _layer_fwd(x, p, mix, ffn)
        saved.append(sv)
    loss, dx, dfw = head_loss(x, final_w[None], tgt)
    grads = []
    for p, sv in zip(reversed(params), reversed(saved)):
        dx, gr, _, _ = _layer_bwd(dx, sv, p)
        grads.append(gr)
    grads.reverse()
    return loss, dx, dfw[0], grads


WEIGHTS = ("norm1_w", "w_in", "conv_w", "conv_b", "dt_bias", "a_log", "d_skip", "ssd_norm_w", "pool_w", "pool_b",
           "pool_scale", "w_out", "norm2_w", "w_gate", "w_up", "w_down", "final_norm_w")
COL_SHARDED = ("w_in", "w_gate", "w_up")
ROW_SHARDED = ("w_out", "w_down")
SMALL = tuple(k for k in WEIGHTS if k not in COL_SHARDED + ROW_SHARDED)


def kernel(x, norm1_w, w_in, conv_w, conv_b, dt_bias, a_log, d_skip, ssd_norm_w, pool_w, pool_b, pool_scale, w_out, norm2_w, w_gate, w_up, w_down, final_norm_w, loss_target, m_norm1_w, m_w_in, m_conv_w, m_conv_b, m_dt_bias, m_a_log, m_d_skip, m_ssd_norm_w, m_pool_w, m_pool_b, m_pool_scale, m_w_out, m_norm2_w, m_w_gate, m_w_up, m_w_down, m_final_norm_w, v_norm1_w, v_w_in, v_conv_w, v_conv_b, v_dt_bias, v_a_log, v_d_skip, v_ssd_norm_w, v_pool_w, v_pool_b, v_pool_scale, v_w_out, v_norm2_w, v_w_gate, v_w_up, v_w_down, v_final_norm_w):
    args = dict(locals())
    w = {k: args[k] for k in WEIGHTS}
    m = {k: args["m_" + k] for k in WEIGHTS}
    v = {k: args["v_" + k] for k in WEIGHTS}
    depth = w_in.shape[0]
    dev = _dev(_me())
    n_cw = conv_w.shape[-1]

    shards = {k: (jnp.swapaxes(w[k], 1, 2) if k in COL_SHARDED else w[k]).astype(BF16) for k in MIX + FFN}
    whole = all_gather([jnp.swapaxes(conv_w, 0, 2).reshape(n_cw, -1)] + [shards[k][0] for k in MIX])
    conv_w_full = jnp.swapaxes(whole[0].reshape(N_DEV * n_cw, CONV_WIDTH, depth), 0, 2)
    small = small_params(w, conv_w_full)
    xs = x[0]
    params, saved = [layer_params(small, l) for l in range(depth)], []
    mix = mix_weights(whole[1:])
    for l in range(depth):
        xs, sv, nxt = _layer_fwd(xs, params[l], mix, ffn_shards=[shards[k][l] for k in FFN],
                                 next_mix_shards=[shards[k][l + 1] for k in MIX] if l + 1 < depth else None)
        saved.append(sv)
        if nxt is not None:
            mix = mix_weights(gathered(gather_d2d(nxt).run("gather_d2d")))
    loss, dx, dfw = head_loss(xs, final_norm_w[None], loss_target[0])
    layer_grads = [None] * depth
    native = {k: lax.empty((depth,) + shards[k].shape[1:], F32) for k in MIX + FFN}
    pending = None
    for l in reversed(range(depth)):
        dx, layer_grads[l], done_ffn, done_mix = _layer_bwd(
            dx, saved[l], params[l], pending, exchange=True, into_ffn=([native[k] for k in FFN], l),
            into_mix=([native[k] for k in MIX], l + 1))
        native.update(zip(FFN, done_ffn))
        if pending is not None:
            native.update(zip(MIX, done_mix))
        pending = [_slabs(layer_grads[l][k]) for k in MIX]

    delta, new_m, new_v = {}, {}, {}
    sums = chip_sums(pending, scatter_d2d(pending).run("scatter_d2d"))
    as_native = lambda t, k: jnp.swapaxes(t[k], 1, 2) if k in COL_SHARDED else t[k]
    outs, (got,) = adamw_many([as_native(w, k) for k in FFN], [native[k] for k in FFN],
                              [as_native(m, k) for k in FFN], [as_native(v, k) for k in FFN], [scatter_ici(sums)])
    for dst, arrs in zip((delta, new_m, new_v), outs):
        dst.update({k: as_native({k: a}, k) for k, a in zip(FFN, arrs)})
    native.update(zip(MIX, device_sums(sums, got, [native[k] for k in MIX], 0)))
    grads = {k: jnp.swapaxes(native[k], 1, 2) if k in COL_SHARDED else native[k] for k in MIX + FFN}
    layered = [k for k in SMALL if k != "final_norm_w"]
    small_shapes = [(1, LANE)] + [(depth,) + layer_grads[0][k].shape for k in layered] + [dfw[0].shape]
    packed = _pack([loss] + [[layer_grads[l][k] for l in range(depth)] for k in layered] + [dfw[0]])
    summed = _unpack(all_reduce_small(packed), small_shapes)
    loss = summed[0][0, 0]
    grads.update(zip(layered + ["final_norm_w"], summed[1:]))
    grads["conv_w"] = lax.dynamic_slice_in_dim(grads["conv_w"], dev * n_cw, n_cw, axis=2)

    for k in MIX:
        delta[k], new_m[k], new_v[k] = adamw(w[k], grads[k], m[k], v[k])
    two_d = lambda a: a.reshape(1, -1) if a.ndim == 1 else a
    outs = adamw_small(*[[two_d(t[k]) for k in SMALL] for t in (w, grads, m, v)])
    for dst, arrs in zip((delta, new_m, new_v), outs):
        dst.update({k: a.reshape(w[k].shape) for k, a in zip(SMALL, arrs)})
    return (loss, dx[None], *[grads[k] for k in WEIGHTS], *[delta[k] for k in WEIGHTS],
            *[new_m[k] for k in WEIGHTS], *[new_v[k] for k in WEIGHTS])
```

```python
import functools

import jax
import jax.numpy as jnp
from jax import lax
from jax.experimental import pallas as pl
from jax.experimental.pallas import tpu as pltpu

F32 = jnp.float32
BF16 = jnp.bfloat16
HIGHEST = lax.Precision.HIGHEST
MESH = pl.DeviceIdType.MESH

EPS = 1e-6
D_MODEL = 1024
SSD_WIDTH = 512
SSD_HEADS = 8
HEAD_DIM = 64
D_STATE = 128
CONV_WIDTH = 4
CONV_DIM = 1024
SB_WIDTH = 256
POOL_WIDTH = 256
POOL_WINDOWS = (2, 4, 8, 16)
D_IN_PROJ = 2568
D_FF = 2816
N_DEV = 8
SEG = (512, 1024, 768, 256, 128)
D_IN_PAD = sum(SEG)
DT_LO, DT_HI = 1536, 1544

LANE = 128
BLK = 128
ROW_TILE = 256
ROW_TILE_LIGHT = 512
VMEM_LIMIT = 56 * 2**20

ADAM_LR, ADAM_B1, ADAM_B2, ADAM_EPS, ADAM_WD, ADAM_STEP = 0.001, 0.9, 0.999, 1e-08, 0.01, 10


def _params(n_axes=1, vmem=None):
    return pltpu.CompilerParams(dimension_semantics=("arbitrary",) * n_axes, vmem_limit_bytes=vmem)


def _dot(a, b, dims, exact=False):
    if exact:
        return lax.dot_general(a.astype(F32), b.astype(F32), (dims, ((), ())), precision=HIGHEST,
                               preferred_element_type=F32)
    return lax.dot_general(a.astype(BF16), b.astype(BF16), (dims, ((), ())), preferred_element_type=F32)


def dot_nn(a, b, exact=False):
    return _dot(a, b, ((1,), (0,)), exact)


def dot_nt(a, b, exact=False):
    return _dot(a, b, ((1,), (1,)), exact)


def dot_tn(a, b, exact=False):
    return _dot(a, b, ((0,), (0,)), exact)


def _iota(shape, axis):
    return lax.broadcasted_iota(jnp.int32, shape, axis)


def _lane_col(x, h):
    return jnp.sum(jnp.where(_iota(x.shape, 1) == h, x, 0.0), axis=1, keepdims=True)


def _sub_row(x, h):
    return jnp.sum(jnp.where(_iota(x.shape, 0) == h, x, 0.0), axis=0, keepdims=True)


def _sigmoid(x):
    return 1.0 / (1.0 + jnp.exp(-x))


def _rms_fwd(x, w):
    r = lax.rsqrt(jnp.mean(x * x, axis=-1, keepdims=True) + EPS)
    return x * r * w


def _rms_bwd(x, w, dy):
    r = lax.rsqrt(jnp.mean(x * x, axis=-1, keepdims=True) + EPS)
    xh = x * r
    dxh = dy * w
    dx = r * (dxh - xh * jnp.mean(dxh * xh, axis=-1, keepdims=True))
    return dx, jnp.sum(dy * xh, axis=0, keepdims=True)


def _acc(ref, first, val):
    @pl.when(first)
    def _():
        ref[...] = val

    @pl.when(jnp.logical_not(first))
    def _():
        ref[...] += val


def _row_spec(tm, n):
    return pl.BlockSpec((tm, n), lambda i: (i, 0))


def _full_spec(shape):
    return pl.BlockSpec(shape, lambda *_: (0,) * len(shape))


def inproj_fwd(x, nw, w, sides=()):
    s, d = x.shape
    tm = min(ROW_TILE_LIGHT, s)
    seg_dtypes = (F32, F32, BF16, F32, F32)

    def body(x_ref, nw_ref, w_ref, z_ref, xbc_ref, qkv_ref, p_ref, dt_ref, h_ref):
        h = _rms_fwd(x_ref[...], nw_ref[...]).astype(BF16)
        h_ref[...] = h
        lo = 0
        for ref, n in zip((z_ref, xbc_ref, qkv_ref, p_ref, dt_ref), SEG):
            ref[...] = dot_nt(h, w_ref[lo:lo + n, :]).astype(ref.dtype)
            lo += n

    return hosted_call(
        body, sides, name="inproj_fwd", grid=(s // tm,),
        in_specs=[_row_spec(tm, d), _full_spec((1, d)), _full_spec(w.shape)],
        out_specs=[_row_spec(tm, n) for n in SEG] + [_row_spec(tm, d)],
        out_shape=[jax.ShapeDtypeStruct((s, n), t) for n, t in zip(SEG, seg_dtypes)]
        + [jax.ShapeDtypeStruct((s, d), BF16)],
        scratch_shapes=[], compiler_params=_params(1, VMEM_LIMIT), operands=(x, nw, w))


def inproj_bwd(pieces, w, x, nw, dres):
    s, d = x.shape
    tm = min(ROW_TILE_LIGHT, s)
    n_p = len(pieces)
    widths = [p.shape[1] for p in pieces]

    def body(*refs):
        w_ref, x_ref, nw_ref, dres_ref, dx_ref, dnw_ref, dp_ref = refs[n_p:]
        dh = None
        lo = 0
        for ref, n in zip(refs[:n_p], widths):
            piece = ref[...].astype(BF16)
            dp_ref[:, lo:lo + n] = piece
            part = dot_nn(piece, w_ref[lo:lo + n, :])
            dh = part if dh is None else dh + part
            lo += n
        dx, dnw = _rms_bwd(x_ref[...], nw_ref[...], dh)
        dx_ref[...] = dres_ref[...] + dx
        _acc(dnw_ref, pl.program_id(0) == 0, dnw)

    return pl.pallas_call(
        body, name="inproj_bwd", grid=(s // tm,),
        in_specs=[_row_spec(tm, n) for n in widths] + [_full_spec(w.shape), _row_spec(tm, d), _full_spec((1, d)),
                                                       _row_spec(tm, d)],
        out_specs=[_row_spec(tm, d), _full_spec((1, d)), _row_spec(tm, sum(widths))],
        out_shape=[jax.ShapeDtypeStruct((s, d), F32), jax.ShapeDtypeStruct((1, d), F32),
                   jax.ShapeDtypeStruct((s, sum(widths)), BF16)],
        compiler_params=_params(1, VMEM_LIMIT),
    )(*pieces, w, x, nw, dres)


def outproj_fwd(pieces, w, res, sides=()):
    s, d = res.shape
    tm = min(ROW_TILE_LIGHT, s)
    n_p = len(pieces)
    widths = [p.shape[1] for p in pieces]

    def body(*refs):
        w_ref, r_ref, o_ref, y_ref = refs[n_p:]
        acc = r_ref[...]
        lo = 0
        for ref, n in zip(refs[:n_p], widths):
            piece = ref[...].astype(BF16)
            y_ref[:, lo:lo + n] = piece
            acc = acc + dot_nn(piece, w_ref[lo:lo + n, :])
            lo += n
        o_ref[...] = acc

    return hosted_call(
        body, sides, name="outproj_fwd", grid=(s // tm,),
        in_specs=[_row_spec(tm, n) for n in widths] + [_full_spec(w.shape), _row_spec(tm, d)],
        out_specs=[_row_spec(tm, d), _row_spec(tm, sum(widths))],
        out_shape=[jax.ShapeDtypeStruct((s, d), F32), jax.ShapeDtypeStruct((s, sum(widths)), BF16)],
        scratch_shapes=[], compiler_params=_params(1, VMEM_LIMIT), operands=(*pieces, w, res))


def outproj_bwd(dx, w, sides=()):
    s, d = dx.shape
    tm = min(ROW_TILE_LIGHT, s)

    def body(dx_ref, w_ref, o_ref):
        o_ref[...] = dot_nt(dx_ref[...], w_ref[...])

    return hosted_call(
        body, sides, name="outproj_bwd", grid=(s // tm,),
        in_specs=[_row_spec(tm, d), _full_spec(w.shape)],
        out_specs=[_row_spec(tm, w.shape[0])], out_shape=[jax.ShapeDtypeStruct((s, w.shape[0]), F32)],
        scratch_shapes=[], compiler_params=_params(1, VMEM_LIMIT), operands=(dx, w))


def ffn_fwd(x, nw, wg, wu, wd, sides=()):
    s, d = x.shape
    f = wg.shape[0]
    tm = min(ROW_TILE, s)

    def body(x_ref, nw_ref, wg_ref, wu_ref, wd_ref, o_ref, g_ref, u_ref):
        xv = x_ref[...]
        h = _rms_fwd(xv, nw_ref[...]).astype(BF16)
        g = dot_nt(h, wg_ref[...])
        u = dot_nt(h, wu_ref[...])
        g_ref[...] = g.astype(BF16)
        u_ref[...] = u.astype(BF16)
        o_ref[...] = xv + dot_nn(g * _sigmoid(g) * u, wd_ref[...])

    return hosted_call(
        body, sides, name="ffn_fwd", grid=(s // tm,),
        in_specs=[_row_spec(tm, d), _full_spec((1, d)), _full_spec(wg.shape), _full_spec(wu.shape),
                  _full_spec(wd.shape)],
        out_specs=[_row_spec(tm, d), _row_spec(tm, f), _row_spec(tm, f)],
        out_shape=[jax.ShapeDtypeStruct((s, d), F32), jax.ShapeDtypeStruct((s, f), BF16),
                   jax.ShapeDtypeStruct((s, f), BF16)],
        scratch_shapes=[], compiler_params=_params(1, VMEM_LIMIT), operands=(x, nw, wg, wu, wd))


def ffn_bwd(dxo, x, g, u, nw, wg, wu, wd, sides=()):
    s, d = x.shape
    f = wg.shape[0]
    tm = min(ROW_TILE, s)

    def body(dxo_ref, x_ref, g_ref, u_ref, nw_ref, wg_ref, wu_ref, wd_ref, dx_ref, dnw_ref, a_ref, dg_ref,
             du_ref, h_ref):
        dxo_v = dxo_ref[...]
        xv = x_ref[...]
        da = dot_nt(dxo_v, wd_ref[...])
        gv = g_ref[...].astype(F32)
        uv = u_ref[...].astype(F32)
        sg = _sigmoid(gv)
        sl = gv * sg
        a_ref[...] = (sl * uv).astype(BF16)
        dg = (da * uv * (sg * (1.0 + gv * (1.0 - sg)))).astype(BF16)
        du = (da * sl).astype(BF16)
        dg_ref[...] = dg
        du_ref[...] = du
        dh = dot_nn(dg, wg_ref[...]) + dot_nn(du, wu_ref[...])
        h_ref[...] = _rms_fwd(xv, nw_ref[...]).astype(BF16)
        dx, dnw = _rms_bwd(xv, nw_ref[...], dh)
        dx_ref[...] = dxo_v + dx
        _acc(dnw_ref, pl.program_id(0) == 0, dnw)

    return hosted_call(
        body, sides, name="ffn_bwd", grid=(s // tm,),
        in_specs=[_row_spec(tm, d), _row_spec(tm, d), _row_spec(tm, f), _row_spec(tm, f), _full_spec((1, d)),
                  _full_spec(wg.shape), _full_spec(wu.shape), _full_spec(wd.shape)],
        out_specs=[_row_spec(tm, d), _full_spec((1, d)), _row_spec(tm, f), _row_spec(tm, f), _row_spec(tm, f),
                   _row_spec(tm, d)],
        out_shape=[jax.ShapeDtypeStruct((s, d), F32), jax.ShapeDtypeStruct((1, d), F32),
                   jax.ShapeDtypeStruct((s, f), BF16), jax.ShapeDtypeStruct((s, f), BF16),
                   jax.ShapeDtypeStruct((s, f), BF16), jax.ShapeDtypeStruct((s, d), BF16)],
        scratch_shapes=[], compiler_params=_params(1, VMEM_LIMIT), operands=(dxo, x, g, u, nw, wg, wu, wd))


def _tile(n, cap=256):
    best = LANE
    for t in range(LANE, cap + 1, LANE):
        if n % t == 0:
            best = t
    return best


def mm_tn(a, b):
    many = isinstance(a, (list, tuple))
    a_list = list(a) if many else [a]
    n_a = len(a_list)
    s, k = a_list[0].shape
    n = b.shape[1]
    tk = _tile(k)

    def body(*refs):
        b_val = refs[n_a][...]
        for a_ref, o_ref in zip(refs[:n_a], refs[n_a + 1:]):
            o_ref[...] = dot_nn(a_ref[...].astype(BF16).T, b_val).astype(BF16)

    outs = pl.pallas_call(
        body, name="mm_tn", grid=(k // tk,),
        in_specs=[pl.BlockSpec((s, tk), lambda i: (0, i))] * n_a + [_full_spec((s, n))],
        out_specs=[pl.BlockSpec((tk, n), lambda i: (i, 0))] * n_a,
        out_shape=[jax.ShapeDtypeStruct((k, n), BF16)] * n_a, compiler_params=_params(1, VMEM_LIMIT),
    )(*a_list, b)
    return outs if many else outs[0]


def head_loss(x, fw, tgt):
    s, d = x.shape
    tm = min(ROW_TILE, s)

    def body(x_ref, fw_ref, t_ref, loss_ref, dx_ref, dfw_ref):
        xv = x_ref[...]
        err = _rms_fwd(xv, fw_ref[...]) - t_ref[...]
        part = jnp.zeros((1, LANE), F32) + 0.5 * jnp.sum(err * err) / d
        dx, dfw = _rms_bwd(xv, fw_ref[...], err / d)
        dx_ref[...] = dx
        first = pl.program_id(0) == 0
        _acc(loss_ref, first, part)
        _acc(dfw_ref, first, dfw)

    return pl.pallas_call(
        body, name="head_loss", grid=(s // tm,),
        in_specs=[_row_spec(tm, d), _full_spec((1, d)), _row_spec(tm, d)],
        out_specs=[_full_spec((1, LANE)), _row_spec(tm, d), _full_spec((1, d))],
        out_shape=[jax.ShapeDtypeStruct((1, LANE), F32), jax.ShapeDtypeStruct((s, d), F32),
                   jax.ShapeDtypeStruct((1, d), F32)],
        compiler_params=_params(1),
    )(x, fw, tgt)


HALO = 8


def _conv_pre(ext, cw_ref, cb_ref):
    shifted = [pltpu.roll(ext, CONV_WIDTH - 1 - i, 0)[HALO:] if i < CONV_WIDTH - 1 else ext[HALO:]
               for i in range(CONV_WIDTH)]
    acc = cb_ref[...] + sum(cw_ref[i:i + 1, :] * shifted[i] for i in range(CONV_WIDTH))
    return acc, shifted


def _halo_spec(n, block_of_step):
    return pl.BlockSpec((HALO, n), lambda i: (jnp.maximum(block_of_step(i) * (BLK // HALO) - 1, 0), 0))


def conv_fwd(xbc, cw, cb, sides=()):
    s, n = xbc.shape

    def body(cur_ref, prev_ref, cw_ref, cb_ref, o_ref):
        prev = jnp.where(pl.program_id(0) > 0, prev_ref[...], 0.0)
        acc, _ = _conv_pre(jnp.concatenate([prev, cur_ref[...]], axis=0), cw_ref, cb_ref)
        o_ref[...] = acc * _sigmoid(acc)

    return hosted_call(
        body, sides, name="conv_fwd", grid=(s // BLK,),
        in_specs=[pl.BlockSpec((BLK, n), lambda c: (c, 0)), _halo_spec(n, lambda c: c),
                  _full_spec(cw.shape), _full_spec((1, n))],
        out_specs=[pl.BlockSpec((BLK, n), lambda c: (c, 0))], out_shape=[jax.ShapeDtypeStruct((s, n), F32)],
        scratch_shapes=[], compiler_params=_params(1), operands=(xbc, xbc, cw, cb))


def conv_bwd(xbc, cw, cb, du, sides=()):
    s, n = xbc.shape
    nb = s // BLK

    def body(cur_ref, prev_ref, cw_ref, cb_ref, du_ref, dx_ref, dcw_ref, dcb_ref, nxt_ref):
        i = pl.program_id(0)
        c = nb - 1 - i
        prev = jnp.where(c > 0, prev_ref[...], 0.0)
        acc, shifted = _conv_pre(jnp.concatenate([prev, cur_ref[...]], axis=0), cw_ref, cb_ref)
        sg = _sigmoid(acc)
        dacc = du_ref[...] * (sg * (1.0 + acc * (1.0 - sg)))

        @pl.when(i == 0)
        def _():
            nxt_ref[...] = jnp.zeros_like(nxt_ref)
            dcw_ref[...] = jnp.zeros_like(dcw_ref)
            dcb_ref[...] = jnp.zeros_like(dcb_ref)

        dcb_ref[...] += jnp.sum(dacc, axis=0, keepdims=True)
        for t in range(CONV_WIDTH):
            dcw_ref[t:t + 1, :] += jnp.sum(dacc * shifted[t], axis=0, keepdims=True)
        ext = jnp.concatenate([dacc, nxt_ref[...]], axis=0)
        dx = cw_ref[CONV_WIDTH - 1:CONV_WIDTH, :] * dacc
        for t in range(CONV_WIDTH - 1):
            dx += cw_ref[t:t + 1, :] * pltpu.roll(ext, BLK + HALO - (CONV_WIDTH - 1 - t), 0)[:BLK]
        dx_ref[...] = dx.astype(dx_ref.dtype)
        nxt_ref[...] = dacc[:HALO]

    rev = lambda i: (nb - 1 - i, 0)
    return hosted_call(
        body, sides, name="conv_bwd", grid=(nb,),
        in_specs=[pl.BlockSpec((BLK, n), rev), _halo_spec(n, lambda i: nb - 1 - i),
                  _full_spec(cw.shape), _full_spec((1, n)), pl.BlockSpec((BLK, n), rev)],
        out_specs=[pl.BlockSpec((BLK, n), rev), _full_spec((8, n)), _full_spec((1, n))],
        out_shape=[jax.ShapeDtypeStruct((s, n), BF16), jax.ShapeDtypeStruct((8, n), F32),
                   jax.ShapeDtypeStruct((1, n), F32)],
        scratch_shapes=[pltpu.VMEM((HALO, n), F32)],
        compiler_params=_params(1), operands=(xbc, xbc, cw, cb, du))


N_PAIR = SSD_HEADS // 2
B_LO = SSD_WIDTH
C_LO = SSD_WIDTH + 2 * D_STATE


def _softplus(x):
    return jnp.maximum(x, 0.0) + jnp.log(1.0 + jnp.exp(-jnp.abs(x)))


def _ssd_chunk(u_ref, dt_ref, dtb_ref, alog_ref):
    shape = (BLK, BLK)
    tri = _iota(shape, 1) <= _iota(shape, 0)
    pre = dt_ref[...] + dtb_ref[...]
    dt = _softplus(pre)
    a = -jnp.exp(alog_ref[...])
    acum = dot_nn(tri.astype(F32), dt * a, exact=True)
    acum_t = acum.T
    last = _sub_row(acum, BLK - 1)
    heads = []
    for h in range(SSD_HEADS):
        col = _lane_col(acum, h)
        seg = jnp.where(tri, col - _sub_row(acum_t, h), -1e30)
        heads.append(dict(col=col, dm=jnp.exp(seg), dt=_lane_col(dt, h), last=_lane_col(last, h)))
    return tri, pre, dt, a, heads


def _pair_mix(lo_mask, v0, v1):
    return jnp.where(lo_mask, v0, v1)


def ssd_fwd(u, z, dtr, dtb, alog, dsk, nw, sides=()):
    s = u.shape[0]
    nc = s // BLK

    def body(u_ref, z_ref, dt_ref, dtb_ref, alog_ref, dsk_ref, nw_ref, y_ref, st_ref, s_ref):
        @pl.when(pl.program_id(0) == 0)
        def _():
            s_ref[...] = jnp.zeros_like(s_ref)

        _, _, _, _, heads = _ssd_chunk(u_ref, dt_ref, dtb_ref, alog_ref)
        lo_lane = _iota((BLK, LANE), 1) < HEAD_DIM
        lo_sub = _iota((BLK, LANE), 0) < HEAD_DIM
        ys = []
        for p in range(N_PAIR):
            g = p // 2
            h0, h1 = heads[2 * p], heads[2 * p + 1]
            bg = u_ref[:, B_LO + g * D_STATE:B_LO + (g + 1) * D_STATE]
            cg = u_ref[:, C_LO + g * D_STATE:C_LO + (g + 1) * D_STATE]
            xs = u_ref[:, p * LANE:(p + 1) * LANE]
            xp = xs * _pair_mix(lo_lane, h0["dt"], h1["dt"])
            gm = dot_nt(cg, bg)
            yd = _pair_mix(lo_lane, dot_nn(gm * h0["dm"], xp), dot_nn(gm * h1["dm"], xp))
            sp = s_ref[p]
            st_ref[0, p] = sp
            yo = _pair_mix(lo_lane, jnp.exp(h0["col"]), jnp.exp(h1["col"])) * dot_nt(cg, sp)
            dskp = _pair_mix(lo_lane, _lane_col(dsk_ref[...], 2 * p), _lane_col(dsk_ref[...], 2 * p + 1))
            ys.append(yd + yo + xs * dskp)
            wp = _pair_mix(lo_lane, jnp.exp(h0["last"] - h0["col"]), jnp.exp(h1["last"] - h1["col"]))
            el = _pair_mix(lo_sub, jnp.exp(h0["last"]), jnp.exp(h1["last"]))
            s_ref[p] = el * sp + dot_tn(wp * xp, bg)
        y = jnp.concatenate(ys, axis=1)
        zv = z_ref[...]
        y_ref[...] = _rms_fwd(y * zv * _sigmoid(zv), nw_ref[...])

    vec = _full_spec((1, LANE))
    return hosted_call(
        body, sides, name="ssd_fwd", grid=(nc,),
        in_specs=[_row_spec(BLK, CONV_DIM), _row_spec(BLK, SSD_WIDTH), _row_spec(BLK, LANE), vec, vec, vec,
                  _full_spec((1, SSD_WIDTH))],
        out_specs=[_row_spec(BLK, SSD_WIDTH), pl.BlockSpec((1, N_PAIR, LANE, D_STATE), lambda c: (c, 0, 0, 0))],
        out_shape=[jax.ShapeDtypeStruct((s, SSD_WIDTH), F32), jax.ShapeDtypeStruct((nc, N_PAIR, LANE, D_STATE), F32)],
        scratch_shapes=[pltpu.VMEM((N_PAIR, LANE, D_STATE), F32)],
        compiler_params=_params(1), operands=(u, z, dtr, dtb, alog, dsk, nw))


def ssd_bwd(u, z, dtr, st, dyo, dtb, alog, dsk, nw, sides=()):
    s = u.shape[0]
    nc = s // BLK

    def body(u_ref, z_ref, dt_ref, st_ref, dyo_ref, dtb_ref, alog_ref, dsk_ref, nw_ref,
             du_ref, dz_ref, ddt_ref, ddtb_ref, dalog_ref, ddsk_ref, dnw_ref, ds_ref):
        first = pl.program_id(0) == 0

        @pl.when(first)
        def _():
            ds_ref[...] = jnp.zeros_like(ds_ref)

        tri, pre, dt, a, heads = _ssd_chunk(u_ref, dt_ref, dtb_ref, alog_ref)
        shape = (BLK, LANE)
        lane = _iota(shape, 1)
        lo_lane = lane < HEAD_DIM
        lo_sub = _iota(shape, 0) < HEAD_DIM
        pairs = []
        ys = []
        for p in range(N_PAIR):
            g = p // 2
            h0, h1 = heads[2 * p], heads[2 * p + 1]
            bg = u_ref[:, B_LO + g * D_STATE:B_LO + (g + 1) * D_STATE]
            cg = u_ref[:, C_LO + g * D_STATE:C_LO + (g + 1) * D_STATE]
            xs = u_ref[:, p * LANE:(p + 1) * LANE]
            dtp = _pair_mix(lo_lane, h0["dt"], h1["dt"])
            xp = xs * dtp
            gm = dot_nt(cg, bg)
            m0, m1 = gm * h0["dm"], gm * h1["dm"]
            sp = st_ref[0, p]
            eap = _pair_mix(lo_lane, jnp.exp(h0["col"]), jnp.exp(h1["col"]))
            yo = eap * dot_nt(cg, sp)
            dskp = _pair_mix(lo_lane, _lane_col(dsk_ref[...], 2 * p), _lane_col(dsk_ref[...], 2 * p + 1))
            ys.append(_pair_mix(lo_lane, dot_nn(m0, xp), dot_nn(m1, xp)) + yo + xs * dskp)
            pairs.append(dict(bg=bg, cg=cg, xs=xs, dtp=dtp, xp=xp, gm=gm, m=(m0, m1), sp=sp, eap=eap, yo=yo, dskp=dskp))
        y = jnp.concatenate(ys, axis=1)
        zv = z_ref[...]
        sz = _sigmoid(zv)
        gate = zv * sz
        dyg, dnw = _rms_bwd(y * gate, nw_ref[...], dyo_ref[...])
        _acc(dnw_ref, first, dnw)
        dy = dyg * gate
        dz_ref[...] = (dyg * y * (sz * (1.0 + zv * (1.0 - sz)))).astype(dz_ref.dtype)

        zeros = jnp.zeros(shape, F32)
        dacum_col = zeros
        dacum_row = zeros
        ddt = zeros
        ddsk = jnp.zeros((1, LANE), F32)
        dlast = jnp.zeros((1, LANE), F32)
        head_row = _iota((1, LANE), 1)
        sub = _iota(shape, 0)
        db = [zeros, zeros]
        dc = [zeros, zeros]
        for p in range(N_PAIR):
            g = p // 2
            q = pairs[p]
            dyp = dy[:, p * LANE:(p + 1) * LANE]
            dsn = ds_ref[p]
            t = dyp * q["xs"]
            dxs = dyp * q["dskp"]
            dcs = dyp * q["eap"]
            dc[g] = dc[g] + dot_nn(dcs, q["sp"])
            dsp = dot_tn(dcs, q["cg"])
            dea = dyp * q["yo"]
            elp = _pair_mix(lo_sub, jnp.exp(heads[2 * p]["last"]), jnp.exp(heads[2 * p + 1]["last"]))
            dsp = dsp + elp * dsn
            dels = dsn * q["sp"] * elp
            wp = _pair_mix(lo_lane, jnp.exp(heads[2 * p]["last"] - heads[2 * p]["col"]),
                           jnp.exp(heads[2 * p + 1]["last"] - heads[2 * p + 1]["col"]))
            dv = dot_nt(q["bg"], dsn)
            db[g] = db[g] + dot_nn(wp * q["xp"], dsn)
            dxp = dv * wp
            dwv = dv * q["xp"] * wp
            dgm = zeros
            for k in range(2):
                h = 2 * p + k
                mine = lo_lane if k == 0 else jnp.logical_not(lo_lane)
                mine_sub = lo_sub if k == 0 else jnp.logical_not(lo_sub)
                dyh = jnp.where(mine, dyp, 0.0)
                dm = dot_nt(dyh, q["xp"])
                dxp = dxp + dot_tn(q["m"][k], dyh)
                dgm = dgm + dm * heads[h]["dm"]
                e = dm * q["m"][k]
                onehot = lane == h
                dw_col = jnp.sum(jnp.where(mine, dwv, 0.0), axis=1, keepdims=True)
                col = (jnp.sum(e, axis=1, keepdims=True) + jnp.sum(jnp.where(mine, dea, 0.0), axis=1, keepdims=True)
                       - dw_col)
                dacum_col = dacum_col + jnp.where(onehot, col, 0.0)
                dacum_row = dacum_row - jnp.where(sub == h, jnp.sum(e, axis=0, keepdims=True), 0.0)
                dl = jnp.sum(dw_col) + jnp.sum(jnp.where(mine_sub, dels, 0.0))
                dlast = dlast + jnp.where(head_row == h, dl, 0.0)
                ddsk = ddsk + jnp.where(head_row == h, jnp.sum(jnp.where(mine, t, 0.0)), 0.0)
            dc[g] = dc[g] + dot_nn(dgm, q["bg"])
            db[g] = db[g] + dot_tn(dgm, q["cg"])
            dxs = dxs + dxp * q["dtp"]
            tt = dxp * q["xs"]
            for k in range(2):
                mine = lo_lane if k == 0 else jnp.logical_not(lo_lane)
                ddt = ddt + jnp.where(lane == 2 * p + k, jnp.sum(jnp.where(mine, tt, 0.0), axis=1, keepdims=True), 0.0)
            du_ref[:, p * LANE:(p + 1) * LANE] = dxs
            ds_ref[p] = dsp
        for g in range(2):
            du_ref[:, B_LO + g * D_STATE:B_LO + (g + 1) * D_STATE] = db[g]
            du_ref[:, C_LO + g * D_STATE:C_LO + (g + 1) * D_STATE] = dc[g]
        dacum = dacum_col + dacum_row.T + jnp.where(sub == BLK - 1, dlast, 0.0)
        dda = dot_tn(tri.astype(F32), dacum, exact=True)
        ddt = ddt + dda * a
        _acc(dalog_ref, first, jnp.sum(dda * dt, axis=0, keepdims=True) * a)
        dpre = ddt * _sigmoid(pre)
        ddt_ref[...] = dpre.astype(ddt_ref.dtype)
        _acc(ddtb_ref, first, jnp.sum(dpre, axis=0, keepdims=True))
        _acc(ddsk_ref, first, ddsk)

    rev = lambda i: (nc - 1 - i, 0)
    vec = _full_spec((1, LANE))
    rows = lambda n: pl.BlockSpec((BLK, n), rev)
    return hosted_call(
        body, sides, name="ssd_bwd", grid=(nc,),
        in_specs=[rows(CONV_DIM), rows(SSD_WIDTH), rows(LANE),
                  pl.BlockSpec((1, N_PAIR, LANE, D_STATE), lambda i: (nc - 1 - i, 0, 0, 0)), rows(SSD_WIDTH),
                  vec, vec, vec, _full_spec((1, SSD_WIDTH))],
        out_specs=[rows(CONV_DIM), rows(SSD_WIDTH), rows(LANE), vec, vec, vec, _full_spec((1, SSD_WIDTH))],
        out_shape=[jax.ShapeDtypeStruct((s, CONV_DIM), F32), jax.ShapeDtypeStruct((s, SSD_WIDTH), BF16),
                   jax.ShapeDtypeStruct((s, LANE), BF16)] + [jax.ShapeDtypeStruct((1, LANE), F32)] * 3
        + [jax.ShapeDtypeStruct((1, SSD_WIDTH), F32)],
        scratch_shapes=[pltpu.VMEM((N_PAIR, LANE, D_STATE), F32)],
        compiler_params=_params(1), operands=(u, z, dtr, st, dyo, dtb, alog, dsk, nw))


SB_PAIRS = SB_WIDTH // LANE
SB_SCALE = HEAD_DIM ** -0.5


SB_TQ = 256


def _sb_tq(s):
    return min(SB_TQ, s)


def _sb_stack(x):
    lo_lane = _iota(x.shape, 1) < HEAD_DIM
    return jnp.concatenate([jnp.where(lo_lane, x, 0.0), jnp.where(lo_lane, 0.0, x)], axis=0)


def _sb_unstack(x2):
    tq = x2.shape[0] // 2
    lo_lane = _iota((tq, LANE), 1) < HEAD_DIM
    return jnp.where(lo_lane, x2[:tq], x2[tq:])


def _sb_logits(q2, kj, row0, col0, masked):
    shape = (q2.shape[0], BLK)
    tq = shape[0] // 2
    z = dot_nt(q2, kj)
    t = jnp.log(1.0 + jnp.exp(-jnp.abs(z)))
    ls = jnp.minimum(z, 0.0) - t
    lk = jnp.minimum(-z, 0.0) - t
    if not masked:
        return None, ls, lk
    row = _iota(shape, 0)
    valid = (col0 + _iota(shape, 1)) < (row0 + jnp.where(row < tq, row, row - tq))
    return valid, ls, jnp.where(valid, lk, 0.0)


def _sb_where(valid, x):
    return x if valid is None else jnp.where(valid, x, 0.0)


def _sb_win(x2, lo):
    if lo == 0:
        return x2
    tq = x2.shape[0] // 2
    return jnp.concatenate([x2[lo:tq], x2[tq + lo:]], axis=0)


def _sb_unwin(x2, xw, lo):
    if lo == 0:
        return xw
    tq = x2.shape[0] // 2
    return jnp.concatenate([x2[:lo], xw[:tq - lo], x2[tq:tq + lo], xw[tq - lo:]], axis=0)


def _sb_add(ref, val, lo):
    if lo == 0:
        ref[...] += val
    else:
        tq = ref.shape[0] // 2
        ref[lo:tq, :] += val[:tq - lo]
        ref[tq + lo:, :] += val[tq - lo:]


def _sums(x, mask2, parts):
    acc = None
    rest = x
    for _ in range(parts):
        term = rest.astype(BF16)
        rest = rest - term.astype(F32)
        d = lax.dot_general(term, mask2, (((1,), (0,)), ((), ())), preferred_element_type=F32)
        acc = d if acc is None else acc + d
    return acc[:, :BLK], acc[:, BLK:]


def _mask2(cond):
    return jnp.concatenate([cond.astype(BF16), jnp.ones(cond.shape, BF16)], axis=1)


def _sb_specs(s):
    tq = _sb_tq(s)
    qspec = pl.BlockSpec((tq, LANE), lambda p, i: (i, p))
    kspec = pl.BlockSpec((s, LANE), lambda p, i: (0, SB_PAIRS + p))
    vspec = pl.BlockSpec((s, LANE), lambda p, i: (0, 2 * SB_PAIRS + p))
    return qspec, kspec, vspec


SB_FLOOR = -104.0


def sb_fwd(qkv, sides=()):
    s = qkv.shape[0]
    tq = _sb_tq(s)
    kpq = tq // BLK

    def body(q_ref, k_ref, v_ref, o_ref, t_ref, n_ref, acc_ref):
        qi = pl.program_id(1)
        q2 = _sb_stack(q_ref[...] * SB_SCALE).astype(BF16)
        later = _mask2(_iota((BLK, BLK), 0) > _iota((BLK, BLK), 1))
        acc_ref[...] = jnp.zeros_like(acc_ref)

        def step(j, r, masked, lo=0):
            rows = pl.ds(pl.multiple_of(j * BLK, BLK), BLK)
            rw = _sb_win(r, lo)
            valid, ls, lk = _sb_logits(_sb_win(q2, lo), k_ref[rows, :], qi * tq + lo, j * BLK, masked)
            after, total = _sums(lk, later, 2)
            w = _sb_where(valid, jnp.exp(ls + rw + after))
            _sb_add(acc_ref, dot_nn(w, v_ref[rows, :]), lo)
            return _sb_unwin(r, rw + total, lo)

        r = jnp.zeros((2 * tq, LANE), F32)
        for d in reversed(range(kpq)):
            r = step(kpq * qi + d, r, True, d * BLK)

        def tile(g, r):
            for d in reversed(range(kpq)):
                r = step(kpq * (qi - 1 - g) + d, r, False)
            return r

        n, r = lax.while_loop(lambda c: jnp.logical_and(c[0] < qi, jnp.max(c[1]) > SB_FLOOR),
                              lambda c: (c[0] + 1, tile(c[0], c[1])), (jnp.int32(0), r))
        o_ref[...] = _sb_unstack(acc_ref[...])
        t_ref[...] = jnp.concatenate([r[:tq], r[tq:]], axis=1)
        n_ref[...] = jnp.zeros(n_ref.shape, F32) + n.astype(F32)

    return hosted_call(
        body, sides, name="sb_fwd", grid=(SB_PAIRS, s // tq),
        in_specs=list(_sb_specs(s)),
        out_specs=[pl.BlockSpec((tq, LANE), lambda p, i: (i, p)), pl.BlockSpec((tq, 2 * LANE), lambda p, i: (i, p)),
                   pl.BlockSpec((None, None, 8, LANE), lambda p, i: (p, i, 0, 0))],
        out_shape=[jax.ShapeDtypeStruct((s, SB_WIDTH), F32), jax.ShapeDtypeStruct((s, 2 * SB_WIDTH), F32),
                   jax.ShapeDtypeStruct((SB_PAIRS, s // tq, 8, LANE), F32)],
        scratch_shapes=[pltpu.VMEM((2 * tq, LANE), F32)],
        compiler_params=_params(2), operands=(qkv, qkv, qkv))


def sb_bwd(qkv, tot, swept, do, do_col=0, sides=()):
    s = qkv.shape[0]
    tq = _sb_tq(s)
    kpq = tq // BLK

    def body(q_ref, k_ref, v_ref, t_ref, n_ref, do_ref, dq_ref, dk_ref, dv_ref, acc_ref):
        qi = pl.program_id(1)
        n = jnp.clip(jnp.max(n_ref[...]).astype(jnp.int32), 0, qi)
        q2 = _sb_stack(q_ref[...] * SB_SCALE).astype(BF16)
        do2 = _sb_stack(do_ref[...]).astype(BF16)
        tot2 = jnp.concatenate([t_ref[:, :LANE], t_ref[:, LANE:]], axis=0)
        sq = (BLK, BLK)
        later = _mask2(_iota(sq, 0) > _iota(sq, 1))
        before = _mask2(_iota(sq, 0) < _iota(sq, 1))
        acc_ref[...] = jnp.zeros_like(acc_ref)

        @pl.when(qi == 0)
        def _():
            dk_ref[...] = jnp.zeros_like(dk_ref)
            dv_ref[...] = jnp.zeros_like(dv_ref)

        def step(j, carry, masked, lo=0):
            rc, fc = carry
            rows = pl.ds(pl.multiple_of(j * BLK, BLK), BLK)
            kj = k_ref[rows, :]
            vj = v_ref[rows, :]
            qw, dow, fw = _sb_win(q2, lo), _sb_win(do2, lo), _sb_win(fc, lo)
            valid, ls, lk = _sb_logits(qw, kj, qi * tq + lo, j * BLK, masked)
            after, total = _sums(lk, later, 2)
            rw = _sb_win(rc, lo) - total
            w = _sb_where(valid, jnp.exp(ls + rw + after))
            e = w * dot_nt(dow, vj)
            f_in, f_tot = _sums(e, before, 2)
            sg = jnp.exp(ls)
            dz = _sb_where(valid, e * (1.0 - sg) - (fw + f_in) * sg)
            _sb_add(acc_ref, dot_nn(dz, kj), lo)
            dk_ref[rows, :] += dot_tn(dz, qw)
            dv_ref[rows, :] += dot_tn(w, dow)
            return _sb_unwin(rc, rw, lo), _sb_unwin(fc, fw + f_tot, lo)

        def tile(g, carry):
            for d in range(kpq):
                carry = step(kpq * g + d, carry, False)
            return carry

        carry = lax.fori_loop(qi - n, qi, tile, (tot2, jnp.zeros((2 * tq, LANE), F32)))
        for d in range(kpq):
            carry = step(kpq * qi + d, carry, True, d * BLK)
        dq_ref[...] = (SB_SCALE * _sb_unstack(acc_ref[...])).astype(dq_ref.dtype)

    qspec, kspec, vspec = _sb_specs(s)
    blk = pl.BlockSpec((tq, LANE), lambda p, i: (i, p))
    acc = pl.BlockSpec((s, LANE), lambda p, i: (0, p))
    return hosted_call(
        body, sides, name="sb_bwd", grid=(SB_PAIRS, s // tq),
        in_specs=[qspec, kspec, vspec, pl.BlockSpec((tq, 2 * LANE), lambda p, i: (i, p)),
                  pl.BlockSpec((None, None, 8, LANE), lambda p, i: (p, i, 0, 0)),
                  pl.BlockSpec((tq, LANE), lambda p, i: (i, do_col + p))],
        out_specs=[blk, acc, acc],
        out_shape=[jax.ShapeDtypeStruct((s, SB_WIDTH), BF16)] + [jax.ShapeDtypeStruct((s, SB_WIDTH), F32)] * 2,
        scratch_shapes=[pltpu.VMEM((2 * tq, LANE), F32)],
        compiler_params=_params(2), operands=(qkv, qkv, qkv, tot, swept, do))


POOL_GROUP_DIM = POOL_WIDTH // len(POOL_WINDOWS)


assert all(w == 2 ** (i + 1) for i, w in enumerate(POOL_WINDOWS))


def _pool_inv(c):
    group = _iota((BLK, POOL_WIDTH), 1) // POOL_GROUP_DIM
    pos = c * BLK + _iota((BLK, POOL_WIDTH), 0)
    win = jnp.zeros((BLK, POOL_WIDTH), jnp.int32)
    for gi, wn in enumerate(POOL_WINDOWS):
        win = jnp.where(group == gi, wn, win)
    return 1.0 / jnp.minimum(pos + 1, win).astype(F32)


def _window_sums(ext, trailing):
    group = _iota(ext.shape, 1) // POOL_GROUP_DIM
    acc = ext
    out = None
    for gi in range(len(POOL_WINDOWS)):
        shift = 2 ** gi
        acc = acc + pltpu.roll(acc, shift if trailing else ext.shape[0] - shift, 0)
        out = acc if out is None else jnp.where(group == gi, acc, out)
    return out


def _pool_pooled(ext, cur, inv):
    return _window_sums(ext, True)[BLK:] * inv - cur


def pool_fwd(p, wblk, pb, ps):
    s, n = p.shape

    def body(cur_ref, prev_ref, w_ref, pb_ref, ps_ref, o_ref):
        c = pl.program_id(0)
        cur = cur_ref[...]
        prev = jnp.where(c > 0, prev_ref[...], 0.0)
        pooled = _pool_pooled(jnp.concatenate([prev, cur], axis=0), cur, _pool_inv(c))
        o_ref[...] = (dot_nn(pooled, w_ref[...]) + pb_ref[...]) * ps_ref[...]

    return pl.pallas_call(
        body, name="pool_fwd", grid=(s // BLK,),
        in_specs=[pl.BlockSpec((BLK, n), lambda c: (c, 0)), pl.BlockSpec((BLK, n), lambda c: (jnp.maximum(c - 1, 0), 0)),
                  _full_spec((n, n)), _full_spec((1, n)), _full_spec((1, n))],
        out_specs=pl.BlockSpec((BLK, n), lambda c: (c, 0)), out_shape=jax.ShapeDtypeStruct((s, n), F32),
        compiler_params=_params(1),
    )(p, p, wblk, pb, ps)


def pool_bwd(p, wblk, pb, ps, dout, do_col=0, sides=()):
    s, n = p.shape
    nb = s // BLK

    def body(cur_ref, prev_ref, w_ref, pb_ref, ps_ref, do_ref, dp_ref, dw_ref, dpb_ref, dps_ref, carry_ref):
        i = pl.program_id(0)
        c = nb - 1 - i
        first = i == 0
        cur = cur_ref[...]
        prev = jnp.where(c > 0, prev_ref[...], 0.0)
        inv = _pool_inv(c)
        pooled = _pool_pooled(jnp.concatenate([prev, cur], axis=0), cur, inv)
        mixed = dot_nn(pooled, w_ref[...]) + pb_ref[...]
        dov = do_ref[...]
        dmixed = dov * ps_ref[...]
        _acc(dps_ref, first, jnp.sum(dov * mixed, axis=0, keepdims=True))
        _acc(dpb_ref, first, jnp.sum(dmixed, axis=0, keepdims=True))
        _acc(dw_ref, first, dot_tn(pooled, dmixed))
        dpooled = dot_nt(dmixed, w_ref[...])
        dext = _window_sums(jnp.concatenate([jnp.zeros((BLK, n), F32), dpooled * inv], axis=0), False)

        @pl.when(first)
        def _():
            carry_ref[...] = jnp.zeros_like(carry_ref)

        dp_ref[...] = (dext[BLK:] - dpooled + carry_ref[...]).astype(dp_ref.dtype)
        carry_ref[...] = dext[:BLK]

    rev = lambda i: (nb - 1 - i, 0)
    return hosted_call(
        body, sides, name="pool_bwd", grid=(nb,),
        in_specs=[pl.BlockSpec((BLK, n), rev), pl.BlockSpec((BLK, n), lambda i: (jnp.maximum(nb - 2 - i, 0), 0)),
                  _full_spec((n, n)), _full_spec((1, n)), _full_spec((1, n)),
                  pl.BlockSpec((BLK, n), lambda i: (nb - 1 - i, do_col))],
        out_specs=[pl.BlockSpec((BLK, n), rev), _full_spec((n, n)), _full_spec((1, n)), _full_spec((1, n))],
        out_shape=[jax.ShapeDtypeStruct((s, n), BF16), jax.ShapeDtypeStruct((n, n), F32),
                   jax.ShapeDtypeStruct((1, n), F32), jax.ShapeDtypeStruct((1, n), F32)],
        scratch_shapes=[pltpu.VMEM((BLK, n), F32)],
        compiler_params=_params(1), operands=(p, p, wblk, pb, ps, dout))


def _row_tile(rows):
    if rows <= 512:
        return rows
    for t in (512, 256, 128, 64, 32, 16, 8):
        if rows % t == 0:
            return t
    return rows


def adamw(w, g, m, v):
    (d, nm, nv), _ = adamw_many([w], [g], [m], [v])
    return d[0], nm[0], nv[0]


def adamw_many(ws, gs, ms, vs, sides=()):
    k = len(ws)
    n, rows, cols = ws[0].shape
    tr = _row_tile(rows)
    while 7 * k * 2 * tr * cols * 4 > VMEM_LIMIT // 2 and tr % 16 == 0:
        tr //= 2

    def body(*refs):
        for i in range(k):
            outs = _adamw_math(*[refs[j * k + i][...] for j in range(4)])
            for j in range(3):
                refs[(4 + j) * k + i][...] = outs[j]

    spec = pl.BlockSpec((1, tr, cols), lambda i, j: (i, j, 0))
    outs, side_outs = hosted_call(
        body, sides, name="adamw", grid=(n, rows // tr), in_specs=[spec] * (4 * k), out_specs=[spec] * (3 * k),
        out_shape=[jax.ShapeDtypeStruct(ws[0].shape, F32)] * (3 * k), scratch_shapes=[],
        compiler_params=_params(2), operands=(*ws, *gs, *ms, *vs))
    return (outs[:k], outs[k:2 * k], outs[2 * k:]), side_outs


def _adamw_math(w, g, m, v):
    nm = ADAM_B1 * m + (1.0 - ADAM_B1) * g
    nv = ADAM_B2 * v + (1.0 - ADAM_B2) * (g * g)
    m_hat = nm / (1.0 - ADAM_B1 ** ADAM_STEP)
    v_hat = nv / (1.0 - ADAM_B2 ** ADAM_STEP)
    return -ADAM_LR * (m_hat / (jnp.sqrt(v_hat) + ADAM_EPS) + ADAM_WD * w), nm, nv


def adamw_small(ws, gs, ms, vs):
    n = len(ws)

    def body(*refs):
        for i in range(n):
            outs = _adamw_math(*[refs[k * n + i][...] for k in range(4)])
            for k in range(3):
                refs[(4 + k) * n + i][...] = outs[k]

    vm = pl.BlockSpec(memory_space=pltpu.VMEM)
    outs = pl.pallas_call(
        body, name="adamw_small", in_specs=[vm] * (4 * n), out_specs=[vm] * (3 * n),
        out_shape=[jax.ShapeDtypeStruct(w.shape, F32) for w in ws] * 3,
    )(*ws, *gs, *ms, *vs)
    return outs[:n], outs[n:2 * n], outs[2 * n:]


def slab_sum(srcs, n_out, out_dtype, into=None, slot=0):
    _, rows, cols = srcs[0][0].shape
    tr = _row_tile(rows)
    n_src = len(srcs)
    sel = jnp.stack([jnp.asarray(base, jnp.int32) for _, base, _ in srcs])

    def body(sel_ref, *refs):
        acc = refs[0][...].astype(F32)
        for r in refs[1:n_src]:
            acc = acc + r[...].astype(F32)
        refs[-1][...] = acc.astype(out_dtype)

    def in_spec(k, step):
        return pl.BlockSpec((None, tr, cols), lambda o, i, sel_ref: (sel_ref[k] + step * o, i, 0))

    shape = (n_out, rows, cols) if into is None else into.shape
    return pl.pallas_call(
        body, name="slab_sum",
        grid_spec=pltpu.PrefetchScalarGridSpec(
            num_scalar_prefetch=1, grid=(n_out, rows // tr),
            in_specs=[in_spec(k, step) for k, (_, _, step) in enumerate(srcs)] + ([] if into is None else [ANY]),
            out_specs=pl.BlockSpec((None, tr, cols), lambda o, i, sel_ref: (slot + o, i, 0))),
        out_shape=jax.ShapeDtypeStruct(shape, out_dtype), compiler_params=_params(2),
        input_output_aliases={} if into is None else {1 + n_src: 0},
    )(sel, *[a for a, _, _ in srcs], *([] if into is None else [into]))


ICI_FLIPS = ((1, 0, 0), (0, 1, 0), (1, 1, 0))
D2D_FLIPS = ((0, 0, 1),)
ANY = pl.BlockSpec(memory_space=pl.ANY)


def _me():
    return lax.axis_index("x"), lax.axis_index("y"), lax.axis_index("c")


def _flipped(me, flip):
    return tuple(1 - m if f else m for m, f in zip(me, flip))


def _chip(dev):
    return 2 * dev[0] + dev[1]


def _dev(dev):
    return 4 * dev[0] + 2 * dev[1] + dev[2]


N_CHIP = 4
D2D = (0, 0, 1)


class Exchange:
    def __init__(self, xs, n_out, copies, own=None, in_place=False):
        self.xs, self.copies, self.own, self.in_place = list(xs), copies, own, in_place
        self.n_arr, self.n_cp = len(self.xs), len(copies)
        self.out_shape = [jax.ShapeDtypeStruct((n_out,) + x.shape[1:], x.dtype) for x in self.xs]
        self.scratch = [pltpu.SemaphoreType.DMA((self.n_arr * self.n_cp,)),
                        pltpu.SemaphoreType.DMA((self.n_arr * self.n_cp,)), pltpu.SemaphoreType.DMA((self.n_arr,))]

    def _own(self, x_refs, o_refs, sems, me):
        if self.own is None:
            return []
        return [pltpu.make_async_copy(x_refs[a].at[self.own[0](me)], o_refs[a].at[self.own[1](me)], sems[2].at[a])
                for a in range(self.n_arr)]

    def _copy(self, x_refs, o_refs, sems, me, a, j, sender):
        flip, src_slot, dst_slot = self.copies[j]
        k = a * self.n_cp + j
        return pltpu.make_async_remote_copy(
            src_ref=x_refs[a].at[src_slot(me)], dst_ref=o_refs[a].at[dst_slot(sender)],
            send_sem=sems[0].at[k], recv_sem=sems[1].at[k], device_id=_flipped(me, flip), device_id_type=MESH)

    def start(self, x_refs, o_refs, sems):
        me = _me()
        for cp in self._own(x_refs, o_refs, sems, me):
            cp.start()
        for j in range(self.n_cp):
            for a in range(self.n_arr):
                self._copy(x_refs, o_refs, sems, me, a, j, me).start()

    def wait(self, x_refs, o_refs, sems):
        me = _me()
        for j in range(self.n_cp):
            for a in range(self.n_arr):
                self._copy(x_refs, o_refs, sems, me, a, j, _flipped(me, self.copies[j][0])).wait_recv()
        for j in range(self.n_cp):
            for a in range(self.n_arr):
                self._copy(x_refs, o_refs, sems, me, a, j, me).wait_send()
        for cp in self._own(x_refs, o_refs, sems, me):
            cp.wait()

    def run(self, name):
        n = self.n_arr

        def body(*refs):
            self.start(refs[:n], refs[n:2 * n], refs[2 * n:])
            self.wait(refs[:n], refs[n:2 * n], refs[2 * n:])

        return pl.pallas_call(
            body, name=name, in_specs=[ANY] * n, out_specs=[ANY] * n, out_shape=self.out_shape,
            input_output_aliases={a: a for a in range(n)} if self.in_place else {}, scratch_shapes=self.scratch,
        )(*self.xs)


def hosted_call(body, sides, *, name, grid, in_specs, out_specs, out_shape, scratch_shapes, compiler_params, operands):
    n_in, n_out, n_scr = len(in_specs), len(out_specs), len(scratch_shapes)
    live = [s for s in sides if s is not None]
    if not live:
        outs = pl.pallas_call(body, name=name, grid=grid, in_specs=in_specs, out_specs=out_specs, out_shape=out_shape,
                              scratch_shapes=scratch_shapes, compiler_params=compiler_params)(*operands)
        return outs, [None] * len(sides)
    n = sum(s.n_arr for s in live)
    lo = [sum(s.n_arr for s in live[:i]) for i in range(len(live))]

    def full_body(*refs):
        ins, sx = refs[:n_in], refs[n_in:n_in + n]
        outs, so = refs[n_in + n:n_in + n + n_out], refs[n_in + n + n_out:n_in + 2 * n + n_out]
        scr, sems = refs[n_in + 2 * n + n_out:n_in + 2 * n + n_out + n_scr], refs[n_in + 2 * n + n_out + n_scr:]
        first = functools.reduce(jnp.logical_and, [pl.program_id(a) == 0 for a in range(len(grid))])
        last = functools.reduce(jnp.logical_and, [pl.program_id(a) == g - 1 for a, g in enumerate(grid)])
        parts = [(s, sx[l:l + s.n_arr], so[l:l + s.n_arr], sems[3 * i:3 * i + 3]) for i, (s, l) in enumerate(zip(live, lo))]

        @pl.when(first)
        def _():
            for s, x, o, m in parts:
                s.start(x, o, m)

        body(*ins, *outs, *scr)

        @pl.when(last)
        def _():
            for s, x, o, m in parts:
                s.wait(x, o, m)

    aliases = {n_in + l + a: n_out + l + a for s, l in zip(live, lo) if s.in_place for a in range(s.n_arr)}
    outs = pl.pallas_call(
        full_body, name=name + "_x", grid=grid, in_specs=list(in_specs) + [ANY] * n,
        out_specs=list(out_specs) + [ANY] * n, out_shape=list(out_shape) + [o for s in live for o in s.out_shape],
        input_output_aliases=aliases,
        scratch_shapes=list(scratch_shapes) + [m for s in live for m in s.scratch], compiler_params=compiler_params,
    )(*operands, *[x for s in live for x in s.xs])
    side_outs = iter([outs[n_out + l:n_out + l + s.n_arr] for s, l in zip(live, lo)])
    return outs[:n_out], [next(side_outs) if s is not None else None for s in sides]


def gather_ici(shards):
    ici = [(f, lambda me: 0, _dev) for f in ICI_FLIPS]
    return Exchange([s[None] for s in shards], N_DEV, ici, (lambda me: 0, _dev))


def gather_d2d(blocks):
    d2d = [(D2D, (lambda me, k=k: 2 * k + me[2]), (lambda sender, k=k: 2 * k + sender[2])) for k in range(N_CHIP)]
    return Exchange(blocks, N_DEV, d2d, None, in_place=True)


def gathered(blocks):
    return [b.reshape(-1, b.shape[2]) for b in blocks]


def scatter_d2d(parts):
    d2d = [(D2D, (lambda me, k=k: 2 * k + 1 - me[2]), (lambda sender, k=k: k)) for k in range(N_CHIP)]
    return Exchange(parts, N_CHIP, d2d)


def chip_sums(parts, sib):
    c = _me()[2]
    return [slab_sum([(p, c, 2), (s, 0, 1)], N_CHIP, BF16) for p, s in zip(parts, sib)]


def scatter_ici(sums):
    ici = [(f, (lambda me, f=f: _chip(_flipped(me, f))), (lambda sender, i=i: i)) for i, f in enumerate(ICI_FLIPS)]
    return Exchange(sums, len(ICI_FLIPS), ici)


def device_sums(sums, got, into=None, slot=0):
    x, y, _ = _me()
    outs = [slab_sum([(cs, 2 * x + y, 0)] + [(g, i, 0) for i in range(len(ICI_FLIPS))], 1, F32,
                     None if into is None else into[a], slot) for a, (cs, g) in enumerate(zip(sums, got))]
    return outs if into is not None else [o[0] for o in outs]


def all_gather(shards):
    blocks = gather_ici(shards).run("gather_ici")
    return gathered(gather_d2d(blocks).run("gather_d2d"))


def all_reduce_small(v):
    flips = D2D_FLIPS + ICI_FLIPS[:2]

    def body(v_ref, o_ref, got_ref, send_sems, recv_sems):
        me = _me()
        o_ref[...] = v_ref[...]
        for i, flip in enumerate(flips):
            cp = pltpu.make_async_remote_copy(
                src_ref=o_ref, dst_ref=got_ref.at[i], send_sem=send_sems.at[i], recv_sem=recv_sems.at[i],
                device_id=_flipped(me, flip), device_id_type=MESH)
            cp.start()
            cp.wait()
            o_ref[...] = o_ref[...] + got_ref[i]

    vm = pl.BlockSpec(memory_space=pltpu.VMEM)
    return pl.pallas_call(
        body, name="all_reduce_small", in_specs=[vm], out_specs=vm, out_shape=jax.ShapeDtypeStruct(v.shape, F32),
        scratch_shapes=[pltpu.VMEM((len(flips),) + v.shape, F32), pltpu.SemaphoreType.DMA((len(flips),)),
                        pltpu.SemaphoreType.DMA((len(flips),))],
    )(v)


def _perm_rows(wt):
    pad = jnp.zeros((D_IN_PAD - D_IN_PROJ, wt.shape[1]), wt.dtype)
    return jnp.concatenate([wt[:DT_LO], wt[DT_HI:], wt[DT_LO:DT_HI], pad], axis=0)


def _unperm_rows(dwt):
    n = D_IN_PROJ - (DT_HI - DT_LO)
    return jnp.concatenate([dwt[:DT_LO], dwt[n:D_IN_PROJ], dwt[DT_LO:n]], axis=0)


def _pad_lanes(v):
    return jnp.pad(v, ((0, 0), (0, LANE - v.shape[1])))[:, None]


def _block_diag(w):
    l, g, n, _ = w.shape
    out = jnp.zeros((l, g * n, g * n), w.dtype)
    for i in range(g):
        out = out.at[:, i * n:(i + 1) * n, i * n:(i + 1) * n].set(w[:, i])
    return out


def _pack(groups):
    flat = []
    for grp in groups:
        parts = [a.reshape(-1) for a in (grp if isinstance(grp, (list, tuple)) else [grp])]
        n = sum(p.shape[0] for p in parts)
        if -n % LANE:
            parts.append(jnp.zeros((-n % LANE,), parts[0].dtype))
        flat += parts
    return jnp.concatenate(flat).reshape(-1, LANE)


def _unpack(buf, shapes):
    out = []
    lo = 0
    buf = buf.reshape(-1)
    for shp in shapes:
        n = 1
        for k in shp:
            n *= k
        out.append(buf[lo:lo + n].reshape(shp))
        lo += n + (-n % LANE)
    return out


def small_params(w, conv_w_full):
    return dict(
        n1w=w["norm1_w"][:, None], cw=jnp.pad(conv_w_full, ((0, 0), (0, 8 - CONV_WIDTH), (0, 0))),
        cb=w["conv_b"][:, None], dtb=_pad_lanes(w["dt_bias"]), alog=_pad_lanes(w["a_log"]), dsk=_pad_lanes(w["d_skip"]),
        snw=w["ssd_norm_w"][:, None], wblk=_block_diag(w["pool_w"]), pb=w["pool_b"].reshape(-1, 1, POOL_WIDTH),
        ps=w["pool_scale"][:, None], n2w=w["norm2_w"][:, None])


MIX = ("w_in", "w_out")
FFN = ("w_gate", "w_up", "w_down")


def layer_params(small, l):
    return {k: v[l] for k, v in small.items()}


def mix_weights(whole):
    return _perm_rows(whole[0]), whole[1]


def _slabs(g):
    return g.reshape(N_DEV, -1, g.shape[-1])


def _layer_fwd(x, p, mix, ffn=None, ffn_shards=None, next_mix_shards=None):
    ici = [gather_ici([sh]) for sh in ffn_shards] if ffn_shards is not None else [None] * len(FFN)
    (z, xbc, qkv, pp, dtr, h1), (blk_g,) = inproj_fwd(x, p["n1w"], mix[0], [ici[0]])
    (u,), _ = conv_fwd(xbc, p["cw"], p["cb"])
    (y_ssd, st), (blk_u,) = ssd_fwd(u, z, dtr, p["dtb"], p["alog"], p["dsk"], p["snw"], [ici[1]])
    (o, tot, swept), (blk_d,) = sb_fwd(qkv, [ici[2]])
    blocks = blk_g + blk_u + blk_d if ffn_shards is not None else None
    yp = pool_fwd(pp, p["wblk"], p["pb"], p["ps"])
    (x_mid, ycat), (blocks,) = outproj_fwd([y_ssd, o, yp], mix[1], x,
                                           [gather_d2d(blocks) if blocks is not None else None])
    if blocks is not None:
        ffn = gathered(blocks)
    (x_out, g, uu), (nxt,) = ffn_fwd(x_mid, p["n2w"], *ffn,
                                     [gather_ici(next_mix_shards) if next_mix_shards is not None else None])
    sv = dict(x=x, z=z, xbc=xbc, qkv=qkv, pp=pp, dtr=dtr, h1=h1, u=u, st=st, tot=tot, swept=swept, ycat=ycat,
              x_mid=x_mid, g=g, uu=uu, w_in=mix[0], w_out=mix[1], wg=ffn[0], wu=ffn[1], wd=ffn[2])
    return x_out, sv, nxt


def _layer_bwd(dxo, sv, p, pending_mix=None, exchange=False, into_ffn=(), into_mix=()):
    (dx_mid, dn2w, a, dg, du, h2), (sib,) = ffn_bwd(
        dxo, sv["x_mid"], sv["g"], sv["uu"], p["n2w"], sv["wg"], sv["wu"], sv["wd"],
        [scatter_d2d(pending_mix) if pending_mix is not None else None])
    sums_mix = chip_sums(pending_mix, sib) if pending_mix is not None else None
    dwg, dwu = mm_tn([dg, du], h2)
    gr = dict(norm2_w=dn2w[0], w_gate=dwg, w_up=dwu, w_down=mm_tn(a, dxo))
    parts = [_slabs(gr[k]) for k in FFN] if exchange else None
    (dycat,), (sib,) = outproj_bwd(dx_mid, sv["w_out"], [scatter_d2d(parts) if exchange else None])
    sums_ffn = chip_sums(parts, sib) if exchange else None
    gr["w_out"] = mm_tn(sv["ycat"], dx_mid)
    (dp, dwblk, dpb, dps), _ = pool_bwd(sv["pp"], p["wblk"], p["pb"], p["ps"], dycat,
                                        (SSD_WIDTH + SB_WIDTH) // POOL_WIDTH)
    n = POOL_GROUP_DIM
    gr["pool_w"] = jnp.stack([dwblk[i * n:(i + 1) * n, i * n:(i + 1) * n] for i in range(len(POOL_WINDOWS))])
    gr["pool_b"] = dpb.reshape(len(POOL_WINDOWS), n)
    gr["pool_scale"] = dps[0]
    (dq, dk, dv), (got_ud,) = sb_bwd(sv["qkv"], sv["tot"], sv["swept"], dycat, SSD_WIDTH // LANE,
                                     [scatter_ici(sums_ffn[1:]) if exchange else None])
    (du_, dz, ddtr, ddtb, dalog, ddsk, dsnw), (got_mix,) = ssd_bwd(
        sv["u"], sv["z"], sv["dtr"], sv["st"], dycat, p["dtb"], p["alog"], p["dsk"], p["snw"],
        [scatter_ici(sums_mix) if sums_mix is not None else None])
    done_mix = device_sums(sums_mix, got_mix, *into_mix) if sums_mix is not None else None
    gr.update(dt_bias=ddtb[0, :SSD_HEADS], a_log=dalog[0, :SSD_HEADS], d_skip=ddsk[0, :SSD_HEADS], ssd_norm_w=dsnw[0])
    (dxbc, dcw, dcb), (got_g,) = conv_bwd(sv["xbc"], p["cw"], p["cb"], du_,
                                          [scatter_ici(sums_ffn[:1]) if exchange else None])
    done_ffn = device_sums(sums_ffn, got_g + got_ud, *into_ffn) if exchange else None
    gr.update(conv_w=dcw[:CONV_WIDTH], conv_b=dcb[0])
    dx, dn1w, dproj = inproj_bwd([dz, dxbc, dq, dk, dv, dp, ddtr], sv["w_in"], sv["x"], p["n1w"], dx_mid)
    gr.update(norm1_w=dn1w[0], w_in=_unperm_rows(mm_tn(dproj, sv["h1"])))
    return dx, gr, done_ffn, done_mix


def local_step(x, tgt, params, weights, final_w):
    saved = []
    for p, (mix, ffn) in zip(params, weights):
        x, sv, _ = _layer_fwd(x, p, mix, ffn)
        saved.append(sv)
    loss, dx, dfw = head_loss(x, final_w[None], tgt)
    grads = []
    for p, sv in zip(reversed(params), reversed(saved)):
        dx, gr, _, _ = _layer_bwd(dx, sv, p)
        grads.append(gr)
    grads.reverse()
    return loss, dx, dfw[0], grads


WEIGHTS = ("norm1_w", "w_in", "conv_w", "conv_b", "dt_bias", "a_log", "d_skip", "ssd_norm_w", "pool_w", "pool_b",
           "pool_scale", "w_out", "norm2_w", "w_gate", "w_up", "w_down", "final_norm_w")
COL_SHARDED = ("w_in", "w_gate", "w_up")
ROW_SHARDED = ("w_out", "w_down")
SMALL = tuple(k for k in WEIGHTS if k not in COL_SHARDED + ROW_SHARDED)


def kernel(x, norm1_w, w_in, conv_w, conv_b, dt_bias, a_log, d_skip, ssd_norm_w, pool_w, pool_b, pool_scale, w_out, norm2_w, w_gate, w_up, w_down, final_norm_w, loss_target, m_norm1_w, m_w_in, m_conv_w, m_conv_b, m_dt_bias, m_a_log, m_d_skip, m_ssd_norm_w, m_pool_w, m_pool_b, m_pool_scale, m_w_out, m_norm2_w, m_w_gate, m_w_up, m_w_down, m_final_norm_w, v_norm1_w, v_w_in, v_conv_w, v_conv_b, v_dt_bias, v_a_log, v_d_skip, v_ssd_norm_w, v_pool_w, v_pool_b, v_pool_scale, v_w_out, v_norm2_w, v_w_gate, v_w_up, v_w_down, v_final_norm_w):
    args = dict(locals())
    w = {k: args[k] for k in WEIGHTS}
    m = {k: args["m_" + k] for k in WEIGHTS}
    v = {k: args["v_" + k] for k in WEIGHTS}
    depth = w_in.shape[0]
    dev = _dev(_me())
    n_cw = conv_w.shape[-1]

    shards = {k: (jnp.swapaxes(w[k], 1, 2) if k in COL_SHARDED else w[k]).astype(BF16) for k in MIX + FFN}
    whole = all_gather([jnp.swapaxes(conv_w, 0, 2).reshape(n_cw, -1)] + [shards[k][0] for k in MIX])
    conv_w_full = jnp.swapaxes(whole[0].reshape(N_DEV * n_cw, CONV_WIDTH, depth), 0, 2)
    small = small_params(w, conv_w_full)
    xs = x[0]
    params, saved = [layer_params(small, l) for l in range(depth)], []
    mix = mix_weights(whole[1:])
    for l in range(depth):
        xs, sv, nxt = _layer_fwd(xs, params[l], mix, ffn_shards=[shards[k][l] for k in FFN],
                                 next_mix_shards=[shards[k][l + 1] for k in MIX] if l + 1 < depth else None)
        saved.append(sv)
        if nxt is not None:
            mix = mix_weights(gathered(gather_d2d(nxt).run("gather_d2d")))
    loss, dx, dfw = head_loss(xs, final_norm_w[None], loss_target[0])
    layer_grads = [None] * depth
    native = {k: lax.empty((depth,) + shards[k].shape[1:], F32) for k in MIX + FFN}
    pending = None
    for l in reversed(range(depth)):
        dx, layer_grads[l], done_ffn, done_mix = _layer_bwd(
            dx, saved[l], params[l], pending, exchange=True, into_ffn=([native[k] for k in FFN], l),
            into_mix=([native[k] for k in MIX], l + 1))
        native.update(zip(FFN, done_ffn))
        if pending is not None:
            native.update(zip(MIX, done_mix))
        pending = [_slabs(layer_grads[l][k]) for k in MIX]

    delta, new_m, new_v = {}, {}, {}
    sums = chip_sums(pending, scatter_d2d(pending).run("scatter_d2d"))
    as_native = lambda t, k: jnp.swapaxes(t[k], 1, 2) if k in COL_SHARDED else t[k]
    outs, (got,) = adamw_many([as_native(w, k) for k in FFN], [native[k] for k in FFN],
                              [as_native(m, k) for k in FFN], [as_native(v, k) for k in FFN], [scatter_ici(sums)])
    for dst, arrs in zip((delta, new_m, new_v), outs):
        dst.update({k: as_native({k: a}, k) for k, a in zip(FFN, arrs)})
    native.update(zip(MIX, device_sums(sums, got, [native[k] for k in MIX], 0)))
    grads = {k: jnp.swapaxes(native[k], 1, 2) if k in COL_SHARDED else native[k] for k in MIX + FFN}
    layered = [k for k in SMALL if k != "final_norm_w"]
    small_shapes = [(1, LANE)] + [(depth,) + layer_grads[0][k].shape for k in layered] + [dfw[0].shape]
    packed = _pack([loss] + [[layer_grads[l][k] for l in range(depth)] for k in layered] + [dfw[0]])
    summed = _unpack(all_reduce_small(packed), small_shapes)
    loss = summed[0][0, 0]
    grads.update(zip(layered + ["final_norm_w"], summed[1:]))
    grads["conv_w"] = lax.dynamic_slice_in_dim(grads["conv_w"], dev * n_cw, n_cw, axis=2)

    for k in MIX:
        delta[k], new_m[k], new_v[k] = adamw(w[k], grads[k], m[k], v[k])
    two_d = lambda a: a.reshape(1, -1) if a.ndim == 1 else a
    outs = adamw_small(*[[two_d(t[k]) for k in SMALL] for t in (w, grads, m, v)])
    for dst, arrs in zip((delta, new_m, new_v), outs):
        dst.update({k: a.reshape(w[k].shape) for k, a in zip(SMALL, arrs)})
    return (loss, dx[None], *[grads[k] for k in WEIGHTS], *[delta[k] for k in WEIGHTS],
            *[new_m[k] for k in WEIGHTS], *[new_v[k] for k in WEIGHTS])
```

```python
import functools

import jax
import jax.numpy as jnp
from jax import lax
from jax.experimental import pallas as pl
from jax.experimental.pallas import tpu as pltpu

F32 = jnp.float32
BF16 = jnp.bfloat16
HIGHEST = lax.Precision.HIGHEST
MESH = pl.DeviceIdType.MESH

EPS = 1e-6
D_MODEL = 1024
SSD_WIDTH = 512
SSD_HEADS = 8
HEAD_DIM = 64
D_STATE = 128
CONV_WIDTH = 4
CONV_DIM = 1024
SB_WIDTH = 256
POOL_WIDTH = 256
POOL_WINDOWS = (2, 4, 8, 16)
D_IN_PROJ = 2568
D_FF = 2816
N_DEV = 8
SEG = (512, 1024, 768, 256, 128)
D_IN_PAD = sum(SEG)
DT_LO, DT_HI = 1536, 1544

LANE = 128
BLK = 128
ROW_TILE = 256
VMEM_LIMIT = 56 * 2**20

ADAM_LR, ADAM_B1, ADAM_B2, ADAM_EPS, ADAM_WD, ADAM_STEP = 0.001, 0.9, 0.999, 1e-08, 0.01, 10


def _params(n_axes=1, vmem=None):
    return pltpu.CompilerParams(dimension_semantics=("arbitrary",) * n_axes, vmem_limit_bytes=vmem)


def _dot(a, b, dims, exact=False):
    if exact:
        return lax.dot_general(a.astype(F32), b.astype(F32), (dims, ((), ())), precision=HIGHEST,
                               preferred_element_type=F32)
    return lax.dot_general(a.astype(BF16), b.astype(BF16), (dims, ((), ())), preferred_element_type=F32)


def dot_nn(a, b, exact=False):
    return _dot(a, b, ((1,), (0,)), exact)


def dot_nt(a, b, exact=False):
    return _dot(a, b, ((1,), (1,)), exact)


def dot_tn(a, b, exact=False):
    return _dot(a, b, ((0,), (0,)), exact)


def _iota(shape, axis):
    return lax.broadcasted_iota(jnp.int32, shape, axis)


def _lane_col(x, h):
    return jnp.sum(jnp.where(_iota(x.shape, 1) == h, x, 0.0), axis=1, keepdims=True)


def _sub_row(x, h):
    return jnp.sum(jnp.where(_iota(x.shape, 0) == h, x, 0.0), axis=0, keepdims=True)


def _sigmoid(x):
    return 1.0 / (1.0 + jnp.exp(-x))


def _rms_fwd(x, w):
    r = lax.rsqrt(jnp.mean(x * x, axis=-1, keepdims=True) + EPS)
    return x * r * w


def _rms_bwd(x, w, dy):
    r = lax.rsqrt(jnp.mean(x * x, axis=-1, keepdims=True) + EPS)
    xh = x * r
    dxh = dy * w
    dx = r * (dxh - xh * jnp.mean(dxh * xh, axis=-1, keepdims=True))
    return dx, jnp.sum(dy * xh, axis=0, keepdims=True)


def _acc(ref, first, val):
    @pl.when(first)
    def _():
        ref[...] = val

    @pl.when(jnp.logical_not(first))
    def _():
        ref[...] += val


def _row_spec(tm, n):
    return pl.BlockSpec((tm, n), lambda i: (i, 0))


def _full_spec(shape):
    return pl.BlockSpec(shape, lambda *_: (0,) * len(shape))


def inproj_fwd(x, nw, w, sides=()):
    s, d = x.shape
    tm = min(ROW_TILE, s)
    seg_dtypes = (F32, F32, BF16, F32, F32)

    def body(x_ref, nw_ref, w_ref, z_ref, xbc_ref, qkv_ref, p_ref, dt_ref, h_ref):
        h = _rms_fwd(x_ref[...], nw_ref[...]).astype(BF16)
        h_ref[...] = h
        lo = 0
        for ref, n in zip((z_ref, xbc_ref, qkv_ref, p_ref, dt_ref), SEG):
            ref[...] = dot_nt(h, w_ref[lo:lo + n, :]).astype(ref.dtype)
            lo += n

    return hosted_call(
        body, sides, name="inproj_fwd", grid=(s // tm,),
        in_specs=[_row_spec(tm, d), _full_spec((1, d)), _full_spec(w.shape)],
        out_specs=[_row_spec(tm, n) for n in SEG] + [_row_spec(tm, d)],
        out_shape=[jax.ShapeDtypeStruct((s, n), t) for n, t in zip(SEG, seg_dtypes)]
        + [jax.ShapeDtypeStruct((s, d), BF16)],
        scratch_shapes=[], compiler_params=_params(1, VMEM_LIMIT), operands=(x, nw, w))


def inproj_bwd(pieces, w, x, nw, dres):
    s, d = x.shape
    tm = min(ROW_TILE, s)
    n_p = len(pieces)
    widths = [p.shape[1] for p in pieces]

    def body(*refs):
        w_ref, x_ref, nw_ref, dres_ref, dx_ref, dnw_ref, dp_ref = refs[n_p:]
        dh = None
        lo = 0
        for ref, n in zip(refs[:n_p], widths):
            piece = ref[...].astype(BF16)
            dp_ref[:, lo:lo + n] = piece
            part = dot_nn(piece, w_ref[lo:lo + n, :])
            dh = part if dh is None else dh + part
            lo += n
        dx, dnw = _rms_bwd(x_ref[...], nw_ref[...], dh)
        dx_ref[...] = dres_ref[...] + dx
        _acc(dnw_ref, pl.program_id(0) == 0, dnw)

    return pl.pallas_call(
        body, name="inproj_bwd", grid=(s // tm,),
        in_specs=[_row_spec(tm, n) for n in widths] + [_full_spec(w.shape), _row_spec(tm, d), _full_spec((1, d)),
                                                       _row_spec(tm, d)],
        out_specs=[_row_spec(tm, d), _full_spec((1, d)), _row_spec(tm, sum(widths))],
        out_shape=[jax.ShapeDtypeStruct((s, d), F32), jax.ShapeDtypeStruct((1, d), F32),
                   jax.ShapeDtypeStruct((s, sum(widths)), BF16)],
        compiler_params=_params(1, VMEM_LIMIT),
    )(*pieces, w, x, nw, dres)


def outproj_fwd(pieces, w, res, sides=()):
    s, d = res.shape
    tm = min(ROW_TILE, s)
    n_p = len(pieces)
    widths = [p.shape[1] for p in pieces]

    def body(*refs):
        w_ref, r_ref, o_ref, y_ref = refs[n_p:]
        acc = r_ref[...]
        lo = 0
        for ref, n in zip(refs[:n_p], widths):
            piece = ref[...].astype(BF16)
            y_ref[:, lo:lo + n] = piece
            acc = acc + dot_nn(piece, w_ref[lo:lo + n, :])
            lo += n
        o_ref[...] = acc

    return hosted_call(
        body, sides, name="outproj_fwd", grid=(s // tm,),
        in_specs=[_row_spec(tm, n) for n in widths] + [_full_spec(w.shape), _row_spec(tm, d)],
        out_specs=[_row_spec(tm, d), _row_spec(tm, sum(widths))],
        out_shape=[jax.ShapeDtypeStruct((s, d), F32), jax.ShapeDtypeStruct((s, sum(widths)), BF16)],
        scratch_shapes=[], compiler_params=_params(1, VMEM_LIMIT), operands=(*pieces, w, res))


def outproj_bwd(dx, w, sides=()):
    s, d = dx.shape
    tm = min(ROW_TILE, s)

    def body(dx_ref, w_ref, o_ref):
        o_ref[...] = dot_nt(dx_ref[...], w_ref[...])

    return hosted_call(
        body, sides, name="outproj_bwd", grid=(s // tm,),
        in_specs=[_row_spec(tm, d), _full_spec(w.shape)],
        out_specs=[_row_spec(tm, w.shape[0])], out_shape=[jax.ShapeDtypeStruct((s, w.shape[0]), F32)],
        scratch_shapes=[], compiler_params=_params(1, VMEM_LIMIT), operands=(dx, w))


def ffn_fwd(x, nw, wg, wu, wd, sides=()):
    s, d = x.shape
    f = wg.shape[0]
    tm = min(ROW_TILE, s)

    def body(x_ref, nw_ref, wg_ref, wu_ref, wd_ref, o_ref, g_ref, u_ref):
        xv = x_ref[...]
        h = _rms_fwd(xv, nw_ref[...]).astype(BF16)
        g = dot_nt(h, wg_ref[...])
        u = dot_nt(h, wu_ref[...])
        g_ref[...] = g.astype(BF16)
        u_ref[...] = u.astype(BF16)
        o_ref[...] = xv + dot_nn(g * _sigmoid(g) * u, wd_ref[...])

    return hosted_call(
        body, sides, name="ffn_fwd", grid=(s // tm,),
        in_specs=[_row_spec(tm, d), _full_spec((1, d)), _full_spec(wg.shape), _full_spec(wu.shape),
                  _full_spec(wd.shape)],
        out_specs=[_row_spec(tm, d), _row_spec(tm, f), _row_spec(tm, f)],
        out_shape=[jax.ShapeDtypeStruct((s, d), F32), jax.ShapeDtypeStruct((s, f), BF16),
                   jax.ShapeDtypeStruct((s, f), BF16)],
        scratch_shapes=[], compiler_params=_params(1, VMEM_LIMIT), operands=(x, nw, wg, wu, wd))


def ffn_bwd(dxo, x, g, u, nw, wg, wu, wd, sides=()):
    s, d = x.shape
    f = wg.shape[0]
    tm = min(ROW_TILE, s)

    def body(dxo_ref, x_ref, g_ref, u_ref, nw_ref, wg_ref, wu_ref, wd_ref, dx_ref, dnw_ref, a_ref, dg_ref,
             du_ref, h_ref):
        dxo_v = dxo_ref[...]
        xv = x_ref[...]
        da = dot_nt(dxo_v, wd_ref[...])
        gv = g_ref[...].astype(F32)
        uv = u_ref[...].astype(F32)
        sg = _sigmoid(gv)
        sl = gv * sg
        a_ref[...] = (sl * uv).astype(BF16)
        dg = (da * uv * (sg * (1.0 + gv * (1.0 - sg)))).astype(BF16)
        du = (da * sl).astype(BF16)
        dg_ref[...] = dg
        du_ref[...] = du
        dh = dot_nn(dg, wg_ref[...]) + dot_nn(du, wu_ref[...])
        h_ref[...] = _rms_fwd(xv, nw_ref[...]).astype(BF16)
        dx, dnw = _rms_bwd(xv, nw_ref[...], dh)
        dx_ref[...] = dxo_v + dx
        _acc(dnw_ref, pl.program_id(0) == 0, dnw)

    return hosted_call(
        body, sides, name="ffn_bwd", grid=(s // tm,),
        in_specs=[_row_spec(tm, d), _row_spec(tm, d), _row_spec(tm, f), _row_spec(tm, f), _full_spec((1, d)),
                  _full_spec(wg.shape), _full_spec(wu.shape), _full_spec(wd.shape)],
        out_specs=[_row_spec(tm, d), _full_spec((1, d)), _row_spec(tm, f), _row_spec(tm, f), _row_spec(tm, f),
                   _row_spec(tm, d)],
        out_shape=[jax.ShapeDtypeStruct((s, d), F32), jax.ShapeDtypeStruct((1, d), F32),
                   jax.ShapeDtypeStruct((s, f), BF16), jax.ShapeDtypeStruct((s, f), BF16),
                   jax.ShapeDtypeStruct((s, f), BF16), jax.ShapeDtypeStruct((s, d), BF16)],
        scratch_shapes=[], compiler_params=_params(1, VMEM_LIMIT), operands=(dxo, x, g, u, nw, wg, wu, wd))


def _tile(n, cap=512):
    best = LANE
    for t in range(LANE, cap + 1, LANE):
        if n % t == 0:
            best = t
    return best


def mm_tn(a, b):
    many = isinstance(a, (list, tuple))
    a_list = list(a) if many else [a]
    n_a = len(a_list)
    s, k = a_list[0].shape
    n = b.shape[1]
    tk = _tile(k)

    def body(*refs):
        b_val = refs[n_a][...]
        for a_ref, o_ref in zip(refs[:n_a], refs[n_a + 1:]):
            o_ref[...] = dot_nn(a_ref[...].astype(BF16).T, b_val).astype(BF16)

    outs = pl.pallas_call(
        body, name="mm_tn", grid=(k // tk,),
        in_specs=[pl.BlockSpec((s, tk), lambda i: (0, i))] * n_a + [_full_spec((s, n))],
        out_specs=[pl.BlockSpec((tk, n), lambda i: (i, 0))] * n_a,
        out_shape=[jax.ShapeDtypeStruct((k, n), BF16)] * n_a, compiler_params=_params(1, VMEM_LIMIT),
    )(*a_list, b)
    return outs if many else outs[0]


def head_loss(x, fw, tgt):
    s, d = x.shape
    tm = min(ROW_TILE, s)

    def body(x_ref, fw_ref, t_ref, loss_ref, dx_ref, dfw_ref):
        xv = x_ref[...]
        err = _rms_fwd(xv, fw_ref[...]) - t_ref[...]
        part = jnp.zeros((1, LANE), F32) + 0.5 * jnp.sum(err * err) / d
        dx, dfw = _rms_bwd(xv, fw_ref[...], err / d)
        dx_ref[...] = dx
        first = pl.program_id(0) == 0
        _acc(loss_ref, first, part)
        _acc(dfw_ref, first, dfw)

    return pl.pallas_call(
        body, name="head_loss", grid=(s // tm,),
        in_specs=[_row_spec(tm, d), _full_spec((1, d)), _row_spec(tm, d)],
        out_specs=[_full_spec((1, LANE)), _row_spec(tm, d), _full_spec((1, d))],
        out_shape=[jax.ShapeDtypeStruct((1, LANE), F32), jax.ShapeDtypeStruct((s, d), F32),
                   jax.ShapeDtypeStruct((1, d), F32)],
        compiler_params=_params(1),
    )(x, fw, tgt)


HALO = 8


def _conv_pre(ext, cw_ref, cb_ref):
    shifted = [pltpu.roll(ext, CONV_WIDTH - 1 - i, 0)[HALO:] if i < CONV_WIDTH - 1 else ext[HALO:]
               for i in range(CONV_WIDTH)]
    acc = cb_ref[...] + sum(cw_ref[i:i + 1, :] * shifted[i] for i in range(CONV_WIDTH))
    return acc, shifted


def _halo_spec(n, block_of_step):
    return pl.BlockSpec((HALO, n), lambda i: (jnp.maximum(block_of_step(i) * (BLK // HALO) - 1, 0), 0))


def conv_fwd(xbc, cw, cb, sides=()):
    s, n = xbc.shape

    def body(cur_ref, prev_ref, cw_ref, cb_ref, o_ref):
        prev = jnp.where(pl.program_id(0) > 0, prev_ref[...], 0.0)
        acc, _ = _conv_pre(jnp.concatenate([prev, cur_ref[...]], axis=0), cw_ref, cb_ref)
        o_ref[...] = acc * _sigmoid(acc)

    return hosted_call(
        body, sides, name="conv_fwd", grid=(s // BLK,),
        in_specs=[pl.BlockSpec((BLK, n), lambda c: (c, 0)), _halo_spec(n, lambda c: c),
                  _full_spec(cw.shape), _full_spec((1, n))],
        out_specs=[pl.BlockSpec((BLK, n), lambda c: (c, 0))], out_shape=[jax.ShapeDtypeStruct((s, n), F32)],
        scratch_shapes=[], compiler_params=_params(1), operands=(xbc, xbc, cw, cb))


def conv_bwd(xbc, cw, cb, du, sides=()):
    s, n = xbc.shape
    nb = s // BLK

    def body(cur_ref, prev_ref, cw_ref, cb_ref, du_ref, dx_ref, dcw_ref, dcb_ref, nxt_ref):
        i = pl.program_id(0)
        c = nb - 1 - i
        prev = jnp.where(c > 0, prev_ref[...], 0.0)
        acc, shifted = _conv_pre(jnp.concatenate([prev, cur_ref[...]], axis=0), cw_ref, cb_ref)
        sg = _sigmoid(acc)
        dacc = du_ref[...] * (sg * (1.0 + acc * (1.0 - sg)))

        @pl.when(i == 0)
        def _():
            nxt_ref[...] = jnp.zeros_like(nxt_ref)
            dcw_ref[...] = jnp.zeros_like(dcw_ref)
            dcb_ref[...] = jnp.zeros_like(dcb_ref)

        dcb_ref[...] += jnp.sum(dacc, axis=0, keepdims=True)
        for t in range(CONV_WIDTH):
            dcw_ref[t:t + 1, :] += jnp.sum(dacc * shifted[t], axis=0, keepdims=True)
        ext = jnp.concatenate([dacc, nxt_ref[...]], axis=0)
        dx = cw_ref[CONV_WIDTH - 1:CONV_WIDTH, :] * dacc
        for t in range(CONV_WIDTH - 1):
            dx += cw_ref[t:t + 1, :] * pltpu.roll(ext, BLK + HALO - (CONV_WIDTH - 1 - t), 0)[:BLK]
        dx_ref[...] = dx.astype(dx_ref.dtype)
        nxt_ref[...] = dacc[:HALO]

    rev = lambda i: (nb - 1 - i, 0)
    return hosted_call(
        body, sides, name="conv_bwd", grid=(nb,),
        in_specs=[pl.BlockSpec((BLK, n), rev), _halo_spec(n, lambda i: nb - 1 - i),
                  _full_spec(cw.shape), _full_spec((1, n)), pl.BlockSpec((BLK, n), rev)],
        out_specs=[pl.BlockSpec((BLK, n), rev), _full_spec((8, n)), _full_spec((1, n))],
        out_shape=[jax.ShapeDtypeStruct((s, n), BF16), jax.ShapeDtypeStruct((8, n), F32),
                   jax.ShapeDtypeStruct((1, n), F32)],
        scratch_shapes=[pltpu.VMEM((HALO, n), F32)],
        compiler_params=_params(1), operands=(xbc, xbc, cw, cb, du))


N_PAIR = SSD_HEADS // 2
B_LO = SSD_WIDTH
C_LO = SSD_WIDTH + 2 * D_STATE


def _softplus(x):
    return jnp.maximum(x, 0.0) + jnp.log(1.0 + jnp.exp(-jnp.abs(x)))


def _ssd_chunk(u_ref, dt_ref, dtb_ref, alog_ref):
    shape = (BLK, BLK)
    tri = _iota(shape, 1) <= _iota(shape, 0)
    pre = dt_ref[...] + dtb_ref[...]
    dt = _softplus(pre)
    a = -jnp.exp(alog_ref[...])
    acum = dot_nn(tri.astype(F32), dt * a, exact=True)
    acum_t = acum.T
    last = _sub_row(acum, BLK - 1)
    heads = []
    for h in range(SSD_HEADS):
        col = _lane_col(acum, h)
        seg = jnp.where(tri, col - _sub_row(acum_t, h), -1e30)
        heads.append(dict(col=col, dm=jnp.exp(seg), dt=_lane_col(dt, h), last=_lane_col(last, h)))
    return tri, pre, dt, a, heads


def _pair_mix(lo_mask, v0, v1):
    return jnp.where(lo_mask, v0, v1)


def ssd_fwd(u, z, dtr, dtb, alog, dsk, nw, sides=()):
    s = u.shape[0]
    nc = s // BLK

    def body(u_ref, z_ref, dt_ref, dtb_ref, alog_ref, dsk_ref, nw_ref, y_ref, st_ref, s_ref):
        @pl.when(pl.program_id(0) == 0)
        def _():
            s_ref[...] = jnp.zeros_like(s_ref)

        _, _, _, _, heads = _ssd_chunk(u_ref, dt_ref, dtb_ref, alog_ref)
        lo_lane = _iota((BLK, LANE), 1) < HEAD_DIM
        lo_sub = _iota((BLK, LANE), 0) < HEAD_DIM
        ys = []
        for p in range(N_PAIR):
            g = p // 2
            h0, h1 = heads[2 * p], heads[2 * p + 1]
            bg = u_ref[:, B_LO + g * D_STATE:B_LO + (g + 1) * D_STATE]
            cg = u_ref[:, C_LO + g * D_STATE:C_LO + (g + 1) * D_STATE]
            xs = u_ref[:, p * LANE:(p + 1) * LANE]
            xp = xs * _pair_mix(lo_lane, h0["dt"], h1["dt"])
            gm = dot_nt(cg, bg)
            yd = _pair_mix(lo_lane, dot_nn(gm * h0["dm"], xp), dot_nn(gm * h1["dm"], xp))
            sp = s_ref[p]
            st_ref[0, p] = sp
            yo = _pair_mix(lo_lane, jnp.exp(h0["col"]), jnp.exp(h1["col"])) * dot_nt(cg, sp)
            dskp = _pair_mix(lo_lane, _lane_col(dsk_ref[...], 2 * p), _lane_col(dsk_ref[...], 2 * p + 1))
            ys.append(yd + yo + xs * dskp)
            wp = _pair_mix(lo_lane, jnp.exp(h0["last"] - h0["col"]), jnp.exp(h1["last"] - h1["col"]))
            el = _pair_mix(lo_sub, jnp.exp(h0["last"]), jnp.exp(h1["last"]))
            s_ref[p] = el * sp + dot_tn(wp * xp, bg)
        y = jnp.concatenate(ys, axis=1)
        zv = z_ref[...]
        y_ref[...] = _rms_fwd(y * zv * _sigmoid(zv), nw_ref[...])

    vec = _full_spec((1, LANE))
    return hosted_call(
        body, sides, name="ssd_fwd", grid=(nc,),
        in_specs=[_row_spec(BLK, CONV_DIM), _row_spec(BLK, SSD_WIDTH), _row_spec(BLK, LANE), vec, vec, vec,
                  _full_spec((1, SSD_WIDTH))],
        out_specs=[_row_spec(BLK, SSD_WIDTH), pl.BlockSpec((1, N_PAIR, LANE, D_STATE), lambda c: (c, 0, 0, 0))],
        out_shape=[jax.ShapeDtypeStruct((s, SSD_WIDTH), F32), jax.ShapeDtypeStruct((nc, N_PAIR, LANE, D_STATE), F32)],
        scratch_shapes=[pltpu.VMEM((N_PAIR, LANE, D_STATE), F32)],
        compiler_params=_params(1), operands=(u, z, dtr, dtb, alog, dsk, nw))


def ssd_bwd(u, z, dtr, st, dyo, dtb, alog, dsk, nw, sides=()):
    s = u.shape[0]
    nc = s // BLK

    def body(u_ref, z_ref, dt_ref, st_ref, dyo_ref, dtb_ref, alog_ref, dsk_ref, nw_ref,
             du_ref, dz_ref, ddt_ref, ddtb_ref, dalog_ref, ddsk_ref, dnw_ref, ds_ref):
        first = pl.program_id(0) == 0

        @pl.when(first)
        def _():
            ds_ref[...] = jnp.zeros_like(ds_ref)

        tri, pre, dt, a, heads = _ssd_chunk(u_ref, dt_ref, dtb_ref, alog_ref)
        shape = (BLK, LANE)
        lane = _iota(shape, 1)
        lo_lane = lane < HEAD_DIM
        lo_sub = _iota(shape, 0) < HEAD_DIM
        pairs = []
        ys = []
        for p in range(N_PAIR):
            g = p // 2
            h0, h1 = heads[2 * p], heads[2 * p + 1]
            bg = u_ref[:, B_LO + g * D_STATE:B_LO + (g + 1) * D_STATE]
            cg = u_ref[:, C_LO + g * D_STATE:C_LO + (g + 1) * D_STATE]
            xs = u_ref[:, p * LANE:(p + 1) * LANE]
            dtp = _pair_mix(lo_lane, h0["dt"], h1["dt"])
            xp = xs * dtp
            gm = dot_nt(cg, bg)
            m0, m1 = gm * h0["dm"], gm * h1["dm"]
            sp = st_ref[0, p]
            eap = _pair_mix(lo_lane, jnp.exp(h0["col"]), jnp.exp(h1["col"]))
            yo = eap * dot_nt(cg, sp)
            dskp = _pair_mix(lo_lane, _lane_col(dsk_ref[...], 2 * p), _lane_col(dsk_ref[...], 2 * p + 1))
            ys.append(_pair_mix(lo_lane, dot_nn(m0, xp), dot_nn(m1, xp)) + yo + xs * dskp)
            pairs.append(dict(bg=bg, cg=cg, xs=xs, dtp=dtp, xp=xp, gm=gm, m=(m0, m1), sp=sp, eap=eap, yo=yo, dskp=dskp))
        y = jnp.concatenate(ys, axis=1)
        zv = z_ref[...]
        sz = _sigmoid(zv)
        gate = zv * sz
        dyg, dnw = _rms_bwd(y * gate, nw_ref[...], dyo_ref[...])
        _acc(dnw_ref, first, dnw)
        dy = dyg * gate
        dz_ref[...] = (dyg * y * (sz * (1.0 + zv * (1.0 - sz)))).astype(dz_ref.dtype)

        zeros = jnp.zeros(shape, F32)
        dacum_col = zeros
        dacum_row = zeros
        ddt = zeros
        ddsk = jnp.zeros((1, LANE), F32)
        dlast = jnp.zeros((1, LANE), F32)
        head_row = _iota((1, LANE), 1)
        sub = _iota(shape, 0)
        db = [zeros, zeros]
        dc = [zeros, zeros]
        for p in range(N_PAIR):
            g = p // 2
            q = pairs[p]
            dyp = dy[:, p * LANE:(p + 1) * LANE]
            dsn = ds_ref[p]
            t = dyp * q["xs"]
            dxs = dyp * q["dskp"]
            dcs = dyp * q["eap"]
            dc[g] = dc[g] + dot_nn(dcs, q["sp"])
            dsp = dot_tn(dcs, q["cg"])
            dea = dyp * q["yo"]
            elp = _pair_mix(lo_sub, jnp.exp(heads[2 * p]["last"]), jnp.exp(heads[2 * p + 1]["last"]))
            dsp = dsp + elp * dsn
            dels = dsn * q["sp"] * elp
            wp = _pair_mix(lo_lane, jnp.exp(heads[2 * p]["last"] - heads[2 * p]["col"]),
                           jnp.exp(heads[2 * p + 1]["last"] - heads[2 * p + 1]["col"]))
            dv = dot_nt(q["bg"], dsn)
            db[g] = db[g] + dot_nn(wp * q["xp"], dsn)
            dxp = dv * wp
            dwv = dv * q["xp"] * wp
            dgm = zeros
            for k in range(2):
                h = 2 * p + k
                mine = lo_lane if k == 0 else jnp.logical_not(lo_lane)
                mine_sub = lo_sub if k == 0 else jnp.logical_not(lo_sub)
                dyh = jnp.where(mine, dyp, 0.0)
                dm = dot_nt(dyh, q["xp"])
                dxp = dxp + dot_tn(q["m"][k], dyh)
                dgm = dgm + dm * heads[h]["dm"]
                e = dm * q["m"][k]
                onehot = lane == h
                dw_col = jnp.sum(jnp.where(mine, dwv, 0.0), axis=1, keepdims=True)
                col = (jnp.sum(e, axis=1, keepdims=True) + jnp.sum(jnp.where(mine, dea, 0.0), axis=1, keepdims=True)
                       - dw_col)
                dacum_col = dacum_col + jnp.where(onehot, col, 0.0)
                dacum_row = dacum_row - jnp.where(sub == h, jnp.sum(e, axis=0, keepdims=True), 0.0)
                dl = jnp.sum(dw_col) + jnp.sum(jnp.where(mine_sub, dels, 0.0))
                dlast = dlast + jnp.where(head_row == h, dl, 0.0)
                ddsk = ddsk + jnp.where(head_row == h, jnp.sum(jnp.where(mine, t, 0.0)), 0.0)
            dc[g] = dc[g] + dot_nn(dgm, q["bg"])
            db[g] = db[g] + dot_tn(dgm, q["cg"])
            dxs = dxs + dxp * q["dtp"]
            tt = dxp * q["xs"]
            for k in range(2):
                mine = lo_lane if k == 0 else jnp.logical_not(lo_lane)
                ddt = ddt + jnp.where(lane == 2 * p + k, jnp.sum(jnp.where(mine, tt, 0.0), axis=1, keepdims=True), 0.0)
            du_ref[:, p * LANE:(p + 1) * LANE] = dxs
            ds_ref[p] = dsp
        for g in range(2):
            du_ref[:, B_LO + g * D_STATE:B_LO + (g + 1) * D_STATE] = db[g]
            du_ref[:, C_LO + g * D_STATE:C_LO + (g + 1) * D_STATE] = dc[g]
        dacum = dacum_col + dacum_row.T + jnp.where(sub == BLK - 1, dlast, 0.0)
        dda = dot_tn(tri.astype(F32), dacum, exact=True)
        ddt = ddt + dda * a
        _acc(dalog_ref, first, jnp.sum(dda * dt, axis=0, keepdims=True) * a)
        dpre = ddt * _sigmoid(pre)
        ddt_ref[...] = dpre.astype(ddt_ref.dtype)
        _acc(ddtb_ref, first, jnp.sum(dpre, axis=0, keepdims=True))
        _acc(ddsk_ref, first, ddsk)

    rev = lambda i: (nc - 1 - i, 0)
    vec = _full_spec((1, LANE))
    rows = lambda n: pl.BlockSpec((BLK, n), rev)
    return hosted_call(
        body, sides, name="ssd_bwd", grid=(nc,),
        in_specs=[rows(CONV_DIM), rows(SSD_WIDTH), rows(LANE),
                  pl.BlockSpec((1, N_PAIR, LANE, D_STATE), lambda i: (nc - 1 - i, 0, 0, 0)), rows(SSD_WIDTH),
                  vec, vec, vec, _full_spec((1, SSD_WIDTH))],
        out_specs=[rows(CONV_DIM), rows(SSD_WIDTH), rows(LANE), vec, vec, vec, _full_spec((1, SSD_WIDTH))],
        out_shape=[jax.ShapeDtypeStruct((s, CONV_DIM), F32), jax.ShapeDtypeStruct((s, SSD_WIDTH), BF16),
                   jax.ShapeDtypeStruct((s, LANE), BF16)] + [jax.ShapeDtypeStruct((1, LANE), F32)] * 3
        + [jax.ShapeDtypeStruct((1, SSD_WIDTH), F32)],
        scratch_shapes=[pltpu.VMEM((N_PAIR, LANE, D_STATE), F32)],
        compiler_params=_params(1), operands=(u, z, dtr, st, dyo, dtb, alog, dsk, nw))


SB_PAIRS = SB_WIDTH // LANE
SB_SCALE = HEAD_DIM ** -0.5


SB_TQ = 256


def _sb_tq(s):
    return min(SB_TQ, s)


def _sb_stack(x):
    lo_lane = _iota(x.shape, 1) < HEAD_DIM
    return jnp.concatenate([jnp.where(lo_lane, x, 0.0), jnp.where(lo_lane, 0.0, x)], axis=0)


def _sb_unstack(x2):
    tq = x2.shape[0] // 2
    lo_lane = _iota((tq, LANE), 1) < HEAD_DIM
    return jnp.where(lo_lane, x2[:tq], x2[tq:])


def _sb_logits(q2, kj, row0, col0, masked):
    shape = (q2.shape[0], BLK)
    tq = shape[0] // 2
    z = dot_nt(q2, kj)
    t = jnp.log(1.0 + jnp.exp(-jnp.abs(z)))
    ls = jnp.minimum(z, 0.0) - t
    lk = jnp.minimum(-z, 0.0) - t
    if not masked:
        return None, ls, lk
    row = _iota(shape, 0)
    valid = (col0 + _iota(shape, 1)) < (row0 + jnp.where(row < tq, row, row - tq))
    return valid, ls, jnp.where(valid, lk, 0.0)


def _sb_where(valid, x):
    return x if valid is None else jnp.where(valid, x, 0.0)


def _sb_win(x2, lo):
    if lo == 0:
        return x2
    tq = x2.shape[0] // 2
    return jnp.concatenate([x2[lo:tq], x2[tq + lo:]], axis=0)


def _sb_unwin(x2, xw, lo):
    if lo == 0:
        return xw
    tq = x2.shape[0] // 2
    return jnp.concatenate([x2[:lo], xw[:tq - lo], x2[tq:tq + lo], xw[tq - lo:]], axis=0)


def _sb_add(ref, val, lo):
    if lo == 0:
        ref[...] += val
    else:
        tq = ref.shape[0] // 2
        ref[lo:tq, :] += val[:tq - lo]
        ref[tq + lo:, :] += val[tq - lo:]


def _sums(x, mask2, parts):
    acc = None
    rest = x
    for _ in range(parts):
        term = rest.astype(BF16)
        rest = rest - term.astype(F32)
        d = lax.dot_general(term, mask2, (((1,), (0,)), ((), ())), preferred_element_type=F32)
        acc = d if acc is None else acc + d
    return acc[:, :BLK], acc[:, BLK:]


def _mask2(cond):
    return jnp.concatenate([cond.astype(BF16), jnp.ones(cond.shape, BF16)], axis=1)


def _sb_specs(s):
    tq = _sb_tq(s)
    qspec = pl.BlockSpec((tq, LANE), lambda p, i: (i, p))
    kspec = pl.BlockSpec((s, LANE), lambda p, i: (0, SB_PAIRS + p))
    vspec = pl.BlockSpec((s, LANE), lambda p, i: (0, 2 * SB_PAIRS + p))
    return qspec, kspec, vspec


SB_FLOOR = -104.0


def sb_fwd(qkv, sides=()):
    s = qkv.shape[0]
    tq = _sb_tq(s)
    kpq = tq // BLK

    def body(q_ref, k_ref, v_ref, o_ref, t_ref, n_ref, acc_ref):
        qi = pl.program_id(1)
        q2 = _sb_stack(q_ref[...] * SB_SCALE).astype(BF16)
        later = _mask2(_iota((BLK, BLK), 0) > _iota((BLK, BLK), 1))
        acc_ref[...] = jnp.zeros_like(acc_ref)

        def step(j, r, masked, lo=0):
            rows = pl.ds(pl.multiple_of(j * BLK, BLK), BLK)
            rw = _sb_win(r, lo)
            valid, ls, lk = _sb_logits(_sb_win(q2, lo), k_ref[rows, :], qi * tq + lo, j * BLK, masked)
            after, total = _sums(lk, later, 2)
            w = _sb_where(valid, jnp.exp(ls + rw + after))
            _sb_add(acc_ref, dot_nn(w, v_ref[rows, :]), lo)
            return _sb_unwin(r, rw + total, lo)

        r = jnp.zeros((2 * tq, LANE), F32)
        for d in reversed(range(kpq)):
            r = step(kpq * qi + d, r, True, d * BLK)

        def tile(g, r):
            for d in reversed(range(kpq)):
                r = step(kpq * (qi - 1 - g) + d, r, False)
            return r

        n, r = lax.while_loop(lambda c: jnp.logical_and(c[0] < qi, jnp.max(c[1]) > SB_FLOOR),
                              lambda c: (c[0] + 1, tile(c[0], c[1])), (jnp.int32(0), r))
        o_ref[...] = _sb_unstack(acc_ref[...])
        t_ref[...] = jnp.concatenate([r[:tq], r[tq:]], axis=1)
        n_ref[...] = jnp.zeros(n_ref.shape, F32) + n.astype(F32)

    return hosted_call(
        body, sides, name="sb_fwd", grid=(SB_PAIRS, s // tq),
        in_specs=list(_sb_specs(s)),
        out_specs=[pl.BlockSpec((tq, LANE), lambda p, i: (i, p)), pl.BlockSpec((tq, 2 * LANE), lambda p, i: (i, p)),
                   pl.BlockSpec((None, None, 8, LANE), lambda p, i: (p, i, 0, 0))],
        out_shape=[jax.ShapeDtypeStruct((s, SB_WIDTH), F32), jax.ShapeDtypeStruct((s, 2 * SB_WIDTH), F32),
                   jax.ShapeDtypeStruct((SB_PAIRS, s // tq, 8, LANE), F32)],
        scratch_shapes=[pltpu.VMEM((2 * tq, LANE), F32)],
        compiler_params=_params(2), operands=(qkv, qkv, qkv))


def sb_bwd(qkv, tot, swept, do, do_col=0, sides=()):
    s = qkv.shape[0]
    tq = _sb_tq(s)
    kpq = tq // BLK

    def body(q_ref, k_ref, v_ref, t_ref, n_ref, do_ref, dq_ref, dk_ref, dv_ref, acc_ref):
        qi = pl.program_id(1)
        n = jnp.clip(jnp.max(n_ref[...]).astype(jnp.int32), 0, qi)
        q2 = _sb_stack(q_ref[...] * SB_SCALE).astype(BF16)
        do2 = _sb_stack(do_ref[...]).astype(BF16)
        tot2 = jnp.concatenate([t_ref[:, :LANE], t_ref[:, LANE:]], axis=0)
        sq = (BLK, BLK)
        later = _mask2(_iota(sq, 0) > _iota(sq, 1))
        before = _mask2(_iota(sq, 0) < _iota(sq, 1))
        acc_ref[...] = jnp.zeros_like(acc_ref)

        @pl.when(qi == 0)
        def _():
            dk_ref[...] = jnp.zeros_like(dk_ref)
            dv_ref[...] = jnp.zeros_like(dv_ref)

        def step(j, carry, masked, lo=0):
            rc, fc = carry
            rows = pl.ds(pl.multiple_of(j * BLK, BLK), BLK)
            kj = k_ref[rows, :]
            vj = v_ref[rows, :]
            qw, dow, fw = _sb_win(q2, lo), _sb_win(do2, lo), _sb_win(fc, lo)
            valid, ls, lk = _sb_logits(qw, kj, qi * tq + lo, j * BLK, masked)
            after, total = _sums(lk, later, 2)
            rw = _sb_win(rc, lo) - total
            w = _sb_where(valid, jnp.exp(ls + rw + after))
            e = w * dot_nt(dow, vj)
            f_in, f_tot = _sums(e, before, 2)
            sg = jnp.exp(ls)
            dz = _sb_where(valid, e * (1.0 - sg) - (fw + f_in) * sg)
            _sb_add(acc_ref, dot_nn(dz, kj), lo)
            dk_ref[rows, :] += dot_tn(dz, qw)
            dv_ref[rows, :] += dot_tn(w, dow)
            return _sb_unwin(rc, rw, lo), _sb_unwin(fc, fw + f_tot, lo)

        def tile(g, carry):
            for d in range(kpq):
                carry = step(kpq * g + d, carry, False)
            return carry

        carry = lax.fori_loop(qi - n, qi, tile, (tot2, jnp.zeros((2 * tq, LANE), F32)))
        for d in range(kpq):
            carry = step(kpq * qi + d, carry, True, d * BLK)
        dq_ref[...] = (SB_SCALE * _sb_unstack(acc_ref[...])).astype(dq_ref.dtype)

    qspec, kspec, vspec = _sb_specs(s)
    blk = pl.BlockSpec((tq, LANE), lambda p, i: (i, p))
    acc = pl.BlockSpec((s, LANE), lambda p, i: (0, p))
    return hosted_call(
        body, sides, name="sb_bwd", grid=(SB_PAIRS, s // tq),
        in_specs=[qspec, kspec, vspec, pl.BlockSpec((tq, 2 * LANE), lambda p, i: (i, p)),
                  pl.BlockSpec((None, None, 8, LANE), lambda p, i: (p, i, 0, 0)),
                  pl.BlockSpec((tq, LANE), lambda p, i: (i, do_col + p))],
        out_specs=[blk, acc, acc],
        out_shape=[jax.ShapeDtypeStruct((s, SB_WIDTH), BF16)] + [jax.ShapeDtypeStruct((s, SB_WIDTH), F32)] * 2,
        scratch_shapes=[pltpu.VMEM((2 * tq, LANE), F32)],
        compiler_params=_params(2), operands=(qkv, qkv, qkv, tot, swept, do))


POOL_GROUP_DIM = POOL_WIDTH // len(POOL_WINDOWS)


assert all(w == 2 ** (i + 1) for i, w in enumerate(POOL_WINDOWS))


def _pool_inv(c):
    group = _iota((BLK, POOL_WIDTH), 1) // POOL_GROUP_DIM
    pos = c * BLK + _iota((BLK, POOL_WIDTH), 0)
    win = jnp.zeros((BLK, POOL_WIDTH), jnp.int32)
    for gi, wn in enumerate(POOL_WINDOWS):
        win = jnp.where(group == gi, wn, win)
    return 1.0 / jnp.minimum(pos + 1, win).astype(F32)


def _window_sums(ext, trailing):
    group = _iota(ext.shape, 1) // POOL_GROUP_DIM
    acc = ext
    out = None
    for gi in range(len(POOL_WINDOWS)):
        shift = 2 ** gi
        acc = acc + pltpu.roll(acc, shift if trailing else ext.shape[0] - shift, 0)
        out = acc if out is None else jnp.where(group == gi, acc, out)
    return out


def _pool_pooled(ext, cur, inv):
    return _window_sums(ext, True)[BLK:] * inv - cur


def pool_fwd(p, wblk, pb, ps):
    s, n = p.shape

    def body(cur_ref, prev_ref, w_ref, pb_ref, ps_ref, o_ref):
        c = pl.program_id(0)
        cur = cur_ref[...]
        prev = jnp.where(c > 0, prev_ref[...], 0.0)
        pooled = _pool_pooled(jnp.concatenate([prev, cur], axis=0), cur, _pool_inv(c))
        o_ref[...] = (dot_nn(pooled, w_ref[...]) + pb_ref[...]) * ps_ref[...]

    return pl.pallas_call(
        body, name="pool_fwd", grid=(s // BLK,),
        in_specs=[pl.BlockSpec((BLK, n), lambda c: (c, 0)), pl.BlockSpec((BLK, n), lambda c: (jnp.maximum(c - 1, 0), 0)),
                  _full_spec((n, n)), _full_spec((1, n)), _full_spec((1, n))],
        out_specs=pl.BlockSpec((BLK, n), lambda c: (c, 0)), out_shape=jax.ShapeDtypeStruct((s, n), F32),
        compiler_params=_params(1),
    )(p, p, wblk, pb, ps)


def pool_bwd(p, wblk, pb, ps, dout, do_col=0, sides=()):
    s, n = p.shape
    nb = s // BLK

    def body(cur_ref, prev_ref, w_ref, pb_ref, ps_ref, do_ref, dp_ref, dw_ref, dpb_ref, dps_ref, carry_ref):
        i = pl.program_id(0)
        c = nb - 1 - i
        first = i == 0
        cur = cur_ref[...]
        prev = jnp.where(c > 0, prev_ref[...], 0.0)
        inv = _pool_inv(c)
        pooled = _pool_pooled(jnp.concatenate([prev, cur], axis=0), cur, inv)
        mixed = dot_nn(pooled, w_ref[...]) + pb_ref[...]
        dov = do_ref[...]
        dmixed = dov * ps_ref[...]
        _acc(dps_ref, first, jnp.sum(dov * mixed, axis=0, keepdims=True))
        _acc(dpb_ref, first, jnp.sum(dmixed, axis=0, keepdims=True))
        _acc(dw_ref, first, dot_tn(pooled, dmixed))
        dpooled = dot_nt(dmixed, w_ref[...])
        dext = _window_sums(jnp.concatenate([jnp.zeros((BLK, n), F32), dpooled * inv], axis=0), False)

        @pl.when(first)
        def _():
            carry_ref[...] = jnp.zeros_like(carry_ref)

        dp_ref[...] = (dext[BLK:] - dpooled + carry_ref[...]).astype(dp_ref.dtype)
        carry_ref[...] = dext[:BLK]

    rev = lambda i: (nb - 1 - i, 0)
    return hosted_call(
        body, sides, name="pool_bwd", grid=(nb,),
        in_specs=[pl.BlockSpec((BLK, n), rev), pl.BlockSpec((BLK, n), lambda i: (jnp.maximum(nb - 2 - i, 0), 0)),
                  _full_spec((n, n)), _full_spec((1, n)), _full_spec((1, n)),
                  pl.BlockSpec((BLK, n), lambda i: (nb - 1 - i, do_col))],
        out_specs=[pl.BlockSpec((BLK, n), rev), _full_spec((n, n)), _full_spec((1, n)), _full_spec((1, n))],
        out_shape=[jax.ShapeDtypeStruct((s, n), BF16), jax.ShapeDtypeStruct((n, n), F32),
                   jax.ShapeDtypeStruct((1, n), F32), jax.ShapeDtypeStruct((1, n), F32)],
        scratch_shapes=[pltpu.VMEM((BLK, n), F32)],
        compiler_params=_params(1), operands=(p, p, wblk, pb, ps, dout))


def _row_tile(rows):
    if rows <= 512:
        return rows
    for t in (512, 256, 128, 64, 32, 16, 8):
        if rows % t == 0:
            return t
    return rows


def adamw(w, g, m, v):
    (d, nm, nv), _ = adamw_many([w], [g], [m], [v])
    return d[0], nm[0], nv[0]


def adamw_many(ws, gs, ms, vs, sides=()):
    k = len(ws)
    n, rows, cols = ws[0].shape
    tr = _row_tile(rows)
    while 7 * k * 2 * tr * cols * 4 > VMEM_LIMIT // 2 and tr % 16 == 0:
        tr //= 2

    def body(*refs):
        for i in range(k):
            outs = _adamw_math(*[refs[j * k + i][...] for j in range(4)])
            for j in range(3):
                refs[(4 + j) * k + i][...] = outs[j]

    spec = pl.BlockSpec((1, tr, cols), lambda i, j: (i, j, 0))
    outs, side_outs = hosted_call(
        body, sides, name="adamw", grid=(n, rows // tr), in_specs=[spec] * (4 * k), out_specs=[spec] * (3 * k),
        out_shape=[jax.ShapeDtypeStruct(ws[0].shape, F32)] * (3 * k), scratch_shapes=[],
        compiler_params=_params(2), operands=(*ws, *gs, *ms, *vs))
    return (outs[:k], outs[k:2 * k], outs[2 * k:]), side_outs


def _adamw_math(w, g, m, v):
    nm = ADAM_B1 * m + (1.0 - ADAM_B1) * g
    nv = ADAM_B2 * v + (1.0 - ADAM_B2) * (g * g)
    m_hat = nm / (1.0 - ADAM_B1 ** ADAM_STEP)
    v_hat = nv / (1.0 - ADAM_B2 ** ADAM_STEP)
    return -ADAM_LR * (m_hat / (jnp.sqrt(v_hat) + ADAM_EPS) + ADAM_WD * w), nm, nv


def adamw_small(ws, gs, ms, vs):
    n = len(ws)

    def body(*refs):
        for i in range(n):
            outs = _adamw_math(*[refs[k * n + i][...] for k in range(4)])
            for k in range(3):
                refs[(4 + k) * n + i][...] = outs[k]

    vm = pl.BlockSpec(memory_space=pltpu.VMEM)
    outs = pl.pallas_call(
        body, name="adamw_small", in_specs=[vm] * (4 * n), out_specs=[vm] * (3 * n),
        out_shape=[jax.ShapeDtypeStruct(w.shape, F32) for w in ws] * 3,
    )(*ws, *gs, *ms, *vs)
    return outs[:n], outs[n:2 * n], outs[2 * n:]


def slab_sum(srcs, n_out, out_dtype, into=None, slot=0):
    _, rows, cols = srcs[0][0].shape
    tr = _row_tile(rows)
    n_src = len(srcs)
    sel = jnp.stack([jnp.asarray(base, jnp.int32) for _, base, _ in srcs])

    def body(sel_ref, *refs):
        acc = refs[0][...].astype(F32)
        for r in refs[1:n_src]:
            acc = acc + r[...].astype(F32)
        refs[-1][...] = acc.astype(out_dtype)

    def in_spec(k, step):
        return pl.BlockSpec((None, tr, cols), lambda o, i, sel_ref: (sel_ref[k] + step * o, i, 0))

    shape = (n_out, rows, cols) if into is None else into.shape
    return pl.pallas_call(
        body, name="slab_sum",
        grid_spec=pltpu.PrefetchScalarGridSpec(
            num_scalar_prefetch=1, grid=(n_out, rows // tr),
            in_specs=[in_spec(k, step) for k, (_, _, step) in enumerate(srcs)] + ([] if into is None else [ANY]),
            out_specs=pl.BlockSpec((None, tr, cols), lambda o, i, sel_ref: (slot + o, i, 0))),
        out_shape=jax.ShapeDtypeStruct(shape, out_dtype), compiler_params=_params(2),
        input_output_aliases={} if into is None else {1 + n_src: 0},
    )(sel, *[a for a, _, _ in srcs], *([] if into is None else [into]))


ICI_FLIPS = ((1, 0, 0), (0, 1, 0), (1, 1, 0))
D2D_FLIPS = ((0, 0, 1),)
ANY = pl.BlockSpec(memory_space=pl.ANY)


def _me():
    return lax.axis_index("x"), lax.axis_index("y"), lax.axis_index("c")


def _flipped(me, flip):
    return tuple(1 - m if f else m for m, f in zip(me, flip))


def _chip(dev):
    return 2 * dev[0] + dev[1]


def _dev(dev):
    return 4 * dev[0] + 2 * dev[1] + dev[2]


N_CHIP = 4
D2D = (0, 0, 1)


class Exchange:
    def __init__(self, xs, n_out, copies, own=None, in_place=False):
        self.xs, self.copies, self.own, self.in_place = list(xs), copies, own, in_place
        self.n_arr, self.n_cp = len(self.xs), len(copies)
        self.out_shape = [jax.ShapeDtypeStruct((n_out,) + x.shape[1:], x.dtype) for x in self.xs]
        self.scratch = [pltpu.SemaphoreType.DMA((self.n_arr * self.n_cp,)),
                        pltpu.SemaphoreType.DMA((self.n_arr * self.n_cp,)), pltpu.SemaphoreType.DMA((self.n_arr,))]

    def _own(self, x_refs, o_refs, sems, me):
        if self.own is None:
            return []
        return [pltpu.make_async_copy(x_refs[a].at[self.own[0](me)], o_refs[a].at[self.own[1](me)], sems[2].at[a])
                for a in range(self.n_arr)]

    def _copy(self, x_refs, o_refs, sems, me, a, j, sender):
        flip, src_slot, dst_slot = self.copies[j]
        k = a * self.n_cp + j
        return pltpu.make_async_remote_copy(
            src_ref=x_refs[a].at[src_slot(me)], dst_ref=o_refs[a].at[dst_slot(sender)],
            send_sem=sems[0].at[k], recv_sem=sems[1].at[k], device_id=_flipped(me, flip), device_id_type=MESH)

    def start(self, x_refs, o_refs, sems):
        me = _me()
        for cp in self._own(x_refs, o_refs, sems, me):
            cp.start()
        for j in range(self.n_cp):
            for a in range(self.n_arr):
                self._copy(x_refs, o_refs, sems, me, a, j, me).start()

    def wait(self, x_refs, o_refs, sems):
        me = _me()
        for j in range(self.n_cp):
            for a in range(self.n_arr):
                self._copy(x_refs, o_refs, sems, me, a, j, _flipped(me, self.copies[j][0])).wait_recv()
        for j in range(self.n_cp):
            for a in range(self.n_arr):
                self._copy(x_refs, o_refs, sems, me, a, j, me).wait_send()
        for cp in self._own(x_refs, o_refs, sems, me):
            cp.wait()

    def run(self, name):
        n = self.n_arr

        def body(*refs):
            self.start(refs[:n], refs[n:2 * n], refs[2 * n:])
            self.wait(refs[:n], refs[n:2 * n], refs[2 * n:])

        return pl.pallas_call(
            body, name=name, in_specs=[ANY] * n, out_specs=[ANY] * n, out_shape=self.out_shape,
            input_output_aliases={a: a for a in range(n)} if self.in_place else {}, scratch_shapes=self.scratch,
        )(*self.xs)


def hosted_call(body, sides, *, name, grid, in_specs, out_specs, out_shape, scratch_shapes, compiler_params, operands):
    n_in, n_out, n_scr = len(in_specs), len(out_specs), len(scratch_shapes)
    live = [s for s in sides if s is not None]
    if not live:
        outs = pl.pallas_call(body, name=name, grid=grid, in_specs=in_specs, out_specs=out_specs, out_shape=out_shape,
                              scratch_shapes=scratch_shapes, compiler_params=compiler_params)(*operands)
        return outs, [None] * len(sides)
    n = sum(s.n_arr for s in live)
    lo = [sum(s.n_arr for s in live[:i]) for i in range(len(live))]

    def full_body(*refs):
        ins, sx = refs[:n_in], refs[n_in:n_in + n]
        outs, so = refs[n_in + n:n_in + n + n_out], refs[n_in + n + n_out:n_in + 2 * n + n_out]
        scr, sems = refs[n_in + 2 * n + n_out:n_in + 2 * n + n_out + n_scr], refs[n_in + 2 * n + n_out + n_scr:]
        first = functools.reduce(jnp.logical_and, [pl.program_id(a) == 0 for a in range(len(grid))])
        last = functools.reduce(jnp.logical_and, [pl.program_id(a) == g - 1 for a, g in enumerate(grid)])
        parts = [(s, sx[l:l + s.n_arr], so[l:l + s.n_arr], sems[3 * i:3 * i + 3]) for i, (s, l) in enumerate(zip(live, lo))]

        @pl.when(first)
        def _():
            for s, x, o, m in parts:
                s.start(x, o, m)

        body(*ins, *outs, *scr)

        @pl.when(last)
        def _():
            for s, x, o, m in parts:
                s.wait(x, o, m)

    aliases = {n_in + l + a: n_out + l + a for s, l in zip(live, lo) if s.in_place for a in range(s.n_arr)}
    outs = pl.pallas_call(
        full_body, name=name + "_x", grid=grid, in_specs=list(in_specs) + [ANY] * n,
        out_specs=list(out_specs) + [ANY] * n, out_shape=list(out_shape) + [o for s in live for o in s.out_shape],
        input_output_aliases=aliases,
        scratch_shapes=list(scratch_shapes) + [m for s in live for m in s.scratch], compiler_params=compiler_params,
    )(*operands, *[x for s in live for x in s.xs])
    side_outs = iter([outs[n_out + l:n_out + l + s.n_arr] for s, l in zip(live, lo)])
    return outs[:n_out], [next(side_outs) if s is not None else None for s in sides]


def gather_ici(shards):
    ici = [(f, lambda me: 0, _dev) for f in ICI_FLIPS]
    return Exchange([s[None] for s in shards], N_DEV, ici, (lambda me: 0, _dev))


def gather_d2d(blocks):
    d2d = [(D2D, (lambda me, k=k: 2 * k + me[2]), (lambda sender, k=k: 2 * k + sender[2])) for k in range(N_CHIP)]
    return Exchange(blocks, N_DEV, d2d, None, in_place=True)


def gathered(blocks):
    return [b.reshape(-1, b.shape[2]) for b in blocks]


def scatter_d2d(parts):
    d2d = [(D2D, (lambda me, k=k: 2 * k + 1 - me[2]), (lambda sender, k=k: k)) for k in range(N_CHIP)]
    return Exchange(parts, N_CHIP, d2d)


def chip_sums(parts, sib):
    c = _me()[2]
    return [slab_sum([(p, c, 2), (s, 0, 1)], N_CHIP, BF16) for p, s in zip(parts, sib)]


def scatter_ici(sums):
    ici = [(f, (lambda me, f=f: _chip(_flipped(me, f))), (lambda sender, i=i: i)) for i, f in enumerate(ICI_FLIPS)]
    return Exchange(sums, len(ICI_FLIPS), ici)


def device_sums(sums, got, into=None, slot=0):
    x, y, _ = _me()
    outs = [slab_sum([(cs, 2 * x + y, 0)] + [(g, i, 0) for i in range(len(ICI_FLIPS))], 1, F32,
                     None if into is None else into[a], slot) for a, (cs, g) in enumerate(zip(sums, got))]
    return outs if into is not None else [o[0] for o in outs]


def all_gather(shards):
    blocks = gather_ici(shards).run("gather_ici")
    return gathered(gather_d2d(blocks).run("gather_d2d"))


def all_reduce_small(v):
    flips = D2D_FLIPS + ICI_FLIPS[:2]

    def body(v_ref, o_ref, got_ref, send_sems, recv_sems):
        me = _me()
        o_ref[...] = v_ref[...]
        for i, flip in enumerate(flips):
            cp = pltpu.make_async_remote_copy(
                src_ref=o_ref, dst_ref=got_ref.at[i], send_sem=send_sems.at[i], recv_sem=recv_sems.at[i],
                device_id=_flipped(me, flip), device_id_type=MESH)
            cp.start()
            cp.wait()
            o_ref[...] = o_ref[...] + got_ref[i]

    vm = pl.BlockSpec(memory_space=pltpu.VMEM)
    return pl.pallas_call(
        body, name="all_reduce_small", in_specs=[vm], out_specs=vm, out_shape=jax.ShapeDtypeStruct(v.shape, F32),
        scratch_shapes=[pltpu.VMEM((len(flips),) + v.shape, F32), pltpu.SemaphoreType.DMA((len(flips),)),
                        pltpu.SemaphoreType.DMA((len(flips),))],
    )(v)


def _perm_rows(wt):
    pad = jnp.zeros((D_IN_PAD - D_IN_PROJ, wt.shape[1]), wt.dtype)
    return jnp.concatenate([wt[:DT_LO], wt[DT_HI:], wt[DT_LO:DT_HI], pad], axis=0)


def _unperm_rows(dwt):
    n = D_IN_PROJ - (DT_HI - DT_LO)
    return jnp.concatenate([dwt[:DT_LO], dwt[n:D_IN_PROJ], dwt[DT_LO:n]], axis=0)


def _pad_lanes(v):
    return jnp.pad(v, ((0, 0), (0, LANE - v.shape[1])))[:, None]


def _block_diag(w):
    l, g, n, _ = w.shape
    out = jnp.zeros((l, g * n, g * n), w.dtype)
    for i in range(g):
        out = out.at[:, i * n:(i + 1) * n, i * n:(i + 1) * n].set(w[:, i])
    return out


def _pack(groups):
    flat = []
    for grp in groups:
        parts = [a.reshape(-1) for a in (grp if isinstance(grp, (list, tuple)) else [grp])]
        n = sum(p.shape[0] for p in parts)
        if -n % LANE:
            parts.append(jnp.zeros((-n % LANE,), parts[0].dtype))
        flat += parts
    return jnp.concatenate(flat).reshape(-1, LANE)


def _unpack(buf, shapes):
    out = []
    lo = 0
    buf = buf.reshape(-1)
    for shp in shapes:
        n = 1
        for k in shp:
            n *= k
        out.append(buf[lo:lo + n].reshape(shp))
        lo += n + (-n % LANE)
    return out


def small_params(w, conv_w_full):
    return dict(
        n1w=w["norm1_w"][:, None], cw=jnp.pad(conv_w_full, ((0, 0), (0, 8 - CONV_WIDTH), (0, 0))),
        cb=w["conv_b"][:, None], dtb=_pad_lanes(w["dt_bias"]), alog=_pad_lanes(w["a_log"]), dsk=_pad_lanes(w["d_skip"]),
        snw=w["ssd_norm_w"][:, None], wblk=_block_diag(w["pool_w"]), pb=w["pool_b"].reshape(-1, 1, POOL_WIDTH),
        ps=w["pool_scale"][:, None], n2w=w["norm2_w"][:, None])


MIX = ("w_in", "w_out")
FFN = ("w_gate", "w_up", "w_down")


def layer_params(small, l):
    return {k: v[l] for k, v in small.items()}


def mix_weights(whole):
    return _perm_rows(whole[0]), whole[1]


def _slabs(g):
    return g.reshape(N_DEV, -1, g.shape[-1])


def _layer_fwd(x, p, mix, ffn=None, ffn_shards=None, next_mix_shards=None):
    ici = [gather_ici([sh]) for sh in ffn_shards] if ffn_shards is not None else [None] * len(FFN)
    (z, xbc, qkv, pp, dtr, h1), (blk_g,) = inproj_fwd(x, p["n1w"], mix[0], [ici[0]])
    (u,), _ = conv_fwd(xbc, p["cw"], p["cb"])
    (y_ssd, st), (blk_u,) = ssd_fwd(u, z, dtr, p["dtb"], p["alog"], p["dsk"], p["snw"], [ici[1]])
    (o, tot, swept), (blk_d,) = sb_fwd(qkv, [ici[2]])
    blocks = blk_g + blk_u + blk_d if ffn_shards is not None else None
    yp = pool_fwd(pp, p["wblk"], p["pb"], p["ps"])
    (x_mid, ycat), (blocks,) = outproj_fwd([y_ssd, o, yp], mix[1], x,
                                           [gather_d2d(blocks) if blocks is not None else None])
    if blocks is not None:
        ffn = gathered(blocks)
    (x_out, g, uu), (nxt,) = ffn_fwd(x_mid, p["n2w"], *ffn,
                                     [gather_ici(next_mix_shards) if next_mix_shards is not None else None])
    sv = dict(x=x, z=z, xbc=xbc, qkv=qkv, pp=pp, dtr=dtr, h1=h1, u=u, st=st, tot=tot, swept=swept, ycat=ycat,
              x_mid=x_mid, g=g, uu=uu, w_in=mix[0], w_out=mix[1], wg=ffn[0], wu=ffn[1], wd=ffn[2])
    return x_out, sv, nxt


def _layer_bwd(dxo, sv, p, pending_mix=None, exchange=False, into_ffn=(), into_mix=()):
    (dx_mid, dn2w, a, dg, du, h2), (sib,) = ffn_bwd(
        dxo, sv["x_mid"], sv["g"], sv["uu"], p["n2w"], sv["wg"], sv["wu"], sv["wd"],
        [scatter_d2d(pending_mix) if pending_mix is not None else None])
    sums_mix = chip_sums(pending_mix, sib) if pending_mix is not None else None
    dwg, dwu = mm_tn([dg, du], h2)
    gr = dict(norm2_w=dn2w[0], w_gate=dwg, w_up=dwu, w_down=mm_tn(a, dxo))
    parts = [_slabs(gr[k]) for k in FFN] if exchange else None
    (dycat,), (sib,) = outproj_bwd(dx_mid, sv["w_out"], [scatter_d2d(parts) if exchange else None])
    sums_ffn = chip_sums(parts, sib) if exchange else None
    gr["w_out"] = mm_tn(sv["ycat"], dx_mid)
    (dp, dwblk, dpb, dps), _ = pool_bwd(sv["pp"], p["wblk"], p["pb"], p["ps"], dycat,
                                        (SSD_WIDTH + SB_WIDTH) // POOL_WIDTH)
    n = POOL_GROUP_DIM
    gr["pool_w"] = jnp.stack([dwblk[i * n:(i + 1) * n, i * n:(i + 1) * n] for i in range(len(POOL_WINDOWS))])
    gr["pool_b"] = dpb.reshape(len(POOL_WINDOWS), n)
    gr["pool_scale"] = dps[0]
    (dq, dk, dv), (got_ud,) = sb_bwd(sv["qkv"], sv["tot"], sv["swept"], dycat, SSD_WIDTH // LANE,
                                     [scatter_ici(sums_ffn[1:]) if exchange else None])
    (du_, dz, ddtr, ddtb, dalog, ddsk, dsnw), (got_mix,) = ssd_bwd(
        sv["u"], sv["z"], sv["dtr"], sv["st"], dycat, p["dtb"], p["alog"], p["dsk"], p["snw"],
        [scatter_ici(sums_mix) if sums_mix is not None else None])
    done_mix = device_sums(sums_mix, got_mix, *into_mix) if sums_mix is not None else None
    gr.update(dt_bias=ddtb[0, :SSD_HEADS], a_log=dalog[0, :SSD_HEADS], d_skip=ddsk[0, :SSD_HEADS], ssd_norm_w=dsnw[0])
    (dxbc, dcw, dcb), (got_g,) = conv_bwd(sv["xbc"], p["cw"], p["cb"], du_,
                                          [scatter_ici(sums_ffn[:1]) if exchange else None])
    done_ffn = device_sums(sums_ffn, got_g + got_ud, *into_ffn) if exchange else None
    gr.update(conv_w=dcw[:CONV_WIDTH], conv_b=dcb[0])
    dx, dn1w, dproj = inproj_bwd([dz, dxbc, dq, dk, dv, dp, ddtr], sv["w_in"], sv["x"], p["n1w"], dx_mid)
    gr.update(norm1_w=dn1w[0], w_in=_unperm_rows(mm_tn(dproj, sv["h1"])))
    return dx, gr, done_ffn, done_mix


def local_step(x, tgt, params, weights, final_w):
    saved = []
    for p, (mix, ffn) in zip(params, weights):
        x, sv, _ = _layer_fwd(x, p, mix, ffn)
        saved.append(sv)
    loss, dx, dfw = head_loss(x, final_w[None], tgt)
    grads = []
    for p, sv in zip(reversed(params), reversed(saved)):
        dx, gr, _, _ = _layer_bwd(dx, sv, p)
        grads.append(gr)
    grads.reverse()
    return loss, dx, dfw[0], grads


WEIGHTS = ("norm1_w", "w_in", "conv_w", "conv_b", "dt_bias", "a_log", "d_skip", "ssd_norm_w", "pool_w", "pool_b",
           "pool_scale", "w_out", "norm2_w", "w_gate", "w_up", "w_down", "final_norm_w")
COL_SHARDED = ("w_in", "w_gate", "w_up")
ROW_SHARDED = ("w_out", "w_down")
SMALL = tuple(k for k in WEIGHTS if k not in COL_SHARDED + ROW_SHARDED)


def kernel(x, norm1_w, w_in, conv_w, conv_b, dt_bias, a_log, d_skip, ssd_norm_w, pool_w, pool_b, pool_scale, w_out, norm2_w, w_gate, w_up, w_down, final_norm_w, loss_target, m_norm1_w, m_w_in, m_conv_w, m_conv_b, m_dt_bias, m_a_log, m_d_skip, m_ssd_norm_w, m_pool_w, m_pool_b, m_pool_scale, m_w_out, m_norm2_w, m_w_gate, m_w_up, m_w_down, m_final_norm_w, v_norm1_w, v_w_in, v_conv_w, v_conv_b, v_dt_bias, v_a_log, v_d_skip, v_ssd_norm_w, v_pool_w, v_pool_b, v_pool_scale, v_w_out, v_norm2_w, v_w_gate, v_w_up, v_w_down, v_final_norm_w):
    args = dict(locals())
    w = {k: args[k] for k in WEIGHTS}
    m = {k: args["m_" + k] for k in WEIGHTS}
    v = {k: args["v_" + k] for k in WEIGHTS}
    depth = w_in.shape[0]
    dev = _dev(_me())
    n_cw = conv_w.shape[-1]

    shards = {k: (jnp.swapaxes(w[k], 1, 2) if k in COL_SHARDED else w[k]).astype(BF16) for k in MIX + FFN}
    whole = all_gather([jnp.swapaxes(conv_w, 0, 2).reshape(n_cw, -1)] + [shards[k][0] for k in MIX])
    conv_w_full = jnp.swapaxes(whole[0].reshape(N_DEV * n_cw, CONV_WIDTH, depth), 0, 2)
    small = small_params(w, conv_w_full)
    xs = x[0]
    params, saved = [layer_params(small, l) for l in range(depth)], []
    mix = mix_weights(whole[1:])
    for l in range(depth):
        xs, sv, nxt = _layer_fwd(xs, params[l], mix, ffn_shards=[shards[k][l] for k in FFN],
                                 next_mix_shards=[shards[k][l + 1] for k in MIX] if l + 1 < depth else None)
        saved.append(sv)
        if nxt is not None:
            mix = mix_weights(gathered(gather_d2d(nxt).run("gather_d2d")))
    loss, dx, dfw = head_loss(xs, final_norm_w[None], loss_target[0])
    layer_grads = [None] * depth
    native = {k: lax.empty((depth,) + shards[k].shape[1:], F32) for k in MIX + FFN}
    pending = None
    for l in reversed(range(depth)):
        dx, layer_grads[l], done_ffn, done_mix = _layer_bwd(
            dx, saved[l], params[l], pending, exchange=True, into_ffn=([native[k] for k in FFN], l),
            into_mix=([native[k] for k in MIX], l + 1))
        native.update(zip(FFN, done_ffn))
        if pending is not None:
            native.update(zip(MIX, done_mix))
        pending = [_slabs(layer_grads[l][k]) for k in MIX]

    delta, new_m, new_v = {}, {}, {}
    sums = chip_sums(pending, scatter_d2d(pending).run("scatter_d2d"))
    as_native = lambda t, k: jnp.swapaxes(t[k], 1, 2) if k in COL_SHARDED else t[k]
    outs, (got,) = adamw_many([as_native(w, k) for k in FFN], [native[k] for k in FFN],
                              [as_native(m, k) for k in FFN], [as_native(v, k) for k in FFN], [scatter_ici(sums)])
    for dst, arrs in zip((delta, new_m, new_v), outs):
        dst.update({k: as_native({k: a}, k) for k, a in zip(FFN, arrs)})
    native.update(zip(MIX, device_sums(sums, got, [native[k] for k in MIX], 0)))
    grads = {k: jnp.swapaxes(native[k], 1, 2) if k in COL_SHARDED else native[k] for k in MIX + FFN}
    layered = [k for k in SMALL if k != "final_norm_w"]
    small_shapes = [(1, LANE)] + [(depth,) + layer_grads[0][k].shape for k in layered] + [dfw[0].shape]
    packed = _pack([loss] + [[layer_grads[l][k] for l in range(depth)] for k in layered] + [dfw[0]])
    summed = _unpack(all_reduce_small(packed), small_shapes)
    loss = summed[0][0, 0]
    grads.update(zip(layered + ["final_norm_w"], summed[1:]))
    grads["conv_w"] = lax.dynamic_slice_in_dim(grads["conv_w"], dev * n_cw, n_cw, axis=2)

    for k in MIX:
        delta[k], new_m[k], new_v[k] = adamw(w[k], grads[k], m[k], v[k])
    two_d = lambda a: a.reshape(1, -1) if a.ndim == 1 else a
    outs = adamw_small(*[[two_d(t[k]) for k in SMALL] for t in (w, grads, m, v)])
    for dst, arrs in zip((delta, new_m, new_v), outs):
        dst.update({k: a.reshape(w[k].shape) for k, a in zip(SMALL, arrs)})
    return (loss, dx[None], *[grads[k] for k in WEIGHTS], *[delta[k] for k in WEIGHTS],
            *[new_m[k] for k in WEIGHTS], *[new_v[k] for k in WEIGHTS])
```

```python
import functools

import jax
import jax.numpy as jnp
from jax import lax
from jax.experimental import pallas as pl
from jax.experimental.pallas import tpu as pltpu

F32 = jnp.float32
BF16 = jnp.bfloat16
HIGHEST = lax.Precision.HIGHEST
MESH = pl.DeviceIdType.MESH

EPS = 1e-6
D_MODEL = 1024
SSD_WIDTH = 512
SSD_HEADS = 8
HEAD_DIM = 64
D_STATE = 128
CONV_WIDTH = 4
CONV_DIM = 1024
SB_WIDTH = 256
POOL_WIDTH = 256
POOL_WINDOWS = (2, 4, 8, 16)
D_IN_PROJ = 2568
D_FF = 2816
N_DEV = 8
SEG = (512, 1024, 768, 256, 128)
D_IN_PAD = sum(SEG)
DT_LO, DT_HI = 1536, 1544

LANE = 128
BLK = 128
ROW_TILE = 256
VMEM_LIMIT = 56 * 2**20

ADAM_LR, ADAM_B1, ADAM_B2, ADAM_EPS, ADAM_WD, ADAM_STEP = 0.001, 0.9, 0.999, 1e-08, 0.01, 10


def _params(n_axes=1, vmem=None):
    return pltpu.CompilerParams(dimension_semantics=("arbitrary",) * n_axes, vmem_limit_bytes=vmem)


def _dot(a, b, dims, exact=False):
    if exact:
        return lax.dot_general(a.astype(F32), b.astype(F32), (dims, ((), ())), precision=HIGHEST,
                               preferred_element_type=F32)
    return lax.dot_general(a.astype(BF16), b.astype(BF16), (dims, ((), ())), preferred_element_type=F32)


def dot_nn(a, b, exact=False):
    return _dot(a, b, ((1,), (0,)), exact)


def dot_nt(a, b, exact=False):
    return _dot(a, b, ((1,), (1,)), exact)


def dot_tn(a, b, exact=False):
    return _dot(a, b, ((0,), (0,)), exact)


def _iota(shape, axis):
    return lax.broadcasted_iota(jnp.int32, shape, axis)


def _lane_col(x, h):
    return jnp.sum(jnp.where(_iota(x.shape, 1) == h, x, 0.0), axis=1, keepdims=True)


def _sub_row(x, h):
    return jnp.sum(jnp.where(_iota(x.shape, 0) == h, x, 0.0), axis=0, keepdims=True)


def _sigmoid(x):
    return 1.0 / (1.0 + jnp.exp(-x))


def _rms_fwd(x, w):
    r = lax.rsqrt(jnp.mean(x * x, axis=-1, keepdims=True) + EPS)
    return x * r * w


def _rms_bwd(x, w, dy):
    r = lax.rsqrt(jnp.mean(x * x, axis=-1, keepdims=True) + EPS)
    xh = x * r
    dxh = dy * w
    dx = r * (dxh - xh * jnp.mean(dxh * xh, axis=-1, keepdims=True))
    return dx, jnp.sum(dy * xh, axis=0, keepdims=True)


def _acc(ref, first, val):
    @pl.when(first)
    def _():
        ref[...] = val

    @pl.when(jnp.logical_not(first))
    def _():
        ref[...] += val


def _row_spec(tm, n):
    return pl.BlockSpec((tm, n), lambda i: (i, 0))


def _full_spec(shape):
    return pl.BlockSpec(shape, lambda *_: (0,) * len(shape))


def inproj_fwd(x, nw, w, sides=()):
    s, d = x.shape
    tm = min(ROW_TILE, s)
    seg_dtypes = (F32, F32, BF16, F32, F32)

    def body(x_ref, nw_ref, w_ref, z_ref, xbc_ref, qkv_ref, p_ref, dt_ref, h_ref):
        h = _rms_fwd(x_ref[...], nw_ref[...]).astype(BF16)
        h_ref[...] = h
        lo = 0
        for ref, n in zip((z_ref, xbc_ref, qkv_ref, p_ref, dt_ref), SEG):
            ref[...] = dot_nt(h, w_ref[lo:lo + n, :]).astype(ref.dtype)
            lo += n

    return hosted_call(
        body, sides, name="inproj_fwd", grid=(s // tm,),
        in_specs=[_row_spec(tm, d), _full_spec((1, d)), _full_spec(w.shape)],
        out_specs=[_row_spec(tm, n) for n in SEG] + [_row_spec(tm, d)],
        out_shape=[jax.ShapeDtypeStruct((s, n), t) for n, t in zip(SEG, seg_dtypes)]
        + [jax.ShapeDtypeStruct((s, d), BF16)],
        scratch_shapes=[], compiler_params=_params(1, VMEM_LIMIT), operands=(x, nw, w))


def inproj_bwd(pieces, w, x, nw, dres):
    s, d = x.shape
    tm = min(ROW_TILE, s)
    n_p = len(pieces)
    widths = [p.shape[1] for p in pieces]

    def body(*refs):
        w_ref, x_ref, nw_ref, dres_ref, dx_ref, dnw_ref, dp_ref = refs[n_p:]
        dh = None
        lo = 0
        for ref, n in zip(refs[:n_p], widths):
            piece = ref[...].astype(BF16)
            dp_ref[:, lo:lo + n] = piece
            part = dot_nn(piece, w_ref[lo:lo + n, :])
            dh = part if dh is None else dh + part
            lo += n
        dx, dnw = _rms_bwd(x_ref[...], nw_ref[...], dh)
        dx_ref[...] = dres_ref[...] + dx
        _acc(dnw_ref, pl.program_id(0) == 0, dnw)

    return pl.pallas_call(
        body, name="inproj_bwd", grid=(s // tm,),
        in_specs=[_row_spec(tm, n) for n in widths] + [_full_spec(w.shape), _row_spec(tm, d), _full_spec((1, d)),
                                                       _row_spec(tm, d)],
        out_specs=[_row_spec(tm, d), _full_spec((1, d)), _row_spec(tm, sum(widths))],
        out_shape=[jax.ShapeDtypeStruct((s, d), F32), jax.ShapeDtypeStruct((1, d), F32),
                   jax.ShapeDtypeStruct((s, sum(widths)), BF16)],
        compiler_params=_params(1, VMEM_LIMIT),
    )(*pieces, w, x, nw, dres)


def outproj_fwd(pieces, w, res, sides=()):
    s, d = res.shape
    tm = min(ROW_TILE, s)
    n_p = len(pieces)
    widths = [p.shape[1] for p in pieces]

    def body(*refs):
        w_ref, r_ref, o_ref, y_ref = refs[n_p:]
        acc = r_ref[...]
        lo = 0
        for ref, n in zip(refs[:n_p], widths):
            piece = ref[...].astype(BF16)
            y_ref[:, lo:lo + n] = piece
            acc = acc + dot_nn(piece, w_ref[lo:lo + n, :])
            lo += n
        o_ref[...] = acc

    return hosted_call(
        body, sides, name="outproj_fwd", grid=(s // tm,),
        in_specs=[_row_spec(tm, n) for n in widths] + [_full_spec(w.shape), _row_spec(tm, d)],
        out_specs=[_row_spec(tm, d), _row_spec(tm, sum(widths))],
        out_shape=[jax.ShapeDtypeStruct((s, d), F32), jax.ShapeDtypeStruct((s, sum(widths)), BF16)],
        scratch_shapes=[], compiler_params=_params(1, VMEM_LIMIT), operands=(*pieces, w, res))


def outproj_bwd(dx, w, sides=()):
    s, d = dx.shape
    tm = min(ROW_TILE, s)

    def body(dx_ref, w_ref, o_ref):
        o_ref[...] = dot_nt(dx_ref[...], w_ref[...])

    return hosted_call(
        body, sides, name="outproj_bwd", grid=(s // tm,),
        in_specs=[_row_spec(tm, d), _full_spec(w.shape)],
        out_specs=[_row_spec(tm, w.shape[0])], out_shape=[jax.ShapeDtypeStruct((s, w.shape[0]), F32)],
        scratch_shapes=[], compiler_params=_params(1, VMEM_LIMIT), operands=(dx, w))


def ffn_fwd(x, nw, wg, wu, wd, sides=()):
    s, d = x.shape
    f = wg.shape[0]
    tm = min(ROW_TILE, s)

    def body(x_ref, nw_ref, wg_ref, wu_ref, wd_ref, o_ref, g_ref, u_ref):
        xv = x_ref[...]
        h = _rms_fwd(xv, nw_ref[...]).astype(BF16)
        g = dot_nt(h, wg_ref[...])
        u = dot_nt(h, wu_ref[...])
        g_ref[...] = g.astype(BF16)
        u_ref[...] = u.astype(BF16)
        o_ref[...] = xv + dot_nn(g * _sigmoid(g) * u, wd_ref[...])

    return hosted_call(
        body, sides, name="ffn_fwd", grid=(s // tm,),
        in_specs=[_row_spec(tm, d), _full_spec((1, d)), _full_spec(wg.shape), _full_spec(wu.shape),
                  _full_spec(wd.shape)],
        out_specs=[_row_spec(tm, d), _row_spec(tm, f), _row_spec(tm, f)],
        out_shape=[jax.ShapeDtypeStruct((s, d), F32), jax.ShapeDtypeStruct((s, f), BF16),
                   jax.ShapeDtypeStruct((s, f), BF16)],
        scratch_shapes=[], compiler_params=_params(1, VMEM_LIMIT), operands=(x, nw, wg, wu, wd))


def ffn_bwd(dxo, x, g, u, nw, wg, wu, wd, sides=()):
    s, d = x.shape
    f = wg.shape[0]
    tm = min(ROW_TILE, s)

    def body(dxo_ref, x_ref, g_ref, u_ref, nw_ref, wg_ref, wu_ref, wd_ref, dx_ref, dnw_ref, a_ref, dg_ref,
             du_ref, h_ref):
        dxo_v = dxo_ref[...]
        xv = x_ref[...]
        da = dot_nt(dxo_v, wd_ref[...])
        gv = g_ref[...].astype(F32)
        uv = u_ref[...].astype(F32)
        sg = _sigmoid(gv)
        sl = gv * sg
        a_ref[...] = (sl * uv).astype(BF16)
        dg = (da * uv * (sg * (1.0 + gv * (1.0 - sg)))).astype(BF16)
        du = (da * sl).astype(BF16)
        dg_ref[...] = dg
        du_ref[...] = du
        dh = dot_nn(dg, wg_ref[...]) + dot_nn(du, wu_ref[...])
        h_ref[...] = _rms_fwd(xv, nw_ref[...]).astype(BF16)
        dx, dnw = _rms_bwd(xv, nw_ref[...], dh)
        dx_ref[...] = dxo_v + dx
        _acc(dnw_ref, pl.program_id(0) == 0, dnw)

    return hosted_call(
        body, sides, name="ffn_bwd", grid=(s // tm,),
        in_specs=[_row_spec(tm, d), _row_spec(tm, d), _row_spec(tm, f), _row_spec(tm, f), _full_spec((1, d)),
                  _full_spec(wg.shape), _full_spec(wu.shape), _full_spec(wd.shape)],
        out_specs=[_row_spec(tm, d), _full_spec((1, d)), _row_spec(tm, f), _row_spec(tm, f), _row_spec(tm, f),
                   _row_spec(tm, d)],
        out_shape=[jax.ShapeDtypeStruct((s, d), F32), jax.ShapeDtypeStruct((1, d), F32),
                   jax.ShapeDtypeStruct((s, f), BF16), jax.ShapeDtypeStruct((s, f), BF16),
                   jax.ShapeDtypeStruct((s, f), BF16), jax.ShapeDtypeStruct((s, d), BF16)],
        scratch_shapes=[], compiler_params=_params(1, VMEM_LIMIT), operands=(dxo, x, g, u, nw, wg, wu, wd))


def _tile(n, cap=512):
    best = LANE
    for t in range(LANE, cap + 1, LANE):
        if n % t == 0:
            best = t
    return best


def mm_tn(a, b):
    many = isinstance(a, (list, tuple))
    a_list = list(a) if many else [a]
    n_a = len(a_list)
    s, k = a_list[0].shape
    n = b.shape[1]
    tk = _tile(k)

    def body(*refs):
        b_val = refs[n_a][...]
        for a_ref, o_ref in zip(refs[:n_a], refs[n_a + 1:]):
            o_ref[...] = dot_nn(a_ref[...].astype(BF16).T, b_val).astype(BF16)

    outs = pl.pallas_call(
        body, name="mm_tn", grid=(k // tk,),
        in_specs=[pl.BlockSpec((s, tk), lambda i: (0, i))] * n_a + [_full_spec((s, n))],
        out_specs=[pl.BlockSpec((tk, n), lambda i: (i, 0))] * n_a,
        out_shape=[jax.ShapeDtypeStruct((k, n), BF16)] * n_a, compiler_params=_params(1, VMEM_LIMIT),
    )(*a_list, b)
    return outs if many else outs[0]


def head_loss(x, fw, tgt):
    s, d = x.shape
    tm = min(ROW_TILE, s)

    def body(x_ref, fw_ref, t_ref, loss_ref, dx_ref, dfw_ref):
        xv = x_ref[...]
        err = _rms_fwd(xv, fw_ref[...]) - t_ref[...]
        part = jnp.zeros((1, LANE), F32) + 0.5 * jnp.sum(err * err) / d
        dx, dfw = _rms_bwd(xv, fw_ref[...], err / d)
        dx_ref[...] = dx
        first = pl.program_id(0) == 0
        _acc(loss_ref, first, part)
        _acc(dfw_ref, first, dfw)

    return pl.pallas_call(
        body, name="head_loss", grid=(s // tm,),
        in_specs=[_row_spec(tm, d), _full_spec((1, d)), _row_spec(tm, d)],
        out_specs=[_full_spec((1, LANE)), _row_spec(tm, d), _full_spec((1, d))],
        out_shape=[jax.ShapeDtypeStruct((1, LANE), F32), jax.ShapeDtypeStruct((s, d), F32),
                   jax.ShapeDtypeStruct((1, d), F32)],
        compiler_params=_params(1),
    )(x, fw, tgt)


HALO = 8


def _conv_pre(ext, cw_ref, cb_ref):
    shifted = [pltpu.roll(ext, CONV_WIDTH - 1 - i, 0)[HALO:] if i < CONV_WIDTH - 1 else ext[HALO:]
               for i in range(CONV_WIDTH)]
    acc = cb_ref[...] + sum(cw_ref[i:i + 1, :] * shifted[i] for i in range(CONV_WIDTH))
    return acc, shifted


def _halo_spec(n, block_of_step):
    return pl.BlockSpec((HALO, n), lambda i: (jnp.maximum(block_of_step(i) * (BLK // HALO) - 1, 0), 0))


def conv_fwd(xbc, cw, cb, sides=()):
    s, n = xbc.shape

    def body(cur_ref, prev_ref, cw_ref, cb_ref, o_ref):
        prev = jnp.where(pl.program_id(0) > 0, prev_ref[...], 0.0)
        acc, _ = _conv_pre(jnp.concatenate([prev, cur_ref[...]], axis=0), cw_ref, cb_ref)
        o_ref[...] = acc * _sigmoid(acc)

    return hosted_call(
        body, sides, name="conv_fwd", grid=(s // BLK,),
        in_specs=[pl.BlockSpec((BLK, n), lambda c: (c, 0)), _halo_spec(n, lambda c: c),
                  _full_spec(cw.shape), _full_spec((1, n))],
        out_specs=[pl.BlockSpec((BLK, n), lambda c: (c, 0))], out_shape=[jax.ShapeDtypeStruct((s, n), F32)],
        scratch_shapes=[], compiler_params=_params(1), operands=(xbc, xbc, cw, cb))


def conv_bwd(xbc, cw, cb, du, sides=()):
    s, n = xbc.shape
    nb = s // BLK

    def body(cur_ref, prev_ref, cw_ref, cb_ref, du_ref, dx_ref, dcw_ref, dcb_ref, nxt_ref):
        i = pl.program_id(0)
        c = nb - 1 - i
        prev = jnp.where(c > 0, prev_ref[...], 0.0)
        acc, shifted = _conv_pre(jnp.concatenate([prev, cur_ref[...]], axis=0), cw_ref, cb_ref)
        sg = _sigmoid(acc)
        dacc = du_ref[...] * (sg * (1.0 + acc * (1.0 - sg)))

        @pl.when(i == 0)
        def _():
            nxt_ref[...] = jnp.zeros_like(nxt_ref)
            dcw_ref[...] = jnp.zeros_like(dcw_ref)
            dcb_ref[...] = jnp.zeros_like(dcb_ref)

        dcb_ref[...] += jnp.sum(dacc, axis=0, keepdims=True)
        for t in range(CONV_WIDTH):
            dcw_ref[t:t + 1, :] += jnp.sum(dacc * shifted[t], axis=0, keepdims=True)
        ext = jnp.concatenate([dacc, nxt_ref[...]], axis=0)
        dx = cw_ref[CONV_WIDTH - 1:CONV_WIDTH, :] * dacc
        for t in range(CONV_WIDTH - 1):
            dx += cw_ref[t:t + 1, :] * pltpu.roll(ext, BLK + HALO - (CONV_WIDTH - 1 - t), 0)[:BLK]
        dx_ref[...] = dx.astype(dx_ref.dtype)
        nxt_ref[...] = dacc[:HALO]

    rev = lambda i: (nb - 1 - i, 0)
    return hosted_call(
        body, sides, name="conv_bwd", grid=(nb,),
        in_specs=[pl.BlockSpec((BLK, n), rev), _halo_spec(n, lambda i: nb - 1 - i),
                  _full_spec(cw.shape), _full_spec((1, n)), pl.BlockSpec((BLK, n), rev)],
        out_specs=[pl.BlockSpec((BLK, n), rev), _full_spec((8, n)), _full_spec((1, n))],
        out_shape=[jax.ShapeDtypeStruct((s, n), BF16), jax.ShapeDtypeStruct((8, n), F32),
                   jax.ShapeDtypeStruct((1, n), F32)],
        scratch_shapes=[pltpu.VMEM((HALO, n), F32)],
        compiler_params=_params(1), operands=(xbc, xbc, cw, cb, du))


N_PAIR = SSD_HEADS // 2
B_LO = SSD_WIDTH
C_LO = SSD_WIDTH + 2 * D_STATE


def _softplus(x):
    return jnp.maximum(x, 0.0) + jnp.log(1.0 + jnp.exp(-jnp.abs(x)))


def _ssd_chunk(u_ref, dt_ref, dtb_ref, alog_ref):
    shape = (BLK, BLK)
    tri = _iota(shape, 1) <= _iota(shape, 0)
    pre = dt_ref[...] + dtb_ref[...]
    dt = _softplus(pre)
    a = -jnp.exp(alog_ref[...])
    acum = dot_nn(tri.astype(F32), dt * a, exact=True)
    acum_t = acum.T
    last = _sub_row(acum, BLK - 1)
    heads = []
    for h in range(SSD_HEADS):
        col = _lane_col(acum, h)
        seg = jnp.where(tri, col - _sub_row(acum_t, h), -1e30)
        heads.append(dict(col=col, dm=jnp.exp(seg), dt=_lane_col(dt, h), last=_lane_col(last, h)))
    return tri, pre, dt, a, heads


def _pair_mix(lo_mask, v0, v1):
    return jnp.where(lo_mask, v0, v1)


def ssd_fwd(u, z, dtr, dtb, alog, dsk, nw, sides=()):
    s = u.shape[0]
    nc = s // BLK

    def body(u_ref, z_ref, dt_ref, dtb_ref, alog_ref, dsk_ref, nw_ref, y_ref, st_ref, s_ref):
        @pl.when(pl.program_id(0) == 0)
        def _():
            s_ref[...] = jnp.zeros_like(s_ref)

        _, _, _, _, heads = _ssd_chunk(u_ref, dt_ref, dtb_ref, alog_ref)
        lo_lane = _iota((BLK, LANE), 1) < HEAD_DIM
        lo_sub = _iota((BLK, LANE), 0) < HEAD_DIM
        ys = []
        for p in range(N_PAIR):
            g = p // 2
            h0, h1 = heads[2 * p], heads[2 * p + 1]
            bg = u_ref[:, B_LO + g * D_STATE:B_LO + (g + 1) * D_STATE]
            cg = u_ref[:, C_LO + g * D_STATE:C_LO + (g + 1) * D_STATE]
            xs = u_ref[:, p * LANE:(p + 1) * LANE]
            xp = xs * _pair_mix(lo_lane, h0["dt"], h1["dt"])
            gm = dot_nt(cg, bg)
            yd = _pair_mix(lo_lane, dot_nn(gm * h0["dm"], xp), dot_nn(gm * h1["dm"], xp))
            sp = s_ref[p]
            st_ref[0, p] = sp
            yo = _pair_mix(lo_lane, jnp.exp(h0["col"]), jnp.exp(h1["col"])) * dot_nt(cg, sp)
            dskp = _pair_mix(lo_lane, _lane_col(dsk_ref[...], 2 * p), _lane_col(dsk_ref[...], 2 * p + 1))
            ys.append(yd + yo + xs * dskp)
            wp = _pair_mix(lo_lane, jnp.exp(h0["last"] - h0["col"]), jnp.exp(h1["last"] - h1["col"]))
            el = _pair_mix(lo_sub, jnp.exp(h0["last"]), jnp.exp(h1["last"]))
            s_ref[p] = el * sp + dot_tn(wp * xp, bg)
        y = jnp.concatenate(ys, axis=1)
        zv = z_ref[...]
        y_ref[...] = _rms_fwd(y * zv * _sigmoid(zv), nw_ref[...])

    vec = _full_spec((1, LANE))
    return hosted_call(
        body, sides, name="ssd_fwd", grid=(nc,),
        in_specs=[_row_spec(BLK, CONV_DIM), _row_spec(BLK, SSD_WIDTH), _row_spec(BLK, LANE), vec, vec, vec,
                  _full_spec((1, SSD_WIDTH))],
        out_specs=[_row_spec(BLK, SSD_WIDTH), pl.BlockSpec((1, N_PAIR, LANE, D_STATE), lambda c: (c, 0, 0, 0))],
        out_shape=[jax.ShapeDtypeStruct((s, SSD_WIDTH), F32), jax.ShapeDtypeStruct((nc, N_PAIR, LANE, D_STATE), F32)],
        scratch_shapes=[pltpu.VMEM((N_PAIR, LANE, D_STATE), F32)],
        compiler_params=_params(1), operands=(u, z, dtr, dtb, alog, dsk, nw))


def ssd_bwd(u, z, dtr, st, dyo, dtb, alog, dsk, nw, sides=()):
    s = u.shape[0]
    nc = s // BLK

    def body(u_ref, z_ref, dt_ref, st_ref, dyo_ref, dtb_ref, alog_ref, dsk_ref, nw_ref,
             du_ref, dz_ref, ddt_ref, ddtb_ref, dalog_ref, ddsk_ref, dnw_ref, ds_ref):
        first = pl.program_id(0) == 0

        @pl.when(first)
        def _():
            ds_ref[...] = jnp.zeros_like(ds_ref)

        tri, pre, dt, a, heads = _ssd_chunk(u_ref, dt_ref, dtb_ref, alog_ref)
        shape = (BLK, LANE)
        lane = _iota(shape, 1)
        lo_lane = lane < HEAD_DIM
        lo_sub = _iota(shape, 0) < HEAD_DIM
        pairs = []
        ys = []
        for p in range(N_PAIR):
            g = p // 2
            h0, h1 = heads[2 * p], heads[2 * p + 1]
            bg = u_ref[:, B_LO + g * D_STATE:B_LO + (g + 1) * D_STATE]
            cg = u_ref[:, C_LO + g * D_STATE:C_LO + (g + 1) * D_STATE]
            xs = u_ref[:, p * LANE:(p + 1) * LANE]
            dtp = _pair_mix(lo_lane, h0["dt"], h1["dt"])
            xp = xs * dtp
            gm = dot_nt(cg, bg)
            m0, m1 = gm * h0["dm"], gm * h1["dm"]
            sp = st_ref[0, p]
            eap = _pair_mix(lo_lane, jnp.exp(h0["col"]), jnp.exp(h1["col"]))
            yo = eap * dot_nt(cg, sp)
            dskp = _pair_mix(lo_lane, _lane_col(dsk_ref[...], 2 * p), _lane_col(dsk_ref[...], 2 * p + 1))
            ys.append(_pair_mix(lo_lane, dot_nn(m0, xp), dot_nn(m1, xp)) + yo + xs * dskp)
            pairs.append(dict(bg=bg, cg=cg, xs=xs, dtp=dtp, xp=xp, gm=gm, m=(m0, m1), sp=sp, eap=eap, yo=yo, dskp=dskp))
        y = jnp.concatenate(ys, axis=1)
        zv = z_ref[...]
        sz = _sigmoid(zv)
        gate = zv * sz
        dyg, dnw = _rms_bwd(y * gate, nw_ref[...], dyo_ref[...])
        _acc(dnw_ref, first, dnw)
        dy = dyg * gate
        dz_ref[...] = (dyg * y * (sz * (1.0 + zv * (1.0 - sz)))).astype(dz_ref.dtype)

        zeros = jnp.zeros(shape, F32)
        dacum_col = zeros
        dacum_row = zeros
        ddt = zeros
        ddsk = jnp.zeros((1, LANE), F32)
        dlast = jnp.zeros((1, LANE), F32)
        head_row = _iota((1, LANE), 1)
        sub = _iota(shape, 0)
        db = [zeros, zeros]
        dc = [zeros, zeros]
        for p in range(N_PAIR):
            g = p // 2
            q = pairs[p]
            dyp = dy[:, p * LANE:(p + 1) * LANE]
            dsn = ds_ref[p]
            t = dyp * q["xs"]
            dxs = dyp * q["dskp"]
            dcs = dyp * q["eap"]
            dc[g] = dc[g] + dot_nn(dcs, q["sp"])
            dsp = dot_tn(dcs, q["cg"])
            dea = dyp * q["yo"]
            elp = _pair_mix(lo_sub, jnp.exp(heads[2 * p]["last"]), jnp.exp(heads[2 * p + 1]["last"]))
            dsp = dsp + elp * dsn
            dels = dsn * q["sp"] * elp
            wp = _pair_mix(lo_lane, jnp.exp(heads[2 * p]["last"] - heads[2 * p]["col"]),
                           jnp.exp(heads[2 * p + 1]["last"] - heads[2 * p + 1]["col"]))
            dv = dot_nt(q["bg"], dsn)
            db[g] = db[g] + dot_nn(wp * q["xp"], dsn)
            dxp = dv * wp
            dwv = dv * q["xp"] * wp
            dgm = zeros
            for k in range(2):
                h = 2 * p + k
                mine = lo_lane if k == 0 else jnp.logical_not(lo_lane)
                mine_sub = lo_sub if k == 0 else jnp.logical_not(lo_sub)
                dyh = jnp.where(mine, dyp, 0.0)
                dm = dot_nt(dyh, q["xp"])
                dxp = dxp + dot_tn(q["m"][k], dyh)
                dgm = dgm + dm * heads[h]["dm"]
                e = dm * q["m"][k]
                onehot = lane == h
                dw_col = jnp.sum(jnp.where(mine, dwv, 0.0), axis=1, keepdims=True)
                col = (jnp.sum(e, axis=1, keepdims=True) + jnp.sum(jnp.where(mine, dea, 0.0), axis=1, keepdims=True)
                       - dw_col)
                dacum_col = dacum_col + jnp.where(onehot, col, 0.0)
                dacum_row = dacum_row - jnp.where(sub == h, jnp.sum(e, axis=0, keepdims=True), 0.0)
                dl = jnp.sum(dw_col) + jnp.sum(jnp.where(mine_sub, dels, 0.0))
                dlast = dlast + jnp.where(head_row == h, dl, 0.0)
                ddsk = ddsk + jnp.where(head_row == h, jnp.sum(jnp.where(mine, t, 0.0)), 0.0)
            dc[g] = dc[g] + dot_nn(dgm, q["bg"])
            db[g] = db[g] + dot_tn(dgm, q["cg"])
            dxs = dxs + dxp * q["dtp"]
            tt = dxp * q["xs"]
            for k in range(2):
                mine = lo_lane if k == 0 else jnp.logical_not(lo_lane)
                ddt = ddt + jnp.where(lane == 2 * p + k, jnp.sum(jnp.where(mine, tt, 0.0), axis=1, keepdims=True), 0.0)
            du_ref[:, p * LANE:(p + 1) * LANE] = dxs
            ds_ref[p] = dsp
        for g in range(2):
            du_ref[:, B_LO + g * D_STATE:B_LO + (g + 1) * D_STATE] = db[g]
            du_ref[:, C_LO + g * D_STATE:C_LO + (g + 1) * D_STATE] = dc[g]
        dacum = dacum_col + dacum_row.T + jnp.where(sub == BLK - 1, dlast, 0.0)
        dda = dot_tn(tri.astype(F32), dacum, exact=True)
        ddt = ddt + dda * a
        _acc(dalog_ref, first, jnp.sum(dda * dt, axis=0, keepdims=True) * a)
        dpre = ddt * _sigmoid(pre)
        ddt_ref[...] = dpre.astype(ddt_ref.dtype)
        _acc(ddtb_ref, first, jnp.sum(dpre, axis=0, keepdims=True))
        _acc(ddsk_ref, first, ddsk)

    rev = lambda i: (nc - 1 - i, 0)
    vec = _full_spec((1, LANE))
    rows = lambda n: pl.BlockSpec((BLK, n), rev)
    return hosted_call(
        body, sides, name="ssd_bwd", grid=(nc,),
        in_specs=[rows(CONV_DIM), rows(SSD_WIDTH), rows(LANE),
                  pl.BlockSpec((1, N_PAIR, LANE, D_STATE), lambda i: (nc - 1 - i, 0, 0, 0)), rows(SSD_WIDTH),
                  vec, vec, vec, _full_spec((1, SSD_WIDTH))],
        out_specs=[rows(CONV_DIM), rows(SSD_WIDTH), rows(LANE), vec, vec, vec, _full_spec((1, SSD_WIDTH))],
        out_shape=[jax.ShapeDtypeStruct((s, CONV_DIM), F32), jax.ShapeDtypeStruct((s, SSD_WIDTH), BF16),
                   jax.ShapeDtypeStruct((s, LANE), BF16)] + [jax.ShapeDtypeStruct((1, LANE), F32)] * 3
        + [jax.ShapeDtypeStruct((1, SSD_WIDTH), F32)],
        scratch_shapes=[pltpu.VMEM((N_PAIR, LANE, D_STATE), F32)],
        compiler_params=_params(1), operands=(u, z, dtr, st, dyo, dtb, alog, dsk, nw))


SB_PAIRS = SB_WIDTH // LANE
SB_SCALE = HEAD_DIM ** -0.5


SB_TQ = 128


def _sb_tq(s):
    return min(SB_TQ, s)


def _sb_stack(x):
    lo_lane = _iota(x.shape, 1) < HEAD_DIM
    return jnp.concatenate([jnp.where(lo_lane, x, 0.0), jnp.where(lo_lane, 0.0, x)], axis=0)


def _sb_unstack(x2):
    tq = x2.shape[0] // 2
    lo_lane = _iota((tq, LANE), 1) < HEAD_DIM
    return jnp.where(lo_lane, x2[:tq], x2[tq:])


def _sb_logits(q2, kj, row0, col0, masked):
    shape = (q2.shape[0], BLK)
    tq = shape[0] // 2
    z = dot_nt(q2, kj)
    t = jnp.log(1.0 + jnp.exp(-jnp.abs(z)))
    ls = jnp.minimum(z, 0.0) - t
    lk = jnp.minimum(-z, 0.0) - t
    if not masked:
        return None, ls, lk
    row = _iota(shape, 0)
    valid = (col0 + _iota(shape, 1)) < (row0 + jnp.where(row < tq, row, row - tq))
    return valid, ls, jnp.where(valid, lk, 0.0)


def _sb_where(valid, x):
    return x if valid is None else jnp.where(valid, x, 0.0)


def _sb_win(x2, lo):
    if lo == 0:
        return x2
    tq = x2.shape[0] // 2
    return jnp.concatenate([x2[lo:tq], x2[tq + lo:]], axis=0)


def _sb_unwin(x2, xw, lo):
    if lo == 0:
        return xw
    tq = x2.shape[0] // 2
    return jnp.concatenate([x2[:lo], xw[:tq - lo], x2[tq:tq + lo], xw[tq - lo:]], axis=0)


def _sb_add(ref, val, lo):
    if lo == 0:
        ref[...] += val
    else:
        tq = ref.shape[0] // 2
        ref[lo:tq, :] += val[:tq - lo]
        ref[tq + lo:, :] += val[tq - lo:]


def _sums(x, mask2, parts):
    acc = None
    rest = x
    for _ in range(parts):
        term = rest.astype(BF16)
        rest = rest - term.astype(F32)
        d = lax.dot_general(term, mask2, (((1,), (0,)), ((), ())), preferred_element_type=F32)
        acc = d if acc is None else acc + d
    return acc[:, :BLK], acc[:, BLK:]


def _mask2(cond):
    return jnp.concatenate([cond.astype(BF16), jnp.ones(cond.shape, BF16)], axis=1)


def _sb_specs(s):
    tq = _sb_tq(s)
    qspec = pl.BlockSpec((tq, LANE), lambda p, i: (i, p))
    kspec = pl.BlockSpec((s, LANE), lambda p, i: (0, SB_PAIRS + p))
    vspec = pl.BlockSpec((s, LANE), lambda p, i: (0, 2 * SB_PAIRS + p))
    return qspec, kspec, vspec


SB_FLOOR = -104.0


def sb_fwd(qkv, sides=()):
    s = qkv.shape[0]
    tq = _sb_tq(s)
    kpq = tq // BLK

    def body(q_ref, k_ref, v_ref, o_ref, t_ref, n_ref, acc_ref):
        qi = pl.program_id(1)
        q2 = _sb_stack(q_ref[...] * SB_SCALE).astype(BF16)
        later = _mask2(_iota((BLK, BLK), 0) > _iota((BLK, BLK), 1))
        acc_ref[...] = jnp.zeros_like(acc_ref)

        def step(j, r, masked, lo=0):
            rows = pl.ds(pl.multiple_of(j * BLK, BLK), BLK)
            rw = _sb_win(r, lo)
            valid, ls, lk = _sb_logits(_sb_win(q2, lo), k_ref[rows, :], qi * tq + lo, j * BLK, masked)
            after, total = _sums(lk, later, 2)
            w = _sb_where(valid, jnp.exp(ls + rw + after))
            _sb_add(acc_ref, dot_nn(w, v_ref[rows, :]), lo)
            return _sb_unwin(r, rw + total, lo)

        r = jnp.zeros((2 * tq, LANE), F32)
        for d in reversed(range(kpq)):
            r = step(kpq * qi + d, r, True, d * BLK)

        def tile(g, r):
            for d in reversed(range(kpq)):
                r = step(kpq * (qi - 1 - g) + d, r, False)
            return r

        n, r = lax.while_loop(lambda c: jnp.logical_and(c[0] < qi, jnp.max(c[1]) > SB_FLOOR),
                              lambda c: (c[0] + 1, tile(c[0], c[1])), (jnp.int32(0), r))
        o_ref[...] = _sb_unstack(acc_ref[...])
        t_ref[...] = jnp.concatenate([r[:tq], r[tq:]], axis=1)
        n_ref[...] = jnp.zeros(n_ref.shape, F32) + n.astype(F32)

    return hosted_call(
        body, sides, name="sb_fwd", grid=(SB_PAIRS, s // tq),
        in_specs=list(_sb_specs(s)),
        out_specs=[pl.BlockSpec((tq, LANE), lambda p, i: (i, p)), pl.BlockSpec((tq, 2 * LANE), lambda p, i: (i, p)),
                   pl.BlockSpec((None, None, 8, LANE), lambda p, i: (p, i, 0, 0))],
        out_shape=[jax.ShapeDtypeStruct((s, SB_WIDTH), F32), jax.ShapeDtypeStruct((s, 2 * SB_WIDTH), F32),
                   jax.ShapeDtypeStruct((SB_PAIRS, s // tq, 8, LANE), F32)],
        scratch_shapes=[pltpu.VMEM((2 * tq, LANE), F32)],
        compiler_params=_params(2), operands=(qkv, qkv, qkv))


def sb_bwd(qkv, tot, swept, do, do_col=0, sides=()):
    s = qkv.shape[0]
    tq = _sb_tq(s)
    kpq = tq // BLK

    def body(q_ref, k_ref, v_ref, t_ref, n_ref, do_ref, dq_ref, dk_ref, dv_ref, acc_ref):
        qi = pl.program_id(1)
        n = jnp.clip(jnp.max(n_ref[...]).astype(jnp.int32), 0, qi)
        q2 = _sb_stack(q_ref[...] * SB_SCALE).astype(BF16)
        do2 = _sb_stack(do_ref[...]).astype(BF16)
        tot2 = jnp.concatenate([t_ref[:, :LANE], t_ref[:, LANE:]], axis=0)
        sq = (BLK, BLK)
        later = _mask2(_iota(sq, 0) > _iota(sq, 1))
        before = _mask2(_iota(sq, 0) < _iota(sq, 1))
        acc_ref[...] = jnp.zeros_like(acc_ref)

        @pl.when(qi == 0)
        def _():
            dk_ref[...] = jnp.zeros_like(dk_ref)
            dv_ref[...] = jnp.zeros_like(dv_ref)

        def step(j, carry, masked, lo=0):
            rc, fc = carry
            rows = pl.ds(pl.multiple_of(j * BLK, BLK), BLK)
            kj = k_ref[rows, :]
            vj = v_ref[rows, :]
            qw, dow, fw = _sb_win(q2, lo), _sb_win(do2, lo), _sb_win(fc, lo)
            valid, ls, lk = _sb_logits(qw, kj, qi * tq + lo, j * BLK, masked)
            after, total = _sums(lk, later, 2)
            rw = _sb_win(rc, lo) - total
            w = _sb_where(valid, jnp.exp(ls + rw + after))
            e = w * dot_nt(dow, vj)
            f_in, f_tot = _sums(e, before, 2)
            sg = jnp.exp(ls)
            dz = _sb_where(valid, e * (1.0 - sg) - (fw + f_in) * sg)
            _sb_add(acc_ref, dot_nn(dz, kj), lo)
            dk_ref[rows, :] += dot_tn(dz, qw)
            dv_ref[rows, :] += dot_tn(w, dow)
            return _sb_unwin(rc, rw, lo), _sb_unwin(fc, fw + f_tot, lo)

        def tile(g, carry):
            for d in range(kpq):
                carry = step(kpq * g + d, carry, False)
            return carry

        carry = lax.fori_loop(qi - n, qi, tile, (tot2, jnp.zeros((2 * tq, LANE), F32)))
        for d in range(kpq):
            carry = step(kpq * qi + d, carry, True, d * BLK)
        dq_ref[...] = (SB_SCALE * _sb_unstack(acc_ref[...])).astype(dq_ref.dtype)

    qspec, kspec, vspec = _sb_specs(s)
    blk = pl.BlockSpec((tq, LANE), lambda p, i: (i, p))
    acc = pl.BlockSpec((s, LANE), lambda p, i: (0, p))
    return hosted_call(
        body, sides, name="sb_bwd", grid=(SB_PAIRS, s // tq),
        in_specs=[qspec, kspec, vspec, pl.BlockSpec((tq, 2 * LANE), lambda p, i: (i, p)),
                  pl.BlockSpec((None, None, 8, LANE), lambda p, i: (p, i, 0, 0)),
                  pl.BlockSpec((tq, LANE), lambda p, i: (i, do_col + p))],
        out_specs=[blk, acc, acc],
        out_shape=[jax.ShapeDtypeStruct((s, SB_WIDTH), BF16)] + [jax.ShapeDtypeStruct((s, SB_WIDTH), F32)] * 2,
        scratch_shapes=[pltpu.VMEM((2 * tq, LANE), F32)],
        compiler_params=_params(2), operands=(qkv, qkv, qkv, tot, swept, do))


POOL_GROUP_DIM = POOL_WIDTH // len(POOL_WINDOWS)


assert all(w == 2 ** (i + 1) for i, w in enumerate(POOL_WINDOWS))


def _pool_inv(c):
    group = _iota((BLK, POOL_WIDTH), 1) // POOL_GROUP_DIM
    pos = c * BLK + _iota((BLK, POOL_WIDTH), 0)
    win = jnp.zeros((BLK, POOL_WIDTH), jnp.int32)
    for gi, wn in enumerate(POOL_WINDOWS):
        win = jnp.where(group == gi, wn, win)
    return 1.0 / jnp.minimum(pos + 1, win).astype(F32)


def _window_sums(ext, trailing):
    group = _iota(ext.shape, 1) // POOL_GROUP_DIM
    acc = ext
    out = None
    for gi in range(len(POOL_WINDOWS)):
        shift = 2 ** gi
        acc = acc + pltpu.roll(acc, shift if trailing else ext.shape[0] - shift, 0)
        out = acc if out is None else jnp.where(group == gi, acc, out)
    return out


def _pool_pooled(ext, cur, inv):
    return _window_sums(ext, True)[BLK:] * inv - cur


def pool_fwd(p, wblk, pb, ps):
    s, n = p.shape

    def body(cur_ref, prev_ref, w_ref, pb_ref, ps_ref, o_ref):
        c = pl.program_id(0)
        cur = cur_ref[...]
        prev = jnp.where(c > 0, prev_ref[...], 0.0)
        pooled = _pool_pooled(jnp.concatenate([prev, cur], axis=0), cur, _pool_inv(c))
        o_ref[...] = (dot_nn(pooled, w_ref[...]) + pb_ref[...]) * ps_ref[...]

    return pl.pallas_call(
        body, name="pool_fwd", grid=(s // BLK,),
        in_specs=[pl.BlockSpec((BLK, n), lambda c: (c, 0)), pl.BlockSpec((BLK, n), lambda c: (jnp.maximum(c - 1, 0), 0)),
                  _full_spec((n, n)), _full_spec((1, n)), _full_spec((1, n))],
        out_specs=pl.BlockSpec((BLK, n), lambda c: (c, 0)), out_shape=jax.ShapeDtypeStruct((s, n), F32),
        compiler_params=_params(1),
    )(p, p, wblk, pb, ps)


def pool_bwd(p, wblk, pb, ps, dout, do_col=0, sides=()):
    s, n = p.shape
    nb = s // BLK

    def body(cur_ref, prev_ref, w_ref, pb_ref, ps_ref, do_ref, dp_ref, dw_ref, dpb_ref, dps_ref, carry_ref):
        i = pl.program_id(0)
        c = nb - 1 - i
        first = i == 0
        cur = cur_ref[...]
        prev = jnp.where(c > 0, prev_ref[...], 0.0)
        inv = _pool_inv(c)
        pooled = _pool_pooled(jnp.concatenate([prev, cur], axis=0), cur, inv)
        mixed = dot_nn(pooled, w_ref[...]) + pb_ref[...]
        dov = do_ref[...]
        dmixed = dov * ps_ref[...]
        _acc(dps_ref, first, jnp.sum(dov * mixed, axis=0, keepdims=True))
        _acc(dpb_ref, first, jnp.sum(dmixed, axis=0, keepdims=True))
        _acc(dw_ref, first, dot_tn(pooled, dmixed))
        dpooled = dot_nt(dmixed, w_ref[...])
        dext = _window_sums(jnp.concatenate([jnp.zeros((BLK, n), F32), dpooled * inv], axis=0), False)

        @pl.when(first)
        def _():
            carry_ref[...] = jnp.zeros_like(carry_ref)

        dp_ref[...] = (dext[BLK:] - dpooled + carry_ref[...]).astype(dp_ref.dtype)
        carry_ref[...] = dext[:BLK]

    rev = lambda i: (nb - 1 - i, 0)
    return hosted_call(
        body, sides, name="pool_bwd", grid=(nb,),
        in_specs=[pl.BlockSpec((BLK, n), rev), pl.BlockSpec((BLK, n), lambda i: (jnp.maximum(nb - 2 - i, 0), 0)),
                  _full_spec((n, n)), _full_spec((1, n)), _full_spec((1, n)),
                  pl.BlockSpec((BLK, n), lambda i: (nb - 1 - i, do_col))],
        out_specs=[pl.BlockSpec((BLK, n), rev), _full_spec((n, n)), _full_spec((1, n)), _full_spec((1, n))],
        out_shape=[jax.ShapeDtypeStruct((s, n), BF16), jax.ShapeDtypeStruct((n, n), F32),
                   jax.ShapeDtypeStruct((1, n), F32), jax.ShapeDtypeStruct((1, n), F32)],
        scratch_shapes=[pltpu.VMEM((BLK, n), F32)],
        compiler_params=_params(1), operands=(p, p, wblk, pb, ps, dout))


def _row_tile(rows):
    if rows <= 512:
        return rows
    for t in (512, 256, 128, 64, 32, 16, 8):
        if rows % t == 0:
            return t
    return rows


def adamw(w, g, m, v):
    (d, nm, nv), _ = adamw_many([w], [g], [m], [v])
    return d[0], nm[0], nv[0]


def adamw_many(ws, gs, ms, vs, sides=()):
    k = len(ws)
    n, rows, cols = ws[0].shape
    tr = _row_tile(rows)
    while 7 * k * 2 * tr * cols * 4 > VMEM_LIMIT // 2 and tr % 16 == 0:
        tr //= 2

    def body(*refs):
        for i in range(k):
            outs = _adamw_math(*[refs[j * k + i][...] for j in range(4)])
            for j in range(3):
                refs[(4 + j) * k + i][...] = outs[j]

    spec = pl.BlockSpec((1, tr, cols), lambda i, j: (i, j, 0))
    outs, side_outs = hosted_call(
        body, sides, name="adamw", grid=(n, rows // tr), in_specs=[spec] * (4 * k), out_specs=[spec] * (3 * k),
        out_shape=[jax.ShapeDtypeStruct(ws[0].shape, F32)] * (3 * k), scratch_shapes=[],
        compiler_params=_params(2), operands=(*ws, *gs, *ms, *vs))
    return (outs[:k], outs[k:2 * k], outs[2 * k:]), side_outs


def _adamw_math(w, g, m, v):
    nm = ADAM_B1 * m + (1.0 - ADAM_B1) * g
    nv = ADAM_B2 * v + (1.0 - ADAM_B2) * (g * g)
    m_hat = nm / (1.0 - ADAM_B1 ** ADAM_STEP)
    v_hat = nv / (1.0 - ADAM_B2 ** ADAM_STEP)
    return -ADAM_LR * (m_hat / (jnp.sqrt(v_hat) + ADAM_EPS) + ADAM_WD * w), nm, nv


def adamw_small(ws, gs, ms, vs):
    n = len(ws)

    def body(*refs):
        for i in range(n):
            outs = _adamw_math(*[refs[k * n + i][...] for k in range(4)])
            for k in range(3):
                refs[(4 + k) * n + i][...] = outs[k]

    vm = pl.BlockSpec(memory_space=pltpu.VMEM)
    outs = pl.pallas_call(
        body, name="adamw_small", in_specs=[vm] * (4 * n), out_specs=[vm] * (3 * n),
        out_shape=[jax.ShapeDtypeStruct(w.shape, F32) for w in ws] * 3,
    )(*ws, *gs, *ms, *vs)
    return outs[:n], outs[n:2 * n], outs[2 * n:]


def slab_sum(srcs, n_out, out_dtype, into=None, slot=0):
    _, rows, cols = srcs[0][0].shape
    tr = _row_tile(rows)
    n_src = len(srcs)
    sel = jnp.stack([jnp.asarray(base, jnp.int32) for _, base, _ in srcs])

    def body(sel_ref, *refs):
        acc = refs[0][...].astype(F32)
        for r in refs[1:n_src]:
            acc = acc + r[...].astype(F32)
        refs[-1][...] = acc.astype(out_dtype)

    def in_spec(k, step):
        return pl.BlockSpec((None, tr, cols), lambda o, i, sel_ref: (sel_ref[k] + step * o, i, 0))

    shape = (n_out, rows, cols) if into is None else into.shape
    return pl.pallas_call(
        body, name="slab_sum",
        grid_spec=pltpu.PrefetchScalarGridSpec(
            num_scalar_prefetch=1, grid=(n_out, rows // tr),
            in_specs=[in_spec(k, step) for k, (_, _, step) in enumerate(srcs)] + ([] if into is None else [ANY]),
            out_specs=pl.BlockSpec((None, tr, cols), lambda o, i, sel_ref: (slot + o, i, 0))),
        out_shape=jax.ShapeDtypeStruct(shape, out_dtype), compiler_params=_params(2),
        input_output_aliases={} if into is None else {1 + n_src: 0},
    )(sel, *[a for a, _, _ in srcs], *([] if into is None else [into]))


ICI_FLIPS = ((1, 0, 0), (0, 1, 0), (1, 1, 0))
D2D_FLIPS = ((0, 0, 1),)
ANY = pl.BlockSpec(memory_space=pl.ANY)


def _me():
    return lax.axis_index("x"), lax.axis_index("y"), lax.axis_index("c")


def _flipped(me, flip):
    return tuple(1 - m if f else m for m, f in zip(me, flip))


def _chip(dev):
    return 2 * dev[0] + dev[1]


def _dev(dev):
    return 4 * dev[0] + 2 * dev[1] + dev[2]


N_CHIP = 4
D2D = (0, 0, 1)


class Exchange:
    def __init__(self, xs, n_out, copies, own=None, in_place=False):
        self.xs, self.copies, self.own, self.in_place = list(xs), copies, own, in_place
        self.n_arr, self.n_cp = len(self.xs), len(copies)
        self.out_shape = [jax.ShapeDtypeStruct((n_out,) + x.shape[1:], x.dtype) for x in self.xs]
        self.scratch = [pltpu.SemaphoreType.DMA((self.n_arr * self.n_cp,)),
                        pltpu.SemaphoreType.DMA((self.n_arr * self.n_cp,)), pltpu.SemaphoreType.DMA((self.n_arr,))]

    def _own(self, x_refs, o_refs, sems, me):
        if self.own is None:
            return []
        return [pltpu.make_async_copy(x_refs[a].at[self.own[0](me)], o_refs[a].at[self.own[1](me)], sems[2].at[a])
                for a in range(self.n_arr)]

    def _copy(self, x_refs, o_refs, sems, me, a, j, sender):
        flip, src_slot, dst_slot = self.copies[j]
        k = a * self.n_cp + j
        return pltpu.make_async_remote_copy(
            src_ref=x_refs[a].at[src_slot(me)], dst_ref=o_refs[a].at[dst_slot(sender)],
            send_sem=sems[0].at[k], recv_sem=sems[1].at[k], device_id=_flipped(me, flip), device_id_type=MESH)

    def start(self, x_refs, o_refs, sems):
        me = _me()
        for cp in self._own(x_refs, o_refs, sems, me):
            cp.start()
        for j in range(self.n_cp):
            for a in range(self.n_arr):
                self._copy(x_refs, o_refs, sems, me, a, j, me).start()

    def wait(self, x_refs, o_refs, sems):
        me = _me()
        for j in range(self.n_cp):
            for a in range(self.n_arr):
                self._copy(x_refs, o_refs, sems, me, a, j, _flipped(me, self.copies[j][0])).wait_recv()
        for j in range(self.n_cp):
            for a in range(self.n_arr):
                self._copy(x_refs, o_refs, sems, me, a, j, me).wait_send()
        for cp in self._own(x_refs, o_refs, sems, me):
            cp.wait()

    def run(self, name):
        n = self.n_arr

        def body(*refs):
            self.start(refs[:n], refs[n:2 * n], refs[2 * n:])
            self.wait(refs[:n], refs[n:2 * n], refs[2 * n:])

        return pl.pallas_call(
            body, name=name, in_specs=[ANY] * n, out_specs=[ANY] * n, out_shape=self.out_shape,
            input_output_aliases={a: a for a in range(n)} if self.in_place else {}, scratch_shapes=self.scratch,
        )(*self.xs)


def hosted_call(body, sides, *, name, grid, in_specs, out_specs, out_shape, scratch_shapes, compiler_params, operands):
    n_in, n_out, n_scr = len(in_specs), len(out_specs), len(scratch_shapes)
    live = [s for s in sides if s is not None]
    if not live:
        outs = pl.pallas_call(body, name=name, grid=grid, in_specs=in_specs, out_specs=out_specs, out_shape=out_shape,
                              scratch_shapes=scratch_shapes, compiler_params=compiler_params)(*operands)
        return outs, [None] * len(sides)
    n = sum(s.n_arr for s in live)
    lo = [sum(s.n_arr for s in live[:i]) for i in range(len(live))]

    def full_body(*refs):
        ins, sx = refs[:n_in], refs[n_in:n_in + n]
        outs, so = refs[n_in + n:n_in + n + n_out], refs[n_in + n + n_out:n_in + 2 * n + n_out]
        scr, sems = refs[n_in + 2 * n + n_out:n_in + 2 * n + n_out + n_scr], refs[n_in + 2 * n + n_out + n_scr:]
        first = functools.reduce(jnp.logical_and, [pl.program_id(a) == 0 for a in range(len(grid))])
        last = functools.reduce(jnp.logical_and, [pl.program_id(a) == g - 1 for a, g in enumerate(grid)])
        parts = [(s, sx[l:l + s.n_arr], so[l:l + s.n_arr], sems[3 * i:3 * i + 3]) for i, (s, l) in enumerate(zip(live, lo))]

        @pl.when(first)
        def _():
            for s, x, o, m in parts:
                s.start(x, o, m)

        body(*ins, *outs, *scr)

        @pl.when(last)
        def _():
            for s, x, o, m in parts:
                s.wait(x, o, m)

    aliases = {n_in + l + a: n_out + l + a for s, l in zip(live, lo) if s.in_place for a in range(s.n_arr)}
    outs = pl.pallas_call(
        full_body, name=name + "_x", grid=grid, in_specs=list(in_specs) + [ANY] * n,
        out_specs=list(out_specs) + [ANY] * n, out_shape=list(out_shape) + [o for s in live for o in s.out_shape],
        input_output_aliases=aliases,
        scratch_shapes=list(scratch_shapes) + [m for s in live for m in s.scratch], compiler_params=compiler_params,
    )(*operands, *[x for s in live for x in s.xs])
    side_outs = iter([outs[n_out + l:n_out + l + s.n_arr] for s, l in zip(live, lo)])
    return outs[:n_out], [next(side_outs) if s is not None else None for s in sides]


def gather_ici(shards):
    ici = [(f, lambda me: 0, _dev) for f in ICI_FLIPS]
    return Exchange([s[None] for s in shards], N_DEV, ici, (lambda me: 0, _dev))


def gather_d2d(blocks):
    d2d = [(D2D, (lambda me, k=k: 2 * k + me[2]), (lambda sender, k=k: 2 * k + sender[2])) for k in range(N_CHIP)]
    return Exchange(blocks, N_DEV, d2d, None, in_place=True)


def gathered(blocks):
    return [b.reshape(-1, b.shape[2]) for b in blocks]


def scatter_d2d(parts):
    d2d = [(D2D, (lambda me, k=k: 2 * k + 1 - me[2]), (lambda sender, k=k: k)) for k in range(N_CHIP)]
    return Exchange(parts, N_CHIP, d2d)


def chip_sums(parts, sib):
    c = _me()[2]
    return [slab_sum([(p, c, 2), (s, 0, 1)], N_CHIP, BF16) for p, s in zip(parts, sib)]


def scatter_ici(sums):
    ici = [(f, (lambda me, f=f: _chip(_flipped(me, f))), (lambda sender, i=i: i)) for i, f in enumerate(ICI_FLIPS)]
    return Exchange(sums, len(ICI_FLIPS), ici)


def device_sums(sums, got, into=None, slot=0):
    x, y, _ = _me()
    outs = [slab_sum([(cs, 2 * x + y, 0)] + [(g, i, 0) for i in range(len(ICI_FLIPS))], 1, F32,
                     None if into is None else into[a], slot) for a, (cs, g) in enumerate(zip(sums, got))]
    return outs if into is not None else [o[0] for o in outs]


def all_gather(shards):
    blocks = gather_ici(shards).run("gather_ici")
    return gathered(gather_d2d(blocks).run("gather_d2d"))


def all_reduce_small(v):
    flips = D2D_FLIPS + ICI_FLIPS[:2]

    def body(v_ref, o_ref, got_ref, send_sems, recv_sems):
        me = _me()
        o_ref[...] = v_ref[...]
        for i, flip in enumerate(flips):
            cp = pltpu.make_async_remote_copy(
                src_ref=o_ref, dst_ref=got_ref.at[i], send_sem=send_sems.at[i], recv_sem=recv_sems.at[i],
                device_id=_flipped(me, flip), device_id_type=MESH)
            cp.start()
            cp.wait()
            o_ref[...] = o_ref[...] + got_ref[i]

    vm = pl.BlockSpec(memory_space=pltpu.VMEM)
    return pl.pallas_call(
        body, name="all_reduce_small", in_specs=[vm], out_specs=vm, out_shape=jax.ShapeDtypeStruct(v.shape, F32),
        scratch_shapes=[pltpu.VMEM((len(flips),) + v.shape, F32), pltpu.SemaphoreType.DMA((len(flips),)),
                        pltpu.SemaphoreType.DMA((len(flips),))],
    )(v)


def _perm_rows(wt):
    pad = jnp.zeros((D_IN_PAD - D_IN_PROJ, wt.shape[1]), wt.dtype)
    return jnp.concatenate([wt[:DT_LO], wt[DT_HI:], wt[DT_LO:DT_HI], pad], axis=0)


def _unperm_rows(dwt):
    n = D_IN_PROJ - (DT_HI - DT_LO)
    return jnp.concatenate([dwt[:DT_LO], dwt[n:D_IN_PROJ], dwt[DT_LO:n]], axis=0)


def _pad_lanes(v):
    return jnp.pad(v, ((0, 0), (0, LANE - v.shape[1])))[:, None]


def _block_diag(w):
    l, g, n, _ = w.shape
    out = jnp.zeros((l, g * n, g * n), w.dtype)
    for i in range(g):
        out = out.at[:, i * n:(i + 1) * n, i * n:(i + 1) * n].set(w[:, i])
    return out


def _pack(groups):
    flat = []
    for grp in groups:
        parts = [a.reshape(-1) for a in (grp if isinstance(grp, (list, tuple)) else [grp])]
        n = sum(p.shape[0] for p in parts)
        if -n % LANE:
            parts.append(jnp.zeros((-n % LANE,), parts[0].dtype))
        flat += parts
    return jnp.concatenate(flat).reshape(-1, LANE)


def _unpack(buf, shapes):
    out = []
    lo = 0
    buf = buf.reshape(-1)
    for shp in shapes:
        n = 1
        for k in shp:
            n *= k
        out.append(buf[lo:lo + n].reshape(shp))
        lo += n + (-n % LANE)
    return out


def small_params(w, conv_w_full):
    return dict(
        n1w=w["norm1_w"][:, None], cw=jnp.pad(conv_w_full, ((0, 0), (0, 8 - CONV_WIDTH), (0, 0))),
        cb=w["conv_b"][:, None], dtb=_pad_lanes(w["dt_bias"]), alog=_pad_lanes(w["a_log"]), dsk=_pad_lanes(w["d_skip"]),
        snw=w["ssd_norm_w"][:, None], wblk=_block_diag(w["pool_w"]), pb=w["pool_b"].reshape(-1, 1, POOL_WIDTH),
        ps=w["pool_scale"][:, None], n2w=w["norm2_w"][:, None])


MIX = ("w_in", "w_out")
FFN = ("w_gate", "w_up", "w_down")


def layer_params(small, l):
    return {k: v[l] for k, v in small.items()}


def mix_weights(whole):
    return _perm_rows(whole[0]), whole[1]


def _slabs(g):
    return g.reshape(N_DEV, -1, g.shape[-1])


def _layer_fwd(x, p, mix, ffn=None, ffn_shards=None, next_mix_shards=None):
    ici = [gather_ici([sh]) for sh in ffn_shards] if ffn_shards is not None else [None] * len(FFN)
    (z, xbc, qkv, pp, dtr, h1), (blk_g,) = inproj_fwd(x, p["n1w"], mix[0], [ici[0]])
    (u,), _ = conv_fwd(xbc, p["cw"], p["cb"])
    (y_ssd, st), (blk_u,) = ssd_fwd(u, z, dtr, p["dtb"], p["alog"], p["dsk"], p["snw"], [ici[1]])
    (o, tot, swept), (blk_d,) = sb_fwd(qkv, [ici[2]])
    blocks = blk_g + blk_u + blk_d if ffn_shards is not None else None
    yp = pool_fwd(pp, p["wblk"], p["pb"], p["ps"])
    (x_mid, ycat), (blocks,) = outproj_fwd([y_ssd, o, yp], mix[1], x,
                                           [gather_d2d(blocks) if blocks is not None else None])
    if blocks is not None:
        ffn = gathered(blocks)
    (x_out, g, uu), (nxt,) = ffn_fwd(x_mid, p["n2w"], *ffn,
                                     [gather_ici(next_mix_shards) if next_mix_shards is not None else None])
    sv = dict(x=x, z=z, xbc=xbc, qkv=qkv, pp=pp, dtr=dtr, h1=h1, u=u, st=st, tot=tot, swept=swept, ycat=ycat,
              x_mid=x_mid, g=g, uu=uu, w_in=mix[0], w_out=mix[1], wg=ffn[0], wu=ffn[1], wd=ffn[2])
    return x_out, sv, nxt


def _layer_bwd(dxo, sv, p, pending_mix=None, exchange=False, into_ffn=(), into_mix=()):
    (dx_mid, dn2w, a, dg, du, h2), (sib,) = ffn_bwd(
        dxo, sv["x_mid"], sv["g"], sv["uu"], p["n2w"], sv["wg"], sv["wu"], sv["wd"],
        [scatter_d2d(pending_mix) if pending_mix is not None else None])
    sums_mix = chip_sums(pending_mix, sib) if pending_mix is not None else None
    dwg, dwu = mm_tn([dg, du], h2)
    gr = dict(norm2_w=dn2w[0], w_gate=dwg, w_up=dwu, w_down=mm_tn(a, dxo))
    parts = [_slabs(gr[k]) for k in FFN] if exchange else None
    (dycat,), (sib,) = outproj_bwd(dx_mid, sv["w_out"], [scatter_d2d(parts) if exchange else None])
    sums_ffn = chip_sums(parts, sib) if exchange else None
    gr["w_out"] = mm_tn(sv["ycat"], dx_mid)
    (dp, dwblk, dpb, dps), _ = pool_bwd(sv["pp"], p["wblk"], p["pb"], p["ps"], dycat,
                                        (SSD_WIDTH + SB_WIDTH) // POOL_WIDTH)
    n = POOL_GROUP_DIM
    gr["pool_w"] = jnp.stack([dwblk[i * n:(i + 1) * n, i * n:(i + 1) * n] for i in range(len(POOL_WINDOWS))])
    gr["pool_b"] = dpb.reshape(len(POOL_WINDOWS), n)
    gr["pool_scale"] = dps[0]
    (dq, dk, dv), (got_ud,) = sb_bwd(sv["qkv"], sv["tot"], sv["swept"], dycat, SSD_WIDTH // LANE,
                                     [scatter_ici(sums_ffn[1:]) if exchange else None])
    (du_, dz, ddtr, ddtb, dalog, ddsk, dsnw), (got_mix,) = ssd_bwd(
        sv["u"], sv["z"], sv["dtr"], sv["st"], dycat, p["dtb"], p["alog"], p["dsk"], p["snw"],
        [scatter_ici(sums_mix) if sums_mix is not None else None])
    done_mix = device_sums(sums_mix, got_mix, *into_mix) if sums_mix is not None else None
    gr.update(dt_bias=ddtb[0, :SSD_HEADS], a_log=dalog[0, :SSD_HEADS], d_skip=ddsk[0, :SSD_HEADS], ssd_norm_w=dsnw[0])
    (dxbc, dcw, dcb), (got_g,) = conv_bwd(sv["xbc"], p["cw"], p["cb"], du_,
                                          [scatter_ici(sums_ffn[:1]) if exchange else None])
    done_ffn = device_sums(sums_ffn, got_g + got_ud, *into_ffn) if exchange else None
    gr.update(conv_w=dcw[:CONV_WIDTH], conv_b=dcb[0])
    dx, dn1w, dproj = inproj_bwd([dz, dxbc, dq, dk, dv, dp, ddtr], sv["w_in"], sv["x"], p["n1w"], dx_mid)
    gr.update(norm1_w=dn1w[0], w_in=_unperm_rows(mm_tn(dproj, sv["h1"])))
    return dx, gr, done_ffn, done_mix


def local_step(x, tgt, params, weights, final_w):
    saved = []
    for p, (mix, ffn) in zip(params, weights):
        x, sv, _ = _layer_fwd(x, p, mix, ffn)
        saved.append(sv)
    loss, dx, dfw = head_loss(x, final_w[None], tgt)
    grads = []
    for p, sv in zip(reversed(params), reversed(saved)):
        dx, gr, _, _ = _layer_bwd(dx, sv, p)
        grads.append(gr)
    grads.reverse()
    return loss, dx, dfw[0], grads


WEIGHTS = ("norm1_w", "w_in", "conv_w", "conv_b", "dt_bias", "a_log", "d_skip", "ssd_norm_w", "pool_w", "pool_b",
           "pool_scale", "w_out", "norm2_w", "w_gate", "w_up", "w_down", "final_norm_w")
COL_SHARDED = ("w_in", "w_gate", "w_up")
ROW_SHARDED = ("w_out", "w_down")
SMALL = tuple(k for k in WEIGHTS if k not in COL_SHARDED + ROW_SHARDED)


def kernel(x, norm1_w, w_in, conv_w, conv_b, dt_bias, a_log, d_skip, ssd_norm_w, pool_w, pool_b, pool_scale, w_out, norm2_w, w_gate, w_up, w_down, final_norm_w, loss_target, m_norm1_w, m_w_in, m_conv_w, m_conv_b, m_dt_bias, m_a_log, m_d_skip, m_ssd_norm_w, m_pool_w, m_pool_b, m_pool_scale, m_w_out, m_norm2_w, m_w_gate, m_w_up, m_w_down, m_final_norm_w, v_norm1_w, v_w_in, v_conv_w, v_conv_b, v_dt_bias, v_a_log, v_d_skip, v_ssd_norm_w, v_pool_w, v_pool_b, v_pool_scale, v_w_out, v_norm2_w, v_w_gate, v_w_up, v_w_down, v_final_norm_w):
    args = dict(locals())
    w = {k: args[k] for k in WEIGHTS}
    m = {k: args["m_" + k] for k in WEIGHTS}
    v = {k: args["v_" + k] for k in WEIGHTS}
    depth = w_in.shape[0]
    dev = _dev(_me())
    n_cw = conv_w.shape[-1]

    shards = {k: (jnp.swapaxes(w[k], 1, 2) if k in COL_SHARDED else w[k]).astype(BF16) for k in MIX + FFN}
    whole = all_gather([jnp.swapaxes(conv_w, 0, 2).reshape(n_cw, -1)] + [shards[k][0] for k in MIX])
    conv_w_full = jnp.swapaxes(whole[0].reshape(N_DEV * n_cw, CONV_WIDTH, depth), 0, 2)
    small = small_params(w, conv_w_full)
    xs = x[0]
    params, saved = [layer_params(small, l) for l in range(depth)], []
    mix = mix_weights(whole[1:])
    for l in range(depth):
        xs, sv, nxt = _layer_fwd(xs, params[l], mix, ffn_shards=[shards[k][l] for k in FFN],
                                 next_mix_shards=[shards[k][l + 1] for k in MIX] if l + 1 < depth else None)
        saved.append(sv)
        if nxt is not None:
            mix = mix_weights(gathered(gather_d2d(nxt).run("gather_d2d")))
    loss, dx, dfw = head_loss(xs, final_norm_w[None], loss_target[0])
    layer_grads = [None] * depth
    native = {k: lax.empty((depth,) + shards[k].shape[1:], F32) for k in MIX + FFN}
    pending = None
    for l in reversed(range(depth)):
        dx, layer_grads[l], done_ffn, done_mix = _layer_bwd(
            dx, saved[l], params[l], pending, exchange=True, into_ffn=([native[k] for k in FFN], l),
            into_mix=([native[k] for k in MIX], l + 1))
        native.update(zip(FFN, done_ffn))
        if pending is not None:
            native.update(zip(MIX, done_mix))
        pending = [_slabs(layer_grads[l][k]) for k in MIX]

    delta, new_m, new_v = {}, {}, {}
    sums = chip_sums(pending, scatter_d2d(pending).run("scatter_d2d"))
    as_native = lambda t, k: jnp.swapaxes(t[k], 1, 2) if k in COL_SHARDED else t[k]
    outs, (got,) = adamw_many([as_native(w, k) for k in FFN], [native[k] for k in FFN],
                              [as_native(m, k) for k in FFN], [as_native(v, k) for k in FFN], [scatter_ici(sums)])
    for dst, arrs in zip((delta, new_m, new_v), outs):
        dst.update({k: as_native({k: a}, k) for k, a in zip(FFN, arrs)})
    native.update(zip(MIX, device_sums(sums, got, [native[k] for k in MIX], 0)))
    grads = {k: jnp.swapaxes(native[k], 1, 2) if k in COL_SHARDED else native[k] for k in MIX + FFN}
    layered = [k for k in SMALL if k != "final_norm_w"]
    small_shapes = [(1, LANE)] + [(depth,) + layer_grads[0][k].shape for k in layered] + [dfw[0].shape]
    packed = _pack([loss] + [[layer_grads[l][k] for l in range(depth)] for k in layered] + [dfw[0]])
    summed = _unpack(all_reduce_small(packed), small_shapes)
    loss = summed[0][0, 0]
    grads.update(zip(layered + ["final_norm_w"], summed[1:]))
    grads["conv_w"] = lax.dynamic_slice_in_dim(grads["conv_w"], dev * n_cw, n_cw, axis=2)

    for k in MIX:
        delta[k], new_m[k], new_v[k] = adamw(w[k], grads[k], m[k], v[k])
    two_d = lambda a: a.reshape(1, -1) if a.ndim == 1 else a
    outs = adamw_small(*[[two_d(t[k]) for k in SMALL] for t in (w, grads, m, v)])
    for dst, arrs in zip((delta, new_m, new_v), outs):
        dst.update({k: a.reshape(w[k].shape) for k, a in zip(SMALL, arrs)})
    return (loss, dx[None], *[grads[k] for k in WEIGHTS], *[delta[k] for k in WEIGHTS],
            *[new_m[k] for k in WEIGHTS], *[new_v[k] for k in WEIGHTS])
```
